```python
import math
import jax, jax.numpy as jnp
from jax import lax
import numpy as np

D_MODEL = 2048
BATCH = 8
SEQ = 8192
DEPTH = 1

N_Q_HEADS = 16
N_KV_HEADS = 2
HEAD_DIM = 64
Q_PER_KV = N_Q_HEADS // N_KV_HEADS
WINDOW = 128
BLOCK = 128
ATTN_WIDTH = N_Q_HEADS * HEAD_DIM
KV_WIDTH = N_KV_HEADS * HEAD_DIM
SSM_GROUP = 16
SSM_GROUPS = 32
SSM_WIDTH = SSM_GROUP * SSM_GROUPS
SSM_STATE = 64
DT_MIN = 0.001
DT_MAX = 0.1
D_FF = 5632
CONV_WIDTH = 3
RMS_EPS = 1e-6
IN_COLS = ATTN_WIDTH + 2 * KV_WIDTH + SSM_WIDTH + 2 * D_MODEL
SPLIT_POINTS = (ATTN_WIDTH, ATTN_WIDTH + KV_WIDTH, ATTN_WIDTH + 2 * KV_WIDTH,
                ATTN_WIDTH + 2 * KV_WIDTH + SSM_WIDTH,
                ATTN_WIDTH + 2 * KV_WIDTH + SSM_WIDTH + D_MODEL)
NEG_BIG = -1e30

kernel_name = "hybrid_swa_s5_convffn_block"


def rms_norm(x, g):
    xf = x.astype(jnp.float32)
    y = xf * lax.rsqrt(jnp.mean(xf * xf, axis=-1, keepdims=True) + RMS_EPS)
    return (y * g.astype(jnp.float32)).astype(x.dtype)


def sliding_window_attention(q, k, v, sinks):
    b, l = q.shape[0], q.shape[1]
    nb = l // BLOCK
    qb = q.reshape(b, nb, BLOCK, N_KV_HEADS, Q_PER_KV, HEAD_DIM)
    kb = k.reshape(b, nb, BLOCK, N_KV_HEADS, HEAD_DIM)
    vb = v.reshape(b, nb, BLOCK, N_KV_HEADS, HEAD_DIM)

    def with_prev(t):
        prev = jnp.pad(t, ((0, 0), (1, 0), (0, 0), (0, 0), (0, 0)))[:, :-1]
        return jnp.concatenate([prev, t], axis=2)

    kx = with_prev(kb)
    vx = with_prev(vb)
    scores = jnp.einsum('bnqgrd,bnsgd->bngrqs', qb, kx).astype(jnp.float32) * (HEAD_DIM ** -0.5)
    q_idx = jnp.arange(BLOCK)[:, None]
    s_idx = jnp.arange(2 * BLOCK)[None, :]
    dist = q_idx + BLOCK - s_idx
    band = (dist >= 0) & (dist < WINDOW)
    valid = band[None] & ((jnp.arange(nb)[:, None, None] > 0) | (s_idx[None] >= BLOCK))
    slopes = 2.0 ** (-8.0 * jnp.arange(1, N_Q_HEADS + 1, dtype=jnp.float32) / N_Q_HEADS)
    slopes = slopes.reshape(N_KV_HEADS, Q_PER_KV)
    alibi = -slopes[:, :, None, None] * dist.astype(jnp.float32)[None, None]
    scores = scores + alibi[None, None]
    scores = jnp.where(valid[None, :, None, None], scores, NEG_BIG)
    sink = sinks.astype(jnp.float32).reshape(N_KV_HEADS, Q_PER_KV)[None, None, :, :, None, None]
    m = jnp.maximum(jnp.max(scores, axis=-1, keepdims=True), sink)
    p = jnp.exp(scores - m)
    p = p / (jnp.sum(p, axis=-1, keepdims=True) + jnp.exp(sink - m))
    out = jnp.einsum('bngrqs,bnsgd->bnqgrd', p.astype(v.dtype), vx)
    return out.reshape(b, l, ATTN_WIDTH)


def s5_ssm(u, a_re, a_im, log_dt, b_re, b_im, c_re, c_im, d_skip):
    bsz, l = u.shape[0], u.shape[1]
    ug = u.reshape(bsz, l, SSM_GROUPS, SSM_GROUP)
    dt = jnp.exp(log_dt)[:, None]
    mag = jnp.exp(a_re * dt)
    ab_re = mag * jnp.cos(a_im * dt)
    ab_im = mag * jnp.sin(a_im * dt)
    nr = ab_re - 1.0
    ni = ab_im
    den = a_re * a_re + a_im * a_im
    z_re = (nr * a_re + ni * a_im) / den
    z_im = (ni * a_re - nr * a_im) / den
    bb_re = z_re[..., None] * b_re - z_im[..., None] * b_im
    bb_im = z_re[..., None] * b_im + z_im[..., None] * b_re
    bu_re = jnp.einsum('gph,blgh->blgp', bb_re, ug)
    bu_im = jnp.einsum('gph,blgh->blgp', bb_im, ug)
    a_re_t = jnp.broadcast_to(ab_re, bu_re.shape)
    a_im_t = jnp.broadcast_to(ab_im, bu_im.shape)

    def combine(left, right):
        a1r, a1i, b1r, b1i = left
        a2r, a2i, b2r, b2i = right
        return (a1r * a2r - a1i * a2i,
                a1r * a2i + a1i * a2r,
                a2r * b1r - a2i * b1i + b2r,
                a2r * b1i + a2i * b1r + b2i)

    _, _, xs_re, xs_im = lax.associative_scan(combine, (a_re_t, a_im_t, bu_re, bu_im), axis=1)
    y = (jnp.einsum('ghp,blgp->blgh', c_re, xs_re)
         - jnp.einsum('ghp,blgp->blgh', c_im, xs_im)
         + d_skip.reshape(SSM_GROUPS, SSM_GROUP) * ug)
    return y.reshape(bsz, l, SSM_WIDTH)


def causal_depthwise_conv(x, w, b):
    l = x.shape[1]
    xp = jnp.pad(x, ((0, 0), (CONV_WIDTH - 1, 0), (0, 0)))
    y = b
    for k in range(CONV_WIDTH):
        y = y + w[k] * xp[:, k:k + l]
    return y


def _fwd_setup_inputs(seed: int = 0) -> dict:
    key = jax.random.key(seed)
    ks = jax.random.split(key, 24)
    f32 = jnp.float32
    nrm = lambda k, shape, s: jax.random.normal(k, shape, f32) * s
    x = jax.random.normal(ks[0], (BATCH, SEQ, D_MODEL), f32)
    attn_norm_g = 1.0 + nrm(ks[1], (DEPTH, D_MODEL), 0.02)
    w_in = nrm(ks[2], (DEPTH, D_MODEL, IN_COLS), D_MODEL ** -0.5)
    b_in = nrm(ks[3], (DEPTH, IN_COLS), 0.02)
    attn_sinks = nrm(ks[4], (DEPTH, N_Q_HEADS), 0.5)
    ssm_a_re = -0.5 + nrm(ks[5], (DEPTH, SSM_GROUPS, SSM_STATE), 0.01)
    ssm_a_im = (jnp.pi * jnp.arange(SSM_STATE, dtype=f32))[None, None, :] + nrm(ks[6], (DEPTH, SSM_GROUPS, SSM_STATE), 0.01)
    ssm_log_dt = jax.random.uniform(ks[7], (DEPTH, SSM_GROUPS), f32, minval=math.log(DT_MIN), maxval=math.log(DT_MAX))
    ssm_b_re = nrm(ks[8], (DEPTH, SSM_GROUPS, SSM_STATE, SSM_GROUP), (2 * SSM_GROUP) ** -0.5)
    ssm_b_im = nrm(ks[9], (DEPTH, SSM_GROUPS, SSM_STATE, SSM_GROUP), (2 * SSM_GROUP) ** -0.5)
    ssm_c_re = nrm(ks[10], (DEPTH, SSM_GROUPS, SSM_GROUP, SSM_STATE), (2 * SSM_STATE) ** -0.5)
    ssm_c_im = nrm(ks[11], (DEPTH, SSM_GROUPS, SSM_GROUP, SSM_STATE), (2 * SSM_STATE) ** -0.5)
    ssm_d = nrm(ks[12], (DEPTH, SSM_WIDTH), 1.0)
    w_glu = nrm(ks[13], (DEPTH, SSM_WIDTH, 2 * SSM_WIDTH), SSM_WIDTH ** -0.5)
    b_glu = nrm(ks[14], (DEPTH, 2 * SSM_WIDTH), 0.02)
    w_branch_attn = nrm(ks[15], (DEPTH, ATTN_WIDTH, D_MODEL), ATTN_WIDTH ** -0.5)
    w_branch_ssm = nrm(ks[16], (DEPTH, SSM_WIDTH, D_MODEL), SSM_WIDTH ** -0.5)
    w_out = nrm(ks[17], (DEPTH, D_MODEL, D_MODEL), D_MODEL ** -0.5)
    ffn_norm_g = 1.0 + nrm(ks[18], (DEPTH, D_MODEL), 0.02)
    w_up = nrm(ks[19], (DEPTH, D_MODEL, 2 * D_FF), D_MODEL ** -0.5)
    conv_w = nrm(ks[20], (DEPTH, CONV_WIDTH, D_FF), CONV_WIDTH ** -0.5)
    conv_b = nrm(ks[21], (DEPTH, D_FF), 0.02)
    w_down = nrm(ks[22], (DEPTH, D_FF, D_MODEL), D_FF ** -0.5)
    final_norm_g = 1.0 + nrm(ks[23], (D_MODEL,), 0.02)
    return {"x": x, "attn_norm_g": attn_norm_g, "w_in": w_in, "b_in": b_in, "attn_sinks": attn_sinks,
            "ssm_a_re": ssm_a_re, "ssm_a_im": ssm_a_im, "ssm_log_dt": ssm_log_dt,
            "ssm_b_re": ssm_b_re, "ssm_b_im": ssm_b_im, "ssm_c_re": ssm_c_re, "ssm_c_im": ssm_c_im,
            "ssm_d": ssm_d, "w_glu": w_glu, "b_glu": b_glu, "w_branch_attn": w_branch_attn,
            "w_branch_ssm": w_branch_ssm, "w_out": w_out, "ffn_norm_g": ffn_norm_g, "w_up": w_up,
            "conv_w": conv_w, "conv_b": conv_b, "w_down": w_down, "final_norm_g": final_norm_g}


def _fwd_reference(x, attn_norm_g, w_in, b_in, attn_sinks, ssm_a_re, ssm_a_im, ssm_log_dt, ssm_b_re, ssm_b_im,
              ssm_c_re, ssm_c_im, ssm_d, w_glu, b_glu, w_branch_attn, w_branch_ssm, w_out, ffn_norm_g,
              w_up, conv_w, conv_b, w_down, final_norm_g):
    bsz, seq = x.shape[0], x.shape[1]
    for i in range(DEPTH):
        h = rms_norm(x, attn_norm_g[i])
        proj = h @ w_in[i] + b_in[i]
        q, k, v, u, gate_attn, gate_ssm = jnp.split(proj, SPLIT_POINTS, axis=-1)
        q = q.reshape(bsz, seq, N_Q_HEADS, HEAD_DIM)
        k = k.reshape(bsz, seq, N_KV_HEADS, HEAD_DIM)
        v = v.reshape(bsz, seq, N_KV_HEADS, HEAD_DIM)
        attn = sliding_window_attention(q, k, v, attn_sinks[i])
        y = s5_ssm(u, ssm_a_re[i], ssm_a_im[i], ssm_log_dt[i], ssm_b_re[i], ssm_b_im[i],
                   ssm_c_re[i], ssm_c_im[i], ssm_d[i])
        y_val, y_gate = jnp.split(jax.nn.gelu(y, approximate=False) @ w_glu[i] + b_glu[i], 2, axis=-1)
        ssm = y_val * jax.nn.sigmoid(y_gate)
        merged = (jax.nn.sigmoid(gate_attn) * (attn @ w_branch_attn[i])
                  + jax.nn.sigmoid(gate_ssm) * (ssm @ w_branch_ssm[i]))
        x = x + merged @ w_out[i]
        h = rms_norm(x, ffn_norm_g[i])
        val, gate = jnp.split(h @ w_up[i], 2, axis=-1)
        gate = causal_depthwise_conv(gate, conv_w[i], conv_b[i])
        x = x + (val * jax.nn.gelu(gate, approximate=False)) @ w_down[i]
    return rms_norm(x, final_norm_g)


import jax as _jax
import jax.numpy as _jnp

TWIN_FORMAT = 'train_step'
FWD_PARAMS = ['x', 'attn_norm_g', 'w_in', 'b_in', 'attn_sinks', 'ssm_a_re', 'ssm_a_im', 'ssm_log_dt', 'ssm_b_re', 'ssm_b_im', 'ssm_c_re', 'ssm_c_im', 'ssm_d', 'w_glu', 'b_glu', 'w_branch_attn', 'w_branch_ssm', 'w_out', 'ffn_norm_g', 'w_up', 'conv_w', 'conv_b', 'w_down', 'final_norm_g']
TWIN_WEIGHTS = ['attn_norm_g', 'w_in', 'b_in', 'attn_sinks', 'ssm_a_re', 'ssm_a_im', 'ssm_log_dt', 'ssm_b_re', 'ssm_b_im', 'ssm_c_re', 'ssm_c_im', 'ssm_d', 'w_glu', 'b_glu', 'w_branch_attn', 'w_branch_ssm', 'w_out', 'ffn_norm_g', 'w_up', 'conv_w', 'conv_b', 'w_down', 'final_norm_g']
TWIN_DIFF_INPUT = 'x'
TWIN_INPUTS = ['x', 'attn_norm_g', 'w_in', 'b_in', 'attn_sinks', 'ssm_a_re', 'ssm_a_im', 'ssm_log_dt', 'ssm_b_re', 'ssm_b_im', 'ssm_c_re', 'ssm_c_im', 'ssm_d', 'w_glu', 'b_glu', 'w_branch_attn', 'w_branch_ssm', 'w_out', 'ffn_norm_g', 'w_up', 'conv_w', 'conv_b', 'w_down', 'final_norm_g', 'loss_target', 'm_attn_norm_g', 'm_w_in', 'm_b_in', 'm_attn_sinks', 'm_ssm_a_re', 'm_ssm_a_im', 'm_ssm_log_dt', 'm_ssm_b_re', 'm_ssm_b_im', 'm_ssm_c_re', 'm_ssm_c_im', 'm_ssm_d', 'm_w_glu', 'm_b_glu', 'm_w_branch_attn', 'm_w_branch_ssm', 'm_w_out', 'm_ffn_norm_g', 'm_w_up', 'm_conv_w', 'm_conv_b', 'm_w_down', 'm_final_norm_g', 'v_attn_norm_g', 'v_w_in', 'v_b_in', 'v_attn_sinks', 'v_ssm_a_re', 'v_ssm_a_im', 'v_ssm_log_dt', 'v_ssm_b_re', 'v_ssm_b_im', 'v_ssm_c_re', 'v_ssm_c_im', 'v_ssm_d', 'v_w_glu', 'v_b_glu', 'v_w_branch_attn', 'v_w_branch_ssm', 'v_w_out', 'v_ffn_norm_g', 'v_w_up', 'v_conv_w', 'v_conv_b', 'v_w_down', 'v_final_norm_g']
TWIN_OUTPUTS = ['loss', 'grad_x', 'grad_attn_norm_g', 'grad_w_in', 'grad_b_in', 'grad_attn_sinks', 'grad_ssm_a_re', 'grad_ssm_a_im', 'grad_ssm_log_dt', 'grad_ssm_b_re', 'grad_ssm_b_im', 'grad_ssm_c_re', 'grad_ssm_c_im', 'grad_ssm_d', 'grad_w_glu', 'grad_b_glu', 'grad_w_branch_attn', 'grad_w_branch_ssm', 'grad_w_out', 'grad_ffn_norm_g', 'grad_w_up', 'grad_conv_w', 'grad_conv_b', 'grad_w_down', 'grad_final_norm_g', 'delta_attn_norm_g', 'delta_w_in', 'delta_b_in', 'delta_attn_sinks', 'delta_ssm_a_re', 'delta_ssm_a_im', 'delta_ssm_log_dt', 'delta_ssm_b_re', 'delta_ssm_b_im', 'delta_ssm_c_re', 'delta_ssm_c_im', 'delta_ssm_d', 'delta_w_glu', 'delta_b_glu', 'delta_w_branch_attn', 'delta_w_branch_ssm', 'delta_w_out', 'delta_ffn_norm_g', 'delta_w_up', 'delta_conv_w', 'delta_conv_b', 'delta_w_down', 'delta_final_norm_g', 'new_m_attn_norm_g', 'new_m_w_in', 'new_m_b_in', 'new_m_attn_sinks', 'new_m_ssm_a_re', 'new_m_ssm_a_im', 'new_m_ssm_log_dt', 'new_m_ssm_b_re', 'new_m_ssm_b_im', 'new_m_ssm_c_re', 'new_m_ssm_c_im', 'new_m_ssm_d', 'new_m_w_glu', 'new_m_b_glu', 'new_m_w_branch_attn', 'new_m_w_branch_ssm', 'new_m_w_out', 'new_m_ffn_norm_g', 'new_m_w_up', 'new_m_conv_w', 'new_m_conv_b', 'new_m_w_down', 'new_m_final_norm_g', 'new_v_attn_norm_g', 'new_v_w_in', 'new_v_b_in', 'new_v_attn_sinks', 'new_v_ssm_a_re', 'new_v_ssm_a_im', 'new_v_ssm_log_dt', 'new_v_ssm_b_re', 'new_v_ssm_b_im', 'new_v_ssm_c_re', 'new_v_ssm_c_im', 'new_v_ssm_d', 'new_v_w_glu', 'new_v_b_glu', 'new_v_w_branch_attn', 'new_v_w_branch_ssm', 'new_v_w_out', 'new_v_ffn_norm_g', 'new_v_w_up', 'new_v_conv_w', 'new_v_conv_b', 'new_v_w_down', 'new_v_final_norm_g']
TWIN_LEAF_KINDS = {'loss': 'loss', 'grad_x': 'grad_x', 'grad_attn_norm_g': 'grad_w', 'grad_w_in': 'grad_w', 'grad_b_in': 'grad_w', 'grad_attn_sinks': 'grad_w', 'grad_ssm_a_re': 'grad_w', 'grad_ssm_a_im': 'grad_w', 'grad_ssm_log_dt': 'grad_w', 'grad_ssm_b_re': 'grad_w', 'grad_ssm_b_im': 'grad_w', 'grad_ssm_c_re': 'grad_w', 'grad_ssm_c_im': 'grad_w', 'grad_ssm_d': 'grad_w', 'grad_w_glu': 'grad_w', 'grad_b_glu': 'grad_w', 'grad_w_branch_attn': 'grad_w', 'grad_w_branch_ssm': 'grad_w', 'grad_w_out': 'grad_w', 'grad_ffn_norm_g': 'grad_w', 'grad_w_up': 'grad_w', 'grad_conv_w': 'grad_w', 'grad_conv_b': 'grad_w', 'grad_w_down': 'grad_w', 'grad_final_norm_g': 'grad_w', 'delta_attn_norm_g': 'delta_w', 'delta_w_in': 'delta_w', 'delta_b_in': 'delta_w', 'delta_attn_sinks': 'delta_w', 'delta_ssm_a_re': 'delta_w', 'delta_ssm_a_im': 'delta_w', 'delta_ssm_log_dt': 'delta_w', 'delta_ssm_b_re': 'delta_w', 'delta_ssm_b_im': 'delta_w', 'delta_ssm_c_re': 'delta_w', 'delta_ssm_c_im': 'delta_w', 'delta_ssm_d': 'delta_w', 'delta_w_glu': 'delta_w', 'delta_b_glu': 'delta_w', 'delta_w_branch_attn': 'delta_w', 'delta_w_branch_ssm': 'delta_w', 'delta_w_out': 'delta_w', 'delta_ffn_norm_g': 'delta_w', 'delta_w_up': 'delta_w', 'delta_conv_w': 'delta_w', 'delta_conv_b': 'delta_w', 'delta_w_down': 'delta_w', 'delta_final_norm_g': 'delta_w', 'new_m_attn_norm_g': 'new_m', 'new_m_w_in': 'new_m', 'new_m_b_in': 'new_m', 'new_m_attn_sinks': 'new_m', 'new_m_ssm_a_re': 'new_m', 'new_m_ssm_a_im': 'new_m', 'new_m_ssm_log_dt': 'new_m', 'new_m_ssm_b_re': 'new_m', 'new_m_ssm_b_im': 'new_m', 'new_m_ssm_c_re': 'new_m', 'new_m_ssm_c_im': 'new_m', 'new_m_ssm_d': 'new_m', 'new_m_w_glu': 'new_m', 'new_m_b_glu': 'new_m', 'new_m_w_branch_attn': 'new_m', 'new_m_w_branch_ssm': 'new_m', 'new_m_w_out': 'new_m', 'new_m_ffn_norm_g': 'new_m', 'new_m_w_up': 'new_m', 'new_m_conv_w': 'new_m', 'new_m_conv_b': 'new_m', 'new_m_w_down': 'new_m', 'new_m_final_norm_g': 'new_m', 'new_v_attn_norm_g': 'new_v', 'new_v_w_in': 'new_v', 'new_v_b_in': 'new_v', 'new_v_attn_sinks': 'new_v', 'new_v_ssm_a_re': 'new_v', 'new_v_ssm_a_im': 'new_v', 'new_v_ssm_log_dt': 'new_v', 'new_v_ssm_b_re': 'new_v', 'new_v_ssm_b_im': 'new_v', 'new_v_ssm_c_re': 'new_v', 'new_v_ssm_c_im': 'new_v', 'new_v_ssm_d': 'new_v', 'new_v_w_glu': 'new_v', 'new_v_b_glu': 'new_v', 'new_v_w_branch_attn': 'new_v', 'new_v_w_branch_ssm': 'new_v', 'new_v_w_out': 'new_v', 'new_v_ffn_norm_g': 'new_v', 'new_v_w_up': 'new_v', 'new_v_conv_w': 'new_v', 'new_v_conv_b': 'new_v', 'new_v_w_down': 'new_v', 'new_v_final_norm_g': 'new_v'}


def _forward(args):
    return _fwd_reference(*[args[k] for k in FWD_PARAMS])


def _output_shape():
    def fwd():
        inp = _fwd_setup_inputs(0)
        return _fwd_reference(*[inp[k] for k in FWD_PARAMS])
    out = _jax.eval_shape(fwd)
    return out.shape, out.dtype

N_MICROBATCH = 1
ADAM_LR = 0.001
ADAM_B1 = 0.9
ADAM_B2 = 0.999
ADAM_EPS = 1e-08
ADAM_WD = 0.01
ADAM_STEP = 10
PER_EXAMPLE_BATCH_AXIS = {'x': 0, 'loss_target': 0}
SHARED_INPUTS = []
_WEIGHT_DTYPES = {'attn_norm_g': _jnp.float32, 'w_in': _jnp.float32, 'b_in': _jnp.float32, 'attn_sinks': _jnp.float32, 'ssm_a_re': _jnp.float32, 'ssm_a_im': _jnp.float32, 'ssm_log_dt': _jnp.float32, 'ssm_b_re': _jnp.float32, 'ssm_b_im': _jnp.float32, 'ssm_c_re': _jnp.float32, 'ssm_c_im': _jnp.float32, 'ssm_d': _jnp.float32, 'w_glu': _jnp.float32, 'b_glu': _jnp.float32, 'w_branch_attn': _jnp.float32, 'w_branch_ssm': _jnp.float32, 'w_out': _jnp.float32, 'ffn_norm_g': _jnp.float32, 'w_up': _jnp.float32, 'conv_w': _jnp.float32, 'conv_b': _jnp.float32, 'w_down': _jnp.float32, 'final_norm_g': _jnp.float32}
MOMENT_SCALE = {'attn_norm_g': 4.701795e-02, 'w_in': 2.762401e-02, 'b_in': 4.956543e-02, 'attn_sinks': 1.519011e-01, 'ssm_a_re': 2.708899e-03, 'ssm_a_im': 3.364007e-03, 'ssm_log_dt': 2.731178e+00, 'ssm_b_re': 1.889941e-03, 'ssm_b_im': 1.852877e-03, 'ssm_c_re': 3.769650e-03, 'ssm_c_im': 3.763967e-03, 'ssm_d': 5.795621e-02, 'w_glu': 3.798516e-02, 'b_glu': 5.806253e-02, 'w_branch_attn': 2.419235e-02, 'w_branch_ssm': 2.604985e-02, 'w_out': 3.511012e-02, 'ffn_norm_g': 9.607305e-02, 'w_up': 4.148275e-02, 'conv_w': 4.286165e-02, 'conv_b': 4.272282e-02, 'w_down': 6.778605e-02, 'final_norm_g': 3.195381e+01}


def _to_microbatches(a, axis):
    t = _jnp.moveaxis(a, axis, 0)
    t = t.reshape((N_MICROBATCH, t.shape[0] // N_MICROBATCH) + t.shape[1:])
    return _jnp.moveaxis(t, 1, axis + 1)


def setup_inputs(seed: int = 0) -> dict:
    inp = _fwd_setup_inputs(seed)
    key = _jax.random.fold_in(_jax.random.key(seed), 7919)
    shape, _ = _output_shape()
    out = dict(inp)
    out["loss_target"] = _jax.random.normal(_jax.random.fold_in(key, 0), shape, _jnp.float32)
    for i, name in enumerate(TWIN_WEIGHTS):
        w = inp[name].astype(_jnp.float32)
        if MOMENT_SCALE is None:
            s = _jnp.sqrt(_jnp.mean(_jnp.square(w)) + 1e-30)
        else:
            s = MOMENT_SCALE[name]
        km, kv = _jax.random.split(_jax.random.fold_in(key, i + 1))
        out[name] = w
        out["m_" + name] = s * _jax.random.normal(km, w.shape, _jnp.float32)
        out["v_" + name] = (s * s) * _jax.random.uniform(kv, w.shape, _jnp.float32, 0.5, 1.5)
    if N_MICROBATCH > 1:
        for name, axis in PER_EXAMPLE_BATCH_AXIS.items():
            out[name] = _to_microbatches(out[name], axis)
    return {'x': out['x'], 'attn_norm_g': out['attn_norm_g'], 'w_in': out['w_in'], 'b_in': out['b_in'], 'attn_sinks': out['attn_sinks'], 'ssm_a_re': out['ssm_a_re'], 'ssm_a_im': out['ssm_a_im'], 'ssm_log_dt': out['ssm_log_dt'], 'ssm_b_re': out['ssm_b_re'], 'ssm_b_im': out['ssm_b_im'], 'ssm_c_re': out['ssm_c_re'], 'ssm_c_im': out['ssm_c_im'], 'ssm_d': out['ssm_d'], 'w_glu': out['w_glu'], 'b_glu': out['b_glu'], 'w_branch_attn': out['w_branch_attn'], 'w_branch_ssm': out['w_branch_ssm'], 'w_out': out['w_out'], 'ffn_norm_g': out['ffn_norm_g'], 'w_up': out['w_up'], 'conv_w': out['conv_w'], 'conv_b': out['conv_b'], 'w_down': out['w_down'], 'final_norm_g': out['final_norm_g'], 'loss_target': out['loss_target'], 'm_attn_norm_g': out['m_attn_norm_g'], 'm_w_in': out['m_w_in'], 'm_b_in': out['m_b_in'], 'm_attn_sinks': out['m_attn_sinks'], 'm_ssm_a_re': out['m_ssm_a_re'], 'm_ssm_a_im': out['m_ssm_a_im'], 'm_ssm_log_dt': out['m_ssm_log_dt'], 'm_ssm_b_re': out['m_ssm_b_re'], 'm_ssm_b_im': out['m_ssm_b_im'], 'm_ssm_c_re': out['m_ssm_c_re'], 'm_ssm_c_im': out['m_ssm_c_im'], 'm_ssm_d': out['m_ssm_d'], 'm_w_glu': out['m_w_glu'], 'm_b_glu': out['m_b_glu'], 'm_w_branch_attn': out['m_w_branch_attn'], 'm_w_branch_ssm': out['m_w_branch_ssm'], 'm_w_out': out['m_w_out'], 'm_ffn_norm_g': out['m_ffn_norm_g'], 'm_w_up': out['m_w_up'], 'm_conv_w': out['m_conv_w'], 'm_conv_b': out['m_conv_b'], 'm_w_down': out['m_w_down'], 'm_final_norm_g': out['m_final_norm_g'], 'v_attn_norm_g': out['v_attn_norm_g'], 'v_w_in': out['v_w_in'], 'v_b_in': out['v_b_in'], 'v_attn_sinks': out['v_attn_sinks'], 'v_ssm_a_re': out['v_ssm_a_re'], 'v_ssm_a_im': out['v_ssm_a_im'], 'v_ssm_log_dt': out['v_ssm_log_dt'], 'v_ssm_b_re': out['v_ssm_b_re'], 'v_ssm_b_im': out['v_ssm_b_im'], 'v_ssm_c_re': out['v_ssm_c_re'], 'v_ssm_c_im': out['v_ssm_c_im'], 'v_ssm_d': out['v_ssm_d'], 'v_w_glu': out['v_w_glu'], 'v_b_glu': out['v_b_glu'], 'v_w_branch_attn': out['v_w_branch_attn'], 'v_w_branch_ssm': out['v_w_branch_ssm'], 'v_w_out': out['v_w_out'], 'v_ffn_norm_g': out['v_ffn_norm_g'], 'v_w_up': out['v_w_up'], 'v_conv_w': out['v_conv_w'], 'v_conv_b': out['v_conv_b'], 'v_w_down': out['v_w_down'], 'v_final_norm_g': out['v_final_norm_g']}


def _loss(weights, diff, rest, loss_target):
    with _jax.named_scope("forward"):
        args = {**rest, TWIN_DIFF_INPUT: diff, **{k: w.astype(_WEIGHT_DTYPES[k]) for k, w in weights.items()}}
        y = _forward(args)
    with _jax.named_scope("loss_head"):
        err = _jnp.square(y.astype(_jnp.float32) - loss_target)
        return 0.5 * _jnp.sum(_jnp.mean(err, axis=-1)) if err.ndim else 0.5 * err


def _adamw(w, g, m, v):
    m = ADAM_B1 * m + (1.0 - ADAM_B1) * g
    v = ADAM_B2 * v + (1.0 - ADAM_B2) * _jnp.square(g)
    m_hat = m / (1.0 - ADAM_B1 ** ADAM_STEP)
    v_hat = v / (1.0 - ADAM_B2 ** ADAM_STEP)
    delta = -ADAM_LR * (m_hat / (_jnp.sqrt(v_hat) + ADAM_EPS) + ADAM_WD * w)
    return delta, m, v


def reference(x, attn_norm_g, w_in, b_in, attn_sinks, ssm_a_re, ssm_a_im, ssm_log_dt, ssm_b_re, ssm_b_im, ssm_c_re, ssm_c_im, ssm_d, w_glu, b_glu, w_branch_attn, w_branch_ssm, w_out, ffn_norm_g, w_up, conv_w, conv_b, w_down, final_norm_g, loss_target, m_attn_norm_g, m_w_in, m_b_in, m_attn_sinks, m_ssm_a_re, m_ssm_a_im, m_ssm_log_dt, m_ssm_b_re, m_ssm_b_im, m_ssm_c_re, m_ssm_c_im, m_ssm_d, m_w_glu, m_b_glu, m_w_branch_attn, m_w_branch_ssm, m_w_out, m_ffn_norm_g, m_w_up, m_conv_w, m_conv_b, m_w_down, m_final_norm_g, v_attn_norm_g, v_w_in, v_b_in, v_attn_sinks, v_ssm_a_re, v_ssm_a_im, v_ssm_log_dt, v_ssm_b_re, v_ssm_b_im, v_ssm_c_re, v_ssm_c_im, v_ssm_d, v_w_glu, v_b_glu, v_w_branch_attn, v_w_branch_ssm, v_w_out, v_ffn_norm_g, v_w_up, v_conv_w, v_conv_b, v_w_down, v_final_norm_g):
    given = dict(x=x, attn_norm_g=attn_norm_g, w_in=w_in, b_in=b_in, attn_sinks=attn_sinks, ssm_a_re=ssm_a_re, ssm_a_im=ssm_a_im, ssm_log_dt=ssm_log_dt, ssm_b_re=ssm_b_re, ssm_b_im=ssm_b_im, ssm_c_re=ssm_c_re, ssm_c_im=ssm_c_im, ssm_d=ssm_d, w_glu=w_glu, b_glu=b_glu, w_branch_attn=w_branch_attn, w_branch_ssm=w_branch_ssm, w_out=w_out, ffn_norm_g=ffn_norm_g, w_up=w_up, conv_w=conv_w, conv_b=conv_b, w_down=w_down, final_norm_g=final_norm_g, loss_target=loss_target, m_attn_norm_g=m_attn_norm_g, m_w_in=m_w_in, m_b_in=m_b_in, m_attn_sinks=m_attn_sinks, m_ssm_a_re=m_ssm_a_re, m_ssm_a_im=m_ssm_a_im, m_ssm_log_dt=m_ssm_log_dt, m_ssm_b_re=m_ssm_b_re, m_ssm_b_im=m_ssm_b_im, m_ssm_c_re=m_ssm_c_re, m_ssm_c_im=m_ssm_c_im, m_ssm_d=m_ssm_d, m_w_glu=m_w_glu, m_b_glu=m_b_glu, m_w_branch_attn=m_w_branch_attn, m_w_branch_ssm=m_w_branch_ssm, m_w_out=m_w_out, m_ffn_norm_g=m_ffn_norm_g, m_w_up=m_w_up, m_conv_w=m_conv_w, m_conv_b=m_conv_b, m_w_down=m_w_down, m_final_norm_g=m_final_norm_g, v_attn_norm_g=v_attn_norm_g, v_w_in=v_w_in, v_b_in=v_b_in, v_attn_sinks=v_attn_sinks, v_ssm_a_re=v_ssm_a_re, v_ssm_a_im=v_ssm_a_im, v_ssm_log_dt=v_ssm_log_dt, v_ssm_b_re=v_ssm_b_re, v_ssm_b_im=v_ssm_b_im, v_ssm_c_re=v_ssm_c_re, v_ssm_c_im=v_ssm_c_im, v_ssm_d=v_ssm_d, v_w_glu=v_w_glu, v_b_glu=v_b_glu, v_w_branch_attn=v_w_branch_attn, v_w_branch_ssm=v_w_branch_ssm, v_w_out=v_w_out, v_ffn_norm_g=v_ffn_norm_g, v_w_up=v_w_up, v_conv_w=v_conv_w, v_conv_b=v_conv_b, v_w_down=v_w_down, v_final_norm_g=v_final_norm_g)
    weights = {n: given[n] for n in TWIN_WEIGHTS}
    shared = {n: given[n] for n in SHARED_INPUTS}
    per_example = {n: given[n] for n in ['x']}
    grad_fn = _jax.value_and_grad(_loss, argnums=(0, 1))

    def one_microbatch(ex, loss_target):
        ex = dict(ex)
        diff = ex.pop(TWIN_DIFF_INPUT)
        return grad_fn(weights, diff, {**shared, **ex}, loss_target)

    if N_MICROBATCH == 1:
        loss, (grad_w, grad_x) = one_microbatch(per_example, given["loss_target"])
    else:
        def body(carry, xs):
            loss_sum, grad_sum = carry
            l_k, (gw_k, gx_k) = one_microbatch(xs[0], xs[1])
            with _jax.named_scope("update"):
                return (loss_sum + l_k, _jax.tree.map(_jnp.add, grad_sum, gw_k)), gx_k

        init = (_jnp.zeros((), _jnp.float32), _jax.tree.map(_jnp.zeros_like, weights))
        (loss, grad_w), grad_x = _jax.lax.scan(body, init, (per_example, given["loss_target"]))
    with _jax.named_scope("update"):
        delta_w, new_m, new_v = {}, {}, {}
        for n in TWIN_WEIGHTS:
            delta_w[n], new_m[n], new_v[n] = _adamw(weights[n], grad_w[n], given["m_" + n], given["v_" + n])
    return (loss, grad_x, *[grad_w[n] for n in TWIN_WEIGHTS], *[delta_w[n] for n in TWIN_WEIGHTS],
            *[new_m[n] for n in TWIN_WEIGHTS], *[new_v[n] for n in TWIN_WEIGHTS])
```

```python
import functools
import math

import jax
import jax.numpy as jnp
from jax import lax
from jax.experimental import pallas as pl
from jax.experimental.pallas import tpu as pltpu

F32 = jnp.float32
BF16 = jnp.bfloat16
SDS = jax.ShapeDtypeStruct

N_DEV = 8
D = 2048
NQ, NKV, HD = 16, 2, 64
AW = NQ * HD
BLK = 128
SW, G, H, P = 512, 32, 16, 64
NS = G * P
DFF = 5632
INC = AW + 2 * NKV * HD + SW + 2 * D
INP = 6144
C_K, C_U, C_GA, C_GS = AW, AW + 2 * NKV * HD, AW + 2 * NKV * HD + SW, AW + 2 * NKV * HD + SW + D
RMS_EPS = 1e-6
NEG_BIG = -1e30
ADAM_LR, ADAM_B1, ADAM_B2, ADAM_EPS, ADAM_WD, ADAM_STEP = 0.001, 0.9, 0.999, 1e-08, 0.01, 10
NSEG = 8
VMEM_CAP_MB = 60
MESH_AXES = ("x", "y", "c")


def _cparams(sem, vmem_mb):
    return pltpu.CompilerParams(dimension_semantics=sem, vmem_limit_bytes=min(int(vmem_mb), VMEM_CAP_MB) << 20)


def _tile(dim, pref):
    if dim <= pref:
        return dim
    t = pref
    while t >= 128:
        if dim % t == 0:
            return t
        t //= 2
    raise ValueError(f"no tile for {dim}")


def _mm(a, b, *, ta=False, tb=False, bias=None, res=None, out_dtype=F32, out2_dtype=None,
        tm=1024, tn=1024, tk=512, name):
    m, k = (a.shape[1], a.shape[0]) if ta else a.shape
    n = b.shape[0] if tb else b.shape[1]
    assert (b.shape[1] if tb else b.shape[0]) == k, (a.shape, b.shape, ta, tb)
    tm, tn, tk = _tile(m, tm), _tile(n, tn), _tile(k, tk)
    nk = k // tk
    dims = (((0 if ta else 1,), (1 if tb else 0,)), ((), ()))
    has_bias, has_res, has_o2 = bias is not None, res is not None, out2_dtype is not None

    def body(*refs):
        a_ref, b_ref = refs[0], refs[1]
        pos = 2
        bias_ref = refs[pos] if has_bias else None
        pos += has_bias
        res_ref = refs[pos] if has_res else None
        pos += has_res
        o_ref = refs[pos]
        o2_ref = refs[pos + 1] if has_o2 else None
        acc_ref = refs[-1]
        kk = pl.program_id(2)

        @pl.when(kk == 0)
        def _():
            acc_ref[...] = jnp.zeros_like(acc_ref)

        acc_ref[...] += lax.dot_general(a_ref[...].astype(BF16), b_ref[...].astype(BF16), dims,
                                        preferred_element_type=F32)

        @pl.when(kk == nk - 1)
        def _():
            r = acc_ref[...]
            if has_bias:
                r = r + bias_ref[...]
            if has_res:
                r = r + res_ref[...].astype(F32)
            o_ref[...] = r.astype(o_ref.dtype)
            if has_o2:
                o2_ref[...] = r.astype(o2_ref.dtype)

    ins = [a, b]
    in_specs = [pl.BlockSpec((tk, tm), lambda i, j, kk: (kk, i)) if ta else pl.BlockSpec((tm, tk), lambda i, j, kk: (i, kk)),
                pl.BlockSpec((tn, tk), lambda i, j, kk: (j, kk)) if tb else pl.BlockSpec((tk, tn), lambda i, j, kk: (kk, j))]
    byt = 2 * tm * tk * a.dtype.itemsize + 2 * tk * tn * b.dtype.itemsize + 4 * tm * tn
    if has_bias:
        ins.append(bias)
        in_specs.append(pl.BlockSpec((1, tn), lambda i, j, kk: (0, j)))
    if has_res:
        ins.append(res)
        in_specs.append(pl.BlockSpec((tm, tn), lambda i, j, kk: (i, j)))
        byt += 2 * tm * tn * res.dtype.itemsize
    o_spec = pl.BlockSpec((tm, tn), lambda i, j, kk: (i, j))
    out_shape, out_specs = SDS((m, n), out_dtype), o_spec
    byt += 2 * tm * tn * jnp.dtype(out_dtype).itemsize
    if has_o2:
        out_shape, out_specs = (out_shape, SDS((m, n), out2_dtype)), (o_spec, o_spec)
        byt += 2 * tm * tn * jnp.dtype(out2_dtype).itemsize
    return pl.pallas_call(
        body, out_shape=out_shape, grid=(m // tm, n // tn, nk), in_specs=in_specs, out_specs=out_specs,
        scratch_shapes=[pltpu.VMEM((tm, tn), F32)], name=name,
        compiler_params=_cparams(("parallel", "parallel", "arbitrary"), byt / 2**20 + 12),
    )(*ins)


def _ew(fn, ins, outs, grid, name, vmem_mb=40):
    n_in = len(ins)
    accs = [o[3] for o in outs]

    def body(*refs):
        c, r = pl.program_id(0), pl.program_id(1)
        vals = fn(c, r, *[ref[...] for ref in refs[:n_in]])
        for o_ref, v, acc in zip(refs[n_in:], vals, accs):
            if acc is None:
                o_ref[...] = v.astype(o_ref.dtype)
            else:
                first = (r == 0) if acc == "r" else jnp.logical_and(r == 0, c == 0)

                @pl.when(first)
                def _(o_ref=o_ref, v=v):
                    o_ref[...] = v.astype(o_ref.dtype)

                @pl.when(jnp.logical_not(first))
                def _(o_ref=o_ref, v=v):
                    o_ref[...] += v.astype(o_ref.dtype)

    res = pl.pallas_call(
        body, out_shape=tuple(o[0] for o in outs), grid=grid,
        in_specs=[pl.BlockSpec(bs, im) for _, bs, im in ins],
        out_specs=tuple(pl.BlockSpec(bs, im) for _, bs, im, _ in outs), name=name,
        compiler_params=_cparams(("arbitrary", "arbitrary"), vmem_mb),
    )(*[a for a, _, _ in ins])
    return res


def _rc(tm, tc, coff=0):
    return (tm, tc), (lambda c, r: (r, c + coff))


def _col(rows, tc, coff=0):
    return (rows, tc), (lambda c, r: (0, c + coff))


def _gelu(x):
    return 0.5 * x * (1.0 + lax.erf(x * (2.0 ** -0.5)))


def _gelu_grad(x):
    return 0.5 * (1.0 + lax.erf(x * (2.0 ** -0.5))) + x * jnp.exp(-0.5 * x * x) * (1.0 / math.sqrt(2.0 * math.pi))


def _sigmoid(x):
    return 1.0 / (1.0 + jnp.exp(-x))


def _shift_rows(x, halo, s):
    rolled = pltpu.roll(x, s, 0)
    row8 = lax.broadcasted_iota(jnp.int32, halo.shape, 0)
    head = jnp.where(row8 < s, pltpu.roll(halo, s, 0), rolled[0:8])
    return jnp.concatenate([head, rolled[8:]], axis=0)


def _shift_rows_up(x, halo, s):
    tm = x.shape[0]
    rolled = pltpu.roll(x, tm - s, 0)
    row8 = lax.broadcasted_iota(jnp.int32, halo.shape, 0)
    tail = jnp.where(row8 >= 8 - s, pltpu.roll(halo, 8 - s, 0), rolled[tm - 8:])
    return jnp.concatenate([rolled[:tm - 8], tail], axis=0)


def _rmsnorm_fwd(x, g, name, tm=256):
    L = x.shape[0]

    def fn(c, r, xb, gb):
        rstd = lax.rsqrt(jnp.mean(xb * xb, axis=-1, keepdims=True) + RMS_EPS)
        return ((xb * rstd) * gb,)

    return _ew(fn, [(x, *_rc(tm, D)), (g, *_col(1, D))], [(SDS((L, D), BF16), *_rc(tm, D), None)], (1, L // tm), name)[0]


def _rmsnorm_bwd(dh, x, g, dres, name, tm=256):
    L = x.shape[0]

    def fn(c, r, dhb, xb, gb, drb):
        rstd = lax.rsqrt(jnp.mean(xb * xb, axis=-1, keepdims=True) + RMS_EPS)
        y = xb * rstd
        dy = dhb * gb
        dx = rstd * (dy - y * jnp.mean(dy * y, axis=-1, keepdims=True))
        return drb + dx, jnp.sum(dhb * y, axis=0, keepdims=True)

    return _ew(fn, [(dh, *_rc(tm, D)), (x, *_rc(tm, D)), (g, *_col(1, D)), (dres, *_rc(tm, D))],
               [(SDS((L, D), F32), *_rc(tm, D), None), (SDS((1, D), F32), *_col(1, D), "all")], (1, L // tm), name)


def _final_loss(x2, g, tgt, name, tm=256):
    L = x2.shape[0]

    def fn(c, r, xb, gb, tb):
        rstd = lax.rsqrt(jnp.mean(xb * xb, axis=-1, keepdims=True) + RMS_EPS)
        y = xb * rstd
        err = y * gb - tb
        dout = err * (1.0 / D)
        dy = dout * gb
        dx = rstd * (dy - y * jnp.mean(dy * y, axis=-1, keepdims=True))
        return dx, jnp.sum(err * err, axis=0, keepdims=True) * (0.5 / D), jnp.sum(dout * y, axis=0, keepdims=True)

    return _ew(fn, [(x2, *_rc(tm, D)), (g, *_col(1, D)), (tgt, *_rc(tm, D))],
               [(SDS((L, D), F32), *_rc(tm, D), None), (SDS((1, D), F32), *_col(1, D), "all"),
                (SDS((1, D), F32), *_col(1, D), "all")], (1, L // tm), name)


def _attn_setup(n, kvc, kvp):
    kv = jnp.concatenate([kvp, kvc], axis=0).astype(F32)
    lo = lax.broadcasted_iota(jnp.int32, (2 * BLK, 128), 1) < HD

    def halves(t):
        tr = pltpu.roll(t, HD, 1)
        z = jnp.zeros_like(t)
        return {(0, 0): jnp.where(lo, t, z).astype(BF16), (0, 1): jnp.where(lo, z, tr).astype(BF16),
                (1, 0): jnp.where(lo, tr, z).astype(BF16), (1, 1): jnp.where(lo, z, t).astype(BF16)}

    kmat, vmat = halves(kv[:, :128]), halves(kv[:, 128:])
    qi = lax.broadcasted_iota(jnp.int32, (BLK, 2 * BLK), 0)
    si = lax.broadcasted_iota(jnp.int32, (BLK, 2 * BLK), 1)
    dist = qi + BLK - si
    valid = (dist >= 0) & (dist < BLK) & ((n > 0) | (si >= BLK))
    return kmat, vmat, valid, dist.astype(F32)


def _attn_probs(qp, kmat_ge, valid, distf, slope, sink):
    s = lax.dot_general(qp, kmat_ge, (((1,), (1,)), ((), ())), preferred_element_type=F32) * (HD ** -0.5)
    s = jnp.where(valid, s - slope * distf, NEG_BIG)
    m = jnp.maximum(jnp.max(s, axis=-1, keepdims=True), sink)
    p = jnp.exp(s - m)
    esink = jnp.exp(sink - m)
    den = jnp.sum(p, axis=-1, keepdims=True) + esink
    return p / den, esink / den


def _slope(h):
    return 2.0 ** (-8.0 * (h + 1) / NQ)


def _attn_fwd(projb, sinks, name):
    L = projb.shape[0]
    nb = L // BLK

    def body(s_ref, q_ref, kvc_ref, kvp_ref, o_ref):
        n = pl.program_id(0)
        kmat, vmat, valid, distf = _attn_setup(n, kvc_ref[...], kvp_ref[...])
        for j in range(NQ // 2):
            g = j // (NQ // 4)
            qp = q_ref[:, 128 * j:128 * (j + 1)]
            acc = jnp.zeros((BLK, 128), F32)
            for e in range(2):
                h = 2 * j + e
                p, _ = _attn_probs(qp, kmat[(g, e)], valid, distf, _slope(h), s_ref[0, h])
                acc = acc + jnp.dot(p.astype(BF16), vmat[(g, e)], preferred_element_type=F32)
            o_ref[:, 128 * j:128 * (j + 1)] = acc.astype(BF16)

    return pl.pallas_call(
        body, out_shape=SDS((L, AW), BF16), grid=(nb,),
        in_specs=[pl.BlockSpec(memory_space=pltpu.SMEM),
                  pl.BlockSpec((BLK, AW), lambda n: (n, 0)),
                  pl.BlockSpec((BLK, 256), lambda n: (n, C_K // 256)),
                  pl.BlockSpec((BLK, 256), lambda n: (jnp.maximum(n - 1, 0), C_K // 256))],
        out_specs=pl.BlockSpec((BLK, AW), lambda n: (n, 0)), name=name,
        compiler_params=_cparams(("arbitrary",), 32),
    )(sinks, projb, projb, projb)


def _attn_bwd(projb, sinks, dattn, name):
    L = projb.shape[0]
    nb = L // BLK

    def body(s_ref, q_ref, kvc_ref, kvp_ref, do_ref, dq_ref, dcur_ref, dprev_ref, dsink_ref):
        n = pl.program_id(0)
        kmat, vmat, valid, distf = _attn_setup(n, kvc_ref[...], kvp_ref[...])
        lo128 = lax.broadcasted_iota(jnp.int32, (BLK, 128), 1) < HD
        lane = lax.broadcasted_iota(jnp.int32, (1, 128), 1)
        dk = [jnp.zeros((2 * BLK, 128), F32) for _ in range(NKV)]
        dv = [jnp.zeros((2 * BLK, 128), F32) for _ in range(NKV)]
        dsv = jnp.zeros((1, 128), F32)
        tn_dims = (((0,), (0,)), ((), ()))
        for j in range(NQ // 2):
            g = j // (NQ // 4)
            qp = q_ref[:, 128 * j:128 * (j + 1)]
            dop = do_ref[:, 128 * j:128 * (j + 1)]
            dqp = jnp.zeros((BLK, 128), F32)
            for e in range(2):
                h = 2 * j + e
                p, psink = _attn_probs(qp, kmat[(g, e)], valid, distf, _slope(h), s_ref[0, h])
                dp = lax.dot_general(dop, vmat[(g, e)], (((1,), (1,)), ((), ())), preferred_element_type=F32)
                drow = jnp.sum(p * dp, axis=-1, keepdims=True)
                ds = p * (dp - drow)
                dsv = dsv + jnp.where(lane == h, -jnp.sum(psink * drow, axis=0, keepdims=True), 0.0)
                dsb = (ds * (HD ** -0.5)).astype(BF16)
                dqp = dqp + jnp.dot(dsb, kmat[(g, e)], preferred_element_type=F32)
                half = lo128 if e == 0 else jnp.logical_not(lo128)
                zb = jnp.zeros_like(qp)
                dk[g] = dk[g] + lax.dot_general(dsb, jnp.where(half, qp, zb), tn_dims, preferred_element_type=F32)
                dv[g] = dv[g] + lax.dot_general(p.astype(BF16), jnp.where(half, dop, zb), tn_dims,
                                                preferred_element_type=F32)
            dq_ref[:, 128 * j:128 * (j + 1)] = dqp.astype(BF16)
        lo256 = lax.broadcasted_iota(jnp.int32, (2 * BLK, 128), 1) < HD
        tot = [t + pltpu.roll(t, HD, 1) for t in (dk[0], dk[1], dv[0], dv[1])]
        dkv = jnp.concatenate([jnp.where(lo256, tot[0], tot[1]), jnp.where(lo256, tot[2], tot[3])], axis=1)
        dprev_ref[...] = dkv[:BLK]
        dcur_ref[...] = dkv[BLK:]

        @pl.when(n == 0)
        def _():
            dsink_ref[...] = dsv

        @pl.when(n > 0)
        def _():
            dsink_ref[...] += dsv

    return pl.pallas_call(
        body, out_shape=(SDS((L, AW), BF16), SDS((L, 256), F32), SDS((L, 256), F32), SDS((1, 128), F32)), grid=(nb,),
        in_specs=[pl.BlockSpec(memory_space=pltpu.SMEM),
                  pl.BlockSpec((BLK, AW), lambda n: (n, 0)),
                  pl.BlockSpec((BLK, 256), lambda n: (n, C_K // 256)),
                  pl.BlockSpec((BLK, 256), lambda n: (jnp.maximum(n - 1, 0), C_K // 256)),
                  pl.BlockSpec((BLK, AW), lambda n: (n, 0))],
        out_specs=(pl.BlockSpec((BLK, AW), lambda n: (n, 0)), pl.BlockSpec((BLK, 256), lambda n: (n, 0)),
                   pl.BlockSpec((BLK, 256), lambda n: (n, 0)), pl.BlockSpec((1, 128), lambda n: (0, 0))),
        name=name, compiler_params=_cparams(("arbitrary",), 32),
    )(sinks, projb, projb, projb, dattn)


def _disc(a_re, a_im, logdt, b_re, b_im):
    dt = jnp.exp(logdt)
    mag = jnp.exp(a_re * dt)
    ab_re = mag * jnp.cos(a_im * dt)
    ab_im = mag * jnp.sin(a_im * dt)
    nr = ab_re - 1.0
    ni = ab_im
    den = a_re * a_re + a_im * a_im
    z_re = (nr * a_re + ni * a_im) / den
    z_im = (ni * a_re - nr * a_im) / den
    return ab_re, ab_im, z_re * b_re - z_im * b_im, z_re * b_im + z_im * b_re


def _group_mask():
    row = lax.broadcasted_iota(jnp.int32, (SW, NS), 0) // H
    col = lax.broadcasted_iota(jnp.int32, (SW, NS), 1) // P
    return row == col


def _block_diag(re, im):
    mask = _group_mask()
    z = jnp.zeros((SW, NS), F32)
    return jnp.concatenate([jnp.where(mask, jnp.tile(re, (G, 1)), z), jnp.where(mask, jnp.tile(im, (G, 1)), z)], axis=1)


def _block_diag_t(big):
    mask = _group_mask()
    z = jnp.zeros((SW, NS), F32)
    re = jnp.sum(jnp.where(mask, big[:, :NS], z).reshape(G, H, NS), axis=0)
    im = jnp.sum(jnp.where(mask, big[:, NS:], z).reshape(G, H, NS), axis=0)
    return re, im


def _ssm_prep(a_re, a_im, logdt, b_re, b_im, c_re, c_im, name):
    def body(are, aim, ldt, bre, bim, cre, cim, ab_ref, bm_ref, cm_ref):
        ab_re, ab_im, bb_re, bb_im = _disc(are[...], aim[...], ldt[...], bre[...], bim[...])
        ab_ref[...] = jnp.concatenate([ab_re, ab_im], axis=1)
        bm_ref[...] = _block_diag(bb_re, bb_im).astype(BF16)
        cm_ref[...] = _block_diag(cre[...], -cim[...]).astype(BF16)

    return pl.pallas_call(body, out_shape=(SDS((1, 2 * NS), F32), SDS((SW, 2 * NS), BF16), SDS((SW, 2 * NS), BF16)),
                          name=name, compiler_params=pltpu.CompilerParams(vmem_limit_bytes=48 << 20),
                          )(a_re, a_im, logdt, b_re, b_im, c_re, c_im)


def _ssm_param_bwd(a_re, a_im, logdt, b_re, b_im, dab8, dbm, dcm, name):
    def body(are, aim, ldt, bre, bim, dab_ref, dbm_ref, dcm_ref, o_are, o_aim, o_ldt, o_bre, o_bim, o_cre, o_cim):
        dab = jnp.sum(dab_ref[...], axis=0, keepdims=True)
        dbb_re, dbb_im = _block_diag_t(dbm_ref[...])
        _, vjp = jax.vjp(_disc, are[...], aim[...], ldt[...], bre[...], bim[...])
        d_are, d_aim, d_ldt, d_bre, d_bim = vjp((dab[:, :NS], dab[:, NS:], dbb_re, dbb_im))
        o_are[...], o_aim[...], o_ldt[...], o_bre[...], o_bim[...] = d_are, d_aim, d_ldt, d_bre, d_bim
        dc_re, dc_imn = _block_diag_t(dcm_ref[...])
        o_cre[...] = dc_re
        o_cim[...] = -dc_imn

    v1, vh = SDS((1, NS), F32), SDS((H, NS), F32)
    return pl.pallas_call(body, out_shape=(v1, v1, v1, vh, vh, vh, vh), name=name,
                          compiler_params=pltpu.CompilerParams(vmem_limit_bytes=56 << 20),
                          )(a_re, a_im, logdt, b_re, b_im, dab8, dbm, dcm)


def _ssm_scan(src, wmat, ab, *, reverse, ends=None, xs=None, init=None, name, tk=32):
    L = src.shape[0]
    rows = NSEG * tk
    nch = L // rows
    seg_len = L // NSEG
    n_sq = int(math.log2(seg_len))
    assert 2 ** n_sq == seg_len and L % rows == 0
    first_pass = ends is None
    with_dab = (not first_pass) and reverse
    slab = 512
    n_slab = NS // slab

    def body(*refs):
        src_ref, w_ref, ab_ref = refs[:3]
        pos = 3
        if not first_pass:
            ends_ref = refs[pos]
            pos += 1
        if with_dab:
            xs_ref, xsh_ref, init_ref = refs[pos:pos + 3]
            pos += 3
        if first_pass:
            (e_ref,) = refs[pos:pos + 1]
            pos += 1
        else:
            st_out_ref, aux_ref = refs[pos:pos + 2]
            pos += 2
        buf_ref, st_ref = refs[pos:pos + 2]
        i = pl.program_id(0)
        a_re = ab_ref[:, :NS]
        a_im = -ab_ref[:, NS:] if reverse else ab_ref[:, NS:]

        @pl.when(i == 0)
        def _():
            if first_pass:
                st_ref[...] = jnp.zeros_like(st_ref)
            else:
                pr, pi = a_re, a_im
                for _ in range(n_sq):
                    pr, pi = pr * pr - pi * pi, 2.0 * pr * pi
                zr = jnp.zeros((1, NS), F32)
                cr, ci = zr, zr
                order = list(range(NSEG - 1, -1, -1)) if reverse else list(range(NSEG))
                st_ref[order[0]:order[0] + 1, :] = jnp.zeros((1, 2 * NS), F32)
                for jprev, j in zip(order[:-1], order[1:]):
                    er, ei = ends_ref[jprev:jprev + 1, :NS], ends_ref[jprev:jprev + 1, NS:]
                    cr, ci = er + pr * cr - pi * ci, ei + pr * ci + pi * cr
                    st_ref[j:j + 1, :NS] = cr
                    st_ref[j:j + 1, NS:] = ci
                if not reverse:
                    aux_ref[...] = st_ref[...]
                else:
                    aux_ref[...] = jnp.zeros_like(aux_ref)

        buf_ref[...] = jnp.dot(src_ref[...].astype(BF16), w_ref[...], preferred_element_type=F32)

        for s in range(n_slab):
            re_sl, im_sl = pl.ds(s * slab, slab), pl.ds(NS + s * slab, slab)
            ar = jnp.broadcast_to(a_re[:, s * slab:(s + 1) * slab], (NSEG, slab))
            ai = jnp.broadcast_to(a_im[:, s * slab:(s + 1) * slab], (NSEG, slab))

            def step(t, carry, re_sl=re_sl, im_sl=im_sl, ar=ar, ai=ai):
                k = (tk - 1 - t) if reverse else t
                r0 = pl.multiple_of(k * NSEG, NSEG)
                xr, xi = carry[0], carry[1]
                nr = ar * xr - ai * xi + buf_ref[pl.ds(r0, NSEG), re_sl]
                ni = ar * xi + ai * xr + buf_ref[pl.ds(r0, NSEG), im_sl]
                if not first_pass:
                    buf_ref[pl.ds(r0, NSEG), re_sl] = nr
                    buf_ref[pl.ds(r0, NSEG), im_sl] = ni
                if not with_dab:
                    return nr, ni
                rp = pl.multiple_of((k - 1) * NSEG, NSEG)
                xpr, xpi = xs_ref[pl.ds(rp, NSEG), re_sl], xs_ref[pl.ds(rp, NSEG), im_sl]
                return nr, ni, carry[2] + nr * xpr + ni * xpi, carry[3] + ni * xpr - nr * xpi

            carry = (st_ref[:, re_sl], st_ref[:, im_sl])
            if with_dab:
                z = jnp.zeros((NSEG, slab), F32)
                carry = lax.fori_loop(0, tk - 1, step, carry + (z, z))
                xr, xi, dr, di = carry
                nr = ar * xr - ai * xi + buf_ref[pl.ds(0, NSEG), re_sl]
                ni = ar * xi + ai * xr + buf_ref[pl.ds(0, NSEG), im_sl]
                buf_ref[pl.ds(0, NSEG), re_sl] = nr
                buf_ref[pl.ds(0, NSEG), im_sl] = ni
                at_start = i == nch - 1
                xpr = jnp.where(at_start, init_ref[:, re_sl], xsh_ref[:, re_sl])
                xpi = jnp.where(at_start, init_ref[:, im_sl], xsh_ref[:, im_sl])
                aux_ref[:, re_sl] += dr + nr * xpr + ni * xpi
                aux_ref[:, im_sl] += di + ni * xpr - nr * xpi
                carry = (nr, ni)
            else:
                carry = lax.fori_loop(0, tk, step, carry)
            st_ref[:, re_sl] = carry[0]
            st_ref[:, im_sl] = carry[1]

        if first_pass:
            @pl.when(i == nch - 1)
            def _():
                e_ref[...] = st_ref[...]
        else:
            st_out_ref[...] = buf_ref[...].astype(st_out_ref.dtype)

    chunk = (lambda i: (nch - 1 - i, 0)) if reverse else (lambda i: (i, 0))
    whole = lambda i: (0, 0)
    ins = [src, wmat, ab]
    in_specs = [pl.BlockSpec((rows, SW), chunk), pl.BlockSpec((SW, 2 * NS), whole), pl.BlockSpec((1, 2 * NS), whole)]
    small = SDS((NSEG, 2 * NS), F32)
    small_spec = pl.BlockSpec((NSEG, 2 * NS), whole)
    if not first_pass:
        ins.append(ends)
        in_specs.append(small_spec)
    if with_dab:
        ins += [xs, xs, init]
        in_specs += [pl.BlockSpec((rows, 2 * NS), chunk),
                     pl.BlockSpec((NSEG, 2 * NS), lambda i: (jnp.maximum((nch - 1 - i) * tk - 1, 0), 0)),
                     small_spec]
    if first_pass:
        out_shape, out_specs = small, small_spec
    else:
        out_shape = (SDS((L, 2 * NS), BF16 if reverse else F32), small)
        out_specs = (pl.BlockSpec((rows, 2 * NS), chunk), small_spec)
    return pl.pallas_call(
        body, out_shape=out_shape, grid=(nch,), in_specs=in_specs, out_specs=out_specs,
        scratch_shapes=[pltpu.VMEM((rows, 2 * NS), F32), pltpu.VMEM((NSEG, 2 * NS), F32)], name=name,
        compiler_params=_cparams(("arbitrary",), 56),
    )(*ins)


def _to_segments(a):
    L, c = a.shape
    return a.reshape(NSEG, L // NSEG, c).transpose(1, 0, 2).reshape(L, c)


def _from_segments(a):
    L, c = a.shape
    return a.reshape(L // NSEG, NSEG, c).transpose(1, 0, 2).reshape(L, c)


def _peer(x, y, c, m):
    return ((1 - x) if (m >> 2) & 1 else x, (1 - y) if (m >> 1) & 1 else y, (1 - c) if m & 1 else c)


def _dev_index(p):
    return 4 * p[0] + 2 * p[1] + p[2]


def _exchange(arrs, scatter, name):
    n = len(arrs)

    def body(*refs):
        ins, outs = refs[:n], refs[n:2 * n]
        send_sems, recv_sems, loc_sems = refs[2 * n:]
        x, y, c = lax.axis_index("x"), lax.axis_index("y"), lax.axis_index("c")
        me = _dev_index((x, y, c))

        def src(w, to):
            return ins[w].at[to] if scatter else ins[w]

        def local(w):
            return pltpu.make_async_copy(src(w, me), outs[w].at[me], loc_sems.at[w])

        def remote(w, m):
            peer = _peer(x, y, c, m)
            return pltpu.make_async_remote_copy(src_ref=src(w, _dev_index(peer)), dst_ref=outs[w].at[me],
                                                send_sem=send_sems.at[w, m - 1], recv_sem=recv_sems.at[w, m - 1],
                                                device_id=peer, device_id_type=pl.DeviceIdType.MESH)

        def arrival(w, m):
            peer = _peer(x, y, c, m)
            return pltpu.make_async_remote_copy(src_ref=src(w, me), dst_ref=outs[w].at[_dev_index(peer)],
                                                send_sem=send_sems.at[w, m - 1], recv_sem=recv_sems.at[w, m - 1],
                                                device_id=peer, device_id_type=pl.DeviceIdType.MESH)

        for w in range(n):
            local(w).start()
        for w in range(n):
            for m in range(1, N_DEV):
                remote(w, m).start()
        for w in range(n):
            for m in range(1, N_DEV):
                arrival(w, m).wait_recv()
        for w in range(n):
            for m in range(1, N_DEV):
                remote(w, m).wait_send()
        for w in range(n):
            local(w).wait()

    anyspec = pl.BlockSpec(memory_space=pl.ANY)
    out_shape = tuple(SDS(a.shape if scatter else (N_DEV,) + a.shape, a.dtype) for a in arrs)
    return pl.pallas_call(
        body, out_shape=out_shape, in_specs=[anyspec] * n, out_specs=tuple([anyspec] * n),
        scratch_shapes=[pltpu.SemaphoreType.DMA((n, N_DEV - 1)), pltpu.SemaphoreType.DMA((n, N_DEV - 1)),
                        pltpu.SemaphoreType.DMA((n,))],
        name=name, compiler_params=pltpu.CompilerParams(has_side_effects=True),
    )(*arrs)


def _adam_math(g, w, m, v):
    m = ADAM_B1 * m + (1.0 - ADAM_B1) * g
    v = ADAM_B2 * v + (1.0 - ADAM_B2) * (g * g)
    m_hat = m / (1.0 - ADAM_B1 ** ADAM_STEP)
    v_hat = v / (1.0 - ADAM_B2 ** ADAM_STEP)
    delta = -ADAM_LR * (m_hat / (jnp.sqrt(v_hat) + ADAM_EPS) + ADAM_WD * w)
    return delta, m, v


def _adam(parts, w, m, v, name, tr=128):
    r, c = w.shape
    tr = next(t for t in (tr, 64, 32, 16, 8) if r % t == 0)

    def fn(cc, rr, pb, wb, mb, vb):
        g = pb[0].astype(F32)
        for d in range(1, N_DEV):
            g = g + pb[d].astype(F32)
        delta, nm, nv = _adam_math(g, wb, mb, vb)
        return g, delta, nm, nv

    blk = ((tr, c), lambda cc, rr: (rr, 0))
    o = SDS((r, c), F32)
    return _ew(fn, [(parts, (N_DEV, tr, c), lambda cc, rr: (0, rr, 0)), (w, *blk), (m, *blk), (v, *blk)],
               [(o, *blk, None)] * 4, (1, r // tr), name)


_SHARDED = ("w_in", "w_glu", "w_branch_attn", "w_branch_ssm", "w_out", "w_up", "w_down")
_COL_SHARDED = ("w_in", "w_glu", "w_branch_attn", "w_branch_ssm", "w_up")
_SMALL = ("attn_norm_g", "b_in", "attn_sinks", "ssm_a_re", "ssm_a_im", "ssm_log_dt", "ssm_b_re", "ssm_b_im",
          "ssm_c_re", "ssm_c_im", "ssm_d", "b_glu", "ffn_norm_g", "conv_w", "conv_b", "final_norm_g")
_WEIGHTS = ("attn_norm_g", "w_in", "b_in", "attn_sinks", "ssm_a_re", "ssm_a_im", "ssm_log_dt", "ssm_b_re", "ssm_b_im",
            "ssm_c_re", "ssm_c_im", "ssm_d", "w_glu", "b_glu", "w_branch_attn", "w_branch_ssm", "w_out", "ffn_norm_g",
            "w_up", "conv_w", "conv_b", "w_down", "final_norm_g")


def _unstack_cols(g):
    return g.transpose(1, 0, 2).reshape(g.shape[1], N_DEV * g.shape[2])


def _stack_cols(a):
    k, n = a.shape
    return a.reshape(k, N_DEV, n // N_DEV).transpose(1, 0, 2)


def _pack(arrs):
    flat = jnp.concatenate([a.reshape(-1) for a in arrs])
    pad = (-flat.shape[0]) % 1024
    return jnp.pad(flat, (0, pad)).reshape(-1, 128)


def _local_step(x, tgt, wts, small):
    L = x.shape[0]
    nr = lambda tm: L // tm

    h = _rmsnorm_fwd(x, small["attn_norm_g"], "norm1")
    proj, projb = _mm(h, wts["w_in"], bias=small["b_in_p"], out_dtype=F32, out2_dtype=BF16, name="proj")
    attn = _attn_fwd(projb, small["attn_sinks"], "attn_fwd")

    ab, bmat, cmat = _ssm_prep(small["a_re"], small["a_im"], small["logdt"], small["b_re"], small["b_im"],
                               small["c_re"], small["c_im"], "ssm_prep")
    u_seg = _to_segments(proj[:, C_U:C_GA])
    ends_f = _ssm_scan(u_seg, bmat, ab, reverse=False, name="ssm_ends_fwd")
    xs, init_f = _ssm_scan(u_seg, bmat, ab, reverse=False, ends=ends_f, name="ssm_scan_fwd")
    y_mm = _from_segments(_mm(xs, cmat, tb=True, tm=512, tk=1024, name="ssm_out"))

    def gelu_fn(c, r, yb, ub, db):
        yv = yb + db * ub
        return yv, _gelu(yv)

    tm = 512
    y, gy = _ew(gelu_fn, [(y_mm, *_rc(tm, 256)), (proj, *_rc(tm, 256, C_U // 256)), (small["ssm_d"], *_col(1, 256))],
                [(SDS((L, SW), F32), *_rc(tm, 256), None), (SDS((L, SW), BF16), *_rc(tm, 256), None)],
                (2, nr(tm)), "ssm_gelu")
    glu = _mm(gy, wts["w_glu"], bias=small["b_glu"], name="glu")

    def glu_fn(c, r, vb, gb):
        return (vb * _sigmoid(gb),)

    (ssm,) = _ew(glu_fn, [(glu, *_rc(tm, SW)), (glu, *_rc(tm, SW, 1))], [(SDS((L, SW), BF16), *_rc(tm, SW), None)],
                 (1, nr(tm)), "glu_gate")
    br_a = _mm(attn, wts["w_branch_attn"], name="branch_attn")
    br_s = _mm(ssm, wts["w_branch_ssm"], name="branch_ssm")

    def merge_fn(c, r, ab_, sb_, ga, gs):
        return (_sigmoid(ga) * ab_ + _sigmoid(gs) * sb_,)

    (merged,) = _ew(merge_fn, [(br_a, *_rc(tm, 256)), (br_s, *_rc(tm, 256)), (proj, *_rc(tm, 256, C_GA // 256)),
                               (proj, *_rc(tm, 256, C_GS // 256))],
                    [(SDS((L, D), BF16), *_rc(tm, 256), None)], (D // 256, nr(tm)), "merge")
    x1 = _mm(merged, wts["w_out"], res=x, name="out_proj")
    h2 = _rmsnorm_fwd(x1, small["ffn_norm_g"], "norm2")
    up = _mm(h2, wts["w_up"], name="ffn_up")
    ncf = DFF // 512
    tma = 256

    def conv_gate(c, r, gate, halo, cw, cb):
        halo = halo * (r > 0).astype(F32)
        return cb + cw[2:3] * gate + cw[1:2] * _shift_rows(gate, halo, 1) + cw[0:1] * _shift_rows(gate, halo, 2)

    gate_specs = [(up, *_rc(tma, 512, ncf)),
                  (up, (8, 512), lambda c, r: (jnp.maximum(r * (tma // 8) - 1, 0), c + ncf)),
                  (small["conv_w"], *_col(3, 512)), (small["conv_b"], *_col(1, 512))]

    def act_fn(c, r, val, gate, halo, cw, cb):
        return (val * _gelu(conv_gate(c, r, gate, halo, cw, cb)),)

    (act,) = _ew(act_fn, [(up, *_rc(tma, 512))] + gate_specs, [(SDS((L, DFF), BF16), *_rc(tma, 512), None)],
                 (ncf, nr(tma)), "ffn_act")
    x2 = _mm(act, wts["w_down"], res=x1, name="ffn_down")
    d_x2, loss_cols, d_gf = _final_loss(x2, small["final_norm_g"], tgt, "final_loss")
    loss = jnp.sum(loss_cols)

    d_act = _mm(d_x2, wts["w_down"], tb=True, tn=512, name="d_act")
    dw_down = _mm(act, d_x2, ta=True, out_dtype=BF16, tm=512, name="dw_down")

    def act_bwd(c, r, da, val, gate, halo, cw, cb):
        halo = halo * (r > 0).astype(F32)
        g1, g2 = _shift_rows(gate, halo, 1), _shift_rows(gate, halo, 2)
        cg = cb + cw[2:3] * gate + cw[1:2] * g1 + cw[0:1] * g2
        d_val = da * _gelu(cg)
        d_cg = da * val * _gelu_grad(cg)
        row3 = lax.broadcasted_iota(jnp.int32, (3, 512), 0)
        s0 = jnp.sum(d_cg * g2, axis=0, keepdims=True)
        s1 = jnp.sum(d_cg * g1, axis=0, keepdims=True)
        s2 = jnp.sum(d_cg * gate, axis=0, keepdims=True)
        dcw = jnp.where(row3 == 0, s0, jnp.where(row3 == 1, s1, s2))
        return d_val, d_cg, dcw, jnp.sum(d_cg, axis=0, keepdims=True)

    d_val, d_cg, d_conv_w, d_conv_b = _ew(
        act_bwd, [(d_act, *_rc(tma, 512)), (up, *_rc(tma, 512))] + gate_specs,
        [(SDS((L, DFF), BF16), *_rc(tma, 512), None), (SDS((L, DFF), F32), *_rc(tma, 512), None),
         (SDS((3, DFF), F32), *_col(3, 512), "r"), (SDS((1, DFF), F32), *_col(1, 512), "r")],
        (ncf, nr(tma)), "ffn_act_bwd")

    def gate_bwd(c, r, dcg, halo, cw):
        halo = halo * (r < nr(tma) - 1).astype(F32)
        return (cw[2:3] * dcg + cw[1:2] * _shift_rows_up(dcg, halo, 1) + cw[0:1] * _shift_rows_up(dcg, halo, 2),)

    (d_gate,) = _ew(gate_bwd, [(d_cg, *_rc(tma, 512)),
                               (d_cg, (8, 512), lambda c, r: (jnp.minimum((r + 1) * (tma // 8), L // 8 - 1), c)),
                               (small["conv_w"], *_col(3, 512))],
                    [(SDS((L, DFF), BF16), *_rc(tma, 512), None)], (ncf, nr(tma)), "ffn_gate_bwd")
    d_h2 = _mm(d_val, wts["w_up"][:, :DFF], tb=True, name="d_h2_val")
    d_h2 = _mm(d_gate, wts["w_up"][:, DFF:], tb=True, res=d_h2, name="d_h2_gate")
    dw_up_v = _mm(h2, d_val, ta=True, out_dtype=BF16, name="dw_up_val")
    dw_up_g = _mm(h2, d_gate, ta=True, out_dtype=BF16, name="dw_up_gate")
    d_x1, d_g2 = _rmsnorm_bwd(d_h2, x1, small["ffn_norm_g"], d_x2, "norm2_bwd")

    d_merged = _mm(d_x1, wts["w_out"], tb=True, name="d_merged")
    dw_out = _mm(merged, d_x1, ta=True, out_dtype=BF16, name="dw_out")

    def merge_bwd(c, r, dm, ab_, sb_, ga, gs):
        sa, ss = _sigmoid(ga), _sigmoid(gs)
        return dm * sa, dm * ss, dm * ab_ * (sa * (1.0 - sa)), dm * sb_ * (ss * (1.0 - ss))

    ob = SDS((L, D), BF16)
    d_bra, d_brs, d_ga, d_gs = _ew(
        merge_bwd, [(d_merged, *_rc(tm, 256)), (br_a, *_rc(tm, 256)), (br_s, *_rc(tm, 256)),
                    (proj, *_rc(tm, 256, C_GA // 256)), (proj, *_rc(tm, 256, C_GS // 256))],
        [(ob, *_rc(tm, 256), None)] * 4, (D // 256, nr(tm)), "merge_bwd")
    d_attn = _mm(d_bra, wts["w_branch_attn"], tb=True, out_dtype=BF16, name="d_attn")
    dw_ba = _mm(attn, d_bra, ta=True, out_dtype=BF16, name="dw_branch_attn")
    d_ssm = _mm(d_brs, wts["w_branch_ssm"], tb=True, name="d_ssm")
    dw_bs = _mm(ssm, d_brs, ta=True, out_dtype=BF16, name="dw_branch_ssm")
    dq, dkv_cur, dkv_prev, d_sinks = _attn_bwd(projb, small["attn_sinks"], d_attn, "attn_bwd")

    def glu_bwd(c, r, ds, vb, gb):
        sg = _sigmoid(gb)
        return ds * sg, ds * vb * (sg * (1.0 - sg))

    d_glu_v, d_glu_g = _ew(glu_bwd, [(d_ssm, *_rc(tm, SW)), (glu, *_rc(tm, SW)), (glu, *_rc(tm, SW, 1))],
                           [(SDS((L, SW), F32), *_rc(tm, SW), None)] * 2, (1, nr(tm)), "glu_gate_bwd")
    d_glu = jnp.concatenate([d_glu_v, d_glu_g], axis=1)
    d_gy = _mm(d_glu, wts["w_glu"], tb=True, name="d_gelu_y")
    dw_glu = _mm(gy, d_glu, ta=True, out_dtype=BF16, name="dw_glu")

    def gelu_bwd(c, r, dg, yb, ub, dgl):
        dy = dg * _gelu_grad(yb)
        return dy, jnp.sum(dy * ub, axis=0, keepdims=True), jnp.sum(dgl, axis=0, keepdims=True)

    dy, d_ssm_d, d_b_glu = _ew(
        gelu_bwd, [(d_gy, *_rc(tm, 256)), (y, *_rc(tm, 256)), (proj, *_rc(tm, 256, C_U // 256)), (d_glu, *_rc(tm, 512))],
        [(SDS((L, SW), F32), *_rc(tm, 256), None), (SDS((1, SW), F32), *_col(1, 256), "r"),
         (SDS((1, 2 * SW), F32), *_col(1, 512), "r")], (2, nr(tm)), "ssm_gelu_bwd")

    dy_seg = _to_segments(dy)
    ends_r = _ssm_scan(dy_seg, cmat, ab, reverse=True, name="ssm_ends_bwd")
    lam, dab8 = _ssm_scan(dy_seg, cmat, ab, reverse=True, ends=ends_r, xs=xs, init=init_f, name="ssm_scan_bwd")
    du_mm = _from_segments(_mm(lam, bmat, tb=True, tm=512, tk=1024, name="ssm_du"))
    dbm = _mm(u_seg, lam, ta=True, tm=512, name="ssm_dbmat")
    dcm = _mm(dy_seg, xs, ta=True, tm=512, name="ssm_dcmat")
    d_are, d_aim, d_ldt, d_bre, d_bim, d_cre, d_cim = _ssm_param_bwd(
        small["a_re"], small["a_im"], small["logdt"], small["b_re"], small["b_im"], dab8, dbm, dcm, "ssm_param_bwd")

    nb = L // BLK

    def dproj_fn(c, r, dqb, cur, prv, du, dyb, dsk, dga, dgs):
        dkv = cur + prv * (r < nb - 1).astype(F32)
        dub = du + dsk * dyb
        full = jnp.concatenate([dqb.astype(F32), dkv, dub, dga.astype(F32), dgs.astype(F32),
                                jnp.zeros((BLK, INP - INC), F32)], axis=1)
        return full, jnp.sum(full, axis=0, keepdims=True)

    rowb = lambda w: ((BLK, w), lambda c, r: (r, 0))
    dproj, d_b_in = _ew(
        dproj_fn, [(dq, *rowb(AW)), (dkv_cur, *rowb(256)),
                   (dkv_prev, (BLK, 256), lambda c, r: (jnp.minimum(r + 1, nb - 1), 0)),
                   (du_mm, *rowb(SW)), (dy, *rowb(SW)), (small["ssm_d"], *_col(1, SW)), (d_ga, *rowb(D)), (d_gs, *rowb(D))],
        [(SDS((L, INP), BF16), *rowb(INP), None), (SDS((1, INP), F32), *_col(1, INP), "all")], (1, nb), "dproj")
    d_h = _mm(dproj, wts["w_in"], tb=True, name="d_h")
    dw_in = _mm(h, dproj, ta=True, out_dtype=BF16, name="dw_in")
    grad_x, d_g1 = _rmsnorm_bwd(d_h, x, small["attn_norm_g"], d_x1, "norm1_bwd")

    wgrads = {"w_in": dw_in[:, :INC], "w_glu": dw_glu, "w_branch_attn": dw_ba, "w_branch_ssm": dw_bs, "w_out": dw_out,
              "w_up_val": dw_up_v, "w_up_gate": dw_up_g, "w_down": dw_down}
    sgrads = {"attn_norm_g": d_g1, "b_in": d_b_in[:, :INC], "attn_sinks": d_sinks[:, :NQ], "a_re": d_are, "a_im": d_aim,
              "logdt": d_ldt, "b_re": d_bre, "b_im": d_bim, "c_re": d_cre, "c_im": d_cim, "ssm_d": d_ssm_d,
              "b_glu": d_b_glu, "ffn_norm_g": d_g2, "conv_w": d_conv_w, "conv_b": d_conv_b, "final_norm_g": d_gf}
    return loss, grad_x, wgrads, sgrads


def _small_layouts(p):
    gp = lambda a: a.reshape(1, NS)
    hgp = lambda a: a.transpose(2, 0, 1).reshape(H, NS)
    chgp = lambda a: a.transpose(1, 0, 2).reshape(H, NS)
    return {
        "attn_norm_g": p["attn_norm_g"].reshape(1, D), "ffn_norm_g": p["ffn_norm_g"].reshape(1, D),
        "final_norm_g": p["final_norm_g"].reshape(1, D),
        "b_in_p": jnp.pad(p["b_in"].reshape(1, INC), ((0, 0), (0, INP - INC))),
        "attn_sinks": p["attn_sinks"].reshape(1, NQ),
        "a_re": gp(p["ssm_a_re"]), "a_im": gp(p["ssm_a_im"]), "logdt": jnp.repeat(p["ssm_log_dt"], P).reshape(1, NS),
        "b_re": hgp(p["ssm_b_re"]), "b_im": hgp(p["ssm_b_im"]), "c_re": chgp(p["ssm_c_re"]), "c_im": chgp(p["ssm_c_im"]),
        "ssm_d": p["ssm_d"].reshape(1, SW), "b_glu": p["b_glu"].reshape(1, 2 * SW),
        "conv_w": p["conv_w"], "conv_b": p["conv_b"].reshape(1, DFF),
    }


def _small_grads_to_param_shapes(sg):
    from_hgp = lambda a: a.reshape(H, G, P).transpose(1, 2, 0)
    from_chgp = lambda a: a.reshape(H, G, P).transpose(1, 0, 2)
    return {
        "attn_norm_g": sg["attn_norm_g"].reshape(D), "b_in": sg["b_in"].reshape(INC),
        "attn_sinks": sg["attn_sinks"].reshape(NQ),
        "ssm_a_re": sg["a_re"].reshape(G, P), "ssm_a_im": sg["a_im"].reshape(G, P),
        "ssm_log_dt": jnp.sum(sg["logdt"].reshape(G, P), axis=1),
        "ssm_b_re": from_hgp(sg["b_re"]), "ssm_b_im": from_hgp(sg["b_im"]),
        "ssm_c_re": from_chgp(sg["c_re"]), "ssm_c_im": from_chgp(sg["c_im"]),
        "ssm_d": sg["ssm_d"].reshape(SW), "b_glu": sg["b_glu"].reshape(2 * SW),
        "ffn_norm_g": sg["ffn_norm_g"].reshape(D), "conv_w": sg["conv_w"], "conv_b": sg["conv_b"].reshape(DFF),
        "final_norm_g": sg["final_norm_g"].reshape(D),
    }


def kernel(x, attn_norm_g, w_in, b_in, attn_sinks, ssm_a_re, ssm_a_im, ssm_log_dt, ssm_b_re, ssm_b_im, ssm_c_re, ssm_c_im, ssm_d, w_glu, b_glu, w_branch_attn, w_branch_ssm, w_out, ffn_norm_g, w_up, conv_w, conv_b, w_down, final_norm_g, loss_target, m_attn_norm_g, m_w_in, m_b_in, m_attn_sinks, m_ssm_a_re, m_ssm_a_im, m_ssm_log_dt, m_ssm_b_re, m_ssm_b_im, m_ssm_c_re, m_ssm_c_im, m_ssm_d, m_w_glu, m_b_glu, m_w_branch_attn, m_w_branch_ssm, m_w_out, m_ffn_norm_g, m_w_up, m_conv_w, m_conv_b, m_w_down, m_final_norm_g, v_attn_norm_g, v_w_in, v_b_in, v_attn_sinks, v_ssm_a_re, v_ssm_a_im, v_ssm_log_dt, v_ssm_b_re, v_ssm_b_im, v_ssm_c_re, v_ssm_c_im, v_ssm_d, v_w_glu, v_b_glu, v_w_branch_attn, v_w_branch_ssm, v_w_out, v_ffn_norm_g, v_w_up, v_conv_w, v_conv_b, v_w_down, v_final_norm_g):
    args = dict(locals())
    sq = lambda a: a if a.ndim == 1 else a[0]
    wv = {n: sq(args[n]) for n in _WEIGHTS}
    mv = {n: sq(args["m_" + n]) for n in _WEIGHTS}
    vv = {n: sq(args["v_" + n]) for n in _WEIGHTS}
    me = 4 * lax.axis_index("x") + 2 * lax.axis_index("y") + lax.axis_index("c")

    cw_shard = jnp.pad(wv["conv_w"], ((0, 5), (0, 64)))
    *gathered, cw_g = _exchange([wv[n].astype(BF16) for n in _SHARDED] + [cw_shard], False, "gather_weights")
    full = {}
    for n, g in zip(_SHARDED, gathered):
        full[n] = _unstack_cols(g) if n in _COL_SHARDED else g.reshape(N_DEV * g.shape[1], g.shape[2])
    full["w_in"] = jnp.pad(full["w_in"], ((0, 0), (0, INP - INC)))
    small_p = dict(wv)
    small_p["conv_w"] = _unstack_cols(cw_g[:, :3, :DFF // N_DEV])
    small = _small_layouts(small_p)

    loss, grad_x, wg, sg = _local_step(x[0], loss_target[0], full, small)
    loss = lax.psum(loss, MESH_AXES)

    stacked = {
        "w_in": _stack_cols(wg["w_in"]), "w_glu": _stack_cols(wg["w_glu"]),
        "w_branch_attn": _stack_cols(wg["w_branch_attn"]), "w_branch_ssm": _stack_cols(wg["w_branch_ssm"]),
        "w_out": wg["w_out"].reshape(N_DEV, D // N_DEV, D),
        "w_up": _stack_cols(jnp.concatenate([wg["w_up_val"], wg["w_up_gate"]], axis=1)),
        "w_down": wg["w_down"].reshape(N_DEV, DFF // N_DEV, D),
    }
    sgp = _small_grads_to_param_shapes(sg)
    small_names = [n for n in _SMALL]
    packed_g = _pack([sgp[n] for n in small_names])
    parts = _exchange([stacked[n] for n in _SHARDED], True, "scatter_grads")
    (small_all,) = _exchange([packed_g], False, "gather_small_grads")

    outs_g, outs_d, outs_m, outs_v = {}, {}, {}, {}
    for n, pt in zip(_SHARDED, parts):
        outs_g[n], outs_d[n], outs_m[n], outs_v[n] = _adam(pt, wv[n], mv[n], vv[n], "adam_" + n)

    sizes = [int(math.prod(sgp[n].shape)) for n in small_names]
    offs = [0]
    for s in sizes:
        offs.append(offs[-1] + s)

    def local_part(n, a):
        if n == "conv_w":
            return lax.dynamic_slice(a, (0, me * (DFF // N_DEV)), (3, DFF // N_DEV))
        return a

    rows = packed_g.shape[0]

    def sum_fn(cc, rr, pb):
        g = pb[0]
        for d in range(1, N_DEV):
            g = g + pb[d]
        return (g,)

    (gsum,) = _ew(sum_fn, [(small_all, (N_DEV, rows, 128), lambda cc, rr: (0, 0, 0))],
                  [(SDS((rows, 128), F32), (rows, 128), lambda cc, rr: (0, 0), None)], (1, 1), "sum_small_grads")
    gflat = gsum.reshape(-1)
    gsmall = {n: local_part(n, gflat[offs[i]:offs[i + 1]].reshape(sgp[n].shape)) for i, n in enumerate(small_names)}
    pw = _pack([wv[n] for n in small_names])
    pm = _pack([mv[n] for n in small_names])
    pv = _pack([vv[n] for n in small_names])
    pg = _pack([gsmall[n] for n in small_names])
    prow = pw.shape[0]

    def adam_small(cc, rr, gb, wb, mb, vb):
        return _adam_math(gb, wb, mb, vb)

    whole = ((prow, 128), lambda cc, rr: (0, 0))
    sd, sm, sv = _ew(adam_small, [(pg, *whole), (pw, *whole), (pm, *whole), (pv, *whole)],
                     [(SDS((prow, 128), F32), *whole, None)] * 3, (1, 1), "adam_small")
    lsizes = [int(math.prod(wv[n].shape)) for n in small_names]
    loffs = [0]
    for s in lsizes:
        loffs.append(loffs[-1] + s)
    for i, n in enumerate(small_names):
        take = lambda a: a.reshape(-1)[loffs[i]:loffs[i + 1]].reshape(wv[n].shape)
        outs_g[n], outs_d[n], outs_m[n], outs_v[n] = gsmall[n], take(sd), take(sm), take(sv)

    lead = lambda n, a: a if args[n].ndim == 1 else a[None]
    grad_x = grad_x[None]
    return (loss, grad_x, *[lead(n, outs_g[n]) for n in _WEIGHTS], *[lead(n, outs_d[n]) for n in _WEIGHTS],
            *[lead(n, outs_m[n]) for n in _WEIGHTS], *[lead(n, outs_v[n]) for n in _WEIGHTS])
```

```python
import functools
import math

import jax
import jax.numpy as jnp
from jax import lax
from jax.experimental import pallas as pl
from jax.experimental.pallas import tpu as pltpu

F32 = jnp.float32
BF16 = jnp.bfloat16
SDS = jax.ShapeDtypeStruct

N_DEV = 8
D = 2048
NQ, NKV, HD = 16, 2, 64
AW = NQ * HD
BLK = 128
SW, G, H, P = 512, 32, 16, 64
NS = G * P
DFF = 5632
INC = AW + 2 * NKV * HD + SW + 2 * D
INP = 6144
C_K, C_U, C_GA, C_GS = AW, AW + 2 * NKV * HD, AW + 2 * NKV * HD + SW, AW + 2 * NKV * HD + SW + D
RMS_EPS = 1e-6
NEG_BIG = -1e30
ADAM_LR, ADAM_B1, ADAM_B2, ADAM_EPS, ADAM_WD, ADAM_STEP = 0.001, 0.9, 0.999, 1e-08, 0.01, 10
NSEG = 8
VMEM_CAP_MB = 60
MESH_AXES = ("x", "y", "c")


def _cparams(sem, vmem_mb):
    return pltpu.CompilerParams(dimension_semantics=sem, vmem_limit_bytes=min(int(vmem_mb), VMEM_CAP_MB) << 20)


LANES = 128


def _tile(dim, pref):
    if dim <= pref:
        return dim
    for t in range(pref - pref % LANES, 0, -LANES):
        if dim % t == 0:
            return t
    raise ValueError(f"no tile for {dim}")


def _mm(a, b, *, ta=False, tb=False, bias=None, res=None, out_dtype=F32, tm=1024, tn=1024, tk=3072, name):
    m, k = (a.shape[1], a.shape[0]) if ta else a.shape
    n = b.shape[0] if tb else b.shape[1]
    assert (b.shape[1] if tb else b.shape[0]) == k, (a.shape, b.shape, ta, tb)
    tm, tn, tk = _tile(m, tm), _tile(n, tn), _tile(k, tk)
    nk = k // tk
    dims = (((0 if ta else 1,), (1 if tb else 0,)), ((), ()))
    has_bias, has_res = bias is not None, res is not None

    def body(*refs):
        a_ref, b_ref = refs[0], refs[1]
        pos = 2
        bias_ref = refs[pos] if has_bias else None
        pos += has_bias
        res_ref = refs[pos] if has_res else None
        pos += has_res
        o_ref = refs[pos]

        def product():
            return lax.dot_general(a_ref[...].astype(BF16), b_ref[...].astype(BF16), dims, preferred_element_type=F32)

        def finish(r):
            if has_bias:
                r = r + bias_ref[...]
            if has_res:
                r = r + res_ref[...].astype(F32)
            o_ref[...] = r.astype(o_ref.dtype)

        if nk == 1:
            finish(product())
            return
        acc_ref = refs[-1]
        kk = pl.program_id(2)

        @pl.when(kk == 0)
        def _():
            acc_ref[...] = product()

        @pl.when(jnp.logical_and(kk > 0, kk < nk - 1))
        def _():
            acc_ref[...] += product()

        @pl.when(kk == nk - 1)
        def _():
            finish(acc_ref[...] + product())

    ins = [a, b]
    in_specs = [pl.BlockSpec((tk, tm), lambda i, j, kk: (kk, i)) if ta else pl.BlockSpec((tm, tk), lambda i, j, kk: (i, kk)),
                pl.BlockSpec((tn, tk), lambda i, j, kk: (j, kk)) if tb else pl.BlockSpec((tk, tn), lambda i, j, kk: (kk, j))]
    byt = 2 * tm * tk * a.dtype.itemsize + 2 * tk * tn * b.dtype.itemsize + 2 * tm * tn * jnp.dtype(out_dtype).itemsize
    byt += 2 * 4 * tm * tn
    if has_bias:
        ins.append(bias)
        in_specs.append(pl.BlockSpec((1, tn), lambda i, j, kk: (0, j)))
    if has_res:
        ins.append(res)
        in_specs.append(pl.BlockSpec((tm, tn), lambda i, j, kk: (i, j)))
        byt += 2 * tm * tn * res.dtype.itemsize
    return pl.pallas_call(
        body, out_shape=SDS((m, n), out_dtype), grid=(m // tm, n // tn, nk), in_specs=in_specs,
        out_specs=pl.BlockSpec((tm, tn), lambda i, j, kk: (i, j)),
        scratch_shapes=[pltpu.VMEM((tm, tn), F32)] if nk > 1 else [], name=name,
        compiler_params=_cparams(("parallel", "parallel", "arbitrary"), byt / 2**20 + 8),
    )(*ins)


def _ew(fn, ins, outs, grid, name, vmem_mb=40):
    n_in = len(ins)
    accs = [o[3] for o in outs]

    def body(*refs):
        c, r = pl.program_id(0), pl.program_id(1)
        vals = fn(c, r, *[ref[...].astype(F32) for ref in refs[:n_in]])
        for o_ref, v, acc in zip(refs[n_in:], vals, accs):
            if acc is None:
                o_ref[...] = v.astype(o_ref.dtype)
            else:
                first = (r == 0) if acc == "r" else jnp.logical_and(r == 0, c == 0)

                @pl.when(first)
                def _(o_ref=o_ref, v=v):
                    o_ref[...] = v.astype(o_ref.dtype)

                @pl.when(jnp.logical_not(first))
                def _(o_ref=o_ref, v=v):
                    o_ref[...] += v.astype(o_ref.dtype)

    res = pl.pallas_call(
        body, out_shape=tuple(o[0] for o in outs), grid=grid,
        in_specs=[pl.BlockSpec(bs, im) for _, bs, im in ins],
        out_specs=tuple(pl.BlockSpec(bs, im) for _, bs, im, _ in outs), name=name,
        compiler_params=_cparams(("arbitrary", "arbitrary"), vmem_mb),
    )(*[a for a, _, _ in ins])
    return res


def _rc(tm, tc, coff=0):
    return (tm, tc), (lambda c, r: (r, c + coff))


def _col(rows, tc, coff=0):
    return (rows, tc), (lambda c, r: (0, c + coff))


def _gelu(x):
    return 0.5 * x * (1.0 + lax.erf(x * (2.0 ** -0.5)))


def _gelu_and_grad(x):
    cdf = 0.5 * (1.0 + lax.erf(x * (2.0 ** -0.5)))
    return x * cdf, cdf + x * jnp.exp(-0.5 * x * x) * (1.0 / math.sqrt(2.0 * math.pi))


def _gelu_grad(x):
    return _gelu_and_grad(x)[1]


def _sigmoid(x):
    return 1.0 / (1.0 + jnp.exp(-x))


def _shift_rows(x, halo, s):
    rolled = pltpu.roll(x, s, 0)
    row8 = lax.broadcasted_iota(jnp.int32, halo.shape, 0)
    head = jnp.where(row8 < s, pltpu.roll(halo, s, 0), rolled[0:8])
    return jnp.concatenate([head, rolled[8:]], axis=0)


def _shift_rows_up(x, halo, s):
    tm = x.shape[0]
    rolled = pltpu.roll(x, tm - s, 0)
    row8 = lax.broadcasted_iota(jnp.int32, halo.shape, 0)
    tail = jnp.where(row8 >= 8 - s, pltpu.roll(halo, 8 - s, 0), rolled[tm - 8:])
    return jnp.concatenate([rolled[:tm - 8], tail], axis=0)


def _rmsnorm_fwd(x, g, name, tm=256):
    L = x.shape[0]

    def fn(c, r, xb, gb):
        rstd = lax.rsqrt(jnp.mean(xb * xb, axis=-1, keepdims=True) + RMS_EPS)
        return ((xb * rstd) * gb,)

    return _ew(fn, [(x, *_rc(tm, D)), (g, *_col(1, D))], [(SDS((L, D), BF16), *_rc(tm, D), None)], (1, L // tm), name)[0]


def _rmsnorm_bwd(dh, x, g, dres, name, tm=256):
    L = x.shape[0]

    def fn(c, r, dhb, xb, gb, drb):
        rstd = lax.rsqrt(jnp.mean(xb * xb, axis=-1, keepdims=True) + RMS_EPS)
        y = xb * rstd
        dy = dhb * gb
        dx = rstd * (dy - y * jnp.mean(dy * y, axis=-1, keepdims=True))
        return drb + dx, jnp.sum(dhb * y, axis=0, keepdims=True)

    return _ew(fn, [(dh, *_rc(tm, D)), (x, *_rc(tm, D)), (g, *_col(1, D)), (dres, *_rc(tm, D))],
               [(SDS((L, D), F32), *_rc(tm, D), None), (SDS((1, D), F32), *_col(1, D), "all")], (1, L // tm), name)


def _final_loss(x2, g, tgt, name, tm=256):
    L = x2.shape[0]

    def fn(c, r, xb, gb, tb):
        rstd = lax.rsqrt(jnp.mean(xb * xb, axis=-1, keepdims=True) + RMS_EPS)
        y = xb * rstd
        err = y * gb - tb
        dout = err * (1.0 / D)
        dy = dout * gb
        dx = rstd * (dy - y * jnp.mean(dy * y, axis=-1, keepdims=True))
        return dx, jnp.sum(err * err, axis=0, keepdims=True) * (0.5 / D), jnp.sum(dout * y, axis=0, keepdims=True)

    return _ew(fn, [(x2, *_rc(tm, D)), (g, *_col(1, D)), (tgt, *_rc(tm, D))],
               [(SDS((L, D), F32), *_rc(tm, D), None), (SDS((1, D), F32), *_col(1, D), "all"),
                (SDS((1, D), F32), *_col(1, D), "all")], (1, L // tm), name)


def _attn_setup(n, kvc, kvp):
    kv = jnp.concatenate([kvp, kvc], axis=0).astype(F32)
    lo = lax.broadcasted_iota(jnp.int32, (2 * BLK, 128), 1) < HD

    def halves(t):
        tr = pltpu.roll(t, HD, 1)
        z = jnp.zeros_like(t)
        return {(0, 0): jnp.where(lo, t, z).astype(BF16), (0, 1): jnp.where(lo, z, tr).astype(BF16),
                (1, 0): jnp.where(lo, tr, z).astype(BF16), (1, 1): jnp.where(lo, z, t).astype(BF16)}

    kmat, vmat = halves(kv[:, :128]), halves(kv[:, 128:])
    qi = lax.broadcasted_iota(jnp.int32, (BLK, 2 * BLK), 0)
    si = lax.broadcasted_iota(jnp.int32, (BLK, 2 * BLK), 1)
    dist = qi + BLK - si
    valid = (dist >= 0) & (dist < BLK) & ((n > 0) | (si >= BLK))
    return kmat, vmat, valid, dist.astype(F32)


def _attn_probs(qp, kmat_ge, valid, distf, slope, sink):
    s = lax.dot_general(qp, kmat_ge, (((1,), (1,)), ((), ())), preferred_element_type=F32) * (HD ** -0.5)
    s = jnp.where(valid, s - slope * distf, NEG_BIG)
    m = jnp.maximum(jnp.max(s, axis=-1, keepdims=True), sink)
    p = jnp.exp(s - m)
    esink = jnp.exp(sink - m)
    den = jnp.sum(p, axis=-1, keepdims=True) + esink
    return p / den, esink / den


def _slope(h):
    return 2.0 ** (-8.0 * (h + 1) / NQ)


def _attn_fwd(projb, sinks, name):
    L = projb.shape[0]
    nb = L // BLK

    def body(s_ref, q_ref, kvc_ref, kvp_ref, o_ref):
        n = pl.program_id(0)
        kmat, vmat, valid, distf = _attn_setup(n, kvc_ref[...], kvp_ref[...])
        for j in range(NQ // 2):
            g = j // (NQ // 4)
            qp = q_ref[:, 128 * j:128 * (j + 1)]
            acc = jnp.zeros((BLK, 128), F32)
            for e in range(2):
                h = 2 * j + e
                p, _ = _attn_probs(qp, kmat[(g, e)], valid, distf, _slope(h), s_ref[0, h])
                acc = acc + jnp.dot(p.astype(BF16), vmat[(g, e)], preferred_element_type=F32)
            o_ref[:, 128 * j:128 * (j + 1)] = acc.astype(BF16)

    return pl.pallas_call(
        body, out_shape=SDS((L, AW), BF16), grid=(nb,),
        in_specs=[pl.BlockSpec(memory_space=pltpu.SMEM),
                  pl.BlockSpec((BLK, AW), lambda n: (n, 0)),
                  pl.BlockSpec((BLK, 256), lambda n: (n, C_K // 256)),
                  pl.BlockSpec((BLK, 256), lambda n: (jnp.maximum(n - 1, 0), C_K // 256))],
        out_specs=pl.BlockSpec((BLK, AW), lambda n: (n, 0)), name=name,
        compiler_params=_cparams(("arbitrary",), 32),
    )(sinks, projb, projb, projb)


def _attn_bwd(projb, sinks, dattn, name):
    L = projb.shape[0]
    nb = L // BLK

    def body(s_ref, q_ref, kvc_ref, kvp_ref, do_ref, dq_ref, dcur_ref, dprev_ref, dsink_ref):
        n = pl.program_id(0)
        kmat, vmat, valid, distf = _attn_setup(n, kvc_ref[...], kvp_ref[...])
        lo128 = lax.broadcasted_iota(jnp.int32, (BLK, 128), 1) < HD
        lane = lax.broadcasted_iota(jnp.int32, (1, 128), 1)
        dk = [jnp.zeros((2 * BLK, 128), F32) for _ in range(NKV)]
        dv = [jnp.zeros((2 * BLK, 128), F32) for _ in range(NKV)]
        dsv = jnp.zeros((1, 128), F32)
        tn_dims = (((0,), (0,)), ((), ()))
        for j in range(NQ // 2):
            g = j // (NQ // 4)
            qp = q_ref[:, 128 * j:128 * (j + 1)]
            dop = do_ref[:, 128 * j:128 * (j + 1)]
            dqp = jnp.zeros((BLK, 128), F32)
            for e in range(2):
                h = 2 * j + e
                p, psink = _attn_probs(qp, kmat[(g, e)], valid, distf, _slope(h), s_ref[0, h])
                dp = lax.dot_general(dop, vmat[(g, e)], (((1,), (1,)), ((), ())), preferred_element_type=F32)
                drow = jnp.sum(p * dp, axis=-1, keepdims=True)
                ds = p * (dp - drow)
                dsv = dsv + jnp.where(lane == h, -jnp.sum(psink * drow, axis=0, keepdims=True), 0.0)
                dsb = (ds * (HD ** -0.5)).astype(BF16)
                dqp = dqp + jnp.dot(dsb, kmat[(g, e)], preferred_element_type=F32)
                half = lo128 if e == 0 else jnp.logical_not(lo128)
                zb = jnp.zeros_like(qp)
                dk[g] = dk[g] + lax.dot_general(dsb, jnp.where(half, qp, zb), tn_dims, preferred_element_type=F32)
                dv[g] = dv[g] + lax.dot_general(p.astype(BF16), jnp.where(half, dop, zb), tn_dims,
                                                preferred_element_type=F32)
            dq_ref[:, 128 * j:128 * (j + 1)] = dqp.astype(BF16)
        lo256 = lax.broadcasted_iota(jnp.int32, (2 * BLK, 128), 1) < HD
        tot = [t + pltpu.roll(t, HD, 1) for t in (dk[0], dk[1], dv[0], dv[1])]
        dkv = jnp.concatenate([jnp.where(lo256, tot[0], tot[1]), jnp.where(lo256, tot[2], tot[3])], axis=1)
        dprev_ref[...] = dkv[:BLK]
        dcur_ref[...] = dkv[BLK:]

        @pl.when(n == 0)
        def _():
            dsink_ref[...] = dsv

        @pl.when(n > 0)
        def _():
            dsink_ref[...] += dsv

    return pl.pallas_call(
        body, out_shape=(SDS((L, AW), BF16), SDS((L, 256), F32), SDS((L, 256), F32), SDS((1, 128), F32)), grid=(nb,),
        in_specs=[pl.BlockSpec(memory_space=pltpu.SMEM),
                  pl.BlockSpec((BLK, AW), lambda n: (n, 0)),
                  pl.BlockSpec((BLK, 256), lambda n: (n, C_K // 256)),
                  pl.BlockSpec((BLK, 256), lambda n: (jnp.maximum(n - 1, 0), C_K // 256)),
                  pl.BlockSpec((BLK, AW), lambda n: (n, 0))],
        out_specs=(pl.BlockSpec((BLK, AW), lambda n: (n, 0)), pl.BlockSpec((BLK, 256), lambda n: (n, 0)),
                   pl.BlockSpec((BLK, 256), lambda n: (n, 0)), pl.BlockSpec((1, 128), lambda n: (0, 0))),
        name=name, compiler_params=_cparams(("arbitrary",), 32),
    )(sinks, projb, projb, projb, dattn)


def _disc(a_re, a_im, logdt, b_re, b_im):
    dt = jnp.exp(logdt)
    mag = jnp.exp(a_re * dt)
    ab_re = mag * jnp.cos(a_im * dt)
    ab_im = mag * jnp.sin(a_im * dt)
    nr = ab_re - 1.0
    ni = ab_im
    den = a_re * a_re + a_im * a_im
    z_re = (nr * a_re + ni * a_im) / den
    z_im = (ni * a_re - nr * a_im) / den
    return ab_re, ab_im, z_re * b_re - z_im * b_im, z_re * b_im + z_im * b_re


def _group_mask():
    row = lax.broadcasted_iota(jnp.int32, (SW, NS), 0) // H
    col = lax.broadcasted_iota(jnp.int32, (SW, NS), 1) // P
    return row == col


def _block_diag(re, im):
    mask = _group_mask()
    z = jnp.zeros((SW, NS), F32)
    return jnp.concatenate([jnp.where(mask, jnp.tile(re, (G, 1)), z), jnp.where(mask, jnp.tile(im, (G, 1)), z)], axis=1)


def _block_diag_t(big):
    mask = _group_mask()
    z = jnp.zeros((SW, NS), F32)
    re = jnp.sum(jnp.where(mask, big[:, :NS], z).reshape(G, H, NS), axis=0)
    im = jnp.sum(jnp.where(mask, big[:, NS:], z).reshape(G, H, NS), axis=0)
    return re, im


def _ssm_prep(a_re, a_im, logdt, b_re, b_im, c_re, c_im, name):
    def body(are, aim, ldt, bre, bim, cre, cim, ab_ref, bm_ref, cm_ref):
        ab_re, ab_im, bb_re, bb_im = _disc(are[...], aim[...], ldt[...], bre[...], bim[...])
        ab_ref[...] = jnp.concatenate([ab_re, ab_im], axis=1)
        bm_ref[...] = _block_diag(bb_re, bb_im).astype(BF16)
        cm_ref[...] = _block_diag(cre[...], -cim[...]).astype(BF16)

    return pl.pallas_call(body, out_shape=(SDS((1, 2 * NS), F32), SDS((SW, 2 * NS), BF16), SDS((SW, 2 * NS), BF16)),
                          name=name, compiler_params=pltpu.CompilerParams(vmem_limit_bytes=48 << 20),
                          )(a_re, a_im, logdt, b_re, b_im, c_re, c_im)


def _ssm_param_bwd(a_re, a_im, logdt, b_re, b_im, dab8, dbm, dcm, name):
    def body(are, aim, ldt, bre, bim, dab_ref, dbm_ref, dcm_ref, o_are, o_aim, o_ldt, o_bre, o_bim, o_cre, o_cim):
        dab = jnp.sum(dab_ref[...], axis=0, keepdims=True)
        dbb_re, dbb_im = _block_diag_t(dbm_ref[...])
        _, vjp = jax.vjp(_disc, are[...], aim[...], ldt[...], bre[...], bim[...])
        d_are, d_aim, d_ldt, d_bre, d_bim = vjp((dab[:, :NS], dab[:, NS:], dbb_re, dbb_im))
        o_are[...], o_aim[...], o_ldt[...], o_bre[...], o_bim[...] = d_are, d_aim, d_ldt, d_bre, d_bim
        dc_re, dc_imn = _block_diag_t(dcm_ref[...])
        o_cre[...] = dc_re
        o_cim[...] = -dc_imn

    v1, vh = SDS((1, NS), F32), SDS((H, NS), F32)
    return pl.pallas_call(body, out_shape=(v1, v1, v1, vh, vh, vh, vh), name=name,
                          compiler_params=pltpu.CompilerParams(vmem_limit_bytes=56 << 20),
                          )(a_re, a_im, logdt, b_re, b_im, dab8, dbm, dcm)


def _ssm_scan(src, wmat, ab, *, reverse, ends=None, xs=None, init=None, name, tk=32):
    L = src.shape[0]
    rows = NSEG * tk
    nch = L // rows
    seg_len = L // NSEG
    n_sq = int(math.log2(seg_len))
    assert 2 ** n_sq == seg_len and L % rows == 0
    first_pass = ends is None
    with_dab = (not first_pass) and reverse
    slab = 512
    n_slab = NS // slab

    def body(*refs):
        src_ref, w_ref, ab_ref = refs[:3]
        pos = 3
        if not first_pass:
            ends_ref = refs[pos]
            pos += 1
        if with_dab:
            xs_ref, xsh_ref, init_ref = refs[pos:pos + 3]
            pos += 3
        if first_pass:
            (e_ref,) = refs[pos:pos + 1]
            pos += 1
        else:
            st_out_ref, aux_ref = refs[pos:pos + 2]
            pos += 2
        buf_ref, st_ref = refs[pos:pos + 2]
        i = pl.program_id(0)
        a_re = ab_ref[:, :NS]
        a_im = -ab_ref[:, NS:] if reverse else ab_ref[:, NS:]

        @pl.when(i == 0)
        def _():
            if first_pass:
                st_ref[...] = jnp.zeros_like(st_ref)
            else:
                pr, pi = a_re, a_im
                for _ in range(n_sq):
                    pr, pi = pr * pr - pi * pi, 2.0 * pr * pi
                zr = jnp.zeros((1, NS), F32)
                cr, ci = zr, zr
                order = list(range(NSEG - 1, -1, -1)) if reverse else list(range(NSEG))
                st_ref[order[0]:order[0] + 1, :] = jnp.zeros((1, 2 * NS), F32)
                for jprev, j in zip(order[:-1], order[1:]):
                    er, ei = ends_ref[jprev:jprev + 1, :NS], ends_ref[jprev:jprev + 1, NS:]
                    cr, ci = er + pr * cr - pi * ci, ei + pr * ci + pi * cr
                    st_ref[j:j + 1, :NS] = cr
                    st_ref[j:j + 1, NS:] = ci
                if not reverse:
                    aux_ref[...] = st_ref[...]
                else:
                    aux_ref[...] = jnp.zeros_like(aux_ref)

        buf_ref[...] = jnp.dot(src_ref[...].astype(BF16), w_ref[...], preferred_element_type=F32)

        for s in range(n_slab):
            re_sl, im_sl = pl.ds(s * slab, slab), pl.ds(NS + s * slab, slab)
            ar = jnp.broadcast_to(a_re[:, s * slab:(s + 1) * slab], (NSEG, slab))
            ai = jnp.broadcast_to(a_im[:, s * slab:(s + 1) * slab], (NSEG, slab))

            def step(t, carry, re_sl=re_sl, im_sl=im_sl, ar=ar, ai=ai):
                k = (tk - 1 - t) if reverse else t
                r0 = pl.multiple_of(k * NSEG, NSEG)
                xr, xi = carry[0], carry[1]
                nr = ar * xr - ai * xi + buf_ref[pl.ds(r0, NSEG), re_sl]
                ni = ar * xi + ai * xr + buf_ref[pl.ds(r0, NSEG), im_sl]
                if not first_pass:
                    buf_ref[pl.ds(r0, NSEG), re_sl] = nr
                    buf_ref[pl.ds(r0, NSEG), im_sl] = ni
                if not with_dab:
                    return nr, ni
                rp = pl.multiple_of((k - 1) * NSEG, NSEG)
                xpr, xpi = xs_ref[pl.ds(rp, NSEG), re_sl], xs_ref[pl.ds(rp, NSEG), im_sl]
                return nr, ni, carry[2] + nr * xpr + ni * xpi, carry[3] + ni * xpr - nr * xpi

            carry = (st_ref[:, re_sl], st_ref[:, im_sl])
            if with_dab:
                z = jnp.zeros((NSEG, slab), F32)
                carry = lax.fori_loop(0, tk - 1, step, carry + (z, z))
                xr, xi, dr, di = carry
                nr = ar * xr - ai * xi + buf_ref[pl.ds(0, NSEG), re_sl]
                ni = ar * xi + ai * xr + buf_ref[pl.ds(0, NSEG), im_sl]
                buf_ref[pl.ds(0, NSEG), re_sl] = nr
                buf_ref[pl.ds(0, NSEG), im_sl] = ni
                at_start = i == nch - 1
                xpr = jnp.where(at_start, init_ref[:, re_sl], xsh_ref[:, re_sl])
                xpi = jnp.where(at_start, init_ref[:, im_sl], xsh_ref[:, im_sl])
                aux_ref[:, re_sl] += dr + nr * xpr + ni * xpi
                aux_ref[:, im_sl] += di + ni * xpr - nr * xpi
                carry = (nr, ni)
            else:
                carry = lax.fori_loop(0, tk, step, carry)
            st_ref[:, re_sl] = carry[0]
            st_ref[:, im_sl] = carry[1]

        if first_pass:
            @pl.when(i == nch - 1)
            def _():
                e_ref[...] = st_ref[...]
        else:
            st_out_ref[...] = buf_ref[...].astype(st_out_ref.dtype)

    chunk = (lambda i: (nch - 1 - i, 0)) if reverse else (lambda i: (i, 0))
    whole = lambda i: (0, 0)
    ins = [src, wmat, ab]
    in_specs = [pl.BlockSpec((rows, SW), chunk), pl.BlockSpec((SW, 2 * NS), whole), pl.BlockSpec((1, 2 * NS), whole)]
    small = SDS((NSEG, 2 * NS), F32)
    small_spec = pl.BlockSpec((NSEG, 2 * NS), whole)
    if not first_pass:
        ins.append(ends)
        in_specs.append(small_spec)
    if with_dab:
        ins += [xs, xs, init]
        in_specs += [pl.BlockSpec((rows, 2 * NS), chunk),
                     pl.BlockSpec((NSEG, 2 * NS), lambda i: (jnp.maximum((nch - 1 - i) * tk - 1, 0), 0)),
                     small_spec]
    if first_pass:
        out_shape, out_specs = small, small_spec
    else:
        out_shape = (SDS((L, 2 * NS), BF16 if reverse else F32), small)
        out_specs = (pl.BlockSpec((rows, 2 * NS), chunk), small_spec)
    return pl.pallas_call(
        body, out_shape=out_shape, grid=(nch,), in_specs=in_specs, out_specs=out_specs,
        scratch_shapes=[pltpu.VMEM((rows, 2 * NS), F32), pltpu.VMEM((NSEG, 2 * NS), F32)], name=name,
        compiler_params=_cparams(("arbitrary",), 56),
    )(*ins)


def _to_segments(a):
    L, c = a.shape
    return a.reshape(NSEG, L // NSEG, c).transpose(1, 0, 2).reshape(L, c)


def _from_segments(a):
    L, c = a.shape
    return a.reshape(L // NSEG, NSEG, c).transpose(1, 0, 2).reshape(L, c)


def _peer(x, y, c, m):
    return ((1 - x) if (m >> 2) & 1 else x, (1 - y) if (m >> 1) & 1 else y, (1 - c) if m & 1 else c)


def _dev_index(p):
    return 4 * p[0] + 2 * p[1] + p[2]


def _exchange(arrs, scatter, name):
    n = len(arrs)

    def body(*refs):
        ins, outs = refs[:n], refs[n:2 * n]
        send_sems, recv_sems, loc_sems = refs[2 * n:]
        x, y, c = lax.axis_index("x"), lax.axis_index("y"), lax.axis_index("c")
        me = _dev_index((x, y, c))

        def src(w, to):
            return ins[w].at[to] if scatter else ins[w]

        def local(w):
            return pltpu.make_async_copy(src(w, me), outs[w].at[me], loc_sems.at[w])

        def remote(w, m):
            peer = _peer(x, y, c, m)
            return pltpu.make_async_remote_copy(src_ref=src(w, _dev_index(peer)), dst_ref=outs[w].at[me],
                                                send_sem=send_sems.at[w, m - 1], recv_sem=recv_sems.at[w, m - 1],
                                                device_id=peer, device_id_type=pl.DeviceIdType.MESH)

        def arrival(w, m):
            peer = _peer(x, y, c, m)
            return pltpu.make_async_remote_copy(src_ref=src(w, me), dst_ref=outs[w].at[_dev_index(peer)],
                                                send_sem=send_sems.at[w, m - 1], recv_sem=recv_sems.at[w, m - 1],
                                                device_id=peer, device_id_type=pl.DeviceIdType.MESH)

        for w in range(n):
            local(w).start()
        for w in range(n):
            for m in range(1, N_DEV):
                remote(w, m).start()
        for w in range(n):
            for m in range(1, N_DEV):
                arrival(w, m).wait_recv()
        for w in range(n):
            for m in range(1, N_DEV):
                remote(w, m).wait_send()
        for w in range(n):
            local(w).wait()

    anyspec = pl.BlockSpec(memory_space=pl.ANY)
    out_shape = tuple(SDS(a.shape if scatter else (N_DEV,) + a.shape, a.dtype) for a in arrs)
    return pl.pallas_call(
        body, out_shape=out_shape, in_specs=[anyspec] * n, out_specs=tuple([anyspec] * n),
        scratch_shapes=[pltpu.SemaphoreType.DMA((n, N_DEV - 1)), pltpu.SemaphoreType.DMA((n, N_DEV - 1)),
                        pltpu.SemaphoreType.DMA((n,))],
        name=name, compiler_params=pltpu.CompilerParams(has_side_effects=True),
    )(*arrs)


def _adam_math(g, w, m, v):
    m = ADAM_B1 * m + (1.0 - ADAM_B1) * g
    v = ADAM_B2 * v + (1.0 - ADAM_B2) * (g * g)
    m_hat = m / (1.0 - ADAM_B1 ** ADAM_STEP)
    v_hat = v / (1.0 - ADAM_B2 ** ADAM_STEP)
    delta = -ADAM_LR * (m_hat / (jnp.sqrt(v_hat) + ADAM_EPS) + ADAM_WD * w)
    return delta, m, v


def _adam(parts, w, m, v, name, tr=128):
    r, c = w.shape
    tr = next(t for t in (tr, 64, 32, 16, 8) if r % t == 0)

    def fn(cc, rr, pb, wb, mb, vb):
        g = pb[0].astype(F32)
        for d in range(1, N_DEV):
            g = g + pb[d].astype(F32)
        delta, nm, nv = _adam_math(g, wb, mb, vb)
        return g, delta, nm, nv

    blk = ((tr, c), lambda cc, rr: (rr, 0))
    o = SDS((r, c), F32)
    return _ew(fn, [(parts, (N_DEV, tr, c), lambda cc, rr: (0, rr, 0)), (w, *blk), (m, *blk), (v, *blk)],
               [(o, *blk, None)] * 4, (1, r // tr), name)


_SHARDED = ("w_in", "w_glu", "w_branch_attn", "w_branch_ssm", "w_out", "w_up", "w_down")
_COL_SHARDED = ("w_in", "w_glu", "w_branch_attn", "w_branch_ssm", "w_up")
_SMALL = ("attn_norm_g", "b_in", "attn_sinks", "ssm_a_re", "ssm_a_im", "ssm_log_dt", "ssm_b_re", "ssm_b_im",
          "ssm_c_re", "ssm_c_im", "ssm_d", "b_glu", "ffn_norm_g", "conv_w", "conv_b", "final_norm_g")
_WEIGHTS = ("attn_norm_g", "w_in", "b_in", "attn_sinks", "ssm_a_re", "ssm_a_im", "ssm_log_dt", "ssm_b_re", "ssm_b_im",
            "ssm_c_re", "ssm_c_im", "ssm_d", "w_glu", "b_glu", "w_branch_attn", "w_branch_ssm", "w_out", "ffn_norm_g",
            "w_up", "conv_w", "conv_b", "w_down", "final_norm_g")


def _unstack_cols(g):
    return g.transpose(1, 0, 2).reshape(g.shape[1], N_DEV * g.shape[2])


def _stack_cols(a):
    k, n = a.shape
    return a.reshape(k, N_DEV, n // N_DEV).transpose(1, 0, 2)


def _pack(arrs):
    flat = jnp.concatenate([a.reshape(-1) for a in arrs])
    pad = (-flat.shape[0]) % 1024
    return jnp.pad(flat, (0, pad)).reshape(-1, 128)


def _local_step(x, tgt, wts, small):
    L = x.shape[0]
    nr = lambda tm: L // tm

    h = _rmsnorm_fwd(x, small["attn_norm_g"], "norm1")
    projb = _mm(h, wts["w_in"], bias=small["b_in_p"], out_dtype=BF16, name="proj")
    proj = projb
    attn = _attn_fwd(projb, small["attn_sinks"], "attn_fwd")

    ab, bmat, cmat = _ssm_prep(small["a_re"], small["a_im"], small["logdt"], small["b_re"], small["b_im"],
                               small["c_re"], small["c_im"], "ssm_prep")
    u_seg = _to_segments(proj[:, C_U:C_GA])
    ends_f = _ssm_scan(u_seg, bmat, ab, reverse=False, name="ssm_ends_fwd")
    xs, init_f = _ssm_scan(u_seg, bmat, ab, reverse=False, ends=ends_f, name="ssm_scan_fwd")
    y_mm = _from_segments(_mm(xs, cmat, tb=True, tm=512, tk=1024, name="ssm_out"))

    def gelu_fn(c, r, yb, ub, db):
        yv = yb + db * ub
        return yv, _gelu(yv)

    tm = 512
    y, gy = _ew(gelu_fn, [(y_mm, *_rc(tm, 256)), (proj, *_rc(tm, 256, C_U // 256)), (small["ssm_d"], *_col(1, 256))],
                [(SDS((L, SW), F32), *_rc(tm, 256), None), (SDS((L, SW), BF16), *_rc(tm, 256), None)],
                (2, nr(tm)), "ssm_gelu")
    glu = _mm(gy, wts["w_glu"], bias=small["b_glu"], name="glu")

    def glu_fn(c, r, vb, gb):
        return (vb * _sigmoid(gb),)

    (ssm,) = _ew(glu_fn, [(glu, *_rc(tm, SW)), (glu, *_rc(tm, SW, 1))], [(SDS((L, SW), BF16), *_rc(tm, SW), None)],
                 (1, nr(tm)), "glu_gate")
    br_a = _mm(attn, wts["w_branch_attn"], out_dtype=BF16, name="branch_attn")
    br_s = _mm(ssm, wts["w_branch_ssm"], out_dtype=BF16, name="branch_ssm")

    def merge_fn(c, r, ab_, sb_, ga, gs):
        return (_sigmoid(ga) * ab_ + _sigmoid(gs) * sb_,)

    (merged,) = _ew(merge_fn, [(br_a, *_rc(tm, 256)), (br_s, *_rc(tm, 256)), (proj, *_rc(tm, 256, C_GA // 256)),
                               (proj, *_rc(tm, 256, C_GS // 256))],
                    [(SDS((L, D), BF16), *_rc(tm, 256), None)], (D // 256, nr(tm)), "merge")
    x1 = _mm(merged, wts["w_out"], res=x, name="out_proj")
    h2 = _rmsnorm_fwd(x1, small["ffn_norm_g"], "norm2")
    up = _mm(h2, wts["w_up"], out_dtype=BF16, name="ffn_up")
    tcf = 1408
    ncf = DFF // tcf
    tma = 256
    hb = 16

    def conv_gate(r, gate, halo, cw, cb):
        halo = halo[hb - 8:] * (r > 0).astype(F32)
        g1, g2 = _shift_rows(gate, halo, 1), _shift_rows(gate, halo, 2)
        return cb + cw[2:3] * gate + cw[1:2] * g1 + cw[0:1] * g2, g1, g2

    gate_specs = [(up, *_rc(tma, tcf, ncf)),
                  (up, (hb, tcf), lambda c, r: (jnp.maximum(r * (tma // hb) - 1, 0), c + ncf)),
                  (small["conv_w"], *_col(3, tcf)), (small["conv_b"], *_col(1, tcf))]

    def act_fn(c, r, val, gate, halo, cw, cb):
        return (val * _gelu(conv_gate(r, gate, halo, cw, cb)[0]),)

    (act,) = _ew(act_fn, [(up, *_rc(tma, tcf))] + gate_specs, [(SDS((L, DFF), BF16), *_rc(tma, tcf), None)],
                 (ncf, nr(tma)), "ffn_act")
    x2 = _mm(act, wts["w_down"], res=x1, name="ffn_down")
    d_x2, loss_cols, d_gf = _final_loss(x2, small["final_norm_g"], tgt, "final_loss")
    loss = jnp.sum(loss_cols)

    d_act = _mm(d_x2, wts["w_down"], tb=True, tn=tcf, out_dtype=BF16, name="d_act")
    dw_down = _mm(act, d_x2, ta=True, out_dtype=BF16, tm=tcf, tk=2048, name="dw_down")

    def act_bwd(c, r, da, val, gate, halo, cw, cb):
        cg, g1, g2 = conv_gate(r, gate, halo, cw, cb)
        gl, glg = _gelu_and_grad(cg)
        d_cg = da * val * glg
        row3 = lax.broadcasted_iota(jnp.int32, (3, tcf), 0)
        s0 = jnp.sum(d_cg * g2, axis=0, keepdims=True)
        s1 = jnp.sum(d_cg * g1, axis=0, keepdims=True)
        s2 = jnp.sum(d_cg * gate, axis=0, keepdims=True)
        dcw = jnp.where(row3 == 0, s0, jnp.where(row3 == 1, s1, s2))
        return da * gl, d_cg, dcw, jnp.sum(d_cg, axis=0, keepdims=True)

    d_val, d_cg, d_conv_w, d_conv_b = _ew(
        act_bwd, [(d_act, *_rc(tma, tcf)), (up, *_rc(tma, tcf))] + gate_specs,
        [(SDS((L, DFF), BF16), *_rc(tma, tcf), None), (SDS((L, DFF), BF16), *_rc(tma, tcf), None),
         (SDS((3, DFF), F32), *_col(3, tcf), "r"), (SDS((1, DFF), F32), *_col(1, tcf), "r")],
        (ncf, nr(tma)), "ffn_act_bwd")

    def gate_bwd(c, r, dcg, halo, cw):
        halo = halo[:8] * (r < nr(tma) - 1).astype(F32)
        return (cw[2:3] * dcg + cw[1:2] * _shift_rows_up(dcg, halo, 1) + cw[0:1] * _shift_rows_up(dcg, halo, 2),)

    (d_gate,) = _ew(gate_bwd, [(d_cg, *_rc(tma, tcf)),
                               (d_cg, (hb, tcf), lambda c, r: (jnp.minimum((r + 1) * (tma // hb), L // hb - 1), c)),
                               (small["conv_w"], *_col(3, tcf))],
                    [(SDS((L, DFF), BF16), *_rc(tma, tcf), None)], (ncf, nr(tma)), "ffn_gate_bwd")
    d_h2 = _mm(d_val, wts["w_up"][:, :DFF], tb=True, name="d_h2_val")
    d_h2 = _mm(d_gate, wts["w_up"][:, DFF:], tb=True, res=d_h2, name="d_h2_gate")
    dw_up_v = _mm(h2, d_val, ta=True, out_dtype=BF16, tn=tcf, tk=2048, name="dw_up_val")
    dw_up_g = _mm(h2, d_gate, ta=True, out_dtype=BF16, tn=tcf, tk=2048, name="dw_up_gate")
    d_x1, d_g2 = _rmsnorm_bwd(d_h2, x1, small["ffn_norm_g"], d_x2, "norm2_bwd")

    d_merged = _mm(d_x1, wts["w_out"], tb=True, out_dtype=BF16, name="d_merged")
    dw_out = _mm(merged, d_x1, ta=True, out_dtype=BF16, name="dw_out")

    def merge_bwd(c, r, dm, ab_, sb_, ga, gs):
        sa, ss = _sigmoid(ga), _sigmoid(gs)
        return dm * sa, dm * ss, dm * ab_ * (sa * (1.0 - sa)), dm * sb_ * (ss * (1.0 - ss))

    ob = SDS((L, D), BF16)
    d_bra, d_brs, d_ga, d_gs = _ew(
        merge_bwd, [(d_merged, *_rc(tm, 256)), (br_a, *_rc(tm, 256)), (br_s, *_rc(tm, 256)),
                    (proj, *_rc(tm, 256, C_GA // 256)), (proj, *_rc(tm, 256, C_GS // 256))],
        [(ob, *_rc(tm, 256), None)] * 4, (D // 256, nr(tm)), "merge_bwd")
    d_attn = _mm(d_bra, wts["w_branch_attn"], tb=True, out_dtype=BF16, name="d_attn")
    dw_ba = _mm(attn, d_bra, ta=True, out_dtype=BF16, name="dw_branch_attn")
    d_ssm = _mm(d_brs, wts["w_branch_ssm"], tb=True, name="d_ssm")
    dw_bs = _mm(ssm, d_brs, ta=True, out_dtype=BF16, name="dw_branch_ssm")
    dq, dkv_cur, dkv_prev, d_sinks = _attn_bwd(projb, small["attn_sinks"], d_attn, "attn_bwd")

    def glu_bwd(c, r, ds, vb, gb):
        sg = _sigmoid(gb)
        return ds * sg, ds * vb * (sg * (1.0 - sg))

    d_glu_v, d_glu_g = _ew(glu_bwd, [(d_ssm, *_rc(tm, SW)), (glu, *_rc(tm, SW)), (glu, *_rc(tm, SW, 1))],
                           [(SDS((L, SW), F32), *_rc(tm, SW), None)] * 2, (1, nr(tm)), "glu_gate_bwd")
    d_glu = jnp.concatenate([d_glu_v, d_glu_g], axis=1)
    d_gy = _mm(d_glu, wts["w_glu"], tb=True, name="d_gelu_y")
    dw_glu = _mm(gy, d_glu, ta=True, out_dtype=BF16, name="dw_glu")

    def gelu_bwd(c, r, dg, yb, ub, dgl):
        dy = dg * _gelu_grad(yb)
        return dy, jnp.sum(dy * ub, axis=0, keepdims=True), jnp.sum(dgl, axis=0, keepdims=True)

    dy, d_ssm_d, d_b_glu = _ew(
        gelu_bwd, [(d_gy, *_rc(tm, 256)), (y, *_rc(tm, 256)), (proj, *_rc(tm, 256, C_U // 256)), (d_glu, *_rc(tm, 512))],
        [(SDS((L, SW), F32), *_rc(tm, 256), None), (SDS((1, SW), F32), *_col(1, 256), "r"),
         (SDS((1, 2 * SW), F32), *_col(1, 512), "r")], (2, nr(tm)), "ssm_gelu_bwd")

    dy_seg = _to_segments(dy)
    ends_r = _ssm_scan(dy_seg, cmat, ab, reverse=True, name="ssm_ends_bwd")
    lam, dab8 = _ssm_scan(dy_seg, cmat, ab, reverse=True, ends=ends_r, xs=xs, init=init_f, name="ssm_scan_bwd")
    du_mm = _from_segments(_mm(lam, bmat, tb=True, tm=512, tk=1024, name="ssm_du"))
    dbm = _mm(u_seg, lam, ta=True, tm=512, name="ssm_dbmat")
    dcm = _mm(dy_seg, xs, ta=True, tm=512, name="ssm_dcmat")
    d_are, d_aim, d_ldt, d_bre, d_bim, d_cre, d_cim = _ssm_param_bwd(
        small["a_re"], small["a_im"], small["logdt"], small["b_re"], small["b_im"], dab8, dbm, dcm, "ssm_param_bwd")

    nb = L // BLK

    def dproj_fn(c, r, dqb, cur, prv, du, dyb, dsk, dga, dgs):
        dkv = cur + prv * (r < nb - 1).astype(F32)
        dub = du + dsk * dyb
        full = jnp.concatenate([dqb.astype(F32), dkv, dub, dga.astype(F32), dgs.astype(F32),
                                jnp.zeros((BLK, INP - INC), F32)], axis=1)
        return full, jnp.sum(full, axis=0, keepdims=True)

    rowb = lambda w: ((BLK, w), lambda c, r: (r, 0))
    dproj, d_b_in = _ew(
        dproj_fn, [(dq, *rowb(AW)), (dkv_cur, *rowb(256)),
                   (dkv_prev, (BLK, 256), lambda c, r: (jnp.minimum(r + 1, nb - 1), 0)),
                   (du_mm, *rowb(SW)), (dy, *rowb(SW)), (small["ssm_d"], *_col(1, SW)), (d_ga, *rowb(D)), (d_gs, *rowb(D))],
        [(SDS((L, INP), BF16), *rowb(INP), None), (SDS((1, INP), F32), *_col(1, INP), "all")], (1, nb), "dproj")
    d_h = _mm(dproj, wts["w_in"], tb=True, name="d_h")
    dw_in = _mm(h, dproj, ta=True, out_dtype=BF16, name="dw_in")
    grad_x, d_g1 = _rmsnorm_bwd(d_h, x, small["attn_norm_g"], d_x1, "norm1_bwd")

    wgrads = {"w_in": dw_in[:, :INC], "w_glu": dw_glu, "w_branch_attn": dw_ba, "w_branch_ssm": dw_bs, "w_out": dw_out,
              "w_up_val": dw_up_v, "w_up_gate": dw_up_g, "w_down": dw_down}
    sgrads = {"attn_norm_g": d_g1, "b_in": d_b_in[:, :INC], "attn_sinks": d_sinks[:, :NQ], "a_re": d_are, "a_im": d_aim,
              "logdt": d_ldt, "b_re": d_bre, "b_im": d_bim, "c_re": d_cre, "c_im": d_cim, "ssm_d": d_ssm_d,
              "b_glu": d_b_glu, "ffn_norm_g": d_g2, "conv_w": d_conv_w, "conv_b": d_conv_b, "final_norm_g": d_gf}
    return loss, grad_x, wgrads, sgrads


def _small_layouts(p):
    gp = lambda a: a.reshape(1, NS)
    hgp = lambda a: a.transpose(2, 0, 1).reshape(H, NS)
    chgp = lambda a: a.transpose(1, 0, 2).reshape(H, NS)
    return {
        "attn_norm_g": p["attn_norm_g"].reshape(1, D), "ffn_norm_g": p["ffn_norm_g"].reshape(1, D),
        "final_norm_g": p["final_norm_g"].reshape(1, D),
        "b_in_p": jnp.pad(p["b_in"].reshape(1, INC), ((0, 0), (0, INP - INC))),
        "attn_sinks": p["attn_sinks"].reshape(1, NQ),
        "a_re": gp(p["ssm_a_re"]), "a_im": gp(p["ssm_a_im"]), "logdt": jnp.repeat(p["ssm_log_dt"], P).reshape(1, NS),
        "b_re": hgp(p["ssm_b_re"]), "b_im": hgp(p["ssm_b_im"]), "c_re": chgp(p["ssm_c_re"]), "c_im": chgp(p["ssm_c_im"]),
        "ssm_d": p["ssm_d"].reshape(1, SW), "b_glu": p["b_glu"].reshape(1, 2 * SW),
        "conv_w": p["conv_w"], "conv_b": p["conv_b"].reshape(1, DFF),
    }


def _small_grads_to_param_shapes(sg):
    from_hgp = lambda a: a.reshape(H, G, P).transpose(1, 2, 0)
    from_chgp = lambda a: a.reshape(H, G, P).transpose(1, 0, 2)
    return {
        "attn_norm_g": sg["attn_norm_g"].reshape(D), "b_in": sg["b_in"].reshape(INC),
        "attn_sinks": sg["attn_sinks"].reshape(NQ),
        "ssm_a_re": sg["a_re"].reshape(G, P), "ssm_a_im": sg["a_im"].reshape(G, P),
        "ssm_log_dt": jnp.sum(sg["logdt"].reshape(G, P), axis=1),
        "ssm_b_re": from_hgp(sg["b_re"]), "ssm_b_im": from_hgp(sg["b_im"]),
        "ssm_c_re": from_chgp(sg["c_re"]), "ssm_c_im": from_chgp(sg["c_im"]),
        "ssm_d": sg["ssm_d"].reshape(SW), "b_glu": sg["b_glu"].reshape(2 * SW),
        "ffn_norm_g": sg["ffn_norm_g"].reshape(D), "conv_w": sg["conv_w"], "conv_b": sg["conv_b"].reshape(DFF),
        "final_norm_g": sg["final_norm_g"].reshape(D),
    }


def kernel(x, attn_norm_g, w_in, b_in, attn_sinks, ssm_a_re, ssm_a_im, ssm_log_dt, ssm_b_re, ssm_b_im, ssm_c_re, ssm_c_im, ssm_d, w_glu, b_glu, w_branch_attn, w_branch_ssm, w_out, ffn_norm_g, w_up, conv_w, conv_b, w_down, final_norm_g, loss_target, m_attn_norm_g, m_w_in, m_b_in, m_attn_sinks, m_ssm_a_re, m_ssm_a_im, m_ssm_log_dt, m_ssm_b_re, m_ssm_b_im, m_ssm_c_re, m_ssm_c_im, m_ssm_d, m_w_glu, m_b_glu, m_w_branch_attn, m_w_branch_ssm, m_w_out, m_ffn_norm_g, m_w_up, m_conv_w, m_conv_b, m_w_down, m_final_norm_g, v_attn_norm_g, v_w_in, v_b_in, v_attn_sinks, v_ssm_a_re, v_ssm_a_im, v_ssm_log_dt, v_ssm_b_re, v_ssm_b_im, v_ssm_c_re, v_ssm_c_im, v_ssm_d, v_w_glu, v_b_glu, v_w_branch_attn, v_w_branch_ssm, v_w_out, v_ffn_norm_g, v_w_up, v_conv_w, v_conv_b, v_w_down, v_final_norm_g):
    args = dict(locals())
    sq = lambda a: a if a.ndim == 1 else a[0]
    wv = {n: sq(args[n]) for n in _WEIGHTS}
    mv = {n: sq(args["m_" + n]) for n in _WEIGHTS}
    vv = {n: sq(args["v_" + n]) for n in _WEIGHTS}
    me = 4 * lax.axis_index("x") + 2 * lax.axis_index("y") + lax.axis_index("c")

    cw_shard = jnp.pad(wv["conv_w"], ((0, 5), (0, 64)))
    *gathered, cw_g = _exchange([wv[n].astype(BF16) for n in _SHARDED] + [cw_shard], False, "gather_weights")
    full = {}
    for n, g in zip(_SHARDED, gathered):
        full[n] = _unstack_cols(g) if n in _COL_SHARDED else g.reshape(N_DEV * g.shape[1], g.shape[2])
    full["w_in"] = jnp.pad(full["w_in"], ((0, 0), (0, INP - INC)))
    small_p = dict(wv)
    small_p["conv_w"] = _unstack_cols(cw_g[:, :3, :DFF // N_DEV])
    small = _small_layouts(small_p)

    loss, grad_x, wg, sg = _local_step(x[0], loss_target[0], full, small)
    loss = lax.psum(loss, MESH_AXES)

    stacked = {
        "w_in": _stack_cols(wg["w_in"]), "w_glu": _stack_cols(wg["w_glu"]),
        "w_branch_attn": _stack_cols(wg["w_branch_attn"]), "w_branch_ssm": _stack_cols(wg["w_branch_ssm"]),
        "w_out": wg["w_out"].reshape(N_DEV, D // N_DEV, D),
        "w_up": _stack_cols(jnp.concatenate([wg["w_up_val"], wg["w_up_gate"]], axis=1)),
        "w_down": wg["w_down"].reshape(N_DEV, DFF // N_DEV, D),
    }
    sgp = _small_grads_to_param_shapes(sg)
    small_names = [n for n in _SMALL]
    packed_g = _pack([sgp[n] for n in small_names])
    parts = _exchange([stacked[n] for n in _SHARDED], True, "scatter_grads")
    (small_all,) = _exchange([packed_g], False, "gather_small_grads")

    outs_g, outs_d, outs_m, outs_v = {}, {}, {}, {}
    for n, pt in zip(_SHARDED, parts):
        outs_g[n], outs_d[n], outs_m[n], outs_v[n] = _adam(pt, wv[n], mv[n], vv[n], "adam_" + n)

    sizes = [int(math.prod(sgp[n].shape)) for n in small_names]
    offs = [0]
    for s in sizes:
        offs.append(offs[-1] + s)

    def local_part(n, a):
        if n == "conv_w":
            return lax.dynamic_slice(a, (0, me * (DFF // N_DEV)), (3, DFF // N_DEV))
        return a

    rows = packed_g.shape[0]

    def sum_fn(cc, rr, pb):
        g = pb[0]
        for d in range(1, N_DEV):
            g = g + pb[d]
        return (g,)

    (gsum,) = _ew(sum_fn, [(small_all, (N_DEV, rows, 128), lambda cc, rr: (0, 0, 0))],
                  [(SDS((rows, 128), F32), (rows, 128), lambda cc, rr: (0, 0), None)], (1, 1), "sum_small_grads")
    gflat = gsum.reshape(-1)
    gsmall = {n: local_part(n, gflat[offs[i]:offs[i + 1]].reshape(sgp[n].shape)) for i, n in enumerate(small_names)}
    pw = _pack([wv[n] for n in small_names])
    pm = _pack([mv[n] for n in small_names])
    pv = _pack([vv[n] for n in small_names])
    pg = _pack([gsmall[n] for n in small_names])
    prow = pw.shape[0]

    def adam_small(cc, rr, gb, wb, mb, vb):
        return _adam_math(gb, wb, mb, vb)

    whole = ((prow, 128), lambda cc, rr: (0, 0))
    sd, sm, sv = _ew(adam_small, [(pg, *whole), (pw, *whole), (pm, *whole), (pv, *whole)],
                     [(SDS((prow, 128), F32), *whole, None)] * 3, (1, 1), "adam_small")
    lsizes = [int(math.prod(wv[n].shape)) for n in small_names]
    loffs = [0]
    for s in lsizes:
        loffs.append(loffs[-1] + s)
    for i, n in enumerate(small_names):
        take = lambda a: a.reshape(-1)[loffs[i]:loffs[i + 1]].reshape(wv[n].shape)
        outs_g[n], outs_d[n], outs_m[n], outs_v[n] = gsmall[n], take(sd), take(sm), take(sv)

    lead = lambda n, a: a if args[n].ndim == 1 else a[None]
    grad_x = grad_x[None]
    return (loss, grad_x, *[lead(n, outs_g[n]) for n in _WEIGHTS], *[lead(n, outs_d[n]) for n in _WEIGHTS],
            *[lead(n, outs_m[n]) for n in _WEIGHTS], *[lead(n, outs_v[n]) for n in _WEIGHTS])
```

```python
import functools
import math

import jax
import jax.numpy as jnp
from jax import lax
from jax.experimental import pallas as pl
from jax.experimental.pallas import tpu as pltpu

F32 = jnp.float32
BF16 = jnp.bfloat16
SDS = jax.ShapeDtypeStruct

N_DEV = 8
D = 2048
NQ, NKV, HD = 16, 2, 64
AW = NQ * HD
BLK = 128
SW, G, H, P = 512, 32, 16, 64
NS = G * P
DFF = 5632
INC = AW + 2 * NKV * HD + SW + 2 * D
INP = 6144
C_K, C_U, C_GA, C_GS = AW, AW + 2 * NKV * HD, AW + 2 * NKV * HD + SW, AW + 2 * NKV * HD + SW + D
RMS_EPS = 1e-6
NEG_BIG = -1e30
ADAM_LR, ADAM_B1, ADAM_B2, ADAM_EPS, ADAM_WD, ADAM_STEP = 0.001, 0.9, 0.999, 1e-08, 0.01, 10
NSEG = 8
VMEM_CAP_MB = 60
MESH_AXES = ("x", "y", "c")


def _cparams(sem, vmem_mb):
    return pltpu.CompilerParams(dimension_semantics=sem, vmem_limit_bytes=min(int(vmem_mb), VMEM_CAP_MB) << 20)


LANES = 128


def _tile(dim, pref):
    if dim <= pref:
        return dim
    for t in range(pref - pref % LANES, 0, -LANES):
        if dim % t == 0:
            return t
    raise ValueError(f"no tile for {dim}")


def _mm(a, b, *, ta=False, tb=False, bias=None, res=None, out_dtype=F32, tm=1024, tn=1024, tk=3072, name):
    m, k = (a.shape[1], a.shape[0]) if ta else a.shape
    n = b.shape[0] if tb else b.shape[1]
    assert (b.shape[1] if tb else b.shape[0]) == k, (a.shape, b.shape, ta, tb)
    tm, tn, tk = _tile(m, tm), _tile(n, tn), _tile(k, tk)
    nk = k // tk
    dims = (((0 if ta else 1,), (1 if tb else 0,)), ((), ()))
    has_bias, has_res = bias is not None, res is not None

    def body(*refs):
        a_ref, b_ref = refs[0], refs[1]
        pos = 2
        bias_ref = refs[pos] if has_bias else None
        pos += has_bias
        res_ref = refs[pos] if has_res else None
        pos += has_res
        o_ref = refs[pos]

        def product():
            return lax.dot_general(a_ref[...].astype(BF16), b_ref[...].astype(BF16), dims, preferred_element_type=F32)

        def finish(r):
            if has_bias:
                r = r + bias_ref[...]
            if has_res:
                r = r + res_ref[...].astype(F32)
            o_ref[...] = r.astype(o_ref.dtype)

        if nk == 1:
            finish(product())
            return
        acc_ref = refs[-1]
        kk = pl.program_id(2)

        @pl.when(kk == 0)
        def _():
            acc_ref[...] = product()

        @pl.when(jnp.logical_and(kk > 0, kk < nk - 1))
        def _():
            acc_ref[...] += product()

        @pl.when(kk == nk - 1)
        def _():
            finish(acc_ref[...] + product())

    ins = [a, b]
    in_specs = [pl.BlockSpec((tk, tm), lambda i, j, kk: (kk, i)) if ta else pl.BlockSpec((tm, tk), lambda i, j, kk: (i, kk)),
                pl.BlockSpec((tn, tk), lambda i, j, kk: (j, kk)) if tb else pl.BlockSpec((tk, tn), lambda i, j, kk: (kk, j))]
    byt = 2 * tm * tk * a.dtype.itemsize + 2 * tk * tn * b.dtype.itemsize + 2 * tm * tn * jnp.dtype(out_dtype).itemsize
    byt += 2 * 4 * tm * tn
    if has_bias:
        ins.append(bias)
        in_specs.append(pl.BlockSpec((1, tn), lambda i, j, kk: (0, j)))
    if has_res:
        ins.append(res)
        in_specs.append(pl.BlockSpec((tm, tn), lambda i, j, kk: (i, j)))
        byt += 2 * tm * tn * res.dtype.itemsize
    return pl.pallas_call(
        body, out_shape=SDS((m, n), out_dtype), grid=(m // tm, n // tn, nk), in_specs=in_specs,
        out_specs=pl.BlockSpec((tm, tn), lambda i, j, kk: (i, j)),
        scratch_shapes=[pltpu.VMEM((tm, tn), F32)] if nk > 1 else [], name=name,
        compiler_params=_cparams(("parallel", "parallel", "arbitrary"), byt / 2**20 + 8),
    )(*ins)


def _ew(fn, ins, outs, grid, name, vmem_mb=40):
    n_in = len(ins)
    accs = [o[3] for o in outs]

    def body(*refs):
        c, r = pl.program_id(0), pl.program_id(1)
        vals = fn(c, r, *[ref[...].astype(F32) for ref in refs[:n_in]])
        for o_ref, v, acc in zip(refs[n_in:], vals, accs):
            if acc is None:
                o_ref[...] = v.astype(o_ref.dtype)
            else:
                first = (r == 0) if acc == "r" else jnp.logical_and(r == 0, c == 0)

                @pl.when(first)
                def _(o_ref=o_ref, v=v):
                    o_ref[...] = v.astype(o_ref.dtype)

                @pl.when(jnp.logical_not(first))
                def _(o_ref=o_ref, v=v):
                    o_ref[...] += v.astype(o_ref.dtype)

    res = pl.pallas_call(
        body, out_shape=tuple(o[0] for o in outs), grid=grid,
        in_specs=[pl.BlockSpec(bs, im) for _, bs, im in ins],
        out_specs=tuple(pl.BlockSpec(bs, im) for _, bs, im, _ in outs), name=name,
        compiler_params=_cparams(("arbitrary", "arbitrary"), vmem_mb),
    )(*[a for a, _, _ in ins])
    return res


def _rc(tm, tc, coff=0):
    return (tm, tc), (lambda c, r: (r, c + coff))


def _col(rows, tc, coff=0):
    return (rows, tc), (lambda c, r: (0, c + coff))


def _gelu(x):
    return 0.5 * x * (1.0 + lax.erf(x * (2.0 ** -0.5)))


def _gelu_and_grad(x):
    cdf = 0.5 * (1.0 + lax.erf(x * (2.0 ** -0.5)))
    return x * cdf, cdf + x * jnp.exp(-0.5 * x * x) * (1.0 / math.sqrt(2.0 * math.pi))


def _gelu_grad(x):
    return _gelu_and_grad(x)[1]


def _sigmoid(x):
    return 1.0 / (1.0 + jnp.exp(-x))


def _shift_rows(x, halo, s):
    rolled = pltpu.roll(x, s, 0)
    row8 = lax.broadcasted_iota(jnp.int32, halo.shape, 0)
    head = jnp.where(row8 < s, pltpu.roll(halo, s, 0), rolled[0:8])
    return jnp.concatenate([head, rolled[8:]], axis=0)


def _shift_rows_up(x, halo, s):
    tm = x.shape[0]
    rolled = pltpu.roll(x, tm - s, 0)
    row8 = lax.broadcasted_iota(jnp.int32, halo.shape, 0)
    tail = jnp.where(row8 >= 8 - s, pltpu.roll(halo, 8 - s, 0), rolled[tm - 8:])
    return jnp.concatenate([rolled[:tm - 8], tail], axis=0)


def _rmsnorm_fwd(x, g, name, tm=256):
    L = x.shape[0]

    def fn(c, r, xb, gb):
        rstd = lax.rsqrt(jnp.mean(xb * xb, axis=-1, keepdims=True) + RMS_EPS)
        return ((xb * rstd) * gb,)

    return _ew(fn, [(x, *_rc(tm, D)), (g, *_col(1, D))], [(SDS((L, D), BF16), *_rc(tm, D), None)], (1, L // tm), name)[0]


def _rmsnorm_bwd(dh, x, g, dres, name, tm=256):
    L = x.shape[0]

    def fn(c, r, dhb, xb, gb, drb):
        rstd = lax.rsqrt(jnp.mean(xb * xb, axis=-1, keepdims=True) + RMS_EPS)
        y = xb * rstd
        dy = dhb * gb
        dx = rstd * (dy - y * jnp.mean(dy * y, axis=-1, keepdims=True))
        return drb + dx, jnp.sum(dhb * y, axis=0, keepdims=True)

    return _ew(fn, [(dh, *_rc(tm, D)), (x, *_rc(tm, D)), (g, *_col(1, D)), (dres, *_rc(tm, D))],
               [(SDS((L, D), F32), *_rc(tm, D), None), (SDS((1, D), F32), *_col(1, D), "all")], (1, L // tm), name)


def _final_loss(x2, g, tgt, name, tm=256):
    L = x2.shape[0]

    def fn(c, r, xb, gb, tb):
        rstd = lax.rsqrt(jnp.mean(xb * xb, axis=-1, keepdims=True) + RMS_EPS)
        y = xb * rstd
        err = y * gb - tb
        dout = err * (1.0 / D)
        dy = dout * gb
        dx = rstd * (dy - y * jnp.mean(dy * y, axis=-1, keepdims=True))
        return dx, jnp.sum(err * err, axis=0, keepdims=True) * (0.5 / D), jnp.sum(dout * y, axis=0, keepdims=True)

    return _ew(fn, [(x2, *_rc(tm, D)), (g, *_col(1, D)), (tgt, *_rc(tm, D))],
               [(SDS((L, D), F32), *_rc(tm, D), None), (SDS((1, D), F32), *_col(1, D), "all"),
                (SDS((1, D), F32), *_col(1, D), "all")], (1, L // tm), name)


def _attn_setup(n, kvc, kvp):
    kv = jnp.concatenate([kvp, kvc], axis=0).astype(F32)
    lo = lax.broadcasted_iota(jnp.int32, (2 * BLK, 128), 1) < HD

    def halves(t):
        tr = pltpu.roll(t, HD, 1)
        z = jnp.zeros_like(t)
        return {(0, 0): jnp.where(lo, t, z).astype(BF16), (0, 1): jnp.where(lo, z, tr).astype(BF16),
                (1, 0): jnp.where(lo, tr, z).astype(BF16), (1, 1): jnp.where(lo, z, t).astype(BF16)}

    kmat, vmat = halves(kv[:, :128]), halves(kv[:, 128:])
    qi = lax.broadcasted_iota(jnp.int32, (BLK, 2 * BLK), 0)
    si = lax.broadcasted_iota(jnp.int32, (BLK, 2 * BLK), 1)
    dist = qi + BLK - si
    valid = (dist >= 0) & (dist < BLK) & ((n > 0) | (si >= BLK))
    return kmat, vmat, valid, dist.astype(F32)


def _attn_probs(qp, kmat_ge, valid, distf, slope, sink):
    s = lax.dot_general(qp, kmat_ge, (((1,), (1,)), ((), ())), preferred_element_type=F32) * (HD ** -0.5)
    s = jnp.where(valid, s - slope * distf, NEG_BIG)
    m = jnp.maximum(jnp.max(s, axis=-1, keepdims=True), sink)
    p = jnp.exp(s - m)
    esink = jnp.exp(sink - m)
    den = jnp.sum(p, axis=-1, keepdims=True) + esink
    return p / den, esink / den


def _slope(h):
    return 2.0 ** (-8.0 * (h + 1) / NQ)


def _attn_fwd(projb, sinks, name):
    L = projb.shape[0]
    nb = L // BLK

    def body(s_ref, q_ref, kvc_ref, kvp_ref, o_ref):
        n = pl.program_id(0)
        kmat, vmat, valid, distf = _attn_setup(n, kvc_ref[...], kvp_ref[...])
        for j in range(NQ // 2):
            g = j // (NQ // 4)
            qp = q_ref[:, 128 * j:128 * (j + 1)]
            acc = jnp.zeros((BLK, 128), F32)
            for e in range(2):
                h = 2 * j + e
                p, _ = _attn_probs(qp, kmat[(g, e)], valid, distf, _slope(h), s_ref[0, h])
                acc = acc + jnp.dot(p.astype(BF16), vmat[(g, e)], preferred_element_type=F32)
            o_ref[:, 128 * j:128 * (j + 1)] = acc.astype(BF16)

    return pl.pallas_call(
        body, out_shape=SDS((L, AW), BF16), grid=(nb,),
        in_specs=[pl.BlockSpec(memory_space=pltpu.SMEM),
                  pl.BlockSpec((BLK, AW), lambda n: (n, 0)),
                  pl.BlockSpec((BLK, 256), lambda n: (n, C_K // 256)),
                  pl.BlockSpec((BLK, 256), lambda n: (jnp.maximum(n - 1, 0), C_K // 256))],
        out_specs=pl.BlockSpec((BLK, AW), lambda n: (n, 0)), name=name,
        compiler_params=_cparams(("arbitrary",), 32),
    )(sinks, projb, projb, projb)


def _attn_bwd(projb, sinks, dattn, name):
    L = projb.shape[0]
    nb = L // BLK

    def body(s_ref, q_ref, kvc_ref, kvp_ref, do_ref, dq_ref, dcur_ref, dprev_ref, dsink_ref):
        n = pl.program_id(0)
        kmat, vmat, valid, distf = _attn_setup(n, kvc_ref[...], kvp_ref[...])
        lo128 = lax.broadcasted_iota(jnp.int32, (BLK, 128), 1) < HD
        lane = lax.broadcasted_iota(jnp.int32, (1, 128), 1)
        dk = [jnp.zeros((2 * BLK, 128), F32) for _ in range(NKV)]
        dv = [jnp.zeros((2 * BLK, 128), F32) for _ in range(NKV)]
        dsv = jnp.zeros((1, 128), F32)
        tn_dims = (((0,), (0,)), ((), ()))
        for j in range(NQ // 2):
            g = j // (NQ // 4)
            qp = q_ref[:, 128 * j:128 * (j + 1)]
            dop = do_ref[:, 128 * j:128 * (j + 1)]
            dqp = jnp.zeros((BLK, 128), F32)
            for e in range(2):
                h = 2 * j + e
                p, psink = _attn_probs(qp, kmat[(g, e)], valid, distf, _slope(h), s_ref[0, h])
                dp = lax.dot_general(dop, vmat[(g, e)], (((1,), (1,)), ((), ())), preferred_element_type=F32)
                drow = jnp.sum(p * dp, axis=-1, keepdims=True)
                ds = p * (dp - drow)
                dsv = dsv + jnp.where(lane == h, -jnp.sum(psink * drow, axis=0, keepdims=True), 0.0)
                dsb = (ds * (HD ** -0.5)).astype(BF16)
                dqp = dqp + jnp.dot(dsb, kmat[(g, e)], preferred_element_type=F32)
                half = lo128 if e == 0 else jnp.logical_not(lo128)
                zb = jnp.zeros_like(qp)
                dk[g] = dk[g] + lax.dot_general(dsb, jnp.where(half, qp, zb), tn_dims, preferred_element_type=F32)
                dv[g] = dv[g] + lax.dot_general(p.astype(BF16), jnp.where(half, dop, zb), tn_dims,
                                                preferred_element_type=F32)
            dq_ref[:, 128 * j:128 * (j + 1)] = dqp.astype(BF16)
        lo256 = lax.broadcasted_iota(jnp.int32, (2 * BLK, 128), 1) < HD
        tot = [t + pltpu.roll(t, HD, 1) for t in (dk[0], dk[1], dv[0], dv[1])]
        dkv = jnp.concatenate([jnp.where(lo256, tot[0], tot[1]), jnp.where(lo256, tot[2], tot[3])], axis=1)
        dprev_ref[...] = dkv[:BLK]
        dcur_ref[...] = dkv[BLK:]

        @pl.when(n == 0)
        def _():
            dsink_ref[...] = dsv

        @pl.when(n > 0)
        def _():
            dsink_ref[...] += dsv

    return pl.pallas_call(
        body, out_shape=(SDS((L, AW), BF16), SDS((L, 256), F32), SDS((L, 256), F32), SDS((1, 128), F32)), grid=(nb,),
        in_specs=[pl.BlockSpec(memory_space=pltpu.SMEM),
                  pl.BlockSpec((BLK, AW), lambda n: (n, 0)),
                  pl.BlockSpec((BLK, 256), lambda n: (n, C_K // 256)),
                  pl.BlockSpec((BLK, 256), lambda n: (jnp.maximum(n - 1, 0), C_K // 256)),
                  pl.BlockSpec((BLK, AW), lambda n: (n, 0))],
        out_specs=(pl.BlockSpec((BLK, AW), lambda n: (n, 0)), pl.BlockSpec((BLK, 256), lambda n: (n, 0)),
                   pl.BlockSpec((BLK, 256), lambda n: (n, 0)), pl.BlockSpec((1, 128), lambda n: (0, 0))),
        name=name, compiler_params=_cparams(("arbitrary",), 32),
    )(sinks, projb, projb, projb, dattn)


def _disc(a_re, a_im, logdt, b_re, b_im):
    dt = jnp.exp(logdt)
    mag = jnp.exp(a_re * dt)
    ab_re = mag * jnp.cos(a_im * dt)
    ab_im = mag * jnp.sin(a_im * dt)
    nr = ab_re - 1.0
    ni = ab_im
    den = a_re * a_re + a_im * a_im
    z_re = (nr * a_re + ni * a_im) / den
    z_im = (ni * a_re - nr * a_im) / den
    return ab_re, ab_im, z_re * b_re - z_im * b_im, z_re * b_im + z_im * b_re


def _group_mask():
    row = lax.broadcasted_iota(jnp.int32, (SW, NS), 0) // H
    col = lax.broadcasted_iota(jnp.int32, (SW, NS), 1) // P
    return row == col


def _block_diag(re, im):
    mask = _group_mask()
    z = jnp.zeros((SW, NS), F32)
    return jnp.concatenate([jnp.where(mask, jnp.tile(re, (G, 1)), z), jnp.where(mask, jnp.tile(im, (G, 1)), z)], axis=1)


def _block_diag_t(big):
    mask = _group_mask()
    z = jnp.zeros((SW, NS), F32)
    re = jnp.sum(jnp.where(mask, big[:, :NS], z).reshape(G, H, NS), axis=0)
    im = jnp.sum(jnp.where(mask, big[:, NS:], z).reshape(G, H, NS), axis=0)
    return re, im


def _ssm_prep(a_re, a_im, logdt, b_re, b_im, c_re, c_im, name):
    def body(are, aim, ldt, bre, bim, cre, cim, ab_ref, bm_ref, cm_ref):
        ab_re, ab_im, bb_re, bb_im = _disc(are[...], aim[...], ldt[...], bre[...], bim[...])
        ab_ref[...] = jnp.concatenate([ab_re, ab_im], axis=1)
        bm_ref[...] = _block_diag(bb_re, bb_im).astype(BF16)
        cm_ref[...] = _block_diag(cre[...], -cim[...]).astype(BF16)

    return pl.pallas_call(body, out_shape=(SDS((1, 2 * NS), F32), SDS((SW, 2 * NS), BF16), SDS((SW, 2 * NS), BF16)),
                          name=name, compiler_params=pltpu.CompilerParams(vmem_limit_bytes=48 << 20),
                          )(a_re, a_im, logdt, b_re, b_im, c_re, c_im)


def _ssm_param_bwd(a_re, a_im, logdt, b_re, b_im, dab8, dbm, dcm, name):
    def body(are, aim, ldt, bre, bim, dab_ref, dbm_ref, dcm_ref, o_are, o_aim, o_ldt, o_bre, o_bim, o_cre, o_cim):
        dab = jnp.sum(dab_ref[...], axis=0, keepdims=True)
        dbb_re, dbb_im = _block_diag_t(dbm_ref[...])
        _, vjp = jax.vjp(_disc, are[...], aim[...], ldt[...], bre[...], bim[...])
        d_are, d_aim, d_ldt, d_bre, d_bim = vjp((dab[:, :NS], dab[:, NS:], dbb_re, dbb_im))
        o_are[...], o_aim[...], o_ldt[...], o_bre[...], o_bim[...] = d_are, d_aim, d_ldt, d_bre, d_bim
        dc_re, dc_imn = _block_diag_t(dcm_ref[...])
        o_cre[...] = dc_re
        o_cim[...] = -dc_imn

    v1, vh = SDS((1, NS), F32), SDS((H, NS), F32)
    return pl.pallas_call(body, out_shape=(v1, v1, v1, vh, vh, vh, vh), name=name,
                          compiler_params=pltpu.CompilerParams(vmem_limit_bytes=56 << 20),
                          )(a_re, a_im, logdt, b_re, b_im, dab8, dbm, dcm)


def _ssm_scan(src, wmat, ab, *, reverse, ends=None, xs=None, init=None, name, tk=32):
    L = src.shape[0]
    rows = NSEG * tk
    nch = L // rows
    seg_len = L // NSEG
    n_sq = int(math.log2(seg_len))
    assert 2 ** n_sq == seg_len and L % rows == 0
    first_pass = ends is None
    with_dab = (not first_pass) and reverse
    slab = 512
    n_slab = NS // slab

    def body(*refs):
        src_ref, w_ref, ab_ref = refs[:3]
        pos = 3
        if not first_pass:
            ends_ref = refs[pos]
            pos += 1
        if with_dab:
            xs_ref, xsh_ref, init_ref = refs[pos:pos + 3]
            pos += 3
        if first_pass:
            (e_ref,) = refs[pos:pos + 1]
            pos += 1
        else:
            st_out_ref, aux_ref = refs[pos:pos + 2]
            pos += 2
        buf_ref, st_ref = refs[pos:pos + 2]
        i = pl.program_id(0)
        a_re = ab_ref[:, :NS]
        a_im = -ab_ref[:, NS:] if reverse else ab_ref[:, NS:]

        @pl.when(i == 0)
        def _():
            if first_pass:
                st_ref[...] = jnp.zeros_like(st_ref)
            else:
                pr, pi = a_re, a_im
                for _ in range(n_sq):
                    pr, pi = pr * pr - pi * pi, 2.0 * pr * pi
                zr = jnp.zeros((1, NS), F32)
                cr, ci = zr, zr
                order = list(range(NSEG - 1, -1, -1)) if reverse else list(range(NSEG))
                st_ref[order[0]:order[0] + 1, :] = jnp.zeros((1, 2 * NS), F32)
                for jprev, j in zip(order[:-1], order[1:]):
                    er, ei = ends_ref[jprev:jprev + 1, :NS], ends_ref[jprev:jprev + 1, NS:]
                    cr, ci = er + pr * cr - pi * ci, ei + pr * ci + pi * cr
                    st_ref[j:j + 1, :NS] = cr
                    st_ref[j:j + 1, NS:] = ci
                if not reverse:
                    aux_ref[...] = st_ref[...]
                else:
                    aux_ref[...] = jnp.zeros_like(aux_ref)

        buf_ref[...] = jnp.dot(src_ref[...].astype(BF16), w_ref[...], preferred_element_type=F32)

        for s in range(n_slab):
            re_sl, im_sl = pl.ds(s * slab, slab), pl.ds(NS + s * slab, slab)
            ar = jnp.broadcast_to(a_re[:, s * slab:(s + 1) * slab], (NSEG, slab))
            ai = jnp.broadcast_to(a_im[:, s * slab:(s + 1) * slab], (NSEG, slab))

            def step(t, carry, re_sl=re_sl, im_sl=im_sl, ar=ar, ai=ai):
                k = (tk - 1 - t) if reverse else t
                r0 = pl.multiple_of(k * NSEG, NSEG)
                xr, xi = carry[0], carry[1]
                nr = ar * xr - ai * xi + buf_ref[pl.ds(r0, NSEG), re_sl]
                ni = ar * xi + ai * xr + buf_ref[pl.ds(r0, NSEG), im_sl]
                if not first_pass:
                    buf_ref[pl.ds(r0, NSEG), re_sl] = nr
                    buf_ref[pl.ds(r0, NSEG), im_sl] = ni
                if not with_dab:
                    return nr, ni
                rp = pl.multiple_of((k - 1) * NSEG, NSEG)
                xpr, xpi = xs_ref[pl.ds(rp, NSEG), re_sl], xs_ref[pl.ds(rp, NSEG), im_sl]
                return nr, ni, carry[2] + nr * xpr + ni * xpi, carry[3] + ni * xpr - nr * xpi

            carry = (st_ref[:, re_sl], st_ref[:, im_sl])
            if with_dab:
                z = jnp.zeros((NSEG, slab), F32)
                carry = lax.fori_loop(0, tk - 1, step, carry + (z, z))
                xr, xi, dr, di = carry
                nr = ar * xr - ai * xi + buf_ref[pl.ds(0, NSEG), re_sl]
                ni = ar * xi + ai * xr + buf_ref[pl.ds(0, NSEG), im_sl]
                buf_ref[pl.ds(0, NSEG), re_sl] = nr
                buf_ref[pl.ds(0, NSEG), im_sl] = ni
                at_start = i == nch - 1
                xpr = jnp.where(at_start, init_ref[:, re_sl], xsh_ref[:, re_sl])
                xpi = jnp.where(at_start, init_ref[:, im_sl], xsh_ref[:, im_sl])
                aux_ref[:, re_sl] += dr + nr * xpr + ni * xpi
                aux_ref[:, im_sl] += di + ni * xpr - nr * xpi
                carry = (nr, ni)
            else:
                carry = lax.fori_loop(0, tk, step, carry)
            st_ref[:, re_sl] = carry[0]
            st_ref[:, im_sl] = carry[1]

        if first_pass:
            @pl.when(i == nch - 1)
            def _():
                e_ref[...] = st_ref[...]
        else:
            st_out_ref[...] = buf_ref[...].astype(st_out_ref.dtype)

    chunk = (lambda i: (nch - 1 - i, 0)) if reverse else (lambda i: (i, 0))
    whole = lambda i: (0, 0)
    ins = [src, wmat, ab]
    in_specs = [pl.BlockSpec((rows, SW), chunk), pl.BlockSpec((SW, 2 * NS), whole), pl.BlockSpec((1, 2 * NS), whole)]
    small = SDS((NSEG, 2 * NS), F32)
    small_spec = pl.BlockSpec((NSEG, 2 * NS), whole)
    if not first_pass:
        ins.append(ends)
        in_specs.append(small_spec)
    if with_dab:
        ins += [xs, xs, init]
        in_specs += [pl.BlockSpec((rows, 2 * NS), chunk),
                     pl.BlockSpec((NSEG, 2 * NS), lambda i: (jnp.maximum((nch - 1 - i) * tk - 1, 0), 0)),
                     small_spec]
    if first_pass:
        out_shape, out_specs = small, small_spec
    else:
        out_shape = (SDS((L, 2 * NS), BF16 if reverse else F32), small)
        out_specs = (pl.BlockSpec((rows, 2 * NS), chunk), small_spec)
    return pl.pallas_call(
        body, out_shape=out_shape, grid=(nch,), in_specs=in_specs, out_specs=out_specs,
        scratch_shapes=[pltpu.VMEM((rows, 2 * NS), F32), pltpu.VMEM((NSEG, 2 * NS), F32)], name=name,
        compiler_params=_cparams(("arbitrary",), 56),
    )(*ins)


def _to_segments(a):
    L, c = a.shape
    return a.reshape(NSEG, L // NSEG, c).transpose(1, 0, 2).reshape(L, c)


def _from_segments(a):
    L, c = a.shape
    return a.reshape(L // NSEG, NSEG, c).transpose(1, 0, 2).reshape(L, c)


def _peer(x, y, c, m):
    return ((1 - x) if (m >> 2) & 1 else x, (1 - y) if (m >> 1) & 1 else y, (1 - c) if m & 1 else c)


def _dev_index(p):
    return 4 * p[0] + 2 * p[1] + p[2]


def _exchange(arrs, scatter, name):
    n = len(arrs)

    def body(*refs):
        ins, outs = refs[:n], refs[n:2 * n]
        send_sems, recv_sems, loc_sems = refs[2 * n:]
        x, y, c = lax.axis_index("x"), lax.axis_index("y"), lax.axis_index("c")
        me = _dev_index((x, y, c))

        def src(w, to):
            return ins[w].at[to] if scatter else ins[w]

        def local(w):
            return pltpu.make_async_copy(src(w, me), outs[w].at[me], loc_sems.at[w])

        def remote(w, m):
            peer = _peer(x, y, c, m)
            return pltpu.make_async_remote_copy(src_ref=src(w, _dev_index(peer)), dst_ref=outs[w].at[me],
                                                send_sem=send_sems.at[w, m - 1], recv_sem=recv_sems.at[w, m - 1],
                                                device_id=peer, device_id_type=pl.DeviceIdType.MESH)

        def arrival(w, m):
            peer = _peer(x, y, c, m)
            return pltpu.make_async_remote_copy(src_ref=src(w, me), dst_ref=outs[w].at[_dev_index(peer)],
                                                send_sem=send_sems.at[w, m - 1], recv_sem=recv_sems.at[w, m - 1],
                                                device_id=peer, device_id_type=pl.DeviceIdType.MESH)

        for w in range(n):
            local(w).start()
        for w in range(n):
            for m in range(1, N_DEV):
                remote(w, m).start()
        for w in range(n):
            for m in range(1, N_DEV):
                arrival(w, m).wait_recv()
        for w in range(n):
            for m in range(1, N_DEV):
                remote(w, m).wait_send()
        for w in range(n):
            local(w).wait()

    anyspec = pl.BlockSpec(memory_space=pl.ANY)
    out_shape = tuple(SDS(a.shape if scatter else (N_DEV,) + a.shape, a.dtype) for a in arrs)
    return pl.pallas_call(
        body, out_shape=out_shape, in_specs=[anyspec] * n, out_specs=tuple([anyspec] * n),
        scratch_shapes=[pltpu.SemaphoreType.DMA((n, N_DEV - 1)), pltpu.SemaphoreType.DMA((n, N_DEV - 1)),
                        pltpu.SemaphoreType.DMA((n,))],
        name=name, compiler_params=pltpu.CompilerParams(has_side_effects=True),
    )(*arrs)


_HBM = pl.BlockSpec(memory_space=pltpu.HBM)
_SEM = pl.BlockSpec(memory_space=pltpu.SEMAPHORE)
_EFFECT = pltpu.SideEffectType.DATAFLOW_SIDE_EFFECTING


def _sem_index(w, m):
    return w * (N_DEV - 1) + m - 1


def _exchange_start(arrs, scatter, name):
    n = len(arrs)
    lands = [lax.empty(a.shape if scatter else (N_DEV,) + a.shape, a.dtype) for a in arrs]

    def body(*refs):
        ins, zones = refs[:n], refs[n:2 * n]
        send_sems, recv_sems = refs[2 * n], refs[2 * n + 1]
        token = refs[-1]
        x, y, c = lax.axis_index("x"), lax.axis_index("y"), lax.axis_index("c")
        me = _dev_index((x, y, c))
        for w in range(n):
            for m in range(1, N_DEV):
                peer = _peer(x, y, c, m)
                pltpu.make_async_remote_copy(
                    src_ref=ins[w].at[_dev_index(peer)] if scatter else ins[w], dst_ref=zones[w].at[me],
                    send_sem=send_sems.at[_sem_index(w, m)], recv_sem=recv_sems.at[_sem_index(w, m)],
                    device_id=peer, device_id_type=pl.DeviceIdType.MESH).start()
        token[...] = jnp.zeros_like(token)

    sems = pltpu.SemaphoreType.DMA((n * (N_DEV - 1),))
    res = pl.pallas_call(
        body, name=name,
        out_shape=(sems, sems, *[pltpu.HBM(a.shape, a.dtype) for a in arrs], *[pltpu.HBM(z.shape, z.dtype) for z in lands],
                   SDS((8, 128), F32)),
        in_specs=[_HBM] * (2 * n), out_specs=(_SEM, _SEM, *([_HBM] * (2 * n)), pl.BlockSpec(memory_space=pltpu.VMEM)),
        input_output_aliases={i: 2 + i for i in range(2 * n)},
        compiler_params=pltpu.CompilerParams(has_side_effects=_EFFECT),
    )(*[pltpu.with_memory_space_constraint(a, pltpu.HBM) for a in arrs],
      *[pltpu.with_memory_space_constraint(z, pltpu.HBM) for z in lands])
    return (res[0], res[1], list(res[2:2 + n]), list(res[2 + n:2 + 2 * n])), res[-1]


def _exchange_wait(handle, after, scatter, name):
    send_sems, recv_sems, thru, lands = handle
    n = len(thru)

    def body(*refs):
        ins, zones = refs[:n], refs[n:2 * n]
        send_sems, recv_sems = refs[2 * n], refs[2 * n + 1]
        x, y, c = lax.axis_index("x"), lax.axis_index("y"), lax.axis_index("c")
        me = _dev_index((x, y, c))
        for w in range(n):
            for m in range(1, N_DEV):
                peer = _peer(x, y, c, m)
                copy = pltpu.make_async_remote_copy(
                    src_ref=ins[w].at[me] if scatter else ins[w], dst_ref=zones[w].at[_dev_index(peer)],
                    send_sem=send_sems.at[_sem_index(w, m)], recv_sem=recv_sems.at[_sem_index(w, m)],
                    device_id=peer, device_id_type=pl.DeviceIdType.MESH)
                copy.wait_send()
                copy.wait_recv()

    res = pl.pallas_call(
        body, name=name,
        out_shape=(*[pltpu.HBM(a.shape, a.dtype) for a in thru], *[pltpu.HBM(z.shape, z.dtype) for z in lands]),
        in_specs=[_HBM] * (2 * n) + [_SEM, _SEM, pl.BlockSpec(memory_space=pl.ANY)], out_specs=tuple([_HBM] * (2 * n)),
        input_output_aliases={i: i for i in range(2 * n)},
        compiler_params=pltpu.CompilerParams(has_side_effects=_EFFECT),
    )(*thru, *lands, send_sems, recv_sems, after)
    return list(res[:n]), list(res[n:])


def _adam_math(g, w, m, v):
    m = ADAM_B1 * m + (1.0 - ADAM_B1) * g
    v = ADAM_B2 * v + (1.0 - ADAM_B2) * (g * g)
    m_hat = m / (1.0 - ADAM_B1 ** ADAM_STEP)
    v_hat = v / (1.0 - ADAM_B2 ** ADAM_STEP)
    delta = -ADAM_LR * (m_hat / (jnp.sqrt(v_hat) + ADAM_EPS) + ADAM_WD * w)
    return delta, m, v


def _adam(parts, w, m, v, name, tr=128):
    r, c = w.shape
    tr = next(t for t in (tr, 64, 32, 16, 8) if r % t == 0)

    def fn(cc, rr, pb, wb, mb, vb):
        g = pb[0].astype(F32)
        for d in range(1, N_DEV):
            g = g + pb[d].astype(F32)
        delta, nm, nv = _adam_math(g, wb, mb, vb)
        return g, delta, nm, nv

    blk = ((tr, c), lambda cc, rr: (rr, 0))
    o = SDS((r, c), F32)
    return _ew(fn, [(parts, (N_DEV, tr, c), lambda cc, rr: (0, rr, 0)), (w, *blk), (m, *blk), (v, *blk)],
               [(o, *blk, None)] * 4, (1, r // tr), name)


_SHARDED = ("w_in", "w_glu", "w_branch_attn", "w_branch_ssm", "w_out", "w_up", "w_down")
_COL_SHARDED = ("w_in", "w_glu", "w_branch_attn", "w_branch_ssm", "w_up")
_GROUPS = {"a": ("w_in",), "b": ("w_glu", "w_branch_attn", "w_branch_ssm", "w_out"), "c": ("w_up", "w_down")}
_SMALL = ("attn_norm_g", "b_in", "attn_sinks", "ssm_a_re", "ssm_a_im", "ssm_log_dt", "ssm_b_re", "ssm_b_im",
          "ssm_c_re", "ssm_c_im", "ssm_d", "b_glu", "ffn_norm_g", "conv_w", "conv_b", "final_norm_g")
_WEIGHTS = ("attn_norm_g", "w_in", "b_in", "attn_sinks", "ssm_a_re", "ssm_a_im", "ssm_log_dt", "ssm_b_re", "ssm_b_im",
            "ssm_c_re", "ssm_c_im", "ssm_d", "w_glu", "b_glu", "w_branch_attn", "w_branch_ssm", "w_out", "ffn_norm_g",
            "w_up", "conv_w", "conv_b", "w_down", "final_norm_g")


def _unstack_cols(g):
    return g.transpose(1, 0, 2).reshape(g.shape[1], N_DEV * g.shape[2])


def _stack_cols(a):
    k, n = a.shape
    return a.reshape(k, N_DEV, n // N_DEV).transpose(1, 0, 2)


def _pack(arrs):
    flat = jnp.concatenate([a.reshape(-1) for a in arrs])
    pad = (-flat.shape[0]) % 1024
    return jnp.pad(flat, (0, pad)).reshape(-1, 128)


def _local_step(x, tgt, wget, small, gput):
    L = x.shape[0]
    nr = lambda tm: L // tm

    h = _rmsnorm_fwd(x, small["attn_norm_g"], "norm1")
    wts = dict(wget("a", h))
    projb = _mm(h, wts["w_in"], bias=small["b_in_p"], out_dtype=BF16, name="proj")
    proj = projb
    attn = _attn_fwd(projb, small["attn_sinks"], "attn_fwd")

    ab, bmat, cmat = _ssm_prep(small["a_re"], small["a_im"], small["logdt"], small["b_re"], small["b_im"],
                               small["c_re"], small["c_im"], "ssm_prep")
    u_seg = _to_segments(proj[:, C_U:C_GA])
    ends_f = _ssm_scan(u_seg, bmat, ab, reverse=False, name="ssm_ends_fwd")
    xs, init_f = _ssm_scan(u_seg, bmat, ab, reverse=False, ends=ends_f, name="ssm_scan_fwd")
    y_mm = _from_segments(_mm(xs, cmat, tb=True, tm=512, tk=1024, name="ssm_out"))

    def gelu_fn(c, r, yb, ub, db):
        yv = yb + db * ub
        return yv, _gelu(yv)

    tm = 512
    y, gy = _ew(gelu_fn, [(y_mm, *_rc(tm, 256)), (proj, *_rc(tm, 256, C_U // 256)), (small["ssm_d"], *_col(1, 256))],
                [(SDS((L, SW), F32), *_rc(tm, 256), None), (SDS((L, SW), BF16), *_rc(tm, 256), None)],
                (2, nr(tm)), "ssm_gelu")
    wts.update(wget("b", gy))
    glu = _mm(gy, wts["w_glu"], bias=small["b_glu"], name="glu")

    def glu_fn(c, r, vb, gb):
        return (vb * _sigmoid(gb),)

    (ssm,) = _ew(glu_fn, [(glu, *_rc(tm, SW)), (glu, *_rc(tm, SW, 1))], [(SDS((L, SW), BF16), *_rc(tm, SW), None)],
                 (1, nr(tm)), "glu_gate")
    br_a = _mm(attn, wts["w_branch_attn"], out_dtype=BF16, name="branch_attn")
    br_s = _mm(ssm, wts["w_branch_ssm"], out_dtype=BF16, name="branch_ssm")

    def merge_fn(c, r, ab_, sb_, ga, gs):
        return (_sigmoid(ga) * ab_ + _sigmoid(gs) * sb_,)

    (merged,) = _ew(merge_fn, [(br_a, *_rc(tm, 256)), (br_s, *_rc(tm, 256)), (proj, *_rc(tm, 256, C_GA // 256)),
                               (proj, *_rc(tm, 256, C_GS // 256))],
                    [(SDS((L, D), BF16), *_rc(tm, 256), None)], (D // 256, nr(tm)), "merge")
    x1 = _mm(merged, wts["w_out"], res=x, name="out_proj")
    h2 = _rmsnorm_fwd(x1, small["ffn_norm_g"], "norm2")
    wts.update(wget("c", h2))
    conv_w = wts["conv_w"]
    up = _mm(h2, wts["w_up"], out_dtype=BF16, name="ffn_up")
    tcf = 1408
    ncf = DFF // tcf
    tma = 256
    hb = 16

    def conv_gate(r, gate, halo, cw, cb):
        halo = halo[hb - 8:] * (r > 0).astype(F32)
        g1, g2 = _shift_rows(gate, halo, 1), _shift_rows(gate, halo, 2)
        return cb + cw[2:3] * gate + cw[1:2] * g1 + cw[0:1] * g2, g1, g2

    gate_specs = [(up, *_rc(tma, tcf, ncf)),
                  (up, (hb, tcf), lambda c, r: (jnp.maximum(r * (tma // hb) - 1, 0), c + ncf)),
                  (conv_w, *_col(3, tcf)), (small["conv_b"], *_col(1, tcf))]

    def act_fn(c, r, val, gate, halo, cw, cb):
        return (val * _gelu(conv_gate(r, gate, halo, cw, cb)[0]),)

    (act,) = _ew(act_fn, [(up, *_rc(tma, tcf))] + gate_specs, [(SDS((L, DFF), BF16), *_rc(tma, tcf), None)],
                 (ncf, nr(tma)), "ffn_act")
    x2 = _mm(act, wts["w_down"], res=x1, name="ffn_down")
    d_x2, loss_cols, d_gf = _final_loss(x2, small["final_norm_g"], tgt, "final_loss")
    loss = jnp.sum(loss_cols)

    d_act = _mm(d_x2, wts["w_down"], tb=True, tn=tcf, out_dtype=BF16, name="d_act")
    dw_down = _mm(act, d_x2, ta=True, out_dtype=BF16, tm=tcf, tk=2048, name="dw_down")

    def act_bwd(c, r, da, val, gate, halo, cw, cb):
        cg, g1, g2 = conv_gate(r, gate, halo, cw, cb)
        gl, glg = _gelu_and_grad(cg)
        d_cg = da * val * glg
        row3 = lax.broadcasted_iota(jnp.int32, (3, tcf), 0)
        s0 = jnp.sum(d_cg * g2, axis=0, keepdims=True)
        s1 = jnp.sum(d_cg * g1, axis=0, keepdims=True)
        s2 = jnp.sum(d_cg * gate, axis=0, keepdims=True)
        dcw = jnp.where(row3 == 0, s0, jnp.where(row3 == 1, s1, s2))
        return da * gl, d_cg, dcw, jnp.sum(d_cg, axis=0, keepdims=True)

    d_val, d_cg, d_conv_w, d_conv_b = _ew(
        act_bwd, [(d_act, *_rc(tma, tcf)), (up, *_rc(tma, tcf))] + gate_specs,
        [(SDS((L, DFF), BF16), *_rc(tma, tcf), None), (SDS((L, DFF), BF16), *_rc(tma, tcf), None),
         (SDS((3, DFF), F32), *_col(3, tcf), "r"), (SDS((1, DFF), F32), *_col(1, tcf), "r")],
        (ncf, nr(tma)), "ffn_act_bwd")

    def gate_bwd(c, r, dcg, halo, cw):
        halo = halo[:8] * (r < nr(tma) - 1).astype(F32)
        return (cw[2:3] * dcg + cw[1:2] * _shift_rows_up(dcg, halo, 1) + cw[0:1] * _shift_rows_up(dcg, halo, 2),)

    (d_gate,) = _ew(gate_bwd, [(d_cg, *_rc(tma, tcf)),
                               (d_cg, (hb, tcf), lambda c, r: (jnp.minimum((r + 1) * (tma // hb), L // hb - 1), c)),
                               (conv_w, *_col(3, tcf))],
                    [(SDS((L, DFF), BF16), *_rc(tma, tcf), None)], (ncf, nr(tma)), "ffn_gate_bwd")
    d_h2 = _mm(d_val, wts["w_up"][:, :DFF], tb=True, name="d_h2_val")
    d_h2 = _mm(d_gate, wts["w_up"][:, DFF:], tb=True, res=d_h2, name="d_h2_gate")
    dw_up_v = _mm(h2, d_val, ta=True, out_dtype=BF16, tn=tcf, tk=2048, name="dw_up_val")
    dw_up_g = _mm(h2, d_gate, ta=True, out_dtype=BF16, tn=tcf, tk=2048, name="dw_up_gate")
    tok = gput("c", {"w_up": jnp.concatenate([dw_up_v, dw_up_g], axis=1), "w_down": dw_down})
    d_x1, d_g2 = _rmsnorm_bwd(d_h2, x1, small["ffn_norm_g"] + tok[0, 0], d_x2, "norm2_bwd")

    d_merged = _mm(d_x1, wts["w_out"], tb=True, out_dtype=BF16, name="d_merged")
    dw_out = _mm(merged, d_x1, ta=True, out_dtype=BF16, name="dw_out")

    def merge_bwd(c, r, dm, ab_, sb_, ga, gs):
        sa, ss = _sigmoid(ga), _sigmoid(gs)
        return dm * sa, dm * ss, dm * ab_ * (sa * (1.0 - sa)), dm * sb_ * (ss * (1.0 - ss))

    ob = SDS((L, D), BF16)
    d_bra, d_brs, d_ga, d_gs = _ew(
        merge_bwd, [(d_merged, *_rc(tm, 256)), (br_a, *_rc(tm, 256)), (br_s, *_rc(tm, 256)),
                    (proj, *_rc(tm, 256, C_GA // 256)), (proj, *_rc(tm, 256, C_GS // 256))],
        [(ob, *_rc(tm, 256), None)] * 4, (D // 256, nr(tm)), "merge_bwd")
    d_attn = _mm(d_bra, wts["w_branch_attn"], tb=True, out_dtype=BF16, name="d_attn")
    dw_ba = _mm(attn, d_bra, ta=True, out_dtype=BF16, name="dw_branch_attn")
    d_ssm = _mm(d_brs, wts["w_branch_ssm"], tb=True, name="d_ssm")
    dw_bs = _mm(ssm, d_brs, ta=True, out_dtype=BF16, name="dw_branch_ssm")
    dq, dkv_cur, dkv_prev, d_sinks = _attn_bwd(projb, small["attn_sinks"], d_attn, "attn_bwd")

    def glu_bwd(c, r, ds, vb, gb):
        sg = _sigmoid(gb)
        return ds * sg, ds * vb * (sg * (1.0 - sg))

    d_glu_v, d_glu_g = _ew(glu_bwd, [(d_ssm, *_rc(tm, SW)), (glu, *_rc(tm, SW)), (glu, *_rc(tm, SW, 1))],
                           [(SDS((L, SW), F32), *_rc(tm, SW), None)] * 2, (1, nr(tm)), "glu_gate_bwd")
    d_glu = jnp.concatenate([d_glu_v, d_glu_g], axis=1)
    d_gy = _mm(d_glu, wts["w_glu"], tb=True, name="d_gelu_y")
    dw_glu = _mm(gy, d_glu, ta=True, out_dtype=BF16, name="dw_glu")

    tok = gput("b", {"w_glu": dw_glu, "w_branch_attn": dw_ba, "w_branch_ssm": dw_bs, "w_out": dw_out})
    ab = ab + tok[0, 0]

    def gelu_bwd(c, r, dg, yb, ub, dgl):
        dy = dg * _gelu_grad(yb)
        return dy, jnp.sum(dy * ub, axis=0, keepdims=True), jnp.sum(dgl, axis=0, keepdims=True)

    dy, d_ssm_d, d_b_glu = _ew(
        gelu_bwd, [(d_gy, *_rc(tm, 256)), (y, *_rc(tm, 256)), (proj, *_rc(tm, 256, C_U // 256)), (d_glu, *_rc(tm, 512))],
        [(SDS((L, SW), F32), *_rc(tm, 256), None), (SDS((1, SW), F32), *_col(1, 256), "r"),
         (SDS((1, 2 * SW), F32), *_col(1, 512), "r")], (2, nr(tm)), "ssm_gelu_bwd")

    dy_seg = _to_segments(dy)
    ends_r = _ssm_scan(dy_seg, cmat, ab, reverse=True, name="ssm_ends_bwd")
    lam, dab8 = _ssm_scan(dy_seg, cmat, ab, reverse=True, ends=ends_r, xs=xs, init=init_f, name="ssm_scan_bwd")
    du_mm = _from_segments(_mm(lam, bmat, tb=True, tm=512, tk=1024, name="ssm_du"))
    dbm = _mm(u_seg, lam, ta=True, tm=512, name="ssm_dbmat")
    dcm = _mm(dy_seg, xs, ta=True, tm=512, name="ssm_dcmat")
    d_are, d_aim, d_ldt, d_bre, d_bim, d_cre, d_cim = _ssm_param_bwd(
        small["a_re"], small["a_im"], small["logdt"], small["b_re"], small["b_im"], dab8, dbm, dcm, "ssm_param_bwd")

    nb = L // BLK

    def dproj_fn(c, r, dqb, cur, prv, du, dyb, dsk, dga, dgs):
        dkv = cur + prv * (r < nb - 1).astype(F32)
        dub = du + dsk * dyb
        full = jnp.concatenate([dqb.astype(F32), dkv, dub, dga.astype(F32), dgs.astype(F32),
                                jnp.zeros((BLK, INP - INC), F32)], axis=1)
        return full, jnp.sum(full, axis=0, keepdims=True)

    rowb = lambda w: ((BLK, w), lambda c, r: (r, 0))
    dproj, d_b_in = _ew(
        dproj_fn, [(dq, *rowb(AW)), (dkv_cur, *rowb(256)),
                   (dkv_prev, (BLK, 256), lambda c, r: (jnp.minimum(r + 1, nb - 1), 0)),
                   (du_mm, *rowb(SW)), (dy, *rowb(SW)), (small["ssm_d"], *_col(1, SW)), (d_ga, *rowb(D)), (d_gs, *rowb(D))],
        [(SDS((L, INP), BF16), *rowb(INP), None), (SDS((1, INP), F32), *_col(1, INP), "all")], (1, nb), "dproj")
    dw_in = _mm(h, dproj, ta=True, out_dtype=BF16, name="dw_in")
    tok = gput("a", {"w_in": dw_in[:, :INC]})
    d_h = _mm(dproj, wts["w_in"], tb=True, bias=jnp.zeros((1, D), F32) + tok[0, 0], name="d_h")
    grad_x, d_g1 = _rmsnorm_bwd(d_h, x, small["attn_norm_g"], d_x1, "norm1_bwd")

    sgrads = {"attn_norm_g": d_g1, "b_in": d_b_in[:, :INC], "attn_sinks": d_sinks[:, :NQ], "a_re": d_are, "a_im": d_aim,
              "logdt": d_ldt, "b_re": d_bre, "b_im": d_bim, "c_re": d_cre, "c_im": d_cim, "ssm_d": d_ssm_d,
              "b_glu": d_b_glu, "ffn_norm_g": d_g2, "conv_w": d_conv_w, "conv_b": d_conv_b, "final_norm_g": d_gf}
    return loss, grad_x, sgrads


def _small_layouts(p):
    gp = lambda a: a.reshape(1, NS)
    hgp = lambda a: a.transpose(2, 0, 1).reshape(H, NS)
    chgp = lambda a: a.transpose(1, 0, 2).reshape(H, NS)
    return {
        "attn_norm_g": p["attn_norm_g"].reshape(1, D), "ffn_norm_g": p["ffn_norm_g"].reshape(1, D),
        "final_norm_g": p["final_norm_g"].reshape(1, D),
        "b_in_p": jnp.pad(p["b_in"].reshape(1, INC), ((0, 0), (0, INP - INC))),
        "attn_sinks": p["attn_sinks"].reshape(1, NQ),
        "a_re": gp(p["ssm_a_re"]), "a_im": gp(p["ssm_a_im"]), "logdt": jnp.repeat(p["ssm_log_dt"], P).reshape(1, NS),
        "b_re": hgp(p["ssm_b_re"]), "b_im": hgp(p["ssm_b_im"]), "c_re": chgp(p["ssm_c_re"]), "c_im": chgp(p["ssm_c_im"]),
        "ssm_d": p["ssm_d"].reshape(1, SW), "b_glu": p["b_glu"].reshape(1, 2 * SW),
        "conv_b": p["conv_b"].reshape(1, DFF),
    }


def _small_grads_to_param_shapes(sg):
    from_hgp = lambda a: a.reshape(H, G, P).transpose(1, 2, 0)
    from_chgp = lambda a: a.reshape(H, G, P).transpose(1, 0, 2)
    return {
        "attn_norm_g": sg["attn_norm_g"].reshape(D), "b_in": sg["b_in"].reshape(INC),
        "attn_sinks": sg["attn_sinks"].reshape(NQ),
        "ssm_a_re": sg["a_re"].reshape(G, P), "ssm_a_im": sg["a_im"].reshape(G, P),
        "ssm_log_dt": jnp.sum(sg["logdt"].reshape(G, P), axis=1),
        "ssm_b_re": from_hgp(sg["b_re"]), "ssm_b_im": from_hgp(sg["b_im"]),
        "ssm_c_re": from_chgp(sg["c_re"]), "ssm_c_im": from_chgp(sg["c_im"]),
        "ssm_d": sg["ssm_d"].reshape(SW), "b_glu": sg["b_glu"].reshape(2 * SW),
        "ffn_norm_g": sg["ffn_norm_g"].reshape(D), "conv_w": sg["conv_w"], "conv_b": sg["conv_b"].reshape(DFF),
        "final_norm_g": sg["final_norm_g"].reshape(D),
    }


def kernel(x, attn_norm_g, w_in, b_in, attn_sinks, ssm_a_re, ssm_a_im, ssm_log_dt, ssm_b_re, ssm_b_im, ssm_c_re, ssm_c_im, ssm_d, w_glu, b_glu, w_branch_attn, w_branch_ssm, w_out, ffn_norm_g, w_up, conv_w, conv_b, w_down, final_norm_g, loss_target, m_attn_norm_g, m_w_in, m_b_in, m_attn_sinks, m_ssm_a_re, m_ssm_a_im, m_ssm_log_dt, m_ssm_b_re, m_ssm_b_im, m_ssm_c_re, m_ssm_c_im, m_ssm_d, m_w_glu, m_b_glu, m_w_branch_attn, m_w_branch_ssm, m_w_out, m_ffn_norm_g, m_w_up, m_conv_w, m_conv_b, m_w_down, m_final_norm_g, v_attn_norm_g, v_w_in, v_b_in, v_attn_sinks, v_ssm_a_re, v_ssm_a_im, v_ssm_log_dt, v_ssm_b_re, v_ssm_b_im, v_ssm_c_re, v_ssm_c_im, v_ssm_d, v_w_glu, v_b_glu, v_w_branch_attn, v_w_branch_ssm, v_w_out, v_ffn_norm_g, v_w_up, v_conv_w, v_conv_b, v_w_down, v_final_norm_g):
    args = dict(locals())
    sq = lambda a: a if a.ndim == 1 else a[0]
    wv = {n: sq(args[n]) for n in _WEIGHTS}
    mv = {n: sq(args["m_" + n]) for n in _WEIGHTS}
    vv = {n: sq(args["v_" + n]) for n in _WEIGHTS}
    me = 4 * lax.axis_index("x") + 2 * lax.axis_index("y") + lax.axis_index("c")

    gather, tok = {}, jnp.zeros((8, 128), F32)
    for grp in ("a", "b", "c"):
        shards = [(wv[n] + tok[0, 0]).astype(BF16) for n in _GROUPS[grp]]
        if grp == "c":
            shards.append(jnp.pad(wv["conv_w"] + tok[0, 0], ((0, 5), (0, 64))))
        gather[grp], tok = _exchange_start(shards, False, "gather_start_" + grp)
    small = _small_layouts(wv)
    small["attn_norm_g"] = small["attn_norm_g"] + tok[0, 0]

    def own_slot(land, src):
        return lax.dynamic_update_slice_in_dim(land, src, me, axis=0)

    def wget(grp, after):
        thru, lands = _exchange_wait(gather[grp], after, False, "gather_wait_" + grp)
        full = {}
        for n, t, g in zip(_GROUPS[grp], thru, lands):
            g = own_slot(g, t[None])
            full[n] = _unstack_cols(g) if n in _COL_SHARDED else g.reshape(N_DEV * g.shape[1], g.shape[2])
        if grp == "a":
            full["w_in"] = jnp.pad(full["w_in"], ((0, 0), (0, INP - INC)))
        if grp == "c":
            full["conv_w"] = _unstack_cols(own_slot(lands[-1], thru[-1][None])[:, :3, :DFF // N_DEV])
        return full

    scatter = {}

    def gput(grp, grads):
        stacked = [_stack_cols(grads[n]) if n in _COL_SHARDED else grads[n].reshape(N_DEV, -1, D) for n in _GROUPS[grp]]
        scatter[grp], token = _exchange_start(stacked, True, "scatter_start_" + grp)
        return token

    loss, grad_x, sg = _local_step(x[0], loss_target[0], wget, small, gput)
    loss = lax.psum(loss, MESH_AXES)

    sgp = _small_grads_to_param_shapes(sg)
    small_names = [n for n in _SMALL]
    packed_g = _pack([sgp[n] for n in small_names])
    (small_all,) = _exchange([packed_g], False, "gather_small_grads")

    outs_g, outs_d, outs_m, outs_v = {}, {}, {}, {}
    for grp in ("c", "b", "a"):
        thru, lands = _exchange_wait(scatter[grp], small_all, True, "scatter_wait_" + grp)
        for n, t, pt in zip(_GROUPS[grp], thru, lands):
            pt = own_slot(pt, lax.dynamic_slice_in_dim(t, me, 1, axis=0))
            outs_g[n], outs_d[n], outs_m[n], outs_v[n] = _adam(pt, wv[n], mv[n], vv[n], "adam_" + n)

    sizes = [int(math.prod(sgp[n].shape)) for n in small_names]
    offs = [0]
    for s in sizes:
        offs.append(offs[-1] + s)

    def local_part(n, a):
        if n == "conv_w":
            return lax.dynamic_slice(a, (0, me * (DFF // N_DEV)), (3, DFF // N_DEV))
        return a

    rows = packed_g.shape[0]

    def sum_fn(cc, rr, pb):
        g = pb[0]
        for d in range(1, N_DEV):
            g = g + pb[d]
        return (g,)

    (gsum,) = _ew(sum_fn, [(small_all, (N_DEV, rows, 128), lambda cc, rr: (0, 0, 0))],
                  [(SDS((rows, 128), F32), (rows, 128), lambda cc, rr: (0, 0), None)], (1, 1), "sum_small_grads")
    gflat = gsum.reshape(-1)
    gsmall = {n: local_part(n, gflat[offs[i]:offs[i + 1]].reshape(sgp[n].shape)) for i, n in enumerate(small_names)}
    pw = _pack([wv[n] for n in small_names])
    pm = _pack([mv[n] for n in small_names])
    pv = _pack([vv[n] for n in small_names])
    pg = _pack([gsmall[n] for n in small_names])
    prow = pw.shape[0]

    def adam_small(cc, rr, gb, wb, mb, vb):
        return _adam_math(gb, wb, mb, vb)

    whole = ((prow, 128), lambda cc, rr: (0, 0))
    sd, sm, sv = _ew(adam_small, [(pg, *whole), (pw, *whole), (pm, *whole), (pv, *whole)],
                     [(SDS((prow, 128), F32), *whole, None)] * 3, (1, 1), "adam_small")
    lsizes = [int(math.prod(wv[n].shape)) for n in small_names]
    loffs = [0]
    for s in lsizes:
        loffs.append(loffs[-1] + s)
    for i, n in enumerate(small_names):
        take = lambda a: a.reshape(-1)[loffs[i]:loffs[i + 1]].reshape(wv[n].shape)
        outs_g[n], outs_d[n], outs_m[n], outs_v[n] = gsmall[n], take(sd), take(sm), take(sv)

    lead = lambda n, a: a if args[n].ndim == 1 else a[None]
    grad_x = grad_x[None]
    return (loss, grad_x, *[lead(n, outs_g[n]) for n in _WEIGHTS], *[lead(n, outs_d[n]) for n in _WEIGHTS],
            *[lead(n, outs_m[n]) for n in _WEIGHTS], *[lead(n, outs_v[n]) for n in _WEIGHTS])
```

```python
import functools
import math

import jax
import jax.numpy as jnp
from jax import lax
from jax.experimental import pallas as pl
from jax.experimental.pallas import tpu as pltpu

F32 = jnp.float32
BF16 = jnp.bfloat16
SDS = jax.ShapeDtypeStruct

N_DEV = 8
D = 2048
NQ, NKV, HD = 16, 2, 64
AW = NQ * HD
BLK = 128
SW, G, H, P = 512, 32, 16, 64
NS = G * P
DFF = 5632
INC = AW + 2 * NKV * HD + SW + 2 * D
C_K, C_U, C_PAD = AW, AW + 2 * NKV * HD, AW + 2 * NKV * HD + SW
C_GA, C_GS, INP = D, 2 * D, 3 * D
RMS_EPS = 1e-6
NEG_BIG = -1e30
ADAM_LR, ADAM_B1, ADAM_B2, ADAM_EPS, ADAM_WD, ADAM_STEP = 0.001, 0.9, 0.999, 1e-08, 0.01, 10
NSEG = 8
VMEM_CAP_MB = 60
MESH_AXES = ("x", "y", "c")


def _pad_cols(a):
    zeros = jnp.zeros(a.shape[:-1] + (C_GA - C_PAD,), a.dtype)
    return jnp.concatenate([a[..., :C_PAD], zeros, a[..., C_PAD:]], axis=-1)


def _unpad_cols(a):
    return jnp.concatenate([a[..., :C_PAD], a[..., C_GA:]], axis=-1)


def _cparams(sem, vmem_mb):
    return pltpu.CompilerParams(dimension_semantics=sem, vmem_limit_bytes=min(int(vmem_mb), VMEM_CAP_MB) << 20)


LANES = 128


def _tile(dim, pref):
    if dim <= pref:
        return dim
    for t in range(pref - pref % LANES, 0, -LANES):
        if dim % t == 0:
            return t
    raise ValueError(f"no tile for {dim}")


def _mm(a, b, *, ta=False, tb=False, bias=None, res=None, out_dtype=F32, tm=1024, tn=1024, tk=3072, name,
        a2=None, b2=None, extras=(), epilogue=None, outs=None):
    m, k = (a.shape[1], a.shape[0]) if ta else a.shape
    n = b.shape[0] if tb else b.shape[1]
    assert (b.shape[1] if tb else b.shape[0]) == k, (a.shape, b.shape, ta, tb)
    tm, tn, tk = _tile(m, tm), _tile(n, tn), _tile(k, tk)
    nk = k // tk
    dims = (((0 if ta else 1,), (1 if tb else 0,)), ((), ()))
    has_bias, has_res, has_b2 = bias is not None, res is not None, b2 is not None
    has_a2 = a2 is not None
    assert not (has_b2 and (nk > 1 or ta or tb)) and not (has_a2 and not has_b2)
    if epilogue is None:
        outs = [(SDS((m, n), out_dtype), "tile")]
    n_ex, n_out = len(extras), len(outs)

    def body(*refs):
        a_ref, b_ref = refs[0], refs[1]
        pos = 2
        a2_ref = refs[pos] if has_a2 else a_ref
        pos += has_a2
        b2_ref = refs[pos] if has_b2 else None
        pos += has_b2
        bias_ref = refs[pos] if has_bias else None
        pos += has_bias
        res_ref = refs[pos] if has_res else None
        pos += has_res
        ex_refs = refs[pos:pos + n_ex]
        o_refs = refs[pos + n_ex:pos + n_ex + n_out]
        i = pl.program_id(1)

        def product(rhs_ref):
            return lax.dot_general(a_ref[...].astype(BF16), rhs_ref[...].astype(BF16), dims, preferred_element_type=F32)

        def finish(r):
            if has_bias:
                r = r + bias_ref[...]
            if has_res:
                r = r + res_ref[...].astype(F32)
            if epilogue is None:
                o_refs[0][...] = r.astype(o_refs[0].dtype)
                return
            r2 = jnp.dot(a2_ref[...].astype(BF16), b2_ref[...].astype(BF16), preferred_element_type=F32) if has_b2 else None
            vals = epilogue(i, r, r2, *ex_refs)
            for o_ref, v, (_, kind) in zip(o_refs, vals, outs):
                if kind == "tile":
                    o_ref[...] = v.astype(o_ref.dtype)
                else:
                    @pl.when(i == 0)
                    def _(o_ref=o_ref, v=v):
                        o_ref[...] = v.astype(o_ref.dtype)

                    @pl.when(i > 0)
                    def _(o_ref=o_ref, v=v):
                        o_ref[...] += v.astype(o_ref.dtype)

        if nk == 1:
            finish(product(b_ref))
            return
        acc_ref = refs[-1]
        kk = pl.program_id(2)

        @pl.when(kk == 0)
        def _():
            acc_ref[...] = product(b_ref)

        @pl.when(jnp.logical_and(kk > 0, kk < nk - 1))
        def _():
            acc_ref[...] += product(b_ref)

        @pl.when(kk == nk - 1)
        def _():
            finish(acc_ref[...] + product(b_ref))

    b_spec = pl.BlockSpec((tn, tk), lambda j, i, kk: (j, kk)) if tb else pl.BlockSpec((tk, tn), lambda j, i, kk: (kk, j))
    ins = [a, b]
    in_specs = [pl.BlockSpec((tk, tm), lambda j, i, kk: (kk, i)) if ta else pl.BlockSpec((tm, tk), lambda j, i, kk: (i, kk)),
                b_spec]
    tile_spec = pl.BlockSpec((tm, tn), lambda j, i, kk: (i, j))
    byt = 2 * tm * tk * a.dtype.itemsize + 2 * tk * tn * b.dtype.itemsize
    byt += (2 + has_b2) * 4 * tm * tn
    if has_a2:
        ins.append(a2)
        in_specs.append(pl.BlockSpec((tm, a2.shape[1]), lambda j, i, kk: (i, 0)))
        byt += 2 * tm * a2.shape[1] * a2.dtype.itemsize
    if has_b2:
        ins.append(b2)
        in_specs.append(pl.BlockSpec((b2.shape[0], tn), lambda j, i, kk: (0, j)))
        byt += 2 * b2.shape[0] * tn * b2.dtype.itemsize
    if has_bias:
        ins.append(bias)
        in_specs.append(pl.BlockSpec((1, tn), lambda j, i, kk: (0, j)))
    if has_res:
        ins.append(res)
        in_specs.append(tile_spec)
        byt += 2 * tm * tn * res.dtype.itemsize
    for arr, kind, arg in extras:
        ins.append(arr)
        if kind == "tile":
            in_specs.append(pl.BlockSpec((tm, tn), lambda j, i, kk, arg=arg: (i, j + arg)))
            byt += 2 * tm * tn * arr.dtype.itemsize + 4 * tm * tn
        elif kind == "col":
            in_specs.append(pl.BlockSpec((arr.shape[0], tn), lambda j, i, kk, arg=arg: (0, j + arg)))
        else:
            in_specs.append(pl.BlockSpec(arg[0], lambda j, i, kk, im=arg[1]: im(j, i)))
    out_specs = []
    for sds, kind in outs:
        if kind == "tile":
            out_specs.append(tile_spec)
            byt += 2 * tm * tn * jnp.dtype(sds.dtype).itemsize
        else:
            out_specs.append(pl.BlockSpec((sds.shape[0], tn), lambda j, i, kk: (0, j)))
    res_ = pl.pallas_call(
        body, out_shape=tuple(o[0] for o in outs), grid=(n // tn, m // tm, nk), in_specs=in_specs,
        out_specs=tuple(out_specs), scratch_shapes=[pltpu.VMEM((tm, tn), F32)] if nk > 1 else [], name=name,
        compiler_params=_cparams(("arbitrary", "arbitrary", "arbitrary"), byt / 2**20 + (8 if epilogue is None else 20)),
    )(*ins)
    return res_[0] if epilogue is None else res_


def _ew(fn, ins, outs, grid, name, vmem_mb=40):
    n_in = len(ins)
    accs = [o[3] for o in outs]

    def body(*refs):
        c, r = pl.program_id(0), pl.program_id(1)
        vals = fn(c, r, *[ref[...].astype(F32) for ref in refs[:n_in]])
        for o_ref, v, acc in zip(refs[n_in:], vals, accs):
            if acc is None:
                o_ref[...] = v.astype(o_ref.dtype)
            else:
                first = (r == 0) if acc == "r" else jnp.logical_and(r == 0, c == 0)

                @pl.when(first)
                def _(o_ref=o_ref, v=v):
                    o_ref[...] = v.astype(o_ref.dtype)

                @pl.when(jnp.logical_not(first))
                def _(o_ref=o_ref, v=v):
                    o_ref[...] += v.astype(o_ref.dtype)

    res = pl.pallas_call(
        body, out_shape=tuple(o[0] for o in outs), grid=grid,
        in_specs=[pl.BlockSpec(bs, im) for _, bs, im in ins],
        out_specs=tuple(pl.BlockSpec(bs, im) for _, bs, im, _ in outs), name=name,
        compiler_params=_cparams(("arbitrary", "arbitrary"), vmem_mb),
    )(*[a for a, _, _ in ins])
    return res


def _rc(tm, tc, coff=0):
    return (tm, tc), (lambda c, r: (r, c + coff))


def _col(rows, tc, coff=0):
    return (rows, tc), (lambda c, r: (0, c + coff))


def _gelu(x):
    return 0.5 * x * (1.0 + lax.erf(x * (2.0 ** -0.5)))


def _gelu_and_grad(x):
    cdf = 0.5 * (1.0 + lax.erf(x * (2.0 ** -0.5)))
    return x * cdf, cdf + x * jnp.exp(-0.5 * x * x) * (1.0 / math.sqrt(2.0 * math.pi))


def _gelu_grad(x):
    return _gelu_and_grad(x)[1]


def _sigmoid(x):
    return 1.0 / (1.0 + jnp.exp(-x))


def _shift_rows(x, halo, s):
    rolled = pltpu.roll(x, s, 0)
    row8 = lax.broadcasted_iota(jnp.int32, halo.shape, 0)
    head = jnp.where(row8 < s, pltpu.roll(halo, s, 0), rolled[0:8])
    return jnp.concatenate([head, rolled[8:]], axis=0)


def _shift_rows_up(x, halo, s):
    tm = x.shape[0]
    rolled = pltpu.roll(x, tm - s, 0)
    row8 = lax.broadcasted_iota(jnp.int32, halo.shape, 0)
    tail = jnp.where(row8 >= 8 - s, pltpu.roll(halo, 8 - s, 0), rolled[tm - 8:])
    return jnp.concatenate([rolled[:tm - 8], tail], axis=0)


def _rmsnorm_fwd(x, g, name, tm=256):
    L = x.shape[0]

    def fn(c, r, xb, gb):
        rstd = lax.rsqrt(jnp.mean(xb * xb, axis=-1, keepdims=True) + RMS_EPS)
        return ((xb * rstd) * gb,)

    return _ew(fn, [(x, *_rc(tm, D)), (g, *_col(1, D))], [(SDS((L, D), BF16), *_rc(tm, D), None)], (1, L // tm), name)[0]


def _rmsnorm_bwd(dh, x, g, dres, name, tm=256):
    L = x.shape[0]

    def fn(c, r, dhb, xb, gb, drb):
        rstd = lax.rsqrt(jnp.mean(xb * xb, axis=-1, keepdims=True) + RMS_EPS)
        y = xb * rstd
        dy = dhb * gb
        dx = rstd * (dy - y * jnp.mean(dy * y, axis=-1, keepdims=True))
        return drb + dx, jnp.sum(dhb * y, axis=0, keepdims=True)

    return _ew(fn, [(dh, *_rc(tm, D)), (x, *_rc(tm, D)), (g, *_col(1, D)), (dres, *_rc(tm, D))],
               [(SDS((L, D), F32), *_rc(tm, D), None), (SDS((1, D), F32), *_col(1, D), "all")], (1, L // tm), name)


def _final_loss(x2, g, tgt, name, tm=256):
    L = x2.shape[0]

    def fn(c, r, xb, gb, tb):
        rstd = lax.rsqrt(jnp.mean(xb * xb, axis=-1, keepdims=True) + RMS_EPS)
        y = xb * rstd
        err = y * gb - tb
        dout = err * (1.0 / D)
        dy = dout * gb
        dx = rstd * (dy - y * jnp.mean(dy * y, axis=-1, keepdims=True))
        return dx, dx, jnp.sum(err * err, axis=0, keepdims=True) * (0.5 / D), jnp.sum(dout * y, axis=0, keepdims=True)

    return _ew(fn, [(x2, *_rc(tm, D)), (g, *_col(1, D)), (tgt, *_rc(tm, D))],
               [(SDS((L, D), F32), *_rc(tm, D), None), (SDS((L, D), BF16), *_rc(tm, D), None),
                (SDS((1, D), F32), *_col(1, D), "all"),
                (SDS((1, D), F32), *_col(1, D), "all")], (1, L // tm), name)


def _attn_setup(n, kvc, kvp):
    kv = jnp.concatenate([kvp, kvc], axis=0).astype(F32)
    lo = lax.broadcasted_iota(jnp.int32, (2 * BLK, 128), 1) < HD

    def halves(t):
        tr = pltpu.roll(t, HD, 1)
        z = jnp.zeros_like(t)
        return {(0, 0): jnp.where(lo, t, z).astype(BF16), (0, 1): jnp.where(lo, z, tr).astype(BF16),
                (1, 0): jnp.where(lo, tr, z).astype(BF16), (1, 1): jnp.where(lo, z, t).astype(BF16)}

    kmat, vmat = halves(kv[:, :128]), halves(kv[:, 128:])
    qi = lax.broadcasted_iota(jnp.int32, (BLK, 2 * BLK), 0)
    si = lax.broadcasted_iota(jnp.int32, (BLK, 2 * BLK), 1)
    dist = qi + BLK - si
    valid = (dist >= 0) & (dist < BLK) & ((n > 0) | (si >= BLK))
    return kmat, vmat, valid, dist.astype(F32)


def _attn_probs(qp, kmat_ge, valid, distf, slope, sink):
    s = lax.dot_general(qp, kmat_ge, (((1,), (1,)), ((), ())), preferred_element_type=F32) * (HD ** -0.5)
    s = jnp.where(valid, s - slope * distf, NEG_BIG)
    m = jnp.maximum(jnp.max(s, axis=-1, keepdims=True), sink)
    p = jnp.exp(s - m)
    esink = jnp.exp(sink - m)
    den = jnp.sum(p, axis=-1, keepdims=True) + esink
    return p / den, esink / den


def _slope(h):
    return 2.0 ** (-8.0 * (h + 1) / NQ)


def _attn_fwd(projb, sinks, name):
    L = projb.shape[0]
    nb = L // BLK

    def body(s_ref, q_ref, kvc_ref, kvp_ref, o_ref):
        n = pl.program_id(0)
        kmat, vmat, valid, distf = _attn_setup(n, kvc_ref[...], kvp_ref[...])
        for j in range(NQ // 2):
            g = j // (NQ // 4)
            qp = q_ref[:, 128 * j:128 * (j + 1)]
            acc = jnp.zeros((BLK, 128), F32)
            for e in range(2):
                h = 2 * j + e
                p, _ = _attn_probs(qp, kmat[(g, e)], valid, distf, _slope(h), s_ref[0, h])
                acc = acc + jnp.dot(p.astype(BF16), vmat[(g, e)], preferred_element_type=F32)
            o_ref[:, 128 * j:128 * (j + 1)] = acc.astype(BF16)

    return pl.pallas_call(
        body, out_shape=SDS((L, AW), BF16), grid=(nb,),
        in_specs=[pl.BlockSpec(memory_space=pltpu.SMEM),
                  pl.BlockSpec((BLK, AW), lambda n: (n, 0)),
                  pl.BlockSpec((BLK, 256), lambda n: (n, C_K // 256)),
                  pl.BlockSpec((BLK, 256), lambda n: (jnp.maximum(n - 1, 0), C_K // 256))],
        out_specs=pl.BlockSpec((BLK, AW), lambda n: (n, 0)), name=name,
        compiler_params=_cparams(("arbitrary",), 32),
    )(sinks, projb, projb, projb)


def _attn_bwd(projb, sinks, dattn, name):
    L = projb.shape[0]
    nb = L // BLK

    def body(s_ref, q_ref, kvc_ref, kvp_ref, do_ref, dq_ref, dcur_ref, dprev_ref, dsink_ref):
        n = pl.program_id(0)
        kmat, vmat, valid, distf = _attn_setup(n, kvc_ref[...], kvp_ref[...])
        lo128 = lax.broadcasted_iota(jnp.int32, (BLK, 128), 1) < HD
        lane = lax.broadcasted_iota(jnp.int32, (1, 128), 1)
        dk = [jnp.zeros((2 * BLK, 128), F32) for _ in range(NKV)]
        dv = [jnp.zeros((2 * BLK, 128), F32) for _ in range(NKV)]
        dsv = jnp.zeros((1, 128), F32)
        tn_dims = (((0,), (0,)), ((), ()))
        for j in range(NQ // 2):
            g = j // (NQ // 4)
            qp = q_ref[:, 128 * j:128 * (j + 1)]
            dop = do_ref[:, 128 * j:128 * (j + 1)]
            dqp = jnp.zeros((BLK, 128), F32)
            for e in range(2):
                h = 2 * j + e
                p, psink = _attn_probs(qp, kmat[(g, e)], valid, distf, _slope(h), s_ref[0, h])
                dp = lax.dot_general(dop, vmat[(g, e)], (((1,), (1,)), ((), ())), preferred_element_type=F32)
                drow = jnp.sum(p * dp, axis=-1, keepdims=True)
                ds = p * (dp - drow)
                dsv = dsv + jnp.where(lane == h, -jnp.sum(psink * drow, axis=0, keepdims=True), 0.0)
                dsb = (ds * (HD ** -0.5)).astype(BF16)
                dqp = dqp + jnp.dot(dsb, kmat[(g, e)], preferred_element_type=F32)
                half = lo128 if e == 0 else jnp.logical_not(lo128)
                zb = jnp.zeros_like(qp)
                dk[g] = dk[g] + lax.dot_general(dsb, jnp.where(half, qp, zb), tn_dims, preferred_element_type=F32)
                dv[g] = dv[g] + lax.dot_general(p.astype(BF16), jnp.where(half, dop, zb), tn_dims,
                                                preferred_element_type=F32)
            dq_ref[:, 128 * j:128 * (j + 1)] = dqp.astype(BF16)
        lo256 = lax.broadcasted_iota(jnp.int32, (2 * BLK, 128), 1) < HD
        tot = [t + pltpu.roll(t, HD, 1) for t in (dk[0], dk[1], dv[0], dv[1])]
        dkv = jnp.concatenate([jnp.where(lo256, tot[0], tot[1]), jnp.where(lo256, tot[2], tot[3])], axis=1)
        dprev_ref[...] = dkv[:BLK]
        dcur_ref[...] = dkv[BLK:]

        @pl.when(n == 0)
        def _():
            dsink_ref[...] = dsv

        @pl.when(n > 0)
        def _():
            dsink_ref[...] += dsv

    return pl.pallas_call(
        body, out_shape=(SDS((L, AW), BF16), SDS((L, 256), F32), SDS((L, 256), F32), SDS((1, 128), F32)), grid=(nb,),
        in_specs=[pl.BlockSpec(memory_space=pltpu.SMEM),
                  pl.BlockSpec((BLK, AW), lambda n: (n, 0)),
                  pl.BlockSpec((BLK, 256), lambda n: (n, C_K // 256)),
                  pl.BlockSpec((BLK, 256), lambda n: (jnp.maximum(n - 1, 0), C_K // 256)),
                  pl.BlockSpec((BLK, AW), lambda n: (n, 0))],
        out_specs=(pl.BlockSpec((BLK, AW), lambda n: (n, 0)), pl.BlockSpec((BLK, 256), lambda n: (n, 0)),
                   pl.BlockSpec((BLK, 256), lambda n: (n, 0)), pl.BlockSpec((1, 128), lambda n: (0, 0))),
        name=name, compiler_params=_cparams(("arbitrary",), 32),
    )(sinks, projb, projb, projb, dattn)


def _disc(a_re, a_im, logdt, b_re, b_im):
    dt = jnp.exp(logdt)
    mag = jnp.exp(a_re * dt)
    ab_re = mag * jnp.cos(a_im * dt)
    ab_im = mag * jnp.sin(a_im * dt)
    nr = ab_re - 1.0
    ni = ab_im
    den = a_re * a_re + a_im * a_im
    z_re = (nr * a_re + ni * a_im) / den
    z_im = (ni * a_re - nr * a_im) / den
    return ab_re, ab_im, z_re * b_re - z_im * b_im, z_re * b_im + z_im * b_re


def _group_mask():
    row = lax.broadcasted_iota(jnp.int32, (SW, NS), 0) // H
    col = lax.broadcasted_iota(jnp.int32, (SW, NS), 1) // P
    return row == col


def _block_diag(re, im):
    mask = _group_mask()
    z = jnp.zeros((SW, NS), F32)
    return jnp.concatenate([jnp.where(mask, jnp.tile(re, (G, 1)), z), jnp.where(mask, jnp.tile(im, (G, 1)), z)], axis=1)


def _block_diag_t(big):
    mask = _group_mask()
    z = jnp.zeros((SW, NS), F32)
    re = jnp.sum(jnp.where(mask, big[:, :NS], z).reshape(G, H, NS), axis=0)
    im = jnp.sum(jnp.where(mask, big[:, NS:], z).reshape(G, H, NS), axis=0)
    return re, im


def _ssm_prep(a_re, a_im, logdt, b_re, b_im, c_re, c_im, name):
    def body(are, aim, ldt, bre, bim, cre, cim, ab_ref, bm_ref, cm_ref):
        ab_re, ab_im, bb_re, bb_im = _disc(are[...], aim[...], ldt[...], bre[...], bim[...])
        ab_ref[...] = jnp.concatenate([ab_re, ab_im], axis=1)
        bm_ref[...] = _block_diag(bb_re, bb_im).astype(BF16)
        cm_ref[...] = _block_diag(cre[...], -cim[...]).astype(BF16)

    return pl.pallas_call(body, out_shape=(SDS((1, 2 * NS), F32), SDS((SW, 2 * NS), BF16), SDS((SW, 2 * NS), BF16)),
                          name=name, compiler_params=pltpu.CompilerParams(vmem_limit_bytes=48 << 20),
                          )(a_re, a_im, logdt, b_re, b_im, c_re, c_im)


def _ssm_param_bwd(a_re, a_im, logdt, b_re, b_im, dab8, dbm, dcm, name):
    def body(are, aim, ldt, bre, bim, dab_ref, dbm_ref, dcm_ref, o_are, o_aim, o_ldt, o_bre, o_bim, o_cre, o_cim):
        dab = jnp.sum(dab_ref[...], axis=0, keepdims=True)
        dbb_re, dbb_im = _block_diag_t(dbm_ref[...])
        _, vjp = jax.vjp(_disc, are[...], aim[...], ldt[...], bre[...], bim[...])
        d_are, d_aim, d_ldt, d_bre, d_bim = vjp((dab[:, :NS], dab[:, NS:], dbb_re, dbb_im))
        o_are[...], o_aim[...], o_ldt[...], o_bre[...], o_bim[...] = d_are, d_aim, d_ldt, d_bre, d_bim
        dc_re, dc_imn = _block_diag_t(dcm_ref[...])
        o_cre[...] = dc_re
        o_cim[...] = -dc_imn

    v1, vh = SDS((1, NS), F32), SDS((H, NS), F32)
    return pl.pallas_call(body, out_shape=(v1, v1, v1, vh, vh, vh, vh), name=name,
                          compiler_params=pltpu.CompilerParams(vmem_limit_bytes=56 << 20),
                          )(a_re, a_im, logdt, b_re, b_im, dab8, dbm, dcm)


def _ssm_scan(src, wmat, ab, *, reverse, ends=None, xs=None, init=None, name, tk=32):
    L = src.shape[0]
    rows = NSEG * tk
    nch = L // rows
    seg_len = L // NSEG
    n_sq = int(math.log2(seg_len))
    assert 2 ** n_sq == seg_len and L % rows == 0
    first_pass = ends is None
    with_dab = (not first_pass) and reverse
    slab = 512
    n_slab = NS // slab

    def body(*refs):
        src_ref, w_ref, ab_ref = refs[:3]
        pos = 3
        if not first_pass:
            ends_ref = refs[pos]
            pos += 1
        if with_dab:
            xs_ref, xsh_ref, init_ref = refs[pos:pos + 3]
            pos += 3
        if first_pass:
            (e_ref,) = refs[pos:pos + 1]
            pos += 1
        else:
            st_out_ref, aux_ref = refs[pos:pos + 2]
            pos += 2
        buf_ref, st_ref = refs[pos:pos + 2]
        i = pl.program_id(0)
        a_re = ab_ref[:, :NS]
        a_im = -ab_ref[:, NS:] if reverse else ab_ref[:, NS:]

        @pl.when(i == 0)
        def _():
            if first_pass:
                st_ref[...] = jnp.zeros_like(st_ref)
            else:
                pr, pi = a_re, a_im
                for _ in range(n_sq):
                    pr, pi = pr * pr - pi * pi, 2.0 * pr * pi
                zr = jnp.zeros((1, NS), F32)
                cr, ci = zr, zr
                order = list(range(NSEG - 1, -1, -1)) if reverse else list(range(NSEG))
                st_ref[order[0]:order[0] + 1, :] = jnp.zeros((1, 2 * NS), F32)
                for jprev, j in zip(order[:-1], order[1:]):
                    er, ei = ends_ref[jprev:jprev + 1, :NS], ends_ref[jprev:jprev + 1, NS:]
                    cr, ci = er + pr * cr - pi * ci, ei + pr * ci + pi * cr
                    st_ref[j:j + 1, :NS] = cr
                    st_ref[j:j + 1, NS:] = ci
                if not reverse:
                    aux_ref[...] = st_ref[...]
                else:
                    aux_ref[...] = jnp.zeros_like(aux_ref)

        buf_ref[...] = jnp.dot(src_ref[...].astype(BF16), w_ref[...], preferred_element_type=F32)

        for s in range(n_slab):
            re_sl, im_sl = pl.ds(s * slab, slab), pl.ds(NS + s * slab, slab)
            ar = jnp.broadcast_to(a_re[:, s * slab:(s + 1) * slab], (NSEG, slab))
            ai = jnp.broadcast_to(a_im[:, s * slab:(s + 1) * slab], (NSEG, slab))

            def step(t, carry, re_sl=re_sl, im_sl=im_sl, ar=ar, ai=ai):
                k = (tk - 1 - t) if reverse else t
                r0 = pl.multiple_of(k * NSEG, NSEG)
                xr, xi = carry[0], carry[1]
                nr = ar * xr - ai * xi + buf_ref[pl.ds(r0, NSEG), re_sl]
                ni = ar * xi + ai * xr + buf_ref[pl.ds(r0, NSEG), im_sl]
                if not first_pass:
                    buf_ref[pl.ds(r0, NSEG), re_sl] = nr
                    buf_ref[pl.ds(r0, NSEG), im_sl] = ni
                if not with_dab:
                    return nr, ni
                rp = pl.multiple_of((k - 1) * NSEG, NSEG)
                xpr, xpi = xs_ref[pl.ds(rp, NSEG), re_sl], xs_ref[pl.ds(rp, NSEG), im_sl]
                return nr, ni, carry[2] + nr * xpr + ni * xpi, carry[3] + ni * xpr - nr * xpi

            carry = (st_ref[:, re_sl], st_ref[:, im_sl])
            if with_dab:
                z = jnp.zeros((NSEG, slab), F32)
                carry = lax.fori_loop(0, tk - 1, step, carry + (z, z))
                xr, xi, dr, di = carry
                nr = ar * xr - ai * xi + buf_ref[pl.ds(0, NSEG), re_sl]
                ni = ar * xi + ai * xr + buf_ref[pl.ds(0, NSEG), im_sl]
                buf_ref[pl.ds(0, NSEG), re_sl] = nr
                buf_ref[pl.ds(0, NSEG), im_sl] = ni
                at_start = i == nch - 1
                xpr = jnp.where(at_start, init_ref[:, re_sl], xsh_ref[:, re_sl])
                xpi = jnp.where(at_start, init_ref[:, im_sl], xsh_ref[:, im_sl])
                aux_ref[:, re_sl] += dr + nr * xpr + ni * xpi
                aux_ref[:, im_sl] += di + ni * xpr - nr * xpi
                carry = (nr, ni)
            else:
                carry = lax.fori_loop(0, tk, step, carry)
            st_ref[:, re_sl] = carry[0]
            st_ref[:, im_sl] = carry[1]

        if first_pass:
            @pl.when(i == nch - 1)
            def _():
                e_ref[...] = st_ref[...]
        else:
            st_out_ref[...] = buf_ref[...].astype(st_out_ref.dtype)

    chunk = (lambda i: (nch - 1 - i, 0)) if reverse else (lambda i: (i, 0))
    whole = lambda i: (0, 0)
    ins = [src, wmat, ab]
    in_specs = [pl.BlockSpec((rows, SW), chunk), pl.BlockSpec((SW, 2 * NS), whole), pl.BlockSpec((1, 2 * NS), whole)]
    small = SDS((NSEG, 2 * NS), F32)
    small_spec = pl.BlockSpec((NSEG, 2 * NS), whole)
    if not first_pass:
        ins.append(ends)
        in_specs.append(small_spec)
    if with_dab:
        ins += [xs, xs, init]
        in_specs += [pl.BlockSpec((rows, 2 * NS), chunk),
                     pl.BlockSpec((NSEG, 2 * NS), lambda i: (jnp.maximum((nch - 1 - i) * tk - 1, 0), 0)),
                     small_spec]
    if first_pass:
        out_shape, out_specs = small, small_spec
    else:
        out_shape = (SDS((L, 2 * NS), BF16 if reverse else F32), small)
        out_specs = (pl.BlockSpec((rows, 2 * NS), chunk), small_spec)
    return pl.pallas_call(
        body, out_shape=out_shape, grid=(nch,), in_specs=in_specs, out_specs=out_specs,
        scratch_shapes=[pltpu.VMEM((rows, 2 * NS), F32), pltpu.VMEM((NSEG, 2 * NS), F32)], name=name,
        compiler_params=_cparams(("arbitrary",), 56),
    )(*ins)


def _to_segments(a):
    L, c = a.shape
    return a.reshape(NSEG, L // NSEG, c).transpose(1, 0, 2).reshape(L, c)


def _from_segments(a):
    L, c = a.shape
    return a.reshape(L // NSEG, NSEG, c).transpose(1, 0, 2).reshape(L, c)


def _peer(x, y, c, m):
    return ((1 - x) if (m >> 2) & 1 else x, (1 - y) if (m >> 1) & 1 else y, (1 - c) if m & 1 else c)


def _dev_index(p):
    return 4 * p[0] + 2 * p[1] + p[2]


def _exchange(arrs, scatter, name):
    n = len(arrs)

    def body(*refs):
        ins, outs = refs[:n], refs[n:2 * n]
        send_sems, recv_sems, loc_sems = refs[2 * n:]
        x, y, c = lax.axis_index("x"), lax.axis_index("y"), lax.axis_index("c")
        me = _dev_index((x, y, c))

        def src(w, to):
            return ins[w].at[to] if scatter else ins[w]

        def local(w):
            return pltpu.make_async_copy(src(w, me), outs[w].at[me], loc_sems.at[w])

        def remote(w, m):
            peer = _peer(x, y, c, m)
            return pltpu.make_async_remote_copy(src_ref=src(w, _dev_index(peer)), dst_ref=outs[w].at[me],
                                                send_sem=send_sems.at[w, m - 1], recv_sem=recv_sems.at[w, m - 1],
                                                device_id=peer, device_id_type=pl.DeviceIdType.MESH)

        def arrival(w, m):
            peer = _peer(x, y, c, m)
            return pltpu.make_async_remote_copy(src_ref=src(w, me), dst_ref=outs[w].at[_dev_index(peer)],
                                                send_sem=send_sems.at[w, m - 1], recv_sem=recv_sems.at[w, m - 1],
                                                device_id=peer, device_id_type=pl.DeviceIdType.MESH)

        for w in range(n):
            local(w).start()
        for w in range(n):
            for m in range(1, N_DEV):
                remote(w, m).start()
        for w in range(n):
            for m in range(1, N_DEV):
                arrival(w, m).wait_recv()
        for w in range(n):
            for m in range(1, N_DEV):
                remote(w, m).wait_send()
        for w in range(n):
            local(w).wait()

    anyspec = pl.BlockSpec(memory_space=pl.ANY)
    out_shape = tuple(SDS(a.shape if scatter else (N_DEV,) + a.shape, a.dtype) for a in arrs)
    return pl.pallas_call(
        body, out_shape=out_shape, in_specs=[anyspec] * n, out_specs=tuple([anyspec] * n),
        scratch_shapes=[pltpu.SemaphoreType.DMA((n, N_DEV - 1)), pltpu.SemaphoreType.DMA((n, N_DEV - 1)),
                        pltpu.SemaphoreType.DMA((n,))],
        name=name, compiler_params=pltpu.CompilerParams(has_side_effects=True),
    )(*arrs)


_HBM = pl.BlockSpec(memory_space=pltpu.HBM)
_SEM = pl.BlockSpec(memory_space=pltpu.SEMAPHORE)
_EFFECT = pltpu.SideEffectType.DATAFLOW_SIDE_EFFECTING


def _sem_index(w, m):
    return w * (N_DEV - 1) + m - 1


def _exchange_start(arrs, scatter, name):
    n = len(arrs)
    lands = [lax.empty(a.shape if scatter else (N_DEV,) + a.shape, a.dtype) for a in arrs]

    def body(*refs):
        ins, zones = refs[:n], refs[n:2 * n]
        send_sems, recv_sems = refs[2 * n], refs[2 * n + 1]
        token = refs[-1]
        x, y, c = lax.axis_index("x"), lax.axis_index("y"), lax.axis_index("c")
        me = _dev_index((x, y, c))
        for w in range(n):
            for m in range(1, N_DEV):
                peer = _peer(x, y, c, m)
                pltpu.make_async_remote_copy(
                    src_ref=ins[w].at[_dev_index(peer)] if scatter else ins[w], dst_ref=zones[w].at[me],
                    send_sem=send_sems.at[_sem_index(w, m)], recv_sem=recv_sems.at[_sem_index(w, m)],
                    device_id=peer, device_id_type=pl.DeviceIdType.MESH).start()
        token[...] = jnp.zeros_like(token)

    sems = pltpu.SemaphoreType.DMA((n * (N_DEV - 1),))
    res = pl.pallas_call(
        body, name=name,
        out_shape=(sems, sems, *[pltpu.HBM(a.shape, a.dtype) for a in arrs], *[pltpu.HBM(z.shape, z.dtype) for z in lands],
                   SDS((8, 128), F32)),
        in_specs=[_HBM] * (2 * n), out_specs=(_SEM, _SEM, *([_HBM] * (2 * n)), pl.BlockSpec(memory_space=pltpu.VMEM)),
        input_output_aliases={i: 2 + i for i in range(2 * n)},
        compiler_params=pltpu.CompilerParams(has_side_effects=_EFFECT),
    )(*[pltpu.with_memory_space_constraint(a, pltpu.HBM) for a in arrs],
      *[pltpu.with_memory_space_constraint(z, pltpu.HBM) for z in lands])
    return (res[0], res[1], list(res[2:2 + n]), list(res[2 + n:2 + 2 * n])), res[-1]


def _exchange_wait(handle, after, scatter, name):
    send_sems, recv_sems, thru, lands = handle
    n = len(thru)

    def body(*refs):
        ins, zones = refs[:n], refs[n:2 * n]
        send_sems, recv_sems = refs[2 * n], refs[2 * n + 1]
        x, y, c = lax.axis_index("x"), lax.axis_index("y"), lax.axis_index("c")
        me = _dev_index((x, y, c))
        for w in range(n):
            for m in range(1, N_DEV):
                peer = _peer(x, y, c, m)
                copy = pltpu.make_async_remote_copy(
                    src_ref=ins[w].at[me] if scatter else ins[w], dst_ref=zones[w].at[_dev_index(peer)],
                    send_sem=send_sems.at[_sem_index(w, m)], recv_sem=recv_sems.at[_sem_index(w, m)],
                    device_id=peer, device_id_type=pl.DeviceIdType.MESH)
                copy.wait_send()
                copy.wait_recv()

    res = pl.pallas_call(
        body, name=name,
        out_shape=(*[pltpu.HBM(a.shape, a.dtype) for a in thru], *[pltpu.HBM(z.shape, z.dtype) for z in lands]),
        in_specs=[_HBM] * (2 * n) + [_SEM, _SEM, pl.BlockSpec(memory_space=pl.ANY)], out_specs=tuple([_HBM] * (2 * n)),
        input_output_aliases={i: i for i in range(2 * n)},
        compiler_params=pltpu.CompilerParams(has_side_effects=_EFFECT),
    )(*thru, *lands, send_sems, recv_sems, after)
    return list(res[:n]), list(res[n:])


def _adam_math(g, w, m, v):
    m = ADAM_B1 * m + (1.0 - ADAM_B1) * g
    v = ADAM_B2 * v + (1.0 - ADAM_B2) * (g * g)
    m_hat = m / (1.0 - ADAM_B1 ** ADAM_STEP)
    v_hat = v / (1.0 - ADAM_B2 ** ADAM_STEP)
    delta = -ADAM_LR * (m_hat / (jnp.sqrt(v_hat) + ADAM_EPS) + ADAM_WD * w)
    return delta, m, v


def _adam(parts, w, m, v, name, tr=128):
    r, c = w.shape
    tr = next(t for t in (tr, 64, 32, 16, 8) if r % t == 0)

    def fn(cc, rr, pb, wb, mb, vb):
        g = pb[0].astype(F32)
        for d in range(1, N_DEV):
            g = g + pb[d].astype(F32)
        delta, nm, nv = _adam_math(g, wb, mb, vb)
        return g, delta, nm, nv

    blk = ((tr, c), lambda cc, rr: (rr, 0))
    o = SDS((r, c), F32)
    return _ew(fn, [(parts, (N_DEV, tr, c), lambda cc, rr: (0, rr, 0)), (w, *blk), (m, *blk), (v, *blk)],
               [(o, *blk, None)] * 4, (1, r // tr), name)


_SHARDED = ("w_in", "w_glu", "w_branch_attn", "w_branch_ssm", "w_out", "w_up", "w_down")
_COL_SHARDED = ("w_in", "w_glu", "w_branch_attn", "w_branch_ssm", "w_up")
_GROUPS = {"a": ("w_in",), "b": ("w_glu", "w_branch_attn", "w_branch_ssm", "w_out"), "c": ("w_up", "w_down")}
_SMALL = ("attn_norm_g", "b_in", "attn_sinks", "ssm_a_re", "ssm_a_im", "ssm_log_dt", "ssm_b_re", "ssm_b_im",
          "ssm_c_re", "ssm_c_im", "ssm_d", "b_glu", "ffn_norm_g", "conv_w", "conv_b", "final_norm_g")
_WEIGHTS = ("attn_norm_g", "w_in", "b_in", "attn_sinks", "ssm_a_re", "ssm_a_im", "ssm_log_dt", "ssm_b_re", "ssm_b_im",
            "ssm_c_re", "ssm_c_im", "ssm_d", "w_glu", "b_glu", "w_branch_attn", "w_branch_ssm", "w_out", "ffn_norm_g",
            "w_up", "conv_w", "conv_b", "w_down", "final_norm_g")


def _unstack_cols(g):
    return g.transpose(1, 0, 2).reshape(g.shape[1], N_DEV * g.shape[2])


def _stack_cols(a):
    k, n = a.shape
    return a.reshape(k, N_DEV, n // N_DEV).transpose(1, 0, 2)


def _pack(arrs):
    flat = jnp.concatenate([a.reshape(-1) for a in arrs])
    pad = (-flat.shape[0]) % 1024
    return jnp.pad(flat, (0, pad)).reshape(-1, 128)


def _local_step(x, tgt, wget, small, gput):
    L = x.shape[0]
    nr = lambda tm: L // tm

    h = _rmsnorm_fwd(x, small["attn_norm_g"], "norm1")
    wts = dict(wget("a", h))
    projb = _mm(h, wts["w_in"], bias=small["b_in_p"], out_dtype=BF16, name="proj")
    proj = projb
    attn = _attn_fwd(projb, small["attn_sinks"], "attn_fwd")

    ab, bmat, cmat = _ssm_prep(small["a_re"], small["a_im"], small["logdt"], small["b_re"], small["b_im"],
                               small["c_re"], small["c_im"], "ssm_prep")
    u_seg = _to_segments(proj[:, C_U:C_PAD])
    ends_f = _ssm_scan(u_seg, bmat, ab, reverse=False, name="ssm_ends_fwd")
    xs, init_f = _ssm_scan(u_seg, bmat, ab, reverse=False, ends=ends_f, name="ssm_scan_fwd")
    y_mm = _from_segments(_mm(xs, cmat, tb=True, tm=512, tk=1024, name="ssm_out"))

    def gelu_fn(c, r, yb, ub, db):
        yv = yb + db * ub
        return yv, _gelu(yv)

    tm = 512
    y, gy = _ew(gelu_fn, [(y_mm, *_rc(tm, 256)), (proj, *_rc(tm, 256, C_U // 256)), (small["ssm_d"], *_col(1, 256))],
                [(SDS((L, SW), F32), *_rc(tm, 256), None), (SDS((L, SW), BF16), *_rc(tm, 256), None)],
                (2, nr(tm)), "ssm_gelu")
    wts.update(wget("b", gy))
    glu = _mm(gy, wts["w_glu"], bias=small["b_glu"], name="glu")

    def glu_fn(c, r, vb, gb):
        return (vb * _sigmoid(gb),)

    (ssm,) = _ew(glu_fn, [(glu, *_rc(tm, SW)), (glu, *_rc(tm, SW, 1))], [(SDS((L, SW), BF16), *_rc(tm, SW), None)],
                 (1, nr(tm)), "glu_gate")
    f32 = lambda ref: ref[...].astype(F32)
    tnm = 1024
    gate_tiles = [(projb, "tile", C_GA // tnm), (projb, "tile", C_GS // tnm)]

    def merge_ep(i, ra, rs, ga, gs):
        return _sigmoid(f32(ga)) * ra + _sigmoid(f32(gs)) * rs, ra, rs

    merged, br_a, br_s = _mm(attn, wts["w_branch_attn"], a2=ssm, b2=wts["w_branch_ssm"], tm=512, tn=tnm,
                             extras=gate_tiles, epilogue=merge_ep, outs=[(SDS((L, D), BF16), "tile")] * 3,
                             name="branch_merge")
    x1 = _mm(merged, wts["w_out"], res=x, name="out_proj")
    h2 = _rmsnorm_fwd(x1, small["ffn_norm_g"], "norm2")
    wts.update(wget("c", h2))
    conv_w = wts["conv_w"]
    w_up_v, w_up_g = wts["w_up"][:, :DFF], wts["w_up"][:, DFF:]
    tcf = 1408
    tma = 256
    hb = 16

    def conv_gate(first, gate, halo, cw, cb):
        halo = halo * jnp.logical_not(first).astype(F32)
        g1, g2 = _shift_rows(gate, halo, 1), _shift_rows(gate, halo, 2)
        return cb + cw[2:3] * gate + cw[1:2] * g1 + cw[0:1] * g2, g1, g2

    tmu, tnu = 1024, 512

    def up_ep(i, rv, rg, h2_halo, wg, cw, cb):
        halo = jnp.dot(h2_halo[...], wg[...], preferred_element_type=F32)[hb - 8:]
        return rv, rg, rv * _gelu(conv_gate(i == 0, rg, halo, cw[...], cb[...])[0])

    up_v, up_g, act = _mm(
        h2, w_up_v, b2=w_up_g, tm=tmu, tn=tnu, epilogue=up_ep, outs=[(SDS((L, DFF), BF16), "tile")] * 3, name="ffn_up_act",
        extras=[(h2, "spec", ((hb, D), lambda j, i: (jnp.maximum(i * (tmu // hb) - 1, 0), 0))),
                (w_up_g, "spec", ((D, tnu), lambda j, i: (0, j))), (conv_w, "col", 0), (small["conv_b"], "col", 0)])
    x2 = _mm(act, wts["w_down"], res=x1, name="ffn_down")
    d_x2, d_x2b, loss_cols, d_gf = _final_loss(x2, small["final_norm_g"], tgt, "final_loss")
    loss = jnp.sum(loss_cols)

    dw_down = _mm(act, d_x2b, ta=True, out_dtype=BF16, tm=tcf, tk=2048, name="dw_down")
    tmd = 512

    def dact_ep(i, da, _, val_ref, gate_ref, halo_ref, cw, cb):
        val, gate = f32(val_ref), f32(gate_ref)
        cg, g1, g2 = conv_gate(i == 0, gate, f32(halo_ref)[hb - 8:], cw[...], cb[...])
        gl, glg = _gelu_and_grad(cg)
        d_cg = da * val * glg
        row3 = lax.broadcasted_iota(jnp.int32, (3, tcf), 0)
        s0 = jnp.sum(d_cg * g2, axis=0, keepdims=True)
        s1 = jnp.sum(d_cg * g1, axis=0, keepdims=True)
        s2 = jnp.sum(d_cg * gate, axis=0, keepdims=True)
        dcw = jnp.where(row3 == 0, s0, jnp.where(row3 == 1, s1, s2))
        return da * gl, d_cg, dcw, jnp.sum(d_cg, axis=0, keepdims=True)

    d_val, d_cg, d_conv_w, d_conv_b = _mm(
        d_x2b, wts["w_down"], tb=True, tm=tmd, tn=tcf, epilogue=dact_ep, name="d_act_bwd",
        extras=[(up_v, "tile", 0), (up_g, "tile", 0),
                (up_g, "spec", ((hb, tcf), lambda j, i: (jnp.maximum(i * (tmd // hb) - 1, 0), j))),
                (conv_w, "col", 0), (small["conv_b"], "col", 0)],
        outs=[(SDS((L, DFF), BF16), "tile")] * 2 + [(SDS((3, DFF), F32), "colacc"), (SDS((1, DFF), F32), "colacc")])
    ncf = DFF // tcf

    def gate_bwd(c, r, dcg, halo, cw):
        halo = halo[:8] * (r < nr(tma) - 1).astype(F32)
        return (cw[2:3] * dcg + cw[1:2] * _shift_rows_up(dcg, halo, 1) + cw[0:1] * _shift_rows_up(dcg, halo, 2),)

    (d_gate,) = _ew(gate_bwd, [(d_cg, *_rc(tma, tcf)),
                               (d_cg, (hb, tcf), lambda c, r: (jnp.minimum((r + 1) * (tma // hb), L // hb - 1), c)),
                               (conv_w, *_col(3, tcf))],
                    [(SDS((L, DFF), BF16), *_rc(tma, tcf), None)], (ncf, nr(tma)), "ffn_gate_bwd")
    d_h2 = _mm(d_val, w_up_v, tb=True, name="d_h2_val")
    d_h2 = _mm(d_gate, w_up_g, tb=True, res=d_h2, name="d_h2_gate")
    dw_up_v = _mm(h2, d_val, ta=True, out_dtype=BF16, tn=tcf, tk=2048, name="dw_up_val")
    dw_up_g = _mm(h2, d_gate, ta=True, out_dtype=BF16, tn=tcf, tk=2048, name="dw_up_gate")
    tok = gput("c", {"w_up": jnp.concatenate([dw_up_v, dw_up_g], axis=1), "w_down": dw_down})
    d_x1, d_g2 = _rmsnorm_bwd(d_h2, x1, small["ffn_norm_g"] + tok[0, 0], d_x2, "norm2_bwd")

    dw_out = _mm(merged, d_x1, ta=True, out_dtype=BF16, name="dw_out")

    def dmerge_ep(i, dm, _, a_ref, s_ref, ga, gs):
        sa, ss = _sigmoid(f32(ga)), _sigmoid(f32(gs))
        return dm * sa, dm * ss, dm * f32(a_ref) * (sa * (1.0 - sa)), dm * f32(s_ref) * (ss * (1.0 - ss))

    d_bra, d_brs, d_ga, d_gs = _mm(d_x1, wts["w_out"], tb=True, tm=512, tn=tnm, epilogue=dmerge_ep, name="d_merged_bwd",
                                   extras=[(br_a, "tile", 0), (br_s, "tile", 0)] + gate_tiles,
                                   outs=[(SDS((L, D), BF16), "tile")] * 4)
    d_attn = _mm(d_bra, wts["w_branch_attn"], tb=True, out_dtype=BF16, name="d_attn")
    dw_ba = _mm(attn, d_bra, ta=True, out_dtype=BF16, name="dw_branch_attn")
    d_ssm = _mm(d_brs, wts["w_branch_ssm"], tb=True, name="d_ssm")
    dw_bs = _mm(ssm, d_brs, ta=True, out_dtype=BF16, name="dw_branch_ssm")
    dq, dkv_cur, dkv_prev, d_sinks = _attn_bwd(projb, small["attn_sinks"], d_attn, "attn_bwd")

    def glu_bwd(c, r, ds, vb, gb):
        sg = _sigmoid(gb)
        return ds * sg, ds * vb * (sg * (1.0 - sg))

    d_glu_v, d_glu_g = _ew(glu_bwd, [(d_ssm, *_rc(tm, SW)), (glu, *_rc(tm, SW)), (glu, *_rc(tm, SW, 1))],
                           [(SDS((L, SW), F32), *_rc(tm, SW), None)] * 2, (1, nr(tm)), "glu_gate_bwd")
    d_glu = jnp.concatenate([d_glu_v, d_glu_g], axis=1)
    d_gy = _mm(d_glu, wts["w_glu"], tb=True, name="d_gelu_y")
    dw_glu = _mm(gy, d_glu, ta=True, out_dtype=BF16, name="dw_glu")

    tok = gput("b", {"w_glu": dw_glu, "w_branch_attn": dw_ba, "w_branch_ssm": dw_bs, "w_out": dw_out})
    ab = ab + tok[0, 0]

    def gelu_bwd(c, r, dg, yb, ub, dgl):
        dy = dg * _gelu_grad(yb)
        return dy, jnp.sum(dy * ub, axis=0, keepdims=True), jnp.sum(dgl, axis=0, keepdims=True)

    dy, d_ssm_d, d_b_glu = _ew(
        gelu_bwd, [(d_gy, *_rc(tm, 256)), (y, *_rc(tm, 256)), (proj, *_rc(tm, 256, C_U // 256)), (d_glu, *_rc(tm, 512))],
        [(SDS((L, SW), F32), *_rc(tm, 256), None), (SDS((1, SW), F32), *_col(1, 256), "r"),
         (SDS((1, 2 * SW), F32), *_col(1, 512), "r")], (2, nr(tm)), "ssm_gelu_bwd")

    dy_seg = _to_segments(dy)
    ends_r = _ssm_scan(dy_seg, cmat, ab, reverse=True, name="ssm_ends_bwd")
    lam, dab8 = _ssm_scan(dy_seg, cmat, ab, reverse=True, ends=ends_r, xs=xs, init=init_f, name="ssm_scan_bwd")
    du_mm = _from_segments(_mm(lam, bmat, tb=True, tm=512, tk=1024, name="ssm_du"))
    dbm = _mm(u_seg, lam, ta=True, tm=512, name="ssm_dbmat")
    dcm = _mm(dy_seg, xs, ta=True, tm=512, name="ssm_dcmat")
    d_are, d_aim, d_ldt, d_bre, d_bim, d_cre, d_cim = _ssm_param_bwd(
        small["a_re"], small["a_im"], small["logdt"], small["b_re"], small["b_im"], dab8, dbm, dcm, "ssm_param_bwd")

    nb = L // BLK

    def dproj_fn(c, r, dqb, cur, prv, du, dyb, dsk, dga, dgs):
        dkv = cur + prv * (r < nb - 1).astype(F32)
        dub = du + dsk * dyb
        full = jnp.concatenate([dqb, dkv, dub, jnp.zeros((BLK, C_GA - C_PAD), F32), dga, dgs], axis=1)
        return full, jnp.sum(full, axis=0, keepdims=True)

    rowb = lambda w: ((BLK, w), lambda c, r: (r, 0))
    dproj, d_b_in = _ew(
        dproj_fn, [(dq, *rowb(AW)), (dkv_cur, *rowb(256)),
                   (dkv_prev, (BLK, 256), lambda c, r: (jnp.minimum(r + 1, nb - 1), 0)),
                   (du_mm, *rowb(SW)), (dy, *rowb(SW)), (small["ssm_d"], *_col(1, SW)), (d_ga, *rowb(D)), (d_gs, *rowb(D))],
        [(SDS((L, INP), BF16), *rowb(INP), None), (SDS((1, INP), F32), *_col(1, INP), "all")], (1, nb), "dproj")
    dw_in = _mm(h, dproj, ta=True, out_dtype=BF16, name="dw_in")
    tok = gput("a", {"w_in": _unpad_cols(dw_in)})
    d_h = _mm(dproj, wts["w_in"], tb=True, bias=jnp.zeros((1, D), F32) + tok[0, 0], name="d_h")
    grad_x, d_g1 = _rmsnorm_bwd(d_h, x, small["attn_norm_g"], d_x1, "norm1_bwd")

    sgrads = {"attn_norm_g": d_g1, "b_in": _unpad_cols(d_b_in), "attn_sinks": d_sinks[:, :NQ], "a_re": d_are, "a_im": d_aim,
              "logdt": d_ldt, "b_re": d_bre, "b_im": d_bim, "c_re": d_cre, "c_im": d_cim, "ssm_d": d_ssm_d,
              "b_glu": d_b_glu, "ffn_norm_g": d_g2, "conv_w": d_conv_w, "conv_b": d_conv_b, "final_norm_g": d_gf}
    return loss, grad_x, sgrads


def _small_layouts(p):
    gp = lambda a: a.reshape(1, NS)
    hgp = lambda a: a.transpose(2, 0, 1).reshape(H, NS)
    chgp = lambda a: a.transpose(1, 0, 2).reshape(H, NS)
    return {
        "attn_norm_g": p["attn_norm_g"].reshape(1, D), "ffn_norm_g": p["ffn_norm_g"].reshape(1, D),
        "final_norm_g": p["final_norm_g"].reshape(1, D),
        "b_in_p": _pad_cols(p["b_in"].reshape(1, INC)),
        "attn_sinks": p["attn_sinks"].reshape(1, NQ),
        "a_re": gp(p["ssm_a_re"]), "a_im": gp(p["ssm_a_im"]), "logdt": jnp.repeat(p["ssm_log_dt"], P).reshape(1, NS),
        "b_re": hgp(p["ssm_b_re"]), "b_im": hgp(p["ssm_b_im"]), "c_re": chgp(p["ssm_c_re"]), "c_im": chgp(p["ssm_c_im"]),
        "ssm_d": p["ssm_d"].reshape(1, SW), "b_glu": p["b_glu"].reshape(1, 2 * SW),
        "conv_b": p["conv_b"].reshape(1, DFF),
    }


def _small_grads_to_param_shapes(sg):
    from_hgp = lambda a: a.reshape(H, G, P).transpose(1, 2, 0)
    from_chgp = lambda a: a.reshape(H, G, P).transpose(1, 0, 2)
    return {
        "attn_norm_g": sg["attn_norm_g"].reshape(D), "b_in": sg["b_in"].reshape(INC),
        "attn_sinks": sg["attn_sinks"].reshape(NQ),
        "ssm_a_re": sg["a_re"].reshape(G, P), "ssm_a_im": sg["a_im"].reshape(G, P),
        "ssm_log_dt": jnp.sum(sg["logdt"].reshape(G, P), axis=1),
        "ssm_b_re": from_hgp(sg["b_re"]), "ssm_b_im": from_hgp(sg["b_im"]),
        "ssm_c_re": from_chgp(sg["c_re"]), "ssm_c_im": from_chgp(sg["c_im"]),
        "ssm_d": sg["ssm_d"].reshape(SW), "b_glu": sg["b_glu"].reshape(2 * SW),
        "ffn_norm_g": sg["ffn_norm_g"].reshape(D), "conv_w": sg["conv_w"], "conv_b": sg["conv_b"].reshape(DFF),
        "final_norm_g": sg["final_norm_g"].reshape(D),
    }


def kernel(x, attn_norm_g, w_in, b_in, attn_sinks, ssm_a_re, ssm_a_im, ssm_log_dt, ssm_b_re, ssm_b_im, ssm_c_re, ssm_c_im, ssm_d, w_glu, b_glu, w_branch_attn, w_branch_ssm, w_out, ffn_norm_g, w_up, conv_w, conv_b, w_down, final_norm_g, loss_target, m_attn_norm_g, m_w_in, m_b_in, m_attn_sinks, m_ssm_a_re, m_ssm_a_im, m_ssm_log_dt, m_ssm_b_re, m_ssm_b_im, m_ssm_c_re, m_ssm_c_im, m_ssm_d, m_w_glu, m_b_glu, m_w_branch_attn, m_w_branch_ssm, m_w_out, m_ffn_norm_g, m_w_up, m_conv_w, m_conv_b, m_w_down, m_final_norm_g, v_attn_norm_g, v_w_in, v_b_in, v_attn_sinks, v_ssm_a_re, v_ssm_a_im, v_ssm_log_dt, v_ssm_b_re, v_ssm_b_im, v_ssm_c_re, v_ssm_c_im, v_ssm_d, v_w_glu, v_b_glu, v_w_branch_attn, v_w_branch_ssm, v_w_out, v_ffn_norm_g, v_w_up, v_conv_w, v_conv_b, v_w_down, v_final_norm_g):
    args = dict(locals())
    sq = lambda a: a if a.ndim == 1 else a[0]
    wv = {n: sq(args[n]) for n in _WEIGHTS}
    mv = {n: sq(args["m_" + n]) for n in _WEIGHTS}
    vv = {n: sq(args["v_" + n]) for n in _WEIGHTS}
    me = 4 * lax.axis_index("x") + 2 * lax.axis_index("y") + lax.axis_index("c")

    gather, tok = {}, jnp.zeros((8, 128), F32)
    for grp in ("a", "b", "c"):
        shards = [(wv[n] + tok[0, 0]).astype(BF16) for n in _GROUPS[grp]]
        if grp == "c":
            shards.append(jnp.pad(wv["conv_w"] + tok[0, 0], ((0, 5), (0, 64))))
        gather[grp], tok = _exchange_start(shards, False, "gather_start_" + grp)
    small = _small_layouts(wv)
    small["attn_norm_g"] = small["attn_norm_g"] + tok[0, 0]

    def own_slot(land, src):
        return lax.dynamic_update_slice_in_dim(land, src, me, axis=0)

    def wget(grp, after):
        thru, lands = _exchange_wait(gather[grp], after, False, "gather_wait_" + grp)
        full = {}
        for n, t, g in zip(_GROUPS[grp], thru, lands):
            g = own_slot(g, t[None])
            full[n] = _unstack_cols(g) if n in _COL_SHARDED else g.reshape(N_DEV * g.shape[1], g.shape[2])
        if grp == "a":
            full["w_in"] = _pad_cols(full["w_in"])
        if grp == "c":
            full["conv_w"] = _unstack_cols(own_slot(lands[-1], thru[-1][None])[:, :3, :DFF // N_DEV])
        return full

    scatter = {}

    def gput(grp, grads):
        stacked = [_stack_cols(grads[n]) if n in _COL_SHARDED else grads[n].reshape(N_DEV, -1, D) for n in _GROUPS[grp]]
        scatter[grp], token = _exchange_start(stacked, True, "scatter_start_" + grp)
        return token

    loss, grad_x, sg = _local_step(x[0], loss_target[0], wget, small, gput)
    loss = lax.psum(loss, MESH_AXES)

    sgp = _small_grads_to_param_shapes(sg)
    small_names = [n for n in _SMALL]
    packed_g = _pack([sgp[n] for n in small_names])
    (small_all,) = _exchange([packed_g], False, "gather_small_grads")

    outs_g, outs_d, outs_m, outs_v = {}, {}, {}, {}
    for grp in ("c", "b", "a"):
        thru, lands = _exchange_wait(scatter[grp], small_all, True, "scatter_wait_" + grp)
        for n, t, pt in zip(_GROUPS[grp], thru, lands):
            pt = own_slot(pt, lax.dynamic_slice_in_dim(t, me, 1, axis=0))
            outs_g[n], outs_d[n], outs_m[n], outs_v[n] = _adam(pt, wv[n], mv[n], vv[n], "adam_" + n)

    sizes = [int(math.prod(sgp[n].shape)) for n in small_names]
    offs = [0]
    for s in sizes:
        offs.append(offs[-1] + s)

    def local_part(n, a):
        if n == "conv_w":
            return lax.dynamic_slice(a, (0, me * (DFF // N_DEV)), (3, DFF // N_DEV))
        return a

    rows = packed_g.shape[0]

    def sum_fn(cc, rr, pb):
        g = pb[0]
        for d in range(1, N_DEV):
            g = g + pb[d]
        return (g,)

    (gsum,) = _ew(sum_fn, [(small_all, (N_DEV, rows, 128), lambda cc, rr: (0, 0, 0))],
                  [(SDS((rows, 128), F32), (rows, 128), lambda cc, rr: (0, 0), None)], (1, 1), "sum_small_grads")
    gflat = gsum.reshape(-1)
    gsmall = {n: local_part(n, gflat[offs[i]:offs[i + 1]].reshape(sgp[n].shape)) for i, n in enumerate(small_names)}
    pw = _pack([wv[n] for n in small_names])
    pm = _pack([mv[n] for n in small_names])
    pv = _pack([vv[n] for n in small_names])
    pg = _pack([gsmall[n] for n in small_names])
    prow = pw.shape[0]

    def adam_small(cc, rr, gb, wb, mb, vb):
        return _adam_math(gb, wb, mb, vb)

    whole = ((prow, 128), lambda cc, rr: (0, 0))
    sd, sm, sv = _ew(adam_small, [(pg, *whole), (pw, *whole), (pm, *whole), (pv, *whole)],
                     [(SDS((prow, 128), F32), *whole, None)] * 3, (1, 1), "adam_small")
    lsizes = [int(math.prod(wv[n].shape)) for n in small_names]
    loffs = [0]
    for s in lsizes:
        loffs.append(loffs[-1] + s)
    for i, n in enumerate(small_names):
        take = lambda a: a.reshape(-1)[loffs[i]:loffs[i + 1]].reshape(wv[n].shape)
        outs_g[n], outs_d[n], outs_m[n], outs_v[n] = gsmall[n], take(sd), take(sm), take(sv)

    lead = lambda n, a: a if args[n].ndim == 1 else a[None]
    grad_x = grad_x[None]
    return (loss, grad_x, *[lead(n, outs_g[n]) for n in _WEIGHTS], *[lead(n, outs_d[n]) for n in _WEIGHTS],
            *[lead(n, outs_m[n]) for n in _WEIGHTS], *[lead(n, outs_v[n]) for n in _WEIGHTS])
```

```python
import functools
import math

import jax
import jax.numpy as jnp
from jax import lax
from jax.experimental import pallas as pl
from jax.experimental.pallas import tpu as pltpu

F32 = jnp.float32
BF16 = jnp.bfloat16
SDS = jax.ShapeDtypeStruct

N_DEV = 8
D = 2048
NQ, NKV, HD = 16, 2, 64
AW = NQ * HD
BLK = 128
SW, G, H, P = 512, 32, 16, 64
NS = G * P
DFF = 5632
INC = AW + 2 * NKV * HD + SW + 2 * D
C_K, C_U, C_PAD = AW, AW + 2 * NKV * HD, AW + 2 * NKV * HD + SW
C_GA, C_GS, INP = D, 2 * D, 3 * D
RMS_EPS = 1e-6
NEG_BIG = -1e30
ADAM_LR, ADAM_B1, ADAM_B2, ADAM_EPS, ADAM_WD, ADAM_STEP = 0.001, 0.9, 0.999, 1e-08, 0.01, 10
NSEG = 8
VMEM_CAP_MB = 60
MESH_AXES = ("x", "y", "c")


def _pad_cols(a):
    zeros = jnp.zeros(a.shape[:-1] + (C_GA - C_PAD,), a.dtype)
    return jnp.concatenate([a[..., :C_PAD], zeros, a[..., C_PAD:]], axis=-1)


def _unpad_cols(a):
    return jnp.concatenate([a[..., :C_PAD], a[..., C_GA:]], axis=-1)


def _cparams(sem, vmem_mb):
    return pltpu.CompilerParams(dimension_semantics=sem, vmem_limit_bytes=min(int(vmem_mb), VMEM_CAP_MB) << 20)


LANES = 128


def _tile(dim, pref):
    if dim <= pref:
        return dim
    for t in range(pref - pref % LANES, 0, -LANES):
        if dim % t == 0:
            return t
    raise ValueError(f"no tile for {dim}")


def _mm(a, b, *, ta=False, tb=False, bias=None, res=None, out_dtype=F32, tm=1024, tn=1024, tk=3072, name,
        a2=None, b2=None, extras=(), epilogue=None, outs=None, ep_cols=None):
    m, k = (a.shape[1], a.shape[0]) if ta else a.shape
    n = b.shape[0] if tb else b.shape[1]
    assert (b.shape[1] if tb else b.shape[0]) == k, (a.shape, b.shape, ta, tb)
    tm, tn, tk = _tile(m, tm), _tile(n, tn), _tile(k, tk)
    nk = k // tk
    dims = (((0 if ta else 1,), (1 if tb else 0,)), ((), ()))
    has_bias, has_res, has_b2 = bias is not None, res is not None, b2 is not None
    has_a2 = a2 is not None
    assert not (has_b2 and (nk > 1 or ta or tb)) and not (has_a2 and not has_b2)
    if epilogue is None:
        outs = [(SDS((m, n), out_dtype), "tile")]
    n_ex, n_out = len(extras), len(outs)
    tcn = tn if (epilogue is None or nk > 1 or ep_cols is None) else _tile(tn, ep_cols)

    def body(*refs):
        a_ref, b_ref = refs[0], refs[1]
        pos = 2
        a2_ref = refs[pos] if has_a2 else a_ref
        pos += has_a2
        b2_ref = refs[pos] if has_b2 else None
        pos += has_b2
        bias_ref = refs[pos] if has_bias else None
        pos += has_bias
        res_ref = refs[pos] if has_res else None
        pos += has_res
        ex_refs = refs[pos:pos + n_ex]
        o_refs = refs[pos + n_ex:pos + n_ex + n_out]
        i = pl.program_id(1)

        def product(rhs_ref, cols=None, lhs=None):
            rhs = rhs_ref[...] if cols is None else (rhs_ref[cols, :] if tb else rhs_ref[:, cols])
            lhs = a_ref[...].astype(BF16) if lhs is None else lhs
            return lax.dot_general(lhs, rhs.astype(BF16), dims, preferred_element_type=F32)

        def finish(r, cols):
            if has_bias:
                r = r + bias_ref[:, cols]
            if has_res:
                r = r + res_ref[:, cols].astype(F32)
            if epilogue is None:
                o_refs[0][:, cols] = r.astype(o_refs[0].dtype)
                return
            r2 = None
            if has_b2:
                r2 = jnp.dot(a2_ref[...].astype(BF16), b2_ref[:, cols].astype(BF16), preferred_element_type=F32)
            vals = epilogue(i, cols, r, r2, *ex_refs)
            for o_ref, v, (_, kind) in zip(o_refs, vals, outs):
                if kind == "tile":
                    o_ref[:, cols] = v.astype(o_ref.dtype)
                else:
                    @pl.when(i == 0)
                    def _(o_ref=o_ref, v=v):
                        o_ref[:, cols] = v.astype(o_ref.dtype)

                    @pl.when(i > 0)
                    def _(o_ref=o_ref, v=v):
                        o_ref[:, cols] += v.astype(o_ref.dtype)

        if nk == 1:
            lhs = a_ref[...].astype(BF16)
            for c0 in range(0, tn, tcn):
                cols = pl.ds(c0, tcn)
                finish(product(b_ref, cols, lhs), cols)
            return
        whole = pl.ds(0, tn)
        acc_ref = refs[-1]
        kk = pl.program_id(2)

        @pl.when(kk == 0)
        def _():
            acc_ref[...] = product(b_ref)

        @pl.when(jnp.logical_and(kk > 0, kk < nk - 1))
        def _():
            acc_ref[...] += product(b_ref)

        @pl.when(kk == nk - 1)
        def _():
            finish(acc_ref[...] + product(b_ref), whole)

    b_spec = pl.BlockSpec((tn, tk), lambda j, i, kk: (j, kk)) if tb else pl.BlockSpec((tk, tn), lambda j, i, kk: (kk, j))
    ins = [a, b]
    in_specs = [pl.BlockSpec((tk, tm), lambda j, i, kk: (kk, i)) if ta else pl.BlockSpec((tm, tk), lambda j, i, kk: (i, kk)),
                b_spec]
    tile_spec = pl.BlockSpec((tm, tn), lambda j, i, kk: (i, j))
    byt = 2 * tm * tk * a.dtype.itemsize + 2 * tk * tn * b.dtype.itemsize
    byt += (2 + has_b2) * 4 * tm * tn
    if has_a2:
        ins.append(a2)
        in_specs.append(pl.BlockSpec((tm, a2.shape[1]), lambda j, i, kk: (i, 0)))
        byt += 2 * tm * a2.shape[1] * a2.dtype.itemsize
    if has_b2:
        ins.append(b2)
        in_specs.append(pl.BlockSpec((b2.shape[0], tn), lambda j, i, kk: (0, j)))
        byt += 2 * b2.shape[0] * tn * b2.dtype.itemsize
    if has_bias:
        ins.append(bias)
        in_specs.append(pl.BlockSpec((1, tn), lambda j, i, kk: (0, j)))
    if has_res:
        ins.append(res)
        in_specs.append(tile_spec)
        byt += 2 * tm * tn * res.dtype.itemsize
    for arr, kind, arg in extras:
        ins.append(arr)
        if kind == "tile":
            in_specs.append(pl.BlockSpec((tm, tn), lambda j, i, kk, arg=arg: (i, j + arg)))
            byt += 2 * tm * tn * arr.dtype.itemsize + 4 * tm * tn
        elif kind == "col":
            in_specs.append(pl.BlockSpec((arr.shape[0], tn), lambda j, i, kk, arg=arg: (0, j + arg)))
        else:
            in_specs.append(pl.BlockSpec(arg[0], lambda j, i, kk, im=arg[1]: im(j, i)))
    out_specs = []
    for sds, kind in outs:
        if kind == "tile":
            out_specs.append(tile_spec)
            byt += 2 * tm * tn * jnp.dtype(sds.dtype).itemsize
        else:
            out_specs.append(pl.BlockSpec((sds.shape[0], tn), lambda j, i, kk: (0, j)))
    res_ = pl.pallas_call(
        body, out_shape=tuple(o[0] for o in outs), grid=(n // tn, m // tm, nk), in_specs=in_specs,
        out_specs=tuple(out_specs), scratch_shapes=[pltpu.VMEM((tm, tn), F32)] if nk > 1 else [], name=name,
        compiler_params=_cparams(("arbitrary", "arbitrary", "arbitrary"), byt / 2**20 + (8 if epilogue is None else 20)),
    )(*ins)
    return res_[0] if epilogue is None else res_


def _ew(fn, ins, outs, grid, name, vmem_mb=40):
    n_in = len(ins)
    accs = [o[3] for o in outs]

    def body(*refs):
        c, r = pl.program_id(0), pl.program_id(1)
        vals = fn(c, r, *[ref[...].astype(F32) for ref in refs[:n_in]])
        for o_ref, v, acc in zip(refs[n_in:], vals, accs):
            if acc is None:
                o_ref[...] = v.astype(o_ref.dtype)
            else:
                first = (r == 0) if acc == "r" else jnp.logical_and(r == 0, c == 0)

                @pl.when(first)
                def _(o_ref=o_ref, v=v):
                    o_ref[...] = v.astype(o_ref.dtype)

                @pl.when(jnp.logical_not(first))
                def _(o_ref=o_ref, v=v):
                    o_ref[...] += v.astype(o_ref.dtype)

    res = pl.pallas_call(
        body, out_shape=tuple(o[0] for o in outs), grid=grid,
        in_specs=[pl.BlockSpec(bs, im) for _, bs, im in ins],
        out_specs=tuple(pl.BlockSpec(bs, im) for _, bs, im, _ in outs), name=name,
        compiler_params=_cparams(("arbitrary", "arbitrary"), vmem_mb),
    )(*[a for a, _, _ in ins])
    return res


def _rc(tm, tc, coff=0):
    return (tm, tc), (lambda c, r: (r, c + coff))


def _col(rows, tc, coff=0):
    return (rows, tc), (lambda c, r: (0, c + coff))


def _gelu(x):
    return 0.5 * x * (1.0 + lax.erf(x * (2.0 ** -0.5)))


def _gelu_and_grad(x):
    cdf = 0.5 * (1.0 + lax.erf(x * (2.0 ** -0.5)))
    return x * cdf, cdf + x * jnp.exp(-0.5 * x * x) * (1.0 / math.sqrt(2.0 * math.pi))


def _gelu_grad(x):
    return _gelu_and_grad(x)[1]


def _sigmoid(x):
    return 1.0 / (1.0 + jnp.exp(-x))


def _shift_rows(x, halo, s):
    rolled = pltpu.roll(x, s, 0)
    row8 = lax.broadcasted_iota(jnp.int32, halo.shape, 0)
    head = jnp.where(row8 < s, pltpu.roll(halo, s, 0), rolled[0:8])
    return jnp.concatenate([head, rolled[8:]], axis=0)


def _shift_rows_up(x, halo, s):
    tm = x.shape[0]
    rolled = pltpu.roll(x, tm - s, 0)
    row8 = lax.broadcasted_iota(jnp.int32, halo.shape, 0)
    tail = jnp.where(row8 >= 8 - s, pltpu.roll(halo, 8 - s, 0), rolled[tm - 8:])
    return jnp.concatenate([rolled[:tm - 8], tail], axis=0)


def _rmsnorm_fwd(x, g, name, tm=256):
    L = x.shape[0]

    def fn(c, r, xb, gb):
        rstd = lax.rsqrt(jnp.mean(xb * xb, axis=-1, keepdims=True) + RMS_EPS)
        return ((xb * rstd) * gb,)

    return _ew(fn, [(x, *_rc(tm, D)), (g, *_col(1, D))], [(SDS((L, D), BF16), *_rc(tm, D), None)], (1, L // tm), name)[0]


def _rmsnorm_bwd(dh, x, g, dres, name, tm=256):
    L = x.shape[0]

    def fn(c, r, dhb, xb, gb, drb):
        rstd = lax.rsqrt(jnp.mean(xb * xb, axis=-1, keepdims=True) + RMS_EPS)
        y = xb * rstd
        dy = dhb * gb
        dx = rstd * (dy - y * jnp.mean(dy * y, axis=-1, keepdims=True))
        return drb + dx, jnp.sum(dhb * y, axis=0, keepdims=True)

    return _ew(fn, [(dh, *_rc(tm, D)), (x, *_rc(tm, D)), (g, *_col(1, D)), (dres, *_rc(tm, D))],
               [(SDS((L, D), F32), *_rc(tm, D), None), (SDS((1, D), F32), *_col(1, D), "all")], (1, L // tm), name)


def _final_loss(x2, g, tgt, name, tm=256):
    L = x2.shape[0]

    def fn(c, r, xb, gb, tb):
        rstd = lax.rsqrt(jnp.mean(xb * xb, axis=-1, keepdims=True) + RMS_EPS)
        y = xb * rstd
        err = y * gb - tb
        dout = err * (1.0 / D)
        dy = dout * gb
        dx = rstd * (dy - y * jnp.mean(dy * y, axis=-1, keepdims=True))
        return dx, dx, jnp.sum(err * err, axis=0, keepdims=True) * (0.5 / D), jnp.sum(dout * y, axis=0, keepdims=True)

    return _ew(fn, [(x2, *_rc(tm, D)), (g, *_col(1, D)), (tgt, *_rc(tm, D))],
               [(SDS((L, D), F32), *_rc(tm, D), None), (SDS((L, D), BF16), *_rc(tm, D), None),
                (SDS((1, D), F32), *_col(1, D), "all"),
                (SDS((1, D), F32), *_col(1, D), "all")], (1, L // tm), name)


def _attn_setup(n, kvc, kvp):
    kv = jnp.concatenate([kvp, kvc], axis=0).astype(F32)
    lo = lax.broadcasted_iota(jnp.int32, (2 * BLK, 128), 1) < HD

    def halves(t):
        tr = pltpu.roll(t, HD, 1)
        z = jnp.zeros_like(t)
        return {(0, 0): jnp.where(lo, t, z).astype(BF16), (0, 1): jnp.where(lo, z, tr).astype(BF16),
                (1, 0): jnp.where(lo, tr, z).astype(BF16), (1, 1): jnp.where(lo, z, t).astype(BF16)}

    kmat, vmat = halves(kv[:, :128]), halves(kv[:, 128:])
    qi = lax.broadcasted_iota(jnp.int32, (BLK, 2 * BLK), 0)
    si = lax.broadcasted_iota(jnp.int32, (BLK, 2 * BLK), 1)
    dist = qi + BLK - si
    valid = (dist >= 0) & (dist < BLK) & ((n > 0) | (si >= BLK))
    return kmat, vmat, valid, dist.astype(F32)


def _attn_probs(qp, kmat_ge, valid, distf, slope, sink):
    s = lax.dot_general(qp, kmat_ge, (((1,), (1,)), ((), ())), preferred_element_type=F32) * (HD ** -0.5)
    s = jnp.where(valid, s - slope * distf, NEG_BIG)
    m = jnp.maximum(jnp.max(s, axis=-1, keepdims=True), sink)
    p = jnp.exp(s - m)
    esink = jnp.exp(sink - m)
    den = jnp.sum(p, axis=-1, keepdims=True) + esink
    return p / den, esink / den


def _slope(h):
    return 2.0 ** (-8.0 * (h + 1) / NQ)


def _attn_fwd(projb, sinks, name):
    L = projb.shape[0]
    nb = L // BLK

    def body(s_ref, q_ref, kvc_ref, kvp_ref, o_ref):
        n = pl.program_id(0)
        kmat, vmat, valid, distf = _attn_setup(n, kvc_ref[...], kvp_ref[...])
        for j in range(NQ // 2):
            g = j // (NQ // 4)
            qp = q_ref[:, 128 * j:128 * (j + 1)]
            acc = jnp.zeros((BLK, 128), F32)
            for e in range(2):
                h = 2 * j + e
                p, _ = _attn_probs(qp, kmat[(g, e)], valid, distf, _slope(h), s_ref[0, h])
                acc = acc + jnp.dot(p.astype(BF16), vmat[(g, e)], preferred_element_type=F32)
            o_ref[:, 128 * j:128 * (j + 1)] = acc.astype(BF16)

    return pl.pallas_call(
        body, out_shape=SDS((L, AW), BF16), grid=(nb,),
        in_specs=[pl.BlockSpec(memory_space=pltpu.SMEM),
                  pl.BlockSpec((BLK, AW), lambda n: (n, 0)),
                  pl.BlockSpec((BLK, 256), lambda n: (n, C_K // 256)),
                  pl.BlockSpec((BLK, 256), lambda n: (jnp.maximum(n - 1, 0), C_K // 256))],
        out_specs=pl.BlockSpec((BLK, AW), lambda n: (n, 0)), name=name,
        compiler_params=_cparams(("arbitrary",), 32),
    )(sinks, projb, projb, projb)


def _attn_bwd(projb, sinks, dattn, name):
    L = projb.shape[0]
    nb = L // BLK

    def body(s_ref, q_ref, kvc_ref, kvp_ref, do_ref, dq_ref, dcur_ref, dprev_ref, dsink_ref):
        n = pl.program_id(0)
        kmat, vmat, valid, distf = _attn_setup(n, kvc_ref[...], kvp_ref[...])
        lo128 = lax.broadcasted_iota(jnp.int32, (BLK, 128), 1) < HD
        lane = lax.broadcasted_iota(jnp.int32, (1, 128), 1)
        dk = [jnp.zeros((2 * BLK, 128), F32) for _ in range(NKV)]
        dv = [jnp.zeros((2 * BLK, 128), F32) for _ in range(NKV)]
        dsv = jnp.zeros((1, 128), F32)
        tn_dims = (((0,), (0,)), ((), ()))
        for j in range(NQ // 2):
            g = j // (NQ // 4)
            qp = q_ref[:, 128 * j:128 * (j + 1)]
            dop = do_ref[:, 128 * j:128 * (j + 1)]
            dqp = jnp.zeros((BLK, 128), F32)
            for e in range(2):
                h = 2 * j + e
                p, psink = _attn_probs(qp, kmat[(g, e)], valid, distf, _slope(h), s_ref[0, h])
                dp = lax.dot_general(dop, vmat[(g, e)], (((1,), (1,)), ((), ())), preferred_element_type=F32)
                drow = jnp.sum(p * dp, axis=-1, keepdims=True)
                ds = p * (dp - drow)
                dsv = dsv + jnp.where(lane == h, -jnp.sum(psink * drow, axis=0, keepdims=True), 0.0)
                dsb = (ds * (HD ** -0.5)).astype(BF16)
                dqp = dqp + jnp.dot(dsb, kmat[(g, e)], preferred_element_type=F32)
                half = lo128 if e == 0 else jnp.logical_not(lo128)
                zb = jnp.zeros_like(qp)
                dk[g] = dk[g] + lax.dot_general(dsb, jnp.where(half, qp, zb), tn_dims, preferred_element_type=F32)
                dv[g] = dv[g] + lax.dot_general(p.astype(BF16), jnp.where(half, dop, zb), tn_dims,
                                                preferred_element_type=F32)
            dq_ref[:, 128 * j:128 * (j + 1)] = dqp.astype(BF16)
        lo256 = lax.broadcasted_iota(jnp.int32, (2 * BLK, 128), 1) < HD
        tot = [t + pltpu.roll(t, HD, 1) for t in (dk[0], dk[1], dv[0], dv[1])]
        dkv = jnp.concatenate([jnp.where(lo256, tot[0], tot[1]), jnp.where(lo256, tot[2], tot[3])], axis=1)
        dprev_ref[...] = dkv[:BLK]
        dcur_ref[...] = dkv[BLK:]

        @pl.when(n == 0)
        def _():
            dsink_ref[...] = dsv

        @pl.when(n > 0)
        def _():
            dsink_ref[...] += dsv

    return pl.pallas_call(
        body, out_shape=(SDS((L, AW), BF16), SDS((L, 256), F32), SDS((L, 256), F32), SDS((1, 128), F32)), grid=(nb,),
        in_specs=[pl.BlockSpec(memory_space=pltpu.SMEM),
                  pl.BlockSpec((BLK, AW), lambda n: (n, 0)),
                  pl.BlockSpec((BLK, 256), lambda n: (n, C_K // 256)),
                  pl.BlockSpec((BLK, 256), lambda n: (jnp.maximum(n - 1, 0), C_K // 256)),
                  pl.BlockSpec((BLK, AW), lambda n: (n, 0))],
        out_specs=(pl.BlockSpec((BLK, AW), lambda n: (n, 0)), pl.BlockSpec((BLK, 256), lambda n: (n, 0)),
                   pl.BlockSpec((BLK, 256), lambda n: (n, 0)), pl.BlockSpec((1, 128), lambda n: (0, 0))),
        name=name, compiler_params=_cparams(("arbitrary",), 32),
    )(sinks, projb, projb, projb, dattn)


def _disc(a_re, a_im, logdt, b_re, b_im):
    dt = jnp.exp(logdt)
    mag = jnp.exp(a_re * dt)
    ab_re = mag * jnp.cos(a_im * dt)
    ab_im = mag * jnp.sin(a_im * dt)
    nr = ab_re - 1.0
    ni = ab_im
    den = a_re * a_re + a_im * a_im
    z_re = (nr * a_re + ni * a_im) / den
    z_im = (ni * a_re - nr * a_im) / den
    return ab_re, ab_im, z_re * b_re - z_im * b_im, z_re * b_im + z_im * b_re


def _group_mask():
    row = lax.broadcasted_iota(jnp.int32, (SW, NS), 0) // H
    col = lax.broadcasted_iota(jnp.int32, (SW, NS), 1) // P
    return row == col


def _block_diag(re, im):
    mask = _group_mask()
    z = jnp.zeros((SW, NS), F32)
    return jnp.concatenate([jnp.where(mask, jnp.tile(re, (G, 1)), z), jnp.where(mask, jnp.tile(im, (G, 1)), z)], axis=1)


def _block_diag_t(big):
    mask = _group_mask()
    z = jnp.zeros((SW, NS), F32)
    re = jnp.sum(jnp.where(mask, big[:, :NS], z).reshape(G, H, NS), axis=0)
    im = jnp.sum(jnp.where(mask, big[:, NS:], z).reshape(G, H, NS), axis=0)
    return re, im


def _ssm_prep(a_re, a_im, logdt, b_re, b_im, c_re, c_im, name):
    def body(are, aim, ldt, bre, bim, cre, cim, ab_ref, bm_ref, cm_ref):
        ab_re, ab_im, bb_re, bb_im = _disc(are[...], aim[...], ldt[...], bre[...], bim[...])
        ab_ref[...] = jnp.concatenate([ab_re, ab_im], axis=1)
        bm_ref[...] = _block_diag(bb_re, bb_im).astype(BF16)
        cm_ref[...] = _block_diag(cre[...], -cim[...]).astype(BF16)

    return pl.pallas_call(body, out_shape=(SDS((1, 2 * NS), F32), SDS((SW, 2 * NS), BF16), SDS((SW, 2 * NS), BF16)),
                          name=name, compiler_params=pltpu.CompilerParams(vmem_limit_bytes=48 << 20),
                          )(a_re, a_im, logdt, b_re, b_im, c_re, c_im)


def _ssm_param_bwd(a_re, a_im, logdt, b_re, b_im, dab8, dbm, dcm, name):
    def body(are, aim, ldt, bre, bim, dab_ref, dbm_ref, dcm_ref, o_are, o_aim, o_ldt, o_bre, o_bim, o_cre, o_cim):
        dab = jnp.sum(dab_ref[...], axis=0, keepdims=True)
        dbb_re, dbb_im = _block_diag_t(dbm_ref[...])
        _, vjp = jax.vjp(_disc, are[...], aim[...], ldt[...], bre[...], bim[...])
        d_are, d_aim, d_ldt, d_bre, d_bim = vjp((dab[:, :NS], dab[:, NS:], dbb_re, dbb_im))
        o_are[...], o_aim[...], o_ldt[...], o_bre[...], o_bim[...] = d_are, d_aim, d_ldt, d_bre, d_bim
        dc_re, dc_imn = _block_diag_t(dcm_ref[...])
        o_cre[...] = dc_re
        o_cim[...] = -dc_imn

    v1, vh = SDS((1, NS), F32), SDS((H, NS), F32)
    return pl.pallas_call(body, out_shape=(v1, v1, v1, vh, vh, vh, vh), name=name,
                          compiler_params=pltpu.CompilerParams(vmem_limit_bytes=56 << 20),
                          )(a_re, a_im, logdt, b_re, b_im, dab8, dbm, dcm)


def _ssm_scan(src, wmat, ab, *, reverse, ends=None, xs=None, init=None, name, tk=32):
    L = src.shape[0]
    rows = NSEG * tk
    nch = L // rows
    seg_len = L // NSEG
    n_sq = int(math.log2(seg_len))
    assert 2 ** n_sq == seg_len and L % rows == 0
    first_pass = ends is None
    with_dab = (not first_pass) and reverse
    slab = 512
    n_slab = NS // slab

    def body(*refs):
        src_ref, w_ref, ab_ref = refs[:3]
        pos = 3
        if not first_pass:
            ends_ref = refs[pos]
            pos += 1
        if with_dab:
            xs_ref, xsh_ref, init_ref = refs[pos:pos + 3]
            pos += 3
        if first_pass:
            (e_ref,) = refs[pos:pos + 1]
            pos += 1
        else:
            st_out_ref, aux_ref = refs[pos:pos + 2]
            pos += 2
        buf_ref, st_ref = refs[pos:pos + 2]
        i = pl.program_id(0)
        a_re = ab_ref[:, :NS]
        a_im = -ab_ref[:, NS:] if reverse else ab_ref[:, NS:]

        @pl.when(i == 0)
        def _():
            if first_pass:
                st_ref[...] = jnp.zeros_like(st_ref)
            else:
                pr, pi = a_re, a_im
                for _ in range(n_sq):
                    pr, pi = pr * pr - pi * pi, 2.0 * pr * pi
                zr = jnp.zeros((1, NS), F32)
                cr, ci = zr, zr
                order = list(range(NSEG - 1, -1, -1)) if reverse else list(range(NSEG))
                st_ref[order[0]:order[0] + 1, :] = jnp.zeros((1, 2 * NS), F32)
                for jprev, j in zip(order[:-1], order[1:]):
                    er, ei = ends_ref[jprev:jprev + 1, :NS], ends_ref[jprev:jprev + 1, NS:]
                    cr, ci = er + pr * cr - pi * ci, ei + pr * ci + pi * cr
                    st_ref[j:j + 1, :NS] = cr
                    st_ref[j:j + 1, NS:] = ci
                if not reverse:
                    aux_ref[...] = st_ref[...]
                else:
                    aux_ref[...] = jnp.zeros_like(aux_ref)

        buf_ref[...] = jnp.dot(src_ref[...].astype(BF16), w_ref[...], preferred_element_type=F32)

        for s in range(n_slab):
            re_sl, im_sl = pl.ds(s * slab, slab), pl.ds(NS + s * slab, slab)
            ar = jnp.broadcast_to(a_re[:, s * slab:(s + 1) * slab], (NSEG, slab))
            ai = jnp.broadcast_to(a_im[:, s * slab:(s + 1) * slab], (NSEG, slab))

            def step(t, carry, re_sl=re_sl, im_sl=im_sl, ar=ar, ai=ai):
                k = (tk - 1 - t) if reverse else t
                r0 = pl.multiple_of(k * NSEG, NSEG)
                xr, xi = carry[0], carry[1]
                nr = ar * xr - ai * xi + buf_ref[pl.ds(r0, NSEG), re_sl]
                ni = ar * xi + ai * xr + buf_ref[pl.ds(r0, NSEG), im_sl]
                if not first_pass:
                    buf_ref[pl.ds(r0, NSEG), re_sl] = nr
                    buf_ref[pl.ds(r0, NSEG), im_sl] = ni
                if not with_dab:
                    return nr, ni
                rp = pl.multiple_of((k - 1) * NSEG, NSEG)
                xpr, xpi = xs_ref[pl.ds(rp, NSEG), re_sl], xs_ref[pl.ds(rp, NSEG), im_sl]
                return nr, ni, carry[2] + nr * xpr + ni * xpi, carry[3] + ni * xpr - nr * xpi

            carry = (st_ref[:, re_sl], st_ref[:, im_sl])
            if with_dab:
                z = jnp.zeros((NSEG, slab), F32)
                carry = lax.fori_loop(0, tk - 1, step, carry + (z, z))
                xr, xi, dr, di = carry
                nr = ar * xr - ai * xi + buf_ref[pl.ds(0, NSEG), re_sl]
                ni = ar * xi + ai * xr + buf_ref[pl.ds(0, NSEG), im_sl]
                buf_ref[pl.ds(0, NSEG), re_sl] = nr
                buf_ref[pl.ds(0, NSEG), im_sl] = ni
                at_start = i == nch - 1
                xpr = jnp.where(at_start, init_ref[:, re_sl], xsh_ref[:, re_sl])
                xpi = jnp.where(at_start, init_ref[:, im_sl], xsh_ref[:, im_sl])
                aux_ref[:, re_sl] += dr + nr * xpr + ni * xpi
                aux_ref[:, im_sl] += di + ni * xpr - nr * xpi
                carry = (nr, ni)
            else:
                carry = lax.fori_loop(0, tk, step, carry)
            st_ref[:, re_sl] = carry[0]
            st_ref[:, im_sl] = carry[1]

        if first_pass:
            @pl.when(i == nch - 1)
            def _():
                e_ref[...] = st_ref[...]
        else:
            st_out_ref[...] = buf_ref[...].astype(st_out_ref.dtype)

    chunk = (lambda i: (nch - 1 - i, 0)) if reverse else (lambda i: (i, 0))
    whole = lambda i: (0, 0)
    ins = [src, wmat, ab]
    in_specs = [pl.BlockSpec((rows, SW), chunk), pl.BlockSpec((SW, 2 * NS), whole), pl.BlockSpec((1, 2 * NS), whole)]
    small = SDS((NSEG, 2 * NS), F32)
    small_spec = pl.BlockSpec((NSEG, 2 * NS), whole)
    if not first_pass:
        ins.append(ends)
        in_specs.append(small_spec)
    if with_dab:
        ins += [xs, xs, init]
        in_specs += [pl.BlockSpec((rows, 2 * NS), chunk),
                     pl.BlockSpec((NSEG, 2 * NS), lambda i: (jnp.maximum((nch - 1 - i) * tk - 1, 0), 0)),
                     small_spec]
    if first_pass:
        out_shape, out_specs = small, small_spec
    else:
        out_shape = (SDS((L, 2 * NS), BF16 if reverse else F32), small)
        out_specs = (pl.BlockSpec((rows, 2 * NS), chunk), small_spec)
    return pl.pallas_call(
        body, out_shape=out_shape, grid=(nch,), in_specs=in_specs, out_specs=out_specs,
        scratch_shapes=[pltpu.VMEM((rows, 2 * NS), F32), pltpu.VMEM((NSEG, 2 * NS), F32)], name=name,
        compiler_params=_cparams(("arbitrary",), 56),
    )(*ins)


def _to_segments(a):
    L, c = a.shape
    return a.reshape(NSEG, L // NSEG, c).transpose(1, 0, 2).reshape(L, c)


def _from_segments(a):
    L, c = a.shape
    return a.reshape(L // NSEG, NSEG, c).transpose(1, 0, 2).reshape(L, c)


def _peer(x, y, c, m):
    return ((1 - x) if (m >> 2) & 1 else x, (1 - y) if (m >> 1) & 1 else y, (1 - c) if m & 1 else c)


def _dev_index(p):
    return 4 * p[0] + 2 * p[1] + p[2]


def _exchange(arrs, scatter, name):
    n = len(arrs)

    def body(*refs):
        ins, outs = refs[:n], refs[n:2 * n]
        send_sems, recv_sems, loc_sems = refs[2 * n:]
        x, y, c = lax.axis_index("x"), lax.axis_index("y"), lax.axis_index("c")
        me = _dev_index((x, y, c))

        def src(w, to):
            return ins[w].at[to] if scatter else ins[w]

        def local(w):
            return pltpu.make_async_copy(src(w, me), outs[w].at[me], loc_sems.at[w])

        def remote(w, m):
            peer = _peer(x, y, c, m)
            return pltpu.make_async_remote_copy(src_ref=src(w, _dev_index(peer)), dst_ref=outs[w].at[me],
                                                send_sem=send_sems.at[w, m - 1], recv_sem=recv_sems.at[w, m - 1],
                                                device_id=peer, device_id_type=pl.DeviceIdType.MESH)

        def arrival(w, m):
            peer = _peer(x, y, c, m)
            return pltpu.make_async_remote_copy(src_ref=src(w, me), dst_ref=outs[w].at[_dev_index(peer)],
                                                send_sem=send_sems.at[w, m - 1], recv_sem=recv_sems.at[w, m - 1],
                                                device_id=peer, device_id_type=pl.DeviceIdType.MESH)

        for w in range(n):
            local(w).start()
        for w in range(n):
            for m in range(1, N_DEV):
                remote(w, m).start()
        for w in range(n):
            for m in range(1, N_DEV):
                arrival(w, m).wait_recv()
        for w in range(n):
            for m in range(1, N_DEV):
                remote(w, m).wait_send()
        for w in range(n):
            local(w).wait()

    anyspec = pl.BlockSpec(memory_space=pl.ANY)
    out_shape = tuple(SDS(a.shape if scatter else (N_DEV,) + a.shape, a.dtype) for a in arrs)
    return pl.pallas_call(
        body, out_shape=out_shape, in_specs=[anyspec] * n, out_specs=tuple([anyspec] * n),
        scratch_shapes=[pltpu.SemaphoreType.DMA((n, N_DEV - 1)), pltpu.SemaphoreType.DMA((n, N_DEV - 1)),
                        pltpu.SemaphoreType.DMA((n,))],
        name=name, compiler_params=pltpu.CompilerParams(has_side_effects=True),
    )(*arrs)


_HBM = pl.BlockSpec(memory_space=pltpu.HBM)
_SEM = pl.BlockSpec(memory_space=pltpu.SEMAPHORE)
_EFFECT = pltpu.SideEffectType.DATAFLOW_SIDE_EFFECTING


def _sem_index(w, m):
    return w * (N_DEV - 1) + m - 1


def _exchange_start(arrs, scatter, name):
    n = len(arrs)
    lands = [lax.empty(a.shape if scatter else (N_DEV,) + a.shape, a.dtype) for a in arrs]

    def body(*refs):
        ins, zones = refs[:n], refs[n:2 * n]
        send_sems, recv_sems = refs[2 * n], refs[2 * n + 1]
        token = refs[-1]
        x, y, c = lax.axis_index("x"), lax.axis_index("y"), lax.axis_index("c")
        me = _dev_index((x, y, c))
        for w in range(n):
            for m in range(1, N_DEV):
                peer = _peer(x, y, c, m)
                pltpu.make_async_remote_copy(
                    src_ref=ins[w].at[_dev_index(peer)] if scatter else ins[w], dst_ref=zones[w].at[me],
                    send_sem=send_sems.at[_sem_index(w, m)], recv_sem=recv_sems.at[_sem_index(w, m)],
                    device_id=peer, device_id_type=pl.DeviceIdType.MESH).start()
        token[...] = jnp.zeros_like(token)

    sems = pltpu.SemaphoreType.DMA((n * (N_DEV - 1),))
    res = pl.pallas_call(
        body, name=name,
        out_shape=(sems, sems, *[pltpu.HBM(a.shape, a.dtype) for a in arrs], *[pltpu.HBM(z.shape, z.dtype) for z in lands],
                   SDS((8, 128), F32)),
        in_specs=[_HBM] * (2 * n), out_specs=(_SEM, _SEM, *([_HBM] * (2 * n)), pl.BlockSpec(memory_space=pltpu.VMEM)),
        input_output_aliases={i: 2 + i for i in range(2 * n)},
        compiler_params=pltpu.CompilerParams(has_side_effects=_EFFECT),
    )(*[pltpu.with_memory_space_constraint(a, pltpu.HBM) for a in arrs],
      *[pltpu.with_memory_space_constraint(z, pltpu.HBM) for z in lands])
    return (res[0], res[1], list(res[2:2 + n]), list(res[2 + n:2 + 2 * n])), res[-1]


def _exchange_wait(handle, after, scatter, name):
    send_sems, recv_sems, thru, lands = handle
    n = len(thru)

    def body(*refs):
        ins, zones = refs[:n], refs[n:2 * n]
        send_sems, recv_sems = refs[2 * n], refs[2 * n + 1]
        x, y, c = lax.axis_index("x"), lax.axis_index("y"), lax.axis_index("c")
        me = _dev_index((x, y, c))
        for w in range(n):
            for m in range(1, N_DEV):
                peer = _peer(x, y, c, m)
                copy = pltpu.make_async_remote_copy(
                    src_ref=ins[w].at[me] if scatter else ins[w], dst_ref=zones[w].at[_dev_index(peer)],
                    send_sem=send_sems.at[_sem_index(w, m)], recv_sem=recv_sems.at[_sem_index(w, m)],
                    device_id=peer, device_id_type=pl.DeviceIdType.MESH)
                copy.wait_send()
                copy.wait_recv()

    res = pl.pallas_call(
        body, name=name,
        out_shape=(*[pltpu.HBM(a.shape, a.dtype) for a in thru], *[pltpu.HBM(z.shape, z.dtype) for z in lands]),
        in_specs=[_HBM] * (2 * n) + [_SEM, _SEM, pl.BlockSpec(memory_space=pl.ANY)], out_specs=tuple([_HBM] * (2 * n)),
        input_output_aliases={i: i for i in range(2 * n)},
        compiler_params=pltpu.CompilerParams(has_side_effects=_EFFECT),
    )(*thru, *lands, send_sems, recv_sems, after)
    return list(res[:n]), list(res[n:])


def _adam_math(g, w, m, v):
    m = ADAM_B1 * m + (1.0 - ADAM_B1) * g
    v = ADAM_B2 * v + (1.0 - ADAM_B2) * (g * g)
    m_hat = m / (1.0 - ADAM_B1 ** ADAM_STEP)
    v_hat = v / (1.0 - ADAM_B2 ** ADAM_STEP)
    delta = -ADAM_LR * (m_hat / (jnp.sqrt(v_hat) + ADAM_EPS) + ADAM_WD * w)
    return delta, m, v


def _adam(parts, w, m, v, name, tr=128):
    r, c = w.shape
    tr = next(t for t in (tr, 64, 32, 16, 8) if r % t == 0)

    def fn(cc, rr, pb, wb, mb, vb):
        g = pb[0].astype(F32)
        for d in range(1, N_DEV):
            g = g + pb[d].astype(F32)
        delta, nm, nv = _adam_math(g, wb, mb, vb)
        return g, delta, nm, nv

    blk = ((tr, c), lambda cc, rr: (rr, 0))
    o = SDS((r, c), F32)
    return _ew(fn, [(parts, (N_DEV, tr, c), lambda cc, rr: (0, rr, 0)), (w, *blk), (m, *blk), (v, *blk)],
               [(o, *blk, None)] * 4, (1, r // tr), name)


_SHARDED = ("w_in", "w_glu", "w_branch_attn", "w_branch_ssm", "w_out", "w_up", "w_down")
_COL_SHARDED = ("w_in", "w_glu", "w_branch_attn", "w_branch_ssm", "w_up")
_GROUPS = {"a": ("w_in",), "b": ("w_glu", "w_branch_attn", "w_branch_ssm", "w_out"), "c": ("w_up", "w_down")}
_SMALL = ("attn_norm_g", "b_in", "attn_sinks", "ssm_a_re", "ssm_a_im", "ssm_log_dt", "ssm_b_re", "ssm_b_im",
          "ssm_c_re", "ssm_c_im", "ssm_d", "b_glu", "ffn_norm_g", "conv_w", "conv_b", "final_norm_g")
_WEIGHTS = ("attn_norm_g", "w_in", "b_in", "attn_sinks", "ssm_a_re", "ssm_a_im", "ssm_log_dt", "ssm_b_re", "ssm_b_im",
            "ssm_c_re", "ssm_c_im", "ssm_d", "w_glu", "b_glu", "w_branch_attn", "w_branch_ssm", "w_out", "ffn_norm_g",
            "w_up", "conv_w", "conv_b", "w_down", "final_norm_g")


def _unstack_cols(g):
    return g.transpose(1, 0, 2).reshape(g.shape[1], g.shape[0] * g.shape[2])


def _stack_cols(a, d=N_DEV):
    k, n = a.shape
    return a.reshape(k, d, n // d).transpose(1, 0, 2)


def _pack(arrs):
    flat = jnp.concatenate([a.reshape(-1) for a in arrs])
    pad = (-flat.shape[0]) % 1024
    return jnp.pad(flat, (0, pad)).reshape(-1, 128)


def _local_step(x, tgt, wget, small, gput):
    L = x.shape[0]
    nr = lambda tm: L // tm

    h = _rmsnorm_fwd(x, small["attn_norm_g"], "norm1")
    wts = dict(wget("a", h))
    projb = _mm(h, wts["w_in"], bias=small["b_in_p"], out_dtype=BF16, name="proj")
    proj = projb
    attn = _attn_fwd(projb, small["attn_sinks"], "attn_fwd")

    ab, bmat, cmat = _ssm_prep(small["a_re"], small["a_im"], small["logdt"], small["b_re"], small["b_im"],
                               small["c_re"], small["c_im"], "ssm_prep")
    u_seg = _to_segments(proj[:, C_U:C_PAD])
    ends_f = _ssm_scan(u_seg, bmat, ab, reverse=False, name="ssm_ends_fwd")
    xs, init_f = _ssm_scan(u_seg, bmat, ab, reverse=False, ends=ends_f, name="ssm_scan_fwd")
    y_mm = _from_segments(_mm(xs, cmat, tb=True, tm=512, tk=1024, name="ssm_out"))

    def gelu_fn(c, r, yb, ub, db):
        yv = yb + db * ub
        return yv, _gelu(yv)

    tm = 512
    y, gy = _ew(gelu_fn, [(y_mm, *_rc(tm, 256)), (proj, *_rc(tm, 256, C_U // 256)), (small["ssm_d"], *_col(1, 256))],
                [(SDS((L, SW), F32), *_rc(tm, 256), None), (SDS((L, SW), BF16), *_rc(tm, 256), None)],
                (2, nr(tm)), "ssm_gelu")
    wts.update(wget("b", gy))
    glu = _mm(gy, wts["w_glu"], bias=small["b_glu"], name="glu")

    def glu_fn(c, r, vb, gb):
        return (vb * _sigmoid(gb),)

    (ssm,) = _ew(glu_fn, [(glu, *_rc(tm, SW)), (glu, *_rc(tm, SW, 1))], [(SDS((L, SW), BF16), *_rc(tm, SW), None)],
                 (1, nr(tm)), "glu_gate")
    f32 = lambda ref, cols: ref[:, cols].astype(F32)
    tnm = 1024
    gate_tiles = [(projb, "tile", C_GA // tnm), (projb, "tile", C_GS // tnm)]

    def merge_ep(i, cols, ra, rs, ga, gs):
        return _sigmoid(f32(ga, cols)) * ra + _sigmoid(f32(gs, cols)) * rs, ra, rs

    merged, br_a, br_s = _mm(attn, wts["w_branch_attn"], a2=ssm, b2=wts["w_branch_ssm"], tm=512, tn=tnm,
                             extras=gate_tiles, epilogue=merge_ep, outs=[(SDS((L, D), BF16), "tile")] * 3,
                             name="branch_merge")
    x1 = _mm(merged, wts["w_out"], res=x, name="out_proj")
    h2 = _rmsnorm_fwd(x1, small["ffn_norm_g"], "norm2")
    wts.update(wget("c", h2))
    conv_w = wts["conv_w"]
    w_up_v, w_up_g = wts["w_up_v"], wts["w_up_g"]
    tcf = 1408
    tma = 256
    hb = 16

    def conv_gate(first, gate, halo, cw, cb):
        halo = halo * jnp.logical_not(first).astype(F32)
        g1, g2 = _shift_rows(gate, halo, 1), _shift_rows(gate, halo, 2)
        return cb + cw[2:3] * gate + cw[1:2] * g1 + cw[0:1] * g2, g1, g2

    tmu, tnu = 1024, 512

    def up_ep(i, cols, rv, rg, h2_halo, wg, cw, cb):
        halo = jnp.dot(h2_halo[...], wg[:, cols], preferred_element_type=F32)[hb - 8:]
        gl, glg = _gelu_and_grad(conv_gate(i == 0, rg, halo, cw[:, cols], cb[:, cols])[0])
        return rv, rg, rv * gl, gl, glg

    up_v, up_g, act, gelu_cg, gelu_grad_cg = _mm(
        h2, w_up_v, b2=w_up_g, tm=tmu, tn=tnu, epilogue=up_ep, outs=[(SDS((L, DFF), BF16), "tile")] * 5, name="ffn_up_act",
        extras=[(h2, "spec", ((hb, D), lambda j, i: (jnp.maximum(i * (tmu // hb) - 1, 0), 0))),
                (w_up_g, "spec", ((D, tnu), lambda j, i: (0, j))), (conv_w, "col", 0), (small["conv_b"], "col", 0)])
    x2 = _mm(act, wts["w_down"], res=x1, name="ffn_down")
    d_x2, d_x2b, loss_cols, d_gf = _final_loss(x2, small["final_norm_g"], tgt, "final_loss")
    loss = jnp.sum(loss_cols)

    dw_down = _mm(act, d_x2b, ta=True, out_dtype=BF16, tm=tcf, tk=2048, name="dw_down")
    tmd, tnd = 1024, 512

    def dact_ep(i, cols, da, _, val_ref, gate_ref, halo_ref, gl_ref, glg_ref):
        val, gate, gl = f32(val_ref, cols), f32(gate_ref, cols), f32(gl_ref, cols)
        halo = f32(halo_ref, cols)[hb - 8:] * (i > 0).astype(F32)
        g1, g2 = _shift_rows(gate, halo, 1), _shift_rows(gate, halo, 2)
        d_cg = da * val * f32(glg_ref, cols)
        row3 = lax.broadcasted_iota(jnp.int32, (3, da.shape[1]), 0)
        s0 = jnp.sum(d_cg * g2, axis=0, keepdims=True)
        s1 = jnp.sum(d_cg * g1, axis=0, keepdims=True)
        s2 = jnp.sum(d_cg * gate, axis=0, keepdims=True)
        dcw = jnp.where(row3 == 0, s0, jnp.where(row3 == 1, s1, s2))
        return da * gl, d_cg, dcw, jnp.sum(d_cg, axis=0, keepdims=True)

    d_val, d_cg, d_conv_w, d_conv_b = _mm(
        d_x2b, wts["w_down"], tb=True, tm=tmd, tn=tnd, epilogue=dact_ep, name="d_act_bwd",
        extras=[(up_v, "tile", 0), (up_g, "tile", 0),
                (up_g, "spec", ((hb, tnd), lambda j, i: (jnp.maximum(i * (tmd // hb) - 1, 0), j))),
                (gelu_cg, "tile", 0), (gelu_grad_cg, "tile", 0)],
        outs=[(SDS((L, DFF), BF16), "tile")] * 2 + [(SDS((3, DFF), F32), "colacc"), (SDS((1, DFF), F32), "colacc")])
    ncf = DFF // tcf

    def gate_bwd(c, r, dcg, halo, cw):
        halo = halo[:8] * (r < nr(tma) - 1).astype(F32)
        return (cw[2:3] * dcg + cw[1:2] * _shift_rows_up(dcg, halo, 1) + cw[0:1] * _shift_rows_up(dcg, halo, 2),)

    (d_gate,) = _ew(gate_bwd, [(d_cg, *_rc(tma, tcf)),
                               (d_cg, (hb, tcf), lambda c, r: (jnp.minimum((r + 1) * (tma // hb), L // hb - 1), c)),
                               (conv_w, *_col(3, tcf))],
                    [(SDS((L, DFF), BF16), *_rc(tma, tcf), None)], (ncf, nr(tma)), "ffn_gate_bwd")
    d_h2 = _mm(d_val, w_up_v, tb=True, name="d_h2_val")
    d_h2 = _mm(d_gate, w_up_g, tb=True, res=d_h2, name="d_h2_gate")
    dw_up_v = _mm(h2, d_val, ta=True, out_dtype=BF16, tn=tcf, tk=2048, name="dw_up_val")
    dw_up_g = _mm(h2, d_gate, ta=True, out_dtype=BF16, tn=tcf, tk=2048, name="dw_up_gate")
    tok = gput("c", {"w_up_v": dw_up_v, "w_up_g": dw_up_g, "w_down": dw_down})
    d_x1, d_g2 = _rmsnorm_bwd(d_h2, x1, small["ffn_norm_g"] + tok[0, 0], d_x2, "norm2_bwd")

    dw_out = _mm(merged, d_x1, ta=True, out_dtype=BF16, name="dw_out")

    def dmerge_ep(i, cols, dm, _, a_ref, s_ref, ga, gs):
        sa, ss = _sigmoid(f32(ga, cols)), _sigmoid(f32(gs, cols))
        return dm * sa, dm * ss, dm * f32(a_ref, cols) * (sa * (1.0 - sa)), dm * f32(s_ref, cols) * (ss * (1.0 - ss))

    d_bra, d_brs, d_ga, d_gs = _mm(d_x1, wts["w_out"], tb=True, tm=512, tn=tnm, epilogue=dmerge_ep, name="d_merged_bwd",
                                   extras=[(br_a, "tile", 0), (br_s, "tile", 0)] + gate_tiles,
                                   outs=[(SDS((L, D), BF16), "tile")] * 4)
    d_attn = _mm(d_bra, wts["w_branch_attn"], tb=True, out_dtype=BF16, name="d_attn")
    dw_ba = _mm(attn, d_bra, ta=True, out_dtype=BF16, name="dw_branch_attn")
    d_ssm = _mm(d_brs, wts["w_branch_ssm"], tb=True, name="d_ssm")
    dw_bs = _mm(ssm, d_brs, ta=True, out_dtype=BF16, name="dw_branch_ssm")
    dq, dkv_cur, dkv_prev, d_sinks = _attn_bwd(projb, small["attn_sinks"], d_attn, "attn_bwd")

    def glu_bwd(c, r, ds, vb, gb):
        sg = _sigmoid(gb)
        return ds * sg, ds * vb * (sg * (1.0 - sg))

    d_glu_v, d_glu_g = _ew(glu_bwd, [(d_ssm, *_rc(tm, SW)), (glu, *_rc(tm, SW)), (glu, *_rc(tm, SW, 1))],
                           [(SDS((L, SW), F32), *_rc(tm, SW), None)] * 2, (1, nr(tm)), "glu_gate_bwd")
    d_glu = jnp.concatenate([d_glu_v, d_glu_g], axis=1)
    d_gy = _mm(d_glu, wts["w_glu"], tb=True, name="d_gelu_y")
    dw_glu = _mm(gy, d_glu, ta=True, out_dtype=BF16, name="dw_glu")

    tok = gput("b", {"w_glu": dw_glu, "w_branch_attn": dw_ba, "w_branch_ssm": dw_bs, "w_out": dw_out})
    ab = ab + tok[0, 0]

    def gelu_bwd(c, r, dg, yb, ub, dgl):
        dy = dg * _gelu_grad(yb)
        return dy, jnp.sum(dy * ub, axis=0, keepdims=True), jnp.sum(dgl, axis=0, keepdims=True)

    dy, d_ssm_d, d_b_glu = _ew(
        gelu_bwd, [(d_gy, *_rc(tm, 256)), (y, *_rc(tm, 256)), (proj, *_rc(tm, 256, C_U // 256)), (d_glu, *_rc(tm, 512))],
        [(SDS((L, SW), F32), *_rc(tm, 256), None), (SDS((1, SW), F32), *_col(1, 256), "r"),
         (SDS((1, 2 * SW), F32), *_col(1, 512), "r")], (2, nr(tm)), "ssm_gelu_bwd")

    dy_seg = _to_segments(dy)
    ends_r = _ssm_scan(dy_seg, cmat, ab, reverse=True, name="ssm_ends_bwd")
    lam, dab8 = _ssm_scan(dy_seg, cmat, ab, reverse=True, ends=ends_r, xs=xs, init=init_f, name="ssm_scan_bwd")
    du_mm = _from_segments(_mm(lam, bmat, tb=True, tm=512, tk=1024, name="ssm_du"))
    dbm = _mm(u_seg, lam, ta=True, tm=512, name="ssm_dbmat")
    dcm = _mm(dy_seg, xs, ta=True, tm=512, name="ssm_dcmat")
    d_are, d_aim, d_ldt, d_bre, d_bim, d_cre, d_cim = _ssm_param_bwd(
        small["a_re"], small["a_im"], small["logdt"], small["b_re"], small["b_im"], dab8, dbm, dcm, "ssm_param_bwd")

    nb = L // BLK

    def dproj_fn(c, r, dqb, cur, prv, du, dyb, dsk, dga, dgs):
        dkv = cur + prv * (r < nb - 1).astype(F32)
        dub = du + dsk * dyb
        full = jnp.concatenate([dqb, dkv, dub, jnp.zeros((BLK, C_GA - C_PAD), F32), dga, dgs], axis=1)
        return full, jnp.sum(full, axis=0, keepdims=True)

    rowb = lambda w: ((BLK, w), lambda c, r: (r, 0))
    dproj, d_b_in = _ew(
        dproj_fn, [(dq, *rowb(AW)), (dkv_cur, *rowb(256)),
                   (dkv_prev, (BLK, 256), lambda c, r: (jnp.minimum(r + 1, nb - 1), 0)),
                   (du_mm, *rowb(SW)), (dy, *rowb(SW)), (small["ssm_d"], *_col(1, SW)), (d_ga, *rowb(D)), (d_gs, *rowb(D))],
        [(SDS((L, INP), BF16), *rowb(INP), None), (SDS((1, INP), F32), *_col(1, INP), "all")], (1, nb), "dproj")
    dw_in = _mm(h, dproj, ta=True, out_dtype=BF16, name="dw_in")
    tok = gput("a", {"w_in": _unpad_cols(dw_in)})
    d_h = _mm(dproj, wts["w_in"], tb=True, bias=jnp.zeros((1, D), F32) + tok[0, 0], name="d_h")
    grad_x, d_g1 = _rmsnorm_bwd(d_h, x, small["attn_norm_g"], d_x1, "norm1_bwd")

    sgrads = {"attn_norm_g": d_g1, "b_in": _unpad_cols(d_b_in), "attn_sinks": d_sinks[:, :NQ], "a_re": d_are, "a_im": d_aim,
              "logdt": d_ldt, "b_re": d_bre, "b_im": d_bim, "c_re": d_cre, "c_im": d_cim, "ssm_d": d_ssm_d,
              "b_glu": d_b_glu, "ffn_norm_g": d_g2, "conv_w": d_conv_w, "conv_b": d_conv_b, "final_norm_g": d_gf}
    return loss, grad_x, sgrads


def _small_layouts(p):
    gp = lambda a: a.reshape(1, NS)
    hgp = lambda a: a.transpose(2, 0, 1).reshape(H, NS)
    chgp = lambda a: a.transpose(1, 0, 2).reshape(H, NS)
    return {
        "attn_norm_g": p["attn_norm_g"].reshape(1, D), "ffn_norm_g": p["ffn_norm_g"].reshape(1, D),
        "final_norm_g": p["final_norm_g"].reshape(1, D),
        "b_in_p": _pad_cols(p["b_in"].reshape(1, INC)),
        "attn_sinks": p["attn_sinks"].reshape(1, NQ),
        "a_re": gp(p["ssm_a_re"]), "a_im": gp(p["ssm_a_im"]), "logdt": jnp.repeat(p["ssm_log_dt"], P).reshape(1, NS),
        "b_re": hgp(p["ssm_b_re"]), "b_im": hgp(p["ssm_b_im"]), "c_re": chgp(p["ssm_c_re"]), "c_im": chgp(p["ssm_c_im"]),
        "ssm_d": p["ssm_d"].reshape(1, SW), "b_glu": p["b_glu"].reshape(1, 2 * SW),
        "conv_b": p["conv_b"].reshape(1, DFF),
    }


def _small_grads_to_param_shapes(sg):
    from_hgp = lambda a: a.reshape(H, G, P).transpose(1, 2, 0)
    from_chgp = lambda a: a.reshape(H, G, P).transpose(1, 0, 2)
    return {
        "attn_norm_g": sg["attn_norm_g"].reshape(D), "b_in": sg["b_in"].reshape(INC),
        "attn_sinks": sg["attn_sinks"].reshape(NQ),
        "ssm_a_re": sg["a_re"].reshape(G, P), "ssm_a_im": sg["a_im"].reshape(G, P),
        "ssm_log_dt": jnp.sum(sg["logdt"].reshape(G, P), axis=1),
        "ssm_b_re": from_hgp(sg["b_re"]), "ssm_b_im": from_hgp(sg["b_im"]),
        "ssm_c_re": from_chgp(sg["c_re"]), "ssm_c_im": from_chgp(sg["c_im"]),
        "ssm_d": sg["ssm_d"].reshape(SW), "b_glu": sg["b_glu"].reshape(2 * SW),
        "ffn_norm_g": sg["ffn_norm_g"].reshape(D), "conv_w": sg["conv_w"], "conv_b": sg["conv_b"].reshape(DFF),
        "final_norm_g": sg["final_norm_g"].reshape(D),
    }


def kernel(x, attn_norm_g, w_in, b_in, attn_sinks, ssm_a_re, ssm_a_im, ssm_log_dt, ssm_b_re, ssm_b_im, ssm_c_re, ssm_c_im, ssm_d, w_glu, b_glu, w_branch_attn, w_branch_ssm, w_out, ffn_norm_g, w_up, conv_w, conv_b, w_down, final_norm_g, loss_target, m_attn_norm_g, m_w_in, m_b_in, m_attn_sinks, m_ssm_a_re, m_ssm_a_im, m_ssm_log_dt, m_ssm_b_re, m_ssm_b_im, m_ssm_c_re, m_ssm_c_im, m_ssm_d, m_w_glu, m_b_glu, m_w_branch_attn, m_w_branch_ssm, m_w_out, m_ffn_norm_g, m_w_up, m_conv_w, m_conv_b, m_w_down, m_final_norm_g, v_attn_norm_g, v_w_in, v_b_in, v_attn_sinks, v_ssm_a_re, v_ssm_a_im, v_ssm_log_dt, v_ssm_b_re, v_ssm_b_im, v_ssm_c_re, v_ssm_c_im, v_ssm_d, v_w_glu, v_b_glu, v_w_branch_attn, v_w_branch_ssm, v_w_out, v_ffn_norm_g, v_w_up, v_conv_w, v_conv_b, v_w_down, v_final_norm_g):
    args = dict(locals())
    sq = lambda a: a if a.ndim == 1 else a[0]
    wv = {n: sq(args[n]) for n in _WEIGHTS}
    mv = {n: sq(args["m_" + n]) for n in _WEIGHTS}
    vv = {n: sq(args["v_" + n]) for n in _WEIGHTS}
    me = 4 * lax.axis_index("x") + 2 * lax.axis_index("y") + lax.axis_index("c")

    gather, tok = {}, jnp.zeros((8, 128), F32)
    for grp in ("a", "b", "c"):
        shards = [(wv[n] + tok[0, 0]).astype(BF16) for n in _GROUPS[grp]]
        if grp == "c":
            shards.append(jnp.pad(wv["conv_w"] + tok[0, 0], ((0, 5), (0, 64))))
        gather[grp], tok = _exchange_start(shards, False, "gather_start_" + grp)
    small = _small_layouts(wv)
    small["attn_norm_g"] = small["attn_norm_g"] + tok[0, 0]

    def own_slot(land, src):
        return lax.dynamic_update_slice_in_dim(land, src, me, axis=0)

    def wget(grp, after):
        thru, lands = _exchange_wait(gather[grp], after, False, "gather_wait_" + grp)
        full = {}
        for n, t, g in zip(_GROUPS[grp], thru, lands):
            g = own_slot(g, t[None])
            full[n] = _unstack_cols(g) if n in _COL_SHARDED else g.reshape(N_DEV * g.shape[1], g.shape[2])
        if grp == "a":
            full["w_in"] = _pad_cols(full["w_in"])
        if grp == "c":
            full["conv_w"] = _unstack_cols(own_slot(lands[-1], thru[-1][None])[:, :3, :DFF // N_DEV])
            g = own_slot(lands[0], thru[0][None])
            full["w_up_v"], full["w_up_g"] = _unstack_cols(g[:N_DEV // 2]), _unstack_cols(g[N_DEV // 2:])
            del full["w_up"]
        return full

    scatter = {}

    def gput(grp, grads):
        stacked = [_stack_cols(grads[n]) if n in _COL_SHARDED else grads[n].reshape(N_DEV, -1, D)
                   for n in _GROUPS[grp] if n != "w_up"]
        if grp == "c":
            half = N_DEV // 2
            stacked.insert(0, jnp.concatenate([_stack_cols(grads["w_up_v"], half), _stack_cols(grads["w_up_g"], half)]))
        scatter[grp], token = _exchange_start(stacked, True, "scatter_start_" + grp)
        return token

    loss, grad_x, sg = _local_step(x[0], loss_target[0], wget, small, gput)
    loss = lax.psum(loss, MESH_AXES)

    sgp = _small_grads_to_param_shapes(sg)
    small_names = [n for n in _SMALL]
    packed_g = _pack([sgp[n] for n in small_names])
    (small_all,) = _exchange([packed_g], False, "gather_small_grads")

    outs_g, outs_d, outs_m, outs_v = {}, {}, {}, {}
    for grp in ("c", "b", "a"):
        thru, lands = _exchange_wait(scatter[grp], small_all, True, "scatter_wait_" + grp)
        for n, t, pt in zip(_GROUPS[grp], thru, lands):
            pt = own_slot(pt, lax.dynamic_slice_in_dim(t, me, 1, axis=0))
            outs_g[n], outs_d[n], outs_m[n], outs_v[n] = _adam(pt, wv[n], mv[n], vv[n], "adam_" + n)

    sizes = [int(math.prod(sgp[n].shape)) for n in small_names]
    offs = [0]
    for s in sizes:
        offs.append(offs[-1] + s)

    def local_part(n, a):
        if n == "conv_w":
            return lax.dynamic_slice(a, (0, me * (DFF // N_DEV)), (3, DFF // N_DEV))
        return a

    rows = packed_g.shape[0]

    def sum_fn(cc, rr, pb):
        g = pb[0]
        for d in range(1, N_DEV):
            g = g + pb[d]
        return (g,)

    (gsum,) = _ew(sum_fn, [(small_all, (N_DEV, rows, 128), lambda cc, rr: (0, 0, 0))],
                  [(SDS((rows, 128), F32), (rows, 128), lambda cc, rr: (0, 0), None)], (1, 1), "sum_small_grads")
    gflat = gsum.reshape(-1)
    gsmall = {n: local_part(n, gflat[offs[i]:offs[i + 1]].reshape(sgp[n].shape)) for i, n in enumerate(small_names)}
    pw = _pack([wv[n] for n in small_names])
    pm = _pack([mv[n] for n in small_names])
    pv = _pack([vv[n] for n in small_names])
    pg = _pack([gsmall[n] for n in small_names])
    prow = pw.shape[0]

    def adam_small(cc, rr, gb, wb, mb, vb):
        return _adam_math(gb, wb, mb, vb)

    whole = ((prow, 128), lambda cc, rr: (0, 0))
    sd, sm, sv = _ew(adam_small, [(pg, *whole), (pw, *whole), (pm, *whole), (pv, *whole)],
                     [(SDS((prow, 128), F32), *whole, None)] * 3, (1, 1), "adam_small")
    lsizes = [int(math.prod(wv[n].shape)) for n in small_names]
    loffs = [0]
    for s in lsizes:
        loffs.append(loffs[-1] + s)
    for i, n in enumerate(small_names):
        take = lambda a: a.reshape(-1)[loffs[i]:loffs[i + 1]].reshape(wv[n].shape)
        outs_g[n], outs_d[n], outs_m[n], outs_v[n] = gsmall[n], take(sd), take(sm), take(sv)

    lead = lambda n, a: a if args[n].ndim == 1 else a[None]
    grad_x = grad_x[None]
    return (loss, grad_x, *[lead(n, outs_g[n]) for n in _WEIGHTS], *[lead(n, outs_d[n]) for n in _WEIGHTS],
            *[lead(n, outs_m[n]) for n in _WEIGHTS], *[lead(n, outs_v[n]) for n in _WEIGHTS])
```

```python
import functools
import math

import jax
import jax.numpy as jnp
from jax import lax
from jax.experimental import pallas as pl
from jax.experimental.pallas import tpu as pltpu

F32 = jnp.float32
BF16 = jnp.bfloat16
SDS = jax.ShapeDtypeStruct

N_DEV = 8
D = 2048
NQ, NKV, HD = 16, 2, 64
AW = NQ * HD
BLK = 128
SW, G, H, P = 512, 32, 16, 64
NS = G * P
DFF = 5632
INC = AW + 2 * NKV * HD + SW + 2 * D
C_K, C_U, C_PAD = AW, AW + 2 * NKV * HD, AW + 2 * NKV * HD + SW
C_GA, C_GS, INP = D, 2 * D, 3 * D
RMS_EPS = 1e-6
NEG_BIG = -1e30
ADAM_LR, ADAM_B1, ADAM_B2, ADAM_EPS, ADAM_WD, ADAM_STEP = 0.001, 0.9, 0.999, 1e-08, 0.01, 10
NSEG = 8
VMEM_CAP_MB = 60
MESH_AXES = ("x", "y", "c")


def _pad_cols(a):
    zeros = jnp.zeros(a.shape[:-1] + (C_GA - C_PAD,), a.dtype)
    return jnp.concatenate([a[..., :C_PAD], zeros, a[..., C_PAD:]], axis=-1)


def _unpad_cols(a):
    return jnp.concatenate([a[..., :C_PAD], a[..., C_GA:]], axis=-1)


def _cparams(sem, vmem_mb):
    return pltpu.CompilerParams(dimension_semantics=sem, vmem_limit_bytes=min(int(vmem_mb), VMEM_CAP_MB) << 20)


LANES = 128


def _tile(dim, pref):
    if dim <= pref:
        return dim
    for t in range(pref - pref % LANES, 0, -LANES):
        if dim % t == 0:
            return t
    raise ValueError(f"no tile for {dim}")


def _mm(a, b, *, ta=False, tb=False, bias=None, res=None, out_dtype=F32, tm=1024, tn=1024, tk=3072, name,
        a2=None, b2=None, extras=(), epilogue=None, outs=None, ep_cols=None):
    m, k = (a.shape[1], a.shape[0]) if ta else a.shape
    n = b.shape[0] if tb else b.shape[1]
    assert (b.shape[1] if tb else b.shape[0]) == k, (a.shape, b.shape, ta, tb)
    tm, tn, tk = _tile(m, tm), _tile(n, tn), _tile(k, tk)
    nk = k // tk
    dims = (((0 if ta else 1,), (1 if tb else 0,)), ((), ()))
    has_bias, has_res, has_b2 = bias is not None, res is not None, b2 is not None
    has_a2 = a2 is not None
    assert not (has_b2 and (nk > 1 or ta or tb)) and not (has_a2 and not has_b2)
    if epilogue is None:
        outs = [(SDS((m, n), out_dtype), "tile")]
    n_ex, n_out = len(extras), len(outs)
    tcn = tn if (epilogue is None or nk > 1 or ep_cols is None) else _tile(tn, ep_cols)

    def body(*refs):
        a_ref, b_ref = refs[0], refs[1]
        pos = 2
        a2_ref = refs[pos] if has_a2 else a_ref
        pos += has_a2
        b2_ref = refs[pos] if has_b2 else None
        pos += has_b2
        bias_ref = refs[pos] if has_bias else None
        pos += has_bias
        res_ref = refs[pos] if has_res else None
        pos += has_res
        ex_refs = refs[pos:pos + n_ex]
        o_refs = refs[pos + n_ex:pos + n_ex + n_out]
        i = pl.program_id(1)

        def product(rhs_ref, cols=None, lhs=None):
            rhs = rhs_ref[...] if cols is None else (rhs_ref[cols, :] if tb else rhs_ref[:, cols])
            lhs = a_ref[...].astype(BF16) if lhs is None else lhs
            return lax.dot_general(lhs, rhs.astype(BF16), dims, preferred_element_type=F32)

        def finish(r, cols):
            if has_bias:
                r = r + bias_ref[:, cols]
            if has_res:
                r = r + res_ref[:, cols].astype(F32)
            if epilogue is None:
                o_refs[0][:, cols] = r.astype(o_refs[0].dtype)
                return
            r2 = None
            if has_b2:
                r2 = jnp.dot(a2_ref[...].astype(BF16), b2_ref[:, cols].astype(BF16), preferred_element_type=F32)
            vals = epilogue(i, cols, r, r2, *ex_refs)
            for o_ref, v, (_, kind) in zip(o_refs, vals, outs):
                if kind == "tile":
                    o_ref[:, cols] = v.astype(o_ref.dtype)
                else:
                    @pl.when(i == 0)
                    def _(o_ref=o_ref, v=v):
                        o_ref[:, cols] = v.astype(o_ref.dtype)

                    @pl.when(i > 0)
                    def _(o_ref=o_ref, v=v):
                        o_ref[:, cols] += v.astype(o_ref.dtype)

        if nk == 1:
            lhs = a_ref[...].astype(BF16)
            for c0 in range(0, tn, tcn):
                cols = pl.ds(c0, tcn)
                finish(product(b_ref, cols, lhs), cols)
            return
        whole = pl.ds(0, tn)
        acc_ref = refs[-1]
        kk = pl.program_id(2)

        @pl.when(kk == 0)
        def _():
            acc_ref[...] = product(b_ref)

        @pl.when(jnp.logical_and(kk > 0, kk < nk - 1))
        def _():
            acc_ref[...] += product(b_ref)

        @pl.when(kk == nk - 1)
        def _():
            finish(acc_ref[...] + product(b_ref), whole)

    b_spec = pl.BlockSpec((tn, tk), lambda j, i, kk: (j, kk)) if tb else pl.BlockSpec((tk, tn), lambda j, i, kk: (kk, j))
    ins = [a, b]
    in_specs = [pl.BlockSpec((tk, tm), lambda j, i, kk: (kk, i)) if ta else pl.BlockSpec((tm, tk), lambda j, i, kk: (i, kk)),
                b_spec]
    tile_spec = pl.BlockSpec((tm, tn), lambda j, i, kk: (i, j))
    byt = 2 * tm * tk * a.dtype.itemsize + 2 * tk * tn * b.dtype.itemsize
    byt += (2 + has_b2) * 4 * tm * tn
    if has_a2:
        ins.append(a2)
        in_specs.append(pl.BlockSpec((tm, a2.shape[1]), lambda j, i, kk: (i, 0)))
        byt += 2 * tm * a2.shape[1] * a2.dtype.itemsize
    if has_b2:
        ins.append(b2)
        in_specs.append(pl.BlockSpec((b2.shape[0], tn), lambda j, i, kk: (0, j)))
        byt += 2 * b2.shape[0] * tn * b2.dtype.itemsize
    if has_bias:
        ins.append(bias)
        in_specs.append(pl.BlockSpec((1, tn), lambda j, i, kk: (0, j)))
    if has_res:
        ins.append(res)
        in_specs.append(tile_spec)
        byt += 2 * tm * tn * res.dtype.itemsize
    for arr, kind, arg in extras:
        ins.append(arr)
        if kind == "tile":
            in_specs.append(pl.BlockSpec((tm, tn), lambda j, i, kk, arg=arg: (i, j + arg)))
            byt += 2 * tm * tn * arr.dtype.itemsize + 4 * tm * tn
        elif kind == "col":
            in_specs.append(pl.BlockSpec((arr.shape[0], tn), lambda j, i, kk, arg=arg: (0, j + arg)))
        else:
            in_specs.append(pl.BlockSpec(arg[0], lambda j, i, kk, im=arg[1]: im(j, i)))
    out_specs = []
    for sds, kind in outs:
        if kind == "tile":
            out_specs.append(tile_spec)
            byt += 2 * tm * tn * jnp.dtype(sds.dtype).itemsize
        else:
            out_specs.append(pl.BlockSpec((sds.shape[0], tn), lambda j, i, kk: (0, j)))
    res_ = pl.pallas_call(
        body, out_shape=tuple(o[0] for o in outs), grid=(n // tn, m // tm, nk), in_specs=in_specs,
        out_specs=tuple(out_specs), scratch_shapes=[pltpu.VMEM((tm, tn), F32)] if nk > 1 else [], name=name,
        compiler_params=_cparams(("arbitrary", "arbitrary", "arbitrary"), byt / 2**20 + (8 if epilogue is None else 20)),
    )(*ins)
    return res_[0] if epilogue is None else res_


def _ew(fn, ins, outs, grid, name, vmem_mb=40):
    n_in = len(ins)
    accs = [o[3] for o in outs]

    def body(*refs):
        c, r = pl.program_id(0), pl.program_id(1)
        vals = fn(c, r, *[ref[...].astype(F32) for ref in refs[:n_in]])
        for o_ref, v, acc in zip(refs[n_in:], vals, accs):
            if acc is None:
                o_ref[...] = v.astype(o_ref.dtype)
            else:
                first = (r == 0) if acc == "r" else jnp.logical_and(r == 0, c == 0)

                @pl.when(first)
                def _(o_ref=o_ref, v=v):
                    o_ref[...] = v.astype(o_ref.dtype)

                @pl.when(jnp.logical_not(first))
                def _(o_ref=o_ref, v=v):
                    o_ref[...] += v.astype(o_ref.dtype)

    res = pl.pallas_call(
        body, out_shape=tuple(o[0] for o in outs), grid=grid,
        in_specs=[pl.BlockSpec(bs, im) for _, bs, im in ins],
        out_specs=tuple(pl.BlockSpec(bs, im) for _, bs, im, _ in outs), name=name,
        compiler_params=_cparams(("arbitrary", "arbitrary"), vmem_mb),
    )(*[a for a, _, _ in ins])
    return res


def _rc(tm, tc, coff=0):
    return (tm, tc), (lambda c, r: (r, c + coff))


def _col(rows, tc, coff=0):
    return (rows, tc), (lambda c, r: (0, c + coff))


def _gelu(x):
    return 0.5 * x * (1.0 + lax.erf(x * (2.0 ** -0.5)))


def _gelu_and_grad(x):
    cdf = 0.5 * (1.0 + lax.erf(x * (2.0 ** -0.5)))
    return x * cdf, cdf + x * jnp.exp(-0.5 * x * x) * (1.0 / math.sqrt(2.0 * math.pi))


def _gelu_grad(x):
    return _gelu_and_grad(x)[1]


def _sigmoid(x):
    return 1.0 / (1.0 + jnp.exp(-x))


def _shift_rows(x, halo, s):
    rolled = pltpu.roll(x, s, 0)
    row8 = lax.broadcasted_iota(jnp.int32, halo.shape, 0)
    head = jnp.where(row8 < s, pltpu.roll(halo, s, 0), rolled[0:8])
    return jnp.concatenate([head, rolled[8:]], axis=0)


def _shift_rows_up(x, halo, s):
    tm = x.shape[0]
    rolled = pltpu.roll(x, tm - s, 0)
    row8 = lax.broadcasted_iota(jnp.int32, halo.shape, 0)
    tail = jnp.where(row8 >= 8 - s, pltpu.roll(halo, 8 - s, 0), rolled[tm - 8:])
    return jnp.concatenate([rolled[:tm - 8], tail], axis=0)


def _rmsnorm_fwd(x, g, name, tm=256):
    L = x.shape[0]

    def fn(c, r, xb, gb):
        rstd = lax.rsqrt(jnp.mean(xb * xb, axis=-1, keepdims=True) + RMS_EPS)
        return ((xb * rstd) * gb,)

    return _ew(fn, [(x, *_rc(tm, D)), (g, *_col(1, D))], [(SDS((L, D), BF16), *_rc(tm, D), None)], (1, L // tm), name)[0]


def _rmsnorm_bwd(dh, x, g, dres, name, tm=256):
    L = x.shape[0]

    def fn(c, r, dhb, xb, gb, drb):
        rstd = lax.rsqrt(jnp.mean(xb * xb, axis=-1, keepdims=True) + RMS_EPS)
        y = xb * rstd
        dy = dhb * gb
        dx = rstd * (dy - y * jnp.mean(dy * y, axis=-1, keepdims=True))
        return drb + dx, jnp.sum(dhb * y, axis=0, keepdims=True)

    return _ew(fn, [(dh, *_rc(tm, D)), (x, *_rc(tm, D)), (g, *_col(1, D)), (dres, *_rc(tm, D))],
               [(SDS((L, D), F32), *_rc(tm, D), None), (SDS((1, D), F32), *_col(1, D), "all")], (1, L // tm), name)


def _final_loss(x2, g, tgt, name, tm=256):
    L = x2.shape[0]

    def fn(c, r, xb, gb, tb):
        rstd = lax.rsqrt(jnp.mean(xb * xb, axis=-1, keepdims=True) + RMS_EPS)
        y = xb * rstd
        err = y * gb - tb
        dout = err * (1.0 / D)
        dy = dout * gb
        dx = rstd * (dy - y * jnp.mean(dy * y, axis=-1, keepdims=True))
        return dx, dx, jnp.sum(err * err, axis=0, keepdims=True) * (0.5 / D), jnp.sum(dout * y, axis=0, keepdims=True)

    return _ew(fn, [(x2, *_rc(tm, D)), (g, *_col(1, D)), (tgt, *_rc(tm, D))],
               [(SDS((L, D), F32), *_rc(tm, D), None), (SDS((L, D), BF16), *_rc(tm, D), None),
                (SDS((1, D), F32), *_col(1, D), "all"),
                (SDS((1, D), F32), *_col(1, D), "all")], (1, L // tm), name)


def _slope(h):
    return 2.0 ** (-8.0 * (h + 1) / NQ)


def _attn_bias():
    qi = lax.broadcasted_iota(jnp.int32, (BLK, 2 * BLK), 0)
    si = lax.broadcasted_iota(jnp.int32, (BLK, 2 * BLK), 1)
    dist = qi + BLK - si
    band = (dist >= 0) & (dist < BLK)
    slopes = jnp.asarray([_slope(h) for h in range(NQ)], F32)[:, None, None]
    alibi = -slopes * dist.astype(F32)[None]
    return jnp.stack([jnp.where((band & (si >= BLK))[None], alibi, NEG_BIG), jnp.where(band[None], alibi, NEG_BIG)])


def _attn_kv(kvc, kvp):
    kv = jnp.concatenate([kvp, kvc], axis=0).astype(F32)
    lo = lax.broadcasted_iota(jnp.int32, (2 * BLK, 128), 1) < HD

    def halves(t):
        tr = pltpu.roll(t, HD, 1)
        z = jnp.zeros_like(t)
        return {(0, 0): jnp.where(lo, t, z).astype(BF16), (0, 1): jnp.where(lo, z, tr).astype(BF16),
                (1, 0): jnp.where(lo, tr, z).astype(BF16), (1, 1): jnp.where(lo, z, t).astype(BF16)}

    return halves(kv[:, :128]), halves(kv[:, 128:])


_NT = (((1,), (1,)), ((), ()))
_TN = (((0,), (0,)), ((), ()))
_ATTN_SPECS = [pl.BlockSpec(memory_space=pltpu.SMEM),
               pl.BlockSpec((None, NQ, BLK, 2 * BLK), lambda n: (jnp.minimum(n, 1), 0, 0, 0)),
               pl.BlockSpec((BLK, AW), lambda n: (n, 0)),
               pl.BlockSpec((BLK, 256), lambda n: (n, C_K // 256)),
               pl.BlockSpec((BLK, 256), lambda n: (jnp.maximum(n - 1, 0), C_K // 256))]


def _attn_scores(q_ref, bias_ref, kmat, sc_ref):
    for j in range(NQ // 2):
        qs = q_ref[:, 128 * j:128 * (j + 1)] * (HD ** -0.5)
        for e in range(2):
            h = 2 * j + e
            sc_ref[h] = lax.dot_general(qs, kmat[(j // (NQ // 4), e)], _NT, preferred_element_type=F32) + bias_ref[h]


def _softmax_with_sink(s, sink):
    m = jnp.maximum(jnp.max(s, axis=-1, keepdims=True), sink)
    p = jnp.exp(s - m)
    esink = jnp.exp(sink - m)
    den = jnp.sum(p, axis=-1, keepdims=True) + esink
    return p / den, esink / den


def _attn_fwd(projb, sinks, bias, name):
    L = projb.shape[0]

    def body(s_ref, bias_ref, q_ref, kvc_ref, kvp_ref, o_ref, sc_ref, pr_ref):
        kmat, vmat = _attn_kv(kvc_ref[...], kvp_ref[...])
        _attn_scores(q_ref, bias_ref, kmat, sc_ref)
        for h in range(NQ):
            pr_ref[h] = _softmax_with_sink(sc_ref[h], s_ref[0, h])[0].astype(BF16)
        for j in range(NQ // 2):
            g = j // (NQ // 4)
            acc = jnp.dot(pr_ref[2 * j], vmat[(g, 0)], preferred_element_type=F32)
            acc = acc + jnp.dot(pr_ref[2 * j + 1], vmat[(g, 1)], preferred_element_type=F32)
            o_ref[:, 128 * j:128 * (j + 1)] = acc.astype(BF16)

    return pl.pallas_call(
        body, out_shape=SDS((L, AW), BF16), grid=(L // BLK,), in_specs=_ATTN_SPECS,
        out_specs=pl.BlockSpec((BLK, AW), lambda n: (n, 0)), name=name,
        scratch_shapes=[pltpu.VMEM((NQ, BLK, 2 * BLK), F32), pltpu.VMEM((NQ, BLK, 2 * BLK), BF16)],
        compiler_params=_cparams(("arbitrary",), 32),
    )(sinks, bias, projb, projb, projb)


def _attn_bwd(projb, sinks, bias, dattn, name):
    L = projb.shape[0]

    def body(s_ref, bias_ref, q_ref, kvc_ref, kvp_ref, do_ref, dq_ref, dcur_ref, dprev_ref, dsink_ref,
             sc_ref, dp_ref, ds_ref, pr_ref):
        n = pl.program_id(0)
        kmat, vmat = _attn_kv(kvc_ref[...], kvp_ref[...])
        _attn_scores(q_ref, bias_ref, kmat, sc_ref)
        for h in range(NQ):
            j, e = h // 2, h % 2
            dp_ref[h] = lax.dot_general(do_ref[:, 128 * j:128 * (j + 1)], vmat[(j // (NQ // 4), e)], _NT,
                                        preferred_element_type=F32)
        lane = lax.broadcasted_iota(jnp.int32, (1, 128), 1)
        dsv = jnp.zeros((1, 128), F32)
        for h in range(NQ):
            p, psink = _softmax_with_sink(sc_ref[h], s_ref[0, h])
            dp = dp_ref[h]
            drow = jnp.sum(p * dp, axis=-1, keepdims=True)
            ds_ref[h] = (p * (dp - drow)).astype(BF16)
            pr_ref[h] = p.astype(BF16)
            dsv = dsv + jnp.where(lane == h, -jnp.sum(psink * drow, axis=0, keepdims=True), 0.0)
        lo128 = lax.broadcasted_iota(jnp.int32, (BLK, 128), 1) < HD
        dk = [jnp.zeros((2 * BLK, 128), F32) for _ in range(NKV)]
        dv = [jnp.zeros((2 * BLK, 128), F32) for _ in range(NKV)]
        for j in range(NQ // 2):
            g = j // (NQ // 4)
            qs = q_ref[:, 128 * j:128 * (j + 1)] * (HD ** -0.5)
            dop = do_ref[:, 128 * j:128 * (j + 1)]
            zb = jnp.zeros_like(qs)
            dqp = jnp.zeros((BLK, 128), F32)
            for e in range(2):
                h = 2 * j + e
                half = lo128 if e == 0 else jnp.logical_not(lo128)
                dqp = dqp + jnp.dot(ds_ref[h], kmat[(g, e)], preferred_element_type=F32)
                dk[g] = dk[g] + lax.dot_general(ds_ref[h], jnp.where(half, qs, zb), _TN, preferred_element_type=F32)
                dv[g] = dv[g] + lax.dot_general(pr_ref[h], jnp.where(half, dop, zb), _TN, preferred_element_type=F32)
            dq_ref[:, 128 * j:128 * (j + 1)] = (dqp * (HD ** -0.5)).astype(BF16)
        lo256 = lax.broadcasted_iota(jnp.int32, (2 * BLK, 128), 1) < HD
        tot = [t + pltpu.roll(t, HD, 1) for t in (dk[0], dk[1], dv[0], dv[1])]
        dkv = jnp.concatenate([jnp.where(lo256, tot[0], tot[1]), jnp.where(lo256, tot[2], tot[3])], axis=1)
        dprev_ref[...] = dkv[:BLK]
        dcur_ref[...] = dkv[BLK:]

        @pl.when(n == 0)
        def _():
            dsink_ref[...] = dsv

        @pl.when(n > 0)
        def _():
            dsink_ref[...] += dsv

    tile = (NQ, BLK, 2 * BLK)
    return pl.pallas_call(
        body, out_shape=(SDS((L, AW), BF16), SDS((L, 256), F32), SDS((L, 256), F32), SDS((1, 128), F32)), grid=(L // BLK,),
        in_specs=_ATTN_SPECS + [pl.BlockSpec((BLK, AW), lambda n: (n, 0))],
        out_specs=(pl.BlockSpec((BLK, AW), lambda n: (n, 0)), pl.BlockSpec((BLK, 256), lambda n: (n, 0)),
                   pl.BlockSpec((BLK, 256), lambda n: (n, 0)), pl.BlockSpec((1, 128), lambda n: (0, 0))),
        scratch_shapes=[pltpu.VMEM(tile, F32), pltpu.VMEM(tile, F32), pltpu.VMEM(tile, BF16), pltpu.VMEM(tile, BF16)],
        name=name, compiler_params=_cparams(("arbitrary",), 40),
    )(sinks, bias, projb, projb, projb, dattn)


def _disc(a_re, a_im, logdt, b_re, b_im):
    dt = jnp.exp(logdt)
    mag = jnp.exp(a_re * dt)
    ab_re = mag * jnp.cos(a_im * dt)
    ab_im = mag * jnp.sin(a_im * dt)
    nr = ab_re - 1.0
    ni = ab_im
    den = a_re * a_re + a_im * a_im
    z_re = (nr * a_re + ni * a_im) / den
    z_im = (ni * a_re - nr * a_im) / den
    return ab_re, ab_im, z_re * b_re - z_im * b_im, z_re * b_im + z_im * b_re


def _group_mask():
    row = lax.broadcasted_iota(jnp.int32, (SW, NS), 0) // H
    col = lax.broadcasted_iota(jnp.int32, (SW, NS), 1) // P
    return row == col


def _block_diag(re, im):
    mask = _group_mask()
    z = jnp.zeros((SW, NS), F32)
    return jnp.concatenate([jnp.where(mask, jnp.tile(re, (G, 1)), z), jnp.where(mask, jnp.tile(im, (G, 1)), z)], axis=1)


def _block_diag_t(big):
    mask = _group_mask()
    z = jnp.zeros((SW, NS), F32)
    re = jnp.sum(jnp.where(mask, big[:, :NS], z).reshape(G, H, NS), axis=0)
    im = jnp.sum(jnp.where(mask, big[:, NS:], z).reshape(G, H, NS), axis=0)
    return re, im


def _ssm_prep(a_re, a_im, logdt, b_re, b_im, c_re, c_im, name):
    def body(are, aim, ldt, bre, bim, cre, cim, ab_ref, bm_ref, cm_ref):
        ab_re, ab_im, bb_re, bb_im = _disc(are[...], aim[...], ldt[...], bre[...], bim[...])
        ab_ref[...] = jnp.concatenate([ab_re, ab_im], axis=1)
        bm_ref[...] = _block_diag(bb_re, bb_im).astype(BF16)
        cm_ref[...] = _block_diag(cre[...], -cim[...]).astype(BF16)

    return pl.pallas_call(body, out_shape=(SDS((1, 2 * NS), F32), SDS((SW, 2 * NS), BF16), SDS((SW, 2 * NS), BF16)),
                          name=name, compiler_params=pltpu.CompilerParams(vmem_limit_bytes=48 << 20),
                          )(a_re, a_im, logdt, b_re, b_im, c_re, c_im)


def _ssm_param_bwd(a_re, a_im, logdt, b_re, b_im, dab8, dbm, dcm, name):
    def body(are, aim, ldt, bre, bim, dab_ref, dbm_ref, dcm_ref, o_are, o_aim, o_ldt, o_bre, o_bim, o_cre, o_cim):
        dab = jnp.sum(dab_ref[...], axis=0, keepdims=True)
        dbb_re, dbb_im = _block_diag_t(dbm_ref[...])
        _, vjp = jax.vjp(_disc, are[...], aim[...], ldt[...], bre[...], bim[...])
        d_are, d_aim, d_ldt, d_bre, d_bim = vjp((dab[:, :NS], dab[:, NS:], dbb_re, dbb_im))
        o_are[...], o_aim[...], o_ldt[...], o_bre[...], o_bim[...] = d_are, d_aim, d_ldt, d_bre, d_bim
        dc_re, dc_imn = _block_diag_t(dcm_ref[...])
        o_cre[...] = dc_re
        o_cim[...] = -dc_imn

    v1, vh = SDS((1, NS), F32), SDS((H, NS), F32)
    return pl.pallas_call(body, out_shape=(v1, v1, v1, vh, vh, vh, vh), name=name,
                          compiler_params=pltpu.CompilerParams(vmem_limit_bytes=56 << 20),
                          )(a_re, a_im, logdt, b_re, b_im, dab8, dbm, dcm)


def _ssm_scan(src, wmat, ab, *, reverse, ends=None, xs=None, init=None, wproj=None, name, tk=32):
    L = src.shape[0]
    rows = NSEG * tk
    nch = L // rows
    seg_len = L // NSEG
    n_sq = int(math.log2(seg_len))
    assert 2 ** n_sq == seg_len and L % rows == 0
    first_pass = ends is None
    with_dab = (not first_pass) and reverse
    with_proj = wproj is not None
    assert not (with_proj and first_pass)
    slab = 512
    n_slab = NS // slab

    def body(*refs):
        src_ref, w_ref, ab_ref = refs[:3]
        pos = 3
        if not first_pass:
            ends_ref = refs[pos]
            pos += 1
        if with_dab:
            xs_ref, xsh_ref, init_ref = refs[pos:pos + 3]
            pos += 3
        if with_proj:
            wproj_ref = refs[pos]
            pos += 1
        if first_pass:
            (e_ref,) = refs[pos:pos + 1]
            pos += 1
        else:
            st_out_ref, aux_ref = refs[pos:pos + 2]
            pos += 2
        if with_proj:
            proj_ref = refs[pos]
            pos += 1
        buf_ref, st_ref = refs[pos:pos + 2]
        i = pl.program_id(0)
        a_re = ab_ref[:, :NS]
        a_im = -ab_ref[:, NS:] if reverse else ab_ref[:, NS:]

        @pl.when(i == 0)
        def _():
            if first_pass:
                st_ref[...] = jnp.zeros_like(st_ref)
            else:
                pr, pi = a_re, a_im
                for _ in range(n_sq):
                    pr, pi = pr * pr - pi * pi, 2.0 * pr * pi
                zr = jnp.zeros((1, NS), F32)
                cr, ci = zr, zr
                order = list(range(NSEG - 1, -1, -1)) if reverse else list(range(NSEG))
                st_ref[order[0]:order[0] + 1, :] = jnp.zeros((1, 2 * NS), F32)
                for jprev, j in zip(order[:-1], order[1:]):
                    er, ei = ends_ref[jprev:jprev + 1, :NS], ends_ref[jprev:jprev + 1, NS:]
                    cr, ci = er + pr * cr - pi * ci, ei + pr * ci + pi * cr
                    st_ref[j:j + 1, :NS] = cr
                    st_ref[j:j + 1, NS:] = ci
                if not reverse:
                    aux_ref[...] = st_ref[...]
                else:
                    aux_ref[...] = jnp.zeros_like(aux_ref)

        buf_ref[...] = jnp.dot(src_ref[...].astype(BF16), w_ref[...], preferred_element_type=F32)

        for s in range(n_slab):
            re_sl, im_sl = pl.ds(s * slab, slab), pl.ds(NS + s * slab, slab)
            ar = jnp.broadcast_to(a_re[:, s * slab:(s + 1) * slab], (NSEG, slab))
            ai = jnp.broadcast_to(a_im[:, s * slab:(s + 1) * slab], (NSEG, slab))

            def step(t, carry, re_sl=re_sl, im_sl=im_sl, ar=ar, ai=ai):
                k = (tk - 1 - t) if reverse else t
                r0 = pl.multiple_of(k * NSEG, NSEG)
                xr, xi = carry[0], carry[1]
                nr = ar * xr - ai * xi + buf_ref[pl.ds(r0, NSEG), re_sl]
                ni = ar * xi + ai * xr + buf_ref[pl.ds(r0, NSEG), im_sl]
                if not first_pass:
                    buf_ref[pl.ds(r0, NSEG), re_sl] = nr
                    buf_ref[pl.ds(r0, NSEG), im_sl] = ni
                if not with_dab:
                    return nr, ni
                rp = pl.multiple_of((k - 1) * NSEG, NSEG)
                xpr, xpi = xs_ref[pl.ds(rp, NSEG), re_sl], xs_ref[pl.ds(rp, NSEG), im_sl]
                return nr, ni, carry[2] + nr * xpr + ni * xpi, carry[3] + ni * xpr - nr * xpi

            carry = (st_ref[:, re_sl], st_ref[:, im_sl])
            if with_dab:
                z = jnp.zeros((NSEG, slab), F32)
                carry = lax.fori_loop(0, tk - 1, step, carry + (z, z))
                xr, xi, dr, di = carry
                nr = ar * xr - ai * xi + buf_ref[pl.ds(0, NSEG), re_sl]
                ni = ar * xi + ai * xr + buf_ref[pl.ds(0, NSEG), im_sl]
                buf_ref[pl.ds(0, NSEG), re_sl] = nr
                buf_ref[pl.ds(0, NSEG), im_sl] = ni
                at_start = i == nch - 1
                xpr = jnp.where(at_start, init_ref[:, re_sl], xsh_ref[:, re_sl])
                xpi = jnp.where(at_start, init_ref[:, im_sl], xsh_ref[:, im_sl])
                aux_ref[:, re_sl] += dr + nr * xpr + ni * xpi
                aux_ref[:, im_sl] += di + ni * xpr - nr * xpi
                carry = (nr, ni)
            else:
                carry = lax.fori_loop(0, tk, step, carry)
            st_ref[:, re_sl] = carry[0]
            st_ref[:, im_sl] = carry[1]

        if first_pass:
            @pl.when(i == nch - 1)
            def _():
                e_ref[...] = st_ref[...]
        else:
            st_out_ref[...] = buf_ref[...].astype(st_out_ref.dtype)
            if with_proj:
                proj_ref[...] = lax.dot_general(buf_ref[...].astype(BF16), wproj_ref[...], _NT, preferred_element_type=F32)

    chunk = (lambda i: (nch - 1 - i, 0)) if reverse else (lambda i: (i, 0))
    whole = lambda i: (0, 0)
    ins = [src, wmat, ab]
    in_specs = [pl.BlockSpec((rows, SW), chunk), pl.BlockSpec((SW, 2 * NS), whole), pl.BlockSpec((1, 2 * NS), whole)]
    small = SDS((NSEG, 2 * NS), F32)
    small_spec = pl.BlockSpec((NSEG, 2 * NS), whole)
    if not first_pass:
        ins.append(ends)
        in_specs.append(small_spec)
    if with_dab:
        ins += [xs, xs, init]
        in_specs += [pl.BlockSpec((rows, 2 * NS), chunk),
                     pl.BlockSpec((NSEG, 2 * NS), lambda i: (jnp.maximum((nch - 1 - i) * tk - 1, 0), 0)),
                     small_spec]
    if with_proj:
        ins.append(wproj)
        in_specs.append(pl.BlockSpec((SW, 2 * NS), whole))
    if first_pass:
        out_shape, out_specs = small, small_spec
    else:
        out_shape = (SDS((L, 2 * NS), BF16 if reverse else F32), small)
        out_specs = (pl.BlockSpec((rows, 2 * NS), chunk), small_spec)
        if with_proj:
            out_shape += (SDS((L, SW), F32),)
            out_specs += (pl.BlockSpec((rows, SW), chunk),)
    return pl.pallas_call(
        body, out_shape=out_shape, grid=(nch,), in_specs=in_specs, out_specs=out_specs,
        scratch_shapes=[pltpu.VMEM((rows, 2 * NS), F32), pltpu.VMEM((NSEG, 2 * NS), F32)], name=name,
        compiler_params=_cparams(("arbitrary",), 56),
    )(*ins)


def _to_segments(a):
    L, c = a.shape
    return a.reshape(NSEG, L // NSEG, c).transpose(1, 0, 2).reshape(L, c)


def _from_segments(a):
    L, c = a.shape
    return a.reshape(L // NSEG, NSEG, c).transpose(1, 0, 2).reshape(L, c)


def _peer(x, y, c, m):
    return ((1 - x) if (m >> 2) & 1 else x, (1 - y) if (m >> 1) & 1 else y, (1 - c) if m & 1 else c)


def _dev_index(p):
    return 4 * p[0] + 2 * p[1] + p[2]


def _exchange(arrs, scatter, name):
    n = len(arrs)

    def body(*refs):
        ins, outs = refs[:n], refs[n:2 * n]
        send_sems, recv_sems, loc_sems = refs[2 * n:]
        x, y, c = lax.axis_index("x"), lax.axis_index("y"), lax.axis_index("c")
        me = _dev_index((x, y, c))

        def src(w, to):
            return ins[w].at[to] if scatter else ins[w]

        def local(w):
            return pltpu.make_async_copy(src(w, me), outs[w].at[me], loc_sems.at[w])

        def remote(w, m):
            peer = _peer(x, y, c, m)
            return pltpu.make_async_remote_copy(src_ref=src(w, _dev_index(peer)), dst_ref=outs[w].at[me],
                                                send_sem=send_sems.at[w, m - 1], recv_sem=recv_sems.at[w, m - 1],
                                                device_id=peer, device_id_type=pl.DeviceIdType.MESH)

        def arrival(w, m):
            peer = _peer(x, y, c, m)
            return pltpu.make_async_remote_copy(src_ref=src(w, me), dst_ref=outs[w].at[_dev_index(peer)],
                                                send_sem=send_sems.at[w, m - 1], recv_sem=recv_sems.at[w, m - 1],
                                                device_id=peer, device_id_type=pl.DeviceIdType.MESH)

        for w in range(n):
            local(w).start()
        for w in range(n):
            for m in range(1, N_DEV):
                remote(w, m).start()
        for w in range(n):
            for m in range(1, N_DEV):
                arrival(w, m).wait_recv()
        for w in range(n):
            for m in range(1, N_DEV):
                remote(w, m).wait_send()
        for w in range(n):
            local(w).wait()

    anyspec = pl.BlockSpec(memory_space=pl.ANY)
    out_shape = tuple(SDS(a.shape if scatter else (N_DEV,) + a.shape, a.dtype) for a in arrs)
    return pl.pallas_call(
        body, out_shape=out_shape, in_specs=[anyspec] * n, out_specs=tuple([anyspec] * n),
        scratch_shapes=[pltpu.SemaphoreType.DMA((n, N_DEV - 1)), pltpu.SemaphoreType.DMA((n, N_DEV - 1)),
                        pltpu.SemaphoreType.DMA((n,))],
        name=name, compiler_params=pltpu.CompilerParams(has_side_effects=True),
    )(*arrs)


_HBM = pl.BlockSpec(memory_space=pltpu.HBM)
_SEM = pl.BlockSpec(memory_space=pltpu.SEMAPHORE)
_EFFECT = pltpu.SideEffectType.DATAFLOW_SIDE_EFFECTING


def _sem_index(w, m):
    return w * (N_DEV - 1) + m - 1


def _exchange_start(arrs, scatter, name):
    n = len(arrs)
    lands = [lax.empty(a.shape if scatter else (N_DEV,) + a.shape, a.dtype) for a in arrs]

    def body(*refs):
        ins, zones = refs[:n], refs[n:2 * n]
        send_sems, recv_sems = refs[2 * n], refs[2 * n + 1]
        token = refs[-1]
        x, y, c = lax.axis_index("x"), lax.axis_index("y"), lax.axis_index("c")
        me = _dev_index((x, y, c))
        for w in range(n):
            for m in range(1, N_DEV):
                peer = _peer(x, y, c, m)
                pltpu.make_async_remote_copy(
                    src_ref=ins[w].at[_dev_index(peer)] if scatter else ins[w], dst_ref=zones[w].at[me],
                    send_sem=send_sems.at[_sem_index(w, m)], recv_sem=recv_sems.at[_sem_index(w, m)],
                    device_id=peer, device_id_type=pl.DeviceIdType.MESH).start()
        token[...] = jnp.zeros_like(token)

    sems = pltpu.SemaphoreType.DMA((n * (N_DEV - 1),))
    res = pl.pallas_call(
        body, name=name,
        out_shape=(sems, sems, *[pltpu.HBM(a.shape, a.dtype) for a in arrs], *[pltpu.HBM(z.shape, z.dtype) for z in lands],
                   SDS((8, 128), F32)),
        in_specs=[_HBM] * (2 * n), out_specs=(_SEM, _SEM, *([_HBM] * (2 * n)), pl.BlockSpec(memory_space=pltpu.VMEM)),
        input_output_aliases={i: 2 + i for i in range(2 * n)},
        compiler_params=pltpu.CompilerParams(has_side_effects=_EFFECT),
    )(*[pltpu.with_memory_space_constraint(a, pltpu.HBM) for a in arrs],
      *[pltpu.with_memory_space_constraint(z, pltpu.HBM) for z in lands])
    return (res[0], res[1], list(res[2:2 + n]), list(res[2 + n:2 + 2 * n])), res[-1]


def _exchange_wait(handle, after, scatter, name):
    send_sems, recv_sems, thru, lands = handle
    n = len(thru)

    def body(*refs):
        ins, zones = refs[:n], refs[n:2 * n]
        send_sems, recv_sems = refs[2 * n], refs[2 * n + 1]
        x, y, c = lax.axis_index("x"), lax.axis_index("y"), lax.axis_index("c")
        me = _dev_index((x, y, c))
        for w in range(n):
            for m in range(1, N_DEV):
                peer = _peer(x, y, c, m)
                copy = pltpu.make_async_remote_copy(
                    src_ref=ins[w].at[me] if scatter else ins[w], dst_ref=zones[w].at[_dev_index(peer)],
                    send_sem=send_sems.at[_sem_index(w, m)], recv_sem=recv_sems.at[_sem_index(w, m)],
                    device_id=peer, device_id_type=pl.DeviceIdType.MESH)
                copy.wait_send()
                copy.wait_recv()

    res = pl.pallas_call(
        body, name=name,
        out_shape=(*[pltpu.HBM(a.shape, a.dtype) for a in thru], *[pltpu.HBM(z.shape, z.dtype) for z in lands]),
        in_specs=[_HBM] * (2 * n) + [_SEM, _SEM, pl.BlockSpec(memory_space=pl.ANY)], out_specs=tuple([_HBM] * (2 * n)),
        input_output_aliases={i: i for i in range(2 * n)},
        compiler_params=pltpu.CompilerParams(has_side_effects=_EFFECT),
    )(*thru, *lands, send_sems, recv_sems, after)
    return list(res[:n]), list(res[n:])


def _adam_math(g, w, m, v):
    m = ADAM_B1 * m + (1.0 - ADAM_B1) * g
    v = ADAM_B2 * v + (1.0 - ADAM_B2) * (g * g)
    m_hat = m / (1.0 - ADAM_B1 ** ADAM_STEP)
    v_hat = v / (1.0 - ADAM_B2 ** ADAM_STEP)
    delta = -ADAM_LR * (m_hat / (jnp.sqrt(v_hat) + ADAM_EPS) + ADAM_WD * w)
    return delta, m, v


def _adam(parts, w, m, v, name, tr=128):
    r, c = w.shape
    tr = next(t for t in (tr, 64, 32, 16, 8) if r % t == 0)

    def fn(cc, rr, pb, wb, mb, vb):
        g = pb[0].astype(F32)
        for d in range(1, N_DEV):
            g = g + pb[d].astype(F32)
        delta, nm, nv = _adam_math(g, wb, mb, vb)
        return g, delta, nm, nv

    blk = ((tr, c), lambda cc, rr: (rr, 0))
    o = SDS((r, c), F32)
    return _ew(fn, [(parts, (N_DEV, tr, c), lambda cc, rr: (0, rr, 0)), (w, *blk), (m, *blk), (v, *blk)],
               [(o, *blk, None)] * 4, (1, r // tr), name)


_SHARDED = ("w_in", "w_glu", "w_branch_attn", "w_branch_ssm", "w_out", "w_up", "w_down")
_COL_SHARDED = ("w_in", "w_glu", "w_branch_attn", "w_branch_ssm", "w_up")
_GROUPS = {"a": ("w_in",), "b": ("w_glu", "w_branch_attn", "w_branch_ssm", "w_out"), "c": ("w_up", "w_down")}
_SMALL = ("attn_norm_g", "b_in", "attn_sinks", "ssm_a_re", "ssm_a_im", "ssm_log_dt", "ssm_b_re", "ssm_b_im",
          "ssm_c_re", "ssm_c_im", "ssm_d", "b_glu", "ffn_norm_g", "conv_w", "conv_b", "final_norm_g")
_WEIGHTS = ("attn_norm_g", "w_in", "b_in", "attn_sinks", "ssm_a_re", "ssm_a_im", "ssm_log_dt", "ssm_b_re", "ssm_b_im",
            "ssm_c_re", "ssm_c_im", "ssm_d", "w_glu", "b_glu", "w_branch_attn", "w_branch_ssm", "w_out", "ffn_norm_g",
            "w_up", "conv_w", "conv_b", "w_down", "final_norm_g")


def _unstack_cols(g):
    return g.transpose(1, 0, 2).reshape(g.shape[1], g.shape[0] * g.shape[2])


def _stack_cols(a, d=N_DEV):
    k, n = a.shape
    return a.reshape(k, d, n // d).transpose(1, 0, 2)


def _pack(arrs):
    flat = jnp.concatenate([a.reshape(-1) for a in arrs])
    pad = (-flat.shape[0]) % 1024
    return jnp.pad(flat, (0, pad)).reshape(-1, 128)


def _local_step(x, tgt, wget, small, gput):
    L = x.shape[0]
    nr = lambda tm: L // tm

    h = _rmsnorm_fwd(x, small["attn_norm_g"], "norm1")
    wts = dict(wget("a", h))
    projb = _mm(h, wts["w_in"], bias=small["b_in_p"], out_dtype=BF16, name="proj")
    proj = projb
    attn_bias = _attn_bias()
    attn = _attn_fwd(projb, small["attn_sinks"], attn_bias, "attn_fwd")

    ab, bmat, cmat = _ssm_prep(small["a_re"], small["a_im"], small["logdt"], small["b_re"], small["b_im"],
                               small["c_re"], small["c_im"], "ssm_prep")
    u_seg = _to_segments(proj[:, C_U:C_PAD])
    ends_f = _ssm_scan(u_seg, bmat, ab, reverse=False, name="ssm_ends_fwd")
    xs, init_f, y_seg = _ssm_scan(u_seg, bmat, ab, reverse=False, ends=ends_f, wproj=cmat, name="ssm_scan_fwd")
    y_mm = _from_segments(y_seg)

    def gelu_fn(c, r, yb, ub, db):
        yv = yb + db * ub
        return yv, _gelu(yv)

    tm = 512
    y, gy = _ew(gelu_fn, [(y_mm, *_rc(tm, 256)), (proj, *_rc(tm, 256, C_U // 256)), (small["ssm_d"], *_col(1, 256))],
                [(SDS((L, SW), F32), *_rc(tm, 256), None), (SDS((L, SW), BF16), *_rc(tm, 256), None)],
                (2, nr(tm)), "ssm_gelu")
    wts.update(wget("b", gy))
    glu = _mm(gy, wts["w_glu"], bias=small["b_glu"], name="glu")

    def glu_fn(c, r, vb, gb):
        return (vb * _sigmoid(gb),)

    (ssm,) = _ew(glu_fn, [(glu, *_rc(tm, SW)), (glu, *_rc(tm, SW, 1))], [(SDS((L, SW), BF16), *_rc(tm, SW), None)],
                 (1, nr(tm)), "glu_gate")
    f32 = lambda ref, cols: ref[:, cols].astype(F32)
    tnm = 1024
    gate_tiles = [(projb, "tile", C_GA // tnm), (projb, "tile", C_GS // tnm)]

    def merge_ep(i, cols, ra, rs, ga, gs):
        return _sigmoid(f32(ga, cols)) * ra + _sigmoid(f32(gs, cols)) * rs, ra, rs

    merged, br_a, br_s = _mm(attn, wts["w_branch_attn"], a2=ssm, b2=wts["w_branch_ssm"], tm=512, tn=tnm,
                             extras=gate_tiles, epilogue=merge_ep, outs=[(SDS((L, D), BF16), "tile")] * 3,
                             name="branch_merge")
    x1 = _mm(merged, wts["w_out"], res=x, name="out_proj")
    h2 = _rmsnorm_fwd(x1, small["ffn_norm_g"], "norm2")
    wts.update(wget("c", h2))
    conv_w = wts["conv_w"]
    w_up_v, w_up_g = wts["w_up_v"], wts["w_up_g"]
    tcf = 1408
    tma = 256
    hb = 16

    def conv_gate(first, gate, halo, cw, cb):
        halo = halo * jnp.logical_not(first).astype(F32)
        g1, g2 = _shift_rows(gate, halo, 1), _shift_rows(gate, halo, 2)
        return cb + cw[2:3] * gate + cw[1:2] * g1 + cw[0:1] * g2, g1, g2

    tmu, tnu = 1024, 512

    def up_ep(i, cols, rv, rg, h2_halo, wg, cw, cb):
        halo = jnp.dot(h2_halo[...], wg[:, cols], preferred_element_type=F32)[hb - 8:]
        gl, glg = _gelu_and_grad(conv_gate(i == 0, rg, halo, cw[:, cols], cb[:, cols])[0])
        return rv, rg, rv * gl, gl, glg

    up_v, up_g, act, gelu_cg, gelu_grad_cg = _mm(
        h2, w_up_v, b2=w_up_g, tm=tmu, tn=tnu, epilogue=up_ep, outs=[(SDS((L, DFF), BF16), "tile")] * 5, name="ffn_up_act",
        extras=[(h2, "spec", ((hb, D), lambda j, i: (jnp.maximum(i * (tmu // hb) - 1, 0), 0))),
                (w_up_g, "spec", ((D, tnu), lambda j, i: (0, j))), (conv_w, "col", 0), (small["conv_b"], "col", 0)])
    x2 = _mm(act, wts["w_down"], res=x1, name="ffn_down")
    d_x2, d_x2b, loss_cols, d_gf = _final_loss(x2, small["final_norm_g"], tgt, "final_loss")
    loss = jnp.sum(loss_cols)

    dw_down = _mm(act, d_x2b, ta=True, out_dtype=BF16, tm=tcf, tk=2048, name="dw_down")
    tmd, tnd = 1024, 512

    def dact_ep(i, cols, da, _, val_ref, gate_ref, halo_ref, gl_ref, glg_ref):
        val, gate, gl = f32(val_ref, cols), f32(gate_ref, cols), f32(gl_ref, cols)
        halo = f32(halo_ref, cols)[hb - 8:] * (i > 0).astype(F32)
        g1, g2 = _shift_rows(gate, halo, 1), _shift_rows(gate, halo, 2)
        d_cg = da * val * f32(glg_ref, cols)
        row3 = lax.broadcasted_iota(jnp.int32, (3, da.shape[1]), 0)
        s0 = jnp.sum(d_cg * g2, axis=0, keepdims=True)
        s1 = jnp.sum(d_cg * g1, axis=0, keepdims=True)
        s2 = jnp.sum(d_cg * gate, axis=0, keepdims=True)
        dcw = jnp.where(row3 == 0, s0, jnp.where(row3 == 1, s1, s2))
        return da * gl, d_cg, dcw, jnp.sum(d_cg, axis=0, keepdims=True)

    d_val, d_cg, d_conv_w, d_conv_b = _mm(
        d_x2b, wts["w_down"], tb=True, tm=tmd, tn=tnd, epilogue=dact_ep, name="d_act_bwd",
        extras=[(up_v, "tile", 0), (up_g, "tile", 0),
                (up_g, "spec", ((hb, tnd), lambda j, i: (jnp.maximum(i * (tmd // hb) - 1, 0), j))),
                (gelu_cg, "tile", 0), (gelu_grad_cg, "tile", 0)],
        outs=[(SDS((L, DFF), BF16), "tile")] * 2 + [(SDS((3, DFF), F32), "colacc"), (SDS((1, DFF), F32), "colacc")])
    ncf = DFF // tcf

    def gate_bwd(c, r, dcg, halo, cw):
        halo = halo[:8] * (r < nr(tma) - 1).astype(F32)
        return (cw[2:3] * dcg + cw[1:2] * _shift_rows_up(dcg, halo, 1) + cw[0:1] * _shift_rows_up(dcg, halo, 2),)

    (d_gate,) = _ew(gate_bwd, [(d_cg, *_rc(tma, tcf)),
                               (d_cg, (hb, tcf), lambda c, r: (jnp.minimum((r + 1) * (tma // hb), L // hb - 1), c)),
                               (conv_w, *_col(3, tcf))],
                    [(SDS((L, DFF), BF16), *_rc(tma, tcf), None)], (ncf, nr(tma)), "ffn_gate_bwd")
    d_h2 = _mm(d_val, w_up_v, tb=True, name="d_h2_val")
    d_h2 = _mm(d_gate, w_up_g, tb=True, res=d_h2, name="d_h2_gate")
    dw_up_v = _mm(h2, d_val, ta=True, out_dtype=BF16, tn=tcf, tk=2048, name="dw_up_val")
    dw_up_g = _mm(h2, d_gate, ta=True, out_dtype=BF16, tn=tcf, tk=2048, name="dw_up_gate")
    tok = gput("c", {"w_up_v": dw_up_v, "w_up_g": dw_up_g, "w_down": dw_down})
    d_x1, d_g2 = _rmsnorm_bwd(d_h2, x1, small["ffn_norm_g"] + tok[0, 0], d_x2, "norm2_bwd")

    dw_out = _mm(merged, d_x1, ta=True, out_dtype=BF16, name="dw_out")

    def dmerge_ep(i, cols, dm, _, a_ref, s_ref, ga, gs):
        sa, ss = _sigmoid(f32(ga, cols)), _sigmoid(f32(gs, cols))
        return dm * sa, dm * ss, dm * f32(a_ref, cols) * (sa * (1.0 - sa)), dm * f32(s_ref, cols) * (ss * (1.0 - ss))

    d_bra, d_brs, d_ga, d_gs = _mm(d_x1, wts["w_out"], tb=True, tm=512, tn=tnm, epilogue=dmerge_ep, name="d_merged_bwd",
                                   extras=[(br_a, "tile", 0), (br_s, "tile", 0)] + gate_tiles,
                                   outs=[(SDS((L, D), BF16), "tile")] * 4)
    d_attn = _mm(d_bra, wts["w_branch_attn"], tb=True, out_dtype=BF16, name="d_attn")
    dw_ba = _mm(attn, d_bra, ta=True, out_dtype=BF16, name="dw_branch_attn")
    d_ssm = _mm(d_brs, wts["w_branch_ssm"], tb=True, name="d_ssm")
    dw_bs = _mm(ssm, d_brs, ta=True, out_dtype=BF16, name="dw_branch_ssm")
    dq, dkv_cur, dkv_prev, d_sinks = _attn_bwd(projb, small["attn_sinks"], attn_bias, d_attn, "attn_bwd")

    def glu_bwd(c, r, ds, vb, gb):
        sg = _sigmoid(gb)
        return ds * sg, ds * vb * (sg * (1.0 - sg))

    d_glu_v, d_glu_g = _ew(glu_bwd, [(d_ssm, *_rc(tm, SW)), (glu, *_rc(tm, SW)), (glu, *_rc(tm, SW, 1))],
                           [(SDS((L, SW), F32), *_rc(tm, SW), None)] * 2, (1, nr(tm)), "glu_gate_bwd")
    d_glu = jnp.concatenate([d_glu_v, d_glu_g], axis=1)
    d_gy = _mm(d_glu, wts["w_glu"], tb=True, name="d_gelu_y")
    dw_glu = _mm(gy, d_glu, ta=True, out_dtype=BF16, name="dw_glu")

    tok = gput("b", {"w_glu": dw_glu, "w_branch_attn": dw_ba, "w_branch_ssm": dw_bs, "w_out": dw_out})
    ab = ab + tok[0, 0]

    def gelu_bwd(c, r, dg, yb, ub, dgl):
        dy = dg * _gelu_grad(yb)
        return dy, jnp.sum(dy * ub, axis=0, keepdims=True), jnp.sum(dgl, axis=0, keepdims=True)

    dy, d_ssm_d, d_b_glu = _ew(
        gelu_bwd, [(d_gy, *_rc(tm, 256)), (y, *_rc(tm, 256)), (proj, *_rc(tm, 256, C_U // 256)), (d_glu, *_rc(tm, 512))],
        [(SDS((L, SW), F32), *_rc(tm, 256), None), (SDS((1, SW), F32), *_col(1, 256), "r"),
         (SDS((1, 2 * SW), F32), *_col(1, 512), "r")], (2, nr(tm)), "ssm_gelu_bwd")

    dy_seg = _to_segments(dy)
    ends_r = _ssm_scan(dy_seg, cmat, ab, reverse=True, name="ssm_ends_bwd")
    lam, dab8, du_seg = _ssm_scan(dy_seg, cmat, ab, reverse=True, ends=ends_r, xs=xs, init=init_f, wproj=bmat,
                                  name="ssm_scan_bwd")
    du_mm = _from_segments(du_seg)
    dbm = _mm(u_seg, lam, ta=True, tm=512, name="ssm_dbmat")
    dcm = _mm(dy_seg, xs, ta=True, tm=512, name="ssm_dcmat")
    d_are, d_aim, d_ldt, d_bre, d_bim, d_cre, d_cim = _ssm_param_bwd(
        small["a_re"], small["a_im"], small["logdt"], small["b_re"], small["b_im"], dab8, dbm, dcm, "ssm_param_bwd")

    nb = L // BLK

    def dproj_fn(c, r, dqb, cur, prv, du, dyb, dsk, dga, dgs):
        dkv = cur + prv * (r < nb - 1).astype(F32)
        dub = du + dsk * dyb
        full = jnp.concatenate([dqb, dkv, dub, jnp.zeros((BLK, C_GA - C_PAD), F32), dga, dgs], axis=1)
        return full, jnp.sum(full, axis=0, keepdims=True)

    rowb = lambda w: ((BLK, w), lambda c, r: (r, 0))
    dproj, d_b_in = _ew(
        dproj_fn, [(dq, *rowb(AW)), (dkv_cur, *rowb(256)),
                   (dkv_prev, (BLK, 256), lambda c, r: (jnp.minimum(r + 1, nb - 1), 0)),
                   (du_mm, *rowb(SW)), (dy, *rowb(SW)), (small["ssm_d"], *_col(1, SW)), (d_ga, *rowb(D)), (d_gs, *rowb(D))],
        [(SDS((L, INP), BF16), *rowb(INP), None), (SDS((1, INP), F32), *_col(1, INP), "all")], (1, nb), "dproj")
    dw_in = _mm(h, dproj, ta=True, out_dtype=BF16, name="dw_in")
    tok = gput("a", {"w_in": _unpad_cols(dw_in)})
    d_h = _mm(dproj, wts["w_in"], tb=True, bias=jnp.zeros((1, D), F32) + tok[0, 0], name="d_h")
    grad_x, d_g1 = _rmsnorm_bwd(d_h, x, small["attn_norm_g"], d_x1, "norm1_bwd")

    sgrads = {"attn_norm_g": d_g1, "b_in": _unpad_cols(d_b_in), "attn_sinks": d_sinks[:, :NQ], "a_re": d_are, "a_im": d_aim,
              "logdt": d_ldt, "b_re": d_bre, "b_im": d_bim, "c_re": d_cre, "c_im": d_cim, "ssm_d": d_ssm_d,
              "b_glu": d_b_glu, "ffn_norm_g": d_g2, "conv_w": d_conv_w, "conv_b": d_conv_b, "final_norm_g": d_gf}
    return loss, grad_x, sgrads


def _small_layouts(p):
    gp = lambda a: a.reshape(1, NS)
    hgp = lambda a: a.transpose(2, 0, 1).reshape(H, NS)
    chgp = lambda a: a.transpose(1, 0, 2).reshape(H, NS)
    return {
        "attn_norm_g": p["attn_norm_g"].reshape(1, D), "ffn_norm_g": p["ffn_norm_g"].reshape(1, D),
        "final_norm_g": p["final_norm_g"].reshape(1, D),
        "b_in_p": _pad_cols(p["b_in"].reshape(1, INC)),
        "attn_sinks": p["attn_sinks"].reshape(1, NQ),
        "a_re": gp(p["ssm_a_re"]), "a_im": gp(p["ssm_a_im"]), "logdt": jnp.repeat(p["ssm_log_dt"], P).reshape(1, NS),
        "b_re": hgp(p["ssm_b_re"]), "b_im": hgp(p["ssm_b_im"]), "c_re": chgp(p["ssm_c_re"]), "c_im": chgp(p["ssm_c_im"]),
        "ssm_d": p["ssm_d"].reshape(1, SW), "b_glu": p["b_glu"].reshape(1, 2 * SW),
        "conv_b": p["conv_b"].reshape(1, DFF),
    }


def _small_grads_to_param_shapes(sg):
    from_hgp = lambda a: a.reshape(H, G, P).transpose(1, 2, 0)
    from_chgp = lambda a: a.reshape(H, G, P).transpose(1, 0, 2)
    return {
        "attn_norm_g": sg["attn_norm_g"].reshape(D), "b_in": sg["b_in"].reshape(INC),
        "attn_sinks": sg["attn_sinks"].reshape(NQ),
        "ssm_a_re": sg["a_re"].reshape(G, P), "ssm_a_im": sg["a_im"].reshape(G, P),
        "ssm_log_dt": jnp.sum(sg["logdt"].reshape(G, P), axis=1),
        "ssm_b_re": from_hgp(sg["b_re"]), "ssm_b_im": from_hgp(sg["b_im"]),
        "ssm_c_re": from_chgp(sg["c_re"]), "ssm_c_im": from_chgp(sg["c_im"]),
        "ssm_d": sg["ssm_d"].reshape(SW), "b_glu": sg["b_glu"].reshape(2 * SW),
        "ffn_norm_g": sg["ffn_norm_g"].reshape(D), "conv_w": sg["conv_w"], "conv_b": sg["conv_b"].reshape(DFF),
        "final_norm_g": sg["final_norm_g"].reshape(D),
    }


def kernel(x, attn_norm_g, w_in, b_in, attn_sinks, ssm_a_re, ssm_a_im, ssm_log_dt, ssm_b_re, ssm_b_im, ssm_c_re, ssm_c_im, ssm_d, w_glu, b_glu, w_branch_attn, w_branch_ssm, w_out, ffn_norm_g, w_up, conv_w, conv_b, w_down, final_norm_g, loss_target, m_attn_norm_g, m_w_in, m_b_in, m_attn_sinks, m_ssm_a_re, m_ssm_a_im, m_ssm_log_dt, m_ssm_b_re, m_ssm_b_im, m_ssm_c_re, m_ssm_c_im, m_ssm_d, m_w_glu, m_b_glu, m_w_branch_attn, m_w_branch_ssm, m_w_out, m_ffn_norm_g, m_w_up, m_conv_w, m_conv_b, m_w_down, m_final_norm_g, v_attn_norm_g, v_w_in, v_b_in, v_attn_sinks, v_ssm_a_re, v_ssm_a_im, v_ssm_log_dt, v_ssm_b_re, v_ssm_b_im, v_ssm_c_re, v_ssm_c_im, v_ssm_d, v_w_glu, v_b_glu, v_w_branch_attn, v_w_branch_ssm, v_w_out, v_ffn_norm_g, v_w_up, v_conv_w, v_conv_b, v_w_down, v_final_norm_g):
    args = dict(locals())
    sq = lambda a: a if a.ndim == 1 else a[0]
    wv = {n: sq(args[n]) for n in _WEIGHTS}
    mv = {n: sq(args["m_" + n]) for n in _WEIGHTS}
    vv = {n: sq(args["v_" + n]) for n in _WEIGHTS}
    me = 4 * lax.axis_index("x") + 2 * lax.axis_index("y") + lax.axis_index("c")

    gather, tok = {}, jnp.zeros((8, 128), F32)
    for grp in ("a", "b", "c"):
        shards = [(wv[n] + tok[0, 0]).astype(BF16) for n in _GROUPS[grp]]
        if grp == "c":
            shards.append(jnp.pad(wv["conv_w"] + tok[0, 0], ((0, 5), (0, 64))))
        gather[grp], tok = _exchange_start(shards, False, "gather_start_" + grp)
    small = _small_layouts(wv)
    small["attn_norm_g"] = small["attn_norm_g"] + tok[0, 0]

    def own_slot(land, src):
        return lax.dynamic_update_slice_in_dim(land, src, me, axis=0)

    def wget(grp, after):
        thru, lands = _exchange_wait(gather[grp], after, False, "gather_wait_" + grp)
        full = {}
        for n, t, g in zip(_GROUPS[grp], thru, lands):
            g = own_slot(g, t[None])
            full[n] = _unstack_cols(g) if n in _COL_SHARDED else g.reshape(N_DEV * g.shape[1], g.shape[2])
        if grp == "a":
            full["w_in"] = _pad_cols(full["w_in"])
        if grp == "c":
            full["conv_w"] = _unstack_cols(own_slot(lands[-1], thru[-1][None])[:, :3, :DFF // N_DEV])
            g = own_slot(lands[0], thru[0][None])
            full["w_up_v"], full["w_up_g"] = _unstack_cols(g[:N_DEV // 2]), _unstack_cols(g[N_DEV // 2:])
            del full["w_up"]
        return full

    scatter = {}

    def gput(grp, grads):
        stacked = [_stack_cols(grads[n]) if n in _COL_SHARDED else grads[n].reshape(N_DEV, -1, D)
                   for n in _GROUPS[grp] if n != "w_up"]
        if grp == "c":
            half = N_DEV // 2
            stacked.insert(0, jnp.concatenate([_stack_cols(grads["w_up_v"], half), _stack_cols(grads["w_up_g"], half)]))
        scatter[grp], token = _exchange_start(stacked, True, "scatter_start_" + grp)
        return token

    loss, grad_x, sg = _local_step(x[0], loss_target[0], wget, small, gput)
    loss = lax.psum(loss, MESH_AXES)

    sgp = _small_grads_to_param_shapes(sg)
    small_names = [n for n in _SMALL]
    packed_g = _pack([sgp[n] for n in small_names])
    (small_all,) = _exchange([packed_g], False, "gather_small_grads")

    outs_g, outs_d, outs_m, outs_v = {}, {}, {}, {}
    for grp in ("c", "b", "a"):
        thru, lands = _exchange_wait(scatter[grp], small_all, True, "scatter_wait_" + grp)
        for n, t, pt in zip(_GROUPS[grp], thru, lands):
            pt = own_slot(pt, lax.dynamic_slice_in_dim(t, me, 1, axis=0))
            outs_g[n], outs_d[n], outs_m[n], outs_v[n] = _adam(pt, wv[n], mv[n], vv[n], "adam_" + n)

    sizes = [int(math.prod(sgp[n].shape)) for n in small_names]
    offs = [0]
    for s in sizes:
        offs.append(offs[-1] + s)

    def local_part(n, a):
        if n == "conv_w":
            return lax.dynamic_slice(a, (0, me * (DFF // N_DEV)), (3, DFF // N_DEV))
        return a

    rows = packed_g.shape[0]

    def sum_fn(cc, rr, pb):
        g = pb[0]
        for d in range(1, N_DEV):
            g = g + pb[d]
        return (g,)

    (gsum,) = _ew(sum_fn, [(small_all, (N_DEV, rows, 128), lambda cc, rr: (0, 0, 0))],
                  [(SDS((rows, 128), F32), (rows, 128), lambda cc, rr: (0, 0), None)], (1, 1), "sum_small_grads")
    gflat = gsum.reshape(-1)
    gsmall = {n: local_part(n, gflat[offs[i]:offs[i + 1]].reshape(sgp[n].shape)) for i, n in enumerate(small_names)}
    pw = _pack([wv[n] for n in small_names])
    pm = _pack([mv[n] for n in small_names])
    pv = _pack([vv[n] for n in small_names])
    pg = _pack([gsmall[n] for n in small_names])
    prow = pw.shape[0]

    def adam_small(cc, rr, gb, wb, mb, vb):
        return _adam_math(gb, wb, mb, vb)

    whole = ((prow, 128), lambda cc, rr: (0, 0))
    sd, sm, sv = _ew(adam_small, [(pg, *whole), (pw, *whole), (pm, *whole), (pv, *whole)],
                     [(SDS((prow, 128), F32), *whole, None)] * 3, (1, 1), "adam_small")
    lsizes = [int(math.prod(wv[n].shape)) for n in small_names]
    loffs = [0]
    for s in lsizes:
        loffs.append(loffs[-1] + s)
    for i, n in enumerate(small_names):
        take = lambda a: a.reshape(-1)[loffs[i]:loffs[i + 1]].reshape(wv[n].shape)
        outs_g[n], outs_d[n], outs_m[n], outs_v[n] = gsmall[n], take(sd), take(sm), take(sv)

    lead = lambda n, a: a if args[n].ndim == 1 else a[None]
    grad_x = grad_x[None]
    return (loss, grad_x, *[lead(n, outs_g[n]) for n in _WEIGHTS], *[lead(n, outs_d[n]) for n in _WEIGHTS],
            *[lead(n, outs_m[n]) for n in _WEIGHTS], *[lead(n, outs_v[n]) for n in _WEIGHTS])
```

```python
import functools
import math

import jax
import jax.numpy as jnp
from jax import lax
from jax.experimental import pallas as pl
from jax.experimental.pallas import tpu as pltpu

F32 = jnp.float32
BF16 = jnp.bfloat16
SDS = jax.ShapeDtypeStruct

N_DEV = 8
D = 2048
NQ, NKV, HD = 16, 2, 64
AW = NQ * HD
BLK = 128
SW, G, H, P = 512, 32, 16, 64
NS = G * P
DFF = 5632
INC = AW + 2 * NKV * HD + SW + 2 * D
C_K, C_U, C_PAD = AW, AW + 2 * NKV * HD, AW + 2 * NKV * HD + SW
C_GA, C_GS, INP = D, 2 * D, 3 * D
RMS_EPS = 1e-6
NEG_BIG = -1e30
ADAM_LR, ADAM_B1, ADAM_B2, ADAM_EPS, ADAM_WD, ADAM_STEP = 0.001, 0.9, 0.999, 1e-08, 0.01, 10
NSEG = 8
VMEM_CAP_MB = 60
MESH_AXES = ("x", "y", "c")


def _pad_cols(a):
    zeros = jnp.zeros(a.shape[:-1] + (C_GA - C_PAD,), a.dtype)
    return jnp.concatenate([a[..., :C_PAD], zeros, a[..., C_PAD:]], axis=-1)


def _unpad_cols(a):
    return jnp.concatenate([a[..., :C_PAD], a[..., C_GA:]], axis=-1)


def _cparams(sem, vmem_mb):
    return pltpu.CompilerParams(dimension_semantics=sem, vmem_limit_bytes=min(int(vmem_mb), VMEM_CAP_MB) << 20)


LANES = 128


def _tile(dim, pref):
    if dim <= pref:
        return dim
    for t in range(pref - pref % LANES, 0, -LANES):
        if dim % t == 0:
            return t
    raise ValueError(f"no tile for {dim}")


def _mm(a, b, *, ta=False, tb=False, bias=None, res=None, out_dtype=F32, tm=1024, tn=1024, tk=3072, name,
        a2=None, b2=None, extras=(), epilogue=None, outs=None, ep_cols=None, stack_out=False):
    m, k = (a.shape[1], a.shape[0]) if ta else a.shape
    n = b.shape[0] if tb else b.shape[1]
    assert (b.shape[1] if tb else b.shape[0]) == k, (a.shape, b.shape, ta, tb)
    tm, tn, tk = _tile(m, tm), _tile(n, tn), _tile(k, tk)
    nk = k // tk
    dims = (((0 if ta else 1,), (1 if tb else 0,)), ((), ()))
    has_bias, has_res, has_b2 = bias is not None, res is not None, b2 is not None
    has_a2 = a2 is not None
    assert not (has_b2 and (nk > 1 or ta or tb)) and not (has_a2 and not has_b2)
    if epilogue is None:
        outs = [(SDS((n // tn, m, tn) if stack_out else (m, n), out_dtype), "tile")]
    n_ex, n_out = len(extras), len(outs)
    tcn = tn if (epilogue is None or nk > 1 or ep_cols is None) else _tile(tn, ep_cols)

    def body(*refs):
        a_ref, b_ref = refs[0], refs[1]
        pos = 2
        a2_ref = refs[pos] if has_a2 else a_ref
        pos += has_a2
        b2_ref = refs[pos] if has_b2 else None
        pos += has_b2
        bias_ref = refs[pos] if has_bias else None
        pos += has_bias
        res_ref = refs[pos] if has_res else None
        pos += has_res
        ex_refs = refs[pos:pos + n_ex]
        o_refs = refs[pos + n_ex:pos + n_ex + n_out]
        i = pl.program_id(1)

        def product(rhs_ref, cols=None, lhs=None):
            rhs = rhs_ref[...] if cols is None else (rhs_ref[cols, :] if tb else rhs_ref[:, cols])
            lhs = a_ref[...].astype(BF16) if lhs is None else lhs
            return lax.dot_general(lhs, rhs.astype(BF16), dims, preferred_element_type=F32)

        def finish(r, cols):
            if has_bias:
                r = r + bias_ref[:, cols]
            if has_res:
                r = r + res_ref[:, cols].astype(F32)
            if epilogue is None:
                o_refs[0][:, cols] = r.astype(o_refs[0].dtype)
                return
            r2 = None
            if has_b2:
                r2 = jnp.dot(a2_ref[...].astype(BF16), b2_ref[:, cols].astype(BF16), preferred_element_type=F32)
            vals = epilogue(i, cols, r, r2, *ex_refs)
            for o_ref, v, (_, kind) in zip(o_refs, vals, outs):
                if kind == "tile":
                    o_ref[:, cols] = v.astype(o_ref.dtype)
                else:
                    @pl.when(i == 0)
                    def _(o_ref=o_ref, v=v):
                        o_ref[:, cols] = v.astype(o_ref.dtype)

                    @pl.when(i > 0)
                    def _(o_ref=o_ref, v=v):
                        o_ref[:, cols] += v.astype(o_ref.dtype)

        if nk == 1:
            lhs = a_ref[...].astype(BF16)
            for c0 in range(0, tn, tcn):
                cols = pl.ds(c0, tcn)
                finish(product(b_ref, cols, lhs), cols)
            return
        whole = pl.ds(0, tn)
        acc_ref = refs[-1]
        kk = pl.program_id(2)

        @pl.when(kk == 0)
        def _():
            acc_ref[...] = product(b_ref)

        @pl.when(jnp.logical_and(kk > 0, kk < nk - 1))
        def _():
            acc_ref[...] += product(b_ref)

        @pl.when(kk == nk - 1)
        def _():
            finish(acc_ref[...] + product(b_ref), whole)

    b_spec = pl.BlockSpec((tn, tk), lambda j, i, kk: (j, kk)) if tb else pl.BlockSpec((tk, tn), lambda j, i, kk: (kk, j))
    ins = [a, b]
    in_specs = [pl.BlockSpec((tk, tm), lambda j, i, kk: (kk, i)) if ta else pl.BlockSpec((tm, tk), lambda j, i, kk: (i, kk)),
                b_spec]
    tile_spec = pl.BlockSpec((tm, tn), lambda j, i, kk: (i, j))
    byt = 2 * tm * tk * a.dtype.itemsize + 2 * tk * tn * b.dtype.itemsize
    byt += (2 + has_b2) * 4 * tm * tn
    if has_a2:
        ins.append(a2)
        in_specs.append(pl.BlockSpec((tm, a2.shape[1]), lambda j, i, kk: (i, 0)))
        byt += 2 * tm * a2.shape[1] * a2.dtype.itemsize
    if has_b2:
        ins.append(b2)
        in_specs.append(pl.BlockSpec((b2.shape[0], tn), lambda j, i, kk: (0, j)))
        byt += 2 * b2.shape[0] * tn * b2.dtype.itemsize
    if has_bias:
        ins.append(bias)
        in_specs.append(pl.BlockSpec((1, tn), lambda j, i, kk: (0, j)))
    if has_res:
        ins.append(res)
        in_specs.append(tile_spec)
        byt += 2 * tm * tn * res.dtype.itemsize
    for arr, kind, arg in extras:
        ins.append(arr)
        if kind == "tile":
            in_specs.append(pl.BlockSpec((tm, tn), lambda j, i, kk, arg=arg: (i, j + arg)))
            byt += 2 * tm * tn * arr.dtype.itemsize + 4 * tm * tn
        elif kind == "col":
            in_specs.append(pl.BlockSpec((arr.shape[0], tn), lambda j, i, kk, arg=arg: (0, j + arg)))
        else:
            in_specs.append(pl.BlockSpec(arg[0], lambda j, i, kk, im=arg[1]: im(j, i)))
    out_specs = []
    for sds, kind in outs:
        if kind == "tile":
            out_specs.append(pl.BlockSpec((None, tm, tn), lambda j, i, kk: (j, i, 0)) if stack_out else tile_spec)
            byt += 2 * tm * tn * jnp.dtype(sds.dtype).itemsize
        else:
            out_specs.append(pl.BlockSpec((sds.shape[0], tn), lambda j, i, kk: (0, j)))
    res_ = pl.pallas_call(
        body, out_shape=tuple(o[0] for o in outs), grid=(n // tn, m // tm, nk), in_specs=in_specs,
        out_specs=tuple(out_specs), scratch_shapes=[pltpu.VMEM((tm, tn), F32)] if nk > 1 else [], name=name,
        compiler_params=_cparams(("arbitrary", "arbitrary", "arbitrary"), byt / 2**20 + (8 if epilogue is None else 20)),
    )(*ins)
    return res_[0] if epilogue is None else res_


def _ew(fn, ins, outs, grid, name, vmem_mb=40):
    n_in = len(ins)
    accs = [o[3] for o in outs]

    def body(*refs):
        c, r = pl.program_id(0), pl.program_id(1)
        vals = fn(c, r, *[ref[...].astype(F32) for ref in refs[:n_in]])
        for o_ref, v, acc in zip(refs[n_in:], vals, accs):
            if acc is None:
                o_ref[...] = v.astype(o_ref.dtype)
            else:
                first = (r == 0) if acc == "r" else jnp.logical_and(r == 0, c == 0)

                @pl.when(first)
                def _(o_ref=o_ref, v=v):
                    o_ref[...] = v.astype(o_ref.dtype)

                @pl.when(jnp.logical_not(first))
                def _(o_ref=o_ref, v=v):
                    o_ref[...] += v.astype(o_ref.dtype)

    res = pl.pallas_call(
        body, out_shape=tuple(o[0] for o in outs), grid=grid,
        in_specs=[pl.BlockSpec(bs, im) for _, bs, im in ins],
        out_specs=tuple(pl.BlockSpec(bs, im) for _, bs, im, _ in outs), name=name,
        compiler_params=_cparams(("arbitrary", "arbitrary"), vmem_mb),
    )(*[a for a, _, _ in ins])
    return res


def _rc(tm, tc, coff=0):
    return (tm, tc), (lambda c, r: (r, c + coff))


def _col(rows, tc, coff=0):
    return (rows, tc), (lambda c, r: (0, c + coff))


def _gelu(x):
    return 0.5 * x * (1.0 + lax.erf(x * (2.0 ** -0.5)))


def _gelu_and_grad(x):
    cdf = 0.5 * (1.0 + lax.erf(x * (2.0 ** -0.5)))
    return x * cdf, cdf + x * jnp.exp(-0.5 * x * x) * (1.0 / math.sqrt(2.0 * math.pi))


def _gelu_grad(x):
    return _gelu_and_grad(x)[1]


def _sigmoid(x):
    return 1.0 / (1.0 + jnp.exp(-x))


def _shift_rows(x, halo, s):
    rolled = pltpu.roll(x, s, 0)
    row8 = lax.broadcasted_iota(jnp.int32, halo.shape, 0)
    head = jnp.where(row8 < s, pltpu.roll(halo, s, 0), rolled[0:8])
    return jnp.concatenate([head, rolled[8:]], axis=0)


def _shift_rows_up(x, halo, s):
    tm = x.shape[0]
    rolled = pltpu.roll(x, tm - s, 0)
    row8 = lax.broadcasted_iota(jnp.int32, halo.shape, 0)
    tail = jnp.where(row8 >= 8 - s, pltpu.roll(halo, 8 - s, 0), rolled[tm - 8:])
    return jnp.concatenate([rolled[:tm - 8], tail], axis=0)


def _rmsnorm_fwd(x, g, name, tm=256):
    L = x.shape[0]

    def fn(c, r, xb, gb):
        rstd = lax.rsqrt(jnp.mean(xb * xb, axis=-1, keepdims=True) + RMS_EPS)
        return ((xb * rstd) * gb,)

    return _ew(fn, [(x, *_rc(tm, D)), (g, *_col(1, D))], [(SDS((L, D), BF16), *_rc(tm, D), None)], (1, L // tm), name)[0]


def _rmsnorm_bwd(dh, x, g, dres, name, tm=256):
    L = x.shape[0]

    def fn(c, r, dhb, xb, gb, drb):
        rstd = lax.rsqrt(jnp.mean(xb * xb, axis=-1, keepdims=True) + RMS_EPS)
        y = xb * rstd
        dy = dhb * gb
        dx = rstd * (dy - y * jnp.mean(dy * y, axis=-1, keepdims=True))
        return drb + dx, jnp.sum(dhb * y, axis=0, keepdims=True)

    return _ew(fn, [(dh, *_rc(tm, D)), (x, *_rc(tm, D)), (g, *_col(1, D)), (dres, *_rc(tm, D))],
               [(SDS((L, D), F32), *_rc(tm, D), None), (SDS((1, D), F32), *_col(1, D), "all")], (1, L // tm), name)


def _final_loss(x2, g, tgt, name, tm=256):
    L = x2.shape[0]

    def fn(c, r, xb, gb, tb):
        rstd = lax.rsqrt(jnp.mean(xb * xb, axis=-1, keepdims=True) + RMS_EPS)
        y = xb * rstd
        err = y * gb - tb
        dout = err * (1.0 / D)
        dy = dout * gb
        dx = rstd * (dy - y * jnp.mean(dy * y, axis=-1, keepdims=True))
        return dx, dx, jnp.sum(err * err, axis=0, keepdims=True) * (0.5 / D), jnp.sum(dout * y, axis=0, keepdims=True)

    return _ew(fn, [(x2, *_rc(tm, D)), (g, *_col(1, D)), (tgt, *_rc(tm, D))],
               [(SDS((L, D), F32), *_rc(tm, D), None), (SDS((L, D), BF16), *_rc(tm, D), None),
                (SDS((1, D), F32), *_col(1, D), "all"),
                (SDS((1, D), F32), *_col(1, D), "all")], (1, L // tm), name)


def _slope(h):
    return 2.0 ** (-8.0 * (h + 1) / NQ)


def _attn_bias():
    qi = lax.broadcasted_iota(jnp.int32, (BLK, 2 * BLK), 0)
    si = lax.broadcasted_iota(jnp.int32, (BLK, 2 * BLK), 1)
    dist = qi + BLK - si
    band = (dist >= 0) & (dist < BLK)
    slopes = jnp.asarray([_slope(h) for h in range(NQ)], F32)[:, None, None]
    alibi = -slopes * dist.astype(F32)[None]
    return jnp.stack([jnp.where((band & (si >= BLK))[None], alibi, NEG_BIG), jnp.where(band[None], alibi, NEG_BIG)])


def _attn_kv(kvc, kvp):
    kv = jnp.concatenate([kvp, kvc], axis=0).astype(F32)
    lo = lax.broadcasted_iota(jnp.int32, (2 * BLK, 128), 1) < HD

    def halves(t):
        tr = pltpu.roll(t, HD, 1)
        z = jnp.zeros_like(t)
        return {(0, 0): jnp.where(lo, t, z).astype(BF16), (0, 1): jnp.where(lo, z, tr).astype(BF16),
                (1, 0): jnp.where(lo, tr, z).astype(BF16), (1, 1): jnp.where(lo, z, t).astype(BF16)}

    return halves(kv[:, :128]), halves(kv[:, 128:])


_NT = (((1,), (1,)), ((), ()))
_TN = (((0,), (0,)), ((), ()))
_ATTN_SPECS = [pl.BlockSpec(memory_space=pltpu.SMEM),
               pl.BlockSpec((None, NQ, BLK, 2 * BLK), lambda n: (jnp.minimum(n, 1), 0, 0, 0)),
               pl.BlockSpec((BLK, AW), lambda n: (n, 0)),
               pl.BlockSpec((BLK, 256), lambda n: (n, C_K // 256)),
               pl.BlockSpec((BLK, 256), lambda n: (jnp.maximum(n - 1, 0), C_K // 256))]


def _attn_scores(q_ref, bias_ref, kmat, sc_ref):
    for j in range(NQ // 2):
        qs = q_ref[:, 128 * j:128 * (j + 1)] * (HD ** -0.5)
        for e in range(2):
            h = 2 * j + e
            sc_ref[h] = lax.dot_general(qs, kmat[(j // (NQ // 4), e)], _NT, preferred_element_type=F32) + bias_ref[h]


def _softmax_with_sink(s, sink):
    m = jnp.maximum(jnp.max(s, axis=-1, keepdims=True), sink)
    p = jnp.exp(s - m)
    esink = jnp.exp(sink - m)
    den = jnp.sum(p, axis=-1, keepdims=True) + esink
    return p / den, esink / den


def _attn_fwd(projb, sinks, bias, name):
    L = projb.shape[0]

    def body(s_ref, bias_ref, q_ref, kvc_ref, kvp_ref, o_ref, sc_ref, pr_ref):
        kmat, vmat = _attn_kv(kvc_ref[...], kvp_ref[...])
        _attn_scores(q_ref, bias_ref, kmat, sc_ref)
        for h in range(NQ):
            pr_ref[h] = _softmax_with_sink(sc_ref[h], s_ref[0, h])[0].astype(BF16)
        for j in range(NQ // 2):
            g = j // (NQ // 4)
            acc = jnp.dot(pr_ref[2 * j], vmat[(g, 0)], preferred_element_type=F32)
            acc = acc + jnp.dot(pr_ref[2 * j + 1], vmat[(g, 1)], preferred_element_type=F32)
            o_ref[:, 128 * j:128 * (j + 1)] = acc.astype(BF16)

    return pl.pallas_call(
        body, out_shape=SDS((L, AW), BF16), grid=(L // BLK,), in_specs=_ATTN_SPECS,
        out_specs=pl.BlockSpec((BLK, AW), lambda n: (n, 0)), name=name,
        scratch_shapes=[pltpu.VMEM((NQ, BLK, 2 * BLK), F32), pltpu.VMEM((NQ, BLK, 2 * BLK), BF16)],
        compiler_params=_cparams(("arbitrary",), 32),
    )(sinks, bias, projb, projb, projb)


def _attn_bwd(projb, sinks, bias, dattn, name):
    L = projb.shape[0]

    def body(s_ref, bias_ref, q_ref, kvc_ref, kvp_ref, do_ref, dq_ref, dcur_ref, dprev_ref, dsink_ref,
             sc_ref, dp_ref, ds_ref, pr_ref):
        n = pl.program_id(0)
        kmat, vmat = _attn_kv(kvc_ref[...], kvp_ref[...])
        _attn_scores(q_ref, bias_ref, kmat, sc_ref)
        for h in range(NQ):
            j, e = h // 2, h % 2
            dp_ref[h] = lax.dot_general(do_ref[:, 128 * j:128 * (j + 1)], vmat[(j // (NQ // 4), e)], _NT,
                                        preferred_element_type=F32)
        lane = lax.broadcasted_iota(jnp.int32, (1, 128), 1)
        dsv = jnp.zeros((1, 128), F32)
        for h in range(NQ):
            p, psink = _softmax_with_sink(sc_ref[h], s_ref[0, h])
            dp = dp_ref[h]
            drow = jnp.sum(p * dp, axis=-1, keepdims=True)
            ds_ref[h] = (p * (dp - drow)).astype(BF16)
            pr_ref[h] = p.astype(BF16)
            dsv = dsv + jnp.where(lane == h, -jnp.sum(psink * drow, axis=0, keepdims=True), 0.0)
        lo128 = lax.broadcasted_iota(jnp.int32, (BLK, 128), 1) < HD
        dk = [jnp.zeros((2 * BLK, 128), F32) for _ in range(NKV)]
        dv = [jnp.zeros((2 * BLK, 128), F32) for _ in range(NKV)]
        for j in range(NQ // 2):
            g = j // (NQ // 4)
            qs = q_ref[:, 128 * j:128 * (j + 1)] * (HD ** -0.5)
            dop = do_ref[:, 128 * j:128 * (j + 1)]
            zb = jnp.zeros_like(qs)
            dqp = jnp.zeros((BLK, 128), F32)
            for e in range(2):
                h = 2 * j + e
                half = lo128 if e == 0 else jnp.logical_not(lo128)
                dqp = dqp + jnp.dot(ds_ref[h], kmat[(g, e)], preferred_element_type=F32)
                dk[g] = dk[g] + lax.dot_general(ds_ref[h], jnp.where(half, qs, zb), _TN, preferred_element_type=F32)
                dv[g] = dv[g] + lax.dot_general(pr_ref[h], jnp.where(half, dop, zb), _TN, preferred_element_type=F32)
            dq_ref[:, 128 * j:128 * (j + 1)] = (dqp * (HD ** -0.5)).astype(BF16)
        lo256 = lax.broadcasted_iota(jnp.int32, (2 * BLK, 128), 1) < HD
        tot = [t + pltpu.roll(t, HD, 1) for t in (dk[0], dk[1], dv[0], dv[1])]
        dkv = jnp.concatenate([jnp.where(lo256, tot[0], tot[1]), jnp.where(lo256, tot[2], tot[3])], axis=1)
        dprev_ref[...] = dkv[:BLK]
        dcur_ref[...] = dkv[BLK:]

        @pl.when(n == 0)
        def _():
            dsink_ref[...] = dsv

        @pl.when(n > 0)
        def _():
            dsink_ref[...] += dsv

    tile = (NQ, BLK, 2 * BLK)
    return pl.pallas_call(
        body, out_shape=(SDS((L, AW), BF16), SDS((L, 256), F32), SDS((L, 256), F32), SDS((1, 128), F32)), grid=(L // BLK,),
        in_specs=_ATTN_SPECS + [pl.BlockSpec((BLK, AW), lambda n: (n, 0))],
        out_specs=(pl.BlockSpec((BLK, AW), lambda n: (n, 0)), pl.BlockSpec((BLK, 256), lambda n: (n, 0)),
                   pl.BlockSpec((BLK, 256), lambda n: (n, 0)), pl.BlockSpec((1, 128), lambda n: (0, 0))),
        scratch_shapes=[pltpu.VMEM(tile, F32), pltpu.VMEM(tile, F32), pltpu.VMEM(tile, BF16), pltpu.VMEM(tile, BF16)],
        name=name, compiler_params=_cparams(("arbitrary",), 40),
    )(sinks, bias, projb, projb, projb, dattn)


def _disc(a_re, a_im, logdt, b_re, b_im):
    dt = jnp.exp(logdt)
    mag = jnp.exp(a_re * dt)
    ab_re = mag * jnp.cos(a_im * dt)
    ab_im = mag * jnp.sin(a_im * dt)
    nr = ab_re - 1.0
    ni = ab_im
    den = a_re * a_re + a_im * a_im
    z_re = (nr * a_re + ni * a_im) / den
    z_im = (ni * a_re - nr * a_im) / den
    return ab_re, ab_im, z_re * b_re - z_im * b_im, z_re * b_im + z_im * b_re


def _group_mask():
    row = lax.broadcasted_iota(jnp.int32, (SW, NS), 0) // H
    col = lax.broadcasted_iota(jnp.int32, (SW, NS), 1) // P
    return row == col


def _block_diag(re, im):
    mask = _group_mask()
    z = jnp.zeros((SW, NS), F32)
    return jnp.concatenate([jnp.where(mask, jnp.tile(re, (G, 1)), z), jnp.where(mask, jnp.tile(im, (G, 1)), z)], axis=1)


def _block_diag_t(big):
    mask = _group_mask()
    z = jnp.zeros((SW, NS), F32)
    re = jnp.sum(jnp.where(mask, big[:, :NS], z).reshape(G, H, NS), axis=0)
    im = jnp.sum(jnp.where(mask, big[:, NS:], z).reshape(G, H, NS), axis=0)
    return re, im


def _ssm_prep(a_re, a_im, logdt, b_re, b_im, c_re, c_im, name):
    def body(are, aim, ldt, bre, bim, cre, cim, ab_ref, bm_ref, cm_ref):
        ab_re, ab_im, bb_re, bb_im = _disc(are[...], aim[...], ldt[...], bre[...], bim[...])
        ab_ref[...] = jnp.concatenate([ab_re, ab_im], axis=1)
        bm_ref[...] = _block_diag(bb_re, bb_im).astype(BF16)
        cm_ref[...] = _block_diag(cre[...], -cim[...]).astype(BF16)

    return pl.pallas_call(body, out_shape=(SDS((1, 2 * NS), F32), SDS((SW, 2 * NS), BF16), SDS((SW, 2 * NS), BF16)),
                          name=name, compiler_params=pltpu.CompilerParams(vmem_limit_bytes=48 << 20),
                          )(a_re, a_im, logdt, b_re, b_im, c_re, c_im)


def _ssm_param_bwd(a_re, a_im, logdt, b_re, b_im, dab8, dbm, dcm, name):
    def body(are, aim, ldt, bre, bim, dab_ref, dbm_ref, dcm_ref, o_are, o_aim, o_ldt, o_bre, o_bim, o_cre, o_cim):
        dab = jnp.sum(dab_ref[...], axis=0, keepdims=True)
        dbb_re, dbb_im = _block_diag_t(dbm_ref[...])
        _, vjp = jax.vjp(_disc, are[...], aim[...], ldt[...], bre[...], bim[...])
        d_are, d_aim, d_ldt, d_bre, d_bim = vjp((dab[:, :NS], dab[:, NS:], dbb_re, dbb_im))
        o_are[...], o_aim[...], o_ldt[...], o_bre[...], o_bim[...] = d_are, d_aim, d_ldt, d_bre, d_bim
        dc_re, dc_imn = _block_diag_t(dcm_ref[...])
        o_cre[...] = dc_re
        o_cim[...] = -dc_imn

    v1, vh = SDS((1, NS), F32), SDS((H, NS), F32)
    return pl.pallas_call(body, out_shape=(v1, v1, v1, vh, vh, vh, vh), name=name,
                          compiler_params=pltpu.CompilerParams(vmem_limit_bytes=56 << 20),
                          )(a_re, a_im, logdt, b_re, b_im, dab8, dbm, dcm)


def _ssm_scan(src, wmat, ab, *, reverse, ends=None, xs=None, init=None, wproj=None, name, tk=32):
    L = src.shape[0]
    rows = NSEG * tk
    nch = L // rows
    seg_len = L // NSEG
    n_sq = int(math.log2(seg_len))
    assert 2 ** n_sq == seg_len and L % rows == 0
    first_pass = ends is None
    with_dab = (not first_pass) and reverse
    with_proj = wproj is not None
    assert not (with_proj and first_pass)
    slab = 512
    n_slab = NS // slab

    def body(*refs):
        src_ref, w_ref, ab_ref = refs[:3]
        pos = 3
        if not first_pass:
            ends_ref = refs[pos]
            pos += 1
        if with_dab:
            xs_ref, xsh_ref, init_ref = refs[pos:pos + 3]
            pos += 3
        if with_proj:
            wproj_ref = refs[pos]
            pos += 1
        if first_pass:
            (e_ref,) = refs[pos:pos + 1]
            pos += 1
        else:
            st_out_ref, aux_ref = refs[pos:pos + 2]
            pos += 2
        if with_proj:
            proj_ref = refs[pos]
            pos += 1
        buf_ref, st_ref = refs[pos:pos + 2]
        i = pl.program_id(0)
        a_re = ab_ref[:, :NS]
        a_im = -ab_ref[:, NS:] if reverse else ab_ref[:, NS:]

        @pl.when(i == 0)
        def _():
            if first_pass:
                st_ref[...] = jnp.zeros_like(st_ref)
            else:
                pr, pi = a_re, a_im
                for _ in range(n_sq):
                    pr, pi = pr * pr - pi * pi, 2.0 * pr * pi
                zr = jnp.zeros((1, NS), F32)
                cr, ci = zr, zr
                order = list(range(NSEG - 1, -1, -1)) if reverse else list(range(NSEG))
                st_ref[order[0]:order[0] + 1, :] = jnp.zeros((1, 2 * NS), F32)
                for jprev, j in zip(order[:-1], order[1:]):
                    er, ei = ends_ref[jprev:jprev + 1, :NS], ends_ref[jprev:jprev + 1, NS:]
                    cr, ci = er + pr * cr - pi * ci, ei + pr * ci + pi * cr
                    st_ref[j:j + 1, :NS] = cr
                    st_ref[j:j + 1, NS:] = ci
                if not reverse:
                    aux_ref[...] = st_ref[...]
                else:
                    aux_ref[...] = jnp.zeros_like(aux_ref)

        buf_ref[...] = jnp.dot(src_ref[...].astype(BF16), w_ref[...], preferred_element_type=F32)

        for s in range(n_slab):
            re_sl, im_sl = pl.ds(s * slab, slab), pl.ds(NS + s * slab, slab)
            ar = jnp.broadcast_to(a_re[:, s * slab:(s + 1) * slab], (NSEG, slab))
            ai = jnp.broadcast_to(a_im[:, s * slab:(s + 1) * slab], (NSEG, slab))

            def step(t, carry, re_sl=re_sl, im_sl=im_sl, ar=ar, ai=ai):
                k = (tk - 1 - t) if reverse else t
                r0 = pl.multiple_of(k * NSEG, NSEG)
                xr, xi = carry[0], carry[1]
                nr = ar * xr - ai * xi + buf_ref[pl.ds(r0, NSEG), re_sl]
                ni = ar * xi + ai * xr + buf_ref[pl.ds(r0, NSEG), im_sl]
                if not first_pass:
                    buf_ref[pl.ds(r0, NSEG), re_sl] = nr
                    buf_ref[pl.ds(r0, NSEG), im_sl] = ni
                if not with_dab:
                    return nr, ni
                rp = pl.multiple_of((k - 1) * NSEG, NSEG)
                xpr, xpi = xs_ref[pl.ds(rp, NSEG), re_sl], xs_ref[pl.ds(rp, NSEG), im_sl]
                return nr, ni, carry[2] + nr * xpr + ni * xpi, carry[3] + ni * xpr - nr * xpi

            carry = (st_ref[:, re_sl], st_ref[:, im_sl])
            if with_dab:
                z = jnp.zeros((NSEG, slab), F32)
                carry = lax.fori_loop(0, tk - 1, step, carry + (z, z))
                xr, xi, dr, di = carry
                nr = ar * xr - ai * xi + buf_ref[pl.ds(0, NSEG), re_sl]
                ni = ar * xi + ai * xr + buf_ref[pl.ds(0, NSEG), im_sl]
                buf_ref[pl.ds(0, NSEG), re_sl] = nr
                buf_ref[pl.ds(0, NSEG), im_sl] = ni
                at_start = i == nch - 1
                xpr = jnp.where(at_start, init_ref[:, re_sl], xsh_ref[:, re_sl])
                xpi = jnp.where(at_start, init_ref[:, im_sl], xsh_ref[:, im_sl])
                aux_ref[:, re_sl] += dr + nr * xpr + ni * xpi
                aux_ref[:, im_sl] += di + ni * xpr - nr * xpi
                carry = (nr, ni)
            else:
                carry = lax.fori_loop(0, tk, step, carry)
            st_ref[:, re_sl] = carry[0]
            st_ref[:, im_sl] = carry[1]

        if first_pass:
            @pl.when(i == nch - 1)
            def _():
                e_ref[...] = st_ref[...]
        else:
            st_out_ref[...] = buf_ref[...].astype(st_out_ref.dtype)
            if with_proj:
                proj_ref[...] = lax.dot_general(buf_ref[...].astype(BF16), wproj_ref[...], _NT, preferred_element_type=F32)

    chunk = (lambda i: (nch - 1 - i, 0)) if reverse else (lambda i: (i, 0))
    whole = lambda i: (0, 0)
    ins = [src, wmat, ab]
    in_specs = [pl.BlockSpec((rows, SW), chunk), pl.BlockSpec((SW, 2 * NS), whole), pl.BlockSpec((1, 2 * NS), whole)]
    small = SDS((NSEG, 2 * NS), F32)
    small_spec = pl.BlockSpec((NSEG, 2 * NS), whole)
    if not first_pass:
        ins.append(ends)
        in_specs.append(small_spec)
    if with_dab:
        ins += [xs, xs, init]
        in_specs += [pl.BlockSpec((rows, 2 * NS), chunk),
                     pl.BlockSpec((NSEG, 2 * NS), lambda i: (jnp.maximum((nch - 1 - i) * tk - 1, 0), 0)),
                     small_spec]
    if with_proj:
        ins.append(wproj)
        in_specs.append(pl.BlockSpec((SW, 2 * NS), whole))
    if first_pass:
        out_shape, out_specs = small, small_spec
    else:
        out_shape = (SDS((L, 2 * NS), BF16 if reverse else F32), small)
        out_specs = (pl.BlockSpec((rows, 2 * NS), chunk), small_spec)
        if with_proj:
            out_shape += (SDS((L, SW), F32),)
            out_specs += (pl.BlockSpec((rows, SW), chunk),)
    return pl.pallas_call(
        body, out_shape=out_shape, grid=(nch,), in_specs=in_specs, out_specs=out_specs,
        scratch_shapes=[pltpu.VMEM((rows, 2 * NS), F32), pltpu.VMEM((NSEG, 2 * NS), F32)], name=name,
        compiler_params=_cparams(("arbitrary",), 56),
    )(*ins)


def _to_segments(a):
    L, c = a.shape
    return a.reshape(NSEG, L // NSEG, c).transpose(1, 0, 2).reshape(L, c)


def _from_segments(a):
    L, c = a.shape
    return a.reshape(L // NSEG, NSEG, c).transpose(1, 0, 2).reshape(L, c)


def _peer(x, y, c, m):
    return ((1 - x) if (m >> 2) & 1 else x, (1 - y) if (m >> 1) & 1 else y, (1 - c) if m & 1 else c)


def _dev_index(p):
    return 4 * p[0] + 2 * p[1] + p[2]


def _exchange(arrs, scatter, name):
    n = len(arrs)

    def body(*refs):
        ins, outs = refs[:n], refs[n:2 * n]
        send_sems, recv_sems, loc_sems = refs[2 * n:]
        x, y, c = lax.axis_index("x"), lax.axis_index("y"), lax.axis_index("c")
        me = _dev_index((x, y, c))

        def src(w, to):
            return ins[w].at[to] if scatter else ins[w]

        def local(w):
            return pltpu.make_async_copy(src(w, me), outs[w].at[me], loc_sems.at[w])

        def remote(w, m):
            peer = _peer(x, y, c, m)
            return pltpu.make_async_remote_copy(src_ref=src(w, _dev_index(peer)), dst_ref=outs[w].at[me],
                                                send_sem=send_sems.at[w, m - 1], recv_sem=recv_sems.at[w, m - 1],
                                                device_id=peer, device_id_type=pl.DeviceIdType.MESH)

        def arrival(w, m):
            peer = _peer(x, y, c, m)
            return pltpu.make_async_remote_copy(src_ref=src(w, me), dst_ref=outs[w].at[_dev_index(peer)],
                                                send_sem=send_sems.at[w, m - 1], recv_sem=recv_sems.at[w, m - 1],
                                                device_id=peer, device_id_type=pl.DeviceIdType.MESH)

        for w in range(n):
            local(w).start()
        for w in range(n):
            for m in range(1, N_DEV):
                remote(w, m).start()
        for w in range(n):
            for m in range(1, N_DEV):
                arrival(w, m).wait_recv()
        for w in range(n):
            for m in range(1, N_DEV):
                remote(w, m).wait_send()
        for w in range(n):
            local(w).wait()

    anyspec = pl.BlockSpec(memory_space=pl.ANY)
    out_shape = tuple(SDS(a.shape if scatter else (N_DEV,) + a.shape, a.dtype) for a in arrs)
    return pl.pallas_call(
        body, out_shape=out_shape, in_specs=[anyspec] * n, out_specs=tuple([anyspec] * n),
        scratch_shapes=[pltpu.SemaphoreType.DMA((n, N_DEV - 1)), pltpu.SemaphoreType.DMA((n, N_DEV - 1)),
                        pltpu.SemaphoreType.DMA((n,))],
        name=name, compiler_params=pltpu.CompilerParams(has_side_effects=True),
    )(*arrs)


_HBM = pl.BlockSpec(memory_space=pltpu.HBM)
_SEM = pl.BlockSpec(memory_space=pltpu.SEMAPHORE)
_EFFECT = pltpu.SideEffectType.DATAFLOW_SIDE_EFFECTING


def _sem_index(w, m):
    return w * (N_DEV - 1) + m - 1


def _exchange_start(arrs, scatter, name):
    n = len(arrs)
    lands = [lax.empty(a.shape if scatter else (N_DEV,) + a.shape, a.dtype) for a in arrs]

    def body(*refs):
        ins, zones = refs[:n], refs[n:2 * n]
        send_sems, recv_sems = refs[2 * n], refs[2 * n + 1]
        token = refs[-1]
        x, y, c = lax.axis_index("x"), lax.axis_index("y"), lax.axis_index("c")
        me = _dev_index((x, y, c))
        for w in range(n):
            for m in range(1, N_DEV):
                peer = _peer(x, y, c, m)
                pltpu.make_async_remote_copy(
                    src_ref=ins[w].at[_dev_index(peer)] if scatter else ins[w], dst_ref=zones[w].at[me],
                    send_sem=send_sems.at[_sem_index(w, m)], recv_sem=recv_sems.at[_sem_index(w, m)],
                    device_id=peer, device_id_type=pl.DeviceIdType.MESH).start()
        token[...] = jnp.zeros_like(token)

    sems = pltpu.SemaphoreType.DMA((n * (N_DEV - 1),))
    res = pl.pallas_call(
        body, name=name,
        out_shape=(sems, sems, *[pltpu.HBM(a.shape, a.dtype) for a in arrs], *[pltpu.HBM(z.shape, z.dtype) for z in lands],
                   SDS((8, 128), F32)),
        in_specs=[_HBM] * (2 * n), out_specs=(_SEM, _SEM, *([_HBM] * (2 * n)), pl.BlockSpec(memory_space=pltpu.VMEM)),
        input_output_aliases={i: 2 + i for i in range(2 * n)},
        compiler_params=pltpu.CompilerParams(has_side_effects=_EFFECT),
    )(*[pltpu.with_memory_space_constraint(a, pltpu.HBM) for a in arrs],
      *[pltpu.with_memory_space_constraint(z, pltpu.HBM) for z in lands])
    return (res[0], res[1], list(res[2:2 + n]), list(res[2 + n:2 + 2 * n])), res[-1]


def _exchange_wait(handle, after, scatter, name):
    send_sems, recv_sems, thru, lands = handle
    n = len(thru)

    def body(*refs):
        ins, zones = refs[:n], refs[n:2 * n]
        send_sems, recv_sems = refs[2 * n], refs[2 * n + 1]
        x, y, c = lax.axis_index("x"), lax.axis_index("y"), lax.axis_index("c")
        me = _dev_index((x, y, c))
        for w in range(n):
            for m in range(1, N_DEV):
                peer = _peer(x, y, c, m)
                copy = pltpu.make_async_remote_copy(
                    src_ref=ins[w].at[me] if scatter else ins[w], dst_ref=zones[w].at[_dev_index(peer)],
                    send_sem=send_sems.at[_sem_index(w, m)], recv_sem=recv_sems.at[_sem_index(w, m)],
                    device_id=peer, device_id_type=pl.DeviceIdType.MESH)
                copy.wait_send()
                copy.wait_recv()

    res = pl.pallas_call(
        body, name=name,
        out_shape=(*[pltpu.HBM(a.shape, a.dtype) for a in thru], *[pltpu.HBM(z.shape, z.dtype) for z in lands]),
        in_specs=[_HBM] * (2 * n) + [_SEM, _SEM, pl.BlockSpec(memory_space=pl.ANY)], out_specs=tuple([_HBM] * (2 * n)),
        input_output_aliases={i: i for i in range(2 * n)},
        compiler_params=pltpu.CompilerParams(has_side_effects=_EFFECT),
    )(*thru, *lands, send_sems, recv_sems, after)
    return list(res[:n]), list(res[n:])


def _adam_math(g, w, m, v):
    m = ADAM_B1 * m + (1.0 - ADAM_B1) * g
    v = ADAM_B2 * v + (1.0 - ADAM_B2) * (g * g)
    m_hat = m / (1.0 - ADAM_B1 ** ADAM_STEP)
    v_hat = v / (1.0 - ADAM_B2 ** ADAM_STEP)
    delta = -ADAM_LR * (m_hat / (jnp.sqrt(v_hat) + ADAM_EPS) + ADAM_WD * w)
    return delta, m, v


def _adam(parts, w, m, v, name, tr=128):
    r, c = w.shape
    tr = next(t for t in (tr, 64, 32, 16, 8) if r % t == 0)

    def fn(cc, rr, pb, wb, mb, vb):
        g = pb[0].astype(F32)
        for d in range(1, N_DEV):
            g = g + pb[d].astype(F32)
        delta, nm, nv = _adam_math(g, wb, mb, vb)
        return g, delta, nm, nv

    blk = ((tr, c), lambda cc, rr: (rr, 0))
    o = SDS((r, c), F32)
    return _ew(fn, [(parts, (N_DEV, tr, c), lambda cc, rr: (0, rr, 0)), (w, *blk), (m, *blk), (v, *blk)],
               [(o, *blk, None)] * 4, (1, r // tr), name)


_SHARDED = ("w_in", "w_glu", "w_branch_attn", "w_branch_ssm", "w_out", "w_up", "w_down")
_COL_SHARDED = ("w_in", "w_glu", "w_branch_attn", "w_branch_ssm", "w_up")
_GROUPS = {"a": ("w_in",), "b": ("w_glu", "w_branch_attn", "w_branch_ssm", "w_out"), "c": ("w_up", "w_down")}
_SMALL = ("attn_norm_g", "b_in", "attn_sinks", "ssm_a_re", "ssm_a_im", "ssm_log_dt", "ssm_b_re", "ssm_b_im",
          "ssm_c_re", "ssm_c_im", "ssm_d", "b_glu", "ffn_norm_g", "conv_w", "conv_b", "final_norm_g")
_WEIGHTS = ("attn_norm_g", "w_in", "b_in", "attn_sinks", "ssm_a_re", "ssm_a_im", "ssm_log_dt", "ssm_b_re", "ssm_b_im",
            "ssm_c_re", "ssm_c_im", "ssm_d", "w_glu", "b_glu", "w_branch_attn", "w_branch_ssm", "w_out", "ffn_norm_g",
            "w_up", "conv_w", "conv_b", "w_down", "final_norm_g")


def _unstack_cols(g):
    return g.transpose(1, 0, 2).reshape(g.shape[1], g.shape[0] * g.shape[2])


def _stack_cols(a, d=N_DEV):
    k, n = a.shape
    return a.reshape(k, d, n // d).transpose(1, 0, 2)


def _pack(arrs):
    flat = jnp.concatenate([a.reshape(-1) for a in arrs])
    pad = (-flat.shape[0]) % 1024
    return jnp.pad(flat, (0, pad)).reshape(-1, 128)


def _local_step(x, tgt, wget, small, gput):
    L = x.shape[0]
    nr = lambda tm: L // tm

    h = _rmsnorm_fwd(x, small["attn_norm_g"], "norm1")
    wts = dict(wget("a", h))
    projb = _mm(h, wts["w_in"], bias=small["b_in_p"], out_dtype=BF16, name="proj")
    proj = projb
    attn_bias = _attn_bias()
    attn = _attn_fwd(projb, small["attn_sinks"], attn_bias, "attn_fwd")

    ab, bmat, cmat = _ssm_prep(small["a_re"], small["a_im"], small["logdt"], small["b_re"], small["b_im"],
                               small["c_re"], small["c_im"], "ssm_prep")
    u_seg = _to_segments(proj[:, C_U:C_PAD])
    ends_f = _ssm_scan(u_seg, bmat, ab, reverse=False, tk=64, name="ssm_ends_fwd")
    xs, init_f, y_seg = _ssm_scan(u_seg, bmat, ab, reverse=False, ends=ends_f, wproj=cmat, tk=64, name="ssm_scan_fwd")
    y_mm = _from_segments(y_seg)

    def gelu_fn(c, r, yb, ub, db):
        yv = yb + db * ub
        return yv, _gelu(yv)

    tm = 512
    y, gy = _ew(gelu_fn, [(y_mm, *_rc(tm, 256)), (proj, *_rc(tm, 256, C_U // 256)), (small["ssm_d"], *_col(1, 256))],
                [(SDS((L, SW), F32), *_rc(tm, 256), None), (SDS((L, SW), BF16), *_rc(tm, 256), None)],
                (2, nr(tm)), "ssm_gelu")
    wts.update(wget("b", gy))
    glu = _mm(gy, wts["w_glu"], bias=small["b_glu"], name="glu")

    def glu_fn(c, r, vb, gb):
        return (vb * _sigmoid(gb),)

    (ssm,) = _ew(glu_fn, [(glu, *_rc(tm, SW)), (glu, *_rc(tm, SW, 1))], [(SDS((L, SW), BF16), *_rc(tm, SW), None)],
                 (1, nr(tm)), "glu_gate")
    f32 = lambda ref, cols: ref[:, cols].astype(F32)
    tnm = 1024
    gate_tiles = [(projb, "tile", C_GA // tnm), (projb, "tile", C_GS // tnm)]

    def merge_ep(i, cols, ra, rs, ga, gs):
        return _sigmoid(f32(ga, cols)) * ra + _sigmoid(f32(gs, cols)) * rs, ra, rs

    merged, br_a, br_s = _mm(attn, wts["w_branch_attn"], a2=ssm, b2=wts["w_branch_ssm"], tm=512, tn=tnm,
                                  extras=gate_tiles, epilogue=merge_ep, outs=[(SDS((L, D), BF16), "tile")] * 3,
                                  name="branch_merge")
    x1 = _mm(merged, wts["w_out"], res=x, name="out_proj")
    h2 = _rmsnorm_fwd(x1, small["ffn_norm_g"], "norm2")
    wts.update(wget("c", h2))
    conv_w = wts["conv_w"]
    w_up_v, w_up_g = wts["w_up_v"], wts["w_up_g"]
    tcf = 1408
    tma = 256
    hb = 16

    def conv_gate(first, gate, halo, cw, cb):
        halo = halo * jnp.logical_not(first).astype(F32)
        g1, g2 = _shift_rows(gate, halo, 1), _shift_rows(gate, halo, 2)
        return cb + cw[2:3] * gate + cw[1:2] * g1 + cw[0:1] * g2, g1, g2

    tmu, tnu = 1024, 512

    def up_ep(i, cols, rv, rg, h2_halo, wg, cw, cb):
        halo = jnp.dot(h2_halo[...], wg[:, cols], preferred_element_type=F32)[hb - 8:]
        gl, glg = _gelu_and_grad(conv_gate(i == 0, rg, halo, cw[:, cols], cb[:, cols])[0])
        return rv, rg, rv * gl, gl, glg

    up_v, up_g, act, gelu_cg, gelu_grad_cg = _mm(
        h2, w_up_v, b2=w_up_g, tm=tmu, tn=tnu, epilogue=up_ep, outs=[(SDS((L, DFF), BF16), "tile")] * 5, name="ffn_up_act",
        extras=[(h2, "spec", ((hb, D), lambda j, i: (jnp.maximum(i * (tmu // hb) - 1, 0), 0))),
                (w_up_g, "spec", ((D, tnu), lambda j, i: (0, j))), (conv_w, "col", 0), (small["conv_b"], "col", 0)])
    x2 = _mm(act, wts["w_down"], res=x1, name="ffn_down")
    d_x2, d_x2b, loss_cols, d_gf = _final_loss(x2, small["final_norm_g"], tgt, "final_loss")
    loss = jnp.sum(loss_cols)

    dw_down = _mm(act, d_x2b, ta=True, out_dtype=BF16, tm=tcf, tk=2048, name="dw_down")
    tmd, tnd = 1024, 512

    def dact_ep(i, cols, da, _, val_ref, gate_ref, halo_ref, gl_ref, glg_ref):
        val, gate, gl = f32(val_ref, cols), f32(gate_ref, cols), f32(gl_ref, cols)
        halo = f32(halo_ref, cols)[hb - 8:] * (i > 0).astype(F32)
        g1, g2 = _shift_rows(gate, halo, 1), _shift_rows(gate, halo, 2)
        d_cg = da * val * f32(glg_ref, cols)
        row3 = lax.broadcasted_iota(jnp.int32, (3, da.shape[1]), 0)
        s0 = jnp.sum(d_cg * g2, axis=0, keepdims=True)
        s1 = jnp.sum(d_cg * g1, axis=0, keepdims=True)
        s2 = jnp.sum(d_cg * gate, axis=0, keepdims=True)
        dcw = jnp.where(row3 == 0, s0, jnp.where(row3 == 1, s1, s2))
        return da * gl, d_cg, dcw, jnp.sum(d_cg, axis=0, keepdims=True)

    d_val, d_cg, d_conv_w, d_conv_b = _mm(
        d_x2b, wts["w_down"], tb=True, tm=tmd, tn=tnd, epilogue=dact_ep, name="d_act_bwd",
        extras=[(up_v, "tile", 0), (up_g, "tile", 0),
                (up_g, "spec", ((hb, tnd), lambda j, i: (jnp.maximum(i * (tmd // hb) - 1, 0), j))),
                (gelu_cg, "tile", 0), (gelu_grad_cg, "tile", 0)],
        outs=[(SDS((L, DFF), BF16), "tile")] * 2 + [(SDS((3, DFF), F32), "colacc"), (SDS((1, DFF), F32), "colacc")])
    ncf = DFF // tcf

    def gate_bwd(c, r, dcg, halo, cw):
        halo = halo[:8] * (r < nr(tma) - 1).astype(F32)
        return (cw[2:3] * dcg + cw[1:2] * _shift_rows_up(dcg, halo, 1) + cw[0:1] * _shift_rows_up(dcg, halo, 2),)

    (d_gate,) = _ew(gate_bwd, [(d_cg, *_rc(tma, tcf)),
                               (d_cg, (hb, tcf), lambda c, r: (jnp.minimum((r + 1) * (tma // hb), L // hb - 1), c)),
                               (conv_w, *_col(3, tcf))],
                    [(SDS((L, DFF), BF16), *_rc(tma, tcf), None)], (ncf, nr(tma)), "ffn_gate_bwd")
    d_h2 = _mm(d_val, w_up_v, tb=True, name="d_h2_val")
    d_h2 = _mm(d_gate, w_up_g, tb=True, res=d_h2, name="d_h2_gate")
    assert tcf == 2 * DFF // N_DEV
    dw_up_v = _mm(h2, d_val, ta=True, out_dtype=BF16, tn=tcf, tk=2048, stack_out=True, name="dw_up_val")
    dw_up_g = _mm(h2, d_gate, ta=True, out_dtype=BF16, tn=tcf, tk=2048, stack_out=True, name="dw_up_gate")
    tok = gput("c", {"w_up_v": dw_up_v, "w_up_g": dw_up_g, "w_down": dw_down})
    d_x1, d_g2 = _rmsnorm_bwd(d_h2, x1, small["ffn_norm_g"] + tok[0, 0], d_x2, "norm2_bwd")

    dw_out = _mm(merged, d_x1, ta=True, out_dtype=BF16, name="dw_out")

    def dmerge_ep(i, cols, dm, _, a_ref, s_ref, ga, gs):
        sa, ss = _sigmoid(f32(ga, cols)), _sigmoid(f32(gs, cols))
        return dm * sa, dm * ss, dm * f32(a_ref, cols) * (sa * (1.0 - sa)), dm * f32(s_ref, cols) * (ss * (1.0 - ss))

    d_bra, d_brs, d_ga, d_gs = _mm(d_x1, wts["w_out"], tb=True, tm=512, tn=tnm, epilogue=dmerge_ep,
                                        extras=[(br_a, "tile", 0), (br_s, "tile", 0)] + gate_tiles,
                                        outs=[(SDS((L, D), BF16), "tile")] * 4, name="d_merged_bwd")
    d_attn = _mm(d_bra, wts["w_branch_attn"], tb=True, out_dtype=BF16, name="d_attn")
    dw_ba = _mm(attn, d_bra, ta=True, out_dtype=BF16, name="dw_branch_attn")
    d_ssm = _mm(d_brs, wts["w_branch_ssm"], tb=True, name="d_ssm")
    dw_bs = _mm(ssm, d_brs, ta=True, out_dtype=BF16, name="dw_branch_ssm")
    dq, dkv_cur, dkv_prev, d_sinks = _attn_bwd(projb, small["attn_sinks"], attn_bias, d_attn, "attn_bwd")

    def glu_bwd(c, r, ds, vb, gb):
        sg = _sigmoid(gb)
        return ds * sg, ds * vb * (sg * (1.0 - sg))

    d_glu_v, d_glu_g = _ew(glu_bwd, [(d_ssm, *_rc(tm, SW)), (glu, *_rc(tm, SW)), (glu, *_rc(tm, SW, 1))],
                           [(SDS((L, SW), F32), *_rc(tm, SW), None)] * 2, (1, nr(tm)), "glu_gate_bwd")
    d_glu = jnp.concatenate([d_glu_v, d_glu_g], axis=1)
    d_gy = _mm(d_glu, wts["w_glu"], tb=True, name="d_gelu_y")
    dw_glu = _mm(gy, d_glu, ta=True, out_dtype=BF16, name="dw_glu")

    tok = gput("b", {"w_glu": dw_glu, "w_branch_attn": dw_ba, "w_branch_ssm": dw_bs, "w_out": dw_out})
    ab = ab + tok[0, 0]

    def gelu_bwd(c, r, dg, yb, ub, dgl):
        dy = dg * _gelu_grad(yb)
        return dy, jnp.sum(dy * ub, axis=0, keepdims=True), jnp.sum(dgl, axis=0, keepdims=True)

    dy, d_ssm_d, d_b_glu = _ew(
        gelu_bwd, [(d_gy, *_rc(tm, 256)), (y, *_rc(tm, 256)), (proj, *_rc(tm, 256, C_U // 256)), (d_glu, *_rc(tm, 512))],
        [(SDS((L, SW), F32), *_rc(tm, 256), None), (SDS((1, SW), F32), *_col(1, 256), "r"),
         (SDS((1, 2 * SW), F32), *_col(1, 512), "r")], (2, nr(tm)), "ssm_gelu_bwd")

    dy_seg = _to_segments(dy)
    ends_r = _ssm_scan(dy_seg, cmat, ab, reverse=True, tk=64, name="ssm_ends_bwd")
    lam, dab8, du_seg = _ssm_scan(dy_seg, cmat, ab, reverse=True, ends=ends_r, xs=xs, init=init_f, wproj=bmat,
                                  name="ssm_scan_bwd")
    du_mm = _from_segments(du_seg)
    dbm = _mm(u_seg, lam, ta=True, tm=512, name="ssm_dbmat")
    dcm = _mm(dy_seg, xs, ta=True, tm=512, name="ssm_dcmat")
    d_are, d_aim, d_ldt, d_bre, d_bim, d_cre, d_cim = _ssm_param_bwd(
        small["a_re"], small["a_im"], small["logdt"], small["b_re"], small["b_im"], dab8, dbm, dcm, "ssm_param_bwd")

    nb = L // BLK

    def dproj_fn(c, r, dqb, cur, prv, du, dyb, dsk, dga, dgs):
        dkv = cur + prv * (r < nb - 1).astype(F32)
        dub = du + dsk * dyb
        full = jnp.concatenate([dqb, dkv, dub, jnp.zeros((BLK, C_GA - C_PAD), F32), dga, dgs], axis=1)
        return full, jnp.sum(full, axis=0, keepdims=True)

    rowb = lambda w: ((BLK, w), lambda c, r: (r, 0))
    dproj, d_b_in = _ew(
        dproj_fn, [(dq, *rowb(AW)), (dkv_cur, *rowb(256)),
                   (dkv_prev, (BLK, 256), lambda c, r: (jnp.minimum(r + 1, nb - 1), 0)),
                   (du_mm, *rowb(SW)), (dy, *rowb(SW)), (small["ssm_d"], *_col(1, SW)), (d_ga, *rowb(D)), (d_gs, *rowb(D))],
        [(SDS((L, INP), BF16), *rowb(INP), None), (SDS((1, INP), F32), *_col(1, INP), "all")], (1, nb), "dproj")
    dw_in = _mm(h, dproj, ta=True, out_dtype=BF16, name="dw_in")
    tok = gput("a", {"w_in": _unpad_cols(dw_in)})
    d_h = _mm(dproj, wts["w_in"], tb=True, bias=jnp.zeros((1, D), F32) + tok[0, 0], name="d_h")
    grad_x, d_g1 = _rmsnorm_bwd(d_h, x, small["attn_norm_g"], d_x1, "norm1_bwd")

    sgrads = {"attn_norm_g": d_g1, "b_in": _unpad_cols(d_b_in), "attn_sinks": d_sinks[:, :NQ], "a_re": d_are, "a_im": d_aim,
              "logdt": d_ldt, "b_re": d_bre, "b_im": d_bim, "c_re": d_cre, "c_im": d_cim, "ssm_d": d_ssm_d,
              "b_glu": d_b_glu, "ffn_norm_g": d_g2, "conv_w": d_conv_w, "conv_b": d_conv_b, "final_norm_g": d_gf}
    return loss, grad_x, sgrads


def _small_layouts(p):
    gp = lambda a: a.reshape(1, NS)
    hgp = lambda a: a.transpose(2, 0, 1).reshape(H, NS)
    chgp = lambda a: a.transpose(1, 0, 2).reshape(H, NS)
    return {
        "attn_norm_g": p["attn_norm_g"].reshape(1, D), "ffn_norm_g": p["ffn_norm_g"].reshape(1, D),
        "final_norm_g": p["final_norm_g"].reshape(1, D),
        "b_in_p": _pad_cols(p["b_in"].reshape(1, INC)),
        "attn_sinks": p["attn_sinks"].reshape(1, NQ),
        "a_re": gp(p["ssm_a_re"]), "a_im": gp(p["ssm_a_im"]), "logdt": jnp.repeat(p["ssm_log_dt"], P).reshape(1, NS),
        "b_re": hgp(p["ssm_b_re"]), "b_im": hgp(p["ssm_b_im"]), "c_re": chgp(p["ssm_c_re"]), "c_im": chgp(p["ssm_c_im"]),
        "ssm_d": p["ssm_d"].reshape(1, SW), "b_glu": p["b_glu"].reshape(1, 2 * SW),
        "conv_b": p["conv_b"].reshape(1, DFF),
    }


def _small_grads_to_param_shapes(sg):
    from_hgp = lambda a: a.reshape(H, G, P).transpose(1, 2, 0)
    from_chgp = lambda a: a.reshape(H, G, P).transpose(1, 0, 2)
    return {
        "attn_norm_g": sg["attn_norm_g"].reshape(D), "b_in": sg["b_in"].reshape(INC),
        "attn_sinks": sg["attn_sinks"].reshape(NQ),
        "ssm_a_re": sg["a_re"].reshape(G, P), "ssm_a_im": sg["a_im"].reshape(G, P),
        "ssm_log_dt": jnp.sum(sg["logdt"].reshape(G, P), axis=1),
        "ssm_b_re": from_hgp(sg["b_re"]), "ssm_b_im": from_hgp(sg["b_im"]),
        "ssm_c_re": from_chgp(sg["c_re"]), "ssm_c_im": from_chgp(sg["c_im"]),
        "ssm_d": sg["ssm_d"].reshape(SW), "b_glu": sg["b_glu"].reshape(2 * SW),
        "ffn_norm_g": sg["ffn_norm_g"].reshape(D), "conv_w": sg["conv_w"], "conv_b": sg["conv_b"].reshape(DFF),
        "final_norm_g": sg["final_norm_g"].reshape(D),
    }


def kernel(x, attn_norm_g, w_in, b_in, attn_sinks, ssm_a_re, ssm_a_im, ssm_log_dt, ssm_b_re, ssm_b_im, ssm_c_re, ssm_c_im, ssm_d, w_glu, b_glu, w_branch_attn, w_branch_ssm, w_out, ffn_norm_g, w_up, conv_w, conv_b, w_down, final_norm_g, loss_target, m_attn_norm_g, m_w_in, m_b_in, m_attn_sinks, m_ssm_a_re, m_ssm_a_im, m_ssm_log_dt, m_ssm_b_re, m_ssm_b_im, m_ssm_c_re, m_ssm_c_im, m_ssm_d, m_w_glu, m_b_glu, m_w_branch_attn, m_w_branch_ssm, m_w_out, m_ffn_norm_g, m_w_up, m_conv_w, m_conv_b, m_w_down, m_final_norm_g, v_attn_norm_g, v_w_in, v_b_in, v_attn_sinks, v_ssm_a_re, v_ssm_a_im, v_ssm_log_dt, v_ssm_b_re, v_ssm_b_im, v_ssm_c_re, v_ssm_c_im, v_ssm_d, v_w_glu, v_b_glu, v_w_branch_attn, v_w_branch_ssm, v_w_out, v_ffn_norm_g, v_w_up, v_conv_w, v_conv_b, v_w_down, v_final_norm_g):
    args = dict(locals())
    sq = lambda a: a if a.ndim == 1 else a[0]
    wv = {n: sq(args[n]) for n in _WEIGHTS}
    mv = {n: sq(args["m_" + n]) for n in _WEIGHTS}
    vv = {n: sq(args["v_" + n]) for n in _WEIGHTS}
    me = 4 * lax.axis_index("x") + 2 * lax.axis_index("y") + lax.axis_index("c")

    gather, tok = {}, jnp.zeros((8, 128), F32)
    for grp in ("a", "b", "c"):
        shards = [(wv[n] + tok[0, 0]).astype(BF16) for n in _GROUPS[grp]]
        if grp == "c":
            shards.append(jnp.pad(wv["conv_w"] + tok[0, 0], ((0, 5), (0, 64))))
        gather[grp], tok = _exchange_start(shards, False, "gather_start_" + grp)
    small = _small_layouts(wv)
    small["attn_norm_g"] = small["attn_norm_g"] + tok[0, 0]

    def own_slot(land, src):
        return lax.dynamic_update_slice_in_dim(land, src, me, axis=0)

    def wget(grp, after):
        thru, lands = _exchange_wait(gather[grp], after, False, "gather_wait_" + grp)
        full = {}
        for n, t, g in zip(_GROUPS[grp], thru, lands):
            g = own_slot(g, t[None])
            full[n] = _unstack_cols(g) if n in _COL_SHARDED else g.reshape(N_DEV * g.shape[1], g.shape[2])
        if grp == "a":
            full["w_in"] = _pad_cols(full["w_in"])
        if grp == "c":
            full["conv_w"] = _unstack_cols(own_slot(lands[-1], thru[-1][None])[:, :3, :DFF // N_DEV])
            g = own_slot(lands[0], thru[0][None])
            full["w_up_v"], full["w_up_g"] = _unstack_cols(g[:N_DEV // 2]), _unstack_cols(g[N_DEV // 2:])
            del full["w_up"]
        return full

    scatter = {}

    def gput(grp, grads):
        stacked = [_stack_cols(grads[n]) if n in _COL_SHARDED else grads[n].reshape(N_DEV, -1, D)
                   for n in _GROUPS[grp] if n != "w_up"]
        if grp == "c":
            half = N_DEV // 2
            stacked.insert(0, jnp.concatenate([grads["w_up_v"], grads["w_up_g"]]))
        scatter[grp], token = _exchange_start(stacked, True, "scatter_start_" + grp)
        return token

    loss, grad_x, sg = _local_step(x[0], loss_target[0], wget, small, gput)
    loss = lax.psum(loss, MESH_AXES)

    sgp = _small_grads_to_param_shapes(sg)
    small_names = [n for n in _SMALL]
    packed_g = _pack([sgp[n] for n in small_names])
    (small_all,) = _exchange([packed_g], False, "gather_small_grads")

    outs_g, outs_d, outs_m, outs_v = {}, {}, {}, {}
    for grp in ("c", "b", "a"):
        thru, lands = _exchange_wait(scatter[grp], small_all, True, "scatter_wait_" + grp)
        for n, t, pt in zip(_GROUPS[grp], thru, lands):
            pt = own_slot(pt, lax.dynamic_slice_in_dim(t, me, 1, axis=0))
            outs_g[n], outs_d[n], outs_m[n], outs_v[n] = _adam(pt, wv[n], mv[n], vv[n], "adam_" + n)

    sizes = [int(math.prod(sgp[n].shape)) for n in small_names]
    offs = [0]
    for s in sizes:
        offs.append(offs[-1] + s)

    def local_part(n, a):
        if n == "conv_w":
            return lax.dynamic_slice(a, (0, me * (DFF // N_DEV)), (3, DFF // N_DEV))
        return a

    rows = packed_g.shape[0]

    def sum_fn(cc, rr, pb):
        g = pb[0]
        for d in range(1, N_DEV):
            g = g + pb[d]
        return (g,)

    (gsum,) = _ew(sum_fn, [(small_all, (N_DEV, rows, 128), lambda cc, rr: (0, 0, 0))],
                  [(SDS((rows, 128), F32), (rows, 128), lambda cc, rr: (0, 0), None)], (1, 1), "sum_small_grads")
    gflat = gsum.reshape(-1)
    gsmall = {n: local_part(n, gflat[offs[i]:offs[i + 1]].reshape(sgp[n].shape)) for i, n in enumerate(small_names)}
    pw = _pack([wv[n] for n in small_names])
    pm = _pack([mv[n] for n in small_names])
    pv = _pack([vv[n] for n in small_names])
    pg = _pack([gsmall[n] for n in small_names])
    prow = pw.shape[0]

    def adam_small(cc, rr, gb, wb, mb, vb):
        return _adam_math(gb, wb, mb, vb)

    whole = ((prow, 128), lambda cc, rr: (0, 0))
    sd, sm, sv = _ew(adam_small, [(pg, *whole), (pw, *whole), (pm, *whole), (pv, *whole)],
                     [(SDS((prow, 128), F32), *whole, None)] * 3, (1, 1), "adam_small")
    lsizes = [int(math.prod(wv[n].shape)) for n in small_names]
    loffs = [0]
    for s in lsizes:
        loffs.append(loffs[-1] + s)
    for i, n in enumerate(small_names):
        take = lambda a: a.reshape(-1)[loffs[i]:loffs[i + 1]].reshape(wv[n].shape)
        outs_g[n], outs_d[n], outs_m[n], outs_v[n] = gsmall[n], take(sd), take(sm), take(sv)

    lead = lambda n, a: a if args[n].ndim == 1 else a[None]
    grad_x = grad_x[None]
    return (loss, grad_x, *[lead(n, outs_g[n]) for n in _WEIGHTS], *[lead(n, outs_d[n]) for n in _WEIGHTS],
            *[lead(n, outs_m[n]) for n in _WEIGHTS], *[lead(n, outs_v[n]) for n in _WEIGHTS])
```

```python
import functools
import math

import jax
import jax.numpy as jnp
from jax import lax
from jax.experimental import pallas as pl
from jax.experimental.pallas import tpu as pltpu

F32 = jnp.float32
BF16 = jnp.bfloat16
SDS = jax.ShapeDtypeStruct

N_DEV = 8
D = 2048
NQ, NKV, HD = 16, 2, 64
AW = NQ * HD
BLK = 128
SW, G, H, P = 512, 32, 16, 64
NS = G * P
DFF = 5632
INC = AW + 2 * NKV * HD + SW + 2 * D
C_K, C_U, C_PAD = AW, AW + 2 * NKV * HD, AW + 2 * NKV * HD + SW
C_GA, C_GS, INP = D, 2 * D, 3 * D
RMS_EPS = 1e-6
NEG_BIG = -1e30
ADAM_LR, ADAM_B1, ADAM_B2, ADAM_EPS, ADAM_WD, ADAM_STEP = 0.001, 0.9, 0.999, 1e-08, 0.01, 10
NSEG = 8
VMEM_CAP_MB = 60
MESH_AXES = ("x", "y", "c")


def _pad_cols(a):
    zeros = jnp.zeros(a.shape[:-1] + (C_GA - C_PAD,), a.dtype)
    return jnp.concatenate([a[..., :C_PAD], zeros, a[..., C_PAD:]], axis=-1)


def _unpad_cols(a):
    return jnp.concatenate([a[..., :C_PAD], a[..., C_GA:]], axis=-1)


def _cparams(sem, vmem_mb):
    return pltpu.CompilerParams(dimension_semantics=sem, vmem_limit_bytes=min(int(vmem_mb), VMEM_CAP_MB) << 20)


LANES = 128


def _tile(dim, pref):
    if dim <= pref:
        return dim
    for t in range(pref - pref % LANES, 0, -LANES):
        if dim % t == 0:
            return t
    raise ValueError(f"no tile for {dim}")


def _mm(a, b, *, ta=False, tb=False, bias=None, res=None, out_dtype=F32, tm=1024, tn=1024, tk=3072, name,
        a2=None, b2=None, extras=(), epilogue=None, outs=None, ep_cols=None, stack_out=False):
    m, k = (a.shape[1], a.shape[0]) if ta else a.shape
    n = b.shape[0] if tb else b.shape[1]
    assert (b.shape[1] if tb else b.shape[0]) == k, (a.shape, b.shape, ta, tb)
    tm, tn, tk = _tile(m, tm), _tile(n, tn), _tile(k, tk)
    nk = k // tk
    dims = (((0 if ta else 1,), (1 if tb else 0,)), ((), ()))
    has_bias, has_res, has_b2 = bias is not None, res is not None, b2 is not None
    has_a2 = a2 is not None
    assert not (has_b2 and (nk > 1 or ta or tb)) and not (has_a2 and not has_b2)
    if epilogue is None:
        outs = [(SDS((n // tn, m, tn) if stack_out else (m, n), out_dtype), "tile")]
    n_ex, n_out = len(extras), len(outs)
    tcn = tn if (epilogue is None or nk > 1 or ep_cols is None) else _tile(tn, ep_cols)

    def body(*refs):
        a_ref, b_ref = refs[0], refs[1]
        pos = 2
        a2_ref = refs[pos] if has_a2 else a_ref
        pos += has_a2
        b2_ref = refs[pos] if has_b2 else None
        pos += has_b2
        bias_ref = refs[pos] if has_bias else None
        pos += has_bias
        res_ref = refs[pos] if has_res else None
        pos += has_res
        ex_refs = refs[pos:pos + n_ex]
        o_refs = refs[pos + n_ex:pos + n_ex + n_out]
        i = pl.program_id(1)

        def product(rhs_ref, cols=None, lhs=None):
            rhs = rhs_ref[...] if cols is None else (rhs_ref[cols, :] if tb else rhs_ref[:, cols])
            lhs = a_ref[...].astype(BF16) if lhs is None else lhs
            return lax.dot_general(lhs, rhs.astype(BF16), dims, preferred_element_type=F32)

        def finish(r, cols):
            if has_bias:
                r = r + bias_ref[:, cols]
            if has_res:
                r = r + res_ref[:, cols].astype(F32)
            if epilogue is None:
                o_refs[0][:, cols] = r.astype(o_refs[0].dtype)
                return
            r2 = None
            if has_b2:
                r2 = jnp.dot(a2_ref[...].astype(BF16), b2_ref[:, cols].astype(BF16), preferred_element_type=F32)
            vals = epilogue(i, cols, r, r2, *ex_refs)
            for o_ref, v, (_, kind) in zip(o_refs, vals, outs):
                if kind == "tile":
                    o_ref[:, cols] = v.astype(o_ref.dtype)
                else:
                    @pl.when(i == 0)
                    def _(o_ref=o_ref, v=v):
                        o_ref[:, cols] = v.astype(o_ref.dtype)

                    @pl.when(i > 0)
                    def _(o_ref=o_ref, v=v):
                        o_ref[:, cols] += v.astype(o_ref.dtype)

        if nk == 1:
            lhs = a_ref[...].astype(BF16)
            for c0 in range(0, tn, tcn):
                cols = pl.ds(c0, tcn)
                finish(product(b_ref, cols, lhs), cols)
            return
        whole = pl.ds(0, tn)
        acc_ref = refs[-1]
        kk = pl.program_id(2)

        @pl.when(kk == 0)
        def _():
            acc_ref[...] = product(b_ref)

        @pl.when(jnp.logical_and(kk > 0, kk < nk - 1))
        def _():
            acc_ref[...] += product(b_ref)

        @pl.when(kk == nk - 1)
        def _():
            finish(acc_ref[...] + product(b_ref), whole)

    b_spec = pl.BlockSpec((tn, tk), lambda j, i, kk: (j, kk)) if tb else pl.BlockSpec((tk, tn), lambda j, i, kk: (kk, j))
    ins = [a, b]
    in_specs = [pl.BlockSpec((tk, tm), lambda j, i, kk: (kk, i)) if ta else pl.BlockSpec((tm, tk), lambda j, i, kk: (i, kk)),
                b_spec]
    tile_spec = pl.BlockSpec((tm, tn), lambda j, i, kk: (i, j))
    byt = 2 * tm * tk * a.dtype.itemsize + 2 * tk * tn * b.dtype.itemsize
    byt += (2 + has_b2) * 4 * tm * tn
    if has_a2:
        ins.append(a2)
        in_specs.append(pl.BlockSpec((tm, a2.shape[1]), lambda j, i, kk: (i, 0)))
        byt += 2 * tm * a2.shape[1] * a2.dtype.itemsize
    if has_b2:
        ins.append(b2)
        in_specs.append(pl.BlockSpec((b2.shape[0], tn), lambda j, i, kk: (0, j)))
        byt += 2 * b2.shape[0] * tn * b2.dtype.itemsize
    if has_bias:
        ins.append(bias)
        in_specs.append(pl.BlockSpec((1, tn), lambda j, i, kk: (0, j)))
    if has_res:
        ins.append(res)
        in_specs.append(tile_spec)
        byt += 2 * tm * tn * res.dtype.itemsize
    for arr, kind, arg in extras:
        ins.append(arr)
        if kind == "tile":
            in_specs.append(pl.BlockSpec((tm, tn), lambda j, i, kk, arg=arg: (i, j + arg)))
            byt += 2 * tm * tn * arr.dtype.itemsize + 4 * tm * tn
        elif kind == "col":
            in_specs.append(pl.BlockSpec((arr.shape[0], tn), lambda j, i, kk, arg=arg: (0, j + arg)))
        else:
            in_specs.append(pl.BlockSpec(arg[0], lambda j, i, kk, im=arg[1]: im(j, i)))
    out_specs = []
    for sds, kind in outs:
        if kind == "tile":
            out_specs.append(pl.BlockSpec((None, tm, tn), lambda j, i, kk: (j, i, 0)) if stack_out else tile_spec)
            byt += 2 * tm * tn * jnp.dtype(sds.dtype).itemsize
        else:
            out_specs.append(pl.BlockSpec((sds.shape[0], tn), lambda j, i, kk: (0, j)))
    res_ = pl.pallas_call(
        body, out_shape=tuple(o[0] for o in outs), grid=(n // tn, m // tm, nk), in_specs=in_specs,
        out_specs=tuple(out_specs), scratch_shapes=[pltpu.VMEM((tm, tn), F32)] if nk > 1 else [], name=name,
        compiler_params=_cparams(("arbitrary", "arbitrary", "arbitrary"), byt / 2**20 + (8 if epilogue is None else 20)),
    )(*ins)
    return res_[0] if epilogue is None else res_


def _ew(fn, ins, outs, grid, name, vmem_mb=40):
    n_in = len(ins)
    accs = [o[3] for o in outs]

    def body(*refs):
        c, r = pl.program_id(0), pl.program_id(1)
        vals = fn(c, r, *[ref[...].astype(F32) for ref in refs[:n_in]])
        for o_ref, v, acc in zip(refs[n_in:], vals, accs):
            if acc is None:
                o_ref[...] = v.astype(o_ref.dtype)
            else:
                first = (r == 0) if acc == "r" else jnp.logical_and(r == 0, c == 0)

                @pl.when(first)
                def _(o_ref=o_ref, v=v):
                    o_ref[...] = v.astype(o_ref.dtype)

                @pl.when(jnp.logical_not(first))
                def _(o_ref=o_ref, v=v):
                    o_ref[...] += v.astype(o_ref.dtype)

    res = pl.pallas_call(
        body, out_shape=tuple(o[0] for o in outs), grid=grid,
        in_specs=[pl.BlockSpec(bs, im) for _, bs, im in ins],
        out_specs=tuple(pl.BlockSpec(bs, im) for _, bs, im, _ in outs), name=name,
        compiler_params=_cparams(("arbitrary", "arbitrary"), vmem_mb),
    )(*[a for a, _, _ in ins])
    return res


def _rc(tm, tc, coff=0):
    return (tm, tc), (lambda c, r: (r, c + coff))


def _col(rows, tc, coff=0):
    return (rows, tc), (lambda c, r: (0, c + coff))


def _gelu(x):
    return 0.5 * x * (1.0 + lax.erf(x * (2.0 ** -0.5)))


def _gelu_and_grad(x):
    cdf = 0.5 * (1.0 + lax.erf(x * (2.0 ** -0.5)))
    return x * cdf, cdf + x * jnp.exp(-0.5 * x * x) * (1.0 / math.sqrt(2.0 * math.pi))


def _gelu_grad(x):
    return _gelu_and_grad(x)[1]


def _sigmoid(x):
    return 1.0 / (1.0 + jnp.exp(-x))


def _shift_rows(x, halo, s):
    rolled = pltpu.roll(x, s, 0)
    row8 = lax.broadcasted_iota(jnp.int32, halo.shape, 0)
    head = jnp.where(row8 < s, pltpu.roll(halo, s, 0), rolled[0:8])
    return jnp.concatenate([head, rolled[8:]], axis=0)


def _shift_rows_up(x, halo, s):
    tm = x.shape[0]
    rolled = pltpu.roll(x, tm - s, 0)
    row8 = lax.broadcasted_iota(jnp.int32, halo.shape, 0)
    tail = jnp.where(row8 >= 8 - s, pltpu.roll(halo, 8 - s, 0), rolled[tm - 8:])
    return jnp.concatenate([rolled[:tm - 8], tail], axis=0)


def _rmsnorm_fwd(x, g, name, tm=256):
    L = x.shape[0]

    def fn(c, r, xb, gb):
        rstd = lax.rsqrt(jnp.mean(xb * xb, axis=-1, keepdims=True) + RMS_EPS)
        return ((xb * rstd) * gb,)

    return _ew(fn, [(x, *_rc(tm, D)), (g, *_col(1, D))], [(SDS((L, D), BF16), *_rc(tm, D), None)], (1, L // tm), name)[0]


def _rmsnorm_bwd(dh, x, g, dres, name, tm=256):
    L = x.shape[0]

    def fn(c, r, dhb, xb, gb, drb):
        rstd = lax.rsqrt(jnp.mean(xb * xb, axis=-1, keepdims=True) + RMS_EPS)
        y = xb * rstd
        dy = dhb * gb
        dx = rstd * (dy - y * jnp.mean(dy * y, axis=-1, keepdims=True))
        return drb + dx, jnp.sum(dhb * y, axis=0, keepdims=True)

    return _ew(fn, [(dh, *_rc(tm, D)), (x, *_rc(tm, D)), (g, *_col(1, D)), (dres, *_rc(tm, D))],
               [(SDS((L, D), F32), *_rc(tm, D), None), (SDS((1, D), F32), *_col(1, D), "all")], (1, L // tm), name)


def _final_loss(x2, g, tgt, name, tm=256):
    L = x2.shape[0]

    def fn(c, r, xb, gb, tb):
        rstd = lax.rsqrt(jnp.mean(xb * xb, axis=-1, keepdims=True) + RMS_EPS)
        y = xb * rstd
        err = y * gb - tb
        dout = err * (1.0 / D)
        dy = dout * gb
        dx = rstd * (dy - y * jnp.mean(dy * y, axis=-1, keepdims=True))
        return dx, dx, jnp.sum(err * err, axis=0, keepdims=True) * (0.5 / D), jnp.sum(dout * y, axis=0, keepdims=True)

    return _ew(fn, [(x2, *_rc(tm, D)), (g, *_col(1, D)), (tgt, *_rc(tm, D))],
               [(SDS((L, D), F32), *_rc(tm, D), None), (SDS((L, D), BF16), *_rc(tm, D), None),
                (SDS((1, D), F32), *_col(1, D), "all"),
                (SDS((1, D), F32), *_col(1, D), "all")], (1, L // tm), name)


def _slope(h):
    return 2.0 ** (-8.0 * (h + 1) / NQ)


def _attn_bias():
    qi = lax.broadcasted_iota(jnp.int32, (BLK, 2 * BLK), 0)
    si = lax.broadcasted_iota(jnp.int32, (BLK, 2 * BLK), 1)
    dist = qi + BLK - si
    band = (dist >= 0) & (dist < BLK)
    slopes = jnp.asarray([_slope(h) for h in range(NQ)], F32)[:, None, None]
    alibi = -slopes * dist.astype(F32)[None]
    return jnp.stack([jnp.where((band & (si >= BLK))[None], alibi, NEG_BIG), jnp.where(band[None], alibi, NEG_BIG)])


def _attn_kv(kvc, kvp):
    kv = jnp.concatenate([kvp, kvc], axis=0).astype(F32)
    lo = lax.broadcasted_iota(jnp.int32, (2 * BLK, 128), 1) < HD

    def halves(t):
        tr = pltpu.roll(t, HD, 1)
        z = jnp.zeros_like(t)
        return {(0, 0): jnp.where(lo, t, z).astype(BF16), (0, 1): jnp.where(lo, z, tr).astype(BF16),
                (1, 0): jnp.where(lo, tr, z).astype(BF16), (1, 1): jnp.where(lo, z, t).astype(BF16)}

    return halves(kv[:, :128]), halves(kv[:, 128:])


_NT = (((1,), (1,)), ((), ()))
_TN = (((0,), (0,)), ((), ()))
_ATTN_SPECS = [pl.BlockSpec(memory_space=pltpu.SMEM),
               pl.BlockSpec((None, NQ, BLK, 2 * BLK), lambda n: (jnp.minimum(n, 1), 0, 0, 0)),
               pl.BlockSpec((BLK, AW), lambda n: (n, 0)),
               pl.BlockSpec((BLK, 256), lambda n: (n, C_K // 256)),
               pl.BlockSpec((BLK, 256), lambda n: (jnp.maximum(n - 1, 0), C_K // 256))]


def _attn_scores(q_ref, bias_ref, kmat, sc_ref):
    for j in range(NQ // 2):
        qs = q_ref[:, 128 * j:128 * (j + 1)] * (HD ** -0.5)
        for e in range(2):
            h = 2 * j + e
            sc_ref[h] = lax.dot_general(qs, kmat[(j // (NQ // 4), e)], _NT, preferred_element_type=F32) + bias_ref[h]


def _softmax_with_sink(s, sink):
    m = jnp.maximum(jnp.max(s, axis=-1, keepdims=True), sink)
    p = jnp.exp(s - m)
    esink = jnp.exp(sink - m)
    den = jnp.sum(p, axis=-1, keepdims=True) + esink
    return p / den, esink / den


def _attn_fwd(projb, sinks, bias, name):
    L = projb.shape[0]

    def body(s_ref, bias_ref, q_ref, kvc_ref, kvp_ref, o_ref, sc_ref, pr_ref):
        kmat, vmat = _attn_kv(kvc_ref[...], kvp_ref[...])
        _attn_scores(q_ref, bias_ref, kmat, sc_ref)
        for h in range(NQ):
            pr_ref[h] = _softmax_with_sink(sc_ref[h], s_ref[0, h])[0].astype(BF16)
        for j in range(NQ // 2):
            g = j // (NQ // 4)
            acc = jnp.dot(pr_ref[2 * j], vmat[(g, 0)], preferred_element_type=F32)
            acc = acc + jnp.dot(pr_ref[2 * j + 1], vmat[(g, 1)], preferred_element_type=F32)
            o_ref[:, 128 * j:128 * (j + 1)] = acc.astype(BF16)

    return pl.pallas_call(
        body, out_shape=SDS((L, AW), BF16), grid=(L // BLK,), in_specs=_ATTN_SPECS,
        out_specs=pl.BlockSpec((BLK, AW), lambda n: (n, 0)), name=name,
        scratch_shapes=[pltpu.VMEM((NQ, BLK, 2 * BLK), F32), pltpu.VMEM((NQ, BLK, 2 * BLK), BF16)],
        compiler_params=_cparams(("arbitrary",), 32),
    )(sinks, bias, projb, projb, projb)


def _attn_bwd(projb, sinks, bias, dattn, name):
    L = projb.shape[0]

    def body(s_ref, bias_ref, q_ref, kvc_ref, kvp_ref, do_ref, dq_ref, dcur_ref, dprev_ref, dsink_ref,
             sc_ref, dp_ref, ds_ref, pr_ref):
        n = pl.program_id(0)
        kmat, vmat = _attn_kv(kvc_ref[...], kvp_ref[...])
        _attn_scores(q_ref, bias_ref, kmat, sc_ref)
        for h in range(NQ):
            j, e = h // 2, h % 2
            dp_ref[h] = lax.dot_general(do_ref[:, 128 * j:128 * (j + 1)], vmat[(j // (NQ // 4), e)], _NT,
                                        preferred_element_type=F32)
        lane = lax.broadcasted_iota(jnp.int32, (1, 128), 1)
        dsv = jnp.zeros((1, 128), F32)
        for h in range(NQ):
            p, psink = _softmax_with_sink(sc_ref[h], s_ref[0, h])
            dp = dp_ref[h]
            drow = jnp.sum(p * dp, axis=-1, keepdims=True)
            ds_ref[h] = (p * (dp - drow)).astype(BF16)
            pr_ref[h] = p.astype(BF16)
            dsv = dsv + jnp.where(lane == h, -jnp.sum(psink * drow, axis=0, keepdims=True), 0.0)
        lo128 = lax.broadcasted_iota(jnp.int32, (BLK, 128), 1) < HD
        dk = [jnp.zeros((2 * BLK, 128), F32) for _ in range(NKV)]
        dv = [jnp.zeros((2 * BLK, 128), F32) for _ in range(NKV)]
        for j in range(NQ // 2):
            g = j // (NQ // 4)
            qs = q_ref[:, 128 * j:128 * (j + 1)] * (HD ** -0.5)
            dop = do_ref[:, 128 * j:128 * (j + 1)]
            zb = jnp.zeros_like(qs)
            dqp = jnp.zeros((BLK, 128), F32)
            for e in range(2):
                h = 2 * j + e
                half = lo128 if e == 0 else jnp.logical_not(lo128)
                dqp = dqp + jnp.dot(ds_ref[h], kmat[(g, e)], preferred_element_type=F32)
                dk[g] = dk[g] + lax.dot_general(ds_ref[h], jnp.where(half, qs, zb), _TN, preferred_element_type=F32)
                dv[g] = dv[g] + lax.dot_general(pr_ref[h], jnp.where(half, dop, zb), _TN, preferred_element_type=F32)
            dq_ref[:, 128 * j:128 * (j + 1)] = (dqp * (HD ** -0.5)).astype(BF16)
        lo256 = lax.broadcasted_iota(jnp.int32, (2 * BLK, 128), 1) < HD
        tot = [t + pltpu.roll(t, HD, 1) for t in (dk[0], dk[1], dv[0], dv[1])]
        dkv = jnp.concatenate([jnp.where(lo256, tot[0], tot[1]), jnp.where(lo256, tot[2], tot[3])], axis=1)
        dprev_ref[...] = dkv[:BLK]
        dcur_ref[...] = dkv[BLK:]

        @pl.when(n == 0)
        def _():
            dsink_ref[...] = dsv

        @pl.when(n > 0)
        def _():
            dsink_ref[...] += dsv

    tile = (NQ, BLK, 2 * BLK)
    return pl.pallas_call(
        body, out_shape=(SDS((L, AW), BF16), SDS((L, 256), F32), SDS((L, 256), F32), SDS((1, 128), F32)), grid=(L // BLK,),
        in_specs=_ATTN_SPECS + [pl.BlockSpec((BLK, AW), lambda n: (n, 0))],
        out_specs=(pl.BlockSpec((BLK, AW), lambda n: (n, 0)), pl.BlockSpec((BLK, 256), lambda n: (n, 0)),
                   pl.BlockSpec((BLK, 256), lambda n: (n, 0)), pl.BlockSpec((1, 128), lambda n: (0, 0))),
        scratch_shapes=[pltpu.VMEM(tile, F32), pltpu.VMEM(tile, F32), pltpu.VMEM(tile, BF16), pltpu.VMEM(tile, BF16)],
        name=name, compiler_params=_cparams(("arbitrary",), 40),
    )(sinks, bias, projb, projb, projb, dattn)


def _disc(a_re, a_im, logdt, b_re, b_im):
    dt = jnp.exp(logdt)
    mag = jnp.exp(a_re * dt)
    ab_re = mag * jnp.cos(a_im * dt)
    ab_im = mag * jnp.sin(a_im * dt)
    nr = ab_re - 1.0
    ni = ab_im
    den = a_re * a_re + a_im * a_im
    z_re = (nr * a_re + ni * a_im) / den
    z_im = (ni * a_re - nr * a_im) / den
    return ab_re, ab_im, z_re * b_re - z_im * b_im, z_re * b_im + z_im * b_re


def _group_mask():
    row = lax.broadcasted_iota(jnp.int32, (SW, NS), 0) // H
    col = lax.broadcasted_iota(jnp.int32, (SW, NS), 1) // P
    return row == col


def _block_diag(re, im):
    mask = _group_mask()
    z = jnp.zeros((SW, NS), F32)
    return jnp.concatenate([jnp.where(mask, jnp.tile(re, (G, 1)), z), jnp.where(mask, jnp.tile(im, (G, 1)), z)], axis=1)


def _block_diag_t(big):
    mask = _group_mask()
    z = jnp.zeros((SW, NS), F32)
    re = jnp.sum(jnp.where(mask, big[:, :NS], z).reshape(G, H, NS), axis=0)
    im = jnp.sum(jnp.where(mask, big[:, NS:], z).reshape(G, H, NS), axis=0)
    return re, im


def _ssm_prep(a_re, a_im, logdt, b_re, b_im, c_re, c_im, name):
    def body(are, aim, ldt, bre, bim, cre, cim, ab_ref, bm_ref, cm_ref):
        ab_re, ab_im, bb_re, bb_im = _disc(are[...], aim[...], ldt[...], bre[...], bim[...])
        ab_ref[...] = jnp.concatenate([ab_re, ab_im], axis=1)
        bm_ref[...] = _block_diag(bb_re, bb_im).astype(BF16)
        cm_ref[...] = _block_diag(cre[...], -cim[...]).astype(BF16)

    return pl.pallas_call(body, out_shape=(SDS((1, 2 * NS), F32), SDS((SW, 2 * NS), BF16), SDS((SW, 2 * NS), BF16)),
                          name=name, compiler_params=pltpu.CompilerParams(vmem_limit_bytes=48 << 20),
                          )(a_re, a_im, logdt, b_re, b_im, c_re, c_im)


def _ssm_param_bwd(a_re, a_im, logdt, b_re, b_im, dab8, dbm, dcm, name):
    def body(are, aim, ldt, bre, bim, dab_ref, dbm_ref, dcm_ref, o_are, o_aim, o_ldt, o_bre, o_bim, o_cre, o_cim):
        dab = jnp.sum(dab_ref[...], axis=0, keepdims=True)
        dbb_re, dbb_im = _block_diag_t(dbm_ref[...])
        _, vjp = jax.vjp(_disc, are[...], aim[...], ldt[...], bre[...], bim[...])
        d_are, d_aim, d_ldt, d_bre, d_bim = vjp((dab[:, :NS], dab[:, NS:], dbb_re, dbb_im))
        o_are[...], o_aim[...], o_ldt[...], o_bre[...], o_bim[...] = d_are, d_aim, d_ldt, d_bre, d_bim
        dc_re, dc_imn = _block_diag_t(dcm_ref[...])
        o_cre[...] = dc_re
        o_cim[...] = -dc_imn

    v1, vh = SDS((1, NS), F32), SDS((H, NS), F32)
    return pl.pallas_call(body, out_shape=(v1, v1, v1, vh, vh, vh, vh), name=name,
                          compiler_params=pltpu.CompilerParams(vmem_limit_bytes=56 << 20),
                          )(a_re, a_im, logdt, b_re, b_im, dab8, dbm, dcm)


def _ssm_scan(src, wmat, ab, *, reverse, ends=None, xs=None, init=None, wproj=None, name, tk=32):
    L = src.shape[0]
    rows = NSEG * tk
    nch = L // rows
    seg_len = L // NSEG
    n_sq = int(math.log2(seg_len))
    assert 2 ** n_sq == seg_len and L % rows == 0
    first_pass = ends is None
    with_dab = (not first_pass) and reverse
    with_proj = wproj is not None
    assert not (with_proj and first_pass)
    slab = 512
    n_slab = NS // slab

    def body(*refs):
        src_ref, w_ref, ab_ref = refs[:3]
        pos = 3
        if not first_pass:
            ends_ref = refs[pos]
            pos += 1
        if with_dab:
            xs_ref, xsh_ref, init_ref = refs[pos:pos + 3]
            pos += 3
        if with_proj:
            wproj_ref = refs[pos]
            pos += 1
        if first_pass:
            (e_ref,) = refs[pos:pos + 1]
            pos += 1
        else:
            st_out_ref, aux_ref = refs[pos:pos + 2]
            pos += 2
        if with_proj:
            proj_ref = refs[pos]
            pos += 1
        buf_ref, st_ref = refs[pos:pos + 2]
        i = pl.program_id(0)
        a_re = ab_ref[:, :NS]
        a_im = -ab_ref[:, NS:] if reverse else ab_ref[:, NS:]

        @pl.when(i == 0)
        def _():
            if first_pass:
                st_ref[...] = jnp.zeros_like(st_ref)
            else:
                pr, pi = a_re, a_im
                for _ in range(n_sq):
                    pr, pi = pr * pr - pi * pi, 2.0 * pr * pi
                zr = jnp.zeros((1, NS), F32)
                cr, ci = zr, zr
                order = list(range(NSEG - 1, -1, -1)) if reverse else list(range(NSEG))
                st_ref[order[0]:order[0] + 1, :] = jnp.zeros((1, 2 * NS), F32)
                for jprev, j in zip(order[:-1], order[1:]):
                    er, ei = ends_ref[jprev:jprev + 1, :NS], ends_ref[jprev:jprev + 1, NS:]
                    cr, ci = er + pr * cr - pi * ci, ei + pr * ci + pi * cr
                    st_ref[j:j + 1, :NS] = cr
                    st_ref[j:j + 1, NS:] = ci
                if not reverse:
                    aux_ref[...] = st_ref[...]
                else:
                    aux_ref[...] = jnp.zeros_like(aux_ref)

        buf_ref[...] = jnp.dot(src_ref[...].astype(BF16), w_ref[...], preferred_element_type=F32)

        for s in range(n_slab):
            re_sl, im_sl = pl.ds(s * slab, slab), pl.ds(NS + s * slab, slab)
            ar = jnp.broadcast_to(a_re[:, s * slab:(s + 1) * slab], (NSEG, slab))
            ai = jnp.broadcast_to(a_im[:, s * slab:(s + 1) * slab], (NSEG, slab))

            def step(t, carry, re_sl=re_sl, im_sl=im_sl, ar=ar, ai=ai):
                k = (tk - 1 - t) if reverse else t
                r0 = pl.multiple_of(k * NSEG, NSEG)
                xr, xi = carry[0], carry[1]
                nr = ar * xr - ai * xi + buf_ref[pl.ds(r0, NSEG), re_sl]
                ni = ar * xi + ai * xr + buf_ref[pl.ds(r0, NSEG), im_sl]
                if not first_pass:
                    buf_ref[pl.ds(r0, NSEG), re_sl] = nr
                    buf_ref[pl.ds(r0, NSEG), im_sl] = ni
                if not with_dab:
                    return nr, ni
                rp = pl.multiple_of((k - 1) * NSEG, NSEG)
                xpr, xpi = xs_ref[pl.ds(rp, NSEG), re_sl], xs_ref[pl.ds(rp, NSEG), im_sl]
                return nr, ni, carry[2] + nr * xpr + ni * xpi, carry[3] + ni * xpr - nr * xpi

            carry = (st_ref[:, re_sl], st_ref[:, im_sl])
            if with_dab:
                z = jnp.zeros((NSEG, slab), F32)
                carry = lax.fori_loop(0, tk - 1, step, carry + (z, z))
                xr, xi, dr, di = carry
                nr = ar * xr - ai * xi + buf_ref[pl.ds(0, NSEG), re_sl]
                ni = ar * xi + ai * xr + buf_ref[pl.ds(0, NSEG), im_sl]
                buf_ref[pl.ds(0, NSEG), re_sl] = nr
                buf_ref[pl.ds(0, NSEG), im_sl] = ni
                at_start = i == nch - 1
                xpr = jnp.where(at_start, init_ref[:, re_sl], xsh_ref[:, re_sl])
                xpi = jnp.where(at_start, init_ref[:, im_sl], xsh_ref[:, im_sl])
                aux_ref[:, re_sl] += dr + nr * xpr + ni * xpi
                aux_ref[:, im_sl] += di + ni * xpr - nr * xpi
                carry = (nr, ni)
            else:
                carry = lax.fori_loop(0, tk, step, carry)
            st_ref[:, re_sl] = carry[0]
            st_ref[:, im_sl] = carry[1]

        if first_pass:
            @pl.when(i == nch - 1)
            def _():
                e_ref[...] = st_ref[...]
        else:
            st_out_ref[...] = buf_ref[...].astype(st_out_ref.dtype)
            if with_proj:
                proj_ref[...] = lax.dot_general(buf_ref[...].astype(BF16), wproj_ref[...], _NT, preferred_element_type=F32)

    chunk = (lambda i: (nch - 1 - i, 0)) if reverse else (lambda i: (i, 0))
    whole = lambda i: (0, 0)
    ins = [src, wmat, ab]
    in_specs = [pl.BlockSpec((rows, SW), chunk), pl.BlockSpec((SW, 2 * NS), whole), pl.BlockSpec((1, 2 * NS), whole)]
    small = SDS((NSEG, 2 * NS), F32)
    small_spec = pl.BlockSpec((NSEG, 2 * NS), whole)
    if not first_pass:
        ins.append(ends)
        in_specs.append(small_spec)
    if with_dab:
        ins += [xs, xs, init]
        in_specs += [pl.BlockSpec((rows, 2 * NS), chunk),
                     pl.BlockSpec((NSEG, 2 * NS), lambda i: (jnp.maximum((nch - 1 - i) * tk - 1, 0), 0)),
                     small_spec]
    if with_proj:
        ins.append(wproj)
        in_specs.append(pl.BlockSpec((SW, 2 * NS), whole))
    if first_pass:
        out_shape, out_specs = small, small_spec
    else:
        out_shape = (SDS((L, 2 * NS), BF16 if reverse else F32), small)
        out_specs = (pl.BlockSpec((rows, 2 * NS), chunk), small_spec)
        if with_proj:
            out_shape += (SDS((L, SW), F32),)
            out_specs += (pl.BlockSpec((rows, SW), chunk),)
    return pl.pallas_call(
        body, out_shape=out_shape, grid=(nch,), in_specs=in_specs, out_specs=out_specs,
        scratch_shapes=[pltpu.VMEM((rows, 2 * NS), F32), pltpu.VMEM((NSEG, 2 * NS), F32)], name=name,
        compiler_params=_cparams(("arbitrary",), 56),
    )(*ins)


def _to_segments(a):
    L, c = a.shape
    return a.reshape(NSEG, L // NSEG, c).transpose(1, 0, 2).reshape(L, c)


def _from_segments(a):
    L, c = a.shape
    return a.reshape(L // NSEG, NSEG, c).transpose(1, 0, 2).reshape(L, c)


def _peer(x, y, c, m):
    return ((1 - x) if (m >> 2) & 1 else x, (1 - y) if (m >> 1) & 1 else y, (1 - c) if m & 1 else c)


def _dev_index(p):
    return 4 * p[0] + 2 * p[1] + p[2]


def _exchange(arrs, scatter, name):
    n = len(arrs)

    def body(*refs):
        ins, outs = refs[:n], refs[n:2 * n]
        send_sems, recv_sems, loc_sems = refs[2 * n:]
        x, y, c = lax.axis_index("x"), lax.axis_index("y"), lax.axis_index("c")
        me = _dev_index((x, y, c))

        def src(w, to):
            return ins[w].at[to] if scatter else ins[w]

        def local(w):
            return pltpu.make_async_copy(src(w, me), outs[w].at[me], loc_sems.at[w])

        def remote(w, m):
            peer = _peer(x, y, c, m)
            return pltpu.make_async_remote_copy(src_ref=src(w, _dev_index(peer)), dst_ref=outs[w].at[me],
                                                send_sem=send_sems.at[w, m - 1], recv_sem=recv_sems.at[w, m - 1],
                                                device_id=peer, device_id_type=pl.DeviceIdType.MESH)

        def arrival(w, m):
            peer = _peer(x, y, c, m)
            return pltpu.make_async_remote_copy(src_ref=src(w, me), dst_ref=outs[w].at[_dev_index(peer)],
                                                send_sem=send_sems.at[w, m - 1], recv_sem=recv_sems.at[w, m - 1],
                                                device_id=peer, device_id_type=pl.DeviceIdType.MESH)

        for w in range(n):
            local(w).start()
        for w in range(n):
            for m in range(1, N_DEV):
                remote(w, m).start()
        for w in range(n):
            for m in range(1, N_DEV):
                arrival(w, m).wait_recv()
        for w in range(n):
            for m in range(1, N_DEV):
                remote(w, m).wait_send()
        for w in range(n):
            local(w).wait()

    anyspec = pl.BlockSpec(memory_space=pl.ANY)
    out_shape = tuple(SDS(a.shape if scatter else (N_DEV,) + a.shape, a.dtype) for a in arrs)
    return pl.pallas_call(
        body, out_shape=out_shape, in_specs=[anyspec] * n, out_specs=tuple([anyspec] * n),
        scratch_shapes=[pltpu.SemaphoreType.DMA((n, N_DEV - 1)), pltpu.SemaphoreType.DMA((n, N_DEV - 1)),
                        pltpu.SemaphoreType.DMA((n,))],
        name=name, compiler_params=pltpu.CompilerParams(has_side_effects=True),
    )(*arrs)


_HBM = pl.BlockSpec(memory_space=pltpu.HBM)
_SEM = pl.BlockSpec(memory_space=pltpu.SEMAPHORE)
_EFFECT = pltpu.SideEffectType.DATAFLOW_SIDE_EFFECTING


def _sem_index(w, m):
    return w * (N_DEV - 1) + m - 1


_ALL_MASKS = tuple(range(1, N_DEV))
_CHIP_MASKS = (2, 4, 6)
_FIRST_HOP_MASKS = (1,) + _CHIP_MASKS


def _exchange_start(arrs, scatter, name, masks=_ALL_MASKS):
    n = len(arrs)
    lands = [lax.empty(a.shape if scatter else (N_DEV,) + a.shape, a.dtype) for a in arrs]

    def body(*refs):
        ins, zones = refs[:n], refs[n:2 * n]
        send_sems, recv_sems = refs[2 * n], refs[2 * n + 1]
        token = refs[-1]
        x, y, c = lax.axis_index("x"), lax.axis_index("y"), lax.axis_index("c")
        me = _dev_index((x, y, c))
        for w in range(n):
            for m in masks:
                peer = _peer(x, y, c, m)
                pltpu.make_async_remote_copy(
                    src_ref=ins[w].at[_dev_index(peer)] if scatter else ins[w], dst_ref=zones[w].at[me],
                    send_sem=send_sems.at[_sem_index(w, m)], recv_sem=recv_sems.at[_sem_index(w, m)],
                    device_id=peer, device_id_type=pl.DeviceIdType.MESH).start()
        token[...] = jnp.zeros_like(token)

    sems = pltpu.SemaphoreType.DMA((n * (N_DEV - 1),))
    res = pl.pallas_call(
        body, name=name,
        out_shape=(sems, sems, *[pltpu.HBM(a.shape, a.dtype) for a in arrs], *[pltpu.HBM(z.shape, z.dtype) for z in lands],
                   SDS((8, 128), F32)),
        in_specs=[_HBM] * (2 * n), out_specs=(_SEM, _SEM, *([_HBM] * (2 * n)), pl.BlockSpec(memory_space=pltpu.VMEM)),
        input_output_aliases={i: 2 + i for i in range(2 * n)},
        compiler_params=pltpu.CompilerParams(has_side_effects=_EFFECT),
    )(*[pltpu.with_memory_space_constraint(a, pltpu.HBM) for a in arrs],
      *[pltpu.with_memory_space_constraint(z, pltpu.HBM) for z in lands])
    return (res[0], res[1], list(res[2:2 + n]), list(res[2 + n:2 + 2 * n])), res[-1]


def _exchange_wait(handle, after, scatter, name, masks=_ALL_MASKS):
    send_sems, recv_sems, thru, lands = handle
    n = len(thru)

    def body(*refs):
        ins, zones = refs[:n], refs[n:2 * n]
        send_sems, recv_sems = refs[2 * n], refs[2 * n + 1]
        x, y, c = lax.axis_index("x"), lax.axis_index("y"), lax.axis_index("c")
        me = _dev_index((x, y, c))
        for w in range(n):
            for m in masks:
                peer = _peer(x, y, c, m)
                copy = pltpu.make_async_remote_copy(
                    src_ref=ins[w].at[me] if scatter else ins[w], dst_ref=zones[w].at[_dev_index(peer)],
                    send_sem=send_sems.at[_sem_index(w, m)], recv_sem=recv_sems.at[_sem_index(w, m)],
                    device_id=peer, device_id_type=pl.DeviceIdType.MESH)
                copy.wait_send()
                copy.wait_recv()

    res = pl.pallas_call(
        body, name=name,
        out_shape=(*[pltpu.HBM(a.shape, a.dtype) for a in thru], *[pltpu.HBM(z.shape, z.dtype) for z in lands]),
        in_specs=[_HBM] * (2 * n) + [_SEM, _SEM, pl.BlockSpec(memory_space=pl.ANY)], out_specs=tuple([_HBM] * (2 * n)),
        input_output_aliases={i: i for i in range(2 * n)},
        compiler_params=pltpu.CompilerParams(has_side_effects=_EFFECT),
    )(*thru, *lands, send_sems, recv_sems, after)
    return list(res[:n]), list(res[n:])


def _forward_start(zones, name):
    n = len(zones)

    def body(*refs):
        zs = refs[:n]
        send_sems, recv_sems = refs[n], refs[n + 1]
        token = refs[-1]
        x, y, c = lax.axis_index("x"), lax.axis_index("y"), lax.axis_index("c")
        for w in range(n):
            for m in _CHIP_MASKS:
                slot = zs[w].at[_dev_index(_peer(x, y, c, m))]
                pltpu.make_async_remote_copy(
                    src_ref=slot, dst_ref=slot, send_sem=send_sems.at[_sem_index(w, m)],
                    recv_sem=recv_sems.at[_sem_index(w, m)], device_id=(x, y, 1 - c),
                    device_id_type=pl.DeviceIdType.MESH).start()
        token[...] = jnp.zeros_like(token)

    sems = pltpu.SemaphoreType.DMA((n * (N_DEV - 1),))
    res = pl.pallas_call(
        body, name=name, out_shape=(sems, sems, *[pltpu.HBM(z.shape, z.dtype) for z in zones], SDS((8, 128), F32)),
        in_specs=[_HBM] * n, out_specs=(_SEM, _SEM, *([_HBM] * n), pl.BlockSpec(memory_space=pltpu.VMEM)),
        input_output_aliases={i: 2 + i for i in range(n)},
        compiler_params=pltpu.CompilerParams(has_side_effects=_EFFECT),
    )(*[pltpu.with_memory_space_constraint(z, pltpu.HBM) for z in zones])
    return (res[0], res[1], list(res[2:2 + n])), res[-1]


def _forward_wait(handle, after, name):
    send_sems, recv_sems, zones = handle
    n = len(zones)

    def body(*refs):
        zs = refs[:n]
        send_sems, recv_sems = refs[n], refs[n + 1]
        x, y, c = lax.axis_index("x"), lax.axis_index("y"), lax.axis_index("c")
        for w in range(n):
            for m in _CHIP_MASKS:
                copy = pltpu.make_async_remote_copy(
                    src_ref=zs[w].at[_dev_index(_peer(x, y, c, m))], dst_ref=zs[w].at[_dev_index(_peer(x, y, 1 - c, m))],
                    send_sem=send_sems.at[_sem_index(w, m)], recv_sem=recv_sems.at[_sem_index(w, m)],
                    device_id=(x, y, 1 - c), device_id_type=pl.DeviceIdType.MESH)
                copy.wait_send()
                copy.wait_recv()

    res = pl.pallas_call(
        body, name=name, out_shape=tuple(pltpu.HBM(z.shape, z.dtype) for z in zones),
        in_specs=[_HBM] * n + [_SEM, _SEM, pl.BlockSpec(memory_space=pl.ANY)], out_specs=tuple([_HBM] * n),
        input_output_aliases={i: i for i in range(n)},
        compiler_params=pltpu.CompilerParams(has_side_effects=_EFFECT),
    )(*zones, send_sems, recv_sems, after)
    return list(res)


def _adam_math(g, w, m, v):
    m = ADAM_B1 * m + (1.0 - ADAM_B1) * g
    v = ADAM_B2 * v + (1.0 - ADAM_B2) * (g * g)
    m_hat = m / (1.0 - ADAM_B1 ** ADAM_STEP)
    v_hat = v / (1.0 - ADAM_B2 ** ADAM_STEP)
    delta = -ADAM_LR * (m_hat / (jnp.sqrt(v_hat) + ADAM_EPS) + ADAM_WD * w)
    return delta, m, v


def _adam(parts, w, m, v, name, tr=128):
    r, c = w.shape
    tr = next(t for t in (tr, 64, 32, 16, 8) if r % t == 0)

    def fn(cc, rr, pb, wb, mb, vb):
        g = pb[0].astype(F32)
        for d in range(1, N_DEV):
            g = g + pb[d].astype(F32)
        delta, nm, nv = _adam_math(g, wb, mb, vb)
        return g, delta, nm, nv

    blk = ((tr, c), lambda cc, rr: (rr, 0))
    o = SDS((r, c), F32)
    return _ew(fn, [(parts, (N_DEV, tr, c), lambda cc, rr: (0, rr, 0)), (w, *blk), (m, *blk), (v, *blk)],
               [(o, *blk, None)] * 4, (1, r // tr), name)


_SHARDED = ("w_in", "w_glu", "w_branch_attn", "w_branch_ssm", "w_out", "w_up", "w_down")
_COL_SHARDED = ("w_in", "w_glu", "w_branch_attn", "w_branch_ssm", "w_up")
_GROUPS = {"a": ("w_in",), "b": ("w_glu", "w_branch_attn", "w_branch_ssm", "w_out"), "c": ("w_up", "w_down")}
_SMALL = ("attn_norm_g", "b_in", "attn_sinks", "ssm_a_re", "ssm_a_im", "ssm_log_dt", "ssm_b_re", "ssm_b_im",
          "ssm_c_re", "ssm_c_im", "ssm_d", "b_glu", "ffn_norm_g", "conv_w", "conv_b", "final_norm_g")
_WEIGHTS = ("attn_norm_g", "w_in", "b_in", "attn_sinks", "ssm_a_re", "ssm_a_im", "ssm_log_dt", "ssm_b_re", "ssm_b_im",
            "ssm_c_re", "ssm_c_im", "ssm_d", "w_glu", "b_glu", "w_branch_attn", "w_branch_ssm", "w_out", "ffn_norm_g",
            "w_up", "conv_w", "conv_b", "w_down", "final_norm_g")


def _unstack_cols(g):
    return g.transpose(1, 0, 2).reshape(g.shape[1], g.shape[0] * g.shape[2])


def _stack_cols(a, d=N_DEV):
    k, n = a.shape
    return a.reshape(k, d, n // d).transpose(1, 0, 2)


def _pack(arrs):
    flat = jnp.concatenate([a.reshape(-1) for a in arrs])
    pad = (-flat.shape[0]) % 1024
    return jnp.pad(flat, (0, pad)).reshape(-1, 128)


def _local_step(x, tgt, wget, small, gput):
    L = x.shape[0]
    nr = lambda tm: L // tm

    h = _rmsnorm_fwd(x, small["attn_norm_g"], "norm1")
    wts = dict(wget("a", h))
    projb = _mm(h, wts["w_in"], bias=small["b_in_p"], out_dtype=BF16, name="proj")
    proj = projb
    attn_bias = _attn_bias()
    attn = _attn_fwd(projb, small["attn_sinks"], attn_bias, "attn_fwd")

    ab, bmat, cmat = _ssm_prep(small["a_re"], small["a_im"], small["logdt"], small["b_re"], small["b_im"],
                               small["c_re"], small["c_im"], "ssm_prep")
    u_seg = _to_segments(proj[:, C_U:C_PAD])
    ends_f = _ssm_scan(u_seg, bmat, ab, reverse=False, tk=128, name="ssm_ends_fwd")
    xs, init_f, y_seg = _ssm_scan(u_seg, bmat, ab, reverse=False, ends=ends_f, wproj=cmat, tk=64, name="ssm_scan_fwd")
    y_mm = _from_segments(y_seg)

    def gelu_fn(c, r, yb, ub, db):
        yv = yb + db * ub
        return yv, _gelu(yv)

    tm = 512
    y, gy = _ew(gelu_fn, [(y_mm, *_rc(tm, 256)), (proj, *_rc(tm, 256, C_U // 256)), (small["ssm_d"], *_col(1, 256))],
                [(SDS((L, SW), F32), *_rc(tm, 256), None), (SDS((L, SW), BF16), *_rc(tm, 256), None)],
                (2, nr(tm)), "ssm_gelu")
    wts.update(wget("b", gy))
    glu = _mm(gy, wts["w_glu"], bias=small["b_glu"], name="glu")

    def glu_fn(c, r, vb, gb):
        return (vb * _sigmoid(gb),)

    (ssm,) = _ew(glu_fn, [(glu, *_rc(tm, SW)), (glu, *_rc(tm, SW, 1))], [(SDS((L, SW), BF16), *_rc(tm, SW), None)],
                 (1, nr(tm)), "glu_gate")
    f32 = lambda ref, cols: ref[:, cols].astype(F32)
    tnm = 1024
    gate_tiles = [(projb, "tile", C_GA // tnm), (projb, "tile", C_GS // tnm)]

    def merge_ep(i, cols, ra, rs, ga, gs):
        return _sigmoid(f32(ga, cols)) * ra + _sigmoid(f32(gs, cols)) * rs, ra, rs

    merged, br_a, br_s = _mm(attn, wts["w_branch_attn"], a2=ssm, b2=wts["w_branch_ssm"], tm=512, tn=tnm,
                                  extras=gate_tiles, epilogue=merge_ep, outs=[(SDS((L, D), BF16), "tile")] * 3,
                                  name="branch_merge")
    x1 = _mm(merged, wts["w_out"], res=x, name="out_proj")
    h2 = _rmsnorm_fwd(x1, small["ffn_norm_g"], "norm2")
    wts.update(wget("c", h2))
    conv_w = wts["conv_w"]
    w_up_v, w_up_g = wts["w_up_v"], wts["w_up_g"]
    tcf = 1408
    tma = 256
    hb = 16

    def conv_gate(first, gate, halo, cw, cb):
        halo = halo * jnp.logical_not(first).astype(F32)
        g1, g2 = _shift_rows(gate, halo, 1), _shift_rows(gate, halo, 2)
        return cb + cw[2:3] * gate + cw[1:2] * g1 + cw[0:1] * g2, g1, g2

    tmu, tnu = 1024, 512

    def up_ep(i, cols, rv, rg, h2_halo, wg, cw, cb):
        halo = jnp.dot(h2_halo[...], wg[:, cols], preferred_element_type=F32)[hb - 8:]
        gl, glg = _gelu_and_grad(conv_gate(i == 0, rg, halo, cw[:, cols], cb[:, cols])[0])
        return rv, rg, rv * gl, gl, glg

    up_v, up_g, act, gelu_cg, gelu_grad_cg = _mm(
        h2, w_up_v, b2=w_up_g, tm=tmu, tn=tnu, epilogue=up_ep, outs=[(SDS((L, DFF), BF16), "tile")] * 5, name="ffn_up_act",
        extras=[(h2, "spec", ((hb, D), lambda j, i: (jnp.maximum(i * (tmu // hb) - 1, 0), 0))),
                (w_up_g, "spec", ((D, tnu), lambda j, i: (0, j))), (conv_w, "col", 0), (small["conv_b"], "col", 0)])
    x2 = _mm(act, wts["w_down"], res=x1, name="ffn_down")
    d_x2, d_x2b, loss_cols, d_gf = _final_loss(x2, small["final_norm_g"], tgt, "final_loss")
    loss = jnp.sum(loss_cols)

    dw_down = _mm(act, d_x2b, ta=True, out_dtype=BF16, tm=tcf, tk=2048, name="dw_down")
    tmd, tnd = 1024, 512

    def dact_ep(i, cols, da, _, val_ref, gate_ref, halo_ref, gl_ref, glg_ref):
        val, gate, gl = f32(val_ref, cols), f32(gate_ref, cols), f32(gl_ref, cols)
        halo = f32(halo_ref, cols)[hb - 8:] * (i > 0).astype(F32)
        g1, g2 = _shift_rows(gate, halo, 1), _shift_rows(gate, halo, 2)
        d_cg = da * val * f32(glg_ref, cols)
        row3 = lax.broadcasted_iota(jnp.int32, (3, da.shape[1]), 0)
        s0 = jnp.sum(d_cg * g2, axis=0, keepdims=True)
        s1 = jnp.sum(d_cg * g1, axis=0, keepdims=True)
        s2 = jnp.sum(d_cg * gate, axis=0, keepdims=True)
        dcw = jnp.where(row3 == 0, s0, jnp.where(row3 == 1, s1, s2))
        return da * gl, d_cg, dcw, jnp.sum(d_cg, axis=0, keepdims=True)

    d_val, d_cg, d_conv_w, d_conv_b = _mm(
        d_x2b, wts["w_down"], tb=True, tm=tmd, tn=tnd, epilogue=dact_ep, name="d_act_bwd",
        extras=[(up_v, "tile", 0), (up_g, "tile", 0),
                (up_g, "spec", ((hb, tnd), lambda j, i: (jnp.maximum(i * (tmd // hb) - 1, 0), j))),
                (gelu_cg, "tile", 0), (gelu_grad_cg, "tile", 0)],
        outs=[(SDS((L, DFF), BF16), "tile")] * 2 + [(SDS((3, DFF), F32), "colacc"), (SDS((1, DFF), F32), "colacc")])
    ncf = DFF // tcf

    def gate_bwd(c, r, dcg, halo, cw):
        halo = halo[:8] * (r < nr(tma) - 1).astype(F32)
        return (cw[2:3] * dcg + cw[1:2] * _shift_rows_up(dcg, halo, 1) + cw[0:1] * _shift_rows_up(dcg, halo, 2),)

    (d_gate,) = _ew(gate_bwd, [(d_cg, *_rc(tma, tcf)),
                               (d_cg, (hb, tcf), lambda c, r: (jnp.minimum((r + 1) * (tma // hb), L // hb - 1), c)),
                               (conv_w, *_col(3, tcf))],
                    [(SDS((L, DFF), BF16), *_rc(tma, tcf), None)], (ncf, nr(tma)), "ffn_gate_bwd")
    d_h2 = _mm(d_val, w_up_v, tb=True, name="d_h2_val")
    d_h2 = _mm(d_gate, w_up_g, tb=True, res=d_h2, name="d_h2_gate")
    assert tcf == 2 * DFF // N_DEV
    dw_up_v = _mm(h2, d_val, ta=True, out_dtype=BF16, tn=tcf, tk=2048, stack_out=True, name="dw_up_val")
    dw_up_g = _mm(h2, d_gate, ta=True, out_dtype=BF16, tn=tcf, tk=2048, stack_out=True, name="dw_up_gate")
    tok = gput("c", {"w_up_v": dw_up_v, "w_up_g": dw_up_g, "w_down": dw_down})
    d_x1, d_g2 = _rmsnorm_bwd(d_h2, x1, small["ffn_norm_g"] + tok[0, 0], d_x2, "norm2_bwd")

    dw_out = _mm(merged, d_x1, ta=True, out_dtype=BF16, name="dw_out")

    def dmerge_ep(i, cols, dm, _, a_ref, s_ref, ga, gs):
        sa, ss = _sigmoid(f32(ga, cols)), _sigmoid(f32(gs, cols))
        return dm * sa, dm * ss, dm * f32(a_ref, cols) * (sa * (1.0 - sa)), dm * f32(s_ref, cols) * (ss * (1.0 - ss))

    d_bra, d_brs, d_ga, d_gs = _mm(d_x1, wts["w_out"], tb=True, tm=512, tn=tnm, epilogue=dmerge_ep,
                                        extras=[(br_a, "tile", 0), (br_s, "tile", 0)] + gate_tiles,
                                        outs=[(SDS((L, D), BF16), "tile")] * 4, name="d_merged_bwd")
    d_attn = _mm(d_bra, wts["w_branch_attn"], tb=True, out_dtype=BF16, name="d_attn")
    dw_ba = _mm(attn, d_bra, ta=True, out_dtype=BF16, name="dw_branch_attn")
    d_ssm = _mm(d_brs, wts["w_branch_ssm"], tb=True, name="d_ssm")
    dw_bs = _mm(ssm, d_brs, ta=True, out_dtype=BF16, name="dw_branch_ssm")
    dq, dkv_cur, dkv_prev, d_sinks = _attn_bwd(projb, small["attn_sinks"], attn_bias, d_attn, "attn_bwd")

    def glu_bwd(c, r, ds, vb, gb):
        sg = _sigmoid(gb)
        return ds * sg, ds * vb * (sg * (1.0 - sg))

    d_glu_v, d_glu_g = _ew(glu_bwd, [(d_ssm, *_rc(tm, SW)), (glu, *_rc(tm, SW)), (glu, *_rc(tm, SW, 1))],
                           [(SDS((L, SW), F32), *_rc(tm, SW), None)] * 2, (1, nr(tm)), "glu_gate_bwd")
    d_glu = jnp.concatenate([d_glu_v, d_glu_g], axis=1)
    d_gy = _mm(d_glu, wts["w_glu"], tb=True, name="d_gelu_y")
    dw_glu = _mm(gy, d_glu, ta=True, out_dtype=BF16, name="dw_glu")

    tok = gput("b", {"w_glu": dw_glu, "w_branch_attn": dw_ba, "w_branch_ssm": dw_bs, "w_out": dw_out})
    ab = ab + tok[0, 0]

    def gelu_bwd(c, r, dg, yb, ub, dgl):
        dy = dg * _gelu_grad(yb)
        return dy, jnp.sum(dy * ub, axis=0, keepdims=True), jnp.sum(dgl, axis=0, keepdims=True)

    dy, d_ssm_d, d_b_glu = _ew(
        gelu_bwd, [(d_gy, *_rc(tm, 256)), (y, *_rc(tm, 256)), (proj, *_rc(tm, 256, C_U // 256)), (d_glu, *_rc(tm, 512))],
        [(SDS((L, SW), F32), *_rc(tm, 256), None), (SDS((1, SW), F32), *_col(1, 256), "r"),
         (SDS((1, 2 * SW), F32), *_col(1, 512), "r")], (2, nr(tm)), "ssm_gelu_bwd")

    dy_seg = _to_segments(dy)
    ends_r = _ssm_scan(dy_seg, cmat, ab, reverse=True, tk=128, name="ssm_ends_bwd")
    lam, dab8, du_seg = _ssm_scan(dy_seg, cmat, ab, reverse=True, ends=ends_r, xs=xs, init=init_f, wproj=bmat,
                                  name="ssm_scan_bwd")
    du_mm = _from_segments(du_seg)
    dbm = _mm(u_seg, lam, ta=True, tm=512, name="ssm_dbmat")
    dcm = _mm(dy_seg, xs, ta=True, tm=512, name="ssm_dcmat")
    d_are, d_aim, d_ldt, d_bre, d_bim, d_cre, d_cim = _ssm_param_bwd(
        small["a_re"], small["a_im"], small["logdt"], small["b_re"], small["b_im"], dab8, dbm, dcm, "ssm_param_bwd")

    nb = L // BLK

    def dproj_fn(c, r, dqb, cur, prv, du, dyb, dsk, dga, dgs):
        dkv = cur + prv * (r < nb - 1).astype(F32)
        dub = du + dsk * dyb
        full = jnp.concatenate([dqb, dkv, dub, jnp.zeros((BLK, C_GA - C_PAD), F32), dga, dgs], axis=1)
        return full, jnp.sum(full, axis=0, keepdims=True)

    rowb = lambda w: ((BLK, w), lambda c, r: (r, 0))
    dproj, d_b_in = _ew(
        dproj_fn, [(dq, *rowb(AW)), (dkv_cur, *rowb(256)),
                   (dkv_prev, (BLK, 256), lambda c, r: (jnp.minimum(r + 1, nb - 1), 0)),
                   (du_mm, *rowb(SW)), (dy, *rowb(SW)), (small["ssm_d"], *_col(1, SW)), (d_ga, *rowb(D)), (d_gs, *rowb(D))],
        [(SDS((L, INP), BF16), *rowb(INP), None), (SDS((1, INP), F32), *_col(1, INP), "all")], (1, nb), "dproj")
    dw_in = _mm(h, dproj, ta=True, out_dtype=BF16, name="dw_in")
    tok = gput("a", {"w_in": _unpad_cols(dw_in)})
    d_h = _mm(dproj, wts["w_in"], tb=True, bias=jnp.zeros((1, D), F32) + tok[0, 0], name="d_h")
    grad_x, d_g1 = _rmsnorm_bwd(d_h, x, small["attn_norm_g"], d_x1, "norm1_bwd")

    sgrads = {"attn_norm_g": d_g1, "b_in": _unpad_cols(d_b_in), "attn_sinks": d_sinks[:, :NQ], "a_re": d_are, "a_im": d_aim,
              "logdt": d_ldt, "b_re": d_bre, "b_im": d_bim, "c_re": d_cre, "c_im": d_cim, "ssm_d": d_ssm_d,
              "b_glu": d_b_glu, "ffn_norm_g": d_g2, "conv_w": d_conv_w, "conv_b": d_conv_b, "final_norm_g": d_gf}
    return loss, grad_x, sgrads


def _small_layouts(p):
    gp = lambda a: a.reshape(1, NS)
    hgp = lambda a: a.transpose(2, 0, 1).reshape(H, NS)
    chgp = lambda a: a.transpose(1, 0, 2).reshape(H, NS)
    return {
        "attn_norm_g": p["attn_norm_g"].reshape(1, D), "ffn_norm_g": p["ffn_norm_g"].reshape(1, D),
        "final_norm_g": p["final_norm_g"].reshape(1, D),
        "b_in_p": _pad_cols(p["b_in"].reshape(1, INC)),
        "attn_sinks": p["attn_sinks"].reshape(1, NQ),
        "a_re": gp(p["ssm_a_re"]), "a_im": gp(p["ssm_a_im"]), "logdt": jnp.repeat(p["ssm_log_dt"], P).reshape(1, NS),
        "b_re": hgp(p["ssm_b_re"]), "b_im": hgp(p["ssm_b_im"]), "c_re": chgp(p["ssm_c_re"]), "c_im": chgp(p["ssm_c_im"]),
        "ssm_d": p["ssm_d"].reshape(1, SW), "b_glu": p["b_glu"].reshape(1, 2 * SW),
        "conv_b": p["conv_b"].reshape(1, DFF),
    }


def _small_grads_to_param_shapes(sg):
    from_hgp = lambda a: a.reshape(H, G, P).transpose(1, 2, 0)
    from_chgp = lambda a: a.reshape(H, G, P).transpose(1, 0, 2)
    return {
        "attn_norm_g": sg["attn_norm_g"].reshape(D), "b_in": sg["b_in"].reshape(INC),
        "attn_sinks": sg["attn_sinks"].reshape(NQ),
        "ssm_a_re": sg["a_re"].reshape(G, P), "ssm_a_im": sg["a_im"].reshape(G, P),
        "ssm_log_dt": jnp.sum(sg["logdt"].reshape(G, P), axis=1),
        "ssm_b_re": from_hgp(sg["b_re"]), "ssm_b_im": from_hgp(sg["b_im"]),
        "ssm_c_re": from_chgp(sg["c_re"]), "ssm_c_im": from_chgp(sg["c_im"]),
        "ssm_d": sg["ssm_d"].reshape(SW), "b_glu": sg["b_glu"].reshape(2 * SW),
        "ffn_norm_g": sg["ffn_norm_g"].reshape(D), "conv_w": sg["conv_w"], "conv_b": sg["conv_b"].reshape(DFF),
        "final_norm_g": sg["final_norm_g"].reshape(D),
    }


def kernel(x, attn_norm_g, w_in, b_in, attn_sinks, ssm_a_re, ssm_a_im, ssm_log_dt, ssm_b_re, ssm_b_im, ssm_c_re, ssm_c_im, ssm_d, w_glu, b_glu, w_branch_attn, w_branch_ssm, w_out, ffn_norm_g, w_up, conv_w, conv_b, w_down, final_norm_g, loss_target, m_attn_norm_g, m_w_in, m_b_in, m_attn_sinks, m_ssm_a_re, m_ssm_a_im, m_ssm_log_dt, m_ssm_b_re, m_ssm_b_im, m_ssm_c_re, m_ssm_c_im, m_ssm_d, m_w_glu, m_b_glu, m_w_branch_attn, m_w_branch_ssm, m_w_out, m_ffn_norm_g, m_w_up, m_conv_w, m_conv_b, m_w_down, m_final_norm_g, v_attn_norm_g, v_w_in, v_b_in, v_attn_sinks, v_ssm_a_re, v_ssm_a_im, v_ssm_log_dt, v_ssm_b_re, v_ssm_b_im, v_ssm_c_re, v_ssm_c_im, v_ssm_d, v_w_glu, v_b_glu, v_w_branch_attn, v_w_branch_ssm, v_w_out, v_ffn_norm_g, v_w_up, v_conv_w, v_conv_b, v_w_down, v_final_norm_g):
    args = dict(locals())
    sq = lambda a: a if a.ndim == 1 else a[0]
    wv = {n: sq(args[n]) for n in _WEIGHTS}
    mv = {n: sq(args["m_" + n]) for n in _WEIGHTS}
    vv = {n: sq(args["v_" + n]) for n in _WEIGHTS}
    me = 4 * lax.axis_index("x") + 2 * lax.axis_index("y") + lax.axis_index("c")

    gather, tok = {}, jnp.zeros((8, 128), F32)
    for grp in ("a", "b", "c"):
        shards = [(wv[n] + tok[0, 0]).astype(BF16) for n in _GROUPS[grp]]
        if grp == "c":
            shards.append(jnp.pad(wv["conv_w"] + tok[0, 0], ((0, 5), (0, 64))))
        gather[grp], tok = _exchange_start(shards, False, "gather_start_" + grp,
                                           masks=_FIRST_HOP_MASKS if grp == "a" else _ALL_MASKS)
    small = _small_layouts(wv)
    small["attn_norm_g"] = small["attn_norm_g"] + tok[0, 0]

    def own_slot(land, src):
        return lax.dynamic_update_slice_in_dim(land, src, me, axis=0)

    def wget(grp, after):
        if grp == "a":
            thru, lands = _exchange_wait(gather[grp], after, False, "gather_wait_a", masks=_FIRST_HOP_MASKS)
            fwd, fwd_tok = _forward_start(lands, "gather_forward_start_a")
            lands = _forward_wait(fwd, fwd_tok, "gather_forward_wait_a")
        else:
            thru, lands = _exchange_wait(gather[grp], after, False, "gather_wait_" + grp)
        full = {}
        for n, t, g in zip(_GROUPS[grp], thru, lands):
            g = own_slot(g, t[None])
            full[n] = _unstack_cols(g) if n in _COL_SHARDED else g.reshape(N_DEV * g.shape[1], g.shape[2])
        if grp == "a":
            full["w_in"] = _pad_cols(full["w_in"])
        if grp == "c":
            full["conv_w"] = _unstack_cols(own_slot(lands[-1], thru[-1][None])[:, :3, :DFF // N_DEV])
            g = own_slot(lands[0], thru[0][None])
            full["w_up_v"], full["w_up_g"] = _unstack_cols(g[:N_DEV // 2]), _unstack_cols(g[N_DEV // 2:])
            del full["w_up"]
        return full

    scatter = {}

    def gput(grp, grads):
        stacked = [_stack_cols(grads[n]) if n in _COL_SHARDED else grads[n].reshape(N_DEV, -1, D)
                   for n in _GROUPS[grp] if n != "w_up"]
        if grp == "c":
            half = N_DEV // 2
            stacked.insert(0, jnp.concatenate([grads["w_up_v"], grads["w_up_g"]]))
        scatter[grp], token = _exchange_start(stacked, True, "scatter_start_" + grp)
        return token

    loss, grad_x, sg = _local_step(x[0], loss_target[0], wget, small, gput)
    loss = lax.psum(loss, MESH_AXES)

    sgp = _small_grads_to_param_shapes(sg)
    small_names = [n for n in _SMALL]
    packed_g = _pack([sgp[n] for n in small_names])
    (small_all,) = _exchange([packed_g], False, "gather_small_grads")

    outs_g, outs_d, outs_m, outs_v = {}, {}, {}, {}
    for grp in ("c", "b", "a"):
        thru, lands = _exchange_wait(scatter[grp], small_all, True, "scatter_wait_" + grp)
        for n, t, pt in zip(_GROUPS[grp], thru, lands):
            pt = own_slot(pt, lax.dynamic_slice_in_dim(t, me, 1, axis=0))
            outs_g[n], outs_d[n], outs_m[n], outs_v[n] = _adam(pt, wv[n], mv[n], vv[n], "adam_" + n)

    sizes = [int(math.prod(sgp[n].shape)) for n in small_names]
    offs = [0]
    for s in sizes:
        offs.append(offs[-1] + s)

    def local_part(n, a):
        if n == "conv_w":
            return lax.dynamic_slice(a, (0, me * (DFF // N_DEV)), (3, DFF // N_DEV))
        return a

    rows = packed_g.shape[0]

    def sum_fn(cc, rr, pb):
        g = pb[0]
        for d in range(1, N_DEV):
            g = g + pb[d]
        return (g,)

    (gsum,) = _ew(sum_fn, [(small_all, (N_DEV, rows, 128), lambda cc, rr: (0, 0, 0))],
                  [(SDS((rows, 128), F32), (rows, 128), lambda cc, rr: (0, 0), None)], (1, 1), "sum_small_grads")
    gflat = gsum.reshape(-1)
    gsmall = {n: local_part(n, gflat[offs[i]:offs[i + 1]].reshape(sgp[n].shape)) for i, n in enumerate(small_names)}
    as2d = lambda a: a.reshape(1, -1) if a.ndim == 1 else a.reshape(a.shape[0], -1)
    n_small = len(small_names)

    def adam_small(*refs):
        for i in range(n_small):
            g_ref, w_ref, m_ref, v_ref = refs[4 * i:4 * i + 4]
            outs = refs[4 * n_small + 3 * i:4 * n_small + 3 * i + 3]
            for o_ref, val in zip(outs, _adam_math(g_ref[...], w_ref[...], m_ref[...], v_ref[...])):
                o_ref[...] = val

    small_ins = [as2d(t[n]) for n in small_names for t in (gsmall, wv, mv, vv)]
    small_outs = pl.pallas_call(adam_small, name="adam_small",
                                out_shape=[SDS(as2d(wv[n]).shape, F32) for n in small_names for _ in range(3)])(*small_ins)
    for i, n in enumerate(small_names):
        sd, sm, sv = (t.reshape(wv[n].shape) for t in small_outs[3 * i:3 * i + 3])
        outs_g[n], outs_d[n], outs_m[n], outs_v[n] = gsmall[n], sd, sm, sv

    lead = lambda n, a: a if args[n].ndim == 1 else a[None]
    grad_x = grad_x[None]
    return (loss, grad_x, *[lead(n, outs_g[n]) for n in _WEIGHTS], *[lead(n, outs_d[n]) for n in _WEIGHTS],
            *[lead(n, outs_m[n]) for n in _WEIGHTS], *[lead(n, outs_v[n]) for n in _WEIGHTS])
```

```python
import functools
import math

import jax
import jax.numpy as jnp
from jax import lax
from jax.experimental import pallas as pl
from jax.experimental.pallas import tpu as pltpu

F32 = jnp.float32
BF16 = jnp.bfloat16
SDS = jax.ShapeDtypeStruct

N_DEV = 8
D = 2048
NQ, NKV, HD = 16, 2, 64
AW = NQ * HD
BLK = 128
SW, G, H, P = 512, 32, 16, 64
NS = G * P
DFF = 5632
INC = AW + 2 * NKV * HD + SW + 2 * D
C_K, C_U, C_PAD = AW, AW + 2 * NKV * HD, AW + 2 * NKV * HD + SW
C_GA, C_GS, INP = D, 2 * D, 3 * D
RMS_EPS = 1e-6
NEG_BIG = -1e30
ADAM_LR, ADAM_B1, ADAM_B2, ADAM_EPS, ADAM_WD, ADAM_STEP = 0.001, 0.9, 0.999, 1e-08, 0.01, 10
NSEG = 8
VMEM_CAP_MB = 60
MESH_AXES = ("x", "y", "c")


def _pad_cols(a):
    zeros = jnp.zeros(a.shape[:-1] + (C_GA - C_PAD,), a.dtype)
    return jnp.concatenate([a[..., :C_PAD], zeros, a[..., C_PAD:]], axis=-1)


def _unpad_cols(a):
    return jnp.concatenate([a[..., :C_PAD], a[..., C_GA:]], axis=-1)


def _cparams(sem, vmem_mb):
    return pltpu.CompilerParams(dimension_semantics=sem, vmem_limit_bytes=min(int(vmem_mb), VMEM_CAP_MB) << 20)


LANES = 128


def _tile(dim, pref):
    if dim <= pref:
        return dim
    for t in range(pref - pref % LANES, 0, -LANES):
        if dim % t == 0:
            return t
    raise ValueError(f"no tile for {dim}")


def _mm(a, b, *, ta=False, tb=False, bias=None, res=None, out_dtype=F32, tm=1024, tn=1024, tk=3072, name,
        a2=None, b2=None, extras=(), epilogue=None, outs=None, ep_cols=None, stack_out=False):
    m, k = (a.shape[1], a.shape[0]) if ta else a.shape
    n = b.shape[0] if tb else b.shape[1]
    assert (b.shape[1] if tb else b.shape[0]) == k, (a.shape, b.shape, ta, tb)
    tm, tn, tk = _tile(m, tm), _tile(n, tn), _tile(k, tk)
    nk = k // tk
    dims = (((0 if ta else 1,), (1 if tb else 0,)), ((), ()))
    has_bias, has_res, has_b2 = bias is not None, res is not None, b2 is not None
    has_a2 = a2 is not None
    assert not (has_b2 and (nk > 1 or ta or tb)) and not (has_a2 and not has_b2)
    if epilogue is None:
        outs = [(SDS((n // tn, m, tn) if stack_out else (m, n), out_dtype), "tile")]
    n_ex, n_out = len(extras), len(outs)
    tcn = tn if (epilogue is None or nk > 1 or ep_cols is None) else _tile(tn, ep_cols)

    def body(*refs):
        a_ref, b_ref = refs[0], refs[1]
        pos = 2
        a2_ref = refs[pos] if has_a2 else a_ref
        pos += has_a2
        b2_ref = refs[pos] if has_b2 else None
        pos += has_b2
        bias_ref = refs[pos] if has_bias else None
        pos += has_bias
        res_ref = refs[pos] if has_res else None
        pos += has_res
        ex_refs = refs[pos:pos + n_ex]
        o_refs = refs[pos + n_ex:pos + n_ex + n_out]
        i = pl.program_id(1)

        def product(rhs_ref, cols=None, lhs=None):
            rhs = rhs_ref[...] if cols is None else (rhs_ref[cols, :] if tb else rhs_ref[:, cols])
            lhs = a_ref[...].astype(BF16) if lhs is None else lhs
            return lax.dot_general(lhs, rhs.astype(BF16), dims, preferred_element_type=F32)

        def finish(r, cols):
            if has_bias:
                r = r + bias_ref[:, cols]
            if has_res:
                r = r + res_ref[:, cols].astype(F32)
            if epilogue is None:
                o_refs[0][:, cols] = r.astype(o_refs[0].dtype)
                return
            r2 = None
            if has_b2:
                r2 = jnp.dot(a2_ref[...].astype(BF16), b2_ref[:, cols].astype(BF16), preferred_element_type=F32)
            vals = epilogue(i, cols, r, r2, *ex_refs)
            for o_ref, v, (_, kind) in zip(o_refs, vals, outs):
                if kind == "tile":
                    o_ref[:, cols] = v.astype(o_ref.dtype)
                else:
                    @pl.when(i == 0)
                    def _(o_ref=o_ref, v=v):
                        o_ref[:, cols] = v.astype(o_ref.dtype)

                    @pl.when(i > 0)
                    def _(o_ref=o_ref, v=v):
                        o_ref[:, cols] += v.astype(o_ref.dtype)

        if nk == 1:
            lhs = a_ref[...].astype(BF16)
            for c0 in range(0, tn, tcn):
                cols = pl.ds(c0, tcn)
                finish(product(b_ref, cols, lhs), cols)
            return
        whole = pl.ds(0, tn)
        acc_ref = refs[-1]
        kk = pl.program_id(2)

        @pl.when(kk == 0)
        def _():
            acc_ref[...] = product(b_ref)

        @pl.when(jnp.logical_and(kk > 0, kk < nk - 1))
        def _():
            acc_ref[...] += product(b_ref)

        @pl.when(kk == nk - 1)
        def _():
            finish(acc_ref[...] + product(b_ref), whole)

    b_spec = pl.BlockSpec((tn, tk), lambda j, i, kk: (j, kk)) if tb else pl.BlockSpec((tk, tn), lambda j, i, kk: (kk, j))
    ins = [a, b]
    in_specs = [pl.BlockSpec((tk, tm), lambda j, i, kk: (kk, i)) if ta else pl.BlockSpec((tm, tk), lambda j, i, kk: (i, kk)),
                b_spec]
    tile_spec = pl.BlockSpec((tm, tn), lambda j, i, kk: (i, j))
    byt = 2 * tm * tk * a.dtype.itemsize + 2 * tk * tn * b.dtype.itemsize
    byt += (2 + has_b2) * 4 * tm * tn
    if has_a2:
        ins.append(a2)
        in_specs.append(pl.BlockSpec((tm, a2.shape[1]), lambda j, i, kk: (i, 0)))
        byt += 2 * tm * a2.shape[1] * a2.dtype.itemsize
    if has_b2:
        ins.append(b2)
        in_specs.append(pl.BlockSpec((b2.shape[0], tn), lambda j, i, kk: (0, j)))
        byt += 2 * b2.shape[0] * tn * b2.dtype.itemsize
    if has_bias:
        ins.append(bias)
        in_specs.append(pl.BlockSpec((1, tn), lambda j, i, kk: (0, j)))
    if has_res:
        ins.append(res)
        in_specs.append(tile_spec)
        byt += 2 * tm * tn * res.dtype.itemsize
    for arr, kind, arg in extras:
        ins.append(arr)
        if kind == "tile":
            in_specs.append(pl.BlockSpec((tm, tn), lambda j, i, kk, arg=arg: (i, j + arg)))
            byt += 2 * tm * tn * arr.dtype.itemsize + 4 * tm * tn
        elif kind == "col":
            in_specs.append(pl.BlockSpec((arr.shape[0], tn), lambda j, i, kk, arg=arg: (0, j + arg)))
        else:
            in_specs.append(pl.BlockSpec(arg[0], lambda j, i, kk, im=arg[1]: im(j, i)))
    out_specs = []
    for sds, kind in outs:
        if kind == "tile":
            out_specs.append(pl.BlockSpec((None, tm, tn), lambda j, i, kk: (j, i, 0)) if stack_out else tile_spec)
            byt += 2 * tm * tn * jnp.dtype(sds.dtype).itemsize
        else:
            out_specs.append(pl.BlockSpec((sds.shape[0], tn), lambda j, i, kk: (0, j)))
    res_ = pl.pallas_call(
        body, out_shape=tuple(o[0] for o in outs), grid=(n // tn, m // tm, nk), in_specs=in_specs,
        out_specs=tuple(out_specs), scratch_shapes=[pltpu.VMEM((tm, tn), F32)] if nk > 1 else [], name=name,
        compiler_params=_cparams(("arbitrary", "arbitrary", "arbitrary"), byt / 2**20 + (8 if epilogue is None else 20)),
    )(*ins)
    return res_[0] if epilogue is None else res_


def _ew(fn, ins, outs, grid, name, vmem_mb=40):
    n_in = len(ins)
    accs = [o[3] for o in outs]

    def body(*refs):
        c, r = pl.program_id(0), pl.program_id(1)
        vals = fn(c, r, *[ref[...].astype(F32) for ref in refs[:n_in]])
        for o_ref, v, acc in zip(refs[n_in:], vals, accs):
            if acc is None:
                o_ref[...] = v.astype(o_ref.dtype)
            else:
                first = (r == 0) if acc == "r" else jnp.logical_and(r == 0, c == 0)

                @pl.when(first)
                def _(o_ref=o_ref, v=v):
                    o_ref[...] = v.astype(o_ref.dtype)

                @pl.when(jnp.logical_not(first))
                def _(o_ref=o_ref, v=v):
                    o_ref[...] += v.astype(o_ref.dtype)

    res = pl.pallas_call(
        body, out_shape=tuple(o[0] for o in outs), grid=grid,
        in_specs=[pl.BlockSpec(bs, im) for _, bs, im in ins],
        out_specs=tuple(pl.BlockSpec(bs, im) for _, bs, im, _ in outs), name=name,
        compiler_params=_cparams(("arbitrary", "arbitrary"), vmem_mb),
    )(*[a for a, _, _ in ins])
    return res


def _rc(tm, tc, coff=0):
    return (tm, tc), (lambda c, r: (r, c + coff))


def _col(rows, tc, coff=0):
    return (rows, tc), (lambda c, r: (0, c + coff))


def _gelu(x):
    return 0.5 * x * (1.0 + lax.erf(x * (2.0 ** -0.5)))


def _gelu_and_grad(x):
    cdf = 0.5 * (1.0 + lax.erf(x * (2.0 ** -0.5)))
    return x * cdf, cdf + x * jnp.exp(-0.5 * x * x) * (1.0 / math.sqrt(2.0 * math.pi))


def _gelu_grad(x):
    return _gelu_and_grad(x)[1]


def _sigmoid(x):
    return 1.0 / (1.0 + jnp.exp(-x))


def _shift_rows(x, halo, s):
    rolled = pltpu.roll(x, s, 0)
    row8 = lax.broadcasted_iota(jnp.int32, halo.shape, 0)
    head = jnp.where(row8 < s, pltpu.roll(halo, s, 0), rolled[0:8])
    return jnp.concatenate([head, rolled[8:]], axis=0)


def _shift_rows_up(x, halo, s):
    tm = x.shape[0]
    rolled = pltpu.roll(x, tm - s, 0)
    row8 = lax.broadcasted_iota(jnp.int32, halo.shape, 0)
    tail = jnp.where(row8 >= 8 - s, pltpu.roll(halo, 8 - s, 0), rolled[tm - 8:])
    return jnp.concatenate([rolled[:tm - 8], tail], axis=0)


def _rmsnorm_fwd(x, g, name, tm=256):
    L = x.shape[0]

    def fn(c, r, xb, gb):
        rstd = lax.rsqrt(jnp.mean(xb * xb, axis=-1, keepdims=True) + RMS_EPS)
        return ((xb * rstd) * gb,)

    return _ew(fn, [(x, *_rc(tm, D)), (g, *_col(1, D))], [(SDS((L, D), BF16), *_rc(tm, D), None)], (1, L // tm), name)[0]


def _rmsnorm_bwd(dh, x, g, dres, name, tm=256):
    L = x.shape[0]

    def fn(c, r, dhb, xb, gb, drb):
        rstd = lax.rsqrt(jnp.mean(xb * xb, axis=-1, keepdims=True) + RMS_EPS)
        y = xb * rstd
        dy = dhb * gb
        dx = rstd * (dy - y * jnp.mean(dy * y, axis=-1, keepdims=True))
        return drb + dx, jnp.sum(dhb * y, axis=0, keepdims=True)

    return _ew(fn, [(dh, *_rc(tm, D)), (x, *_rc(tm, D)), (g, *_col(1, D)), (dres, *_rc(tm, D))],
               [(SDS((L, D), F32), *_rc(tm, D), None), (SDS((1, D), F32), *_col(1, D), "all")], (1, L // tm), name)


def _final_loss(x2, g, tgt, name, tm=256):
    L = x2.shape[0]

    def fn(c, r, xb, gb, tb):
        rstd = lax.rsqrt(jnp.mean(xb * xb, axis=-1, keepdims=True) + RMS_EPS)
        y = xb * rstd
        err = y * gb - tb
        dout = err * (1.0 / D)
        dy = dout * gb
        dx = rstd * (dy - y * jnp.mean(dy * y, axis=-1, keepdims=True))
        return dx, dx, jnp.sum(err * err, axis=0, keepdims=True) * (0.5 / D), jnp.sum(dout * y, axis=0, keepdims=True)

    return _ew(fn, [(x2, *_rc(tm, D)), (g, *_col(1, D)), (tgt, *_rc(tm, D))],
               [(SDS((L, D), F32), *_rc(tm, D), None), (SDS((L, D), BF16), *_rc(tm, D), None),
                (SDS((1, D), F32), *_col(1, D), "all"),
                (SDS((1, D), F32), *_col(1, D), "all")], (1, L // tm), name)


def _slope(h):
    return 2.0 ** (-8.0 * (h + 1) / NQ)


def _attn_bias():
    qi = lax.broadcasted_iota(jnp.int32, (BLK, 2 * BLK), 0)
    si = lax.broadcasted_iota(jnp.int32, (BLK, 2 * BLK), 1)
    dist = qi + BLK - si
    band = (dist >= 0) & (dist < BLK)
    slopes = jnp.asarray([_slope(h) for h in range(NQ)], F32)[:, None, None]
    alibi = -slopes * dist.astype(F32)[None]
    return jnp.stack([jnp.where((band & (si >= BLK))[None], alibi, NEG_BIG), jnp.where(band[None], alibi, NEG_BIG)])


def _attn_kv(kvc, kvp):
    kv = jnp.concatenate([kvp, kvc], axis=0).astype(F32)
    lo = lax.broadcasted_iota(jnp.int32, (2 * BLK, 128), 1) < HD

    def halves(t):
        tr = pltpu.roll(t, HD, 1)
        z = jnp.zeros_like(t)
        return {(0, 0): jnp.where(lo, t, z).astype(BF16), (0, 1): jnp.where(lo, z, tr).astype(BF16),
                (1, 0): jnp.where(lo, tr, z).astype(BF16), (1, 1): jnp.where(lo, z, t).astype(BF16)}

    return halves(kv[:, :128]), halves(kv[:, 128:])


_NT = (((1,), (1,)), ((), ()))
_TN = (((0,), (0,)), ((), ()))
_ATTN_SPECS = [pl.BlockSpec(memory_space=pltpu.SMEM),
               pl.BlockSpec((None, NQ, BLK, 2 * BLK), lambda n: (jnp.minimum(n, 1), 0, 0, 0)),
               pl.BlockSpec((BLK, AW), lambda n: (n, 0)),
               pl.BlockSpec((BLK, 256), lambda n: (n, C_K // 256)),
               pl.BlockSpec((BLK, 256), lambda n: (jnp.maximum(n - 1, 0), C_K // 256))]


def _attn_scores(q_ref, bias_ref, kmat, sc_ref):
    for j in range(NQ // 2):
        qs = q_ref[:, 128 * j:128 * (j + 1)] * (HD ** -0.5)
        for e in range(2):
            h = 2 * j + e
            sc_ref[h] = lax.dot_general(qs, kmat[(j // (NQ // 4), e)], _NT, preferred_element_type=F32) + bias_ref[h]


def _softmax_with_sink(s, sink):
    m = jnp.maximum(jnp.max(s, axis=-1, keepdims=True), sink)
    p = jnp.exp(s - m)
    esink = jnp.exp(sink - m)
    den = jnp.sum(p, axis=-1, keepdims=True) + esink
    return p / den, esink / den


def _attn_fwd(projb, sinks, bias, name):
    L = projb.shape[0]

    def body(s_ref, bias_ref, q_ref, kvc_ref, kvp_ref, o_ref, sc_ref, pr_ref):
        kmat, vmat = _attn_kv(kvc_ref[...], kvp_ref[...])
        _attn_scores(q_ref, bias_ref, kmat, sc_ref)
        for h in range(NQ):
            pr_ref[h] = _softmax_with_sink(sc_ref[h], s_ref[0, h])[0].astype(BF16)
        for j in range(NQ // 2):
            g = j // (NQ // 4)
            acc = jnp.dot(pr_ref[2 * j], vmat[(g, 0)], preferred_element_type=F32)
            acc = acc + jnp.dot(pr_ref[2 * j + 1], vmat[(g, 1)], preferred_element_type=F32)
            o_ref[:, 128 * j:128 * (j + 1)] = acc.astype(BF16)

    return pl.pallas_call(
        body, out_shape=SDS((L, AW), BF16), grid=(L // BLK,), in_specs=_ATTN_SPECS,
        out_specs=pl.BlockSpec((BLK, AW), lambda n: (n, 0)), name=name,
        scratch_shapes=[pltpu.VMEM((NQ, BLK, 2 * BLK), F32), pltpu.VMEM((NQ, BLK, 2 * BLK), BF16)],
        compiler_params=_cparams(("arbitrary",), 32),
    )(sinks, bias, projb, projb, projb)


def _attn_bwd(projb, sinks, bias, dattn, name):
    L = projb.shape[0]

    def body(s_ref, bias_ref, q_ref, kvc_ref, kvp_ref, do_ref, dq_ref, dcur_ref, dprev_ref, dsink_ref,
             sc_ref, dp_ref, ds_ref, pr_ref):
        n = pl.program_id(0)
        kmat, vmat = _attn_kv(kvc_ref[...], kvp_ref[...])
        _attn_scores(q_ref, bias_ref, kmat, sc_ref)
        for h in range(NQ):
            j, e = h // 2, h % 2
            dp_ref[h] = lax.dot_general(do_ref[:, 128 * j:128 * (j + 1)], vmat[(j // (NQ // 4), e)], _NT,
                                        preferred_element_type=F32)
        lane = lax.broadcasted_iota(jnp.int32, (1, 128), 1)
        dsv = jnp.zeros((1, 128), F32)
        for h in range(NQ):
            p, psink = _softmax_with_sink(sc_ref[h], s_ref[0, h])
            dp = dp_ref[h]
            drow = jnp.sum(p * dp, axis=-1, keepdims=True)
            ds_ref[h] = (p * (dp - drow)).astype(BF16)
            pr_ref[h] = p.astype(BF16)
            dsv = dsv + jnp.where(lane == h, -jnp.sum(psink * drow, axis=0, keepdims=True), 0.0)
        lo128 = lax.broadcasted_iota(jnp.int32, (BLK, 128), 1) < HD
        dk = [jnp.zeros((2 * BLK, 128), F32) for _ in range(NKV)]
        dv = [jnp.zeros((2 * BLK, 128), F32) for _ in range(NKV)]
        for j in range(NQ // 2):
            g = j // (NQ // 4)
            qs = q_ref[:, 128 * j:128 * (j + 1)] * (HD ** -0.5)
            dop = do_ref[:, 128 * j:128 * (j + 1)]
            zb = jnp.zeros_like(qs)
            dqp = jnp.zeros((BLK, 128), F32)
            for e in range(2):
                h = 2 * j + e
                half = lo128 if e == 0 else jnp.logical_not(lo128)
                dqp = dqp + jnp.dot(ds_ref[h], kmat[(g, e)], preferred_element_type=F32)
                dk[g] = dk[g] + lax.dot_general(ds_ref[h], jnp.where(half, qs, zb), _TN, preferred_element_type=F32)
                dv[g] = dv[g] + lax.dot_general(pr_ref[h], jnp.where(half, dop, zb), _TN, preferred_element_type=F32)
            dq_ref[:, 128 * j:128 * (j + 1)] = (dqp * (HD ** -0.5)).astype(BF16)
        lo256 = lax.broadcasted_iota(jnp.int32, (2 * BLK, 128), 1) < HD
        tot = [t + pltpu.roll(t, HD, 1) for t in (dk[0], dk[1], dv[0], dv[1])]
        dkv = jnp.concatenate([jnp.where(lo256, tot[0], tot[1]), jnp.where(lo256, tot[2], tot[3])], axis=1)
        dprev_ref[...] = dkv[:BLK]
        dcur_ref[...] = dkv[BLK:]

        @pl.when(n == 0)
        def _():
            dsink_ref[...] = dsv

        @pl.when(n > 0)
        def _():
            dsink_ref[...] += dsv

    tile = (NQ, BLK, 2 * BLK)
    return pl.pallas_call(
        body, out_shape=(SDS((L, AW), BF16), SDS((L, 256), F32), SDS((L, 256), F32), SDS((1, 128), F32)), grid=(L // BLK,),
        in_specs=_ATTN_SPECS + [pl.BlockSpec((BLK, AW), lambda n: (n, 0))],
        out_specs=(pl.BlockSpec((BLK, AW), lambda n: (n, 0)), pl.BlockSpec((BLK, 256), lambda n: (n, 0)),
                   pl.BlockSpec((BLK, 256), lambda n: (n, 0)), pl.BlockSpec((1, 128), lambda n: (0, 0))),
        scratch_shapes=[pltpu.VMEM(tile, F32), pltpu.VMEM(tile, F32), pltpu.VMEM(tile, BF16), pltpu.VMEM(tile, BF16)],
        name=name, compiler_params=_cparams(("arbitrary",), 40),
    )(sinks, bias, projb, projb, projb, dattn)


def _disc(a_re, a_im, logdt, b_re, b_im):
    dt = jnp.exp(logdt)
    mag = jnp.exp(a_re * dt)
    ab_re = mag * jnp.cos(a_im * dt)
    ab_im = mag * jnp.sin(a_im * dt)
    nr = ab_re - 1.0
    ni = ab_im
    den = a_re * a_re + a_im * a_im
    z_re = (nr * a_re + ni * a_im) / den
    z_im = (ni * a_re - nr * a_im) / den
    return ab_re, ab_im, z_re * b_re - z_im * b_im, z_re * b_im + z_im * b_re


def _group_mask():
    row = lax.broadcasted_iota(jnp.int32, (SW, NS), 0) // H
    col = lax.broadcasted_iota(jnp.int32, (SW, NS), 1) // P
    return row == col


def _block_diag(re, im):
    mask = _group_mask()
    z = jnp.zeros((SW, NS), F32)
    return jnp.concatenate([jnp.where(mask, jnp.tile(re, (G, 1)), z), jnp.where(mask, jnp.tile(im, (G, 1)), z)], axis=1)


def _block_diag_t(big):
    mask = _group_mask()
    z = jnp.zeros((SW, NS), F32)
    re = jnp.sum(jnp.where(mask, big[:, :NS], z).reshape(G, H, NS), axis=0)
    im = jnp.sum(jnp.where(mask, big[:, NS:], z).reshape(G, H, NS), axis=0)
    return re, im


def _ssm_prep(a_re, a_im, logdt, b_re, b_im, c_re, c_im, name):
    def body(are, aim, ldt, bre, bim, cre, cim, ab_ref, bm_ref, cm_ref):
        ab_re, ab_im, bb_re, bb_im = _disc(are[...], aim[...], ldt[...], bre[...], bim[...])
        ab_ref[...] = jnp.concatenate([ab_re, ab_im], axis=1)
        bm_ref[...] = _block_diag(bb_re, bb_im).astype(BF16)
        cm_ref[...] = _block_diag(cre[...], -cim[...]).astype(BF16)

    return pl.pallas_call(body, out_shape=(SDS((1, 2 * NS), F32), SDS((SW, 2 * NS), BF16), SDS((SW, 2 * NS), BF16)),
                          name=name, compiler_params=pltpu.CompilerParams(vmem_limit_bytes=48 << 20),
                          )(a_re, a_im, logdt, b_re, b_im, c_re, c_im)


def _ssm_param_bwd(a_re, a_im, logdt, b_re, b_im, dab8, dbm, dcm, name):
    def body(are, aim, ldt, bre, bim, dab_ref, dbm_ref, dcm_ref, o_are, o_aim, o_ldt, o_bre, o_bim, o_cre, o_cim):
        dab = jnp.sum(dab_ref[...], axis=0, keepdims=True)
        dbb_re, dbb_im = _block_diag_t(dbm_ref[...])
        _, vjp = jax.vjp(_disc, are[...], aim[...], ldt[...], bre[...], bim[...])
        d_are, d_aim, d_ldt, d_bre, d_bim = vjp((dab[:, :NS], dab[:, NS:], dbb_re, dbb_im))
        o_are[...], o_aim[...], o_ldt[...], o_bre[...], o_bim[...] = d_are, d_aim, d_ldt, d_bre, d_bim
        dc_re, dc_imn = _block_diag_t(dcm_ref[...])
        o_cre[...] = dc_re
        o_cim[...] = -dc_imn

    v1, vh = SDS((1, NS), F32), SDS((H, NS), F32)
    return pl.pallas_call(body, out_shape=(v1, v1, v1, vh, vh, vh, vh), name=name,
                          compiler_params=pltpu.CompilerParams(vmem_limit_bytes=56 << 20),
                          )(a_re, a_im, logdt, b_re, b_im, dab8, dbm, dcm)


def _ssm_scan(src, wmat, ab, *, reverse, ends=None, xs=None, init=None, wproj=None, name, tk=32):
    L = src.shape[0]
    rows = NSEG * tk
    nch = L // rows
    seg_len = L // NSEG
    n_sq = int(math.log2(seg_len))
    assert 2 ** n_sq == seg_len and L % rows == 0
    first_pass = ends is None
    with_dab = (not first_pass) and reverse
    with_proj = wproj is not None
    assert not (with_proj and first_pass)
    slab = 512
    n_slab = NS // slab

    def body(*refs):
        src_ref, w_ref, ab_ref = refs[:3]
        pos = 3
        if not first_pass:
            ends_ref = refs[pos]
            pos += 1
        if with_dab:
            xs_ref, xsh_ref, init_ref = refs[pos:pos + 3]
            pos += 3
        if with_proj:
            wproj_ref = refs[pos]
            pos += 1
        if first_pass:
            (e_ref,) = refs[pos:pos + 1]
            pos += 1
        else:
            st_out_ref, aux_ref = refs[pos:pos + 2]
            pos += 2
        if with_proj:
            proj_ref = refs[pos]
            pos += 1
        buf_ref, st_ref = refs[pos:pos + 2]
        i = pl.program_id(0)
        a_re = ab_ref[:, :NS]
        a_im = -ab_ref[:, NS:] if reverse else ab_ref[:, NS:]

        @pl.when(i == 0)
        def _():
            if first_pass:
                st_ref[...] = jnp.zeros_like(st_ref)
            else:
                pr, pi = a_re, a_im
                for _ in range(n_sq):
                    pr, pi = pr * pr - pi * pi, 2.0 * pr * pi
                zr = jnp.zeros((1, NS), F32)
                cr, ci = zr, zr
                order = list(range(NSEG - 1, -1, -1)) if reverse else list(range(NSEG))
                st_ref[order[0]:order[0] + 1, :] = jnp.zeros((1, 2 * NS), F32)
                for jprev, j in zip(order[:-1], order[1:]):
                    er, ei = ends_ref[jprev:jprev + 1, :NS], ends_ref[jprev:jprev + 1, NS:]
                    cr, ci = er + pr * cr - pi * ci, ei + pr * ci + pi * cr
                    st_ref[j:j + 1, :NS] = cr
                    st_ref[j:j + 1, NS:] = ci
                if not reverse:
                    aux_ref[...] = st_ref[...]
                else:
                    aux_ref[...] = jnp.zeros_like(aux_ref)

        buf_ref[...] = jnp.dot(src_ref[...].astype(BF16), w_ref[...], preferred_element_type=F32)

        for s in range(n_slab):
            re_sl, im_sl = pl.ds(s * slab, slab), pl.ds(NS + s * slab, slab)
            ar = jnp.broadcast_to(a_re[:, s * slab:(s + 1) * slab], (NSEG, slab))
            ai = jnp.broadcast_to(a_im[:, s * slab:(s + 1) * slab], (NSEG, slab))

            def step(t, carry, re_sl=re_sl, im_sl=im_sl, ar=ar, ai=ai):
                k = (tk - 1 - t) if reverse else t
                r0 = pl.multiple_of(k * NSEG, NSEG)
                xr, xi = carry[0], carry[1]
                nr = ar * xr - ai * xi + buf_ref[pl.ds(r0, NSEG), re_sl]
                ni = ar * xi + ai * xr + buf_ref[pl.ds(r0, NSEG), im_sl]
                if not first_pass:
                    buf_ref[pl.ds(r0, NSEG), re_sl] = nr
                    buf_ref[pl.ds(r0, NSEG), im_sl] = ni
                if not with_dab:
                    return nr, ni
                rp = pl.multiple_of((k - 1) * NSEG, NSEG)
                xpr, xpi = xs_ref[pl.ds(rp, NSEG), re_sl], xs_ref[pl.ds(rp, NSEG), im_sl]
                return nr, ni, carry[2] + nr * xpr + ni * xpi, carry[3] + ni * xpr - nr * xpi

            carry = (st_ref[:, re_sl], st_ref[:, im_sl])
            if with_dab:
                z = jnp.zeros((NSEG, slab), F32)
                carry = lax.fori_loop(0, tk - 1, step, carry + (z, z))
                xr, xi, dr, di = carry
                nr = ar * xr - ai * xi + buf_ref[pl.ds(0, NSEG), re_sl]
                ni = ar * xi + ai * xr + buf_ref[pl.ds(0, NSEG), im_sl]
                buf_ref[pl.ds(0, NSEG), re_sl] = nr
                buf_ref[pl.ds(0, NSEG), im_sl] = ni
                at_start = i == nch - 1
                xpr = jnp.where(at_start, init_ref[:, re_sl], xsh_ref[:, re_sl])
                xpi = jnp.where(at_start, init_ref[:, im_sl], xsh_ref[:, im_sl])
                aux_ref[:, re_sl] += dr + nr * xpr + ni * xpi
                aux_ref[:, im_sl] += di + ni * xpr - nr * xpi
                carry = (nr, ni)
            else:
                carry = lax.fori_loop(0, tk, step, carry)
            st_ref[:, re_sl] = carry[0]
            st_ref[:, im_sl] = carry[1]

        if first_pass:
            @pl.when(i == nch - 1)
            def _():
                e_ref[...] = st_ref[...]
        else:
            st_out_ref[...] = buf_ref[...].astype(st_out_ref.dtype)
            if with_proj:
                proj_ref[...] = lax.dot_general(buf_ref[...].astype(BF16), wproj_ref[...], _NT, preferred_element_type=F32)

    chunk = (lambda i: (nch - 1 - i, 0)) if reverse else (lambda i: (i, 0))
    whole = lambda i: (0, 0)
    ins = [src, wmat, ab]
    in_specs = [pl.BlockSpec((rows, SW), chunk), pl.BlockSpec((SW, 2 * NS), whole), pl.BlockSpec((1, 2 * NS), whole)]
    small = SDS((NSEG, 2 * NS), F32)
    small_spec = pl.BlockSpec((NSEG, 2 * NS), whole)
    if not first_pass:
        ins.append(ends)
        in_specs.append(small_spec)
    if with_dab:
        ins += [xs, xs, init]
        in_specs += [pl.BlockSpec((rows, 2 * NS), chunk),
                     pl.BlockSpec((NSEG, 2 * NS), lambda i: (jnp.maximum((nch - 1 - i) * tk - 1, 0), 0)),
                     small_spec]
    if with_proj:
        ins.append(wproj)
        in_specs.append(pl.BlockSpec((SW, 2 * NS), whole))
    if first_pass:
        out_shape, out_specs = small, small_spec
    else:
        out_shape = (SDS((L, 2 * NS), BF16 if reverse else F32), small)
        out_specs = (pl.BlockSpec((rows, 2 * NS), chunk), small_spec)
        if with_proj:
            out_shape += (SDS((L, SW), F32),)
            out_specs += (pl.BlockSpec((rows, SW), chunk),)
    return pl.pallas_call(
        body, out_shape=out_shape, grid=(nch,), in_specs=in_specs, out_specs=out_specs,
        scratch_shapes=[pltpu.VMEM((rows, 2 * NS), F32), pltpu.VMEM((NSEG, 2 * NS), F32)], name=name,
        compiler_params=_cparams(("arbitrary",), 56),
    )(*ins)


def _to_segments(a):
    L, c = a.shape
    return a.reshape(NSEG, L // NSEG, c).transpose(1, 0, 2).reshape(L, c)


def _from_segments(a):
    L, c = a.shape
    return a.reshape(L // NSEG, NSEG, c).transpose(1, 0, 2).reshape(L, c)


def _peer(x, y, c, m):
    return ((1 - x) if (m >> 2) & 1 else x, (1 - y) if (m >> 1) & 1 else y, (1 - c) if m & 1 else c)


def _dev_index(p):
    return 4 * p[0] + 2 * p[1] + p[2]


def _exchange(arrs, scatter, name):
    n = len(arrs)

    def body(*refs):
        ins, outs = refs[:n], refs[n:2 * n]
        send_sems, recv_sems, loc_sems = refs[2 * n:]
        x, y, c = lax.axis_index("x"), lax.axis_index("y"), lax.axis_index("c")
        me = _dev_index((x, y, c))

        def src(w, to):
            return ins[w].at[to] if scatter else ins[w]

        def local(w):
            return pltpu.make_async_copy(src(w, me), outs[w].at[me], loc_sems.at[w])

        def remote(w, m):
            peer = _peer(x, y, c, m)
            return pltpu.make_async_remote_copy(src_ref=src(w, _dev_index(peer)), dst_ref=outs[w].at[me],
                                                send_sem=send_sems.at[w, m - 1], recv_sem=recv_sems.at[w, m - 1],
                                                device_id=peer, device_id_type=pl.DeviceIdType.MESH)

        def arrival(w, m):
            peer = _peer(x, y, c, m)
            return pltpu.make_async_remote_copy(src_ref=src(w, me), dst_ref=outs[w].at[_dev_index(peer)],
                                                send_sem=send_sems.at[w, m - 1], recv_sem=recv_sems.at[w, m - 1],
                                                device_id=peer, device_id_type=pl.DeviceIdType.MESH)

        for w in range(n):
            local(w).start()
        for w in range(n):
            for m in range(1, N_DEV):
                remote(w, m).start()
        for w in range(n):
            for m in range(1, N_DEV):
                arrival(w, m).wait_recv()
        for w in range(n):
            for m in range(1, N_DEV):
                remote(w, m).wait_send()
        for w in range(n):
            local(w).wait()

    anyspec = pl.BlockSpec(memory_space=pl.ANY)
    out_shape = tuple(SDS(a.shape if scatter else (N_DEV,) + a.shape, a.dtype) for a in arrs)
    return pl.pallas_call(
        body, out_shape=out_shape, in_specs=[anyspec] * n, out_specs=tuple([anyspec] * n),
        scratch_shapes=[pltpu.SemaphoreType.DMA((n, N_DEV - 1)), pltpu.SemaphoreType.DMA((n, N_DEV - 1)),
                        pltpu.SemaphoreType.DMA((n,))],
        name=name, compiler_params=pltpu.CompilerParams(has_side_effects=True),
    )(*arrs)


_HBM = pl.BlockSpec(memory_space=pltpu.HBM)
_SEM = pl.BlockSpec(memory_space=pltpu.SEMAPHORE)
_EFFECT = pltpu.SideEffectType.DATAFLOW_SIDE_EFFECTING


def _sem_index(w, m):
    return w * (N_DEV - 1) + m - 1


_ALL_MASKS = tuple(range(1, N_DEV))
_CHIP_MASKS = (2, 4, 6)
_FIRST_HOP_MASKS = (1,) + _CHIP_MASKS


def _exchange_start(arrs, scatter, name, masks=_ALL_MASKS):
    n = len(arrs)
    lands = [lax.empty(a.shape if scatter else (N_DEV,) + a.shape, a.dtype) for a in arrs]

    def body(*refs):
        ins, zones = refs[:n], refs[n:2 * n]
        send_sems, recv_sems = refs[2 * n], refs[2 * n + 1]
        token = refs[-1]
        x, y, c = lax.axis_index("x"), lax.axis_index("y"), lax.axis_index("c")
        me = _dev_index((x, y, c))
        for w in range(n):
            for m in masks:
                peer = _peer(x, y, c, m)
                pltpu.make_async_remote_copy(
                    src_ref=ins[w].at[_dev_index(peer)] if scatter else ins[w], dst_ref=zones[w].at[me],
                    send_sem=send_sems.at[_sem_index(w, m)], recv_sem=recv_sems.at[_sem_index(w, m)],
                    device_id=peer, device_id_type=pl.DeviceIdType.MESH).start()
        token[...] = jnp.zeros_like(token)

    sems = pltpu.SemaphoreType.DMA((n * (N_DEV - 1),))
    res = pl.pallas_call(
        body, name=name,
        out_shape=(sems, sems, *[pltpu.HBM(a.shape, a.dtype) for a in arrs], *[pltpu.HBM(z.shape, z.dtype) for z in lands],
                   SDS((8, 128), F32)),
        in_specs=[_HBM] * (2 * n), out_specs=(_SEM, _SEM, *([_HBM] * (2 * n)), pl.BlockSpec(memory_space=pltpu.VMEM)),
        input_output_aliases={i: 2 + i for i in range(2 * n)},
        compiler_params=pltpu.CompilerParams(has_side_effects=_EFFECT),
    )(*[pltpu.with_memory_space_constraint(a, pltpu.HBM) for a in arrs],
      *[pltpu.with_memory_space_constraint(z, pltpu.HBM) for z in lands])
    return (res[0], res[1], list(res[2:2 + n]), list(res[2 + n:2 + 2 * n])), res[-1]


def _exchange_wait(handle, after, scatter, name, masks=_ALL_MASKS):
    send_sems, recv_sems, thru, lands = handle
    n = len(thru)

    def body(*refs):
        ins, zones = refs[:n], refs[n:2 * n]
        send_sems, recv_sems = refs[2 * n], refs[2 * n + 1]
        x, y, c = lax.axis_index("x"), lax.axis_index("y"), lax.axis_index("c")
        me = _dev_index((x, y, c))
        for w in range(n):
            for m in masks:
                peer = _peer(x, y, c, m)
                copy = pltpu.make_async_remote_copy(
                    src_ref=ins[w].at[me] if scatter else ins[w], dst_ref=zones[w].at[_dev_index(peer)],
                    send_sem=send_sems.at[_sem_index(w, m)], recv_sem=recv_sems.at[_sem_index(w, m)],
                    device_id=peer, device_id_type=pl.DeviceIdType.MESH)
                copy.wait_send()
                copy.wait_recv()

    res = pl.pallas_call(
        body, name=name,
        out_shape=(*[pltpu.HBM(a.shape, a.dtype) for a in thru], *[pltpu.HBM(z.shape, z.dtype) for z in lands]),
        in_specs=[_HBM] * (2 * n) + [_SEM, _SEM, pl.BlockSpec(memory_space=pl.ANY)], out_specs=tuple([_HBM] * (2 * n)),
        input_output_aliases={i: i for i in range(2 * n)},
        compiler_params=pltpu.CompilerParams(has_side_effects=_EFFECT),
    )(*thru, *lands, send_sems, recv_sems, after)
    return list(res[:n]), list(res[n:])


def _forward_start(zones, name):
    n = len(zones)

    def body(*refs):
        zs = refs[:n]
        send_sems, recv_sems = refs[n], refs[n + 1]
        token = refs[-1]
        x, y, c = lax.axis_index("x"), lax.axis_index("y"), lax.axis_index("c")
        for w in range(n):
            for m in _CHIP_MASKS:
                slot = zs[w].at[_dev_index(_peer(x, y, c, m))]
                pltpu.make_async_remote_copy(
                    src_ref=slot, dst_ref=slot, send_sem=send_sems.at[_sem_index(w, m)],
                    recv_sem=recv_sems.at[_sem_index(w, m)], device_id=(x, y, 1 - c),
                    device_id_type=pl.DeviceIdType.MESH).start()
        token[...] = jnp.zeros_like(token)

    sems = pltpu.SemaphoreType.DMA((n * (N_DEV - 1),))
    res = pl.pallas_call(
        body, name=name, out_shape=(sems, sems, *[pltpu.HBM(z.shape, z.dtype) for z in zones], SDS((8, 128), F32)),
        in_specs=[_HBM] * n, out_specs=(_SEM, _SEM, *([_HBM] * n), pl.BlockSpec(memory_space=pltpu.VMEM)),
        input_output_aliases={i: 2 + i for i in range(n)},
        compiler_params=pltpu.CompilerParams(has_side_effects=_EFFECT),
    )(*[pltpu.with_memory_space_constraint(z, pltpu.HBM) for z in zones])
    return (res[0], res[1], list(res[2:2 + n])), res[-1]


def _forward_wait(handle, after, name):
    send_sems, recv_sems, zones = handle
    n = len(zones)

    def body(*refs):
        zs = refs[:n]
        send_sems, recv_sems = refs[n], refs[n + 1]
        x, y, c = lax.axis_index("x"), lax.axis_index("y"), lax.axis_index("c")
        for w in range(n):
            for m in _CHIP_MASKS:
                copy = pltpu.make_async_remote_copy(
                    src_ref=zs[w].at[_dev_index(_peer(x, y, c, m))], dst_ref=zs[w].at[_dev_index(_peer(x, y, 1 - c, m))],
                    send_sem=send_sems.at[_sem_index(w, m)], recv_sem=recv_sems.at[_sem_index(w, m)],
                    device_id=(x, y, 1 - c), device_id_type=pl.DeviceIdType.MESH)
                copy.wait_send()
                copy.wait_recv()

    res = pl.pallas_call(
        body, name=name, out_shape=tuple(pltpu.HBM(z.shape, z.dtype) for z in zones),
        in_specs=[_HBM] * n + [_SEM, _SEM, pl.BlockSpec(memory_space=pl.ANY)], out_specs=tuple([_HBM] * n),
        input_output_aliases={i: i for i in range(n)},
        compiler_params=pltpu.CompilerParams(has_side_effects=_EFFECT),
    )(*zones, send_sems, recv_sems, after)
    return list(res)


def _adam_math(g, w, m, v):
    m = ADAM_B1 * m + (1.0 - ADAM_B1) * g
    v = ADAM_B2 * v + (1.0 - ADAM_B2) * (g * g)
    m_hat = m / (1.0 - ADAM_B1 ** ADAM_STEP)
    v_hat = v / (1.0 - ADAM_B2 ** ADAM_STEP)
    delta = -ADAM_LR * (m_hat / (jnp.sqrt(v_hat) + ADAM_EPS) + ADAM_WD * w)
    return delta, m, v


def _adam(parts, w, m, v, name, tr=128):
    r, c = w.shape
    tr = next(t for t in (tr, 64, 32, 16, 8) if r % t == 0)

    def fn(cc, rr, pb, wb, mb, vb):
        g = pb[0].astype(F32)
        for d in range(1, N_DEV):
            g = g + pb[d].astype(F32)
        delta, nm, nv = _adam_math(g, wb, mb, vb)
        return g, delta, nm, nv

    blk = ((tr, c), lambda cc, rr: (rr, 0))
    o = SDS((r, c), F32)
    return _ew(fn, [(parts, (N_DEV, tr, c), lambda cc, rr: (0, rr, 0)), (w, *blk), (m, *blk), (v, *blk)],
               [(o, *blk, None)] * 4, (1, r // tr), name)


_SHARDED = ("w_in", "w_glu", "w_branch_attn", "w_branch_ssm", "w_out", "w_up", "w_down")
_COL_SHARDED = ("w_in", "w_glu", "w_branch_attn", "w_branch_ssm", "w_up")
_GROUPS = {"a": ("w_in",), "b": ("w_glu", "w_branch_attn", "w_branch_ssm", "w_out"), "c": ("w_up", "w_down")}
_SMALL = ("attn_norm_g", "b_in", "attn_sinks", "ssm_a_re", "ssm_a_im", "ssm_log_dt", "ssm_b_re", "ssm_b_im",
          "ssm_c_re", "ssm_c_im", "ssm_d", "b_glu", "ffn_norm_g", "conv_w", "conv_b", "final_norm_g")
_WEIGHTS = ("attn_norm_g", "w_in", "b_in", "attn_sinks", "ssm_a_re", "ssm_a_im", "ssm_log_dt", "ssm_b_re", "ssm_b_im",
            "ssm_c_re", "ssm_c_im", "ssm_d", "w_glu", "b_glu", "w_branch_attn", "w_branch_ssm", "w_out", "ffn_norm_g",
            "w_up", "conv_w", "conv_b", "w_down", "final_norm_g")


def _unstack_cols(g):
    return g.transpose(1, 0, 2).reshape(g.shape[1], g.shape[0] * g.shape[2])


def _stack_cols(a, d=N_DEV):
    k, n = a.shape
    return a.reshape(k, d, n // d).transpose(1, 0, 2)


def _pack(arrs):
    flat = jnp.concatenate([a.reshape(-1) for a in arrs])
    pad = (-flat.shape[0]) % 1024
    return jnp.pad(flat, (0, pad)).reshape(-1, 128)


def _local_step(x, tgt, wget, small, gput):
    L = x.shape[0]
    nr = lambda tm: L // tm

    h = _rmsnorm_fwd(x, small["attn_norm_g"], "norm1")
    wts = dict(wget("a", h))
    projb = _mm(h, wts["w_in"], bias=small["b_in_p"], out_dtype=BF16, name="proj")
    proj = projb
    attn_bias = _attn_bias()
    attn = _attn_fwd(projb, small["attn_sinks"], attn_bias, "attn_fwd")

    ab, bmat, cmat = _ssm_prep(small["a_re"], small["a_im"], small["logdt"], small["b_re"], small["b_im"],
                               small["c_re"], small["c_im"], "ssm_prep")
    u_seg = _to_segments(proj[:, C_U:C_PAD])
    ends_f = _ssm_scan(u_seg, bmat, ab, reverse=False, tk=128, name="ssm_ends_fwd")
    xs, init_f, y_seg = _ssm_scan(u_seg, bmat, ab, reverse=False, ends=ends_f, wproj=cmat, tk=64, name="ssm_scan_fwd")
    y_mm = _from_segments(y_seg)

    def gelu_fn(c, r, yb, ub, db):
        yv = yb + db * ub
        return yv, _gelu(yv)

    tm = 512
    y, gy = _ew(gelu_fn, [(y_mm, *_rc(tm, 256)), (proj, *_rc(tm, 256, C_U // 256)), (small["ssm_d"], *_col(1, 256))],
                [(SDS((L, SW), F32), *_rc(tm, 256), None), (SDS((L, SW), BF16), *_rc(tm, 256), None)],
                (2, nr(tm)), "ssm_gelu")
    wts.update(wget("b", gy))
    glu = _mm(gy, wts["w_glu"], bias=small["b_glu"], name="glu")

    def glu_fn(c, r, vb, gb):
        return (vb * _sigmoid(gb),)

    (ssm,) = _ew(glu_fn, [(glu, *_rc(tm, SW)), (glu, *_rc(tm, SW, 1))], [(SDS((L, SW), BF16), *_rc(tm, SW), None)],
                 (1, nr(tm)), "glu_gate")
    f32 = lambda ref, cols: ref[:, cols].astype(F32)
    tnm = 1024
    gate_tiles = [(projb, "tile", C_GA // tnm), (projb, "tile", C_GS // tnm)]

    def merge_ep(i, cols, ra, rs, ga, gs):
        sa, ss = _sigmoid(f32(ga, cols)), _sigmoid(f32(gs, cols))
        return sa * ra + ss * rs, ra, rs, sa, ss

    merged, br_a, br_s, sig_a, sig_s = _mm(attn, wts["w_branch_attn"], a2=ssm, b2=wts["w_branch_ssm"], tm=512, tn=tnm,
                                           extras=gate_tiles, epilogue=merge_ep,
                                           outs=[(SDS((L, D), BF16), "tile")] * 5, name="branch_merge")
    x1 = _mm(merged, wts["w_out"], res=x, name="out_proj")
    h2 = _rmsnorm_fwd(x1, small["ffn_norm_g"], "norm2")
    wts.update(wget("c", h2))
    conv_w = wts["conv_w"]
    w_up_v, w_up_g = wts["w_up_v"], wts["w_up_g"]
    tcf = 1408
    tma = 256
    hb = 16

    def conv_gate(first, gate, halo, cw, cb):
        halo = halo * jnp.logical_not(first).astype(F32)
        g1, g2 = _shift_rows(gate, halo, 1), _shift_rows(gate, halo, 2)
        return cb + cw[2:3] * gate + cw[1:2] * g1 + cw[0:1] * g2, g1, g2

    tmu, tnu = 1024, 512

    def up_ep(i, cols, rv, rg, h2_halo, wg, cw, cb):
        halo = jnp.dot(h2_halo[...], wg[:, cols], preferred_element_type=F32)[hb - 8:]
        gl, glg = _gelu_and_grad(conv_gate(i == 0, rg, halo, cw[:, cols], cb[:, cols])[0])
        return rv, rg, rv * gl, gl, glg

    up_v, up_g, act, gelu_cg, gelu_grad_cg = _mm(
        h2, w_up_v, b2=w_up_g, tm=tmu, tn=tnu, epilogue=up_ep, outs=[(SDS((L, DFF), BF16), "tile")] * 5, name="ffn_up_act",
        extras=[(h2, "spec", ((hb, D), lambda j, i: (jnp.maximum(i * (tmu // hb) - 1, 0), 0))),
                (w_up_g, "spec", ((D, tnu), lambda j, i: (0, j))), (conv_w, "col", 0), (small["conv_b"], "col", 0)])
    x2 = _mm(act, wts["w_down"], res=x1, name="ffn_down")
    d_x2, d_x2b, loss_cols, d_gf = _final_loss(x2, small["final_norm_g"], tgt, "final_loss")
    loss = jnp.sum(loss_cols)

    dw_down = _mm(act, d_x2b, ta=True, out_dtype=BF16, tm=tcf, tk=2048, name="dw_down")
    tmd, tnd = 1024, 512

    def dact_ep(i, cols, da, _, val_ref, gate_ref, halo_ref, gl_ref, glg_ref):
        val, gate, gl = f32(val_ref, cols), f32(gate_ref, cols), f32(gl_ref, cols)
        halo = f32(halo_ref, cols)[hb - 8:] * (i > 0).astype(F32)
        g1, g2 = _shift_rows(gate, halo, 1), _shift_rows(gate, halo, 2)
        d_cg = da * val * f32(glg_ref, cols)
        row3 = lax.broadcasted_iota(jnp.int32, (3, da.shape[1]), 0)
        s0 = jnp.sum(d_cg * g2, axis=0, keepdims=True)
        s1 = jnp.sum(d_cg * g1, axis=0, keepdims=True)
        s2 = jnp.sum(d_cg * gate, axis=0, keepdims=True)
        dcw = jnp.where(row3 == 0, s0, jnp.where(row3 == 1, s1, s2))
        return da * gl, d_cg, dcw, jnp.sum(d_cg, axis=0, keepdims=True)

    d_val, d_cg, d_conv_w, d_conv_b = _mm(
        d_x2b, wts["w_down"], tb=True, tm=tmd, tn=tnd, epilogue=dact_ep, name="d_act_bwd",
        extras=[(up_v, "tile", 0), (up_g, "tile", 0),
                (up_g, "spec", ((hb, tnd), lambda j, i: (jnp.maximum(i * (tmd // hb) - 1, 0), j))),
                (gelu_cg, "tile", 0), (gelu_grad_cg, "tile", 0)],
        outs=[(SDS((L, DFF), BF16), "tile")] * 2 + [(SDS((3, DFF), F32), "colacc"), (SDS((1, DFF), F32), "colacc")])
    ncf = DFF // tcf

    tmg = 512

    def gate_bwd(c, r, dcg, halo, cw):
        halo = halo[:8] * (r < nr(tmg) - 1).astype(F32)
        return (cw[2:3] * dcg + cw[1:2] * _shift_rows_up(dcg, halo, 1) + cw[0:1] * _shift_rows_up(dcg, halo, 2),)

    (d_gate,) = _ew(gate_bwd, [(d_cg, *_rc(tmg, tcf)),
                               (d_cg, (hb, tcf), lambda c, r: (jnp.minimum((r + 1) * (tmg // hb), L // hb - 1), c)),
                               (conv_w, *_col(3, tcf))],
                    [(SDS((L, DFF), BF16), *_rc(tmg, tcf), None)], (ncf, nr(tmg)), "ffn_gate_bwd")
    d_h2 = _mm(d_val, w_up_v, tb=True, name="d_h2_val")
    d_h2 = _mm(d_gate, w_up_g, tb=True, res=d_h2, name="d_h2_gate")
    assert tcf == 2 * DFF // N_DEV
    dw_up_v = _mm(h2, d_val, ta=True, out_dtype=BF16, tn=tcf, tk=2048, stack_out=True, name="dw_up_val")
    dw_up_g = _mm(h2, d_gate, ta=True, out_dtype=BF16, tn=tcf, tk=2048, stack_out=True, name="dw_up_gate")
    tok = gput("c", {"w_up_v": dw_up_v, "w_up_g": dw_up_g, "w_down": dw_down})
    d_x1, d_g2 = _rmsnorm_bwd(d_h2, x1, small["ffn_norm_g"] + tok[0, 0], d_x2, "norm2_bwd")

    dw_out = _mm(merged, d_x1, ta=True, out_dtype=BF16, name="dw_out")

    def dmerge_ep(i, cols, dm, _, a_ref, s_ref, sa_ref, ss_ref):
        sa, ss = f32(sa_ref, cols), f32(ss_ref, cols)
        return dm * sa, dm * ss, dm * f32(a_ref, cols) * (sa * (1.0 - sa)), dm * f32(s_ref, cols) * (ss * (1.0 - ss))

    d_bra, d_brs, d_ga, d_gs = _mm(d_x1, wts["w_out"], tb=True, tm=512, tn=tnm, epilogue=dmerge_ep,
                                   extras=[(br_a, "tile", 0), (br_s, "tile", 0), (sig_a, "tile", 0), (sig_s, "tile", 0)],
                                   outs=[(SDS((L, D), BF16), "tile")] * 4, name="d_merged_bwd")
    d_attn = _mm(d_bra, wts["w_branch_attn"], tb=True, out_dtype=BF16, name="d_attn")
    dw_ba = _mm(attn, d_bra, ta=True, out_dtype=BF16, name="dw_branch_attn")
    d_ssm = _mm(d_brs, wts["w_branch_ssm"], tb=True, name="d_ssm")
    dw_bs = _mm(ssm, d_brs, ta=True, out_dtype=BF16, name="dw_branch_ssm")
    dq, dkv_cur, dkv_prev, d_sinks = _attn_bwd(projb, small["attn_sinks"], attn_bias, d_attn, "attn_bwd")

    def glu_bwd(c, r, ds, vb, gb):
        sg = _sigmoid(gb)
        return ds * sg, ds * vb * (sg * (1.0 - sg))

    d_glu_v, d_glu_g = _ew(glu_bwd, [(d_ssm, *_rc(tm, SW)), (glu, *_rc(tm, SW)), (glu, *_rc(tm, SW, 1))],
                           [(SDS((L, SW), F32), *_rc(tm, SW), None)] * 2, (1, nr(tm)), "glu_gate_bwd")
    d_glu = jnp.concatenate([d_glu_v, d_glu_g], axis=1)
    d_gy = _mm(d_glu, wts["w_glu"], tb=True, name="d_gelu_y")
    dw_glu = _mm(gy, d_glu, ta=True, out_dtype=BF16, name="dw_glu")

    tok = gput("b", {"w_glu": dw_glu, "w_branch_attn": dw_ba, "w_branch_ssm": dw_bs, "w_out": dw_out})
    ab = ab + tok[0, 0]

    def gelu_bwd(c, r, dg, yb, ub, dgl):
        dy = dg * _gelu_grad(yb)
        return dy, jnp.sum(dy * ub, axis=0, keepdims=True), jnp.sum(dgl, axis=0, keepdims=True)

    dy, d_ssm_d, d_b_glu = _ew(
        gelu_bwd, [(d_gy, *_rc(tm, 256)), (y, *_rc(tm, 256)), (proj, *_rc(tm, 256, C_U // 256)), (d_glu, *_rc(tm, 512))],
        [(SDS((L, SW), F32), *_rc(tm, 256), None), (SDS((1, SW), F32), *_col(1, 256), "r"),
         (SDS((1, 2 * SW), F32), *_col(1, 512), "r")], (2, nr(tm)), "ssm_gelu_bwd")

    dy_seg = _to_segments(dy)
    ends_r = _ssm_scan(dy_seg, cmat, ab, reverse=True, tk=128, name="ssm_ends_bwd")
    lam, dab8, du_seg = _ssm_scan(dy_seg, cmat, ab, reverse=True, ends=ends_r, xs=xs, init=init_f, wproj=bmat,
                                  name="ssm_scan_bwd")
    du_mm = _from_segments(du_seg)
    dbm = _mm(u_seg, lam, ta=True, tm=512, name="ssm_dbmat")
    dcm = _mm(dy_seg, xs, ta=True, tm=512, name="ssm_dcmat")
    d_are, d_aim, d_ldt, d_bre, d_bim, d_cre, d_cim = _ssm_param_bwd(
        small["a_re"], small["a_im"], small["logdt"], small["b_re"], small["b_im"], dab8, dbm, dcm, "ssm_param_bwd")

    nb = L // BLK

    def dproj_fn(c, r, dqb, cur, prv, du, dyb, dsk, dga, dgs):
        dkv = cur + prv * (r < nb - 1).astype(F32)
        dub = du + dsk * dyb
        full = jnp.concatenate([dqb, dkv, dub, jnp.zeros((BLK, C_GA - C_PAD), F32), dga, dgs], axis=1)
        return full, jnp.sum(full, axis=0, keepdims=True)

    rowb = lambda w: ((BLK, w), lambda c, r: (r, 0))
    dproj, d_b_in = _ew(
        dproj_fn, [(dq, *rowb(AW)), (dkv_cur, *rowb(256)),
                   (dkv_prev, (BLK, 256), lambda c, r: (jnp.minimum(r + 1, nb - 1), 0)),
                   (du_mm, *rowb(SW)), (dy, *rowb(SW)), (small["ssm_d"], *_col(1, SW)), (d_ga, *rowb(D)), (d_gs, *rowb(D))],
        [(SDS((L, INP), BF16), *rowb(INP), None), (SDS((1, INP), F32), *_col(1, INP), "all")], (1, nb), "dproj")
    tok_small = gput("small", {
        "b_in": _unpad_cols(d_b_in), "attn_sinks": d_sinks[:, :NQ], "a_re": d_are, "a_im": d_aim, "logdt": d_ldt,
        "b_re": d_bre, "b_im": d_bim, "c_re": d_cre, "c_im": d_cim, "ssm_d": d_ssm_d, "b_glu": d_b_glu,
        "ffn_norm_g": d_g2, "conv_w": d_conv_w, "conv_b": d_conv_b, "final_norm_g": d_gf})
    dw_in = _mm(h, dproj, ta=True, out_dtype=BF16, name="dw_in")
    tok = gput("a", {"w_in": _unpad_cols(dw_in)}) + tok_small
    d_h = _mm(dproj, wts["w_in"], tb=True, bias=jnp.zeros((1, D), F32) + tok[0, 0], name="d_h")
    grad_x, d_g1 = _rmsnorm_bwd(d_h, x, small["attn_norm_g"], d_x1, "norm1_bwd")
    return loss, grad_x, {"attn_norm_g": d_g1}


def _small_layouts(p):
    gp = lambda a: a.reshape(1, NS)
    hgp = lambda a: a.transpose(2, 0, 1).reshape(H, NS)
    chgp = lambda a: a.transpose(1, 0, 2).reshape(H, NS)
    return {
        "attn_norm_g": p["attn_norm_g"].reshape(1, D), "ffn_norm_g": p["ffn_norm_g"].reshape(1, D),
        "final_norm_g": p["final_norm_g"].reshape(1, D),
        "b_in_p": _pad_cols(p["b_in"].reshape(1, INC)),
        "attn_sinks": p["attn_sinks"].reshape(1, NQ),
        "a_re": gp(p["ssm_a_re"]), "a_im": gp(p["ssm_a_im"]), "logdt": jnp.repeat(p["ssm_log_dt"], P).reshape(1, NS),
        "b_re": hgp(p["ssm_b_re"]), "b_im": hgp(p["ssm_b_im"]), "c_re": chgp(p["ssm_c_re"]), "c_im": chgp(p["ssm_c_im"]),
        "ssm_d": p["ssm_d"].reshape(1, SW), "b_glu": p["b_glu"].reshape(1, 2 * SW),
        "conv_b": p["conv_b"].reshape(1, DFF),
    }


def _small_grads_to_param_shapes(sg):
    from_hgp = lambda a: a.reshape(H, G, P).transpose(1, 2, 0)
    from_chgp = lambda a: a.reshape(H, G, P).transpose(1, 0, 2)
    flat = lambda a: a.reshape(-1)
    to_param = {
        "attn_norm_g": ("attn_norm_g", flat), "b_in": ("b_in", flat), "attn_sinks": ("attn_sinks", flat),
        "a_re": ("ssm_a_re", lambda a: a.reshape(G, P)), "a_im": ("ssm_a_im", lambda a: a.reshape(G, P)),
        "logdt": ("ssm_log_dt", lambda a: jnp.sum(a.reshape(G, P), axis=1)),
        "b_re": ("ssm_b_re", from_hgp), "b_im": ("ssm_b_im", from_hgp),
        "c_re": ("ssm_c_re", from_chgp), "c_im": ("ssm_c_im", from_chgp),
        "ssm_d": ("ssm_d", flat), "b_glu": ("b_glu", flat), "ffn_norm_g": ("ffn_norm_g", flat),
        "conv_w": ("conv_w", lambda a: a), "conv_b": ("conv_b", flat), "final_norm_g": ("final_norm_g", flat),
    }
    return {to_param[k][0]: to_param[k][1](a) for k, a in sg.items()}


def kernel(x, attn_norm_g, w_in, b_in, attn_sinks, ssm_a_re, ssm_a_im, ssm_log_dt, ssm_b_re, ssm_b_im, ssm_c_re, ssm_c_im, ssm_d, w_glu, b_glu, w_branch_attn, w_branch_ssm, w_out, ffn_norm_g, w_up, conv_w, conv_b, w_down, final_norm_g, loss_target, m_attn_norm_g, m_w_in, m_b_in, m_attn_sinks, m_ssm_a_re, m_ssm_a_im, m_ssm_log_dt, m_ssm_b_re, m_ssm_b_im, m_ssm_c_re, m_ssm_c_im, m_ssm_d, m_w_glu, m_b_glu, m_w_branch_attn, m_w_branch_ssm, m_w_out, m_ffn_norm_g, m_w_up, m_conv_w, m_conv_b, m_w_down, m_final_norm_g, v_attn_norm_g, v_w_in, v_b_in, v_attn_sinks, v_ssm_a_re, v_ssm_a_im, v_ssm_log_dt, v_ssm_b_re, v_ssm_b_im, v_ssm_c_re, v_ssm_c_im, v_ssm_d, v_w_glu, v_b_glu, v_w_branch_attn, v_w_branch_ssm, v_w_out, v_ffn_norm_g, v_w_up, v_conv_w, v_conv_b, v_w_down, v_final_norm_g):
    args = dict(locals())
    sq = lambda a: a if a.ndim == 1 else a[0]
    wv = {n: sq(args[n]) for n in _WEIGHTS}
    mv = {n: sq(args["m_" + n]) for n in _WEIGHTS}
    vv = {n: sq(args["v_" + n]) for n in _WEIGHTS}
    me = 4 * lax.axis_index("x") + 2 * lax.axis_index("y") + lax.axis_index("c")

    gather, tok = {}, jnp.zeros((8, 128), F32)
    for grp in ("a", "b", "c"):
        shards = [(wv[n] + tok[0, 0]).astype(BF16) for n in _GROUPS[grp]]
        if grp == "c":
            shards.append(jnp.pad(wv["conv_w"] + tok[0, 0], ((0, 5), (0, 64))))
        gather[grp], tok = _exchange_start(shards, False, "gather_start_" + grp,
                                           masks=_FIRST_HOP_MASKS if grp == "a" else _ALL_MASKS)
    small = _small_layouts(wv)
    small["attn_norm_g"] = small["attn_norm_g"] + tok[0, 0]

    def own_slot(land, src):
        return lax.dynamic_update_slice_in_dim(land, src, me, axis=0)

    def wget(grp, after):
        if grp == "a":
            thru, lands = _exchange_wait(gather[grp], after, False, "gather_wait_a", masks=_FIRST_HOP_MASKS)
            fwd, fwd_tok = _forward_start(lands, "gather_forward_start_a")
            lands = _forward_wait(fwd, fwd_tok, "gather_forward_wait_a")
        else:
            thru, lands = _exchange_wait(gather[grp], after, False, "gather_wait_" + grp)
        full = {}
        for n, t, g in zip(_GROUPS[grp], thru, lands):
            g = own_slot(g, t[None])
            full[n] = _unstack_cols(g) if n in _COL_SHARDED else g.reshape(N_DEV * g.shape[1], g.shape[2])
        if grp == "a":
            full["w_in"] = _pad_cols(full["w_in"])
        if grp == "c":
            full["conv_w"] = _unstack_cols(own_slot(lands[-1], thru[-1][None])[:, :3, :DFF // N_DEV])
            g = own_slot(lands[0], thru[0][None])
            full["w_up_v"], full["w_up_g"] = _unstack_cols(g[:N_DEV // 2]), _unstack_cols(g[N_DEV // 2:])
            del full["w_up"]
        return full

    scatter = {}

    early_names = [n for n in _SMALL if n != "attn_norm_g"]
    sgp = {}

    def gput(grp, grads):
        if grp == "small":
            sgp.update(_small_grads_to_param_shapes(grads))
            scatter[grp], token = _exchange_start([_pack([sgp[n] for n in early_names])], False, "gather_small_start")
            return token
        stacked = [_stack_cols(grads[n]) if n in _COL_SHARDED else grads[n].reshape(N_DEV, -1, D)
                   for n in _GROUPS[grp] if n != "w_up"]
        if grp == "c":
            half = N_DEV // 2
            stacked.insert(0, jnp.concatenate([grads["w_up_v"], grads["w_up_g"]]))
        scatter[grp], token = _exchange_start(stacked, True, "scatter_start_" + grp)
        return token

    loss, grad_x, sg = _local_step(x[0], loss_target[0], wget, small, gput)
    loss = lax.psum(loss, MESH_AXES)

    sgp.update(_small_grads_to_param_shapes(sg))
    small_names = [n for n in _SMALL]
    (norm_all,) = _exchange([jnp.pad(sgp["attn_norm_g"].reshape(1, D), ((0, 7), (0, 0)))], False, "gather_norm_grad")
    thru, (small_all,) = _exchange_wait(scatter["small"], norm_all, False, "gather_small_wait")
    small_all = own_slot(small_all, thru[0][None])

    outs_g, outs_d, outs_m, outs_v = {}, {}, {}, {}
    for grp in ("c", "b", "a"):
        thru, lands = _exchange_wait(scatter[grp], norm_all, True, "scatter_wait_" + grp)
        for n, t, pt in zip(_GROUPS[grp], thru, lands):
            pt = own_slot(pt, lax.dynamic_slice_in_dim(t, me, 1, axis=0))
            outs_g[n], outs_d[n], outs_m[n], outs_v[n] = _adam(pt, wv[n], mv[n], vv[n], "adam_" + n)

    sizes = [int(math.prod(sgp[n].shape)) for n in early_names]
    offs = [0]
    for s in sizes:
        offs.append(offs[-1] + s)

    def local_part(n, a):
        if n == "conv_w":
            return lax.dynamic_slice(a, (0, me * (DFF // N_DEV)), (3, DFF // N_DEV))
        return a

    rows = small_all.shape[1]

    def sum_fn(cc, rr, pb, nb_):
        g, gn = pb[0], nb_[0]
        for d in range(1, N_DEV):
            g, gn = g + pb[d], gn + nb_[d]
        return g, gn

    gsum, gnorm = _ew(sum_fn, [(small_all, (N_DEV, rows, 128), lambda cc, rr: (0, 0, 0)),
                               (norm_all, (N_DEV, 8, D), lambda cc, rr: (0, 0, 0))],
                      [(SDS((rows, 128), F32), (rows, 128), lambda cc, rr: (0, 0), None),
                       (SDS((8, D), F32), (8, D), lambda cc, rr: (0, 0), None)], (1, 1), "sum_small_grads")
    gflat = gsum.reshape(-1)
    gsmall = {n: local_part(n, gflat[offs[i]:offs[i + 1]].reshape(sgp[n].shape)) for i, n in enumerate(early_names)}
    gsmall["attn_norm_g"] = gnorm[0]
    as2d = lambda a: a.reshape(1, -1) if a.ndim == 1 else a.reshape(a.shape[0], -1)
    n_small = len(small_names)

    def adam_small(*refs):
        for i in range(n_small):
            g_ref, w_ref, m_ref, v_ref = refs[4 * i:4 * i + 4]
            outs = refs[4 * n_small + 3 * i:4 * n_small + 3 * i + 3]
            for o_ref, val in zip(outs, _adam_math(g_ref[...], w_ref[...], m_ref[...], v_ref[...])):
                o_ref[...] = val

    small_ins = [as2d(t[n]) for n in small_names for t in (gsmall, wv, mv, vv)]
    small_outs = pl.pallas_call(adam_small, name="adam_small",
                                out_shape=[SDS(as2d(wv[n]).shape, F32) for n in small_names for _ in range(3)])(*small_ins)
    for i, n in enumerate(small_names):
        sd, sm, sv = (t.reshape(wv[n].shape) for t in small_outs[3 * i:3 * i + 3])
        outs_g[n], outs_d[n], outs_m[n], outs_v[n] = gsmall[n], sd, sm, sv

    lead = lambda n, a: a if args[n].ndim == 1 else a[None]
    grad_x = grad_x[None]
    return (loss, grad_x, *[lead(n, outs_g[n]) for n in _WEIGHTS], *[lead(n, outs_d[n]) for n in _WEIGHTS],
            *[lead(n, outs_m[n]) for n in _WEIGHTS], *[lead(n, outs_v[n]) for n in _WEIGHTS])
```

```python
import functools
import math

import jax
import jax.numpy as jnp
from jax import lax
from jax.experimental import pallas as pl
from jax.experimental.pallas import tpu as pltpu

F32 = jnp.float32
BF16 = jnp.bfloat16
SDS = jax.ShapeDtypeStruct

N_DEV = 8
D = 2048
NQ, NKV, HD = 16, 2, 64
AW = NQ * HD
BLK = 128
SW, G, H, P = 512, 32, 16, 64
NS = G * P
DFF = 5632
INC = AW + 2 * NKV * HD + SW + 2 * D
C_K, C_U, C_PAD = AW, AW + 2 * NKV * HD, AW + 2 * NKV * HD + SW
C_GA, C_GS, INP = D, 2 * D, 3 * D
RMS_EPS = 1e-6
NEG_BIG = -1e30
ADAM_LR, ADAM_B1, ADAM_B2, ADAM_EPS, ADAM_WD, ADAM_STEP = 0.001, 0.9, 0.999, 1e-08, 0.01, 10
NSEG = 8
VMEM_CAP_MB = 60
MESH_AXES = ("x", "y", "c")


def _pad_cols(a):
    zeros = jnp.zeros(a.shape[:-1] + (C_GA - C_PAD,), a.dtype)
    return jnp.concatenate([a[..., :C_PAD], zeros, a[..., C_PAD:]], axis=-1)


def _unpad_cols(a):
    return jnp.concatenate([a[..., :C_PAD], a[..., C_GA:]], axis=-1)


def _cparams(sem, vmem_mb):
    return pltpu.CompilerParams(dimension_semantics=sem, vmem_limit_bytes=min(int(vmem_mb), VMEM_CAP_MB) << 20)


LANES = 128


def _tile(dim, pref):
    if dim <= pref:
        return dim
    for t in range(pref - pref % LANES, 0, -LANES):
        if dim % t == 0:
            return t
    raise ValueError(f"no tile for {dim}")


def _mm(a, b, *, ta=False, tb=False, bias=None, res=None, out_dtype=F32, tm=1024, tn=1024, tk=3072, name,
        a2=None, b2=None, extras=(), epilogue=None, outs=None, ep_cols=None, stack_out=False):
    m, k = (a.shape[1], a.shape[0]) if ta else a.shape
    n = b.shape[0] if tb else b.shape[1]
    assert (b.shape[1] if tb else b.shape[0]) == k, (a.shape, b.shape, ta, tb)
    tm, tn, tk = _tile(m, tm), _tile(n, tn), _tile(k, tk)
    nk = k // tk
    dims = (((0 if ta else 1,), (1 if tb else 0,)), ((), ()))
    has_bias, has_res, has_b2 = bias is not None, res is not None, b2 is not None
    has_a2 = a2 is not None
    assert not (has_b2 and (nk > 1 or ta or tb)) and not (has_a2 and not has_b2)
    if epilogue is None:
        outs = [(SDS((n // tn, m, tn) if stack_out else (m, n), out_dtype), "tile")]
    n_ex, n_out = len(extras), len(outs)
    tcn = tn if (epilogue is None or nk > 1 or ep_cols is None) else _tile(tn, ep_cols)

    def body(*refs):
        a_ref, b_ref = refs[0], refs[1]
        pos = 2
        a2_ref = refs[pos] if has_a2 else a_ref
        pos += has_a2
        b2_ref = refs[pos] if has_b2 else None
        pos += has_b2
        bias_ref = refs[pos] if has_bias else None
        pos += has_bias
        res_ref = refs[pos] if has_res else None
        pos += has_res
        ex_refs = refs[pos:pos + n_ex]
        o_refs = refs[pos + n_ex:pos + n_ex + n_out]
        i = pl.program_id(1)

        def product(rhs_ref, cols=None, lhs=None):
            rhs = rhs_ref[...] if cols is None else (rhs_ref[cols, :] if tb else rhs_ref[:, cols])
            lhs = a_ref[...].astype(BF16) if lhs is None else lhs
            return lax.dot_general(lhs, rhs.astype(BF16), dims, preferred_element_type=F32)

        def finish(r, cols):
            if has_bias:
                r = r + bias_ref[:, cols]
            if has_res:
                r = r + res_ref[:, cols].astype(F32)
            if epilogue is None:
                o_refs[0][:, cols] = r.astype(o_refs[0].dtype)
                return
            r2 = None
            if has_b2:
                r2 = jnp.dot(a2_ref[...].astype(BF16), b2_ref[:, cols].astype(BF16), preferred_element_type=F32)
            vals = epilogue(i, cols, r, r2, *ex_refs)
            for o_ref, v, (_, kind) in zip(o_refs, vals, outs):
                if kind == "tile":
                    o_ref[:, cols] = v.astype(o_ref.dtype)
                else:
                    @pl.when(i == 0)
                    def _(o_ref=o_ref, v=v):
                        o_ref[:, cols] = v.astype(o_ref.dtype)

                    @pl.when(i > 0)
                    def _(o_ref=o_ref, v=v):
                        o_ref[:, cols] += v.astype(o_ref.dtype)

        if nk == 1:
            lhs = a_ref[...].astype(BF16)
            for c0 in range(0, tn, tcn):
                cols = pl.ds(c0, tcn)
                finish(product(b_ref, cols, lhs), cols)
            return
        whole = pl.ds(0, tn)
        acc_ref = refs[-1]
        kk = pl.program_id(2)

        @pl.when(kk == 0)
        def _():
            acc_ref[...] = product(b_ref)

        @pl.when(jnp.logical_and(kk > 0, kk < nk - 1))
        def _():
            acc_ref[...] += product(b_ref)

        @pl.when(kk == nk - 1)
        def _():
            finish(acc_ref[...] + product(b_ref), whole)

    b_spec = pl.BlockSpec((tn, tk), lambda j, i, kk: (j, kk)) if tb else pl.BlockSpec((tk, tn), lambda j, i, kk: (kk, j))
    ins = [a, b]
    in_specs = [pl.BlockSpec((tk, tm), lambda j, i, kk: (kk, i)) if ta else pl.BlockSpec((tm, tk), lambda j, i, kk: (i, kk)),
                b_spec]
    tile_spec = pl.BlockSpec((tm, tn), lambda j, i, kk: (i, j))
    byt = 2 * tm * tk * a.dtype.itemsize + 2 * tk * tn * b.dtype.itemsize
    byt += (2 + has_b2) * 4 * tm * tn
    if has_a2:
        ins.append(a2)
        in_specs.append(pl.BlockSpec((tm, a2.shape[1]), lambda j, i, kk: (i, 0)))
        byt += 2 * tm * a2.shape[1] * a2.dtype.itemsize
    if has_b2:
        ins.append(b2)
        in_specs.append(pl.BlockSpec((b2.shape[0], tn), lambda j, i, kk: (0, j)))
        byt += 2 * b2.shape[0] * tn * b2.dtype.itemsize
    if has_bias:
        ins.append(bias)
        in_specs.append(pl.BlockSpec((1, tn), lambda j, i, kk: (0, j)))
    if has_res:
        ins.append(res)
        in_specs.append(tile_spec)
        byt += 2 * tm * tn * res.dtype.itemsize
    for arr, kind, arg in extras:
        ins.append(arr)
        if kind == "tile":
            in_specs.append(pl.BlockSpec((tm, tn), lambda j, i, kk, arg=arg: (i, j + arg)))
            byt += 2 * tm * tn * arr.dtype.itemsize + 4 * tm * tn
        elif kind == "col":
            in_specs.append(pl.BlockSpec((arr.shape[0], tn), lambda j, i, kk, arg=arg: (0, j + arg)))
        else:
            in_specs.append(pl.BlockSpec(arg[0], lambda j, i, kk, im=arg[1]: im(j, i)))
    out_specs = []
    for sds, kind in outs:
        if kind == "tile":
            out_specs.append(pl.BlockSpec((None, tm, tn), lambda j, i, kk: (j, i, 0)) if stack_out else tile_spec)
            byt += 2 * tm * tn * jnp.dtype(sds.dtype).itemsize
        else:
            out_specs.append(pl.BlockSpec((sds.shape[0], tn), lambda j, i, kk: (0, j)))
    res_ = pl.pallas_call(
        body, out_shape=tuple(o[0] for o in outs), grid=(n // tn, m // tm, nk), in_specs=in_specs,
        out_specs=tuple(out_specs), scratch_shapes=[pltpu.VMEM((tm, tn), F32)] if nk > 1 else [], name=name,
        compiler_params=_cparams(("arbitrary", "arbitrary", "arbitrary"), byt / 2**20 + (8 if epilogue is None else 20)),
    )(*ins)
    return res_[0] if epilogue is None else res_


def _ew(fn, ins, outs, grid, name, vmem_mb=40):
    n_in = len(ins)
    accs = [o[3] for o in outs]

    def body(*refs):
        c, r = pl.program_id(0), pl.program_id(1)
        vals = fn(c, r, *[ref[...].astype(F32) for ref in refs[:n_in]])
        for o_ref, v, acc in zip(refs[n_in:], vals, accs):
            if acc is None:
                o_ref[...] = v.astype(o_ref.dtype)
            else:
                first = (r == 0) if acc == "r" else jnp.logical_and(r == 0, c == 0)

                @pl.when(first)
                def _(o_ref=o_ref, v=v):
                    o_ref[...] = v.astype(o_ref.dtype)

                @pl.when(jnp.logical_not(first))
                def _(o_ref=o_ref, v=v):
                    o_ref[...] += v.astype(o_ref.dtype)

    res = pl.pallas_call(
        body, out_shape=tuple(o[0] for o in outs), grid=grid,
        in_specs=[pl.BlockSpec(bs, im) for _, bs, im in ins],
        out_specs=tuple(pl.BlockSpec(bs, im) for _, bs, im, _ in outs), name=name,
        compiler_params=_cparams(("arbitrary", "arbitrary"), vmem_mb),
    )(*[a for a, _, _ in ins])
    return res


def _rc(tm, tc, coff=0):
    return (tm, tc), (lambda c, r: (r, c + coff))


def _col(rows, tc, coff=0):
    return (rows, tc), (lambda c, r: (0, c + coff))


def _gelu(x):
    return 0.5 * x * (1.0 + lax.erf(x * (2.0 ** -0.5)))


def _gelu_and_grad(x):
    cdf = 0.5 * (1.0 + lax.erf(x * (2.0 ** -0.5)))
    return x * cdf, cdf + x * jnp.exp(-0.5 * x * x) * (1.0 / math.sqrt(2.0 * math.pi))


def _gelu_grad(x):
    return _gelu_and_grad(x)[1]


def _sigmoid(x):
    return 1.0 / (1.0 + jnp.exp(-x))


def _shift_rows(x, halo, s):
    rolled = pltpu.roll(x, s, 0)
    row8 = lax.broadcasted_iota(jnp.int32, halo.shape, 0)
    head = jnp.where(row8 < s, pltpu.roll(halo, s, 0), rolled[0:8])
    return jnp.concatenate([head, rolled[8:]], axis=0)


def _shift_rows_up(x, halo, s):
    tm = x.shape[0]
    rolled = pltpu.roll(x, tm - s, 0)
    row8 = lax.broadcasted_iota(jnp.int32, halo.shape, 0)
    tail = jnp.where(row8 >= 8 - s, pltpu.roll(halo, 8 - s, 0), rolled[tm - 8:])
    return jnp.concatenate([rolled[:tm - 8], tail], axis=0)


def _rmsnorm_fwd(x, g, name, tm=512):
    L = x.shape[0]

    def fn(c, r, xb, gb):
        rstd = lax.rsqrt(jnp.mean(xb * xb, axis=-1, keepdims=True) + RMS_EPS)
        return ((xb * rstd) * gb,)

    return _ew(fn, [(x, *_rc(tm, D)), (g, *_col(1, D))], [(SDS((L, D), BF16), *_rc(tm, D), None)], (1, L // tm), name)[0]


def _rmsnorm_bwd(dh, x, g, dres, name, tm=512):
    L = x.shape[0]

    def fn(c, r, dhb, xb, gb, drb):
        rstd = lax.rsqrt(jnp.mean(xb * xb, axis=-1, keepdims=True) + RMS_EPS)
        y = xb * rstd
        dy = dhb * gb
        dx = rstd * (dy - y * jnp.mean(dy * y, axis=-1, keepdims=True))
        return drb + dx, jnp.sum(dhb * y, axis=0, keepdims=True)

    return _ew(fn, [(dh, *_rc(tm, D)), (x, *_rc(tm, D)), (g, *_col(1, D)), (dres, *_rc(tm, D))],
               [(SDS((L, D), F32), *_rc(tm, D), None), (SDS((1, D), F32), *_col(1, D), "all")], (1, L // tm), name,
               vmem_mb=56)


def _final_loss(x2, g, tgt, name, tm=512):
    L = x2.shape[0]

    def fn(c, r, xb, gb, tb):
        rstd = lax.rsqrt(jnp.mean(xb * xb, axis=-1, keepdims=True) + RMS_EPS)
        y = xb * rstd
        err = y * gb - tb
        dout = err * (1.0 / D)
        dy = dout * gb
        dx = rstd * (dy - y * jnp.mean(dy * y, axis=-1, keepdims=True))
        return dx, dx, jnp.sum(err * err, axis=0, keepdims=True) * (0.5 / D), jnp.sum(dout * y, axis=0, keepdims=True)

    return _ew(fn, [(x2, *_rc(tm, D)), (g, *_col(1, D)), (tgt, *_rc(tm, D))],
               [(SDS((L, D), F32), *_rc(tm, D), None), (SDS((L, D), BF16), *_rc(tm, D), None),
                (SDS((1, D), F32), *_col(1, D), "all"),
                (SDS((1, D), F32), *_col(1, D), "all")], (1, L // tm), name, vmem_mb=56)


def _slope(h):
    return 2.0 ** (-8.0 * (h + 1) / NQ)


def _attn_bias():
    qi = lax.broadcasted_iota(jnp.int32, (BLK, 2 * BLK), 0)
    si = lax.broadcasted_iota(jnp.int32, (BLK, 2 * BLK), 1)
    dist = qi + BLK - si
    band = (dist >= 0) & (dist < BLK)
    slopes = jnp.asarray([_slope(h) for h in range(NQ)], F32)[:, None, None]
    alibi = -slopes * dist.astype(F32)[None]
    return jnp.stack([jnp.where((band & (si >= BLK))[None], alibi, NEG_BIG), jnp.where(band[None], alibi, NEG_BIG)])


def _attn_kv(kvc, kvp):
    kv = jnp.concatenate([kvp, kvc], axis=0).astype(F32)
    lo = lax.broadcasted_iota(jnp.int32, (2 * BLK, 128), 1) < HD

    def halves(t):
        tr = pltpu.roll(t, HD, 1)
        z = jnp.zeros_like(t)
        return {(0, 0): jnp.where(lo, t, z).astype(BF16), (0, 1): jnp.where(lo, z, tr).astype(BF16),
                (1, 0): jnp.where(lo, tr, z).astype(BF16), (1, 1): jnp.where(lo, z, t).astype(BF16)}

    return halves(kv[:, :128]), halves(kv[:, 128:])


_NT = (((1,), (1,)), ((), ()))
_TN = (((0,), (0,)), ((), ()))
_ATTN_SPECS = [pl.BlockSpec(memory_space=pltpu.SMEM),
               pl.BlockSpec((None, NQ, BLK, 2 * BLK), lambda n: (jnp.minimum(n, 1), 0, 0, 0)),
               pl.BlockSpec((BLK, AW), lambda n: (n, 0)),
               pl.BlockSpec((BLK, 256), lambda n: (n, C_K // 256)),
               pl.BlockSpec((BLK, 256), lambda n: (jnp.maximum(n - 1, 0), C_K // 256))]


def _attn_scores(q_ref, bias_ref, kmat, sc_ref):
    for j in range(NQ // 2):
        qs = q_ref[:, 128 * j:128 * (j + 1)] * (HD ** -0.5)
        for e in range(2):
            h = 2 * j + e
            sc_ref[h] = lax.dot_general(qs, kmat[(j // (NQ // 4), e)], _NT, preferred_element_type=F32) + bias_ref[h]


def _softmax_with_sink(s, sink):
    m = jnp.maximum(jnp.max(s, axis=-1, keepdims=True), sink)
    p = jnp.exp(s - m)
    esink = jnp.exp(sink - m)
    den = jnp.sum(p, axis=-1, keepdims=True) + esink
    return p / den, esink / den


def _attn_fwd(projb, sinks, bias, name):
    L = projb.shape[0]

    def body(s_ref, bias_ref, q_ref, kvc_ref, kvp_ref, o_ref, sc_ref, pr_ref):
        kmat, vmat = _attn_kv(kvc_ref[...], kvp_ref[...])
        _attn_scores(q_ref, bias_ref, kmat, sc_ref)
        for h in range(NQ):
            pr_ref[h] = _softmax_with_sink(sc_ref[h], s_ref[0, h])[0].astype(BF16)
        for j in range(NQ // 2):
            g = j // (NQ // 4)
            acc = jnp.dot(pr_ref[2 * j], vmat[(g, 0)], preferred_element_type=F32)
            acc = acc + jnp.dot(pr_ref[2 * j + 1], vmat[(g, 1)], preferred_element_type=F32)
            o_ref[:, 128 * j:128 * (j + 1)] = acc.astype(BF16)

    return pl.pallas_call(
        body, out_shape=SDS((L, AW), BF16), grid=(L // BLK,), in_specs=_ATTN_SPECS,
        out_specs=pl.BlockSpec((BLK, AW), lambda n: (n, 0)), name=name,
        scratch_shapes=[pltpu.VMEM((NQ, BLK, 2 * BLK), F32), pltpu.VMEM((NQ, BLK, 2 * BLK), BF16)],
        compiler_params=_cparams(("arbitrary",), 32),
    )(sinks, bias, projb, projb, projb)


def _attn_bwd(projb, sinks, bias, dattn, name):
    L = projb.shape[0]

    def body(s_ref, bias_ref, q_ref, kvc_ref, kvp_ref, do_ref, dq_ref, dcur_ref, dprev_ref, dsink_ref,
             sc_ref, dp_ref, ds_ref, pr_ref):
        n = pl.program_id(0)
        kmat, vmat = _attn_kv(kvc_ref[...], kvp_ref[...])
        _attn_scores(q_ref, bias_ref, kmat, sc_ref)
        for h in range(NQ):
            j, e = h // 2, h % 2
            dp_ref[h] = lax.dot_general(do_ref[:, 128 * j:128 * (j + 1)], vmat[(j // (NQ // 4), e)], _NT,
                                        preferred_element_type=F32)
        lane = lax.broadcasted_iota(jnp.int32, (1, 128), 1)
        dsv = jnp.zeros((1, 128), F32)
        for h in range(NQ):
            p, psink = _softmax_with_sink(sc_ref[h], s_ref[0, h])
            dp = dp_ref[h]
            drow = jnp.sum(p * dp, axis=-1, keepdims=True)
            ds_ref[h] = (p * (dp - drow)).astype(BF16)
            pr_ref[h] = p.astype(BF16)
            dsv = dsv + jnp.where(lane == h, -jnp.sum(psink * drow, axis=0, keepdims=True), 0.0)
        lo128 = lax.broadcasted_iota(jnp.int32, (BLK, 128), 1) < HD
        dk = [jnp.zeros((2 * BLK, 128), F32) for _ in range(NKV)]
        dv = [jnp.zeros((2 * BLK, 128), F32) for _ in range(NKV)]
        for j in range(NQ // 2):
            g = j // (NQ // 4)
            qs = q_ref[:, 128 * j:128 * (j + 1)] * (HD ** -0.5)
            dop = do_ref[:, 128 * j:128 * (j + 1)]
            zb = jnp.zeros_like(qs)
            dqp = jnp.zeros((BLK, 128), F32)
            for e in range(2):
                h = 2 * j + e
                half = lo128 if e == 0 else jnp.logical_not(lo128)
                dqp = dqp + jnp.dot(ds_ref[h], kmat[(g, e)], preferred_element_type=F32)
                dk[g] = dk[g] + lax.dot_general(ds_ref[h], jnp.where(half, qs, zb), _TN, preferred_element_type=F32)
                dv[g] = dv[g] + lax.dot_general(pr_ref[h], jnp.where(half, dop, zb), _TN, preferred_element_type=F32)
            dq_ref[:, 128 * j:128 * (j + 1)] = (dqp * (HD ** -0.5)).astype(BF16)
        lo256 = lax.broadcasted_iota(jnp.int32, (2 * BLK, 128), 1) < HD
        tot = [t + pltpu.roll(t, HD, 1) for t in (dk[0], dk[1], dv[0], dv[1])]
        dkv = jnp.concatenate([jnp.where(lo256, tot[0], tot[1]), jnp.where(lo256, tot[2], tot[3])], axis=1)
        dprev_ref[...] = dkv[:BLK]
        dcur_ref[...] = dkv[BLK:]

        @pl.when(n == 0)
        def _():
            dsink_ref[...] = dsv

        @pl.when(n > 0)
        def _():
            dsink_ref[...] += dsv

    tile = (NQ, BLK, 2 * BLK)
    return pl.pallas_call(
        body, out_shape=(SDS((L, AW), BF16), SDS((L, 256), F32), SDS((L, 256), F32), SDS((1, 128), F32)), grid=(L // BLK,),
        in_specs=_ATTN_SPECS + [pl.BlockSpec((BLK, AW), lambda n: (n, 0))],
        out_specs=(pl.BlockSpec((BLK, AW), lambda n: (n, 0)), pl.BlockSpec((BLK, 256), lambda n: (n, 0)),
                   pl.BlockSpec((BLK, 256), lambda n: (n, 0)), pl.BlockSpec((1, 128), lambda n: (0, 0))),
        scratch_shapes=[pltpu.VMEM(tile, F32), pltpu.VMEM(tile, F32), pltpu.VMEM(tile, BF16), pltpu.VMEM(tile, BF16)],
        name=name, compiler_params=_cparams(("arbitrary",), 40),
    )(sinks, bias, projb, projb, projb, dattn)


def _disc(a_re, a_im, logdt, b_re, b_im):
    dt = jnp.exp(logdt)
    mag = jnp.exp(a_re * dt)
    ab_re = mag * jnp.cos(a_im * dt)
    ab_im = mag * jnp.sin(a_im * dt)
    nr = ab_re - 1.0
    ni = ab_im
    den = a_re * a_re + a_im * a_im
    z_re = (nr * a_re + ni * a_im) / den
    z_im = (ni * a_re - nr * a_im) / den
    return ab_re, ab_im, z_re * b_re - z_im * b_im, z_re * b_im + z_im * b_re


def _group_mask():
    row = lax.broadcasted_iota(jnp.int32, (SW, NS), 0) // H
    col = lax.broadcasted_iota(jnp.int32, (SW, NS), 1) // P
    return row == col


def _block_diag(re, im):
    mask = _group_mask()
    z = jnp.zeros((SW, NS), F32)
    return jnp.concatenate([jnp.where(mask, jnp.tile(re, (G, 1)), z), jnp.where(mask, jnp.tile(im, (G, 1)), z)], axis=1)


def _block_diag_t(big):
    mask = _group_mask()
    z = jnp.zeros((SW, NS), F32)
    re = jnp.sum(jnp.where(mask, big[:, :NS], z).reshape(G, H, NS), axis=0)
    im = jnp.sum(jnp.where(mask, big[:, NS:], z).reshape(G, H, NS), axis=0)
    return re, im


def _ssm_prep(a_re, a_im, logdt, b_re, b_im, c_re, c_im, name):
    def body(are, aim, ldt, bre, bim, cre, cim, ab_ref, bm_ref, cm_ref):
        ab_re, ab_im, bb_re, bb_im = _disc(are[...], aim[...], ldt[...], bre[...], bim[...])
        ab_ref[...] = jnp.concatenate([ab_re, ab_im], axis=1)
        bm_ref[...] = _block_diag(bb_re, bb_im).astype(BF16)
        cm_ref[...] = _block_diag(cre[...], -cim[...]).astype(BF16)

    return pl.pallas_call(body, out_shape=(SDS((1, 2 * NS), F32), SDS((SW, 2 * NS), BF16), SDS((SW, 2 * NS), BF16)),
                          name=name, compiler_params=pltpu.CompilerParams(vmem_limit_bytes=48 << 20),
                          )(a_re, a_im, logdt, b_re, b_im, c_re, c_im)


def _ssm_param_bwd(a_re, a_im, logdt, b_re, b_im, dab8, dbm, dcm, name):
    def body(are, aim, ldt, bre, bim, dab_ref, dbm_ref, dcm_ref, o_are, o_aim, o_ldt, o_bre, o_bim, o_cre, o_cim):
        dab = jnp.sum(dab_ref[...], axis=0, keepdims=True)
        dbb_re, dbb_im = _block_diag_t(dbm_ref[...])
        _, vjp = jax.vjp(_disc, are[...], aim[...], ldt[...], bre[...], bim[...])
        d_are, d_aim, d_ldt, d_bre, d_bim = vjp((dab[:, :NS], dab[:, NS:], dbb_re, dbb_im))
        o_are[...], o_aim[...], o_ldt[...], o_bre[...], o_bim[...] = d_are, d_aim, d_ldt, d_bre, d_bim
        dc_re, dc_imn = _block_diag_t(dcm_ref[...])
        o_cre[...] = dc_re
        o_cim[...] = -dc_imn

    v1, vh = SDS((1, NS), F32), SDS((H, NS), F32)
    return pl.pallas_call(body, out_shape=(v1, v1, v1, vh, vh, vh, vh), name=name,
                          compiler_params=pltpu.CompilerParams(vmem_limit_bytes=56 << 20),
                          )(a_re, a_im, logdt, b_re, b_im, dab8, dbm, dcm)


def _ssm_scan(src, wmat, ab, *, reverse, ends=None, xs=None, init=None, wproj=None, name, tk=32):
    L = src.shape[0]
    rows = NSEG * tk
    nch = L // rows
    seg_len = L // NSEG
    n_sq = int(math.log2(seg_len))
    assert 2 ** n_sq == seg_len and L % rows == 0
    first_pass = ends is None
    with_dab = (not first_pass) and reverse
    with_proj = wproj is not None
    assert not (with_proj and first_pass)
    slab = 512
    n_slab = NS // slab

    def body(*refs):
        src_ref, w_ref, ab_ref = refs[:3]
        pos = 3
        if not first_pass:
            ends_ref = refs[pos]
            pos += 1
        if with_dab:
            xs_ref, xsh_ref, init_ref = refs[pos:pos + 3]
            pos += 3
        if with_proj:
            wproj_ref = refs[pos]
            pos += 1
        if first_pass:
            (e_ref,) = refs[pos:pos + 1]
            pos += 1
        else:
            st_out_ref, aux_ref = refs[pos:pos + 2]
            pos += 2
        if with_proj:
            proj_ref = refs[pos]
            pos += 1
        buf_ref, st_ref = refs[pos:pos + 2]
        i = pl.program_id(0)
        a_re = ab_ref[:, :NS]
        a_im = -ab_ref[:, NS:] if reverse else ab_ref[:, NS:]

        @pl.when(i == 0)
        def _():
            if first_pass:
                st_ref[...] = jnp.zeros_like(st_ref)
            else:
                pr, pi = a_re, a_im
                for _ in range(n_sq):
                    pr, pi = pr * pr - pi * pi, 2.0 * pr * pi
                zr = jnp.zeros((1, NS), F32)
                cr, ci = zr, zr
                order = list(range(NSEG - 1, -1, -1)) if reverse else list(range(NSEG))
                st_ref[order[0]:order[0] + 1, :] = jnp.zeros((1, 2 * NS), F32)
                for jprev, j in zip(order[:-1], order[1:]):
                    er, ei = ends_ref[jprev:jprev + 1, :NS], ends_ref[jprev:jprev + 1, NS:]
                    cr, ci = er + pr * cr - pi * ci, ei + pr * ci + pi * cr
                    st_ref[j:j + 1, :NS] = cr
                    st_ref[j:j + 1, NS:] = ci
                if not reverse:
                    aux_ref[...] = st_ref[...]
                else:
                    aux_ref[...] = jnp.zeros_like(aux_ref)

        buf_ref[...] = jnp.dot(src_ref[...].astype(BF16), w_ref[...], preferred_element_type=F32)

        for s in range(n_slab):
            re_sl, im_sl = pl.ds(s * slab, slab), pl.ds(NS + s * slab, slab)
            ar = jnp.broadcast_to(a_re[:, s * slab:(s + 1) * slab], (NSEG, slab))
            ai = jnp.broadcast_to(a_im[:, s * slab:(s + 1) * slab], (NSEG, slab))

            def step(t, carry, re_sl=re_sl, im_sl=im_sl, ar=ar, ai=ai):
                k = (tk - 1 - t) if reverse else t
                r0 = pl.multiple_of(k * NSEG, NSEG)
                xr, xi = carry[0], carry[1]
                nr = ar * xr - ai * xi + buf_ref[pl.ds(r0, NSEG), re_sl]
                ni = ar * xi + ai * xr + buf_ref[pl.ds(r0, NSEG), im_sl]
                if not first_pass:
                    buf_ref[pl.ds(r0, NSEG), re_sl] = nr
                    buf_ref[pl.ds(r0, NSEG), im_sl] = ni
                if not with_dab:
                    return nr, ni
                rp = pl.multiple_of((k - 1) * NSEG, NSEG)
                xpr, xpi = xs_ref[pl.ds(rp, NSEG), re_sl], xs_ref[pl.ds(rp, NSEG), im_sl]
                return nr, ni, carry[2] + nr * xpr + ni * xpi, carry[3] + ni * xpr - nr * xpi

            carry = (st_ref[:, re_sl], st_ref[:, im_sl])
            if with_dab:
                z = jnp.zeros((NSEG, slab), F32)
                carry = lax.fori_loop(0, tk - 1, step, carry + (z, z))
                xr, xi, dr, di = carry
                nr = ar * xr - ai * xi + buf_ref[pl.ds(0, NSEG), re_sl]
                ni = ar * xi + ai * xr + buf_ref[pl.ds(0, NSEG), im_sl]
                buf_ref[pl.ds(0, NSEG), re_sl] = nr
                buf_ref[pl.ds(0, NSEG), im_sl] = ni
                at_start = i == nch - 1
                xpr = jnp.where(at_start, init_ref[:, re_sl], xsh_ref[:, re_sl])
                xpi = jnp.where(at_start, init_ref[:, im_sl], xsh_ref[:, im_sl])
                aux_ref[:, re_sl] += dr + nr * xpr + ni * xpi
                aux_ref[:, im_sl] += di + ni * xpr - nr * xpi
                carry = (nr, ni)
            else:
                carry = lax.fori_loop(0, tk, step, carry)
            st_ref[:, re_sl] = carry[0]
            st_ref[:, im_sl] = carry[1]

        if first_pass:
            @pl.when(i == nch - 1)
            def _():
                e_ref[...] = st_ref[...]
        else:
            st_out_ref[...] = buf_ref[...].astype(st_out_ref.dtype)
            if with_proj:
                proj_ref[...] = lax.dot_general(buf_ref[...].astype(BF16), wproj_ref[...], _NT, preferred_element_type=F32)

    chunk = (lambda i: (nch - 1 - i, 0)) if reverse else (lambda i: (i, 0))
    whole = lambda i: (0, 0)
    ins = [src, wmat, ab]
    once = pl.Buffered(1)
    in_specs = [pl.BlockSpec((rows, SW), chunk), pl.BlockSpec((SW, 2 * NS), whole, pipeline_mode=once),
                pl.BlockSpec((1, 2 * NS), whole)]
    small = SDS((NSEG, 2 * NS), F32)
    small_spec = pl.BlockSpec((NSEG, 2 * NS), whole)
    if not first_pass:
        ins.append(ends)
        in_specs.append(small_spec)
    if with_dab:
        ins += [xs, xs, init]
        in_specs += [pl.BlockSpec((rows, 2 * NS), chunk),
                     pl.BlockSpec((NSEG, 2 * NS), lambda i: (jnp.maximum((nch - 1 - i) * tk - 1, 0), 0)),
                     small_spec]
    if with_proj:
        ins.append(wproj)
        in_specs.append(pl.BlockSpec((SW, 2 * NS), whole, pipeline_mode=once))
    if first_pass:
        out_shape, out_specs = small, small_spec
    else:
        out_shape = (SDS((L, 2 * NS), BF16 if reverse else F32), small)
        out_specs = (pl.BlockSpec((rows, 2 * NS), chunk), small_spec)
        if with_proj:
            out_shape += (SDS((L, SW), F32),)
            out_specs += (pl.BlockSpec((rows, SW), chunk),)
    return pl.pallas_call(
        body, out_shape=out_shape, grid=(nch,), in_specs=in_specs, out_specs=out_specs,
        scratch_shapes=[pltpu.VMEM((rows, 2 * NS), F32), pltpu.VMEM((NSEG, 2 * NS), F32)], name=name,
        compiler_params=_cparams(("arbitrary",), 56),
    )(*ins)


def _to_segments(a):
    L, c = a.shape
    return a.reshape(NSEG, L // NSEG, c).transpose(1, 0, 2).reshape(L, c)


def _from_segments(a):
    L, c = a.shape
    return a.reshape(L // NSEG, NSEG, c).transpose(1, 0, 2).reshape(L, c)


def _peer(x, y, c, m):
    return ((1 - x) if (m >> 2) & 1 else x, (1 - y) if (m >> 1) & 1 else y, (1 - c) if m & 1 else c)


def _dev_index(p):
    return 4 * p[0] + 2 * p[1] + p[2]


def _exchange(arrs, scatter, name):
    n = len(arrs)

    def body(*refs):
        ins, outs = refs[:n], refs[n:2 * n]
        send_sems, recv_sems, loc_sems = refs[2 * n:]
        x, y, c = lax.axis_index("x"), lax.axis_index("y"), lax.axis_index("c")
        me = _dev_index((x, y, c))

        def src(w, to):
            return ins[w].at[to] if scatter else ins[w]

        def local(w):
            return pltpu.make_async_copy(src(w, me), outs[w].at[me], loc_sems.at[w])

        def remote(w, m):
            peer = _peer(x, y, c, m)
            return pltpu.make_async_remote_copy(src_ref=src(w, _dev_index(peer)), dst_ref=outs[w].at[me],
                                                send_sem=send_sems.at[w, m - 1], recv_sem=recv_sems.at[w, m - 1],
                                                device_id=peer, device_id_type=pl.DeviceIdType.MESH)

        def arrival(w, m):
            peer = _peer(x, y, c, m)
            return pltpu.make_async_remote_copy(src_ref=src(w, me), dst_ref=outs[w].at[_dev_index(peer)],
                                                send_sem=send_sems.at[w, m - 1], recv_sem=recv_sems.at[w, m - 1],
                                                device_id=peer, device_id_type=pl.DeviceIdType.MESH)

        for w in range(n):
            local(w).start()
        for w in range(n):
            for m in range(1, N_DEV):
                remote(w, m).start()
        for w in range(n):
            for m in range(1, N_DEV):
                arrival(w, m).wait_recv()
        for w in range(n):
            for m in range(1, N_DEV):
                remote(w, m).wait_send()
        for w in range(n):
            local(w).wait()

    anyspec = pl.BlockSpec(memory_space=pl.ANY)
    out_shape = tuple(SDS(a.shape if scatter else (N_DEV,) + a.shape, a.dtype) for a in arrs)
    return pl.pallas_call(
        body, out_shape=out_shape, in_specs=[anyspec] * n, out_specs=tuple([anyspec] * n),
        scratch_shapes=[pltpu.SemaphoreType.DMA((n, N_DEV - 1)), pltpu.SemaphoreType.DMA((n, N_DEV - 1)),
                        pltpu.SemaphoreType.DMA((n,))],
        name=name, compiler_params=pltpu.CompilerParams(has_side_effects=True),
    )(*arrs)


_HBM = pl.BlockSpec(memory_space=pltpu.HBM)
_SEM = pl.BlockSpec(memory_space=pltpu.SEMAPHORE)
_EFFECT = pltpu.SideEffectType.DATAFLOW_SIDE_EFFECTING


def _sem_index(w, m):
    return w * (N_DEV - 1) + m - 1


_ALL_MASKS = tuple(range(1, N_DEV))
_CHIP_MASKS = (2, 4, 6)
_FIRST_HOP_MASKS = (1,) + _CHIP_MASKS


def _exchange_start(arrs, scatter, name, masks=_ALL_MASKS):
    n = len(arrs)
    lands = [lax.empty(a.shape if scatter else (N_DEV,) + a.shape, a.dtype) for a in arrs]

    def body(*refs):
        ins, zones = refs[:n], refs[n:2 * n]
        send_sems, recv_sems = refs[2 * n], refs[2 * n + 1]
        token = refs[-1]
        x, y, c = lax.axis_index("x"), lax.axis_index("y"), lax.axis_index("c")
        me = _dev_index((x, y, c))
        for w in range(n):
            for m in masks:
                peer = _peer(x, y, c, m)
                pltpu.make_async_remote_copy(
                    src_ref=ins[w].at[_dev_index(peer)] if scatter else ins[w], dst_ref=zones[w].at[me],
                    send_sem=send_sems.at[_sem_index(w, m)], recv_sem=recv_sems.at[_sem_index(w, m)],
                    device_id=peer, device_id_type=pl.DeviceIdType.MESH).start()
        token[...] = jnp.zeros_like(token)

    sems = pltpu.SemaphoreType.DMA((n * (N_DEV - 1),))
    res = pl.pallas_call(
        body, name=name,
        out_shape=(sems, sems, *[pltpu.HBM(a.shape, a.dtype) for a in arrs], *[pltpu.HBM(z.shape, z.dtype) for z in lands],
                   SDS((8, 128), F32)),
        in_specs=[_HBM] * (2 * n), out_specs=(_SEM, _SEM, *([_HBM] * (2 * n)), pl.BlockSpec(memory_space=pltpu.VMEM)),
        input_output_aliases={i: 2 + i for i in range(2 * n)},
        compiler_params=pltpu.CompilerParams(has_side_effects=_EFFECT),
    )(*[pltpu.with_memory_space_constraint(a, pltpu.HBM) for a in arrs],
      *[pltpu.with_memory_space_constraint(z, pltpu.HBM) for z in lands])
    return (res[0], res[1], list(res[2:2 + n]), list(res[2 + n:2 + 2 * n])), res[-1]


def _exchange_wait(handle, after, scatter, name, masks=_ALL_MASKS):
    send_sems, recv_sems, thru, lands = handle
    n = len(thru)

    def body(*refs):
        ins, zones = refs[:n], refs[n:2 * n]
        send_sems, recv_sems = refs[2 * n], refs[2 * n + 1]
        x, y, c = lax.axis_index("x"), lax.axis_index("y"), lax.axis_index("c")
        me = _dev_index((x, y, c))
        for w in range(n):
            for m in masks:
                peer = _peer(x, y, c, m)
                copy = pltpu.make_async_remote_copy(
                    src_ref=ins[w].at[me] if scatter else ins[w], dst_ref=zones[w].at[_dev_index(peer)],
                    send_sem=send_sems.at[_sem_index(w, m)], recv_sem=recv_sems.at[_sem_index(w, m)],
                    device_id=peer, device_id_type=pl.DeviceIdType.MESH)
                copy.wait_send()
                copy.wait_recv()

    res = pl.pallas_call(
        body, name=name,
        out_shape=(*[pltpu.HBM(a.shape, a.dtype) for a in thru], *[pltpu.HBM(z.shape, z.dtype) for z in lands]),
        in_specs=[_HBM] * (2 * n) + [_SEM, _SEM, pl.BlockSpec(memory_space=pl.ANY)], out_specs=tuple([_HBM] * (2 * n)),
        input_output_aliases={i: i for i in range(2 * n)},
        compiler_params=pltpu.CompilerParams(has_side_effects=_EFFECT),
    )(*thru, *lands, send_sems, recv_sems, after)
    return list(res[:n]), list(res[n:])


def _forward_start(zones, name):
    n = len(zones)

    def body(*refs):
        zs = refs[:n]
        send_sems, recv_sems = refs[n], refs[n + 1]
        token = refs[-1]
        x, y, c = lax.axis_index("x"), lax.axis_index("y"), lax.axis_index("c")
        for w in range(n):
            for m in _CHIP_MASKS:
                slot = zs[w].at[_dev_index(_peer(x, y, c, m))]
                pltpu.make_async_remote_copy(
                    src_ref=slot, dst_ref=slot, send_sem=send_sems.at[_sem_index(w, m)],
                    recv_sem=recv_sems.at[_sem_index(w, m)], device_id=(x, y, 1 - c),
                    device_id_type=pl.DeviceIdType.MESH).start()
        token[...] = jnp.zeros_like(token)

    sems = pltpu.SemaphoreType.DMA((n * (N_DEV - 1),))
    res = pl.pallas_call(
        body, name=name, out_shape=(sems, sems, *[pltpu.HBM(z.shape, z.dtype) for z in zones], SDS((8, 128), F32)),
        in_specs=[_HBM] * n, out_specs=(_SEM, _SEM, *([_HBM] * n), pl.BlockSpec(memory_space=pltpu.VMEM)),
        input_output_aliases={i: 2 + i for i in range(n)},
        compiler_params=pltpu.CompilerParams(has_side_effects=_EFFECT),
    )(*[pltpu.with_memory_space_constraint(z, pltpu.HBM) for z in zones])
    return (res[0], res[1], list(res[2:2 + n])), res[-1]


def _forward_wait(handle, after, name):
    send_sems, recv_sems, zones = handle
    n = len(zones)

    def body(*refs):
        zs = refs[:n]
        send_sems, recv_sems = refs[n], refs[n + 1]
        x, y, c = lax.axis_index("x"), lax.axis_index("y"), lax.axis_index("c")
        for w in range(n):
            for m in _CHIP_MASKS:
                copy = pltpu.make_async_remote_copy(
                    src_ref=zs[w].at[_dev_index(_peer(x, y, c, m))], dst_ref=zs[w].at[_dev_index(_peer(x, y, 1 - c, m))],
                    send_sem=send_sems.at[_sem_index(w, m)], recv_sem=recv_sems.at[_sem_index(w, m)],
                    device_id=(x, y, 1 - c), device_id_type=pl.DeviceIdType.MESH)
                copy.wait_send()
                copy.wait_recv()

    res = pl.pallas_call(
        body, name=name, out_shape=tuple(pltpu.HBM(z.shape, z.dtype) for z in zones),
        in_specs=[_HBM] * n + [_SEM, _SEM, pl.BlockSpec(memory_space=pl.ANY)], out_specs=tuple([_HBM] * n),
        input_output_aliases={i: i for i in range(n)},
        compiler_params=pltpu.CompilerParams(has_side_effects=_EFFECT),
    )(*zones, send_sems, recv_sems, after)
    return list(res)


def _adam_math(g, w, m, v):
    m = ADAM_B1 * m + (1.0 - ADAM_B1) * g
    v = ADAM_B2 * v + (1.0 - ADAM_B2) * (g * g)
    m_hat = m / (1.0 - ADAM_B1 ** ADAM_STEP)
    v_hat = v / (1.0 - ADAM_B2 ** ADAM_STEP)
    delta = -ADAM_LR * (m_hat / (jnp.sqrt(v_hat) + ADAM_EPS) + ADAM_WD * w)
    return delta, m, v


def _adam(parts, w, m, v, name, tr=128):
    r, c = w.shape
    tr = next(t for t in (tr, 64, 32, 16, 8) if r % t == 0)

    def fn(cc, rr, pb, wb, mb, vb):
        g = pb[0].astype(F32)
        for d in range(1, N_DEV):
            g = g + pb[d].astype(F32)
        delta, nm, nv = _adam_math(g, wb, mb, vb)
        return g, delta, nm, nv

    blk = ((tr, c), lambda cc, rr: (rr, 0))
    o = SDS((r, c), F32)
    return _ew(fn, [(parts, (N_DEV, tr, c), lambda cc, rr: (0, rr, 0)), (w, *blk), (m, *blk), (v, *blk)],
               [(o, *blk, None)] * 4, (1, r // tr), name)


_SHARDED = ("w_in", "w_glu", "w_branch_attn", "w_branch_ssm", "w_out", "w_up", "w_down")
_COL_SHARDED = ("w_in", "w_glu", "w_branch_attn", "w_branch_ssm", "w_up")
_GROUPS = {"a": ("w_in",), "b": ("w_glu", "w_branch_attn", "w_branch_ssm", "w_out"), "c": ("w_up", "w_down")}
_SMALL = ("attn_norm_g", "b_in", "attn_sinks", "ssm_a_re", "ssm_a_im", "ssm_log_dt", "ssm_b_re", "ssm_b_im",
          "ssm_c_re", "ssm_c_im", "ssm_d", "b_glu", "ffn_norm_g", "conv_w", "conv_b", "final_norm_g")
_WEIGHTS = ("attn_norm_g", "w_in", "b_in", "attn_sinks", "ssm_a_re", "ssm_a_im", "ssm_log_dt", "ssm_b_re", "ssm_b_im",
            "ssm_c_re", "ssm_c_im", "ssm_d", "w_glu", "b_glu", "w_branch_attn", "w_branch_ssm", "w_out", "ffn_norm_g",
            "w_up", "conv_w", "conv_b", "w_down", "final_norm_g")


def _unstack_cols(g):
    return g.transpose(1, 0, 2).reshape(g.shape[1], g.shape[0] * g.shape[2])


def _stack_cols(a, d=N_DEV):
    k, n = a.shape
    return a.reshape(k, d, n // d).transpose(1, 0, 2)


def _pack(arrs):
    flat = jnp.concatenate([a.reshape(-1) for a in arrs])
    pad = (-flat.shape[0]) % 1024
    return jnp.pad(flat, (0, pad)).reshape(-1, 128)


def _local_step(x, tgt, wget, small, gput):
    L = x.shape[0]
    nr = lambda tm: L // tm

    h = _rmsnorm_fwd(x, small["attn_norm_g"], "norm1")
    wts = dict(wget("a", h))
    projb = _mm(h, wts["w_in"], bias=small["b_in_p"], out_dtype=BF16, name="proj")
    proj = projb
    attn_bias = _attn_bias()
    attn = _attn_fwd(projb, small["attn_sinks"], attn_bias, "attn_fwd")

    ab, bmat, cmat = _ssm_prep(small["a_re"], small["a_im"], small["logdt"], small["b_re"], small["b_im"],
                               small["c_re"], small["c_im"], "ssm_prep")
    u_seg = _to_segments(proj[:, C_U:C_PAD])
    ends_f = _ssm_scan(u_seg, bmat, ab, reverse=False, tk=128, name="ssm_ends_fwd")
    xs, init_f, y_seg = _ssm_scan(u_seg, bmat, ab, reverse=False, ends=ends_f, wproj=cmat, tk=64, name="ssm_scan_fwd")
    y_mm = _from_segments(y_seg)

    def gelu_fn(c, r, yb, ub, db):
        yv = yb + db * ub
        return yv, _gelu(yv)

    tm = 512
    y, gy = _ew(gelu_fn, [(y_mm, *_rc(tm, 256)), (proj, *_rc(tm, 256, C_U // 256)), (small["ssm_d"], *_col(1, 256))],
                [(SDS((L, SW), F32), *_rc(tm, 256), None), (SDS((L, SW), BF16), *_rc(tm, 256), None)],
                (2, nr(tm)), "ssm_gelu")
    wts.update(wget("b", gy))
    glu = _mm(gy, wts["w_glu"], bias=small["b_glu"], name="glu")

    def glu_fn(c, r, vb, gb):
        return (vb * _sigmoid(gb),)

    (ssm,) = _ew(glu_fn, [(glu, *_rc(tm, SW)), (glu, *_rc(tm, SW, 1))], [(SDS((L, SW), BF16), *_rc(tm, SW), None)],
                 (1, nr(tm)), "glu_gate")
    f32 = lambda ref, cols: ref[:, cols].astype(F32)
    tnm = 1024
    gate_tiles = [(projb, "tile", C_GA // tnm), (projb, "tile", C_GS // tnm)]

    def merge_ep(i, cols, ra, rs, ga, gs):
        sa, ss = _sigmoid(f32(ga, cols)), _sigmoid(f32(gs, cols))
        return sa * ra + ss * rs, ra, rs, sa, ss

    merged, br_a, br_s, sig_a, sig_s = _mm(attn, wts["w_branch_attn"], a2=ssm, b2=wts["w_branch_ssm"], tm=512, tn=tnm,
                                           extras=gate_tiles, epilogue=merge_ep,
                                           outs=[(SDS((L, D), BF16), "tile")] * 5, name="branch_merge")
    x1 = _mm(merged, wts["w_out"], res=x, name="out_proj")
    h2 = _rmsnorm_fwd(x1, small["ffn_norm_g"], "norm2")
    wts.update(wget("c", h2))
    conv_w = wts["conv_w"]
    w_up_v, w_up_g = wts["w_up_v"], wts["w_up_g"]
    tcf = 1408
    tma = 256
    hb = 16

    def conv_gate(first, gate, halo, cw, cb):
        halo = halo * jnp.logical_not(first).astype(F32)
        g1, g2 = _shift_rows(gate, halo, 1), _shift_rows(gate, halo, 2)
        return cb + cw[2:3] * gate + cw[1:2] * g1 + cw[0:1] * g2, g1, g2

    tmu, tnu = 1024, 512

    def up_ep(i, cols, rv, rg, h2_halo, wg, cw, cb):
        halo = jnp.dot(h2_halo[...], wg[:, cols], preferred_element_type=F32)[hb - 8:]
        gl, glg = _gelu_and_grad(conv_gate(i == 0, rg, halo, cw[:, cols], cb[:, cols])[0])
        return rg, rv * gl, gl, rv * glg

    up_g, act, gelu_cg, val_gelu_grad = _mm(
        h2, w_up_v, b2=w_up_g, tm=tmu, tn=tnu, epilogue=up_ep, outs=[(SDS((L, DFF), BF16), "tile")] * 4, name="ffn_up_act",
        extras=[(h2, "spec", ((hb, D), lambda j, i: (jnp.maximum(i * (tmu // hb) - 1, 0), 0))),
                (w_up_g, "spec", ((D, tnu), lambda j, i: (0, j))), (conv_w, "col", 0), (small["conv_b"], "col", 0)])
    x2 = _mm(act, wts["w_down"], res=x1, name="ffn_down")
    d_x2, d_x2b, loss_cols, d_gf = _final_loss(x2, small["final_norm_g"], tgt, "final_loss")
    loss = jnp.sum(loss_cols)

    dw_down = _mm(act, d_x2b, ta=True, out_dtype=BF16, tm=tcf, tk=2048, name="dw_down")
    tmd, tnd = 1024, 512

    def dact_ep(i, cols, da, _, gate_ref, halo_ref, gl_ref, vg_ref):
        gate, gl = f32(gate_ref, cols), f32(gl_ref, cols)
        halo = f32(halo_ref, cols)[hb - 8:] * (i > 0).astype(F32)
        g1, g2 = _shift_rows(gate, halo, 1), _shift_rows(gate, halo, 2)
        d_cg = da * f32(vg_ref, cols)
        row3 = lax.broadcasted_iota(jnp.int32, (3, da.shape[1]), 0)
        s0 = jnp.sum(d_cg * g2, axis=0, keepdims=True)
        s1 = jnp.sum(d_cg * g1, axis=0, keepdims=True)
        s2 = jnp.sum(d_cg * gate, axis=0, keepdims=True)
        dcw = jnp.where(row3 == 0, s0, jnp.where(row3 == 1, s1, s2))
        return da * gl, d_cg, dcw, jnp.sum(d_cg, axis=0, keepdims=True)

    d_val, d_cg, d_conv_w, d_conv_b = _mm(
        d_x2b, wts["w_down"], tb=True, tm=tmd, tn=tnd, epilogue=dact_ep, name="d_act_bwd",
        extras=[(up_g, "tile", 0), (up_g, "spec", ((hb, tnd), lambda j, i: (jnp.maximum(i * (tmd // hb) - 1, 0), j))),
                (gelu_cg, "tile", 0), (val_gelu_grad, "tile", 0)],
        outs=[(SDS((L, DFF), BF16), "tile")] * 2 + [(SDS((3, DFF), F32), "colacc"), (SDS((1, DFF), F32), "colacc")])
    ncf = DFF // tcf

    tmg = 512

    def gate_bwd(c, r, dcg, halo, cw):
        halo = halo[:8] * (r < nr(tmg) - 1).astype(F32)
        return (cw[2:3] * dcg + cw[1:2] * _shift_rows_up(dcg, halo, 1) + cw[0:1] * _shift_rows_up(dcg, halo, 2),)

    (d_gate,) = _ew(gate_bwd, [(d_cg, *_rc(tmg, tcf)),
                               (d_cg, (hb, tcf), lambda c, r: (jnp.minimum((r + 1) * (tmg // hb), L // hb - 1), c)),
                               (conv_w, *_col(3, tcf))],
                    [(SDS((L, DFF), BF16), *_rc(tmg, tcf), None)], (ncf, nr(tmg)), "ffn_gate_bwd")
    d_h2 = _mm(d_val, w_up_v, tb=True, name="d_h2_val")
    d_h2 = _mm(d_gate, w_up_g, tb=True, res=d_h2, name="d_h2_gate")
    assert tcf == 2 * DFF // N_DEV
    dw_up_v = _mm(h2, d_val, ta=True, out_dtype=BF16, tn=tcf, tk=2048, stack_out=True, name="dw_up_val")
    dw_up_g = _mm(h2, d_gate, ta=True, out_dtype=BF16, tn=tcf, tk=2048, stack_out=True, name="dw_up_gate")
    tok = gput("c", {"w_up_v": dw_up_v, "w_up_g": dw_up_g, "w_down": dw_down})
    d_x1, d_g2 = _rmsnorm_bwd(d_h2, x1, small["ffn_norm_g"] + tok[0, 0], d_x2, "norm2_bwd")

    dw_out = _mm(merged, d_x1, ta=True, out_dtype=BF16, name="dw_out")

    def dmerge_ep(i, cols, dm, _, a_ref, s_ref, sa_ref, ss_ref):
        sa, ss = f32(sa_ref, cols), f32(ss_ref, cols)
        return dm * sa, dm * ss, dm * f32(a_ref, cols) * (sa * (1.0 - sa)), dm * f32(s_ref, cols) * (ss * (1.0 - ss))

    d_bra, d_brs, d_ga, d_gs = _mm(d_x1, wts["w_out"], tb=True, tm=512, tn=tnm, epilogue=dmerge_ep,
                                   extras=[(br_a, "tile", 0), (br_s, "tile", 0), (sig_a, "tile", 0), (sig_s, "tile", 0)],
                                   outs=[(SDS((L, D), BF16), "tile")] * 4, name="d_merged_bwd")
    d_attn = _mm(d_bra, wts["w_branch_attn"], tb=True, out_dtype=BF16, name="d_attn")
    dw_ba = _mm(attn, d_bra, ta=True, out_dtype=BF16, name="dw_branch_attn")
    d_ssm = _mm(d_brs, wts["w_branch_ssm"], tb=True, name="d_ssm")
    dw_bs = _mm(ssm, d_brs, ta=True, out_dtype=BF16, name="dw_branch_ssm")
    dq, dkv_cur, dkv_prev, d_sinks = _attn_bwd(projb, small["attn_sinks"], attn_bias, d_attn, "attn_bwd")

    def glu_bwd(c, r, ds, vb, gb):
        sg = _sigmoid(gb)
        return ds * sg, ds * vb * (sg * (1.0 - sg))

    d_glu_v, d_glu_g = _ew(glu_bwd, [(d_ssm, *_rc(tm, SW)), (glu, *_rc(tm, SW)), (glu, *_rc(tm, SW, 1))],
                           [(SDS((L, SW), F32), *_rc(tm, SW), None)] * 2, (1, nr(tm)), "glu_gate_bwd")
    d_glu = jnp.concatenate([d_glu_v, d_glu_g], axis=1)
    d_gy = _mm(d_glu, wts["w_glu"], tb=True, name="d_gelu_y")
    dw_glu = _mm(gy, d_glu, ta=True, out_dtype=BF16, name="dw_glu")

    tok = gput("b", {"w_glu": dw_glu, "w_branch_attn": dw_ba, "w_branch_ssm": dw_bs, "w_out": dw_out})
    ab = ab + tok[0, 0]

    def gelu_bwd(c, r, dg, yb, ub, dgl):
        dy = dg * _gelu_grad(yb)
        return dy, jnp.sum(dy * ub, axis=0, keepdims=True), jnp.sum(dgl, axis=0, keepdims=True)

    dy, d_ssm_d, d_b_glu = _ew(
        gelu_bwd, [(d_gy, *_rc(tm, 256)), (y, *_rc(tm, 256)), (proj, *_rc(tm, 256, C_U // 256)), (d_glu, *_rc(tm, 512))],
        [(SDS((L, SW), F32), *_rc(tm, 256), None), (SDS((1, SW), F32), *_col(1, 256), "r"),
         (SDS((1, 2 * SW), F32), *_col(1, 512), "r")], (2, nr(tm)), "ssm_gelu_bwd")

    dy_seg = _to_segments(dy)
    ends_r = _ssm_scan(dy_seg, cmat, ab, reverse=True, tk=128, name="ssm_ends_bwd")
    lam, dab8, du_seg = _ssm_scan(dy_seg, cmat, ab, reverse=True, ends=ends_r, xs=xs, init=init_f, wproj=bmat,
                                  tk=64, name="ssm_scan_bwd")
    du_mm = _from_segments(du_seg)
    dbm = _mm(u_seg, lam, ta=True, tm=512, name="ssm_dbmat")
    dcm = _mm(dy_seg, xs, ta=True, tm=512, name="ssm_dcmat")
    d_are, d_aim, d_ldt, d_bre, d_bim, d_cre, d_cim = _ssm_param_bwd(
        small["a_re"], small["a_im"], small["logdt"], small["b_re"], small["b_im"], dab8, dbm, dcm, "ssm_param_bwd")

    nb = L // BLK

    def dproj_fn(c, r, dqb, cur, prv, du, dyb, dsk, dga, dgs):
        dkv = cur + prv * (r < nb - 1).astype(F32)
        dub = du + dsk * dyb
        full = jnp.concatenate([dqb, dkv, dub, jnp.zeros((BLK, C_GA - C_PAD), F32), dga, dgs], axis=1)
        return full, jnp.sum(full, axis=0, keepdims=True)

    rowb = lambda w: ((BLK, w), lambda c, r: (r, 0))
    dproj, d_b_in = _ew(
        dproj_fn, [(dq, *rowb(AW)), (dkv_cur, *rowb(256)),
                   (dkv_prev, (BLK, 256), lambda c, r: (jnp.minimum(r + 1, nb - 1), 0)),
                   (du_mm, *rowb(SW)), (dy, *rowb(SW)), (small["ssm_d"], *_col(1, SW)), (d_ga, *rowb(D)), (d_gs, *rowb(D))],
        [(SDS((L, INP), BF16), *rowb(INP), None), (SDS((1, INP), F32), *_col(1, INP), "all")], (1, nb), "dproj")
    tok_small = gput("small", {
        "b_in": _unpad_cols(d_b_in), "attn_sinks": d_sinks[:, :NQ], "a_re": d_are, "a_im": d_aim, "logdt": d_ldt,
        "b_re": d_bre, "b_im": d_bim, "c_re": d_cre, "c_im": d_cim, "ssm_d": d_ssm_d, "b_glu": d_b_glu,
        "ffn_norm_g": d_g2, "conv_w": d_conv_w, "conv_b": d_conv_b, "final_norm_g": d_gf})
    dw_in = _mm(h, dproj, ta=True, out_dtype=BF16, name="dw_in")
    tok = gput("a", {"w_in": _unpad_cols(dw_in)}) + tok_small
    d_h = _mm(dproj, wts["w_in"], tb=True, bias=jnp.zeros((1, D), F32) + tok[0, 0], name="d_h")
    grad_x, d_g1 = _rmsnorm_bwd(d_h, x, small["attn_norm_g"], d_x1, "norm1_bwd")
    return loss, grad_x, {"attn_norm_g": d_g1}


def _small_layouts(p):
    gp = lambda a: a.reshape(1, NS)
    hgp = lambda a: a.transpose(2, 0, 1).reshape(H, NS)
    chgp = lambda a: a.transpose(1, 0, 2).reshape(H, NS)
    return {
        "attn_norm_g": p["attn_norm_g"].reshape(1, D), "ffn_norm_g": p["ffn_norm_g"].reshape(1, D),
        "final_norm_g": p["final_norm_g"].reshape(1, D),
        "b_in_p": _pad_cols(p["b_in"].reshape(1, INC)),
        "attn_sinks": p["attn_sinks"].reshape(1, NQ),
        "a_re": gp(p["ssm_a_re"]), "a_im": gp(p["ssm_a_im"]), "logdt": jnp.repeat(p["ssm_log_dt"], P).reshape(1, NS),
        "b_re": hgp(p["ssm_b_re"]), "b_im": hgp(p["ssm_b_im"]), "c_re": chgp(p["ssm_c_re"]), "c_im": chgp(p["ssm_c_im"]),
        "ssm_d": p["ssm_d"].reshape(1, SW), "b_glu": p["b_glu"].reshape(1, 2 * SW),
        "conv_b": p["conv_b"].reshape(1, DFF),
    }


def _small_grads_to_param_shapes(sg):
    from_hgp = lambda a: a.reshape(H, G, P).transpose(1, 2, 0)
    from_chgp = lambda a: a.reshape(H, G, P).transpose(1, 0, 2)
    flat = lambda a: a.reshape(-1)
    to_param = {
        "attn_norm_g": ("attn_norm_g", flat), "b_in": ("b_in", flat), "attn_sinks": ("attn_sinks", flat),
        "a_re": ("ssm_a_re", lambda a: a.reshape(G, P)), "a_im": ("ssm_a_im", lambda a: a.reshape(G, P)),
        "logdt": ("ssm_log_dt", lambda a: jnp.sum(a.reshape(G, P), axis=1)),
        "b_re": ("ssm_b_re", from_hgp), "b_im": ("ssm_b_im", from_hgp),
        "c_re": ("ssm_c_re", from_chgp), "c_im": ("ssm_c_im", from_chgp),
        "ssm_d": ("ssm_d", flat), "b_glu": ("b_glu", flat), "ffn_norm_g": ("ffn_norm_g", flat),
        "conv_w": ("conv_w", lambda a: a), "conv_b": ("conv_b", flat), "final_norm_g": ("final_norm_g", flat),
    }
    return {to_param[k][0]: to_param[k][1](a) for k, a in sg.items()}


def kernel(x, attn_norm_g, w_in, b_in, attn_sinks, ssm_a_re, ssm_a_im, ssm_log_dt, ssm_b_re, ssm_b_im, ssm_c_re, ssm_c_im, ssm_d, w_glu, b_glu, w_branch_attn, w_branch_ssm, w_out, ffn_norm_g, w_up, conv_w, conv_b, w_down, final_norm_g, loss_target, m_attn_norm_g, m_w_in, m_b_in, m_attn_sinks, m_ssm_a_re, m_ssm_a_im, m_ssm_log_dt, m_ssm_b_re, m_ssm_b_im, m_ssm_c_re, m_ssm_c_im, m_ssm_d, m_w_glu, m_b_glu, m_w_branch_attn, m_w_branch_ssm, m_w_out, m_ffn_norm_g, m_w_up, m_conv_w, m_conv_b, m_w_down, m_final_norm_g, v_attn_norm_g, v_w_in, v_b_in, v_attn_sinks, v_ssm_a_re, v_ssm_a_im, v_ssm_log_dt, v_ssm_b_re, v_ssm_b_im, v_ssm_c_re, v_ssm_c_im, v_ssm_d, v_w_glu, v_b_glu, v_w_branch_attn, v_w_branch_ssm, v_w_out, v_ffn_norm_g, v_w_up, v_conv_w, v_conv_b, v_w_down, v_final_norm_g):
    args = dict(locals())
    sq = lambda a: a if a.ndim == 1 else a[0]
    wv = {n: sq(args[n]) for n in _WEIGHTS}
    mv = {n: sq(args["m_" + n]) for n in _WEIGHTS}
    vv = {n: sq(args["v_" + n]) for n in _WEIGHTS}
    me = 4 * lax.axis_index("x") + 2 * lax.axis_index("y") + lax.axis_index("c")

    gather, tok = {}, jnp.zeros((8, 128), F32)
    for grp in ("a", "b", "c"):
        shards = [(wv[n] + tok[0, 0]).astype(BF16) for n in _GROUPS[grp]]
        if grp == "c":
            shards.append(jnp.pad(wv["conv_w"] + tok[0, 0], ((0, 5), (0, 64))))
        gather[grp], tok = _exchange_start(shards, False, "gather_start_" + grp,
                                           masks=_FIRST_HOP_MASKS if grp == "a" else _ALL_MASKS)
    small = _small_layouts(wv)
    small["attn_norm_g"] = small["attn_norm_g"] + tok[0, 0]

    def own_slot(land, src):
        return lax.dynamic_update_slice_in_dim(land, src, me, axis=0)

    def wget(grp, after):
        if grp == "a":
            thru, lands = _exchange_wait(gather[grp], after, False, "gather_wait_a", masks=_FIRST_HOP_MASKS)
            fwd, fwd_tok = _forward_start(lands, "gather_forward_start_a")
            lands = _forward_wait(fwd, fwd_tok, "gather_forward_wait_a")
        else:
            thru, lands = _exchange_wait(gather[grp], after, False, "gather_wait_" + grp)
        full = {}
        for n, t, g in zip(_GROUPS[grp], thru, lands):
            g = own_slot(g, t[None])
            full[n] = _unstack_cols(g) if n in _COL_SHARDED else g.reshape(N_DEV * g.shape[1], g.shape[2])
        if grp == "a":
            full["w_in"] = _pad_cols(full["w_in"])
        if grp == "c":
            full["conv_w"] = _unstack_cols(own_slot(lands[-1], thru[-1][None])[:, :3, :DFF // N_DEV])
            g = own_slot(lands[0], thru[0][None])
            full["w_up_v"], full["w_up_g"] = _unstack_cols(g[:N_DEV // 2]), _unstack_cols(g[N_DEV // 2:])
            del full["w_up"]
        return full

    scatter = {}

    early_names = [n for n in _SMALL if n != "attn_norm_g"]
    sgp = {}

    def gput(grp, grads):
        if grp == "small":
            sgp.update(_small_grads_to_param_shapes(grads))
            scatter[grp], token = _exchange_start([_pack([sgp[n] for n in early_names])], False, "gather_small_start")
            return token
        stacked = [_stack_cols(grads[n]) if n in _COL_SHARDED else grads[n].reshape(N_DEV, -1, D)
                   for n in _GROUPS[grp] if n != "w_up"]
        if grp == "c":
            half = N_DEV // 2
            stacked.insert(0, jnp.concatenate([grads["w_up_v"], grads["w_up_g"]]))
        scatter[grp], token = _exchange_start(stacked, True, "scatter_start_" + grp)
        return token

    loss, grad_x, sg = _local_step(x[0], loss_target[0], wget, small, gput)
    loss = lax.psum(loss, MESH_AXES)

    sgp.update(_small_grads_to_param_shapes(sg))
    small_names = [n for n in _SMALL]
    (norm_all,) = _exchange([jnp.pad(sgp["attn_norm_g"].reshape(1, D), ((0, 7), (0, 0)))], False, "gather_norm_grad")
    thru, (small_all,) = _exchange_wait(scatter["small"], norm_all, False, "gather_small_wait")
    small_all = own_slot(small_all, thru[0][None])

    outs_g, outs_d, outs_m, outs_v = {}, {}, {}, {}
    for grp in ("c", "b", "a"):
        thru, lands = _exchange_wait(scatter[grp], norm_all, True, "scatter_wait_" + grp)
        for n, t, pt in zip(_GROUPS[grp], thru, lands):
            pt = own_slot(pt, lax.dynamic_slice_in_dim(t, me, 1, axis=0))
            outs_g[n], outs_d[n], outs_m[n], outs_v[n] = _adam(pt, wv[n], mv[n], vv[n], "adam_" + n)

    sizes = [int(math.prod(sgp[n].shape)) for n in early_names]
    offs = [0]
    for s in sizes:
        offs.append(offs[-1] + s)

    def local_part(n, a):
        if n == "conv_w":
            return lax.dynamic_slice(a, (0, me * (DFF // N_DEV)), (3, DFF // N_DEV))
        return a

    rows = small_all.shape[1]

    def sum_fn(cc, rr, pb, nb_):
        g, gn = pb[0], nb_[0]
        for d in range(1, N_DEV):
            g, gn = g + pb[d], gn + nb_[d]
        return g, gn

    gsum, gnorm = _ew(sum_fn, [(small_all, (N_DEV, rows, 128), lambda cc, rr: (0, 0, 0)),
                               (norm_all, (N_DEV, 8, D), lambda cc, rr: (0, 0, 0))],
                      [(SDS((rows, 128), F32), (rows, 128), lambda cc, rr: (0, 0), None),
                       (SDS((8, D), F32), (8, D), lambda cc, rr: (0, 0), None)], (1, 1), "sum_small_grads")
    gflat = gsum.reshape(-1)
    gsmall = {n: local_part(n, gflat[offs[i]:offs[i + 1]].reshape(sgp[n].shape)) for i, n in enumerate(early_names)}
    gsmall["attn_norm_g"] = gnorm[0]
    as2d = lambda a: a.reshape(1, -1) if a.ndim == 1 else a.reshape(a.shape[0], -1)
    n_small = len(small_names)

    def adam_small(*refs):
        for i in range(n_small):
            g_ref, w_ref, m_ref, v_ref = refs[4 * i:4 * i + 4]
            outs = refs[4 * n_small + 3 * i:4 * n_small + 3 * i + 3]
            for o_ref, val in zip(outs, _adam_math(g_ref[...], w_ref[...], m_ref[...], v_ref[...])):
                o_ref[...] = val

    small_ins = [as2d(t[n]) for n in small_names for t in (gsmall, wv, mv, vv)]
    small_outs = pl.pallas_call(adam_small, name="adam_small",
                                out_shape=[SDS(as2d(wv[n]).shape, F32) for n in small_names for _ in range(3)])(*small_ins)
    for i, n in enumerate(small_names):
        sd, sm, sv = (t.reshape(wv[n].shape) for t in small_outs[3 * i:3 * i + 3])
        outs_g[n], outs_d[n], outs_m[n], outs_v[n] = gsmall[n], sd, sm, sv

    lead = lambda n, a: a if args[n].ndim == 1 else a[None]
    grad_x = grad_x[None]
    return (loss, grad_x, *[lead(n, outs_g[n]) for n in _WEIGHTS], *[lead(n, outs_d[n]) for n in _WEIGHTS],
            *[lead(n, outs_m[n]) for n in _WEIGHTS], *[lead(n, outs_v[n]) for n in _WEIGHTS])
```

```python
import functools
import math

import jax
import jax.numpy as jnp
from jax import lax
from jax.experimental import pallas as pl
from jax.experimental.pallas import tpu as pltpu

F32 = jnp.float32
BF16 = jnp.bfloat16
SDS = jax.ShapeDtypeStruct

N_DEV = 8
D = 2048
NQ, NKV, HD = 16, 2, 64
AW = NQ * HD
BLK = 128
SW, G, H, P = 512, 32, 16, 64
NS = G * P
DFF = 5632
INC = AW + 2 * NKV * HD + SW + 2 * D
C_K, C_U, C_PAD = AW, AW + 2 * NKV * HD, AW + 2 * NKV * HD + SW
C_GA, C_GS, INP = D, 2 * D, 3 * D
RMS_EPS = 1e-6
NEG_BIG = -1e30
ADAM_LR, ADAM_B1, ADAM_B2, ADAM_EPS, ADAM_WD, ADAM_STEP = 0.001, 0.9, 0.999, 1e-08, 0.01, 10
NSEG = 8
VMEM_CAP_MB = 60
MESH_AXES = ("x", "y", "c")


def _pad_cols(a):
    zeros = jnp.zeros(a.shape[:-1] + (C_GA - C_PAD,), a.dtype)
    return jnp.concatenate([a[..., :C_PAD], zeros, a[..., C_PAD:]], axis=-1)


def _unpad_cols(a):
    return jnp.concatenate([a[..., :C_PAD], a[..., C_GA:]], axis=-1)


def _cparams(sem, vmem_mb):
    return pltpu.CompilerParams(dimension_semantics=sem, vmem_limit_bytes=min(int(vmem_mb), VMEM_CAP_MB) << 20)


LANES = 128


def _tile(dim, pref):
    if dim <= pref:
        return dim
    for t in range(pref - pref % LANES, 0, -LANES):
        if dim % t == 0:
            return t
    raise ValueError(f"no tile for {dim}")


def _mm(a, b, *, ta=False, tb=False, bias=None, res=None, out_dtype=F32, tm=1024, tn=1024, tk=3072, name,
        a2=None, b2=None, extras=(), epilogue=None, outs=None, ep_cols=None, stack_out=False, n_cols=None,
        b2_col_off=0):
    m, k = (a.shape[1], a.shape[0]) if ta else a.shape
    n = n_cols or (b.shape[0] if tb else b.shape[1])
    assert (b.shape[1] if tb else b.shape[0]) == k, (a.shape, b.shape, ta, tb)
    tm, tn, tk = _tile(m, tm), _tile(n, tn), _tile(k, tk)
    nk = k // tk
    dims = (((0 if ta else 1,), (1 if tb else 0,)), ((), ()))
    has_bias, has_res, has_b2 = bias is not None, res is not None, b2 is not None
    has_a2 = a2 is not None
    assert not (has_b2 and (nk > 1 or ta or tb)) and not (has_a2 and not has_b2)
    if epilogue is None:
        outs = [(SDS((n // tn, m, tn) if stack_out else (m, n), out_dtype), "tile")]
    n_ex, n_out = len(extras), len(outs)
    tcn = tn if (epilogue is None or nk > 1 or ep_cols is None) else _tile(tn, ep_cols)

    def body(*refs):
        a_ref, b_ref = refs[0], refs[1]
        pos = 2
        a2_ref = refs[pos] if has_a2 else a_ref
        pos += has_a2
        b2_ref = refs[pos] if has_b2 else None
        pos += has_b2
        bias_ref = refs[pos] if has_bias else None
        pos += has_bias
        res_ref = refs[pos] if has_res else None
        pos += has_res
        ex_refs = refs[pos:pos + n_ex]
        o_refs = refs[pos + n_ex:pos + n_ex + n_out]
        i = pl.program_id(1)

        def product(rhs_ref, cols=None, lhs=None):
            rhs = rhs_ref[...] if cols is None else (rhs_ref[cols, :] if tb else rhs_ref[:, cols])
            lhs = a_ref[...].astype(BF16) if lhs is None else lhs
            return lax.dot_general(lhs, rhs.astype(BF16), dims, preferred_element_type=F32)

        def finish(r, cols):
            if has_bias:
                r = r + bias_ref[:, cols]
            if has_res:
                r = r + res_ref[:, cols].astype(F32)
            if epilogue is None:
                o_refs[0][:, cols] = r.astype(o_refs[0].dtype)
                return
            r2 = None
            if has_b2:
                r2 = jnp.dot(a2_ref[...].astype(BF16), b2_ref[:, cols].astype(BF16), preferred_element_type=F32)
            vals = epilogue(i, cols, r, r2, *ex_refs)
            for o_ref, v, (_, kind) in zip(o_refs, vals, outs):
                if kind == "tile":
                    o_ref[:, cols] = v.astype(o_ref.dtype)
                else:
                    @pl.when(i == 0)
                    def _(o_ref=o_ref, v=v):
                        o_ref[:, cols] = v.astype(o_ref.dtype)

                    @pl.when(i > 0)
                    def _(o_ref=o_ref, v=v):
                        o_ref[:, cols] += v.astype(o_ref.dtype)

        if nk == 1:
            lhs = a_ref[...].astype(BF16)
            for c0 in range(0, tn, tcn):
                cols = pl.ds(c0, tcn)
                finish(product(b_ref, cols, lhs), cols)
            return
        whole = pl.ds(0, tn)
        acc_ref = refs[-1]
        kk = pl.program_id(2)

        @pl.when(kk == 0)
        def _():
            acc_ref[...] = product(b_ref)

        @pl.when(jnp.logical_and(kk > 0, kk < nk - 1))
        def _():
            acc_ref[...] += product(b_ref)

        @pl.when(kk == nk - 1)
        def _():
            finish(acc_ref[...] + product(b_ref), whole)

    b_spec = pl.BlockSpec((tn, tk), lambda j, i, kk: (j, kk)) if tb else pl.BlockSpec((tk, tn), lambda j, i, kk: (kk, j))
    ins = [a, b]
    in_specs = [pl.BlockSpec((tk, tm), lambda j, i, kk: (kk, i)) if ta else pl.BlockSpec((tm, tk), lambda j, i, kk: (i, kk)),
                b_spec]
    tile_spec = pl.BlockSpec((tm, tn), lambda j, i, kk: (i, j))
    byt = 2 * tm * tk * a.dtype.itemsize + 2 * tk * tn * b.dtype.itemsize
    byt += (2 + has_b2) * 4 * tm * tn
    if has_a2:
        ins.append(a2)
        in_specs.append(pl.BlockSpec((tm, a2.shape[1]), lambda j, i, kk: (i, 0)))
        byt += 2 * tm * a2.shape[1] * a2.dtype.itemsize
    if has_b2:
        ins.append(b2)
        in_specs.append(pl.BlockSpec((b2.shape[0], tn), lambda j, i, kk: (0, j + b2_col_off)))
        byt += 2 * b2.shape[0] * tn * b2.dtype.itemsize
    if has_bias:
        ins.append(bias)
        in_specs.append(pl.BlockSpec((1, tn), lambda j, i, kk: (0, j)))
    if has_res:
        ins.append(res)
        in_specs.append(tile_spec)
        byt += 2 * tm * tn * res.dtype.itemsize
    for arr, kind, arg in extras:
        ins.append(arr)
        if kind == "tile":
            in_specs.append(pl.BlockSpec((tm, tn), lambda j, i, kk, arg=arg: (i, j + arg)))
            byt += 2 * tm * tn * arr.dtype.itemsize + 4 * tm * tn
        elif kind == "col":
            in_specs.append(pl.BlockSpec((arr.shape[0], tn), lambda j, i, kk, arg=arg: (0, j + arg)))
        else:
            in_specs.append(pl.BlockSpec(arg[0], lambda j, i, kk, im=arg[1]: im(j, i)))
    out_specs = []
    for sds, kind in outs:
        if kind == "tile":
            out_specs.append(pl.BlockSpec((None, tm, tn), lambda j, i, kk: (j, i, 0)) if stack_out else tile_spec)
            byt += 2 * tm * tn * jnp.dtype(sds.dtype).itemsize
        else:
            out_specs.append(pl.BlockSpec((sds.shape[0], tn), lambda j, i, kk: (0, j)))
    res_ = pl.pallas_call(
        body, out_shape=tuple(o[0] for o in outs), grid=(n // tn, m // tm, nk), in_specs=in_specs,
        out_specs=tuple(out_specs), scratch_shapes=[pltpu.VMEM((tm, tn), F32)] if nk > 1 else [], name=name,
        compiler_params=_cparams(("arbitrary", "arbitrary", "arbitrary"), byt / 2**20 + (8 if epilogue is None else 20)),
    )(*ins)
    return res_[0] if epilogue is None else res_


def _ew(fn, ins, outs, grid, name, vmem_mb=40, into=None):
    n_in = len(ins)
    accs = [o[3] for o in outs]

    def body(*refs):
        c, r = pl.program_id(0), pl.program_id(1)
        vals = fn(c, r, *[ref[...].astype(F32) for ref in refs[:n_in]])
        for o_ref, v, acc in zip(refs[n_in + (into is not None):], vals, accs):
            if acc is None:
                o_ref[...] = v.astype(o_ref.dtype)
            else:
                first = (r == 0) if acc == "r" else jnp.logical_and(r == 0, c == 0)

                @pl.when(first)
                def _(o_ref=o_ref, v=v):
                    o_ref[...] = v.astype(o_ref.dtype)

                @pl.when(jnp.logical_not(first))
                def _(o_ref=o_ref, v=v):
                    o_ref[...] += v.astype(o_ref.dtype)

    in_specs = [pl.BlockSpec(bs, im) for _, bs, im in ins]
    args = [a for a, _, _ in ins]
    if into is not None:
        in_specs.append(pl.BlockSpec(memory_space=pl.ANY))
        args.append(into)
    res = pl.pallas_call(
        body, out_shape=tuple(o[0] for o in outs), grid=grid, in_specs=in_specs,
        out_specs=tuple(pl.BlockSpec(bs, im) for _, bs, im, _ in outs), name=name,
        input_output_aliases={} if into is None else {n_in: 0},
        compiler_params=_cparams(("arbitrary", "arbitrary"), vmem_mb),
    )(*args)
    return res


def _rc(tm, tc, coff=0):
    return (tm, tc), (lambda c, r: (r, c + coff))


def _col(rows, tc, coff=0):
    return (rows, tc), (lambda c, r: (0, c + coff))


def _gelu(x):
    return 0.5 * x * (1.0 + lax.erf(x * (2.0 ** -0.5)))


def _gelu_and_grad(x):
    cdf = 0.5 * (1.0 + lax.erf(x * (2.0 ** -0.5)))
    return x * cdf, cdf + x * jnp.exp(-0.5 * x * x) * (1.0 / math.sqrt(2.0 * math.pi))


def _gelu_grad(x):
    return _gelu_and_grad(x)[1]


def _sigmoid(x):
    return 1.0 / (1.0 + jnp.exp(-x))


def _shift_rows(x, halo, s):
    rolled = pltpu.roll(x, s, 0)
    row8 = lax.broadcasted_iota(jnp.int32, halo.shape, 0)
    head = jnp.where(row8 < s, pltpu.roll(halo, s, 0), rolled[0:8])
    return jnp.concatenate([head, rolled[8:]], axis=0)


def _shift_rows_up(x, halo, s):
    tm = x.shape[0]
    rolled = pltpu.roll(x, tm - s, 0)
    row8 = lax.broadcasted_iota(jnp.int32, halo.shape, 0)
    tail = jnp.where(row8 >= 8 - s, pltpu.roll(halo, 8 - s, 0), rolled[tm - 8:])
    return jnp.concatenate([rolled[:tm - 8], tail], axis=0)


def _rmsnorm_fwd(x, g, name, tm=512):
    L = x.shape[0]

    def fn(c, r, xb, gb):
        rstd = lax.rsqrt(jnp.mean(xb * xb, axis=-1, keepdims=True) + RMS_EPS)
        return ((xb * rstd) * gb,)

    return _ew(fn, [(x, *_rc(tm, D)), (g, *_col(1, D))], [(SDS((L, D), BF16), *_rc(tm, D), None)], (1, L // tm), name)[0]


def _rmsnorm_bwd(dh, x, g, dres, name, tm=512):
    L = x.shape[0]

    def fn(c, r, dhb, xb, gb, drb):
        rstd = lax.rsqrt(jnp.mean(xb * xb, axis=-1, keepdims=True) + RMS_EPS)
        y = xb * rstd
        dy = dhb * gb
        dx = rstd * (dy - y * jnp.mean(dy * y, axis=-1, keepdims=True))
        return drb + dx, jnp.sum(dhb * y, axis=0, keepdims=True)

    return _ew(fn, [(dh, *_rc(tm, D)), (x, *_rc(tm, D)), (g, *_col(1, D)), (dres, *_rc(tm, D))],
               [(SDS((L, D), F32), *_rc(tm, D), None), (SDS((1, D), F32), *_col(1, D), "all")], (1, L // tm), name,
               vmem_mb=56)


def _final_loss(x2, g, tgt, name, tm=512):
    L = x2.shape[0]

    def fn(c, r, xb, gb, tb):
        rstd = lax.rsqrt(jnp.mean(xb * xb, axis=-1, keepdims=True) + RMS_EPS)
        y = xb * rstd
        err = y * gb - tb
        dout = err * (1.0 / D)
        dy = dout * gb
        dx = rstd * (dy - y * jnp.mean(dy * y, axis=-1, keepdims=True))
        return dx, dx, jnp.sum(err * err, axis=0, keepdims=True) * (0.5 / D), jnp.sum(dout * y, axis=0, keepdims=True)

    return _ew(fn, [(x2, *_rc(tm, D)), (g, *_col(1, D)), (tgt, *_rc(tm, D))],
               [(SDS((L, D), F32), *_rc(tm, D), None), (SDS((L, D), BF16), *_rc(tm, D), None),
                (SDS((1, D), F32), *_col(1, D), "all"),
                (SDS((1, D), F32), *_col(1, D), "all")], (1, L // tm), name, vmem_mb=56)


def _slope(h):
    return 2.0 ** (-8.0 * (h + 1) / NQ)


def _attn_bias():
    qi = lax.broadcasted_iota(jnp.int32, (BLK, 2 * BLK), 0)
    si = lax.broadcasted_iota(jnp.int32, (BLK, 2 * BLK), 1)
    dist = qi + BLK - si
    band = (dist >= 0) & (dist < BLK)
    slopes = jnp.asarray([_slope(h) for h in range(NQ)], F32)[:, None, None]
    alibi = -slopes * dist.astype(F32)[None]
    return jnp.stack([jnp.where((band & (si >= BLK))[None], alibi, NEG_BIG), jnp.where(band[None], alibi, NEG_BIG)])


def _attn_kv(kvc, kvp):
    kv = jnp.concatenate([kvp, kvc], axis=0).astype(F32)
    lo = lax.broadcasted_iota(jnp.int32, (2 * BLK, 128), 1) < HD

    def halves(t):
        tr = pltpu.roll(t, HD, 1)
        z = jnp.zeros_like(t)
        return {(0, 0): jnp.where(lo, t, z).astype(BF16), (0, 1): jnp.where(lo, z, tr).astype(BF16),
                (1, 0): jnp.where(lo, tr, z).astype(BF16), (1, 1): jnp.where(lo, z, t).astype(BF16)}

    return halves(kv[:, :128]), halves(kv[:, 128:])


_NT = (((1,), (1,)), ((), ()))
_TN = (((0,), (0,)), ((), ()))
_ATTN_SPECS = [pl.BlockSpec(memory_space=pltpu.SMEM),
               pl.BlockSpec((None, NQ, BLK, 2 * BLK), lambda n: (jnp.minimum(n, 1), 0, 0, 0)),
               pl.BlockSpec((BLK, AW), lambda n: (n, 0)),
               pl.BlockSpec((BLK, 256), lambda n: (n, C_K // 256)),
               pl.BlockSpec((BLK, 256), lambda n: (jnp.maximum(n - 1, 0), C_K // 256))]


def _attn_scores(q_ref, bias_ref, kmat, sc_ref):
    for j in range(NQ // 2):
        qs = q_ref[:, 128 * j:128 * (j + 1)] * (HD ** -0.5)
        for e in range(2):
            h = 2 * j + e
            sc_ref[h] = lax.dot_general(qs, kmat[(j // (NQ // 4), e)], _NT, preferred_element_type=F32) + bias_ref[h]


def _softmax_with_sink(s, sink):
    m = jnp.maximum(jnp.max(s, axis=-1, keepdims=True), sink)
    p = jnp.exp(s - m)
    esink = jnp.exp(sink - m)
    den = jnp.sum(p, axis=-1, keepdims=True) + esink
    return p / den, esink / den


def _attn_fwd(projb, sinks, bias, name):
    L = projb.shape[0]

    def body(s_ref, bias_ref, q_ref, kvc_ref, kvp_ref, o_ref, sc_ref, pr_ref):
        kmat, vmat = _attn_kv(kvc_ref[...], kvp_ref[...])
        _attn_scores(q_ref, bias_ref, kmat, sc_ref)
        for h in range(NQ):
            pr_ref[h] = _softmax_with_sink(sc_ref[h], s_ref[0, h])[0].astype(BF16)
        for j in range(NQ // 2):
            g = j // (NQ // 4)
            acc = jnp.dot(pr_ref[2 * j], vmat[(g, 0)], preferred_element_type=F32)
            acc = acc + jnp.dot(pr_ref[2 * j + 1], vmat[(g, 1)], preferred_element_type=F32)
            o_ref[:, 128 * j:128 * (j + 1)] = acc.astype(BF16)

    return pl.pallas_call(
        body, out_shape=SDS((L, AW), BF16), grid=(L // BLK,), in_specs=_ATTN_SPECS,
        out_specs=pl.BlockSpec((BLK, AW), lambda n: (n, 0)), name=name,
        scratch_shapes=[pltpu.VMEM((NQ, BLK, 2 * BLK), F32), pltpu.VMEM((NQ, BLK, 2 * BLK), BF16)],
        compiler_params=_cparams(("arbitrary",), 32),
    )(sinks, bias, projb, projb, projb)


def _attn_bwd(projb, sinks, bias, dattn, name):
    L = projb.shape[0]

    def body(s_ref, bias_ref, q_ref, kvc_ref, kvp_ref, do_ref, dq_ref, dcur_ref, dprev_ref, dsink_ref,
             sc_ref, dp_ref, ds_ref, pr_ref):
        n = pl.program_id(0)
        kmat, vmat = _attn_kv(kvc_ref[...], kvp_ref[...])
        _attn_scores(q_ref, bias_ref, kmat, sc_ref)
        for h in range(NQ):
            j, e = h // 2, h % 2
            dp_ref[h] = lax.dot_general(do_ref[:, 128 * j:128 * (j + 1)], vmat[(j // (NQ // 4), e)], _NT,
                                        preferred_element_type=F32)
        lane = lax.broadcasted_iota(jnp.int32, (1, 128), 1)
        dsv = jnp.zeros((1, 128), F32)
        for h in range(NQ):
            p, psink = _softmax_with_sink(sc_ref[h], s_ref[0, h])
            dp = dp_ref[h]
            drow = jnp.sum(p * dp, axis=-1, keepdims=True)
            ds_ref[h] = (p * (dp - drow)).astype(BF16)
            pr_ref[h] = p.astype(BF16)
            dsv = dsv + jnp.where(lane == h, -jnp.sum(psink * drow, axis=0, keepdims=True), 0.0)
        lo128 = lax.broadcasted_iota(jnp.int32, (BLK, 128), 1) < HD
        dk = [jnp.zeros((2 * BLK, 128), F32) for _ in range(NKV)]
        dv = [jnp.zeros((2 * BLK, 128), F32) for _ in range(NKV)]
        for j in range(NQ // 2):
            g = j // (NQ // 4)
            qs = q_ref[:, 128 * j:128 * (j + 1)] * (HD ** -0.5)
            dop = do_ref[:, 128 * j:128 * (j + 1)]
            zb = jnp.zeros_like(qs)
            dqp = jnp.zeros((BLK, 128), F32)
            for e in range(2):
                h = 2 * j + e
                half = lo128 if e == 0 else jnp.logical_not(lo128)
                dqp = dqp + jnp.dot(ds_ref[h], kmat[(g, e)], preferred_element_type=F32)
                dk[g] = dk[g] + lax.dot_general(ds_ref[h], jnp.where(half, qs, zb), _TN, preferred_element_type=F32)
                dv[g] = dv[g] + lax.dot_general(pr_ref[h], jnp.where(half, dop, zb), _TN, preferred_element_type=F32)
            dq_ref[:, 128 * j:128 * (j + 1)] = (dqp * (HD ** -0.5)).astype(BF16)
        lo256 = lax.broadcasted_iota(jnp.int32, (2 * BLK, 128), 1) < HD
        tot = [t + pltpu.roll(t, HD, 1) for t in (dk[0], dk[1], dv[0], dv[1])]
        dkv = jnp.concatenate([jnp.where(lo256, tot[0], tot[1]), jnp.where(lo256, tot[2], tot[3])], axis=1)
        dprev_ref[...] = dkv[:BLK]
        dcur_ref[...] = dkv[BLK:]

        @pl.when(n == 0)
        def _():
            dsink_ref[...] = dsv

        @pl.when(n > 0)
        def _():
            dsink_ref[...] += dsv

    tile = (NQ, BLK, 2 * BLK)
    return pl.pallas_call(
        body, out_shape=(SDS((L, AW), BF16), SDS((L, 256), F32), SDS((L, 256), F32), SDS((1, 128), F32)), grid=(L // BLK,),
        in_specs=_ATTN_SPECS + [pl.BlockSpec((BLK, AW), lambda n: (n, 0))],
        out_specs=(pl.BlockSpec((BLK, AW), lambda n: (n, 0)), pl.BlockSpec((BLK, 256), lambda n: (n, 0)),
                   pl.BlockSpec((BLK, 256), lambda n: (n, 0)), pl.BlockSpec((1, 128), lambda n: (0, 0))),
        scratch_shapes=[pltpu.VMEM(tile, F32), pltpu.VMEM(tile, F32), pltpu.VMEM(tile, BF16), pltpu.VMEM(tile, BF16)],
        name=name, compiler_params=_cparams(("arbitrary",), 40),
    )(sinks, bias, projb, projb, projb, dattn)


def _disc(a_re, a_im, logdt, b_re, b_im):
    dt = jnp.exp(logdt)
    mag = jnp.exp(a_re * dt)
    ab_re = mag * jnp.cos(a_im * dt)
    ab_im = mag * jnp.sin(a_im * dt)
    nr = ab_re - 1.0
    ni = ab_im
    den = a_re * a_re + a_im * a_im
    z_re = (nr * a_re + ni * a_im) / den
    z_im = (ni * a_re - nr * a_im) / den
    return ab_re, ab_im, z_re * b_re - z_im * b_im, z_re * b_im + z_im * b_re


def _group_mask():
    row = lax.broadcasted_iota(jnp.int32, (SW, NS), 0) // H
    col = lax.broadcasted_iota(jnp.int32, (SW, NS), 1) // P
    return row == col


def _block_diag(re, im):
    mask = _group_mask()
    z = jnp.zeros((SW, NS), F32)
    return jnp.concatenate([jnp.where(mask, jnp.tile(re, (G, 1)), z), jnp.where(mask, jnp.tile(im, (G, 1)), z)], axis=1)


def _block_diag_t(big):
    mask = _group_mask()
    z = jnp.zeros((SW, NS), F32)
    re = jnp.sum(jnp.where(mask, big[:, :NS], z).reshape(G, H, NS), axis=0)
    im = jnp.sum(jnp.where(mask, big[:, NS:], z).reshape(G, H, NS), axis=0)
    return re, im


def _ssm_prep(a_re, a_im, logdt, b_re, b_im, c_re, c_im, name):
    def body(are, aim, ldt, bre, bim, cre, cim, ab_ref, bm_ref, cm_ref):
        ab_re, ab_im, bb_re, bb_im = _disc(are[...], aim[...], ldt[...], bre[...], bim[...])
        ab_ref[...] = jnp.concatenate([ab_re, ab_im], axis=1)
        bm_ref[...] = _block_diag(bb_re, bb_im).astype(BF16)
        cm_ref[...] = _block_diag(cre[...], -cim[...]).astype(BF16)

    return pl.pallas_call(body, out_shape=(SDS((1, 2 * NS), F32), SDS((SW, 2 * NS), BF16), SDS((SW, 2 * NS), BF16)),
                          name=name, compiler_params=pltpu.CompilerParams(vmem_limit_bytes=48 << 20),
                          )(a_re, a_im, logdt, b_re, b_im, c_re, c_im)


def _ssm_param_bwd(a_re, a_im, logdt, b_re, b_im, dab8, dbm, dcm, name):
    def body(are, aim, ldt, bre, bim, dab_ref, dbm_ref, dcm_ref, o_are, o_aim, o_ldt, o_bre, o_bim, o_cre, o_cim):
        dab = jnp.sum(dab_ref[...], axis=0, keepdims=True)
        dbb_re, dbb_im = _block_diag_t(dbm_ref[...])
        _, vjp = jax.vjp(_disc, are[...], aim[...], ldt[...], bre[...], bim[...])
        d_are, d_aim, d_ldt, d_bre, d_bim = vjp((dab[:, :NS], dab[:, NS:], dbb_re, dbb_im))
        o_are[...], o_aim[...], o_ldt[...], o_bre[...], o_bim[...] = d_are, d_aim, d_ldt, d_bre, d_bim
        dc_re, dc_imn = _block_diag_t(dcm_ref[...])
        o_cre[...] = dc_re
        o_cim[...] = -dc_imn

    v1, vh = SDS((1, NS), F32), SDS((H, NS), F32)
    return pl.pallas_call(body, out_shape=(v1, v1, v1, vh, vh, vh, vh), name=name,
                          compiler_params=pltpu.CompilerParams(vmem_limit_bytes=56 << 20),
                          )(a_re, a_im, logdt, b_re, b_im, dab8, dbm, dcm)


def _ssm_scan(src, wmat, ab, *, reverse, ends=None, xs=None, init=None, wproj=None, name, tk=32):
    L = src.shape[0]
    rows = NSEG * tk
    nch = L // rows
    seg_len = L // NSEG
    n_sq = int(math.log2(seg_len))
    assert 2 ** n_sq == seg_len and L % rows == 0
    first_pass = ends is None
    with_dab = (not first_pass) and reverse
    with_proj = wproj is not None
    assert not (with_proj and first_pass)
    slab = 512
    n_slab = NS // slab

    def body(*refs):
        src_ref, w_ref, ab_ref = refs[:3]
        pos = 3
        if not first_pass:
            ends_ref = refs[pos]
            pos += 1
        if with_dab:
            xs_ref, xsh_ref, init_ref = refs[pos:pos + 3]
            pos += 3
        if with_proj:
            wproj_ref = refs[pos]
            pos += 1
        if first_pass:
            (e_ref,) = refs[pos:pos + 1]
            pos += 1
        else:
            st_out_ref, aux_ref = refs[pos:pos + 2]
            pos += 2
        if with_proj:
            proj_ref = refs[pos]
            pos += 1
        buf_ref, st_ref = refs[pos:pos + 2]
        i = pl.program_id(0)
        a_re = ab_ref[:, :NS]
        a_im = -ab_ref[:, NS:] if reverse else ab_ref[:, NS:]

        @pl.when(i == 0)
        def _():
            if first_pass:
                st_ref[...] = jnp.zeros_like(st_ref)
            else:
                pr, pi = a_re, a_im
                for _ in range(n_sq):
                    pr, pi = pr * pr - pi * pi, 2.0 * pr * pi
                zr = jnp.zeros((1, NS), F32)
                cr, ci = zr, zr
                order = list(range(NSEG - 1, -1, -1)) if reverse else list(range(NSEG))
                st_ref[order[0]:order[0] + 1, :] = jnp.zeros((1, 2 * NS), F32)
                for jprev, j in zip(order[:-1], order[1:]):
                    er, ei = ends_ref[jprev:jprev + 1, :NS], ends_ref[jprev:jprev + 1, NS:]
                    cr, ci = er + pr * cr - pi * ci, ei + pr * ci + pi * cr
                    st_ref[j:j + 1, :NS] = cr
                    st_ref[j:j + 1, NS:] = ci
                if not reverse:
                    aux_ref[...] = st_ref[...]
                else:
                    aux_ref[...] = jnp.zeros_like(aux_ref)

        buf_ref[...] = jnp.dot(src_ref[...].astype(BF16), w_ref[...], preferred_element_type=F32)

        for s in range(n_slab):
            re_sl, im_sl = pl.ds(s * slab, slab), pl.ds(NS + s * slab, slab)
            ar = jnp.broadcast_to(a_re[:, s * slab:(s + 1) * slab], (NSEG, slab))
            ai = jnp.broadcast_to(a_im[:, s * slab:(s + 1) * slab], (NSEG, slab))

            def step(t, carry, re_sl=re_sl, im_sl=im_sl, ar=ar, ai=ai):
                k = (tk - 1 - t) if reverse else t
                r0 = pl.multiple_of(k * NSEG, NSEG)
                xr, xi = carry[0], carry[1]
                nr = ar * xr - ai * xi + buf_ref[pl.ds(r0, NSEG), re_sl]
                ni = ar * xi + ai * xr + buf_ref[pl.ds(r0, NSEG), im_sl]
                if not first_pass:
                    buf_ref[pl.ds(r0, NSEG), re_sl] = nr
                    buf_ref[pl.ds(r0, NSEG), im_sl] = ni
                if not with_dab:
                    return nr, ni
                rp = pl.multiple_of((k - 1) * NSEG, NSEG)
                xpr, xpi = xs_ref[pl.ds(rp, NSEG), re_sl], xs_ref[pl.ds(rp, NSEG), im_sl]
                return nr, ni, carry[2] + nr * xpr + ni * xpi, carry[3] + ni * xpr - nr * xpi

            carry = (st_ref[:, re_sl], st_ref[:, im_sl])
            if with_dab:
                z = jnp.zeros((NSEG, slab), F32)
                carry = lax.fori_loop(0, tk - 1, step, carry + (z, z))
                xr, xi, dr, di = carry
                nr = ar * xr - ai * xi + buf_ref[pl.ds(0, NSEG), re_sl]
                ni = ar * xi + ai * xr + buf_ref[pl.ds(0, NSEG), im_sl]
                buf_ref[pl.ds(0, NSEG), re_sl] = nr
                buf_ref[pl.ds(0, NSEG), im_sl] = ni
                at_start = i == nch - 1
                xpr = jnp.where(at_start, init_ref[:, re_sl], xsh_ref[:, re_sl])
                xpi = jnp.where(at_start, init_ref[:, im_sl], xsh_ref[:, im_sl])
                aux_ref[:, re_sl] += dr + nr * xpr + ni * xpi
                aux_ref[:, im_sl] += di + ni * xpr - nr * xpi
                carry = (nr, ni)
            else:
                carry = lax.fori_loop(0, tk, step, carry)
            st_ref[:, re_sl] = carry[0]
            st_ref[:, im_sl] = carry[1]

        if first_pass:
            @pl.when(i == nch - 1)
            def _():
                e_ref[...] = st_ref[...]
        else:
            st_out_ref[...] = buf_ref[...].astype(st_out_ref.dtype)
            if with_proj:
                proj_ref[...] = lax.dot_general(buf_ref[...].astype(BF16), wproj_ref[...], _NT, preferred_element_type=F32)

    chunk = (lambda i: (nch - 1 - i, 0)) if reverse else (lambda i: (i, 0))
    whole = lambda i: (0, 0)
    ins = [src, wmat, ab]
    once = pl.Buffered(1)
    in_specs = [pl.BlockSpec((rows, SW), chunk), pl.BlockSpec((SW, 2 * NS), whole, pipeline_mode=once),
                pl.BlockSpec((1, 2 * NS), whole)]
    small = SDS((NSEG, 2 * NS), F32)
    small_spec = pl.BlockSpec((NSEG, 2 * NS), whole)
    if not first_pass:
        ins.append(ends)
        in_specs.append(small_spec)
    if with_dab:
        ins += [xs, xs, init]
        in_specs += [pl.BlockSpec((rows, 2 * NS), chunk),
                     pl.BlockSpec((NSEG, 2 * NS), lambda i: (jnp.maximum((nch - 1 - i) * tk - 1, 0), 0)),
                     small_spec]
    if with_proj:
        ins.append(wproj)
        in_specs.append(pl.BlockSpec((SW, 2 * NS), whole, pipeline_mode=once))
    if first_pass:
        out_shape, out_specs = small, small_spec
    else:
        out_shape = (SDS((L, 2 * NS), BF16 if reverse else F32), small)
        out_specs = (pl.BlockSpec((rows, 2 * NS), chunk), small_spec)
        if with_proj:
            out_shape += (SDS((L, SW), F32),)
            out_specs += (pl.BlockSpec((rows, SW), chunk),)
    return pl.pallas_call(
        body, out_shape=out_shape, grid=(nch,), in_specs=in_specs, out_specs=out_specs,
        scratch_shapes=[pltpu.VMEM((rows, 2 * NS), F32), pltpu.VMEM((NSEG, 2 * NS), F32)], name=name,
        compiler_params=_cparams(("arbitrary",), 56),
    )(*ins)


def _to_segments(a):
    L, c = a.shape
    return a.reshape(NSEG, L // NSEG, c).transpose(1, 0, 2).reshape(L, c)


def _from_segments(a):
    L, c = a.shape
    return a.reshape(L // NSEG, NSEG, c).transpose(1, 0, 2).reshape(L, c)


def _peer(x, y, c, m):
    return ((1 - x) if (m >> 2) & 1 else x, (1 - y) if (m >> 1) & 1 else y, (1 - c) if m & 1 else c)


def _dev_index(p):
    return 4 * p[0] + 2 * p[1] + p[2]


def _exchange(arrs, scatter, name):
    n = len(arrs)

    def body(*refs):
        ins, outs = refs[:n], refs[n:2 * n]
        send_sems, recv_sems, loc_sems = refs[2 * n:]
        x, y, c = lax.axis_index("x"), lax.axis_index("y"), lax.axis_index("c")
        me = _dev_index((x, y, c))

        def src(w, to):
            return ins[w].at[to] if scatter else ins[w]

        def local(w):
            return pltpu.make_async_copy(src(w, me), outs[w].at[me], loc_sems.at[w])

        def remote(w, m):
            peer = _peer(x, y, c, m)
            return pltpu.make_async_remote_copy(src_ref=src(w, _dev_index(peer)), dst_ref=outs[w].at[me],
                                                send_sem=send_sems.at[w, m - 1], recv_sem=recv_sems.at[w, m - 1],
                                                device_id=peer, device_id_type=pl.DeviceIdType.MESH)

        def arrival(w, m):
            peer = _peer(x, y, c, m)
            return pltpu.make_async_remote_copy(src_ref=src(w, me), dst_ref=outs[w].at[_dev_index(peer)],
                                                send_sem=send_sems.at[w, m - 1], recv_sem=recv_sems.at[w, m - 1],
                                                device_id=peer, device_id_type=pl.DeviceIdType.MESH)

        for w in range(n):
            local(w).start()
        for w in range(n):
            for m in range(1, N_DEV):
                remote(w, m).start()
        for w in range(n):
            for m in range(1, N_DEV):
                arrival(w, m).wait_recv()
        for w in range(n):
            for m in range(1, N_DEV):
                remote(w, m).wait_send()
        for w in range(n):
            local(w).wait()

    anyspec = pl.BlockSpec(memory_space=pl.ANY)
    out_shape = tuple(SDS(a.shape if scatter else (N_DEV,) + a.shape, a.dtype) for a in arrs)
    return pl.pallas_call(
        body, out_shape=out_shape, in_specs=[anyspec] * n, out_specs=tuple([anyspec] * n),
        scratch_shapes=[pltpu.SemaphoreType.DMA((n, N_DEV - 1)), pltpu.SemaphoreType.DMA((n, N_DEV - 1)),
                        pltpu.SemaphoreType.DMA((n,))],
        name=name, compiler_params=pltpu.CompilerParams(has_side_effects=True),
    )(*arrs)


_HBM = pl.BlockSpec(memory_space=pltpu.HBM)
_SEM = pl.BlockSpec(memory_space=pltpu.SEMAPHORE)
_EFFECT = pltpu.SideEffectType.DATAFLOW_SIDE_EFFECTING


def _sem_index(w, m):
    return w * (N_DEV - 1) + m - 1


_ALL_MASKS = tuple(range(1, N_DEV))
_CHIP_MASKS = (2, 4, 6)
_FIRST_HOP_MASKS = (1,) + _CHIP_MASKS


def _exchange_start(arrs, scatter, name, masks=_ALL_MASKS):
    n = len(arrs)
    lands = [lax.empty(a.shape if scatter else (N_DEV,) + a.shape, a.dtype) for a in arrs]

    def body(*refs):
        ins, zones = refs[:n], refs[n:2 * n]
        send_sems, recv_sems = refs[2 * n], refs[2 * n + 1]
        token = refs[-1]
        x, y, c = lax.axis_index("x"), lax.axis_index("y"), lax.axis_index("c")
        me = _dev_index((x, y, c))
        for w in range(n):
            for m in masks:
                peer = _peer(x, y, c, m)
                pltpu.make_async_remote_copy(
                    src_ref=ins[w].at[_dev_index(peer)] if scatter else ins[w], dst_ref=zones[w].at[me],
                    send_sem=send_sems.at[_sem_index(w, m)], recv_sem=recv_sems.at[_sem_index(w, m)],
                    device_id=peer, device_id_type=pl.DeviceIdType.MESH).start()
        token[...] = jnp.zeros_like(token)

    sems = pltpu.SemaphoreType.DMA((n * (N_DEV - 1),))
    res = pl.pallas_call(
        body, name=name,
        out_shape=(sems, sems, *[pltpu.HBM(a.shape, a.dtype) for a in arrs], *[pltpu.HBM(z.shape, z.dtype) for z in lands],
                   SDS((8, 128), F32)),
        in_specs=[_HBM] * (2 * n), out_specs=(_SEM, _SEM, *([_HBM] * (2 * n)), pl.BlockSpec(memory_space=pltpu.VMEM)),
        input_output_aliases={i: 2 + i for i in range(2 * n)},
        compiler_params=pltpu.CompilerParams(has_side_effects=_EFFECT),
    )(*[pltpu.with_memory_space_constraint(a, pltpu.HBM) for a in arrs],
      *[pltpu.with_memory_space_constraint(z, pltpu.HBM) for z in lands])
    return (res[0], res[1], list(res[2:2 + n]), list(res[2 + n:2 + 2 * n])), res[-1]


def _exchange_wait(handle, after, scatter, name, masks=_ALL_MASKS):
    send_sems, recv_sems, thru, lands = handle
    n = len(thru)

    def body(*refs):
        ins, zones = refs[:n], refs[n:2 * n]
        send_sems, recv_sems = refs[2 * n], refs[2 * n + 1]
        x, y, c = lax.axis_index("x"), lax.axis_index("y"), lax.axis_index("c")
        me = _dev_index((x, y, c))
        for w in range(n):
            for m in masks:
                peer = _peer(x, y, c, m)
                copy = pltpu.make_async_remote_copy(
                    src_ref=ins[w].at[me] if scatter else ins[w], dst_ref=zones[w].at[_dev_index(peer)],
                    send_sem=send_sems.at[_sem_index(w, m)], recv_sem=recv_sems.at[_sem_index(w, m)],
                    device_id=peer, device_id_type=pl.DeviceIdType.MESH)
                copy.wait_send()
                copy.wait_recv()

    res = pl.pallas_call(
        body, name=name,
        out_shape=(*[pltpu.HBM(a.shape, a.dtype) for a in thru], *[pltpu.HBM(z.shape, z.dtype) for z in lands]),
        in_specs=[_HBM] * (2 * n) + [_SEM, _SEM, pl.BlockSpec(memory_space=pl.ANY)], out_specs=tuple([_HBM] * (2 * n)),
        input_output_aliases={i: i for i in range(2 * n)},
        compiler_params=pltpu.CompilerParams(has_side_effects=_EFFECT),
    )(*thru, *lands, send_sems, recv_sems, after)
    return list(res[:n]), list(res[n:])


def _forward_start(zones, name):
    n = len(zones)

    def body(*refs):
        zs = refs[:n]
        send_sems, recv_sems = refs[n], refs[n + 1]
        token = refs[-1]
        x, y, c = lax.axis_index("x"), lax.axis_index("y"), lax.axis_index("c")
        for w in range(n):
            for m in _CHIP_MASKS:
                slot = zs[w].at[_dev_index(_peer(x, y, c, m))]
                pltpu.make_async_remote_copy(
                    src_ref=slot, dst_ref=slot, send_sem=send_sems.at[_sem_index(w, m)],
                    recv_sem=recv_sems.at[_sem_index(w, m)], device_id=(x, y, 1 - c),
                    device_id_type=pl.DeviceIdType.MESH).start()
        token[...] = jnp.zeros_like(token)

    sems = pltpu.SemaphoreType.DMA((n * (N_DEV - 1),))
    res = pl.pallas_call(
        body, name=name, out_shape=(sems, sems, *[pltpu.HBM(z.shape, z.dtype) for z in zones], SDS((8, 128), F32)),
        in_specs=[_HBM] * n, out_specs=(_SEM, _SEM, *([_HBM] * n), pl.BlockSpec(memory_space=pltpu.VMEM)),
        input_output_aliases={i: 2 + i for i in range(n)},
        compiler_params=pltpu.CompilerParams(has_side_effects=_EFFECT),
    )(*[pltpu.with_memory_space_constraint(z, pltpu.HBM) for z in zones])
    return (res[0], res[1], list(res[2:2 + n])), res[-1]


def _forward_wait(handle, after, name):
    send_sems, recv_sems, zones = handle
    n = len(zones)

    def body(*refs):
        zs = refs[:n]
        send_sems, recv_sems = refs[n], refs[n + 1]
        x, y, c = lax.axis_index("x"), lax.axis_index("y"), lax.axis_index("c")
        for w in range(n):
            for m in _CHIP_MASKS:
                copy = pltpu.make_async_remote_copy(
                    src_ref=zs[w].at[_dev_index(_peer(x, y, c, m))], dst_ref=zs[w].at[_dev_index(_peer(x, y, 1 - c, m))],
                    send_sem=send_sems.at[_sem_index(w, m)], recv_sem=recv_sems.at[_sem_index(w, m)],
                    device_id=(x, y, 1 - c), device_id_type=pl.DeviceIdType.MESH)
                copy.wait_send()
                copy.wait_recv()

    res = pl.pallas_call(
        body, name=name, out_shape=tuple(pltpu.HBM(z.shape, z.dtype) for z in zones),
        in_specs=[_HBM] * n + [_SEM, _SEM, pl.BlockSpec(memory_space=pl.ANY)], out_specs=tuple([_HBM] * n),
        input_output_aliases={i: i for i in range(n)},
        compiler_params=pltpu.CompilerParams(has_side_effects=_EFFECT),
    )(*zones, send_sems, recv_sems, after)
    return list(res)


def _adam_math(g, w, m, v):
    m = ADAM_B1 * m + (1.0 - ADAM_B1) * g
    v = ADAM_B2 * v + (1.0 - ADAM_B2) * (g * g)
    m_hat = m / (1.0 - ADAM_B1 ** ADAM_STEP)
    v_hat = v / (1.0 - ADAM_B2 ** ADAM_STEP)
    delta = -ADAM_LR * (m_hat / (jnp.sqrt(v_hat) + ADAM_EPS) + ADAM_WD * w)
    return delta, m, v


def _adam(parts, w, m, v, name, tr=128):
    r, c = w.shape
    tr = next(t for t in (tr, 64, 32, 16, 8) if r % t == 0)

    def fn(cc, rr, pb, wb, mb, vb):
        g = pb[0].astype(F32)
        for d in range(1, N_DEV):
            g = g + pb[d].astype(F32)
        delta, nm, nv = _adam_math(g, wb, mb, vb)
        return g, delta, nm, nv

    blk = ((tr, c), lambda cc, rr: (rr, 0))
    o = SDS((r, c), F32)
    return _ew(fn, [(parts, (N_DEV, tr, c), lambda cc, rr: (0, rr, 0)), (w, *blk), (m, *blk), (v, *blk)],
               [(o, *blk, None)] * 4, (1, r // tr), name)


_SHARDED = ("w_in", "w_glu", "w_branch_attn", "w_branch_ssm", "w_out", "w_up", "w_down")
_COL_SHARDED = ("w_in", "w_glu", "w_branch_attn", "w_branch_ssm", "w_up")
_GROUPS = {"a": ("w_in",), "b": ("w_glu", "w_branch_attn", "w_branch_ssm", "w_out"), "c": ("w_up", "w_down")}
_SMALL = ("attn_norm_g", "b_in", "attn_sinks", "ssm_a_re", "ssm_a_im", "ssm_log_dt", "ssm_b_re", "ssm_b_im",
          "ssm_c_re", "ssm_c_im", "ssm_d", "b_glu", "ffn_norm_g", "conv_w", "conv_b", "final_norm_g")
_WEIGHTS = ("attn_norm_g", "w_in", "b_in", "attn_sinks", "ssm_a_re", "ssm_a_im", "ssm_log_dt", "ssm_b_re", "ssm_b_im",
            "ssm_c_re", "ssm_c_im", "ssm_d", "w_glu", "b_glu", "w_branch_attn", "w_branch_ssm", "w_out", "ffn_norm_g",
            "w_up", "conv_w", "conv_b", "w_down", "final_norm_g")


def _unstack_cols(g):
    return g.transpose(1, 0, 2).reshape(g.shape[1], g.shape[0] * g.shape[2])


def _stack_cols(a, d=N_DEV):
    k, n = a.shape
    return a.reshape(k, d, n // d).transpose(1, 0, 2)


def _pack(arrs):
    flat = jnp.concatenate([a.reshape(-1) for a in arrs])
    pad = (-flat.shape[0]) % 1024
    return jnp.pad(flat, (0, pad)).reshape(-1, 128)


def _local_step(x, tgt, wget, small, gput):
    L = x.shape[0]
    nr = lambda tm: L // tm

    h = _rmsnorm_fwd(x, small["attn_norm_g"], "norm1")
    wts = dict(wget("a", h))
    projb = _mm(h, wts["w_in"], bias=small["b_in_p"], out_dtype=BF16, name="proj")
    proj = projb
    attn_bias = _attn_bias()
    attn = _attn_fwd(projb, small["attn_sinks"], attn_bias, "attn_fwd")

    ab, bmat, cmat = _ssm_prep(small["a_re"], small["a_im"], small["logdt"], small["b_re"], small["b_im"],
                               small["c_re"], small["c_im"], "ssm_prep")
    u_seg = _to_segments(proj[:, C_U:C_PAD])
    ends_f = _ssm_scan(u_seg, bmat, ab, reverse=False, tk=128, name="ssm_ends_fwd")
    xs, init_f, y_seg = _ssm_scan(u_seg, bmat, ab, reverse=False, ends=ends_f, wproj=cmat, tk=64, name="ssm_scan_fwd")
    y_mm = _from_segments(y_seg)

    def gelu_fn(c, r, yb, ub, db):
        yv = yb + db * ub
        return yv, _gelu(yv)

    tm = 512
    y, gy = _ew(gelu_fn, [(y_mm, *_rc(tm, 256)), (proj, *_rc(tm, 256, C_U // 256)), (small["ssm_d"], *_col(1, 256))],
                [(SDS((L, SW), F32), *_rc(tm, 256), None), (SDS((L, SW), BF16), *_rc(tm, 256), None)],
                (2, nr(tm)), "ssm_gelu")
    wts.update(wget("b", gy))
    glu = _mm(gy, wts["w_glu"], bias=small["b_glu"], name="glu")

    def glu_fn(c, r, vb, gb):
        return (vb * _sigmoid(gb),)

    (ssm,) = _ew(glu_fn, [(glu, *_rc(tm, SW)), (glu, *_rc(tm, SW, 1))], [(SDS((L, SW), BF16), *_rc(tm, SW), None)],
                 (1, nr(tm)), "glu_gate")
    f32 = lambda ref, cols: ref[:, cols].astype(F32)
    tnm = 1024
    gate_tiles = [(projb, "tile", C_GA // tnm), (projb, "tile", C_GS // tnm)]

    def merge_ep(i, cols, ra, rs, ga, gs):
        sa, ss = _sigmoid(f32(ga, cols)), _sigmoid(f32(gs, cols))
        return sa * ra + ss * rs, ra, rs, sa, ss

    merged, br_a, br_s, sig_a, sig_s = _mm(attn, wts["w_branch_attn"], a2=ssm, b2=wts["w_branch_ssm"], tm=512, tn=tnm,
                                           extras=gate_tiles, epilogue=merge_ep,
                                           outs=[(SDS((L, D), BF16), "tile")] * 5, name="branch_merge")
    x1 = _mm(merged, wts["w_out"], res=x, name="out_proj")
    h2 = _rmsnorm_fwd(x1, small["ffn_norm_g"], "norm2")
    wts.update(wget("c", h2))
    conv_w = wts["conv_w"]
    w_up = wts["w_up"]
    tcf = 1408
    tma = 256
    hb = 16

    def conv_gate(first, gate, halo, cw, cb):
        halo = halo * jnp.logical_not(first).astype(F32)
        g1, g2 = _shift_rows(gate, halo, 1), _shift_rows(gate, halo, 2)
        return cb + cw[2:3] * gate + cw[1:2] * g1 + cw[0:1] * g2, g1, g2

    tmu, tnu = 1024, 512

    def up_ep(i, cols, rv, rg, h2_halo, wg, cw, cb):
        halo = jnp.dot(h2_halo[...], wg[:, cols], preferred_element_type=F32)[hb - 8:]
        gl, glg = _gelu_and_grad(conv_gate(i == 0, rg, halo, cw[:, cols], cb[:, cols])[0])
        return rg, rv * gl, gl, rv * glg

    up_g, act, gelu_cg, val_gelu_grad = _mm(
        h2, w_up, b2=w_up, n_cols=DFF, b2_col_off=DFF // tnu, tm=tmu, tn=tnu, epilogue=up_ep,
        outs=[(SDS((L, DFF), BF16), "tile")] * 4, name="ffn_up_act",
        extras=[(h2, "spec", ((hb, D), lambda j, i: (jnp.maximum(i * (tmu // hb) - 1, 0), 0))),
                (w_up, "spec", ((D, tnu), lambda j, i: (0, j + DFF // tnu))), (conv_w, "col", 0),
                (small["conv_b"], "col", 0)])
    x2 = _mm(act, wts["w_down"], res=x1, name="ffn_down")
    d_x2, d_x2b, loss_cols, d_gf = _final_loss(x2, small["final_norm_g"], tgt, "final_loss")
    loss = jnp.sum(loss_cols)

    dw_down = _mm(act, d_x2b, ta=True, out_dtype=BF16, tm=tcf, tk=2048, name="dw_down")
    tmd, tnd = 1024, 512

    def dact_ep(i, cols, da, _, gate_ref, halo_ref, gl_ref, vg_ref):
        gate, gl = f32(gate_ref, cols), f32(gl_ref, cols)
        halo = f32(halo_ref, cols)[hb - 8:] * (i > 0).astype(F32)
        g1, g2 = _shift_rows(gate, halo, 1), _shift_rows(gate, halo, 2)
        d_cg = da * f32(vg_ref, cols)
        row3 = lax.broadcasted_iota(jnp.int32, (3, da.shape[1]), 0)
        s0 = jnp.sum(d_cg * g2, axis=0, keepdims=True)
        s1 = jnp.sum(d_cg * g1, axis=0, keepdims=True)
        s2 = jnp.sum(d_cg * gate, axis=0, keepdims=True)
        dcw = jnp.where(row3 == 0, s0, jnp.where(row3 == 1, s1, s2))
        return da * gl, d_cg, dcw, jnp.sum(d_cg, axis=0, keepdims=True)

    d_up, d_cg, d_conv_w, d_conv_b = _mm(
        d_x2b, wts["w_down"], tb=True, tm=tmd, tn=tnd, epilogue=dact_ep, name="d_act_bwd",
        extras=[(up_g, "tile", 0), (up_g, "spec", ((hb, tnd), lambda j, i: (jnp.maximum(i * (tmd // hb) - 1, 0), j))),
                (gelu_cg, "tile", 0), (val_gelu_grad, "tile", 0)],
        outs=[(SDS((L, 2 * DFF), BF16), "tile"), (SDS((L, DFF), BF16), "tile"), (SDS((3, DFF), F32), "colacc"),
              (SDS((1, DFF), F32), "colacc")])
    ncf = DFF // tcf

    tmg = 512

    def gate_bwd(c, r, dcg, halo, cw):
        halo = halo[:8] * (r < nr(tmg) - 1).astype(F32)
        return (cw[2:3] * dcg + cw[1:2] * _shift_rows_up(dcg, halo, 1) + cw[0:1] * _shift_rows_up(dcg, halo, 2),)

    (d_up,) = _ew(gate_bwd, [(d_cg, *_rc(tmg, tcf)),
                             (d_cg, (hb, tcf), lambda c, r: (jnp.minimum((r + 1) * (tmg // hb), L // hb - 1), c)),
                             (conv_w, *_col(3, tcf))],
                  [(SDS((L, 2 * DFF), BF16), *_rc(tmg, tcf, ncf), None)], (ncf, nr(tmg)), "ffn_gate_bwd", into=d_up)
    d_h2 = _mm(d_up, w_up, tb=True, name="d_h2")
    assert tcf == 2 * DFF // N_DEV
    dw_up = _mm(h2, d_up, ta=True, out_dtype=BF16, tn=tcf, tk=2048, stack_out=True, name="dw_up")
    tok = gput("c", {"w_up": dw_up, "w_down": dw_down})
    d_x1, d_g2 = _rmsnorm_bwd(d_h2, x1, small["ffn_norm_g"] + tok[0, 0], d_x2, "norm2_bwd")

    dw_out = _mm(merged, d_x1, ta=True, out_dtype=BF16, name="dw_out")

    def dmerge_ep(i, cols, dm, _, a_ref, s_ref, sa_ref, ss_ref):
        sa, ss = f32(sa_ref, cols), f32(ss_ref, cols)
        return dm * sa, dm * ss, dm * f32(a_ref, cols) * (sa * (1.0 - sa)), dm * f32(s_ref, cols) * (ss * (1.0 - ss))

    d_bra, d_brs, d_ga, d_gs = _mm(d_x1, wts["w_out"], tb=True, tm=512, tn=tnm, epilogue=dmerge_ep,
                                   extras=[(br_a, "tile", 0), (br_s, "tile", 0), (sig_a, "tile", 0), (sig_s, "tile", 0)],
                                   outs=[(SDS((L, D), BF16), "tile")] * 4, name="d_merged_bwd")
    d_attn = _mm(d_bra, wts["w_branch_attn"], tb=True, out_dtype=BF16, name="d_attn")
    dw_ba = _mm(attn, d_bra, ta=True, out_dtype=BF16, name="dw_branch_attn")
    d_ssm = _mm(d_brs, wts["w_branch_ssm"], tb=True, name="d_ssm")
    dw_bs = _mm(ssm, d_brs, ta=True, out_dtype=BF16, name="dw_branch_ssm")
    dq, dkv_cur, dkv_prev, d_sinks = _attn_bwd(projb, small["attn_sinks"], attn_bias, d_attn, "attn_bwd")

    def glu_bwd(c, r, ds, vb, gb):
        sg = _sigmoid(gb)
        return ds * sg, ds * vb * (sg * (1.0 - sg))

    d_glu_v, d_glu_g = _ew(glu_bwd, [(d_ssm, *_rc(tm, SW)), (glu, *_rc(tm, SW)), (glu, *_rc(tm, SW, 1))],
                           [(SDS((L, SW), F32), *_rc(tm, SW), None)] * 2, (1, nr(tm)), "glu_gate_bwd")
    d_glu = jnp.concatenate([d_glu_v, d_glu_g], axis=1)
    d_gy = _mm(d_glu, wts["w_glu"], tb=True, name="d_gelu_y")
    dw_glu = _mm(gy, d_glu, ta=True, out_dtype=BF16, name="dw_glu")

    tok = gput("b", {"w_glu": dw_glu, "w_branch_attn": dw_ba, "w_branch_ssm": dw_bs, "w_out": dw_out})
    ab = ab + tok[0, 0]

    def gelu_bwd(c, r, dg, yb, ub, dgl):
        dy = dg * _gelu_grad(yb)
        return dy, jnp.sum(dy * ub, axis=0, keepdims=True), jnp.sum(dgl, axis=0, keepdims=True)

    dy, d_ssm_d, d_b_glu = _ew(
        gelu_bwd, [(d_gy, *_rc(tm, 256)), (y, *_rc(tm, 256)), (proj, *_rc(tm, 256, C_U // 256)), (d_glu, *_rc(tm, 512))],
        [(SDS((L, SW), F32), *_rc(tm, 256), None), (SDS((1, SW), F32), *_col(1, 256), "r"),
         (SDS((1, 2 * SW), F32), *_col(1, 512), "r")], (2, nr(tm)), "ssm_gelu_bwd")

    dy_seg = _to_segments(dy)
    ends_r = _ssm_scan(dy_seg, cmat, ab, reverse=True, tk=128, name="ssm_ends_bwd")
    lam, dab8, du_seg = _ssm_scan(dy_seg, cmat, ab, reverse=True, ends=ends_r, xs=xs, init=init_f, wproj=bmat,
                                  tk=64, name="ssm_scan_bwd")
    du_mm = _from_segments(du_seg)
    dbm = _mm(u_seg, lam, ta=True, tm=512, name="ssm_dbmat")
    dcm = _mm(dy_seg, xs, ta=True, tm=512, name="ssm_dcmat")
    d_are, d_aim, d_ldt, d_bre, d_bim, d_cre, d_cim = _ssm_param_bwd(
        small["a_re"], small["a_im"], small["logdt"], small["b_re"], small["b_im"], dab8, dbm, dcm, "ssm_param_bwd")

    nb = L // BLK

    def dproj_fn(c, r, dqb, cur, prv, du, dyb, dsk, dga, dgs):
        dkv = cur + prv * (r < nb - 1).astype(F32)
        dub = du + dsk * dyb
        full = jnp.concatenate([dqb, dkv, dub, jnp.zeros((BLK, C_GA - C_PAD), F32), dga, dgs], axis=1)
        return full, jnp.sum(full, axis=0, keepdims=True)

    rowb = lambda w: ((BLK, w), lambda c, r: (r, 0))
    dproj, d_b_in = _ew(
        dproj_fn, [(dq, *rowb(AW)), (dkv_cur, *rowb(256)),
                   (dkv_prev, (BLK, 256), lambda c, r: (jnp.minimum(r + 1, nb - 1), 0)),
                   (du_mm, *rowb(SW)), (dy, *rowb(SW)), (small["ssm_d"], *_col(1, SW)), (d_ga, *rowb(D)), (d_gs, *rowb(D))],
        [(SDS((L, INP), BF16), *rowb(INP), None), (SDS((1, INP), F32), *_col(1, INP), "all")], (1, nb), "dproj")
    tok_small = gput("small", {
        "b_in": _unpad_cols(d_b_in), "attn_sinks": d_sinks[:, :NQ], "a_re": d_are, "a_im": d_aim, "logdt": d_ldt,
        "b_re": d_bre, "b_im": d_bim, "c_re": d_cre, "c_im": d_cim, "ssm_d": d_ssm_d, "b_glu": d_b_glu,
        "ffn_norm_g": d_g2, "conv_w": d_conv_w, "conv_b": d_conv_b, "final_norm_g": d_gf})
    dw_in = _mm(h, dproj, ta=True, out_dtype=BF16, name="dw_in")
    tok = gput("a", {"w_in": _unpad_cols(dw_in)}) + tok_small
    d_h = _mm(dproj, wts["w_in"], tb=True, bias=jnp.zeros((1, D), F32) + tok[0, 0], name="d_h")
    grad_x, d_g1 = _rmsnorm_bwd(d_h, x, small["attn_norm_g"], d_x1, "norm1_bwd")
    return loss, grad_x, {"attn_norm_g": d_g1}


def _small_layouts(p):
    gp = lambda a: a.reshape(1, NS)
    hgp = lambda a: a.transpose(2, 0, 1).reshape(H, NS)
    chgp = lambda a: a.transpose(1, 0, 2).reshape(H, NS)
    return {
        "attn_norm_g": p["attn_norm_g"].reshape(1, D), "ffn_norm_g": p["ffn_norm_g"].reshape(1, D),
        "final_norm_g": p["final_norm_g"].reshape(1, D),
        "b_in_p": _pad_cols(p["b_in"].reshape(1, INC)),
        "attn_sinks": p["attn_sinks"].reshape(1, NQ),
        "a_re": gp(p["ssm_a_re"]), "a_im": gp(p["ssm_a_im"]), "logdt": jnp.repeat(p["ssm_log_dt"], P).reshape(1, NS),
        "b_re": hgp(p["ssm_b_re"]), "b_im": hgp(p["ssm_b_im"]), "c_re": chgp(p["ssm_c_re"]), "c_im": chgp(p["ssm_c_im"]),
        "ssm_d": p["ssm_d"].reshape(1, SW), "b_glu": p["b_glu"].reshape(1, 2 * SW),
        "conv_b": p["conv_b"].reshape(1, DFF),
    }


def _small_grads_to_param_shapes(sg):
    from_hgp = lambda a: a.reshape(H, G, P).transpose(1, 2, 0)
    from_chgp = lambda a: a.reshape(H, G, P).transpose(1, 0, 2)
    flat = lambda a: a.reshape(-1)
    to_param = {
        "attn_norm_g": ("attn_norm_g", flat), "b_in": ("b_in", flat), "attn_sinks": ("attn_sinks", flat),
        "a_re": ("ssm_a_re", lambda a: a.reshape(G, P)), "a_im": ("ssm_a_im", lambda a: a.reshape(G, P)),
        "logdt": ("ssm_log_dt", lambda a: jnp.sum(a.reshape(G, P), axis=1)),
        "b_re": ("ssm_b_re", from_hgp), "b_im": ("ssm_b_im", from_hgp),
        "c_re": ("ssm_c_re", from_chgp), "c_im": ("ssm_c_im", from_chgp),
        "ssm_d": ("ssm_d", flat), "b_glu": ("b_glu", flat), "ffn_norm_g": ("ffn_norm_g", flat),
        "conv_w": ("conv_w", lambda a: a), "conv_b": ("conv_b", flat), "final_norm_g": ("final_norm_g", flat),
    }
    return {to_param[k][0]: to_param[k][1](a) for k, a in sg.items()}


def kernel(x, attn_norm_g, w_in, b_in, attn_sinks, ssm_a_re, ssm_a_im, ssm_log_dt, ssm_b_re, ssm_b_im, ssm_c_re, ssm_c_im, ssm_d, w_glu, b_glu, w_branch_attn, w_branch_ssm, w_out, ffn_norm_g, w_up, conv_w, conv_b, w_down, final_norm_g, loss_target, m_attn_norm_g, m_w_in, m_b_in, m_attn_sinks, m_ssm_a_re, m_ssm_a_im, m_ssm_log_dt, m_ssm_b_re, m_ssm_b_im, m_ssm_c_re, m_ssm_c_im, m_ssm_d, m_w_glu, m_b_glu, m_w_branch_attn, m_w_branch_ssm, m_w_out, m_ffn_norm_g, m_w_up, m_conv_w, m_conv_b, m_w_down, m_final_norm_g, v_attn_norm_g, v_w_in, v_b_in, v_attn_sinks, v_ssm_a_re, v_ssm_a_im, v_ssm_log_dt, v_ssm_b_re, v_ssm_b_im, v_ssm_c_re, v_ssm_c_im, v_ssm_d, v_w_glu, v_b_glu, v_w_branch_attn, v_w_branch_ssm, v_w_out, v_ffn_norm_g, v_w_up, v_conv_w, v_conv_b, v_w_down, v_final_norm_g):
    args = dict(locals())
    sq = lambda a: a if a.ndim == 1 else a[0]
    wv = {n: sq(args[n]) for n in _WEIGHTS}
    mv = {n: sq(args["m_" + n]) for n in _WEIGHTS}
    vv = {n: sq(args["v_" + n]) for n in _WEIGHTS}
    me = 4 * lax.axis_index("x") + 2 * lax.axis_index("y") + lax.axis_index("c")

    gather, tok = {}, jnp.zeros((8, 128), F32)
    for grp in ("a", "b", "c"):
        shards = [(wv[n] + tok[0, 0]).astype(BF16) for n in _GROUPS[grp]]
        if grp == "c":
            shards.append(jnp.pad(wv["conv_w"] + tok[0, 0], ((0, 5), (0, 64))))
        gather[grp], tok = _exchange_start(shards, False, "gather_start_" + grp,
                                           masks=_FIRST_HOP_MASKS if grp == "a" else _ALL_MASKS)
    small = _small_layouts(wv)
    small["attn_norm_g"] = small["attn_norm_g"] + tok[0, 0]

    def own_slot(land, src):
        return lax.dynamic_update_slice_in_dim(land, src, me, axis=0)

    def wget(grp, after):
        if grp == "a":
            thru, lands = _exchange_wait(gather[grp], after, False, "gather_wait_a", masks=_FIRST_HOP_MASKS)
            fwd, fwd_tok = _forward_start(lands, "gather_forward_start_a")
            lands = _forward_wait(fwd, fwd_tok, "gather_forward_wait_a")
        else:
            thru, lands = _exchange_wait(gather[grp], after, False, "gather_wait_" + grp)
        full = {}
        for n, t, g in zip(_GROUPS[grp], thru, lands):
            g = own_slot(g, t[None])
            full[n] = _unstack_cols(g) if n in _COL_SHARDED else g.reshape(N_DEV * g.shape[1], g.shape[2])
        if grp == "a":
            full["w_in"] = _pad_cols(full["w_in"])
        if grp == "c":
            full["conv_w"] = _unstack_cols(own_slot(lands[-1], thru[-1][None])[:, :3, :DFF // N_DEV])
        return full

    scatter = {}

    early_names = [n for n in _SMALL if n != "attn_norm_g"]
    sgp = {}

    def gput(grp, grads):
        if grp == "small":
            sgp.update(_small_grads_to_param_shapes(grads))
            scatter[grp], token = _exchange_start([_pack([sgp[n] for n in early_names])], False, "gather_small_start")
            return token
        stacked = [grads[n] if n == "w_up" else
                   _stack_cols(grads[n]) if n in _COL_SHARDED else grads[n].reshape(N_DEV, -1, D) for n in _GROUPS[grp]]
        scatter[grp], token = _exchange_start(stacked, True, "scatter_start_" + grp)
        return token

    loss, grad_x, sg = _local_step(x[0], loss_target[0], wget, small, gput)
    loss = lax.psum(loss, MESH_AXES)

    sgp.update(_small_grads_to_param_shapes(sg))
    small_names = [n for n in _SMALL]
    (norm_all,) = _exchange([jnp.pad(sgp["attn_norm_g"].reshape(1, D), ((0, 7), (0, 0)))], False, "gather_norm_grad")
    thru, (small_all,) = _exchange_wait(scatter["small"], norm_all, False, "gather_small_wait")
    small_all = own_slot(small_all, thru[0][None])

    outs_g, outs_d, outs_m, outs_v = {}, {}, {}, {}
    for grp in ("c", "b", "a"):
        thru, lands = _exchange_wait(scatter[grp], norm_all, True, "scatter_wait_" + grp)
        for n, t, pt in zip(_GROUPS[grp], thru, lands):
            pt = own_slot(pt, lax.dynamic_slice_in_dim(t, me, 1, axis=0))
            outs_g[n], outs_d[n], outs_m[n], outs_v[n] = _adam(pt, wv[n], mv[n], vv[n], "adam_" + n)

    sizes = [int(math.prod(sgp[n].shape)) for n in early_names]
    offs = [0]
    for s in sizes:
        offs.append(offs[-1] + s)

    def local_part(n, a):
        if n == "conv_w":
            return lax.dynamic_slice(a, (0, me * (DFF // N_DEV)), (3, DFF // N_DEV))
        return a

    rows = small_all.shape[1]

    def sum_fn(cc, rr, pb, nb_):
        g, gn = pb[0], nb_[0]
        for d in range(1, N_DEV):
            g, gn = g + pb[d], gn + nb_[d]
        return g, gn

    gsum, gnorm = _ew(sum_fn, [(small_all, (N_DEV, rows, 128), lambda cc, rr: (0, 0, 0)),
                               (norm_all, (N_DEV, 8, D), lambda cc, rr: (0, 0, 0))],
                      [(SDS((rows, 128), F32), (rows, 128), lambda cc, rr: (0, 0), None),
                       (SDS((8, D), F32), (8, D), lambda cc, rr: (0, 0), None)], (1, 1), "sum_small_grads")
    gflat = gsum.reshape(-1)
    gsmall = {n: local_part(n, gflat[offs[i]:offs[i + 1]].reshape(sgp[n].shape)) for i, n in enumerate(early_names)}
    gsmall["attn_norm_g"] = gnorm[0]
    as2d = lambda a: a.reshape(1, -1) if a.ndim == 1 else a.reshape(a.shape[0], -1)
    n_small = len(small_names)

    def adam_small(*refs):
        for i in range(n_small):
            g_ref, w_ref, m_ref, v_ref = refs[4 * i:4 * i + 4]
            outs = refs[4 * n_small + 3 * i:4 * n_small + 3 * i + 3]
            for o_ref, val in zip(outs, _adam_math(g_ref[...], w_ref[...], m_ref[...], v_ref[...])):
                o_ref[...] = val

    small_ins = [as2d(t[n]) for n in small_names for t in (gsmall, wv, mv, vv)]
    small_outs = pl.pallas_call(adam_small, name="adam_small",
                                out_shape=[SDS(as2d(wv[n]).shape, F32) for n in small_names for _ in range(3)])(*small_ins)
    for i, n in enumerate(small_names):
        sd, sm, sv = (t.reshape(wv[n].shape) for t in small_outs[3 * i:3 * i + 3])
        outs_g[n], outs_d[n], outs_m[n], outs_v[n] = gsmall[n], sd, sm, sv

    lead = lambda n, a: a if args[n].ndim == 1 else a[None]
    grad_x = grad_x[None]
    return (loss, grad_x, *[lead(n, outs_g[n]) for n in _WEIGHTS], *[lead(n, outs_d[n]) for n in _WEIGHTS],
            *[lead(n, outs_m[n]) for n in _WEIGHTS], *[lead(n, outs_v[n]) for n in _WEIGHTS])
```

```python
import functools
import math

import jax
import jax.numpy as jnp
from jax import lax
from jax.experimental import pallas as pl
from jax.experimental.pallas import tpu as pltpu

F32 = jnp.float32
BF16 = jnp.bfloat16
SDS = jax.ShapeDtypeStruct

N_DEV = 8
D = 2048
NQ, NKV, HD = 16, 2, 64
AW = NQ * HD
BLK = 128
SW, G, H, P = 512, 32, 16, 64
NS = G * P
DFF = 5632
INC = AW + 2 * NKV * HD + SW + 2 * D
C_K, C_U, C_PAD = AW, AW + 2 * NKV * HD, AW + 2 * NKV * HD + SW
C_GA, C_GS, INP = D, 2 * D, 3 * D
RMS_EPS = 1e-6
NEG_BIG = -1e30
ADAM_LR, ADAM_B1, ADAM_B2, ADAM_EPS, ADAM_WD, ADAM_STEP = 0.001, 0.9, 0.999, 1e-08, 0.01, 10
NSEG = 8
VMEM_CAP_MB = 60
MESH_AXES = ("x", "y", "c")


def _pad_cols(a):
    zeros = jnp.zeros(a.shape[:-1] + (C_GA - C_PAD,), a.dtype)
    return jnp.concatenate([a[..., :C_PAD], zeros, a[..., C_PAD:]], axis=-1)


def _unpad_cols(a):
    return jnp.concatenate([a[..., :C_PAD], a[..., C_GA:]], axis=-1)


def _cparams(sem, vmem_mb):
    return pltpu.CompilerParams(dimension_semantics=sem, vmem_limit_bytes=min(int(vmem_mb), VMEM_CAP_MB) << 20)


LANES = 128


def _tile(dim, pref):
    if dim <= pref:
        return dim
    for t in range(pref - pref % LANES, 0, -LANES):
        if dim % t == 0:
            return t
    raise ValueError(f"no tile for {dim}")


def _mm(a, b, *, ta=False, tb=False, bias=None, res=None, out_dtype=F32, tm=1024, tn=1024, tk=3072, name,
        a2=None, b2=None, extras=(), epilogue=None, outs=None, ep_cols=None, stack_out=False, n_cols=None,
        b2_col_off=0):
    m, k = (a.shape[1], a.shape[0]) if ta else a.shape
    n = n_cols or (b.shape[0] if tb else b.shape[1])
    assert (b.shape[1] if tb else b.shape[0]) == k, (a.shape, b.shape, ta, tb)
    tm, tn, tk = _tile(m, tm), _tile(n, tn), _tile(k, tk)
    nk = k // tk
    dims = (((0 if ta else 1,), (1 if tb else 0,)), ((), ()))
    has_bias, has_res, has_b2 = bias is not None, res is not None, b2 is not None
    has_a2 = a2 is not None
    assert not (has_b2 and (nk > 1 or ta or tb)) and not (has_a2 and not has_b2)
    if epilogue is None:
        outs = [(SDS((n // tn, m, tn) if stack_out else (m, n), out_dtype), "tile")]
    n_ex, n_out = len(extras), len(outs)
    tcn = tn if (epilogue is None or nk > 1 or ep_cols is None) else _tile(tn, ep_cols)

    def body(*refs):
        a_ref, b_ref = refs[0], refs[1]
        pos = 2
        a2_ref = refs[pos] if has_a2 else a_ref
        pos += has_a2
        b2_ref = refs[pos] if has_b2 else None
        pos += has_b2
        bias_ref = refs[pos] if has_bias else None
        pos += has_bias
        res_ref = refs[pos] if has_res else None
        pos += has_res
        ex_refs = refs[pos:pos + n_ex]
        o_refs = refs[pos + n_ex:pos + n_ex + n_out]
        i = pl.program_id(1)

        def product(rhs_ref, cols=None, lhs=None):
            rhs = rhs_ref[...] if cols is None else (rhs_ref[cols, :] if tb else rhs_ref[:, cols])
            lhs = a_ref[...].astype(BF16) if lhs is None else lhs
            return lax.dot_general(lhs, rhs.astype(BF16), dims, preferred_element_type=F32)

        def finish(r, cols):
            if has_bias:
                r = r + bias_ref[:, cols]
            if has_res:
                r = r + res_ref[:, cols].astype(F32)
            if epilogue is None:
                o_refs[0][:, cols] = r.astype(o_refs[0].dtype)
                return
            r2 = None
            if has_b2:
                r2 = jnp.dot(a2_ref[...].astype(BF16), b2_ref[:, cols].astype(BF16), preferred_element_type=F32)
            vals = epilogue(i, cols, r, r2, *ex_refs)
            for o_ref, v, (_, kind) in zip(o_refs, vals, outs):
                if kind == "tile":
                    o_ref[:, cols] = v.astype(o_ref.dtype)
                else:
                    @pl.when(i == 0)
                    def _(o_ref=o_ref, v=v):
                        o_ref[:, cols] = v.astype(o_ref.dtype)

                    @pl.when(i > 0)
                    def _(o_ref=o_ref, v=v):
                        o_ref[:, cols] += v.astype(o_ref.dtype)

        if nk == 1:
            lhs = a_ref[...].astype(BF16)
            for c0 in range(0, tn, tcn):
                cols = pl.ds(c0, tcn)
                finish(product(b_ref, cols, lhs), cols)
            return
        whole = pl.ds(0, tn)
        acc_ref = refs[-1]
        kk = pl.program_id(2)

        @pl.when(kk == 0)
        def _():
            acc_ref[...] = product(b_ref)

        @pl.when(jnp.logical_and(kk > 0, kk < nk - 1))
        def _():
            acc_ref[...] += product(b_ref)

        @pl.when(kk == nk - 1)
        def _():
            finish(acc_ref[...] + product(b_ref), whole)

    b_spec = pl.BlockSpec((tn, tk), lambda j, i, kk: (j, kk)) if tb else pl.BlockSpec((tk, tn), lambda j, i, kk: (kk, j))
    ins = [a, b]
    in_specs = [pl.BlockSpec((tk, tm), lambda j, i, kk: (kk, i)) if ta else pl.BlockSpec((tm, tk), lambda j, i, kk: (i, kk)),
                b_spec]
    tile_spec = pl.BlockSpec((tm, tn), lambda j, i, kk: (i, j))
    byt = 2 * tm * tk * a.dtype.itemsize + 2 * tk * tn * b.dtype.itemsize
    byt += (2 + has_b2) * 4 * tm * tn
    if has_a2:
        ins.append(a2)
        in_specs.append(pl.BlockSpec((tm, a2.shape[1]), lambda j, i, kk: (i, 0)))
        byt += 2 * tm * a2.shape[1] * a2.dtype.itemsize
    if has_b2:
        ins.append(b2)
        in_specs.append(pl.BlockSpec((b2.shape[0], tn), lambda j, i, kk: (0, j + b2_col_off)))
        byt += 2 * b2.shape[0] * tn * b2.dtype.itemsize
    if has_bias:
        ins.append(bias)
        in_specs.append(pl.BlockSpec((1, tn), lambda j, i, kk: (0, j)))
    if has_res:
        ins.append(res)
        in_specs.append(tile_spec)
        byt += 2 * tm * tn * res.dtype.itemsize
    for arr, kind, arg in extras:
        ins.append(arr)
        if kind == "tile":
            in_specs.append(pl.BlockSpec((tm, tn), lambda j, i, kk, arg=arg: (i, j + arg)))
            byt += 2 * tm * tn * arr.dtype.itemsize + 4 * tm * tn
        elif kind == "col":
            in_specs.append(pl.BlockSpec((arr.shape[0], tn), lambda j, i, kk, arg=arg: (0, j + arg)))
        else:
            in_specs.append(pl.BlockSpec(arg[0], lambda j, i, kk, im=arg[1]: im(j, i)))
    out_specs = []
    for sds, kind in outs:
        if kind == "tile":
            out_specs.append(pl.BlockSpec((None, tm, tn), lambda j, i, kk: (j, i, 0)) if stack_out else tile_spec)
            byt += 2 * tm * tn * jnp.dtype(sds.dtype).itemsize
        else:
            out_specs.append(pl.BlockSpec((sds.shape[0], tn), lambda j, i, kk: (0, j)))
    res_ = pl.pallas_call(
        body, out_shape=tuple(o[0] for o in outs), grid=(n // tn, m // tm, nk), in_specs=in_specs,
        out_specs=tuple(out_specs), scratch_shapes=[pltpu.VMEM((tm, tn), F32)] if nk > 1 else [], name=name,
        compiler_params=_cparams(("arbitrary", "arbitrary", "arbitrary"), byt / 2**20 + (8 if epilogue is None else 20)),
    )(*ins)
    return res_[0] if epilogue is None else res_


def _ew(fn, ins, outs, grid, name, vmem_mb=40, into=None):
    n_in = len(ins)
    accs = [o[3] for o in outs]

    def body(*refs):
        c, r = pl.program_id(0), pl.program_id(1)
        vals = fn(c, r, *[ref[...].astype(F32) for ref in refs[:n_in]])
        for o_ref, v, acc in zip(refs[n_in + (into is not None):], vals, accs):
            if acc is None:
                o_ref[...] = v.astype(o_ref.dtype)
            else:
                first = (r == 0) if acc == "r" else jnp.logical_and(r == 0, c == 0)

                @pl.when(first)
                def _(o_ref=o_ref, v=v):
                    o_ref[...] = v.astype(o_ref.dtype)

                @pl.when(jnp.logical_not(first))
                def _(o_ref=o_ref, v=v):
                    o_ref[...] += v.astype(o_ref.dtype)

    in_specs = [pl.BlockSpec(bs, im) for _, bs, im in ins]
    args = [a for a, _, _ in ins]
    if into is not None:
        in_specs.append(pl.BlockSpec(memory_space=pl.ANY))
        args.append(into)
    res = pl.pallas_call(
        body, out_shape=tuple(o[0] for o in outs), grid=grid, in_specs=in_specs,
        out_specs=tuple(pl.BlockSpec(bs, im) for _, bs, im, _ in outs), name=name,
        input_output_aliases={} if into is None else {n_in: 0},
        compiler_params=_cparams(("arbitrary", "arbitrary"), vmem_mb),
    )(*args)
    return res


def _rc(tm, tc, coff=0):
    return (tm, tc), (lambda c, r: (r, c + coff))


def _col(rows, tc, coff=0):
    return (rows, tc), (lambda c, r: (0, c + coff))


def _gelu(x):
    return 0.5 * x * (1.0 + lax.erf(x * (2.0 ** -0.5)))


def _gelu_and_grad(x):
    cdf = 0.5 * (1.0 + lax.erf(x * (2.0 ** -0.5)))
    return x * cdf, cdf + x * jnp.exp(-0.5 * x * x) * (1.0 / math.sqrt(2.0 * math.pi))


def _gelu_grad(x):
    return _gelu_and_grad(x)[1]


def _sigmoid(x):
    return 1.0 / (1.0 + jnp.exp(-x))


def _shift_rows(x, halo, s):
    rolled = pltpu.roll(x, s, 0)
    row8 = lax.broadcasted_iota(jnp.int32, halo.shape, 0)
    head = jnp.where(row8 < s, pltpu.roll(halo, s, 0), rolled[0:8])
    return jnp.concatenate([head, rolled[8:]], axis=0)


def _shift_rows_up(x, halo, s):
    tm = x.shape[0]
    rolled = pltpu.roll(x, tm - s, 0)
    row8 = lax.broadcasted_iota(jnp.int32, halo.shape, 0)
    tail = jnp.where(row8 >= 8 - s, pltpu.roll(halo, 8 - s, 0), rolled[tm - 8:])
    return jnp.concatenate([rolled[:tm - 8], tail], axis=0)


def _rmsnorm_fwd(x, g, name, tm=512):
    L = x.shape[0]

    def fn(c, r, xb, gb):
        rstd = lax.rsqrt(jnp.mean(xb * xb, axis=-1, keepdims=True) + RMS_EPS)
        return ((xb * rstd) * gb,)

    return _ew(fn, [(x, *_rc(tm, D)), (g, *_col(1, D))], [(SDS((L, D), BF16), *_rc(tm, D), None)], (1, L // tm), name)[0]


def _rmsnorm_bwd(dh, x, g, dres, name, tm=512):
    L = x.shape[0]

    def fn(c, r, dhb, xb, gb, drb):
        rstd = lax.rsqrt(jnp.mean(xb * xb, axis=-1, keepdims=True) + RMS_EPS)
        y = xb * rstd
        dy = dhb * gb
        dx = rstd * (dy - y * jnp.mean(dy * y, axis=-1, keepdims=True))
        return drb + dx, jnp.sum(dhb * y, axis=0, keepdims=True)

    return _ew(fn, [(dh, *_rc(tm, D)), (x, *_rc(tm, D)), (g, *_col(1, D)), (dres, *_rc(tm, D))],
               [(SDS((L, D), F32), *_rc(tm, D), None), (SDS((1, D), F32), *_col(1, D), "all")], (1, L // tm), name,
               vmem_mb=56)


def _final_loss(x2, g, tgt, name, tm=512):
    L = x2.shape[0]

    def fn(c, r, xb, gb, tb):
        rstd = lax.rsqrt(jnp.mean(xb * xb, axis=-1, keepdims=True) + RMS_EPS)
        y = xb * rstd
        err = y * gb - tb
        dout = err * (1.0 / D)
        dy = dout * gb
        dx = rstd * (dy - y * jnp.mean(dy * y, axis=-1, keepdims=True))
        return dx, dx, jnp.sum(err * err, axis=0, keepdims=True) * (0.5 / D), jnp.sum(dout * y, axis=0, keepdims=True)

    return _ew(fn, [(x2, *_rc(tm, D)), (g, *_col(1, D)), (tgt, *_rc(tm, D))],
               [(SDS((L, D), F32), *_rc(tm, D), None), (SDS((L, D), BF16), *_rc(tm, D), None),
                (SDS((1, D), F32), *_col(1, D), "all"),
                (SDS((1, D), F32), *_col(1, D), "all")], (1, L // tm), name, vmem_mb=56)


def _slope(h):
    return 2.0 ** (-8.0 * (h + 1) / NQ)


def _attn_bias():
    qi = lax.broadcasted_iota(jnp.int32, (BLK, 2 * BLK), 0)
    si = lax.broadcasted_iota(jnp.int32, (BLK, 2 * BLK), 1)
    dist = qi + BLK - si
    band = (dist >= 0) & (dist < BLK)
    slopes = jnp.asarray([_slope(h) for h in range(NQ)], F32)[:, None, None]
    alibi = -slopes * dist.astype(F32)[None]
    return jnp.stack([jnp.where((band & (si >= BLK))[None], alibi, NEG_BIG), jnp.where(band[None], alibi, NEG_BIG)])


def _attn_kv(kvc, kvp):
    kv = jnp.concatenate([kvp, kvc], axis=0).astype(F32)
    lo = lax.broadcasted_iota(jnp.int32, (2 * BLK, 128), 1) < HD

    def halves(t):
        tr = pltpu.roll(t, HD, 1)
        z = jnp.zeros_like(t)
        return {(0, 0): jnp.where(lo, t, z).astype(BF16), (0, 1): jnp.where(lo, z, tr).astype(BF16),
                (1, 0): jnp.where(lo, tr, z).astype(BF16), (1, 1): jnp.where(lo, z, t).astype(BF16)}

    return halves(kv[:, :128]), halves(kv[:, 128:])


_NT = (((1,), (1,)), ((), ()))
_TN = (((0,), (0,)), ((), ()))
_ATTN_SPECS = [pl.BlockSpec(memory_space=pltpu.SMEM),
               pl.BlockSpec((None, NQ, BLK, 2 * BLK), lambda n: (jnp.minimum(n, 1), 0, 0, 0)),
               pl.BlockSpec((BLK, AW), lambda n: (n, 0)),
               pl.BlockSpec((BLK, 256), lambda n: (n, C_K // 256)),
               pl.BlockSpec((BLK, 256), lambda n: (jnp.maximum(n - 1, 0), C_K // 256))]


def _attn_scores(q_ref, bias_ref, kmat, sc_ref):
    for j in range(NQ // 2):
        qs = q_ref[:, 128 * j:128 * (j + 1)] * (HD ** -0.5)
        for e in range(2):
            h = 2 * j + e
            sc_ref[h] = lax.dot_general(qs, kmat[(j // (NQ // 4), e)], _NT, preferred_element_type=F32) + bias_ref[h]


def _softmax_with_sink(s, sink):
    m = jnp.maximum(jnp.max(s, axis=-1, keepdims=True), sink)
    p = jnp.exp(s - m)
    esink = jnp.exp(sink - m)
    den = jnp.sum(p, axis=-1, keepdims=True) + esink
    return p / den, esink / den


def _attn_fwd(projb, sinks, bias, name):
    L = projb.shape[0]

    def body(s_ref, bias_ref, q_ref, kvc_ref, kvp_ref, o_ref, sc_ref, pr_ref):
        kmat, vmat = _attn_kv(kvc_ref[...], kvp_ref[...])
        _attn_scores(q_ref, bias_ref, kmat, sc_ref)
        for h in range(NQ):
            pr_ref[h] = _softmax_with_sink(sc_ref[h], s_ref[0, h])[0].astype(BF16)
        for j in range(NQ // 2):
            g = j // (NQ // 4)
            acc = jnp.dot(pr_ref[2 * j], vmat[(g, 0)], preferred_element_type=F32)
            acc = acc + jnp.dot(pr_ref[2 * j + 1], vmat[(g, 1)], preferred_element_type=F32)
            o_ref[:, 128 * j:128 * (j + 1)] = acc.astype(BF16)

    return pl.pallas_call(
        body, out_shape=SDS((L, AW), BF16), grid=(L // BLK,), in_specs=_ATTN_SPECS,
        out_specs=pl.BlockSpec((BLK, AW), lambda n: (n, 0)), name=name,
        scratch_shapes=[pltpu.VMEM((NQ, BLK, 2 * BLK), F32), pltpu.VMEM((NQ, BLK, 2 * BLK), BF16)],
        compiler_params=_cparams(("arbitrary",), 32),
    )(sinks, bias, projb, projb, projb)


def _attn_bwd(projb, sinks, bias, dattn, name):
    L = projb.shape[0]

    def body(s_ref, bias_ref, q_ref, kvc_ref, kvp_ref, do_ref, dq_ref, dcur_ref, dprev_ref, dsink_ref,
             sc_ref, dp_ref, ds_ref, pr_ref, qm_ref, dm_ref):
        n = pl.program_id(0)
        kmat, vmat = _attn_kv(kvc_ref[...], kvp_ref[...])
        _attn_scores(q_ref, bias_ref, kmat, sc_ref)
        for h in range(NQ):
            j, e = h // 2, h % 2
            dp_ref[h] = lax.dot_general(do_ref[:, 128 * j:128 * (j + 1)], vmat[(j // (NQ // 4), e)], _NT,
                                        preferred_element_type=F32)
        lane = lax.broadcasted_iota(jnp.int32, (1, 128), 1)
        dsv = jnp.zeros((1, 128), F32)
        for h in range(NQ):
            p, psink = _softmax_with_sink(sc_ref[h], s_ref[0, h])
            dp = dp_ref[h]
            drow = jnp.sum(p * dp, axis=-1, keepdims=True)
            ds_ref[h] = (p * (dp - drow)).astype(BF16)
            pr_ref[h] = p.astype(BF16)
            dsv = dsv + jnp.where(lane == h, -jnp.sum(psink * drow, axis=0, keepdims=True), 0.0)
        lo128 = lax.broadcasted_iota(jnp.int32, (BLK, 128), 1) < HD
        for j in range(NQ // 2):
            g = j // (NQ // 4)
            qs = q_ref[:, 128 * j:128 * (j + 1)] * (HD ** -0.5)
            dop = do_ref[:, 128 * j:128 * (j + 1)]
            zb = jnp.zeros_like(qs)
            dqp = jnp.zeros((BLK, 128), F32)
            for e in range(2):
                h = 2 * j + e
                half = lo128 if e == 0 else jnp.logical_not(lo128)
                dqp = dqp + jnp.dot(ds_ref[h], kmat[(g, e)], preferred_element_type=F32)
                qm_ref[h] = jnp.where(half, qs, zb)
                dm_ref[h] = jnp.where(half, dop, zb)
            dq_ref[:, 128 * j:128 * (j + 1)] = (dqp * (HD ** -0.5)).astype(BF16)
        hk = NQ // NKV
        rows = lambda ref, g: ref[g * hk:(g + 1) * hk].reshape(hk * BLK, ref.shape[-1])
        dk = [lax.dot_general(rows(ds_ref, g), rows(qm_ref, g), _TN, preferred_element_type=F32) for g in range(NKV)]
        dv = [lax.dot_general(rows(pr_ref, g), rows(dm_ref, g), _TN, preferred_element_type=F32) for g in range(NKV)]
        lo256 = lax.broadcasted_iota(jnp.int32, (2 * BLK, 128), 1) < HD
        tot = [t + pltpu.roll(t, HD, 1) for t in (dk[0], dk[1], dv[0], dv[1])]
        dkv = jnp.concatenate([jnp.where(lo256, tot[0], tot[1]), jnp.where(lo256, tot[2], tot[3])], axis=1)
        dprev_ref[...] = dkv[:BLK]
        dcur_ref[...] = dkv[BLK:]

        @pl.when(n == 0)
        def _():
            dsink_ref[...] = dsv

        @pl.when(n > 0)
        def _():
            dsink_ref[...] += dsv

    tile = (NQ, BLK, 2 * BLK)
    return pl.pallas_call(
        body, out_shape=(SDS((L, AW), BF16), SDS((L, 256), F32), SDS((L, 256), F32), SDS((1, 128), F32)), grid=(L // BLK,),
        in_specs=_ATTN_SPECS + [pl.BlockSpec((BLK, AW), lambda n: (n, 0))],
        out_specs=(pl.BlockSpec((BLK, AW), lambda n: (n, 0)), pl.BlockSpec((BLK, 256), lambda n: (n, 0)),
                   pl.BlockSpec((BLK, 256), lambda n: (n, 0)), pl.BlockSpec((1, 128), lambda n: (0, 0))),
        scratch_shapes=[pltpu.VMEM(tile, F32), pltpu.VMEM(tile, F32), pltpu.VMEM(tile, BF16), pltpu.VMEM(tile, BF16),
                        pltpu.VMEM((NQ, BLK, 128), BF16), pltpu.VMEM((NQ, BLK, 128), BF16)],
        name=name, compiler_params=_cparams(("arbitrary",), 40),
    )(sinks, bias, projb, projb, projb, dattn)


def _disc(a_re, a_im, logdt, b_re, b_im):
    dt = jnp.exp(logdt)
    mag = jnp.exp(a_re * dt)
    ab_re = mag * jnp.cos(a_im * dt)
    ab_im = mag * jnp.sin(a_im * dt)
    nr = ab_re - 1.0
    ni = ab_im
    den = a_re * a_re + a_im * a_im
    z_re = (nr * a_re + ni * a_im) / den
    z_im = (ni * a_re - nr * a_im) / den
    return ab_re, ab_im, z_re * b_re - z_im * b_im, z_re * b_im + z_im * b_re


def _group_mask():
    row = lax.broadcasted_iota(jnp.int32, (SW, NS), 0) // H
    col = lax.broadcasted_iota(jnp.int32, (SW, NS), 1) // P
    return row == col


def _block_diag(re, im):
    mask = _group_mask()
    z = jnp.zeros((SW, NS), F32)
    return jnp.concatenate([jnp.where(mask, jnp.tile(re, (G, 1)), z), jnp.where(mask, jnp.tile(im, (G, 1)), z)], axis=1)


def _block_diag_t(big):
    mask = _group_mask()
    z = jnp.zeros((SW, NS), F32)
    re = jnp.sum(jnp.where(mask, big[:, :NS], z).reshape(G, H, NS), axis=0)
    im = jnp.sum(jnp.where(mask, big[:, NS:], z).reshape(G, H, NS), axis=0)
    return re, im


def _ssm_prep(a_re, a_im, logdt, b_re, b_im, c_re, c_im, name):
    def body(are, aim, ldt, bre, bim, cre, cim, ab_ref, bm_ref, cm_ref):
        ab_re, ab_im, bb_re, bb_im = _disc(are[...], aim[...], ldt[...], bre[...], bim[...])
        ab_ref[...] = jnp.concatenate([ab_re, ab_im], axis=1)
        bm_ref[...] = _block_diag(bb_re, bb_im).astype(BF16)
        cm_ref[...] = _block_diag(cre[...], -cim[...]).astype(BF16)

    return pl.pallas_call(body, out_shape=(SDS((1, 2 * NS), F32), SDS((SW, 2 * NS), BF16), SDS((SW, 2 * NS), BF16)),
                          name=name, compiler_params=pltpu.CompilerParams(vmem_limit_bytes=48 << 20),
                          )(a_re, a_im, logdt, b_re, b_im, c_re, c_im)


def _ssm_param_bwd(a_re, a_im, logdt, b_re, b_im, dab8, dbm, dcm, name):
    def body(are, aim, ldt, bre, bim, dab_ref, dbm_ref, dcm_ref, o_are, o_aim, o_ldt, o_bre, o_bim, o_cre, o_cim):
        dab = jnp.sum(dab_ref[...], axis=0, keepdims=True)
        dbb_re, dbb_im = _block_diag_t(dbm_ref[...])
        _, vjp = jax.vjp(_disc, are[...], aim[...], ldt[...], bre[...], bim[...])
        d_are, d_aim, d_ldt, d_bre, d_bim = vjp((dab[:, :NS], dab[:, NS:], dbb_re, dbb_im))
        o_are[...], o_aim[...], o_ldt[...], o_bre[...], o_bim[...] = d_are, d_aim, d_ldt, d_bre, d_bim
        dc_re, dc_imn = _block_diag_t(dcm_ref[...])
        o_cre[...] = dc_re
        o_cim[...] = -dc_imn

    v1, vh = SDS((1, NS), F32), SDS((H, NS), F32)
    return pl.pallas_call(body, out_shape=(v1, v1, v1, vh, vh, vh, vh), name=name,
                          compiler_params=pltpu.CompilerParams(vmem_limit_bytes=56 << 20),
                          )(a_re, a_im, logdt, b_re, b_im, dab8, dbm, dcm)


def _ssm_scan(src, wmat, ab, *, reverse, ends=None, xs=None, init=None, wproj=None, name, tk=32):
    L = src.shape[0]
    rows = NSEG * tk
    nch = L // rows
    seg_len = L // NSEG
    n_sq = int(math.log2(seg_len))
    assert 2 ** n_sq == seg_len and L % rows == 0
    first_pass = ends is None
    with_dab = (not first_pass) and reverse
    with_proj = wproj is not None
    assert not (with_proj and first_pass)
    slab = 512
    n_slab = NS // slab

    def body(*refs):
        src_ref, w_ref, ab_ref = refs[:3]
        pos = 3
        if not first_pass:
            ends_ref = refs[pos]
            pos += 1
        if with_dab:
            xs_ref, xsh_ref, init_ref = refs[pos:pos + 3]
            pos += 3
        if with_proj:
            wproj_ref = refs[pos]
            pos += 1
        if first_pass:
            (e_ref,) = refs[pos:pos + 1]
            pos += 1
        else:
            st_out_ref, aux_ref = refs[pos:pos + 2]
            pos += 2
        if with_proj:
            proj_ref = refs[pos]
            pos += 1
        buf_ref, st_ref = refs[pos:pos + 2]
        i = pl.program_id(0)
        a_re = ab_ref[:, :NS]
        a_im = -ab_ref[:, NS:] if reverse else ab_ref[:, NS:]

        @pl.when(i == 0)
        def _():
            if first_pass:
                st_ref[...] = jnp.zeros_like(st_ref)
            else:
                pr, pi = a_re, a_im
                for _ in range(n_sq):
                    pr, pi = pr * pr - pi * pi, 2.0 * pr * pi
                zr = jnp.zeros((1, NS), F32)
                cr, ci = zr, zr
                order = list(range(NSEG - 1, -1, -1)) if reverse else list(range(NSEG))
                st_ref[order[0]:order[0] + 1, :] = jnp.zeros((1, 2 * NS), F32)
                for jprev, j in zip(order[:-1], order[1:]):
                    er, ei = ends_ref[jprev:jprev + 1, :NS], ends_ref[jprev:jprev + 1, NS:]
                    cr, ci = er + pr * cr - pi * ci, ei + pr * ci + pi * cr
                    st_ref[j:j + 1, :NS] = cr
                    st_ref[j:j + 1, NS:] = ci
                if not reverse:
                    aux_ref[...] = st_ref[...]
                else:
                    aux_ref[...] = jnp.zeros_like(aux_ref)

        buf_ref[...] = jnp.dot(src_ref[...].astype(BF16), w_ref[...], preferred_element_type=F32)

        for s in range(n_slab):
            re_sl, im_sl = pl.ds(s * slab, slab), pl.ds(NS + s * slab, slab)
            ar = jnp.broadcast_to(a_re[:, s * slab:(s + 1) * slab], (NSEG, slab))
            ai = jnp.broadcast_to(a_im[:, s * slab:(s + 1) * slab], (NSEG, slab))

            def step(t, carry, re_sl=re_sl, im_sl=im_sl, ar=ar, ai=ai):
                k = (tk - 1 - t) if reverse else t
                r0 = pl.multiple_of(k * NSEG, NSEG)
                xr, xi = carry[0], carry[1]
                nr = ar * xr - ai * xi + buf_ref[pl.ds(r0, NSEG), re_sl]
                ni = ar * xi + ai * xr + buf_ref[pl.ds(r0, NSEG), im_sl]
                if not first_pass:
                    buf_ref[pl.ds(r0, NSEG), re_sl] = nr
                    buf_ref[pl.ds(r0, NSEG), im_sl] = ni
                if not with_dab:
                    return nr, ni
                rp = pl.multiple_of((k - 1) * NSEG, NSEG)
                xpr, xpi = xs_ref[pl.ds(rp, NSEG), re_sl], xs_ref[pl.ds(rp, NSEG), im_sl]
                return nr, ni, carry[2] + nr * xpr + ni * xpi, carry[3] + ni * xpr - nr * xpi

            carry = (st_ref[:, re_sl], st_ref[:, im_sl])
            if with_dab:
                z = jnp.zeros((NSEG, slab), F32)
                carry = lax.fori_loop(0, tk - 1, step, carry + (z, z))
                xr, xi, dr, di = carry
                nr = ar * xr - ai * xi + buf_ref[pl.ds(0, NSEG), re_sl]
                ni = ar * xi + ai * xr + buf_ref[pl.ds(0, NSEG), im_sl]
                buf_ref[pl.ds(0, NSEG), re_sl] = nr
                buf_ref[pl.ds(0, NSEG), im_sl] = ni
                at_start = i == nch - 1
                xpr = jnp.where(at_start, init_ref[:, re_sl], xsh_ref[:, re_sl])
                xpi = jnp.where(at_start, init_ref[:, im_sl], xsh_ref[:, im_sl])
                aux_ref[:, re_sl] += dr + nr * xpr + ni * xpi
                aux_ref[:, im_sl] += di + ni * xpr - nr * xpi
                carry = (nr, ni)
            else:
                carry = lax.fori_loop(0, tk, step, carry)
            st_ref[:, re_sl] = carry[0]
            st_ref[:, im_sl] = carry[1]

        if first_pass:
            @pl.when(i == nch - 1)
            def _():
                e_ref[...] = st_ref[...]
        else:
            st_out_ref[...] = buf_ref[...].astype(st_out_ref.dtype)
            if with_proj:
                proj_ref[...] = lax.dot_general(buf_ref[...].astype(BF16), wproj_ref[...], _NT, preferred_element_type=F32)

    chunk = (lambda i: (nch - 1 - i, 0)) if reverse else (lambda i: (i, 0))
    whole = lambda i: (0, 0)
    ins = [src, wmat, ab]
    once = pl.Buffered(1)
    in_specs = [pl.BlockSpec((rows, SW), chunk), pl.BlockSpec((SW, 2 * NS), whole, pipeline_mode=once),
                pl.BlockSpec((1, 2 * NS), whole)]
    small = SDS((NSEG, 2 * NS), F32)
    small_spec = pl.BlockSpec((NSEG, 2 * NS), whole)
    if not first_pass:
        ins.append(ends)
        in_specs.append(small_spec)
    if with_dab:
        ins += [xs, xs, init]
        in_specs += [pl.BlockSpec((rows, 2 * NS), chunk),
                     pl.BlockSpec((NSEG, 2 * NS), lambda i: (jnp.maximum((nch - 1 - i) * tk - 1, 0), 0)),
                     small_spec]
    if with_proj:
        ins.append(wproj)
        in_specs.append(pl.BlockSpec((SW, 2 * NS), whole, pipeline_mode=once))
    if first_pass:
        out_shape, out_specs = small, small_spec
    else:
        out_shape = (SDS((L, 2 * NS), BF16 if reverse else F32), small)
        out_specs = (pl.BlockSpec((rows, 2 * NS), chunk), small_spec)
        if with_proj:
            out_shape += (SDS((L, SW), F32),)
            out_specs += (pl.BlockSpec((rows, SW), chunk),)
    return pl.pallas_call(
        body, out_shape=out_shape, grid=(nch,), in_specs=in_specs, out_specs=out_specs,
        scratch_shapes=[pltpu.VMEM((rows, 2 * NS), F32), pltpu.VMEM((NSEG, 2 * NS), F32)], name=name,
        compiler_params=_cparams(("arbitrary",), 56),
    )(*ins)


def _to_segments(a):
    L, c = a.shape
    return a.reshape(NSEG, L // NSEG, c).transpose(1, 0, 2).reshape(L, c)


def _from_segments(a):
    L, c = a.shape
    return a.reshape(L // NSEG, NSEG, c).transpose(1, 0, 2).reshape(L, c)


def _peer(x, y, c, m):
    return ((1 - x) if (m >> 2) & 1 else x, (1 - y) if (m >> 1) & 1 else y, (1 - c) if m & 1 else c)


def _dev_index(p):
    return 4 * p[0] + 2 * p[1] + p[2]


def _exchange(arrs, scatter, name):
    n = len(arrs)

    def body(*refs):
        ins, outs = refs[:n], refs[n:2 * n]
        send_sems, recv_sems, loc_sems = refs[2 * n:]
        x, y, c = lax.axis_index("x"), lax.axis_index("y"), lax.axis_index("c")
        me = _dev_index((x, y, c))

        def src(w, to):
            return ins[w].at[to] if scatter else ins[w]

        def local(w):
            return pltpu.make_async_copy(src(w, me), outs[w].at[me], loc_sems.at[w])

        def remote(w, m):
            peer = _peer(x, y, c, m)
            return pltpu.make_async_remote_copy(src_ref=src(w, _dev_index(peer)), dst_ref=outs[w].at[me],
                                                send_sem=send_sems.at[w, m - 1], recv_sem=recv_sems.at[w, m - 1],
                                                device_id=peer, device_id_type=pl.DeviceIdType.MESH)

        def arrival(w, m):
            peer = _peer(x, y, c, m)
            return pltpu.make_async_remote_copy(src_ref=src(w, me), dst_ref=outs[w].at[_dev_index(peer)],
                                                send_sem=send_sems.at[w, m - 1], recv_sem=recv_sems.at[w, m - 1],
                                                device_id=peer, device_id_type=pl.DeviceIdType.MESH)

        for w in range(n):
            local(w).start()
        for w in range(n):
            for m in range(1, N_DEV):
                remote(w, m).start()
        for w in range(n):
            for m in range(1, N_DEV):
                arrival(w, m).wait_recv()
        for w in range(n):
            for m in range(1, N_DEV):
                remote(w, m).wait_send()
        for w in range(n):
            local(w).wait()

    anyspec = pl.BlockSpec(memory_space=pl.ANY)
    out_shape = tuple(SDS(a.shape if scatter else (N_DEV,) + a.shape, a.dtype) for a in arrs)
    return pl.pallas_call(
        body, out_shape=out_shape, in_specs=[anyspec] * n, out_specs=tuple([anyspec] * n),
        scratch_shapes=[pltpu.SemaphoreType.DMA((n, N_DEV - 1)), pltpu.SemaphoreType.DMA((n, N_DEV - 1)),
                        pltpu.SemaphoreType.DMA((n,))],
        name=name, compiler_params=pltpu.CompilerParams(has_side_effects=True),
    )(*arrs)


_HBM = pl.BlockSpec(memory_space=pltpu.HBM)
_SEM = pl.BlockSpec(memory_space=pltpu.SEMAPHORE)
_EFFECT = pltpu.SideEffectType.DATAFLOW_SIDE_EFFECTING


def _sem_index(w, m):
    return w * (N_DEV - 1) + m - 1


_ALL_MASKS = tuple(range(1, N_DEV))
_CHIP_MASKS = (2, 4, 6)
_FIRST_HOP_MASKS = (1,) + _CHIP_MASKS


def _exchange_start(arrs, scatter, name, masks=_ALL_MASKS):
    n = len(arrs)
    lands = [lax.empty(a.shape if scatter else (N_DEV,) + a.shape, a.dtype) for a in arrs]

    def body(*refs):
        ins, zones = refs[:n], refs[n:2 * n]
        send_sems, recv_sems = refs[2 * n], refs[2 * n + 1]
        token = refs[-1]
        x, y, c = lax.axis_index("x"), lax.axis_index("y"), lax.axis_index("c")
        me = _dev_index((x, y, c))
        for w in range(n):
            for m in masks:
                peer = _peer(x, y, c, m)
                pltpu.make_async_remote_copy(
                    src_ref=ins[w].at[_dev_index(peer)] if scatter else ins[w], dst_ref=zones[w].at[me],
                    send_sem=send_sems.at[_sem_index(w, m)], recv_sem=recv_sems.at[_sem_index(w, m)],
                    device_id=peer, device_id_type=pl.DeviceIdType.MESH).start()
        token[...] = jnp.zeros_like(token)

    sems = pltpu.SemaphoreType.DMA((n * (N_DEV - 1),))
    res = pl.pallas_call(
        body, name=name,
        out_shape=(sems, sems, *[pltpu.HBM(a.shape, a.dtype) for a in arrs], *[pltpu.HBM(z.shape, z.dtype) for z in lands],
                   SDS((8, 128), F32)),
        in_specs=[_HBM] * (2 * n), out_specs=(_SEM, _SEM, *([_HBM] * (2 * n)), pl.BlockSpec(memory_space=pltpu.VMEM)),
        input_output_aliases={i: 2 + i for i in range(2 * n)},
        compiler_params=pltpu.CompilerParams(has_side_effects=_EFFECT),
    )(*[pltpu.with_memory_space_constraint(a, pltpu.HBM) for a in arrs],
      *[pltpu.with_memory_space_constraint(z, pltpu.HBM) for z in lands])
    return (res[0], res[1], list(res[2:2 + n]), list(res[2 + n:2 + 2 * n])), res[-1]


def _exchange_wait(handle, after, scatter, name, masks=_ALL_MASKS):
    send_sems, recv_sems, thru, lands = handle
    n = len(thru)

    def body(*refs):
        ins, zones = refs[:n], refs[n:2 * n]
        send_sems, recv_sems = refs[2 * n], refs[2 * n + 1]
        x, y, c = lax.axis_index("x"), lax.axis_index("y"), lax.axis_index("c")
        me = _dev_index((x, y, c))
        for w in range(n):
            for m in masks:
                peer = _peer(x, y, c, m)
                copy = pltpu.make_async_remote_copy(
                    src_ref=ins[w].at[me] if scatter else ins[w], dst_ref=zones[w].at[_dev_index(peer)],
                    send_sem=send_sems.at[_sem_index(w, m)], recv_sem=recv_sems.at[_sem_index(w, m)],
                    device_id=peer, device_id_type=pl.DeviceIdType.MESH)
                copy.wait_send()
                copy.wait_recv()

    res = pl.pallas_call(
        body, name=name,
        out_shape=(*[pltpu.HBM(a.shape, a.dtype) for a in thru], *[pltpu.HBM(z.shape, z.dtype) for z in lands]),
        in_specs=[_HBM] * (2 * n) + [_SEM, _SEM, pl.BlockSpec(memory_space=pl.ANY)], out_specs=tuple([_HBM] * (2 * n)),
        input_output_aliases={i: i for i in range(2 * n)},
        compiler_params=pltpu.CompilerParams(has_side_effects=_EFFECT),
    )(*thru, *lands, send_sems, recv_sems, after)
    return list(res[:n]), list(res[n:])


def _forward_start(zones, name):
    n = len(zones)

    def body(*refs):
        zs = refs[:n]
        send_sems, recv_sems = refs[n], refs[n + 1]
        token = refs[-1]
        x, y, c = lax.axis_index("x"), lax.axis_index("y"), lax.axis_index("c")
        for w in range(n):
            for m in _CHIP_MASKS:
                slot = zs[w].at[_dev_index(_peer(x, y, c, m))]
                pltpu.make_async_remote_copy(
                    src_ref=slot, dst_ref=slot, send_sem=send_sems.at[_sem_index(w, m)],
                    recv_sem=recv_sems.at[_sem_index(w, m)], device_id=(x, y, 1 - c),
                    device_id_type=pl.DeviceIdType.MESH).start()
        token[...] = jnp.zeros_like(token)

    sems = pltpu.SemaphoreType.DMA((n * (N_DEV - 1),))
    res = pl.pallas_call(
        body, name=name, out_shape=(sems, sems, *[pltpu.HBM(z.shape, z.dtype) for z in zones], SDS((8, 128), F32)),
        in_specs=[_HBM] * n, out_specs=(_SEM, _SEM, *([_HBM] * n), pl.BlockSpec(memory_space=pltpu.VMEM)),
        input_output_aliases={i: 2 + i for i in range(n)},
        compiler_params=pltpu.CompilerParams(has_side_effects=_EFFECT),
    )(*[pltpu.with_memory_space_constraint(z, pltpu.HBM) for z in zones])
    return (res[0], res[1], list(res[2:2 + n])), res[-1]


def _forward_wait(handle, after, name):
    send_sems, recv_sems, zones = handle
    n = len(zones)

    def body(*refs):
        zs = refs[:n]
        send_sems, recv_sems = refs[n], refs[n + 1]
        x, y, c = lax.axis_index("x"), lax.axis_index("y"), lax.axis_index("c")
        for w in range(n):
            for m in _CHIP_MASKS:
                copy = pltpu.make_async_remote_copy(
                    src_ref=zs[w].at[_dev_index(_peer(x, y, c, m))], dst_ref=zs[w].at[_dev_index(_peer(x, y, 1 - c, m))],
                    send_sem=send_sems.at[_sem_index(w, m)], recv_sem=recv_sems.at[_sem_index(w, m)],
                    device_id=(x, y, 1 - c), device_id_type=pl.DeviceIdType.MESH)
                copy.wait_send()
                copy.wait_recv()

    res = pl.pallas_call(
        body, name=name, out_shape=tuple(pltpu.HBM(z.shape, z.dtype) for z in zones),
        in_specs=[_HBM] * n + [_SEM, _SEM, pl.BlockSpec(memory_space=pl.ANY)], out_specs=tuple([_HBM] * n),
        input_output_aliases={i: i for i in range(n)},
        compiler_params=pltpu.CompilerParams(has_side_effects=_EFFECT),
    )(*zones, send_sems, recv_sems, after)
    return list(res)


def _adam_math(g, w, m, v):
    m = ADAM_B1 * m + (1.0 - ADAM_B1) * g
    v = ADAM_B2 * v + (1.0 - ADAM_B2) * (g * g)
    m_hat = m / (1.0 - ADAM_B1 ** ADAM_STEP)
    v_hat = v / (1.0 - ADAM_B2 ** ADAM_STEP)
    delta = -ADAM_LR * (m_hat / (jnp.sqrt(v_hat) + ADAM_EPS) + ADAM_WD * w)
    return delta, m, v


def _adam(parts, w, m, v, name, tr=128):
    r, c = w.shape
    tr = next(t for t in (tr, 64, 32, 16, 8) if r % t == 0)

    def fn(cc, rr, pb, wb, mb, vb):
        g = pb[0].astype(F32)
        for d in range(1, N_DEV):
            g = g + pb[d].astype(F32)
        delta, nm, nv = _adam_math(g, wb, mb, vb)
        return g, delta, nm, nv

    blk = ((tr, c), lambda cc, rr: (rr, 0))
    o = SDS((r, c), F32)
    return _ew(fn, [(parts, (N_DEV, tr, c), lambda cc, rr: (0, rr, 0)), (w, *blk), (m, *blk), (v, *blk)],
               [(o, *blk, None)] * 4, (1, r // tr), name)


_SHARDED = ("w_in", "w_glu", "w_branch_attn", "w_branch_ssm", "w_out", "w_up", "w_down")
_COL_SHARDED = ("w_in", "w_glu", "w_branch_attn", "w_branch_ssm", "w_up")
_GROUPS = {"a": ("w_in",), "b": ("w_glu", "w_branch_attn", "w_branch_ssm", "w_out"), "c": ("w_up", "w_down")}
_SMALL = ("attn_norm_g", "b_in", "attn_sinks", "ssm_a_re", "ssm_a_im", "ssm_log_dt", "ssm_b_re", "ssm_b_im",
          "ssm_c_re", "ssm_c_im", "ssm_d", "b_glu", "ffn_norm_g", "conv_w", "conv_b", "final_norm_g")
_WEIGHTS = ("attn_norm_g", "w_in", "b_in", "attn_sinks", "ssm_a_re", "ssm_a_im", "ssm_log_dt", "ssm_b_re", "ssm_b_im",
            "ssm_c_re", "ssm_c_im", "ssm_d", "w_glu", "b_glu", "w_branch_attn", "w_branch_ssm", "w_out", "ffn_norm_g",
            "w_up", "conv_w", "conv_b", "w_down", "final_norm_g")


def _unstack_cols(g):
    return g.transpose(1, 0, 2).reshape(g.shape[1], g.shape[0] * g.shape[2])


def _stack_cols(a, d=N_DEV):
    k, n = a.shape
    return a.reshape(k, d, n // d).transpose(1, 0, 2)


def _pack(arrs):
    flat = jnp.concatenate([a.reshape(-1) for a in arrs])
    pad = (-flat.shape[0]) % 1024
    return jnp.pad(flat, (0, pad)).reshape(-1, 128)


def _local_step(x, tgt, wget, small, gput):
    L = x.shape[0]
    nr = lambda tm: L // tm

    h = _rmsnorm_fwd(x, small["attn_norm_g"], "norm1")
    wts = dict(wget("a", h))
    projb = _mm(h, wts["w_in"], bias=small["b_in_p"], out_dtype=BF16, name="proj")
    proj = projb
    attn_bias = _attn_bias()
    attn = _attn_fwd(projb, small["attn_sinks"], attn_bias, "attn_fwd")

    ab, bmat, cmat = _ssm_prep(small["a_re"], small["a_im"], small["logdt"], small["b_re"], small["b_im"],
                               small["c_re"], small["c_im"], "ssm_prep")
    u_seg = _to_segments(proj[:, C_U:C_PAD])
    ends_f = _ssm_scan(u_seg, bmat, ab, reverse=False, tk=128, name="ssm_ends_fwd")
    xs, init_f, y_seg = _ssm_scan(u_seg, bmat, ab, reverse=False, ends=ends_f, wproj=cmat, tk=64, name="ssm_scan_fwd")
    y_mm = _from_segments(y_seg)

    def gelu_fn(c, r, yb, ub, db):
        yv = yb + db * ub
        return yv, _gelu(yv)

    tm = 512
    y, gy = _ew(gelu_fn, [(y_mm, *_rc(tm, 256)), (proj, *_rc(tm, 256, C_U // 256)), (small["ssm_d"], *_col(1, 256))],
                [(SDS((L, SW), F32), *_rc(tm, 256), None), (SDS((L, SW), BF16), *_rc(tm, 256), None)],
                (2, nr(tm)), "ssm_gelu")
    wts.update(wget("b", gy))
    glu = _mm(gy, wts["w_glu"], bias=small["b_glu"], name="glu")

    def glu_fn(c, r, vb, gb):
        return (vb * _sigmoid(gb),)

    (ssm,) = _ew(glu_fn, [(glu, *_rc(tm, SW)), (glu, *_rc(tm, SW, 1))], [(SDS((L, SW), BF16), *_rc(tm, SW), None)],
                 (1, nr(tm)), "glu_gate")
    f32 = lambda ref, cols: ref[:, cols].astype(F32)
    tnm = 1024
    gate_tiles = [(projb, "tile", C_GA // tnm), (projb, "tile", C_GS // tnm)]

    def merge_ep(i, cols, ra, rs, ga, gs):
        sa, ss = _sigmoid(f32(ga, cols)), _sigmoid(f32(gs, cols))
        return sa * ra + ss * rs, ra, rs, sa, ss

    merged, br_a, br_s, sig_a, sig_s = _mm(attn, wts["w_branch_attn"], a2=ssm, b2=wts["w_branch_ssm"], tm=512, tn=tnm,
                                           extras=gate_tiles, epilogue=merge_ep,
                                           outs=[(SDS((L, D), BF16), "tile")] * 5, name="branch_merge")
    x1 = _mm(merged, wts["w_out"], res=x, name="out_proj")
    h2 = _rmsnorm_fwd(x1, small["ffn_norm_g"], "norm2")
    wts.update(wget("c", h2))
    conv_w = wts["conv_w"]
    w_up = wts["w_up"]
    tcf = 1408
    tma = 256
    hb = 16

    def conv_gate(first, gate, halo, cw, cb):
        halo = halo * jnp.logical_not(first).astype(F32)
        g1, g2 = _shift_rows(gate, halo, 1), _shift_rows(gate, halo, 2)
        return cb + cw[2:3] * gate + cw[1:2] * g1 + cw[0:1] * g2, g1, g2

    tmu, tnu = 1024, 512

    def up_ep(i, cols, rv, rg, h2_halo, wg, cw, cb):
        halo = jnp.dot(h2_halo[...], wg[:, cols], preferred_element_type=F32)[hb - 8:]
        gl, glg = _gelu_and_grad(conv_gate(i == 0, rg, halo, cw[:, cols], cb[:, cols])[0])
        return rg, rv * gl, gl, rv * glg

    up_g, act, gelu_cg, val_gelu_grad = _mm(
        h2, w_up, b2=w_up, n_cols=DFF, b2_col_off=DFF // tnu, tm=tmu, tn=tnu, epilogue=up_ep,
        outs=[(SDS((L, DFF), BF16), "tile")] * 4, name="ffn_up_act",
        extras=[(h2, "spec", ((hb, D), lambda j, i: (jnp.maximum(i * (tmu // hb) - 1, 0), 0))),
                (w_up, "spec", ((D, tnu), lambda j, i: (0, j + DFF // tnu))), (conv_w, "col", 0),
                (small["conv_b"], "col", 0)])
    x2 = _mm(act, wts["w_down"], res=x1, name="ffn_down")
    d_x2, d_x2b, loss_cols, d_gf = _final_loss(x2, small["final_norm_g"], tgt, "final_loss")
    loss = jnp.sum(loss_cols)

    dw_down = _mm(act, d_x2b, ta=True, out_dtype=BF16, tm=tcf, tk=2048, name="dw_down")
    tmd, tnd = 1024, 512

    def dact_ep(i, cols, da, _, gate_ref, halo_ref, gl_ref, vg_ref):
        gate, gl = f32(gate_ref, cols), f32(gl_ref, cols)
        halo = f32(halo_ref, cols)[hb - 8:] * (i > 0).astype(F32)
        g1, g2 = _shift_rows(gate, halo, 1), _shift_rows(gate, halo, 2)
        d_cg = da * f32(vg_ref, cols)
        row3 = lax.broadcasted_iota(jnp.int32, (3, da.shape[1]), 0)
        s0 = jnp.sum(d_cg * g2, axis=0, keepdims=True)
        s1 = jnp.sum(d_cg * g1, axis=0, keepdims=True)
        s2 = jnp.sum(d_cg * gate, axis=0, keepdims=True)
        dcw = jnp.where(row3 == 0, s0, jnp.where(row3 == 1, s1, s2))
        return da * gl, d_cg, dcw, jnp.sum(d_cg, axis=0, keepdims=True)

    d_up, d_cg, d_conv_w, d_conv_b = _mm(
        d_x2b, wts["w_down"], tb=True, tm=tmd, tn=tnd, epilogue=dact_ep, name="d_act_bwd",
        extras=[(up_g, "tile", 0), (up_g, "spec", ((hb, tnd), lambda j, i: (jnp.maximum(i * (tmd // hb) - 1, 0), j))),
                (gelu_cg, "tile", 0), (val_gelu_grad, "tile", 0)],
        outs=[(SDS((L, 2 * DFF), BF16), "tile"), (SDS((L, DFF), BF16), "tile"), (SDS((3, DFF), F32), "colacc"),
              (SDS((1, DFF), F32), "colacc")])
    ncf = DFF // tcf

    tmg = 512

    def gate_bwd(c, r, dcg, halo, cw):
        halo = halo[:8] * (r < nr(tmg) - 1).astype(F32)
        return (cw[2:3] * dcg + cw[1:2] * _shift_rows_up(dcg, halo, 1) + cw[0:1] * _shift_rows_up(dcg, halo, 2),)

    (d_up,) = _ew(gate_bwd, [(d_cg, *_rc(tmg, tcf)),
                             (d_cg, (hb, tcf), lambda c, r: (jnp.minimum((r + 1) * (tmg // hb), L // hb - 1), c)),
                             (conv_w, *_col(3, tcf))],
                  [(SDS((L, 2 * DFF), BF16), *_rc(tmg, tcf, ncf), None)], (ncf, nr(tmg)), "ffn_gate_bwd", into=d_up)
    d_h2 = _mm(d_up, w_up, tb=True, name="d_h2")
    assert tcf == 2 * DFF // N_DEV
    dw_up = _mm(h2, d_up, ta=True, out_dtype=BF16, tn=tcf, tk=2048, stack_out=True, name="dw_up")
    tok = gput("c", {"w_up": dw_up, "w_down": dw_down})
    d_x1, d_g2 = _rmsnorm_bwd(d_h2, x1, small["ffn_norm_g"] + tok[0, 0], d_x2, "norm2_bwd")

    dw_out = _mm(merged, d_x1, ta=True, out_dtype=BF16, name="dw_out")

    def dmerge_ep(i, cols, dm, _, a_ref, s_ref, sa_ref, ss_ref):
        sa, ss = f32(sa_ref, cols), f32(ss_ref, cols)
        return dm * sa, dm * ss, dm * f32(a_ref, cols) * (sa * (1.0 - sa)), dm * f32(s_ref, cols) * (ss * (1.0 - ss))

    d_bra, d_brs, d_ga, d_gs = _mm(d_x1, wts["w_out"], tb=True, tm=512, tn=tnm, epilogue=dmerge_ep,
                                   extras=[(br_a, "tile", 0), (br_s, "tile", 0), (sig_a, "tile", 0), (sig_s, "tile", 0)],
                                   outs=[(SDS((L, D), BF16), "tile")] * 4, name="d_merged_bwd")
    d_attn = _mm(d_bra, wts["w_branch_attn"], tb=True, out_dtype=BF16, name="d_attn")
    dw_ba = _mm(attn, d_bra, ta=True, out_dtype=BF16, name="dw_branch_attn")
    d_ssm = _mm(d_brs, wts["w_branch_ssm"], tb=True, name="d_ssm")
    dw_bs = _mm(ssm, d_brs, ta=True, out_dtype=BF16, name="dw_branch_ssm")
    dq, dkv_cur, dkv_prev, d_sinks = _attn_bwd(projb, small["attn_sinks"], attn_bias, d_attn, "attn_bwd")

    def glu_bwd(c, r, ds, vb, gb):
        sg = _sigmoid(gb)
        return ds * sg, ds * vb * (sg * (1.0 - sg))

    d_glu_v, d_glu_g = _ew(glu_bwd, [(d_ssm, *_rc(tm, SW)), (glu, *_rc(tm, SW)), (glu, *_rc(tm, SW, 1))],
                           [(SDS((L, SW), F32), *_rc(tm, SW), None)] * 2, (1, nr(tm)), "glu_gate_bwd")
    d_glu = jnp.concatenate([d_glu_v, d_glu_g], axis=1)
    d_gy = _mm(d_glu, wts["w_glu"], tb=True, name="d_gelu_y")
    dw_glu = _mm(gy, d_glu, ta=True, out_dtype=BF16, name="dw_glu")

    tok = gput("b", {"w_glu": dw_glu, "w_branch_attn": dw_ba, "w_branch_ssm": dw_bs, "w_out": dw_out})
    ab = ab + tok[0, 0]

    def gelu_bwd(c, r, dg, yb, ub, dgl):
        dy = dg * _gelu_grad(yb)
        return dy, jnp.sum(dy * ub, axis=0, keepdims=True), jnp.sum(dgl, axis=0, keepdims=True)

    dy, d_ssm_d, d_b_glu = _ew(
        gelu_bwd, [(d_gy, *_rc(tm, 256)), (y, *_rc(tm, 256)), (proj, *_rc(tm, 256, C_U // 256)), (d_glu, *_rc(tm, 512))],
        [(SDS((L, SW), F32), *_rc(tm, 256), None), (SDS((1, SW), F32), *_col(1, 256), "r"),
         (SDS((1, 2 * SW), F32), *_col(1, 512), "r")], (2, nr(tm)), "ssm_gelu_bwd")

    dy_seg = _to_segments(dy)
    ends_r = _ssm_scan(dy_seg, cmat, ab, reverse=True, tk=128, name="ssm_ends_bwd")
    lam, dab8, du_seg = _ssm_scan(dy_seg, cmat, ab, reverse=True, ends=ends_r, xs=xs, init=init_f, wproj=bmat,
                                  tk=64, name="ssm_scan_bwd")
    du_mm = _from_segments(du_seg)
    dbm = _mm(u_seg, lam, ta=True, tm=512, name="ssm_dbmat")
    dcm = _mm(dy_seg, xs, ta=True, tm=512, name="ssm_dcmat")
    d_are, d_aim, d_ldt, d_bre, d_bim, d_cre, d_cim = _ssm_param_bwd(
        small["a_re"], small["a_im"], small["logdt"], small["b_re"], small["b_im"], dab8, dbm, dcm, "ssm_param_bwd")

    nb = L // BLK

    def dproj_fn(c, r, dqb, cur, prv, du, dyb, dsk, dga, dgs):
        dkv = cur + prv * (r < nb - 1).astype(F32)
        dub = du + dsk * dyb
        full = jnp.concatenate([dqb, dkv, dub, jnp.zeros((BLK, C_GA - C_PAD), F32), dga, dgs], axis=1)
        return full, jnp.sum(full, axis=0, keepdims=True)

    rowb = lambda w: ((BLK, w), lambda c, r: (r, 0))
    dproj, d_b_in = _ew(
        dproj_fn, [(dq, *rowb(AW)), (dkv_cur, *rowb(256)),
                   (dkv_prev, (BLK, 256), lambda c, r: (jnp.minimum(r + 1, nb - 1), 0)),
                   (du_mm, *rowb(SW)), (dy, *rowb(SW)), (small["ssm_d"], *_col(1, SW)), (d_ga, *rowb(D)), (d_gs, *rowb(D))],
        [(SDS((L, INP), BF16), *rowb(INP), None), (SDS((1, INP), F32), *_col(1, INP), "all")], (1, nb), "dproj")
    tok_small = gput("small", {
        "b_in": _unpad_cols(d_b_in), "attn_sinks": d_sinks[:, :NQ], "a_re": d_are, "a_im": d_aim, "logdt": d_ldt,
        "b_re": d_bre, "b_im": d_bim, "c_re": d_cre, "c_im": d_cim, "ssm_d": d_ssm_d, "b_glu": d_b_glu,
        "ffn_norm_g": d_g2, "conv_w": d_conv_w, "conv_b": d_conv_b, "final_norm_g": d_gf})
    dw_in = _mm(h, dproj, ta=True, out_dtype=BF16, name="dw_in")
    tok = gput("a", {"w_in": _unpad_cols(dw_in)}) + tok_small
    d_h = _mm(dproj, wts["w_in"], tb=True, bias=jnp.zeros((1, D), F32) + tok[0, 0], name="d_h")
    grad_x, d_g1 = _rmsnorm_bwd(d_h, x, small["attn_norm_g"], d_x1, "norm1_bwd")
    return loss, grad_x, {"attn_norm_g": d_g1}


def _small_layouts(p):
    gp = lambda a: a.reshape(1, NS)
    hgp = lambda a: a.transpose(2, 0, 1).reshape(H, NS)
    chgp = lambda a: a.transpose(1, 0, 2).reshape(H, NS)
    return {
        "attn_norm_g": p["attn_norm_g"].reshape(1, D), "ffn_norm_g": p["ffn_norm_g"].reshape(1, D),
        "final_norm_g": p["final_norm_g"].reshape(1, D),
        "b_in_p": _pad_cols(p["b_in"].reshape(1, INC)),
        "attn_sinks": p["attn_sinks"].reshape(1, NQ),
        "a_re": gp(p["ssm_a_re"]), "a_im": gp(p["ssm_a_im"]), "logdt": jnp.repeat(p["ssm_log_dt"], P).reshape(1, NS),
        "b_re": hgp(p["ssm_b_re"]), "b_im": hgp(p["ssm_b_im"]), "c_re": chgp(p["ssm_c_re"]), "c_im": chgp(p["ssm_c_im"]),
        "ssm_d": p["ssm_d"].reshape(1, SW), "b_glu": p["b_glu"].reshape(1, 2 * SW),
        "conv_b": p["conv_b"].reshape(1, DFF),
    }


def _small_grads_to_param_shapes(sg):
    from_hgp = lambda a: a.reshape(H, G, P).transpose(1, 2, 0)
    from_chgp = lambda a: a.reshape(H, G, P).transpose(1, 0, 2)
    flat = lambda a: a.reshape(-1)
    to_param = {
        "attn_norm_g": ("attn_norm_g", flat), "b_in": ("b_in", flat), "attn_sinks": ("attn_sinks", flat),
        "a_re": ("ssm_a_re", lambda a: a.reshape(G, P)), "a_im": ("ssm_a_im", lambda a: a.reshape(G, P)),
        "logdt": ("ssm_log_dt", lambda a: jnp.sum(a.reshape(G, P), axis=1)),
        "b_re": ("ssm_b_re", from_hgp), "b_im": ("ssm_b_im", from_hgp),
        "c_re": ("ssm_c_re", from_chgp), "c_im": ("ssm_c_im", from_chgp),
        "ssm_d": ("ssm_d", flat), "b_glu": ("b_glu", flat), "ffn_norm_g": ("ffn_norm_g", flat),
        "conv_w": ("conv_w", lambda a: a), "conv_b": ("conv_b", flat), "final_norm_g": ("final_norm_g", flat),
    }
    return {to_param[k][0]: to_param[k][1](a) for k, a in sg.items()}


def kernel(x, attn_norm_g, w_in, b_in, attn_sinks, ssm_a_re, ssm_a_im, ssm_log_dt, ssm_b_re, ssm_b_im, ssm_c_re, ssm_c_im, ssm_d, w_glu, b_glu, w_branch_attn, w_branch_ssm, w_out, ffn_norm_g, w_up, conv_w, conv_b, w_down, final_norm_g, loss_target, m_attn_norm_g, m_w_in, m_b_in, m_attn_sinks, m_ssm_a_re, m_ssm_a_im, m_ssm_log_dt, m_ssm_b_re, m_ssm_b_im, m_ssm_c_re, m_ssm_c_im, m_ssm_d, m_w_glu, m_b_glu, m_w_branch_attn, m_w_branch_ssm, m_w_out, m_ffn_norm_g, m_w_up, m_conv_w, m_conv_b, m_w_down, m_final_norm_g, v_attn_norm_g, v_w_in, v_b_in, v_attn_sinks, v_ssm_a_re, v_ssm_a_im, v_ssm_log_dt, v_ssm_b_re, v_ssm_b_im, v_ssm_c_re, v_ssm_c_im, v_ssm_d, v_w_glu, v_b_glu, v_w_branch_attn, v_w_branch_ssm, v_w_out, v_ffn_norm_g, v_w_up, v_conv_w, v_conv_b, v_w_down, v_final_norm_g):
    args = dict(locals())
    sq = lambda a: a if a.ndim == 1 else a[0]
    wv = {n: sq(args[n]) for n in _WEIGHTS}
    mv = {n: sq(args["m_" + n]) for n in _WEIGHTS}
    vv = {n: sq(args["v_" + n]) for n in _WEIGHTS}
    me = 4 * lax.axis_index("x") + 2 * lax.axis_index("y") + lax.axis_index("c")

    gather, tok = {}, jnp.zeros((8, 128), F32)
    for grp in ("a", "b", "c"):
        shards = [(wv[n] + tok[0, 0]).astype(BF16) for n in _GROUPS[grp]]
        if grp == "c":
            shards.append(jnp.pad(wv["conv_w"] + tok[0, 0], ((0, 5), (0, 64))))
        gather[grp], tok = _exchange_start(shards, False, "gather_start_" + grp,
                                           masks=_FIRST_HOP_MASKS if grp == "a" else _ALL_MASKS)
    small = _small_layouts(wv)
    small["attn_norm_g"] = small["attn_norm_g"] + tok[0, 0]

    def own_slot(land, src):
        return lax.dynamic_update_slice_in_dim(land, src, me, axis=0)

    def wget(grp, after):
        if grp == "a":
            thru, lands = _exchange_wait(gather[grp], after, False, "gather_wait_a", masks=_FIRST_HOP_MASKS)
            fwd, fwd_tok = _forward_start(lands, "gather_forward_start_a")
            lands = _forward_wait(fwd, fwd_tok, "gather_forward_wait_a")
        else:
            thru, lands = _exchange_wait(gather[grp], after, False, "gather_wait_" + grp)
        full = {}
        for n, t, g in zip(_GROUPS[grp], thru, lands):
            g = own_slot(g, t[None])
            full[n] = _unstack_cols(g) if n in _COL_SHARDED else g.reshape(N_DEV * g.shape[1], g.shape[2])
        if grp == "a":
            full["w_in"] = _pad_cols(full["w_in"])
        if grp == "c":
            full["conv_w"] = _unstack_cols(own_slot(lands[-1], thru[-1][None])[:, :3, :DFF // N_DEV])
        return full

    scatter = {}

    early_names = [n for n in _SMALL if n != "attn_norm_g"]
    sgp = {}

    def gput(grp, grads):
        if grp == "small":
            sgp.update(_small_grads_to_param_shapes(grads))
            scatter[grp], token = _exchange_start([_pack([sgp[n] for n in early_names])], False, "gather_small_start")
            return token
        stacked = [grads[n] if n == "w_up" else
                   _stack_cols(grads[n]) if n in _COL_SHARDED else grads[n].reshape(N_DEV, -1, D) for n in _GROUPS[grp]]
        scatter[grp], token = _exchange_start(stacked, True, "scatter_start_" + grp)
        return token

    loss, grad_x, sg = _local_step(x[0], loss_target[0], wget, small, gput)
    loss = lax.psum(loss, MESH_AXES)

    sgp.update(_small_grads_to_param_shapes(sg))
    small_names = [n for n in _SMALL]
    (norm_all,) = _exchange([jnp.pad(sgp["attn_norm_g"].reshape(1, D), ((0, 7), (0, 0)))], False, "gather_norm_grad")
    thru, (small_all,) = _exchange_wait(scatter["small"], norm_all, False, "gather_small_wait")
    small_all = own_slot(small_all, thru[0][None])

    outs_g, outs_d, outs_m, outs_v = {}, {}, {}, {}
    for grp in ("c", "b", "a"):
        thru, lands = _exchange_wait(scatter[grp], norm_all, True, "scatter_wait_" + grp)
        for n, t, pt in zip(_GROUPS[grp], thru, lands):
            pt = own_slot(pt, lax.dynamic_slice_in_dim(t, me, 1, axis=0))
            outs_g[n], outs_d[n], outs_m[n], outs_v[n] = _adam(pt, wv[n], mv[n], vv[n], "adam_" + n)

    sizes = [int(math.prod(sgp[n].shape)) for n in early_names]
    offs = [0]
    for s in sizes:
        offs.append(offs[-1] + s)

    def local_part(n, a):
        if n == "conv_w":
            return lax.dynamic_slice(a, (0, me * (DFF // N_DEV)), (3, DFF // N_DEV))
        return a

    rows = small_all.shape[1]

    def sum_fn(cc, rr, pb, nb_):
        g, gn = pb[0], nb_[0]
        for d in range(1, N_DEV):
            g, gn = g + pb[d], gn + nb_[d]
        return g, gn

    gsum, gnorm = _ew(sum_fn, [(small_all, (N_DEV, rows, 128), lambda cc, rr: (0, 0, 0)),
                               (norm_all, (N_DEV, 8, D), lambda cc, rr: (0, 0, 0))],
                      [(SDS((rows, 128), F32), (rows, 128), lambda cc, rr: (0, 0), None),
                       (SDS((8, D), F32), (8, D), lambda cc, rr: (0, 0), None)], (1, 1), "sum_small_grads")
    gflat = gsum.reshape(-1)
    gsmall = {n: local_part(n, gflat[offs[i]:offs[i + 1]].reshape(sgp[n].shape)) for i, n in enumerate(early_names)}
    gsmall["attn_norm_g"] = gnorm[0]
    as2d = lambda a: a.reshape(1, -1) if a.ndim == 1 else a.reshape(a.shape[0], -1)
    n_small = len(small_names)

    def adam_small(*refs):
        for i in range(n_small):
            g_ref, w_ref, m_ref, v_ref = refs[4 * i:4 * i + 4]
            outs = refs[4 * n_small + 3 * i:4 * n_small + 3 * i + 3]
            for o_ref, val in zip(outs, _adam_math(g_ref[...], w_ref[...], m_ref[...], v_ref[...])):
                o_ref[...] = val

    small_ins = [as2d(t[n]) for n in small_names for t in (gsmall, wv, mv, vv)]
    small_outs = pl.pallas_call(adam_small, name="adam_small",
                                out_shape=[SDS(as2d(wv[n]).shape, F32) for n in small_names for _ in range(3)])(*small_ins)
    for i, n in enumerate(small_names):
        sd, sm, sv = (t.reshape(wv[n].shape) for t in small_outs[3 * i:3 * i + 3])
        outs_g[n], outs_d[n], outs_m[n], outs_v[n] = gsmall[n], sd, sm, sv

    lead = lambda n, a: a if args[n].ndim == 1 else a[None]
    grad_x = grad_x[None]
    return (loss, grad_x, *[lead(n, outs_g[n]) for n in _WEIGHTS], *[lead(n, outs_d[n]) for n in _WEIGHTS],
            *[lead(n, outs_m[n]) for n in _WEIGHTS], *[lead(n, outs_v[n]) for n in _WEIGHTS])
```

```python
import functools
import math

import jax
import jax.numpy as jnp
from jax import lax
from jax.experimental import pallas as pl
from jax.experimental.pallas import tpu as pltpu

F32 = jnp.float32
BF16 = jnp.bfloat16
SDS = jax.ShapeDtypeStruct

N_DEV = 8
D = 2048
NQ, NKV, HD = 16, 2, 64
AW = NQ * HD
BLK = 128
SW, G, H, P = 512, 32, 16, 64
NS = G * P
DFF = 5632
INC = AW + 2 * NKV * HD + SW + 2 * D
C_K, C_U, C_PAD = AW, AW + 2 * NKV * HD, AW + 2 * NKV * HD + SW
C_GA, C_GS, INP = D, 2 * D, 3 * D
RMS_EPS = 1e-6
NEG_BIG = -1e30
ADAM_LR, ADAM_B1, ADAM_B2, ADAM_EPS, ADAM_WD, ADAM_STEP = 0.001, 0.9, 0.999, 1e-08, 0.01, 10
NSEG = 8
VMEM_CAP_MB = 60
MESH_AXES = ("x", "y", "c")


def _pad_cols(a):
    zeros = jnp.zeros(a.shape[:-1] + (C_GA - C_PAD,), a.dtype)
    return jnp.concatenate([a[..., :C_PAD], zeros, a[..., C_PAD:]], axis=-1)


def _unpad_cols(a):
    return jnp.concatenate([a[..., :C_PAD], a[..., C_GA:]], axis=-1)


def _cparams(sem, vmem_mb):
    return pltpu.CompilerParams(dimension_semantics=sem, vmem_limit_bytes=min(int(vmem_mb), VMEM_CAP_MB) << 20)


LANES = 128


def _tile(dim, pref):
    if dim <= pref:
        return dim
    for t in range(pref - pref % LANES, 0, -LANES):
        if dim % t == 0:
            return t
    raise ValueError(f"no tile for {dim}")


def _mm(a, b, *, ta=False, tb=False, bias=None, res=None, out_dtype=F32, tm=1024, tn=1024, tk=3072, name,
        a2=None, b2=None, extras=(), epilogue=None, outs=None, ep_cols=None, stack_out=False, n_cols=None,
        b_col_off=0, b2_col_off=0):
    m, k = (a.shape[1], a.shape[0]) if ta else a.shape
    n = n_cols or (b.shape[0] if tb else b.shape[1])
    assert (b.shape[1] if tb else b.shape[0]) == k, (a.shape, b.shape, ta, tb)
    tm, tn, tk = _tile(m, tm), _tile(n, tn), _tile(k, tk)
    nk = k // tk
    dims = (((0 if ta else 1,), (1 if tb else 0,)), ((), ()))
    has_bias, has_res, has_b2 = bias is not None, res is not None, b2 is not None
    has_a2 = a2 is not None
    assert not (has_b2 and (nk > 1 or ta or tb)) and not (has_a2 and not has_b2)
    if epilogue is None:
        outs = [(SDS((n // tn, m, tn) if stack_out else (m, n), out_dtype), "tile")]
    n_ex, n_out = len(extras), len(outs)
    tcn = tn if (epilogue is None or nk > 1 or ep_cols is None) else _tile(tn, ep_cols)

    def body(*refs):
        a_ref, b_ref = refs[0], refs[1]
        pos = 2
        a2_ref = refs[pos] if has_a2 else a_ref
        pos += has_a2
        b2_ref = refs[pos] if has_b2 else None
        pos += has_b2
        bias_ref = refs[pos] if has_bias else None
        pos += has_bias
        res_ref = refs[pos] if has_res else None
        pos += has_res
        ex_refs = refs[pos:pos + n_ex]
        o_refs = refs[pos + n_ex:pos + n_ex + n_out]
        i = pl.program_id(1)

        def product(rhs_ref, cols=None, lhs=None):
            rhs = rhs_ref[...] if cols is None else (rhs_ref[cols, :] if tb else rhs_ref[:, cols])
            lhs = a_ref[...].astype(BF16) if lhs is None else lhs
            return lax.dot_general(lhs, rhs.astype(BF16), dims, preferred_element_type=F32)

        def finish(r, cols):
            if has_bias:
                r = r + bias_ref[:, cols]
            if has_res:
                r = r + res_ref[:, cols].astype(F32)
            if epilogue is None:
                o_refs[0][:, cols] = r.astype(o_refs[0].dtype)
                return
            r2 = None
            if has_b2:
                r2 = jnp.dot(a2_ref[...].astype(BF16), b2_ref[:, cols].astype(BF16), preferred_element_type=F32)
            vals = epilogue(i, cols, r, r2, *ex_refs)
            for o_ref, v, (_, kind) in zip(o_refs, vals, outs):
                if kind == "tile":
                    o_ref[:, cols] = v.astype(o_ref.dtype)
                else:
                    @pl.when(i == 0)
                    def _(o_ref=o_ref, v=v):
                        o_ref[:, cols] = v.astype(o_ref.dtype)

                    @pl.when(i > 0)
                    def _(o_ref=o_ref, v=v):
                        o_ref[:, cols] += v.astype(o_ref.dtype)

        if nk == 1:
            lhs = a_ref[...].astype(BF16)
            for c0 in range(0, tn, tcn):
                cols = pl.ds(c0, tcn)
                finish(product(b_ref, cols, lhs), cols)
            return
        whole = pl.ds(0, tn)
        acc_ref = refs[-1]
        kk = pl.program_id(2)

        @pl.when(kk == 0)
        def _():
            acc_ref[...] = product(b_ref)

        @pl.when(jnp.logical_and(kk > 0, kk < nk - 1))
        def _():
            acc_ref[...] += product(b_ref)

        @pl.when(kk == nk - 1)
        def _():
            finish(acc_ref[...] + product(b_ref), whole)

    b_spec = (pl.BlockSpec((tn, tk), lambda j, i, kk: (j + b_col_off, kk)) if tb else
              pl.BlockSpec((tk, tn), lambda j, i, kk: (kk, j + b_col_off)))
    ins = [a, b]
    in_specs = [pl.BlockSpec((tk, tm), lambda j, i, kk: (kk, i)) if ta else pl.BlockSpec((tm, tk), lambda j, i, kk: (i, kk)),
                b_spec]
    tile_spec = pl.BlockSpec((tm, tn), lambda j, i, kk: (i, j))
    byt = 2 * tm * tk * a.dtype.itemsize + 2 * tk * tn * b.dtype.itemsize
    byt += (2 + has_b2) * 4 * tm * tn
    if has_a2:
        ins.append(a2)
        in_specs.append(pl.BlockSpec((tm, a2.shape[1]), lambda j, i, kk: (i, 0)))
        byt += 2 * tm * a2.shape[1] * a2.dtype.itemsize
    if has_b2:
        ins.append(b2)
        in_specs.append(pl.BlockSpec((b2.shape[0], tn), lambda j, i, kk: (0, j + b2_col_off)))
        byt += 2 * b2.shape[0] * tn * b2.dtype.itemsize
    if has_bias:
        ins.append(bias)
        in_specs.append(pl.BlockSpec((1, tn), lambda j, i, kk: (0, j)))
    if has_res:
        ins.append(res)
        in_specs.append(tile_spec)
        byt += 2 * tm * tn * res.dtype.itemsize
    for arr, kind, arg in extras:
        ins.append(arr)
        if kind == "tile":
            in_specs.append(pl.BlockSpec((tm, tn), lambda j, i, kk, arg=arg: (i, j + arg)))
            byt += 2 * tm * tn * arr.dtype.itemsize + 4 * tm * tn
        elif kind == "col":
            in_specs.append(pl.BlockSpec((arr.shape[0], tn), lambda j, i, kk, arg=arg: (0, j + arg)))
        else:
            in_specs.append(pl.BlockSpec(arg[0], lambda j, i, kk, im=arg[1]: im(j, i)))
    out_specs = []
    for sds, kind in outs:
        if kind == "tile":
            out_specs.append(pl.BlockSpec((None, tm, tn), lambda j, i, kk: (j, i, 0)) if stack_out else tile_spec)
            byt += 2 * tm * tn * jnp.dtype(sds.dtype).itemsize
        else:
            out_specs.append(pl.BlockSpec((sds.shape[0], tn), lambda j, i, kk: (0, j)))
    res_ = pl.pallas_call(
        body, out_shape=tuple(o[0] for o in outs), grid=(n // tn, m // tm, nk), in_specs=in_specs,
        out_specs=tuple(out_specs), scratch_shapes=[pltpu.VMEM((tm, tn), F32)] if nk > 1 else [], name=name,
        compiler_params=_cparams(("arbitrary", "arbitrary", "arbitrary"), byt / 2**20 + (8 if epilogue is None else 20)),
    )(*ins)
    return res_[0] if epilogue is None else res_


def _ew(fn, ins, outs, grid, name, vmem_mb=40, into=None):
    n_in = len(ins)
    accs = [o[3] for o in outs]

    def body(*refs):
        c, r = pl.program_id(0), pl.program_id(1)
        vals = fn(c, r, *[ref[...].astype(F32) for ref in refs[:n_in]])
        for o_ref, v, acc in zip(refs[n_in + (into is not None):], vals, accs):
            if acc is None:
                o_ref[...] = v.astype(o_ref.dtype)
            else:
                first = (r == 0) if acc == "r" else jnp.logical_and(r == 0, c == 0)

                @pl.when(first)
                def _(o_ref=o_ref, v=v):
                    o_ref[...] = v.astype(o_ref.dtype)

                @pl.when(jnp.logical_not(first))
                def _(o_ref=o_ref, v=v):
                    o_ref[...] += v.astype(o_ref.dtype)

    in_specs = [pl.BlockSpec(bs, im) for _, bs, im in ins]
    args = [a for a, _, _ in ins]
    if into is not None:
        in_specs.append(pl.BlockSpec(memory_space=pl.ANY))
        args.append(into)
    res = pl.pallas_call(
        body, out_shape=tuple(o[0] for o in outs), grid=grid, in_specs=in_specs,
        out_specs=tuple(pl.BlockSpec(bs, im) for _, bs, im, _ in outs), name=name,
        input_output_aliases={} if into is None else {n_in: 0},
        compiler_params=_cparams(("arbitrary", "arbitrary"), vmem_mb),
    )(*args)
    return res


def _rc(tm, tc, coff=0):
    return (tm, tc), (lambda c, r: (r, c + coff))


def _col(rows, tc, coff=0):
    return (rows, tc), (lambda c, r: (0, c + coff))


def _gelu(x):
    return 0.5 * x * (1.0 + lax.erf(x * (2.0 ** -0.5)))


def _gelu_and_grad(x):
    cdf = 0.5 * (1.0 + lax.erf(x * (2.0 ** -0.5)))
    return x * cdf, cdf + x * jnp.exp(-0.5 * x * x) * (1.0 / math.sqrt(2.0 * math.pi))


def _gelu_grad(x):
    return _gelu_and_grad(x)[1]


def _sigmoid(x):
    return 1.0 / (1.0 + jnp.exp(-x))


def _shift_rows(x, halo, s):
    rolled = pltpu.roll(x, s, 0)
    row8 = lax.broadcasted_iota(jnp.int32, halo.shape, 0)
    head = jnp.where(row8 < s, pltpu.roll(halo, s, 0), rolled[0:8])
    return jnp.concatenate([head, rolled[8:]], axis=0)


def _shift_rows_up(x, halo, s):
    tm = x.shape[0]
    rolled = pltpu.roll(x, tm - s, 0)
    row8 = lax.broadcasted_iota(jnp.int32, halo.shape, 0)
    tail = jnp.where(row8 >= 8 - s, pltpu.roll(halo, 8 - s, 0), rolled[tm - 8:])
    return jnp.concatenate([rolled[:tm - 8], tail], axis=0)


def _rmsnorm_fwd(x, g, name, tm=512):
    L = x.shape[0]

    def fn(c, r, xb, gb):
        rstd = lax.rsqrt(jnp.mean(xb * xb, axis=-1, keepdims=True) + RMS_EPS)
        return ((xb * rstd) * gb,)

    return _ew(fn, [(x, *_rc(tm, D)), (g, *_col(1, D))], [(SDS((L, D), BF16), *_rc(tm, D), None)], (1, L // tm), name)[0]


def _rmsnorm_bwd(dh, x, g, dres, name, tm=512):
    L = x.shape[0]

    def fn(c, r, dhb, xb, gb, drb):
        rstd = lax.rsqrt(jnp.mean(xb * xb, axis=-1, keepdims=True) + RMS_EPS)
        y = xb * rstd
        dy = dhb * gb
        dx = rstd * (dy - y * jnp.mean(dy * y, axis=-1, keepdims=True))
        return drb + dx, jnp.sum(dhb * y, axis=0, keepdims=True)

    return _ew(fn, [(dh, *_rc(tm, D)), (x, *_rc(tm, D)), (g, *_col(1, D)), (dres, *_rc(tm, D))],
               [(SDS((L, D), F32), *_rc(tm, D), None), (SDS((1, D), F32), *_col(1, D), "all")], (1, L // tm), name,
               vmem_mb=56)


def _final_loss(x2, g, tgt, name, tm=512):
    L = x2.shape[0]

    def fn(c, r, xb, gb, tb):
        rstd = lax.rsqrt(jnp.mean(xb * xb, axis=-1, keepdims=True) + RMS_EPS)
        y = xb * rstd
        err = y * gb - tb
        dout = err * (1.0 / D)
        dy = dout * gb
        dx = rstd * (dy - y * jnp.mean(dy * y, axis=-1, keepdims=True))
        return dx, dx, jnp.sum(err * err, axis=0, keepdims=True) * (0.5 / D), jnp.sum(dout * y, axis=0, keepdims=True)

    return _ew(fn, [(x2, *_rc(tm, D)), (g, *_col(1, D)), (tgt, *_rc(tm, D))],
               [(SDS((L, D), F32), *_rc(tm, D), None), (SDS((L, D), BF16), *_rc(tm, D), None),
                (SDS((1, D), F32), *_col(1, D), "all"),
                (SDS((1, D), F32), *_col(1, D), "all")], (1, L // tm), name, vmem_mb=56)


def _slope(h):
    return 2.0 ** (-8.0 * (h + 1) / NQ)


def _attn_bias():
    qi = lax.broadcasted_iota(jnp.int32, (BLK, 2 * BLK), 0)
    si = lax.broadcasted_iota(jnp.int32, (BLK, 2 * BLK), 1)
    dist = qi + BLK - si
    band = (dist >= 0) & (dist < BLK)
    slopes = jnp.asarray([_slope(h) for h in range(NQ)], F32)[:, None, None]
    alibi = -slopes * dist.astype(F32)[None]
    return jnp.stack([jnp.where((band & (si >= BLK))[None], alibi, NEG_BIG), jnp.where(band[None], alibi, NEG_BIG)])


def _attn_kv(kvc, kvp):
    kv = jnp.concatenate([kvp, kvc], axis=0).astype(F32)
    lo = lax.broadcasted_iota(jnp.int32, (2 * BLK, 128), 1) < HD

    def halves(t):
        tr = pltpu.roll(t, HD, 1)
        z = jnp.zeros_like(t)
        return {(0, 0): jnp.where(lo, t, z).astype(BF16), (0, 1): jnp.where(lo, z, tr).astype(BF16),
                (1, 0): jnp.where(lo, tr, z).astype(BF16), (1, 1): jnp.where(lo, z, t).astype(BF16)}

    return halves(kv[:, :128]), halves(kv[:, 128:])


_NT = (((1,), (1,)), ((), ()))
_TN = (((0,), (0,)), ((), ()))
_ATTN_SPECS = [pl.BlockSpec(memory_space=pltpu.SMEM),
               pl.BlockSpec((None, NQ, BLK, 2 * BLK), lambda n: (jnp.minimum(n, 1), 0, 0, 0)),
               pl.BlockSpec((BLK, AW), lambda n: (n, 0)),
               pl.BlockSpec((BLK, 256), lambda n: (n, C_K // 256)),
               pl.BlockSpec((BLK, 256), lambda n: (jnp.maximum(n - 1, 0), C_K // 256))]


def _attn_scores(q_ref, bias_ref, kmat, sc_ref):
    for j in range(NQ // 2):
        qs = q_ref[:, 128 * j:128 * (j + 1)] * (HD ** -0.5)
        for e in range(2):
            h = 2 * j + e
            sc_ref[h] = lax.dot_general(qs, kmat[(j // (NQ // 4), e)], _NT, preferred_element_type=F32) + bias_ref[h]


def _softmax_with_sink(s, sink):
    m = jnp.maximum(jnp.max(s, axis=-1, keepdims=True), sink)
    p = jnp.exp(s - m)
    esink = jnp.exp(sink - m)
    den = jnp.sum(p, axis=-1, keepdims=True) + esink
    return p / den, esink / den


def _attn_fwd(projb, sinks, bias, name):
    L = projb.shape[0]

    def body(s_ref, bias_ref, q_ref, kvc_ref, kvp_ref, o_ref, sc_ref, pr_ref):
        kmat, vmat = _attn_kv(kvc_ref[...], kvp_ref[...])
        _attn_scores(q_ref, bias_ref, kmat, sc_ref)
        for h in range(NQ):
            pr_ref[h] = _softmax_with_sink(sc_ref[h], s_ref[0, h])[0].astype(BF16)
        for j in range(NQ // 2):
            g = j // (NQ // 4)
            acc = jnp.dot(pr_ref[2 * j], vmat[(g, 0)], preferred_element_type=F32)
            acc = acc + jnp.dot(pr_ref[2 * j + 1], vmat[(g, 1)], preferred_element_type=F32)
            o_ref[:, 128 * j:128 * (j + 1)] = acc.astype(BF16)

    return pl.pallas_call(
        body, out_shape=SDS((L, AW), BF16), grid=(L // BLK,), in_specs=_ATTN_SPECS,
        out_specs=pl.BlockSpec((BLK, AW), lambda n: (n, 0)), name=name,
        scratch_shapes=[pltpu.VMEM((NQ, BLK, 2 * BLK), F32), pltpu.VMEM((NQ, BLK, 2 * BLK), BF16)],
        compiler_params=_cparams(("arbitrary",), 32),
    )(sinks, bias, projb, projb, projb)


def _attn_bwd(projb, sinks, bias, dattn, name):
    L = projb.shape[0]

    def body(s_ref, bias_ref, q_ref, kvc_ref, kvp_ref, do_ref, dq_ref, dcur_ref, dprev_ref, dsink_ref,
             sc_ref, dp_ref, ds_ref, pr_ref, qm_ref, dm_ref):
        n = pl.program_id(0)
        kmat, vmat = _attn_kv(kvc_ref[...], kvp_ref[...])
        _attn_scores(q_ref, bias_ref, kmat, sc_ref)
        for h in range(NQ):
            j, e = h // 2, h % 2
            dp_ref[h] = lax.dot_general(do_ref[:, 128 * j:128 * (j + 1)], vmat[(j // (NQ // 4), e)], _NT,
                                        preferred_element_type=F32)
        lane = lax.broadcasted_iota(jnp.int32, (1, 128), 1)
        dsv = jnp.zeros((1, 128), F32)
        for h in range(NQ):
            p, psink = _softmax_with_sink(sc_ref[h], s_ref[0, h])
            dp = dp_ref[h]
            drow = jnp.sum(p * dp, axis=-1, keepdims=True)
            ds_ref[h] = (p * (dp - drow)).astype(BF16)
            pr_ref[h] = p.astype(BF16)
            dsv = dsv + jnp.where(lane == h, -jnp.sum(psink * drow, axis=0, keepdims=True), 0.0)
        lo128 = lax.broadcasted_iota(jnp.int32, (BLK, 128), 1) < HD
        for j in range(NQ // 2):
            g = j // (NQ // 4)
            qs = q_ref[:, 128 * j:128 * (j + 1)] * (HD ** -0.5)
            dop = do_ref[:, 128 * j:128 * (j + 1)]
            zb = jnp.zeros_like(qs)
            dqp = jnp.zeros((BLK, 128), F32)
            for e in range(2):
                h = 2 * j + e
                half = lo128 if e == 0 else jnp.logical_not(lo128)
                dqp = dqp + jnp.dot(ds_ref[h], kmat[(g, e)], preferred_element_type=F32)
                qm_ref[h] = jnp.where(half, qs, zb)
                dm_ref[h] = jnp.where(half, dop, zb)
            dq_ref[:, 128 * j:128 * (j + 1)] = (dqp * (HD ** -0.5)).astype(BF16)
        hk = NQ // NKV
        rows = lambda ref, g: ref[g * hk:(g + 1) * hk].reshape(hk * BLK, ref.shape[-1])
        dk = [lax.dot_general(rows(ds_ref, g), rows(qm_ref, g), _TN, preferred_element_type=F32) for g in range(NKV)]
        dv = [lax.dot_general(rows(pr_ref, g), rows(dm_ref, g), _TN, preferred_element_type=F32) for g in range(NKV)]
        lo256 = lax.broadcasted_iota(jnp.int32, (2 * BLK, 128), 1) < HD
        tot = [t + pltpu.roll(t, HD, 1) for t in (dk[0], dk[1], dv[0], dv[1])]
        dkv = jnp.concatenate([jnp.where(lo256, tot[0], tot[1]), jnp.where(lo256, tot[2], tot[3])], axis=1)
        dprev_ref[...] = dkv[:BLK]
        dcur_ref[...] = dkv[BLK:]

        @pl.when(n == 0)
        def _():
            dsink_ref[...] = dsv

        @pl.when(n > 0)
        def _():
            dsink_ref[...] += dsv

    tile = (NQ, BLK, 2 * BLK)
    return pl.pallas_call(
        body, out_shape=(SDS((L, AW), BF16), SDS((L, 256), F32), SDS((L, 256), F32), SDS((1, 128), F32)), grid=(L // BLK,),
        in_specs=_ATTN_SPECS + [pl.BlockSpec((BLK, AW), lambda n: (n, 0))],
        out_specs=(pl.BlockSpec((BLK, AW), lambda n: (n, 0)), pl.BlockSpec((BLK, 256), lambda n: (n, 0)),
                   pl.BlockSpec((BLK, 256), lambda n: (n, 0)), pl.BlockSpec((1, 128), lambda n: (0, 0))),
        scratch_shapes=[pltpu.VMEM(tile, F32), pltpu.VMEM(tile, F32), pltpu.VMEM(tile, BF16), pltpu.VMEM(tile, BF16),
                        pltpu.VMEM((NQ, BLK, 128), BF16), pltpu.VMEM((NQ, BLK, 128), BF16)],
        name=name, compiler_params=_cparams(("arbitrary",), 40),
    )(sinks, bias, projb, projb, projb, dattn)


def _disc(a_re, a_im, logdt, b_re, b_im):
    dt = jnp.exp(logdt)
    mag = jnp.exp(a_re * dt)
    ab_re = mag * jnp.cos(a_im * dt)
    ab_im = mag * jnp.sin(a_im * dt)
    nr = ab_re - 1.0
    ni = ab_im
    den = a_re * a_re + a_im * a_im
    z_re = (nr * a_re + ni * a_im) / den
    z_im = (ni * a_re - nr * a_im) / den
    return ab_re, ab_im, z_re * b_re - z_im * b_im, z_re * b_im + z_im * b_re


def _group_mask():
    row = lax.broadcasted_iota(jnp.int32, (SW, NS), 0) // H
    col = lax.broadcasted_iota(jnp.int32, (SW, NS), 1) // P
    return row == col


def _block_diag(re, im):
    mask = _group_mask()
    z = jnp.zeros((SW, NS), F32)
    return jnp.concatenate([jnp.where(mask, jnp.tile(re, (G, 1)), z), jnp.where(mask, jnp.tile(im, (G, 1)), z)], axis=1)


def _block_diag_t(big):
    mask = _group_mask()
    z = jnp.zeros((SW, NS), F32)
    re = jnp.sum(jnp.where(mask, big[:, :NS], z).reshape(G, H, NS), axis=0)
    im = jnp.sum(jnp.where(mask, big[:, NS:], z).reshape(G, H, NS), axis=0)
    return re, im


def _ssm_prep(a_re, a_im, logdt, b_re, b_im, c_re, c_im, name):
    def body(are, aim, ldt, bre, bim, cre, cim, ab_ref, bm_ref, cm_ref):
        ab_re, ab_im, bb_re, bb_im = _disc(are[...], aim[...], ldt[...], bre[...], bim[...])
        ab_ref[...] = jnp.concatenate([ab_re, ab_im], axis=1)
        bm_ref[...] = _block_diag(bb_re, bb_im).astype(BF16)
        cm_ref[...] = _block_diag(cre[...], -cim[...]).astype(BF16)

    return pl.pallas_call(body, out_shape=(SDS((1, 2 * NS), F32), SDS((SW, 2 * NS), BF16), SDS((SW, 2 * NS), BF16)),
                          name=name, compiler_params=pltpu.CompilerParams(vmem_limit_bytes=48 << 20),
                          )(a_re, a_im, logdt, b_re, b_im, c_re, c_im)


def _ssm_param_bwd(a_re, a_im, logdt, b_re, b_im, dab8, dbm, dcm, name):
    def body(are, aim, ldt, bre, bim, dab_ref, dbm_ref, dcm_ref, o_are, o_aim, o_ldt, o_bre, o_bim, o_cre, o_cim):
        dab = jnp.sum(dab_ref[...], axis=0, keepdims=True)
        dbb_re, dbb_im = _block_diag_t(dbm_ref[...])
        _, vjp = jax.vjp(_disc, are[...], aim[...], ldt[...], bre[...], bim[...])
        d_are, d_aim, d_ldt, d_bre, d_bim = vjp((dab[:, :NS], dab[:, NS:], dbb_re, dbb_im))
        o_are[...], o_aim[...], o_ldt[...], o_bre[...], o_bim[...] = d_are, d_aim, d_ldt, d_bre, d_bim
        dc_re, dc_imn = _block_diag_t(dcm_ref[...])
        o_cre[...] = dc_re
        o_cim[...] = -dc_imn

    v1, vh = SDS((1, NS), F32), SDS((H, NS), F32)
    return pl.pallas_call(body, out_shape=(v1, v1, v1, vh, vh, vh, vh), name=name,
                          compiler_params=pltpu.CompilerParams(vmem_limit_bytes=56 << 20),
                          )(a_re, a_im, logdt, b_re, b_im, dab8, dbm, dcm)


def _ssm_scan(src, wmat, ab, *, reverse, ends=None, xs=None, init=None, wproj=None, name, tk=32):
    L = src.shape[0]
    rows = NSEG * tk
    nch = L // rows
    seg_len = L // NSEG
    n_sq = int(math.log2(seg_len))
    assert 2 ** n_sq == seg_len and L % rows == 0
    first_pass = ends is None
    with_dab = (not first_pass) and reverse
    with_proj = wproj is not None
    assert not (with_proj and first_pass)
    slab = 512
    n_slab = NS // slab

    def body(*refs):
        src_ref, w_ref, ab_ref = refs[:3]
        pos = 3
        if not first_pass:
            ends_ref = refs[pos]
            pos += 1
        if with_dab:
            xs_ref, xsh_ref, init_ref = refs[pos:pos + 3]
            pos += 3
        if with_proj:
            wproj_ref = refs[pos]
            pos += 1
        if first_pass:
            (e_ref,) = refs[pos:pos + 1]
            pos += 1
        else:
            st_out_ref, aux_ref = refs[pos:pos + 2]
            pos += 2
        if with_proj:
            proj_ref = refs[pos]
            pos += 1
        buf_ref, st_ref = refs[pos:pos + 2]
        i = pl.program_id(0)
        a_re = ab_ref[:, :NS]
        a_im = -ab_ref[:, NS:] if reverse else ab_ref[:, NS:]

        @pl.when(i == 0)
        def _():
            if first_pass:
                st_ref[...] = jnp.zeros_like(st_ref)
            else:
                pr, pi = a_re, a_im
                for _ in range(n_sq):
                    pr, pi = pr * pr - pi * pi, 2.0 * pr * pi
                zr = jnp.zeros((1, NS), F32)
                cr, ci = zr, zr
                order = list(range(NSEG - 1, -1, -1)) if reverse else list(range(NSEG))
                st_ref[order[0]:order[0] + 1, :] = jnp.zeros((1, 2 * NS), F32)
                for jprev, j in zip(order[:-1], order[1:]):
                    er, ei = ends_ref[jprev:jprev + 1, :NS], ends_ref[jprev:jprev + 1, NS:]
                    cr, ci = er + pr * cr - pi * ci, ei + pr * ci + pi * cr
                    st_ref[j:j + 1, :NS] = cr
                    st_ref[j:j + 1, NS:] = ci
                if not reverse:
                    aux_ref[...] = st_ref[...]
                else:
                    aux_ref[...] = jnp.zeros_like(aux_ref)

        buf_ref[...] = jnp.dot(src_ref[...].astype(BF16), w_ref[...], preferred_element_type=F32)

        for s in range(n_slab):
            re_sl, im_sl = pl.ds(s * slab, slab), pl.ds(NS + s * slab, slab)
            ar = jnp.broadcast_to(a_re[:, s * slab:(s + 1) * slab], (NSEG, slab))
            ai = jnp.broadcast_to(a_im[:, s * slab:(s + 1) * slab], (NSEG, slab))

            def step(t, carry, re_sl=re_sl, im_sl=im_sl, ar=ar, ai=ai):
                k = (tk - 1 - t) if reverse else t
                r0 = pl.multiple_of(k * NSEG, NSEG)
                xr, xi = carry[0], carry[1]
                nr = ar * xr - ai * xi + buf_ref[pl.ds(r0, NSEG), re_sl]
                ni = ar * xi + ai * xr + buf_ref[pl.ds(r0, NSEG), im_sl]
                if not first_pass:
                    buf_ref[pl.ds(r0, NSEG), re_sl] = nr
                    buf_ref[pl.ds(r0, NSEG), im_sl] = ni
                if not with_dab:
                    return nr, ni
                rp = pl.multiple_of((k - 1) * NSEG, NSEG)
                xpr, xpi = xs_ref[pl.ds(rp, NSEG), re_sl], xs_ref[pl.ds(rp, NSEG), im_sl]
                return nr, ni, carry[2] + nr * xpr + ni * xpi, carry[3] + ni * xpr - nr * xpi

            carry = (st_ref[:, re_sl], st_ref[:, im_sl])
            if with_dab:
                z = jnp.zeros((NSEG, slab), F32)
                carry = lax.fori_loop(0, tk - 1, step, carry + (z, z))
                xr, xi, dr, di = carry
                nr = ar * xr - ai * xi + buf_ref[pl.ds(0, NSEG), re_sl]
                ni = ar * xi + ai * xr + buf_ref[pl.ds(0, NSEG), im_sl]
                buf_ref[pl.ds(0, NSEG), re_sl] = nr
                buf_ref[pl.ds(0, NSEG), im_sl] = ni
                at_start = i == nch - 1
                xpr = jnp.where(at_start, init_ref[:, re_sl], xsh_ref[:, re_sl])
                xpi = jnp.where(at_start, init_ref[:, im_sl], xsh_ref[:, im_sl])
                aux_ref[:, re_sl] += dr + nr * xpr + ni * xpi
                aux_ref[:, im_sl] += di + ni * xpr - nr * xpi
                carry = (nr, ni)
            else:
                carry = lax.fori_loop(0, tk, step, carry)
            st_ref[:, re_sl] = carry[0]
            st_ref[:, im_sl] = carry[1]

        if first_pass:
            @pl.when(i == nch - 1)
            def _():
                e_ref[...] = st_ref[...]
        else:
            st_out_ref[...] = buf_ref[...].astype(st_out_ref.dtype)
            if with_proj:
                proj_ref[...] = lax.dot_general(buf_ref[...].astype(BF16), wproj_ref[...], _NT, preferred_element_type=F32)

    chunk = (lambda i: (nch - 1 - i, 0)) if reverse else (lambda i: (i, 0))
    whole = lambda i: (0, 0)
    ins = [src, wmat, ab]
    once = pl.Buffered(1)
    in_specs = [pl.BlockSpec((rows, SW), chunk), pl.BlockSpec((SW, 2 * NS), whole, pipeline_mode=once),
                pl.BlockSpec((1, 2 * NS), whole)]
    small = SDS((NSEG, 2 * NS), F32)
    small_spec = pl.BlockSpec((NSEG, 2 * NS), whole)
    if not first_pass:
        ins.append(ends)
        in_specs.append(small_spec)
    if with_dab:
        ins += [xs, xs, init]
        in_specs += [pl.BlockSpec((rows, 2 * NS), chunk),
                     pl.BlockSpec((NSEG, 2 * NS), lambda i: (jnp.maximum((nch - 1 - i) * tk - 1, 0), 0)),
                     small_spec]
    if with_proj:
        ins.append(wproj)
        in_specs.append(pl.BlockSpec((SW, 2 * NS), whole, pipeline_mode=once))
    if first_pass:
        out_shape, out_specs = small, small_spec
    else:
        out_shape = (SDS((L, 2 * NS), BF16 if reverse else F32), small)
        out_specs = (pl.BlockSpec((rows, 2 * NS), chunk), small_spec)
        if with_proj:
            out_shape += (SDS((L, SW), F32),)
            out_specs += (pl.BlockSpec((rows, SW), chunk),)
    return pl.pallas_call(
        body, out_shape=out_shape, grid=(nch,), in_specs=in_specs, out_specs=out_specs,
        scratch_shapes=[pltpu.VMEM((rows, 2 * NS), F32), pltpu.VMEM((NSEG, 2 * NS), F32)], name=name,
        compiler_params=_cparams(("arbitrary",), 56),
    )(*ins)


def _to_segments(a):
    L, c = a.shape
    return a.reshape(NSEG, L // NSEG, c).transpose(1, 0, 2).reshape(L, c)


def _from_segments(a):
    L, c = a.shape
    return a.reshape(L // NSEG, NSEG, c).transpose(1, 0, 2).reshape(L, c)


def _peer(x, y, c, m):
    return ((1 - x) if (m >> 2) & 1 else x, (1 - y) if (m >> 1) & 1 else y, (1 - c) if m & 1 else c)


def _dev_index(p):
    return 4 * p[0] + 2 * p[1] + p[2]


def _exchange(arrs, scatter, name):
    n = len(arrs)

    def body(*refs):
        ins, outs = refs[:n], refs[n:2 * n]
        send_sems, recv_sems, loc_sems = refs[2 * n:]
        x, y, c = lax.axis_index("x"), lax.axis_index("y"), lax.axis_index("c")
        me = _dev_index((x, y, c))

        def src(w, to):
            return ins[w].at[to] if scatter else ins[w]

        def local(w):
            return pltpu.make_async_copy(src(w, me), outs[w].at[me], loc_sems.at[w])

        def remote(w, m):
            peer = _peer(x, y, c, m)
            return pltpu.make_async_remote_copy(src_ref=src(w, _dev_index(peer)), dst_ref=outs[w].at[me],
                                                send_sem=send_sems.at[w, m - 1], recv_sem=recv_sems.at[w, m - 1],
                                                device_id=peer, device_id_type=pl.DeviceIdType.MESH)

        def arrival(w, m):
            peer = _peer(x, y, c, m)
            return pltpu.make_async_remote_copy(src_ref=src(w, me), dst_ref=outs[w].at[_dev_index(peer)],
                                                send_sem=send_sems.at[w, m - 1], recv_sem=recv_sems.at[w, m - 1],
                                                device_id=peer, device_id_type=pl.DeviceIdType.MESH)

        for w in range(n):
            local(w).start()
        for w in range(n):
            for m in range(1, N_DEV):
                remote(w, m).start()
        for w in range(n):
            for m in range(1, N_DEV):
                arrival(w, m).wait_recv()
        for w in range(n):
            for m in range(1, N_DEV):
                remote(w, m).wait_send()
        for w in range(n):
            local(w).wait()

    anyspec = pl.BlockSpec(memory_space=pl.ANY)
    out_shape = tuple(SDS(a.shape if scatter else (N_DEV,) + a.shape, a.dtype) for a in arrs)
    return pl.pallas_call(
        body, out_shape=out_shape, in_specs=[anyspec] * n, out_specs=tuple([anyspec] * n),
        scratch_shapes=[pltpu.SemaphoreType.DMA((n, N_DEV - 1)), pltpu.SemaphoreType.DMA((n, N_DEV - 1)),
                        pltpu.SemaphoreType.DMA((n,))],
        name=name, compiler_params=pltpu.CompilerParams(has_side_effects=True),
    )(*arrs)


_HBM = pl.BlockSpec(memory_space=pltpu.HBM)
_SEM = pl.BlockSpec(memory_space=pltpu.SEMAPHORE)
_EFFECT = pltpu.SideEffectType.DATAFLOW_SIDE_EFFECTING


def _sem_index(w, m):
    return w * (N_DEV - 1) + m - 1


_ALL_MASKS = tuple(range(1, N_DEV))
_CHIP_MASKS = (2, 4, 6)
_FIRST_HOP_MASKS = (1,) + _CHIP_MASKS


def _exchange_start(arrs, scatter, name, masks=_ALL_MASKS):
    n = len(arrs)
    lands = [lax.empty(a.shape if scatter else (N_DEV,) + a.shape, a.dtype) for a in arrs]

    def body(*refs):
        ins, zones = refs[:n], refs[n:2 * n]
        send_sems, recv_sems = refs[2 * n], refs[2 * n + 1]
        token = refs[-1]
        x, y, c = lax.axis_index("x"), lax.axis_index("y"), lax.axis_index("c")
        me = _dev_index((x, y, c))
        for w in range(n):
            for m in masks:
                peer = _peer(x, y, c, m)
                pltpu.make_async_remote_copy(
                    src_ref=ins[w].at[_dev_index(peer)] if scatter else ins[w], dst_ref=zones[w].at[me],
                    send_sem=send_sems.at[_sem_index(w, m)], recv_sem=recv_sems.at[_sem_index(w, m)],
                    device_id=peer, device_id_type=pl.DeviceIdType.MESH).start()
        token[...] = jnp.zeros_like(token)

    sems = pltpu.SemaphoreType.DMA((n * (N_DEV - 1),))
    res = pl.pallas_call(
        body, name=name,
        out_shape=(sems, sems, *[pltpu.HBM(a.shape, a.dtype) for a in arrs], *[pltpu.HBM(z.shape, z.dtype) for z in lands],
                   SDS((8, 128), F32)),
        in_specs=[_HBM] * (2 * n), out_specs=(_SEM, _SEM, *([_HBM] * (2 * n)), pl.BlockSpec(memory_space=pltpu.VMEM)),
        input_output_aliases={i: 2 + i for i in range(2 * n)},
        compiler_params=pltpu.CompilerParams(has_side_effects=_EFFECT),
    )(*[pltpu.with_memory_space_constraint(a, pltpu.HBM) for a in arrs],
      *[pltpu.with_memory_space_constraint(z, pltpu.HBM) for z in lands])
    return (res[0], res[1], list(res[2:2 + n]), list(res[2 + n:2 + 2 * n])), res[-1]


def _exchange_wait(handle, after, scatter, name, masks=_ALL_MASKS):
    send_sems, recv_sems, thru, lands = handle
    n = len(thru)

    def body(*refs):
        ins, zones = refs[:n], refs[n:2 * n]
        send_sems, recv_sems = refs[2 * n], refs[2 * n + 1]
        x, y, c = lax.axis_index("x"), lax.axis_index("y"), lax.axis_index("c")
        me = _dev_index((x, y, c))
        for w in range(n):
            for m in masks:
                peer = _peer(x, y, c, m)
                copy = pltpu.make_async_remote_copy(
                    src_ref=ins[w].at[me] if scatter else ins[w], dst_ref=zones[w].at[_dev_index(peer)],
                    send_sem=send_sems.at[_sem_index(w, m)], recv_sem=recv_sems.at[_sem_index(w, m)],
                    device_id=peer, device_id_type=pl.DeviceIdType.MESH)
                copy.wait_send()
                copy.wait_recv()

    res = pl.pallas_call(
        body, name=name,
        out_shape=(*[pltpu.HBM(a.shape, a.dtype) for a in thru], *[pltpu.HBM(z.shape, z.dtype) for z in lands]),
        in_specs=[_HBM] * (2 * n) + [_SEM, _SEM, pl.BlockSpec(memory_space=pl.ANY)], out_specs=tuple([_HBM] * (2 * n)),
        input_output_aliases={i: i for i in range(2 * n)},
        compiler_params=pltpu.CompilerParams(has_side_effects=_EFFECT),
    )(*thru, *lands, send_sems, recv_sems, after)
    return list(res[:n]), list(res[n:])


def _forward_start(zones, name):
    n = len(zones)

    def body(*refs):
        zs = refs[:n]
        send_sems, recv_sems = refs[n], refs[n + 1]
        token = refs[-1]
        x, y, c = lax.axis_index("x"), lax.axis_index("y"), lax.axis_index("c")
        for w in range(n):
            for m in _CHIP_MASKS:
                slot = zs[w].at[_dev_index(_peer(x, y, c, m))]
                pltpu.make_async_remote_copy(
                    src_ref=slot, dst_ref=slot, send_sem=send_sems.at[_sem_index(w, m)],
                    recv_sem=recv_sems.at[_sem_index(w, m)], device_id=(x, y, 1 - c),
                    device_id_type=pl.DeviceIdType.MESH).start()
        token[...] = jnp.zeros_like(token)

    sems = pltpu.SemaphoreType.DMA((n * (N_DEV - 1),))
    res = pl.pallas_call(
        body, name=name, out_shape=(sems, sems, *[pltpu.HBM(z.shape, z.dtype) for z in zones], SDS((8, 128), F32)),
        in_specs=[_HBM] * n, out_specs=(_SEM, _SEM, *([_HBM] * n), pl.BlockSpec(memory_space=pltpu.VMEM)),
        input_output_aliases={i: 2 + i for i in range(n)},
        compiler_params=pltpu.CompilerParams(has_side_effects=_EFFECT),
    )(*[pltpu.with_memory_space_constraint(z, pltpu.HBM) for z in zones])
    return (res[0], res[1], list(res[2:2 + n])), res[-1]


def _forward_wait(handle, after, name):
    send_sems, recv_sems, zones = handle
    n = len(zones)

    def body(*refs):
        zs = refs[:n]
        send_sems, recv_sems = refs[n], refs[n + 1]
        x, y, c = lax.axis_index("x"), lax.axis_index("y"), lax.axis_index("c")
        for w in range(n):
            for m in _CHIP_MASKS:
                copy = pltpu.make_async_remote_copy(
                    src_ref=zs[w].at[_dev_index(_peer(x, y, c, m))], dst_ref=zs[w].at[_dev_index(_peer(x, y, 1 - c, m))],
                    send_sem=send_sems.at[_sem_index(w, m)], recv_sem=recv_sems.at[_sem_index(w, m)],
                    device_id=(x, y, 1 - c), device_id_type=pl.DeviceIdType.MESH)
                copy.wait_send()
                copy.wait_recv()

    res = pl.pallas_call(
        body, name=name, out_shape=tuple(pltpu.HBM(z.shape, z.dtype) for z in zones),
        in_specs=[_HBM] * n + [_SEM, _SEM, pl.BlockSpec(memory_space=pl.ANY)], out_specs=tuple([_HBM] * n),
        input_output_aliases={i: i for i in range(n)},
        compiler_params=pltpu.CompilerParams(has_side_effects=_EFFECT),
    )(*zones, send_sems, recv_sems, after)
    return list(res)


def _adam_math(g, w, m, v):
    m = ADAM_B1 * m + (1.0 - ADAM_B1) * g
    v = ADAM_B2 * v + (1.0 - ADAM_B2) * (g * g)
    m_hat = m / (1.0 - ADAM_B1 ** ADAM_STEP)
    v_hat = v / (1.0 - ADAM_B2 ** ADAM_STEP)
    delta = -ADAM_LR * (m_hat / (jnp.sqrt(v_hat) + ADAM_EPS) + ADAM_WD * w)
    return delta, m, v


def _adam(parts, w, m, v, name, tr=128):
    r, c = w.shape
    tr = next(t for t in (tr, 64, 32, 16, 8) if r % t == 0)

    def fn(cc, rr, pb, wb, mb, vb):
        g = pb[0].astype(F32)
        for d in range(1, N_DEV):
            g = g + pb[d].astype(F32)
        delta, nm, nv = _adam_math(g, wb, mb, vb)
        return g, delta, nm, nv

    blk = ((tr, c), lambda cc, rr: (rr, 0))
    o = SDS((r, c), F32)
    return _ew(fn, [(parts, (N_DEV, tr, c), lambda cc, rr: (0, rr, 0)), (w, *blk), (m, *blk), (v, *blk)],
               [(o, *blk, None)] * 4, (1, r // tr), name)


_SHARDED = ("w_in", "w_glu", "w_branch_attn", "w_branch_ssm", "w_out", "w_up", "w_down")
_COL_SHARDED = ("w_in", "w_glu", "w_branch_attn", "w_branch_ssm", "w_up")
_GROUPS = {"a": ("w_in",), "b": ("w_glu", "w_branch_attn", "w_branch_ssm", "w_out"), "c": ("w_up", "w_down")}
_SMALL = ("attn_norm_g", "b_in", "attn_sinks", "ssm_a_re", "ssm_a_im", "ssm_log_dt", "ssm_b_re", "ssm_b_im",
          "ssm_c_re", "ssm_c_im", "ssm_d", "b_glu", "ffn_norm_g", "conv_w", "conv_b", "final_norm_g")
_WEIGHTS = ("attn_norm_g", "w_in", "b_in", "attn_sinks", "ssm_a_re", "ssm_a_im", "ssm_log_dt", "ssm_b_re", "ssm_b_im",
            "ssm_c_re", "ssm_c_im", "ssm_d", "w_glu", "b_glu", "w_branch_attn", "w_branch_ssm", "w_out", "ffn_norm_g",
            "w_up", "conv_w", "conv_b", "w_down", "final_norm_g")


def _unstack_cols(g):
    return g.transpose(1, 0, 2).reshape(g.shape[1], g.shape[0] * g.shape[2])


def _stack_cols(a, d=N_DEV):
    k, n = a.shape
    return a.reshape(k, d, n // d).transpose(1, 0, 2)


def _pack(arrs):
    flat = jnp.concatenate([a.reshape(-1) for a in arrs])
    pad = (-flat.shape[0]) % 1024
    return jnp.pad(flat, (0, pad)).reshape(-1, 128)


def _local_step(x, tgt, wget, small, gput):
    L = x.shape[0]
    nr = lambda tm: L // tm

    h = _rmsnorm_fwd(x, small["attn_norm_g"], "norm1")
    wts = dict(wget("a", h))
    projb = _mm(h, wts["w_in"], bias=small["b_in_p"], out_dtype=BF16, name="proj")
    proj = projb
    attn_bias = _attn_bias()
    attn = _attn_fwd(projb, small["attn_sinks"], attn_bias, "attn_fwd")

    ab, bmat, cmat = _ssm_prep(small["a_re"], small["a_im"], small["logdt"], small["b_re"], small["b_im"],
                               small["c_re"], small["c_im"], "ssm_prep")
    u_seg = _to_segments(proj[:, C_U:C_PAD])
    ends_f = _ssm_scan(u_seg, bmat, ab, reverse=False, tk=128, name="ssm_ends_fwd")
    xs, init_f, y_seg = _ssm_scan(u_seg, bmat, ab, reverse=False, ends=ends_f, wproj=cmat, tk=64, name="ssm_scan_fwd")
    y_mm = _from_segments(y_seg)

    def gelu_fn(c, r, yb, ub, db):
        yv = yb + db * ub
        return yv, _gelu(yv)

    tm = 512
    y, gy = _ew(gelu_fn, [(y_mm, *_rc(tm, 256)), (proj, *_rc(tm, 256, C_U // 256)), (small["ssm_d"], *_col(1, 256))],
                [(SDS((L, SW), F32), *_rc(tm, 256), None), (SDS((L, SW), BF16), *_rc(tm, 256), None)],
                (2, nr(tm)), "ssm_gelu")
    wts.update(wget("b", gy))
    glu = _mm(gy, wts["w_glu"], bias=small["b_glu"], name="glu")

    def glu_fn(c, r, vb, gb):
        return (vb * _sigmoid(gb),)

    (ssm,) = _ew(glu_fn, [(glu, *_rc(tm, SW)), (glu, *_rc(tm, SW, 1))], [(SDS((L, SW), BF16), *_rc(tm, SW), None)],
                 (1, nr(tm)), "glu_gate")
    f32 = lambda ref, cols: ref[:, cols].astype(F32)
    tnm = 1024
    gate_tiles = [(projb, "tile", C_GA // tnm), (projb, "tile", C_GS // tnm)]

    def merge_ep(i, cols, ra, rs, ga, gs):
        sa, ss = _sigmoid(f32(ga, cols)), _sigmoid(f32(gs, cols))
        return sa * ra + ss * rs, ra, rs, sa, ss

    merged, br_a, br_s, sig_a, sig_s = _mm(attn, wts["w_branch_attn"], a2=ssm, b2=wts["w_branch_ssm"], tm=512, tn=tnm,
                                           extras=gate_tiles, epilogue=merge_ep,
                                           outs=[(SDS((L, D), BF16), "tile")] * 5, name="branch_merge")
    x1 = _mm(merged, wts["w_out"], res=x, name="out_proj")
    h2 = _rmsnorm_fwd(x1, small["ffn_norm_g"], "norm2")
    wts.update(wget("c", h2))
    conv_w = wts["conv_w"]
    w_up = wts["w_up"]
    tcf = 1408
    tma = 256
    hb = 16

    def conv_gate(first, gate, halo, cw, cb):
        halo = halo * jnp.logical_not(first).astype(F32)
        g1, g2 = _shift_rows(gate, halo, 1), _shift_rows(gate, halo, 2)
        return cb + cw[2:3] * gate + cw[1:2] * g1 + cw[0:1] * g2, g1, g2

    tmu, tnu = 1024, 512

    def up_ep(i, cols, rg, rv, h2_halo, wg, cw, cb):
        halo = jnp.dot(h2_halo[...], wg[:, cols], preferred_element_type=F32)[hb - 8:]
        gl, glg = _gelu_and_grad(conv_gate(i == 0, rg, halo, cw[:, cols], cb[:, cols])[0])
        return rg, rv * gl, gl, rv * glg

    up_g, act, gelu_cg, val_gelu_grad = _mm(
        h2, w_up, b2=w_up, n_cols=DFF, b_col_off=DFF // tnu, tm=tmu, tn=tnu, epilogue=up_ep,
        outs=[(SDS((L, DFF), BF16), "tile")] * 4, name="ffn_up_act",
        extras=[(h2, "spec", ((hb, D), lambda j, i: (jnp.maximum(i * (tmu // hb) - 1, 0), 0))),
                (w_up, "spec", ((D, tnu), lambda j, i: (0, j + DFF // tnu))), (conv_w, "col", 0),
                (small["conv_b"], "col", 0)])
    x2 = _mm(act, wts["w_down"], res=x1, name="ffn_down")
    d_x2, d_x2b, loss_cols, d_gf = _final_loss(x2, small["final_norm_g"], tgt, "final_loss")
    loss = jnp.sum(loss_cols)

    dw_down = _mm(act, d_x2b, ta=True, out_dtype=BF16, tm=tcf, tk=2048, name="dw_down")
    tmd, tnd = 1024, 512

    def dact_ep(i, cols, da, _, gate_ref, halo_ref, gl_ref, vg_ref):
        gate, gl = f32(gate_ref, cols), f32(gl_ref, cols)
        halo = f32(halo_ref, cols)[hb - 8:] * (i > 0).astype(F32)
        g1, g2 = _shift_rows(gate, halo, 1), _shift_rows(gate, halo, 2)
        d_cg = da * f32(vg_ref, cols)
        row3 = lax.broadcasted_iota(jnp.int32, (3, da.shape[1]), 0)
        s0 = jnp.sum(d_cg * g2, axis=0, keepdims=True)
        s1 = jnp.sum(d_cg * g1, axis=0, keepdims=True)
        s2 = jnp.sum(d_cg * gate, axis=0, keepdims=True)
        dcw = jnp.where(row3 == 0, s0, jnp.where(row3 == 1, s1, s2))
        return da * gl, d_cg, dcw, jnp.sum(d_cg, axis=0, keepdims=True)

    d_up, d_cg, d_conv_w, d_conv_b = _mm(
        d_x2b, wts["w_down"], tb=True, tm=tmd, tn=tnd, epilogue=dact_ep, name="d_act_bwd",
        extras=[(up_g, "tile", 0), (up_g, "spec", ((hb, tnd), lambda j, i: (jnp.maximum(i * (tmd // hb) - 1, 0), j))),
                (gelu_cg, "tile", 0), (val_gelu_grad, "tile", 0)],
        outs=[(SDS((L, 2 * DFF), BF16), "tile"), (SDS((L, DFF), BF16), "tile"), (SDS((3, DFF), F32), "colacc"),
              (SDS((1, DFF), F32), "colacc")])
    ncf = DFF // tcf

    tmg = 512

    def gate_bwd(c, r, dcg, halo, cw):
        halo = halo[:8] * (r < nr(tmg) - 1).astype(F32)
        return (cw[2:3] * dcg + cw[1:2] * _shift_rows_up(dcg, halo, 1) + cw[0:1] * _shift_rows_up(dcg, halo, 2),)

    (d_up,) = _ew(gate_bwd, [(d_cg, *_rc(tmg, tcf)),
                             (d_cg, (hb, tcf), lambda c, r: (jnp.minimum((r + 1) * (tmg // hb), L // hb - 1), c)),
                             (conv_w, *_col(3, tcf))],
                  [(SDS((L, 2 * DFF), BF16), *_rc(tmg, tcf, ncf), None)], (ncf, nr(tmg)), "ffn_gate_bwd", into=d_up)
    d_h2 = _mm(d_up, w_up, tb=True, name="d_h2")
    assert tcf == 2 * DFF // N_DEV
    dw_up = _mm(h2, d_up, ta=True, out_dtype=BF16, tn=tcf, tk=2048, stack_out=True, name="dw_up")
    tok = gput("c", {"w_up": dw_up, "w_down": dw_down})
    d_x1, d_g2 = _rmsnorm_bwd(d_h2, x1, small["ffn_norm_g"] + tok[0, 0], d_x2, "norm2_bwd")

    dw_out = _mm(merged, d_x1, ta=True, out_dtype=BF16, name="dw_out")

    def dmerge_ep(i, cols, dm, _, a_ref, s_ref, sa_ref, ss_ref):
        sa, ss = f32(sa_ref, cols), f32(ss_ref, cols)
        return dm * sa, dm * ss, dm * (f32(a_ref, cols) * (sa * (1.0 - sa))), dm * (f32(s_ref, cols) * (ss * (1.0 - ss)))

    d_bra, d_brs, d_ga, d_gs = _mm(d_x1, wts["w_out"], tb=True, tm=512, tn=tnm, epilogue=dmerge_ep,
                                   extras=[(br_a, "tile", 0), (br_s, "tile", 0), (sig_a, "tile", 0), (sig_s, "tile", 0)],
                                   outs=[(SDS((L, D), BF16), "tile")] * 4, name="d_merged_bwd")
    d_attn = _mm(d_bra, wts["w_branch_attn"], tb=True, out_dtype=BF16, name="d_attn")
    dw_ba = _mm(attn, d_bra, ta=True, out_dtype=BF16, name="dw_branch_attn")
    d_ssm = _mm(d_brs, wts["w_branch_ssm"], tb=True, name="d_ssm")
    dw_bs = _mm(ssm, d_brs, ta=True, out_dtype=BF16, name="dw_branch_ssm")
    dq, dkv_cur, dkv_prev, d_sinks = _attn_bwd(projb, small["attn_sinks"], attn_bias, d_attn, "attn_bwd")

    def glu_bwd(c, r, ds, vb, gb):
        sg = _sigmoid(gb)
        return ds * sg, ds * vb * (sg * (1.0 - sg))

    d_glu_v, d_glu_g = _ew(glu_bwd, [(d_ssm, *_rc(tm, SW)), (glu, *_rc(tm, SW)), (glu, *_rc(tm, SW, 1))],
                           [(SDS((L, SW), F32), *_rc(tm, SW), None)] * 2, (1, nr(tm)), "glu_gate_bwd")
    d_glu = jnp.concatenate([d_glu_v, d_glu_g], axis=1)
    d_gy = _mm(d_glu, wts["w_glu"], tb=True, name="d_gelu_y")
    dw_glu = _mm(gy, d_glu, ta=True, out_dtype=BF16, name="dw_glu")

    tok = gput("b", {"w_glu": dw_glu, "w_branch_attn": dw_ba, "w_branch_ssm": dw_bs, "w_out": dw_out})
    ab = ab + tok[0, 0]

    def gelu_bwd(c, r, dg, yb, ub, dgl):
        dy = dg * _gelu_grad(yb)
        return dy, jnp.sum(dy * ub, axis=0, keepdims=True), jnp.sum(dgl, axis=0, keepdims=True)

    dy, d_ssm_d, d_b_glu = _ew(
        gelu_bwd, [(d_gy, *_rc(tm, 256)), (y, *_rc(tm, 256)), (proj, *_rc(tm, 256, C_U // 256)), (d_glu, *_rc(tm, 512))],
        [(SDS((L, SW), F32), *_rc(tm, 256), None), (SDS((1, SW), F32), *_col(1, 256), "r"),
         (SDS((1, 2 * SW), F32), *_col(1, 512), "r")], (2, nr(tm)), "ssm_gelu_bwd")

    dy_seg = _to_segments(dy)
    ends_r = _ssm_scan(dy_seg, cmat, ab, reverse=True, tk=128, name="ssm_ends_bwd")
    lam, dab8, du_seg = _ssm_scan(dy_seg, cmat, ab, reverse=True, ends=ends_r, xs=xs, init=init_f, wproj=bmat,
                                  tk=64, name="ssm_scan_bwd")
    du_mm = _from_segments(du_seg)
    dbm = _mm(u_seg, lam, ta=True, tm=512, name="ssm_dbmat")
    dcm = _mm(dy_seg, xs, ta=True, tm=512, name="ssm_dcmat")
    d_are, d_aim, d_ldt, d_bre, d_bim, d_cre, d_cim = _ssm_param_bwd(
        small["a_re"], small["a_im"], small["logdt"], small["b_re"], small["b_im"], dab8, dbm, dcm, "ssm_param_bwd")

    nb = L // BLK

    def dproj_fn(c, r, dqb, cur, prv, du, dyb, dsk, dga, dgs):
        dkv = cur + prv * (r < nb - 1).astype(F32)
        dub = du + dsk * dyb
        full = jnp.concatenate([dqb, dkv, dub, jnp.zeros((BLK, C_GA - C_PAD), F32), dga, dgs], axis=1)
        return full, jnp.sum(full, axis=0, keepdims=True)

    rowb = lambda w: ((BLK, w), lambda c, r: (r, 0))
    dproj, d_b_in = _ew(
        dproj_fn, [(dq, *rowb(AW)), (dkv_cur, *rowb(256)),
                   (dkv_prev, (BLK, 256), lambda c, r: (jnp.minimum(r + 1, nb - 1), 0)),
                   (du_mm, *rowb(SW)), (dy, *rowb(SW)), (small["ssm_d"], *_col(1, SW)), (d_ga, *rowb(D)), (d_gs, *rowb(D))],
        [(SDS((L, INP), BF16), *rowb(INP), None), (SDS((1, INP), F32), *_col(1, INP), "all")], (1, nb), "dproj")
    tok_small = gput("small", {
        "b_in": _unpad_cols(d_b_in), "attn_sinks": d_sinks[:, :NQ], "a_re": d_are, "a_im": d_aim, "logdt": d_ldt,
        "b_re": d_bre, "b_im": d_bim, "c_re": d_cre, "c_im": d_cim, "ssm_d": d_ssm_d, "b_glu": d_b_glu,
        "ffn_norm_g": d_g2, "conv_w": d_conv_w, "conv_b": d_conv_b, "final_norm_g": d_gf})
    dw_in = _mm(h, dproj, ta=True, out_dtype=BF16, name="dw_in")
    tok = gput("a", {"w_in": _unpad_cols(dw_in)}) + tok_small
    d_h = _mm(dproj, wts["w_in"], tb=True, bias=jnp.zeros((1, D), F32) + tok[0, 0], name="d_h")
    grad_x, d_g1 = _rmsnorm_bwd(d_h, x, small["attn_norm_g"], d_x1, "norm1_bwd")
    return loss, grad_x, {"attn_norm_g": d_g1}


def _small_layouts(p):
    gp = lambda a: a.reshape(1, NS)
    hgp = lambda a: a.transpose(2, 0, 1).reshape(H, NS)
    chgp = lambda a: a.transpose(1, 0, 2).reshape(H, NS)
    return {
        "attn_norm_g": p["attn_norm_g"].reshape(1, D), "ffn_norm_g": p["ffn_norm_g"].reshape(1, D),
        "final_norm_g": p["final_norm_g"].reshape(1, D),
        "b_in_p": _pad_cols(p["b_in"].reshape(1, INC)),
        "attn_sinks": p["attn_sinks"].reshape(1, NQ),
        "a_re": gp(p["ssm_a_re"]), "a_im": gp(p["ssm_a_im"]), "logdt": jnp.repeat(p["ssm_log_dt"], P).reshape(1, NS),
        "b_re": hgp(p["ssm_b_re"]), "b_im": hgp(p["ssm_b_im"]), "c_re": chgp(p["ssm_c_re"]), "c_im": chgp(p["ssm_c_im"]),
        "ssm_d": p["ssm_d"].reshape(1, SW), "b_glu": p["b_glu"].reshape(1, 2 * SW),
        "conv_b": p["conv_b"].reshape(1, DFF),
    }


def _small_grads_to_param_shapes(sg):
    from_hgp = lambda a: a.reshape(H, G, P).transpose(1, 2, 0)
    from_chgp = lambda a: a.reshape(H, G, P).transpose(1, 0, 2)
    flat = lambda a: a.reshape(-1)
    to_param = {
        "attn_norm_g": ("attn_norm_g", flat), "b_in": ("b_in", flat), "attn_sinks": ("attn_sinks", flat),
        "a_re": ("ssm_a_re", lambda a: a.reshape(G, P)), "a_im": ("ssm_a_im", lambda a: a.reshape(G, P)),
        "logdt": ("ssm_log_dt", lambda a: jnp.sum(a.reshape(G, P), axis=1)),
        "b_re": ("ssm_b_re", from_hgp), "b_im": ("ssm_b_im", from_hgp),
        "c_re": ("ssm_c_re", from_chgp), "c_im": ("ssm_c_im", from_chgp),
        "ssm_d": ("ssm_d", flat), "b_glu": ("b_glu", flat), "ffn_norm_g": ("ffn_norm_g", flat),
        "conv_w": ("conv_w", lambda a: a), "conv_b": ("conv_b", flat), "final_norm_g": ("final_norm_g", flat),
    }
    return {to_param[k][0]: to_param[k][1](a) for k, a in sg.items()}


def kernel(x, attn_norm_g, w_in, b_in, attn_sinks, ssm_a_re, ssm_a_im, ssm_log_dt, ssm_b_re, ssm_b_im, ssm_c_re, ssm_c_im, ssm_d, w_glu, b_glu, w_branch_attn, w_branch_ssm, w_out, ffn_norm_g, w_up, conv_w, conv_b, w_down, final_norm_g, loss_target, m_attn_norm_g, m_w_in, m_b_in, m_attn_sinks, m_ssm_a_re, m_ssm_a_im, m_ssm_log_dt, m_ssm_b_re, m_ssm_b_im, m_ssm_c_re, m_ssm_c_im, m_ssm_d, m_w_glu, m_b_glu, m_w_branch_attn, m_w_branch_ssm, m_w_out, m_ffn_norm_g, m_w_up, m_conv_w, m_conv_b, m_w_down, m_final_norm_g, v_attn_norm_g, v_w_in, v_b_in, v_attn_sinks, v_ssm_a_re, v_ssm_a_im, v_ssm_log_dt, v_ssm_b_re, v_ssm_b_im, v_ssm_c_re, v_ssm_c_im, v_ssm_d, v_w_glu, v_b_glu, v_w_branch_attn, v_w_branch_ssm, v_w_out, v_ffn_norm_g, v_w_up, v_conv_w, v_conv_b, v_w_down, v_final_norm_g):
    args = dict(locals())
    sq = lambda a: a if a.ndim == 1 else a[0]
    wv = {n: sq(args[n]) for n in _WEIGHTS}
    mv = {n: sq(args["m_" + n]) for n in _WEIGHTS}
    vv = {n: sq(args["v_" + n]) for n in _WEIGHTS}
    me = 4 * lax.axis_index("x") + 2 * lax.axis_index("y") + lax.axis_index("c")

    gather, tok = {}, jnp.zeros((8, 128), F32)
    for grp in ("a", "b", "c"):
        shards = [(wv[n] + tok[0, 0]).astype(BF16) for n in _GROUPS[grp]]
        if grp == "c":
            shards.append(jnp.pad(wv["conv_w"] + tok[0, 0], ((0, 5), (0, 64))))
        gather[grp], tok = _exchange_start(shards, False, "gather_start_" + grp,
                                           masks=_FIRST_HOP_MASKS if grp == "a" else _ALL_MASKS)
    small = _small_layouts(wv)
    small["attn_norm_g"] = small["attn_norm_g"] + tok[0, 0]

    def own_slot(land, src):
        return lax.dynamic_update_slice_in_dim(land, src, me, axis=0)

    def wget(grp, after):
        if grp == "a":
            thru, lands = _exchange_wait(gather[grp], after, False, "gather_wait_a", masks=_FIRST_HOP_MASKS)
            fwd, fwd_tok = _forward_start(lands, "gather_forward_start_a")
            lands = _forward_wait(fwd, fwd_tok, "gather_forward_wait_a")
        else:
            thru, lands = _exchange_wait(gather[grp], after, False, "gather_wait_" + grp)
        full = {}
        for n, t, g in zip(_GROUPS[grp], thru, lands):
            g = own_slot(g, t[None])
            full[n] = _unstack_cols(g) if n in _COL_SHARDED else g.reshape(N_DEV * g.shape[1], g.shape[2])
        if grp == "a":
            full["w_in"] = _pad_cols(full["w_in"])
        if grp == "c":
            full["conv_w"] = _unstack_cols(own_slot(lands[-1], thru[-1][None])[:, :3, :DFF // N_DEV])
        return full

    scatter = {}

    early_names = [n for n in _SMALL if n != "attn_norm_g"]
    sgp = {}

    def gput(grp, grads):
        if grp == "small":
            sgp.update(_small_grads_to_param_shapes(grads))
            scatter[grp], token = _exchange_start([_pack([sgp[n] for n in early_names])], False, "gather_small_start")
            return token
        stacked = [grads[n] if n == "w_up" else
                   _stack_cols(grads[n]) if n in _COL_SHARDED else grads[n].reshape(N_DEV, -1, D) for n in _GROUPS[grp]]
        scatter[grp], token = _exchange_start(stacked, True, "scatter_start_" + grp)
        return token

    loss, grad_x, sg = _local_step(x[0], loss_target[0], wget, small, gput)
    loss = lax.psum(loss, MESH_AXES)

    sgp.update(_small_grads_to_param_shapes(sg))
    small_names = [n for n in _SMALL]
    (norm_all,) = _exchange([jnp.pad(sgp["attn_norm_g"].reshape(1, D), ((0, 7), (0, 0)))], False, "gather_norm_grad")
    thru, (small_all,) = _exchange_wait(scatter["small"], norm_all, False, "gather_small_wait")
    small_all = own_slot(small_all, thru[0][None])

    outs_g, outs_d, outs_m, outs_v = {}, {}, {}, {}
    for grp in ("c", "b", "a"):
        thru, lands = _exchange_wait(scatter[grp], norm_all, True, "scatter_wait_" + grp)
        for n, t, pt in zip(_GROUPS[grp], thru, lands):
            pt = own_slot(pt, lax.dynamic_slice_in_dim(t, me, 1, axis=0))
            outs_g[n], outs_d[n], outs_m[n], outs_v[n] = _adam(pt, wv[n], mv[n], vv[n], "adam_" + n)

    sizes = [int(math.prod(sgp[n].shape)) for n in early_names]
    offs = [0]
    for s in sizes:
        offs.append(offs[-1] + s)

    def local_part(n, a):
        if n == "conv_w":
            return lax.dynamic_slice(a, (0, me * (DFF // N_DEV)), (3, DFF // N_DEV))
        return a

    rows = small_all.shape[1]

    def sum_fn(cc, rr, pb, nb_):
        g, gn = pb[0], nb_[0]
        for d in range(1, N_DEV):
            g, gn = g + pb[d], gn + nb_[d]
        return g, gn

    gsum, gnorm = _ew(sum_fn, [(small_all, (N_DEV, rows, 128), lambda cc, rr: (0, 0, 0)),
                               (norm_all, (N_DEV, 8, D), lambda cc, rr: (0, 0, 0))],
                      [(SDS((rows, 128), F32), (rows, 128), lambda cc, rr: (0, 0), None),
                       (SDS((8, D), F32), (8, D), lambda cc, rr: (0, 0), None)], (1, 1), "sum_small_grads")
    gflat = gsum.reshape(-1)
    gsmall = {n: local_part(n, gflat[offs[i]:offs[i + 1]].reshape(sgp[n].shape)) for i, n in enumerate(early_names)}
    gsmall["attn_norm_g"] = gnorm[0]
    as2d = lambda a: a.reshape(1, -1) if a.ndim == 1 else a.reshape(a.shape[0], -1)
    n_small = len(small_names)

    def adam_small(*refs):
        for i in range(n_small):
            g_ref, w_ref, m_ref, v_ref = refs[4 * i:4 * i + 4]
            outs = refs[4 * n_small + 3 * i:4 * n_small + 3 * i + 3]
            for o_ref, val in zip(outs, _adam_math(g_ref[...], w_ref[...], m_ref[...], v_ref[...])):
                o_ref[...] = val

    small_ins = [as2d(t[n]) for n in small_names for t in (gsmall, wv, mv, vv)]
    small_outs = pl.pallas_call(adam_small, name="adam_small",
                                out_shape=[SDS(as2d(wv[n]).shape, F32) for n in small_names for _ in range(3)])(*small_ins)
    for i, n in enumerate(small_names):
        sd, sm, sv = (t.reshape(wv[n].shape) for t in small_outs[3 * i:3 * i + 3])
        outs_g[n], outs_d[n], outs_m[n], outs_v[n] = gsmall[n], sd, sm, sv

    lead = lambda n, a: a if args[n].ndim == 1 else a[None]
    grad_x = grad_x[None]
    return (loss, grad_x, *[lead(n, outs_g[n]) for n in _WEIGHTS], *[lead(n, outs_d[n]) for n in _WEIGHTS],
            *[lead(n, outs_m[n]) for n in _WEIGHTS], *[lead(n, outs_v[n]) for n in _WEIGHTS])
```

```python
import functools
import math

import jax
import jax.numpy as jnp
from jax import lax
from jax.experimental import pallas as pl
from jax.experimental.pallas import tpu as pltpu

F32 = jnp.float32
BF16 = jnp.bfloat16
SDS = jax.ShapeDtypeStruct

N_DEV = 8
D = 2048
NQ, NKV, HD = 16, 2, 64
AW = NQ * HD
BLK = 128
SW, G, H, P = 512, 32, 16, 64
NS = G * P
DFF = 5632
INC = AW + 2 * NKV * HD + SW + 2 * D
C_K, C_U, C_PAD = AW, AW + 2 * NKV * HD, AW + 2 * NKV * HD + SW
C_GA, C_GS, INP = D, 2 * D, 3 * D
RMS_EPS = 1e-6
NEG_BIG = -1e30
ADAM_LR, ADAM_B1, ADAM_B2, ADAM_EPS, ADAM_WD, ADAM_STEP = 0.001, 0.9, 0.999, 1e-08, 0.01, 10
NSEG = 8
VMEM_CAP_MB = 60
MESH_AXES = ("x", "y", "c")


def _pad_cols(a):
    zeros = jnp.zeros(a.shape[:-1] + (C_GA - C_PAD,), a.dtype)
    return jnp.concatenate([a[..., :C_PAD], zeros, a[..., C_PAD:]], axis=-1)


def _unpad_cols(a):
    return jnp.concatenate([a[..., :C_PAD], a[..., C_GA:]], axis=-1)


def _cparams(sem, vmem_mb):
    return pltpu.CompilerParams(dimension_semantics=sem, vmem_limit_bytes=min(int(vmem_mb), VMEM_CAP_MB) << 20)


LANES = 128


def _tile(dim, pref):
    if dim <= pref:
        return dim
    for t in range(pref - pref % LANES, 0, -LANES):
        if dim % t == 0:
            return t
    raise ValueError(f"no tile for {dim}")


def _mm(a, b, *, ta=False, tb=False, bias=None, res=None, out_dtype=F32, tm=1024, tn=1024, tk=3072, name,
        a2=None, b2=None, extras=(), epilogue=None, outs=None, stack_out=False, n_cols=None,
        b_col_off=0, b2_col_off=0):
    m, k = (a.shape[1], a.shape[0]) if ta else a.shape
    n = n_cols or (b.shape[0] if tb else b.shape[1])
    assert (b.shape[1] if tb else b.shape[0]) == k, (a.shape, b.shape, ta, tb)
    tm, tn, tk = _tile(m, tm), _tile(n, tn), _tile(k, tk)
    nk = k // tk
    dims = (((0 if ta else 1,), (1 if tb else 0,)), ((), ()))
    has_bias, has_res, has_b2 = bias is not None, res is not None, b2 is not None
    has_a2 = a2 is not None
    assert not (has_b2 and (nk > 1 or ta or tb)) and not (has_a2 and not has_b2)
    if epilogue is None:
        outs = [(SDS((n // tn, m, tn) if stack_out else (m, n), out_dtype), "tile")]
    n_ex, n_out = len(extras), len(outs)

    def body(*refs):
        a_ref, b_ref = refs[0], refs[1]
        pos = 2
        a2_ref = refs[pos] if has_a2 else a_ref
        pos += has_a2
        b2_ref = refs[pos] if has_b2 else None
        pos += has_b2
        bias_ref = refs[pos] if has_bias else None
        pos += has_bias
        res_ref = refs[pos] if has_res else None
        pos += has_res
        ex_refs = refs[pos:pos + n_ex]
        o_refs = refs[pos + n_ex:pos + n_ex + n_out]
        i = pl.program_id(1)

        def product(rhs_ref, cols=None, lhs=None):
            rhs = rhs_ref[...] if cols is None else (rhs_ref[cols, :] if tb else rhs_ref[:, cols])
            lhs = a_ref[...].astype(BF16) if lhs is None else lhs
            return lax.dot_general(lhs, rhs.astype(BF16), dims, preferred_element_type=F32)

        def finish(r, cols):
            if has_bias:
                r = r + bias_ref[:, cols]
            if has_res:
                r = r + res_ref[:, cols].astype(F32)
            if epilogue is None:
                o_refs[0][:, cols] = r.astype(o_refs[0].dtype)
                return
            r2 = None
            if has_b2:
                r2 = jnp.dot(a2_ref[...].astype(BF16), b2_ref[:, cols].astype(BF16), preferred_element_type=F32)
            vals = epilogue(i, cols, r, r2, *ex_refs)
            for o_ref, v, (_, kind) in zip(o_refs, vals, outs):
                if kind == "tile":
                    o_ref[:, cols] = v.astype(o_ref.dtype)
                else:
                    @pl.when(i == 0)
                    def _(o_ref=o_ref, v=v):
                        o_ref[:, cols] = v.astype(o_ref.dtype)

                    @pl.when(i > 0)
                    def _(o_ref=o_ref, v=v):
                        o_ref[:, cols] += v.astype(o_ref.dtype)

        whole = pl.ds(0, tn)
        if nk == 1:
            finish(product(b_ref), whole)
            return
        acc_ref = refs[-1]
        kk = pl.program_id(2)

        @pl.when(kk == 0)
        def _():
            acc_ref[...] = product(b_ref)

        @pl.when(jnp.logical_and(kk > 0, kk < nk - 1))
        def _():
            acc_ref[...] += product(b_ref)

        @pl.when(kk == nk - 1)
        def _():
            finish(acc_ref[...] + product(b_ref), whole)

    b_spec = (pl.BlockSpec((tn, tk), lambda j, i, kk: (j + b_col_off, kk)) if tb else
              pl.BlockSpec((tk, tn), lambda j, i, kk: (kk, j + b_col_off)))
    ins = [a, b]
    in_specs = [pl.BlockSpec((tk, tm), lambda j, i, kk: (kk, i)) if ta else pl.BlockSpec((tm, tk), lambda j, i, kk: (i, kk)),
                b_spec]
    tile_spec = pl.BlockSpec((tm, tn), lambda j, i, kk: (i, j))
    byt = 2 * tm * tk * a.dtype.itemsize + 2 * tk * tn * b.dtype.itemsize
    byt += (2 + has_b2) * 4 * tm * tn
    if has_a2:
        ins.append(a2)
        in_specs.append(pl.BlockSpec((tm, a2.shape[1]), lambda j, i, kk: (i, 0)))
        byt += 2 * tm * a2.shape[1] * a2.dtype.itemsize
    if has_b2:
        ins.append(b2)
        in_specs.append(pl.BlockSpec((b2.shape[0], tn), lambda j, i, kk: (0, j + b2_col_off)))
        byt += 2 * b2.shape[0] * tn * b2.dtype.itemsize
    if has_bias:
        ins.append(bias)
        in_specs.append(pl.BlockSpec((1, tn), lambda j, i, kk: (0, j)))
    if has_res:
        ins.append(res)
        in_specs.append(tile_spec)
        byt += 2 * tm * tn * res.dtype.itemsize
    for arr, kind, arg in extras:
        ins.append(arr)
        if kind == "tile":
            in_specs.append(pl.BlockSpec((tm, tn), lambda j, i, kk, arg=arg: (i, j + arg)))
            byt += 2 * tm * tn * arr.dtype.itemsize + 4 * tm * tn
        elif kind == "col":
            in_specs.append(pl.BlockSpec((arr.shape[0], tn), lambda j, i, kk, arg=arg: (0, j + arg)))
        else:
            in_specs.append(pl.BlockSpec(arg[0], lambda j, i, kk, im=arg[1]: im(j, i)))
    out_specs = []
    for sds, kind in outs:
        if kind == "tile":
            out_specs.append(pl.BlockSpec((None, tm, tn), lambda j, i, kk: (j, i, 0)) if stack_out else tile_spec)
            byt += 2 * tm * tn * jnp.dtype(sds.dtype).itemsize
        else:
            out_specs.append(pl.BlockSpec((sds.shape[0], tn), lambda j, i, kk: (0, j)))
    res_ = pl.pallas_call(
        body, out_shape=tuple(o[0] for o in outs), grid=(n // tn, m // tm, nk), in_specs=in_specs,
        out_specs=tuple(out_specs), scratch_shapes=[pltpu.VMEM((tm, tn), F32)] if nk > 1 else [], name=name,
        compiler_params=_cparams(("arbitrary", "arbitrary", "arbitrary"), byt / 2**20 + (8 if epilogue is None else 20)),
    )(*ins)
    return res_[0] if epilogue is None else res_


def _ew(fn, ins, outs, grid, name, vmem_mb=40, into=None):
    n_in = len(ins)
    accs = [o[3] for o in outs]

    def body(*refs):
        c, r = pl.program_id(0), pl.program_id(1)
        vals = fn(c, r, *[ref[...].astype(F32) for ref in refs[:n_in]])
        for o_ref, v, acc in zip(refs[n_in + (into is not None):], vals, accs):
            if acc is None:
                o_ref[...] = v.astype(o_ref.dtype)
            else:
                first = (r == 0) if acc == "r" else jnp.logical_and(r == 0, c == 0)

                @pl.when(first)
                def _(o_ref=o_ref, v=v):
                    o_ref[...] = v.astype(o_ref.dtype)

                @pl.when(jnp.logical_not(first))
                def _(o_ref=o_ref, v=v):
                    o_ref[...] += v.astype(o_ref.dtype)

    in_specs = [pl.BlockSpec(bs, im) for _, bs, im in ins]
    args = [a for a, _, _ in ins]
    if into is not None:
        in_specs.append(pl.BlockSpec(memory_space=pl.ANY))
        args.append(into)
    res = pl.pallas_call(
        body, out_shape=tuple(o[0] for o in outs), grid=grid, in_specs=in_specs,
        out_specs=tuple(pl.BlockSpec(bs, im) for _, bs, im, _ in outs), name=name,
        input_output_aliases={} if into is None else {n_in: 0},
        compiler_params=_cparams(("arbitrary", "arbitrary"), vmem_mb),
    )(*args)
    return res


def _rc(tm, tc, coff=0):
    return (tm, tc), (lambda c, r: (r, c + coff))


def _col(rows, tc, coff=0):
    return (rows, tc), (lambda c, r: (0, c + coff))


def _gelu(x):
    return 0.5 * x * (1.0 + lax.erf(x * (2.0 ** -0.5)))


def _gelu_and_grad(x):
    cdf = 0.5 * (1.0 + lax.erf(x * (2.0 ** -0.5)))
    return x * cdf, cdf + x * jnp.exp(-0.5 * x * x) * (1.0 / math.sqrt(2.0 * math.pi))


def _gelu_grad(x):
    return _gelu_and_grad(x)[1]


def _sigmoid(x):
    return 1.0 / (1.0 + jnp.exp(-x))


def _shift_rows(x, halo, s):
    rolled = pltpu.roll(x, s, 0)
    row8 = lax.broadcasted_iota(jnp.int32, halo.shape, 0)
    head = jnp.where(row8 < s, pltpu.roll(halo, s, 0), rolled[0:8])
    return jnp.concatenate([head, rolled[8:]], axis=0)


def _shift_rows_up(x, halo, s):
    tm = x.shape[0]
    rolled = pltpu.roll(x, tm - s, 0)
    row8 = lax.broadcasted_iota(jnp.int32, halo.shape, 0)
    tail = jnp.where(row8 >= 8 - s, pltpu.roll(halo, 8 - s, 0), rolled[tm - 8:])
    return jnp.concatenate([rolled[:tm - 8], tail], axis=0)


def _rmsnorm_fwd(x, g, name, tm=512):
    L = x.shape[0]

    def fn(c, r, xb, gb):
        rstd = lax.rsqrt(jnp.mean(xb * xb, axis=-1, keepdims=True) + RMS_EPS)
        return ((xb * rstd) * gb,)

    return _ew(fn, [(x, *_rc(tm, D)), (g, *_col(1, D))], [(SDS((L, D), BF16), *_rc(tm, D), None)], (1, L // tm), name)[0]


def _rmsnorm_bwd(dh, x, g, dres, name, tm=512):
    L = x.shape[0]

    def fn(c, r, dhb, xb, gb, drb):
        rstd = lax.rsqrt(jnp.mean(xb * xb, axis=-1, keepdims=True) + RMS_EPS)
        y = xb * rstd
        dy = dhb * gb
        dx = rstd * (dy - y * jnp.mean(dy * y, axis=-1, keepdims=True))
        return drb + dx, jnp.sum(dhb * y, axis=0, keepdims=True)

    return _ew(fn, [(dh, *_rc(tm, D)), (x, *_rc(tm, D)), (g, *_col(1, D)), (dres, *_rc(tm, D))],
               [(SDS((L, D), F32), *_rc(tm, D), None), (SDS((1, D), F32), *_col(1, D), "all")], (1, L // tm), name,
               vmem_mb=56)


def _final_loss(x2, g, tgt, name, tm=512):
    L = x2.shape[0]

    def fn(c, r, xb, gb, tb):
        rstd = lax.rsqrt(jnp.mean(xb * xb, axis=-1, keepdims=True) + RMS_EPS)
        y = xb * rstd
        err = y * gb - tb
        dout = err * (1.0 / D)
        dy = dout * gb
        dx = rstd * (dy - y * jnp.mean(dy * y, axis=-1, keepdims=True))
        return dx, dx, jnp.sum(err * err, axis=0, keepdims=True) * (0.5 / D), jnp.sum(dout * y, axis=0, keepdims=True)

    return _ew(fn, [(x2, *_rc(tm, D)), (g, *_col(1, D)), (tgt, *_rc(tm, D))],
               [(SDS((L, D), F32), *_rc(tm, D), None), (SDS((L, D), BF16), *_rc(tm, D), None),
                (SDS((1, D), F32), *_col(1, D), "all"),
                (SDS((1, D), F32), *_col(1, D), "all")], (1, L // tm), name, vmem_mb=56)


def _slope(h):
    return 2.0 ** (-8.0 * (h + 1) / NQ)


def _attn_bias():
    qi = lax.broadcasted_iota(jnp.int32, (BLK, 2 * BLK), 0)
    si = lax.broadcasted_iota(jnp.int32, (BLK, 2 * BLK), 1)
    dist = qi + BLK - si
    band = (dist >= 0) & (dist < BLK)
    slopes = jnp.asarray([_slope(h) for h in range(NQ)], F32)[:, None, None]
    alibi = -slopes * dist.astype(F32)[None]
    return jnp.stack([jnp.where((band & (si >= BLK))[None], alibi, NEG_BIG), jnp.where(band[None], alibi, NEG_BIG)])


def _attn_kv(kvc, kvp):
    kv = jnp.concatenate([kvp, kvc], axis=0).astype(F32)
    lo = lax.broadcasted_iota(jnp.int32, (2 * BLK, 128), 1) < HD

    def halves(t):
        tr = pltpu.roll(t, HD, 1)
        z = jnp.zeros_like(t)
        return {(0, 0): jnp.where(lo, t, z).astype(BF16), (0, 1): jnp.where(lo, z, tr).astype(BF16),
                (1, 0): jnp.where(lo, tr, z).astype(BF16), (1, 1): jnp.where(lo, z, t).astype(BF16)}

    return halves(kv[:, :128]), halves(kv[:, 128:])


_NT = (((1,), (1,)), ((), ()))
_TN = (((0,), (0,)), ((), ()))
_ATTN_SPECS = [pl.BlockSpec(memory_space=pltpu.SMEM),
               pl.BlockSpec((None, NQ, BLK, 2 * BLK), lambda n: (jnp.minimum(n, 1), 0, 0, 0)),
               pl.BlockSpec((BLK, AW), lambda n: (n, 0)),
               pl.BlockSpec((BLK, 256), lambda n: (n, C_K // 256)),
               pl.BlockSpec((BLK, 256), lambda n: (jnp.maximum(n - 1, 0), C_K // 256))]


def _attn_scores(q_ref, bias_ref, kmat, sc_ref):
    for j in range(NQ // 2):
        qs = q_ref[:, 128 * j:128 * (j + 1)] * (HD ** -0.5)
        for e in range(2):
            h = 2 * j + e
            sc_ref[h] = lax.dot_general(qs, kmat[(j // (NQ // 4), e)], _NT, preferred_element_type=F32) + bias_ref[h]


def _softmax_with_sink(s, sink):
    m = jnp.maximum(jnp.max(s, axis=-1, keepdims=True), sink)
    p = jnp.exp(s - m)
    esink = jnp.exp(sink - m)
    den = jnp.sum(p, axis=-1, keepdims=True) + esink
    return p / den, esink / den


def _attn_fwd(projb, sinks, bias, name):
    L = projb.shape[0]

    def body(s_ref, bias_ref, q_ref, kvc_ref, kvp_ref, o_ref, sc_ref, pr_ref):
        kmat, vmat = _attn_kv(kvc_ref[...], kvp_ref[...])
        _attn_scores(q_ref, bias_ref, kmat, sc_ref)
        for h in range(NQ):
            pr_ref[h] = _softmax_with_sink(sc_ref[h], s_ref[0, h])[0].astype(BF16)
        for j in range(NQ // 2):
            g = j // (NQ // 4)
            acc = jnp.dot(pr_ref[2 * j], vmat[(g, 0)], preferred_element_type=F32)
            acc = acc + jnp.dot(pr_ref[2 * j + 1], vmat[(g, 1)], preferred_element_type=F32)
            o_ref[:, 128 * j:128 * (j + 1)] = acc.astype(BF16)

    return pl.pallas_call(
        body, out_shape=SDS((L, AW), BF16), grid=(L // BLK,), in_specs=_ATTN_SPECS,
        out_specs=pl.BlockSpec((BLK, AW), lambda n: (n, 0)), name=name,
        scratch_shapes=[pltpu.VMEM((NQ, BLK, 2 * BLK), F32), pltpu.VMEM((NQ, BLK, 2 * BLK), BF16)],
        compiler_params=_cparams(("arbitrary",), 32),
    )(sinks, bias, projb, projb, projb)


def _attn_bwd(projb, sinks, bias, dattn, name):
    L = projb.shape[0]

    def body(s_ref, bias_ref, q_ref, kvc_ref, kvp_ref, do_ref, dq_ref, dcur_ref, dprev_ref, dsink_ref,
             sc_ref, dp_ref, ds_ref, pr_ref, qm_ref, dm_ref):
        n = pl.program_id(0)
        kmat, vmat = _attn_kv(kvc_ref[...], kvp_ref[...])
        _attn_scores(q_ref, bias_ref, kmat, sc_ref)
        for h in range(NQ):
            j, e = h // 2, h % 2
            dp_ref[h] = lax.dot_general(do_ref[:, 128 * j:128 * (j + 1)], vmat[(j // (NQ // 4), e)], _NT,
                                        preferred_element_type=F32)
        lane = lax.broadcasted_iota(jnp.int32, (1, 128), 1)
        dsv = jnp.zeros((1, 128), F32)
        for h in range(NQ):
            p, psink = _softmax_with_sink(sc_ref[h], s_ref[0, h])
            dp = dp_ref[h]
            drow = jnp.sum(p * dp, axis=-1, keepdims=True)
            ds_ref[h] = (p * (dp - drow)).astype(BF16)
            pr_ref[h] = p.astype(BF16)
            dsv = dsv + jnp.where(lane == h, -jnp.sum(psink * drow, axis=0, keepdims=True), 0.0)
        lo128 = lax.broadcasted_iota(jnp.int32, (BLK, 128), 1) < HD
        for j in range(NQ // 2):
            g = j // (NQ // 4)
            qs = q_ref[:, 128 * j:128 * (j + 1)] * (HD ** -0.5)
            dop = do_ref[:, 128 * j:128 * (j + 1)]
            zb = jnp.zeros_like(qs)
            dqp = jnp.zeros((BLK, 128), F32)
            for e in range(2):
                h = 2 * j + e
                half = lo128 if e == 0 else jnp.logical_not(lo128)
                dqp = dqp + jnp.dot(ds_ref[h], kmat[(g, e)], preferred_element_type=F32)
                qm_ref[h] = jnp.where(half, qs, zb)
                dm_ref[h] = jnp.where(half, dop, zb)
            dq_ref[:, 128 * j:128 * (j + 1)] = (dqp * (HD ** -0.5)).astype(BF16)
        hk = NQ // NKV
        rows = lambda ref, g: ref[g * hk:(g + 1) * hk].reshape(hk * BLK, ref.shape[-1])
        dk = [lax.dot_general(rows(ds_ref, g), rows(qm_ref, g), _TN, preferred_element_type=F32) for g in range(NKV)]
        dv = [lax.dot_general(rows(pr_ref, g), rows(dm_ref, g), _TN, preferred_element_type=F32) for g in range(NKV)]
        lo256 = lax.broadcasted_iota(jnp.int32, (2 * BLK, 128), 1) < HD
        tot = [t + pltpu.roll(t, HD, 1) for t in (dk[0], dk[1], dv[0], dv[1])]
        dkv = jnp.concatenate([jnp.where(lo256, tot[0], tot[1]), jnp.where(lo256, tot[2], tot[3])], axis=1)
        dprev_ref[...] = dkv[:BLK]
        dcur_ref[...] = dkv[BLK:]

        @pl.when(n == 0)
        def _():
            dsink_ref[...] = dsv

        @pl.when(n > 0)
        def _():
            dsink_ref[...] += dsv

    tile = (NQ, BLK, 2 * BLK)
    return pl.pallas_call(
        body, out_shape=(SDS((L, AW), BF16), SDS((L, 256), F32), SDS((L, 256), F32), SDS((1, 128), F32)), grid=(L // BLK,),
        in_specs=_ATTN_SPECS + [pl.BlockSpec((BLK, AW), lambda n: (n, 0))],
        out_specs=(pl.BlockSpec((BLK, AW), lambda n: (n, 0)), pl.BlockSpec((BLK, 256), lambda n: (n, 0)),
                   pl.BlockSpec((BLK, 256), lambda n: (n, 0)), pl.BlockSpec((1, 128), lambda n: (0, 0))),
        scratch_shapes=[pltpu.VMEM(tile, F32), pltpu.VMEM(tile, F32), pltpu.VMEM(tile, BF16), pltpu.VMEM(tile, BF16),
                        pltpu.VMEM((NQ, BLK, 128), BF16), pltpu.VMEM((NQ, BLK, 128), BF16)],
        name=name, compiler_params=_cparams(("arbitrary",), 40),
    )(sinks, bias, projb, projb, projb, dattn)


def _disc(a_re, a_im, logdt, b_re, b_im):
    dt = jnp.exp(logdt)
    mag = jnp.exp(a_re * dt)
    ab_re = mag * jnp.cos(a_im * dt)
    ab_im = mag * jnp.sin(a_im * dt)
    nr = ab_re - 1.0
    ni = ab_im
    den = a_re * a_re + a_im * a_im
    z_re = (nr * a_re + ni * a_im) / den
    z_im = (ni * a_re - nr * a_im) / den
    return ab_re, ab_im, z_re * b_re - z_im * b_im, z_re * b_im + z_im * b_re


def _group_mask():
    row = lax.broadcasted_iota(jnp.int32, (SW, NS), 0) // H
    col = lax.broadcasted_iota(jnp.int32, (SW, NS), 1) // P
    return row == col


def _block_diag(re, im):
    mask = _group_mask()
    z = jnp.zeros((SW, NS), F32)
    return jnp.concatenate([jnp.where(mask, jnp.tile(re, (G, 1)), z), jnp.where(mask, jnp.tile(im, (G, 1)), z)], axis=1)


def _block_diag_t(big):
    mask = _group_mask()
    z = jnp.zeros((SW, NS), F32)
    re = jnp.sum(jnp.where(mask, big[:, :NS], z).reshape(G, H, NS), axis=0)
    im = jnp.sum(jnp.where(mask, big[:, NS:], z).reshape(G, H, NS), axis=0)
    return re, im


def _ssm_prep(a_re, a_im, logdt, b_re, b_im, c_re, c_im, name):
    def body(are, aim, ldt, bre, bim, cre, cim, ab_ref, bm_ref, cm_ref):
        ab_re, ab_im, bb_re, bb_im = _disc(are[...], aim[...], ldt[...], bre[...], bim[...])
        ab_ref[...] = jnp.concatenate([ab_re, ab_im], axis=1)
        bm_ref[...] = _block_diag(bb_re, bb_im).astype(BF16)
        cm_ref[...] = _block_diag(cre[...], -cim[...]).astype(BF16)

    return pl.pallas_call(body, out_shape=(SDS((1, 2 * NS), F32), SDS((SW, 2 * NS), BF16), SDS((SW, 2 * NS), BF16)),
                          name=name, compiler_params=pltpu.CompilerParams(vmem_limit_bytes=48 << 20),
                          )(a_re, a_im, logdt, b_re, b_im, c_re, c_im)


def _ssm_param_bwd(a_re, a_im, logdt, b_re, b_im, dab8, dbm, dcm, name):
    def body(are, aim, ldt, bre, bim, dab_ref, dbm_ref, dcm_ref, o_are, o_aim, o_ldt, o_bre, o_bim, o_cre, o_cim):
        dab = jnp.sum(dab_ref[...], axis=0, keepdims=True)
        dbb_re, dbb_im = _block_diag_t(dbm_ref[...])
        _, vjp = jax.vjp(_disc, are[...], aim[...], ldt[...], bre[...], bim[...])
        d_are, d_aim, d_ldt, d_bre, d_bim = vjp((dab[:, :NS], dab[:, NS:], dbb_re, dbb_im))
        o_are[...], o_aim[...], o_ldt[...], o_bre[...], o_bim[...] = d_are, d_aim, d_ldt, d_bre, d_bim
        dc_re, dc_imn = _block_diag_t(dcm_ref[...])
        o_cre[...] = dc_re
        o_cim[...] = -dc_imn

    v1, vh = SDS((1, NS), F32), SDS((H, NS), F32)
    return pl.pallas_call(body, out_shape=(v1, v1, v1, vh, vh, vh, vh), name=name,
                          compiler_params=pltpu.CompilerParams(vmem_limit_bytes=56 << 20),
                          )(a_re, a_im, logdt, b_re, b_im, dab8, dbm, dcm)


def _ssm_scan(src, wmat, ab, *, reverse, ends=None, xs=None, init=None, wproj=None, name, tk=32):
    L = src.shape[0]
    seg_len = L // NSEG
    tk = min(tk, seg_len)
    rows = NSEG * tk
    nch = L // rows
    n_sq = int(math.log2(seg_len))
    assert 2 ** n_sq == seg_len and L % rows == 0
    first_pass = ends is None
    with_dab = (not first_pass) and reverse
    with_proj = wproj is not None
    assert not (with_proj and first_pass)
    slab = 512
    n_slab = NS // slab

    def body(*refs):
        src_ref, w_ref, ab_ref = refs[:3]
        pos = 3
        if not first_pass:
            ends_ref = refs[pos]
            pos += 1
        if with_dab:
            xs_ref, xsh_ref, init_ref = refs[pos:pos + 3]
            pos += 3
        if with_proj:
            wproj_ref = refs[pos]
            pos += 1
        if first_pass:
            (e_ref,) = refs[pos:pos + 1]
            pos += 1
        else:
            st_out_ref, aux_ref = refs[pos:pos + 2]
            pos += 2
        if with_proj:
            proj_ref = refs[pos]
            pos += 1
        buf_ref, st_ref = refs[pos:pos + 2]
        i = pl.program_id(0)
        a_re = ab_ref[:, :NS]
        a_im = -ab_ref[:, NS:] if reverse else ab_ref[:, NS:]

        @pl.when(i == 0)
        def _():
            if first_pass:
                st_ref[...] = jnp.zeros_like(st_ref)
            else:
                pr, pi = a_re, a_im
                for _ in range(n_sq):
                    pr, pi = pr * pr - pi * pi, 2.0 * pr * pi
                zr = jnp.zeros((1, NS), F32)
                cr, ci = zr, zr
                order = list(range(NSEG - 1, -1, -1)) if reverse else list(range(NSEG))
                st_ref[order[0]:order[0] + 1, :] = jnp.zeros((1, 2 * NS), F32)
                for jprev, j in zip(order[:-1], order[1:]):
                    er, ei = ends_ref[jprev:jprev + 1, :NS], ends_ref[jprev:jprev + 1, NS:]
                    cr, ci = er + pr * cr - pi * ci, ei + pr * ci + pi * cr
                    st_ref[j:j + 1, :NS] = cr
                    st_ref[j:j + 1, NS:] = ci
                if not reverse:
                    aux_ref[...] = st_ref[...]
                else:
                    aux_ref[...] = jnp.zeros_like(aux_ref)

        buf_ref[...] = jnp.dot(src_ref[...].astype(BF16), w_ref[...], preferred_element_type=F32)

        for s in range(n_slab):
            re_sl, im_sl = pl.ds(s * slab, slab), pl.ds(NS + s * slab, slab)
            ar = jnp.broadcast_to(a_re[:, s * slab:(s + 1) * slab], (NSEG, slab))
            ai = jnp.broadcast_to(a_im[:, s * slab:(s + 1) * slab], (NSEG, slab))

            def step(t, carry, re_sl=re_sl, im_sl=im_sl, ar=ar, ai=ai):
                k = (tk - 1 - t) if reverse else t
                r0 = pl.multiple_of(k * NSEG, NSEG)
                xr, xi = carry[0], carry[1]
                nr = ar * xr - ai * xi + buf_ref[pl.ds(r0, NSEG), re_sl]
                ni = ar * xi + ai * xr + buf_ref[pl.ds(r0, NSEG), im_sl]
                if not first_pass:
                    buf_ref[pl.ds(r0, NSEG), re_sl] = nr
                    buf_ref[pl.ds(r0, NSEG), im_sl] = ni
                if not with_dab:
                    return nr, ni
                rp = pl.multiple_of((k - 1) * NSEG, NSEG)
                xpr, xpi = xs_ref[pl.ds(rp, NSEG), re_sl], xs_ref[pl.ds(rp, NSEG), im_sl]
                return nr, ni, carry[2] + nr * xpr + ni * xpi, carry[3] + ni * xpr - nr * xpi

            carry = (st_ref[:, re_sl], st_ref[:, im_sl])
            if with_dab:
                z = jnp.zeros((NSEG, slab), F32)
                carry = lax.fori_loop(0, tk - 1, step, carry + (z, z))
                xr, xi, dr, di = carry
                nr = ar * xr - ai * xi + buf_ref[pl.ds(0, NSEG), re_sl]
                ni = ar * xi + ai * xr + buf_ref[pl.ds(0, NSEG), im_sl]
                buf_ref[pl.ds(0, NSEG), re_sl] = nr
                buf_ref[pl.ds(0, NSEG), im_sl] = ni
                at_start = i == nch - 1
                xpr = jnp.where(at_start, init_ref[:, re_sl], xsh_ref[:, re_sl])
                xpi = jnp.where(at_start, init_ref[:, im_sl], xsh_ref[:, im_sl])
                aux_ref[:, re_sl] += dr + nr * xpr + ni * xpi
                aux_ref[:, im_sl] += di + ni * xpr - nr * xpi
                carry = (nr, ni)
            else:
                carry = lax.fori_loop(0, tk, step, carry)
            st_ref[:, re_sl] = carry[0]
            st_ref[:, im_sl] = carry[1]

        if first_pass:
            @pl.when(i == nch - 1)
            def _():
                e_ref[...] = st_ref[...]
        else:
            st_out_ref[...] = buf_ref[...].astype(st_out_ref.dtype)
            if with_proj:
                proj_ref[...] = lax.dot_general(buf_ref[...].astype(BF16), wproj_ref[...], _NT, preferred_element_type=F32)

    chunk = (lambda i: (nch - 1 - i, 0)) if reverse else (lambda i: (i, 0))
    whole = lambda i: (0, 0)
    ins = [src, wmat, ab]
    once = pl.Buffered(1)
    in_specs = [pl.BlockSpec((rows, SW), chunk), pl.BlockSpec((SW, 2 * NS), whole, pipeline_mode=once),
                pl.BlockSpec((1, 2 * NS), whole)]
    small = SDS((NSEG, 2 * NS), F32)
    small_spec = pl.BlockSpec((NSEG, 2 * NS), whole)
    if not first_pass:
        ins.append(ends)
        in_specs.append(small_spec)
    if with_dab:
        ins += [xs, xs, init]
        in_specs += [pl.BlockSpec((rows, 2 * NS), chunk),
                     pl.BlockSpec((NSEG, 2 * NS), lambda i: (jnp.maximum((nch - 1 - i) * tk - 1, 0), 0)),
                     small_spec]
    if with_proj:
        ins.append(wproj)
        in_specs.append(pl.BlockSpec((SW, 2 * NS), whole, pipeline_mode=once))
    if first_pass:
        out_shape, out_specs = small, small_spec
    else:
        out_shape = (SDS((L, 2 * NS), BF16 if reverse else F32), small)
        out_specs = (pl.BlockSpec((rows, 2 * NS), chunk), small_spec)
        if with_proj:
            out_shape += (SDS((L, SW), F32),)
            out_specs += (pl.BlockSpec((rows, SW), chunk),)
    return pl.pallas_call(
        body, out_shape=out_shape, grid=(nch,), in_specs=in_specs, out_specs=out_specs,
        scratch_shapes=[pltpu.VMEM((rows, 2 * NS), F32), pltpu.VMEM((NSEG, 2 * NS), F32)], name=name,
        compiler_params=_cparams(("arbitrary",), 56),
    )(*ins)


def _to_segments(a):
    L, c = a.shape
    return a.reshape(NSEG, L // NSEG, c).transpose(1, 0, 2).reshape(L, c)


def _from_segments(a):
    L, c = a.shape
    return a.reshape(L // NSEG, NSEG, c).transpose(1, 0, 2).reshape(L, c)


def _peer(x, y, c, m):
    return ((1 - x) if (m >> 2) & 1 else x, (1 - y) if (m >> 1) & 1 else y, (1 - c) if m & 1 else c)


def _dev_index(p):
    return 4 * p[0] + 2 * p[1] + p[2]


def _exchange(arrs, scatter, name):
    n = len(arrs)

    def body(*refs):
        ins, outs = refs[:n], refs[n:2 * n]
        send_sems, recv_sems, loc_sems = refs[2 * n:]
        x, y, c = lax.axis_index("x"), lax.axis_index("y"), lax.axis_index("c")
        me = _dev_index((x, y, c))

        def src(w, to):
            return ins[w].at[to] if scatter else ins[w]

        def local(w):
            return pltpu.make_async_copy(src(w, me), outs[w].at[me], loc_sems.at[w])

        def remote(w, m):
            peer = _peer(x, y, c, m)
            return pltpu.make_async_remote_copy(src_ref=src(w, _dev_index(peer)), dst_ref=outs[w].at[me],
                                                send_sem=send_sems.at[w, m - 1], recv_sem=recv_sems.at[w, m - 1],
                                                device_id=peer, device_id_type=pl.DeviceIdType.MESH)

        def arrival(w, m):
            peer = _peer(x, y, c, m)
            return pltpu.make_async_remote_copy(src_ref=src(w, me), dst_ref=outs[w].at[_dev_index(peer)],
                                                send_sem=send_sems.at[w, m - 1], recv_sem=recv_sems.at[w, m - 1],
                                                device_id=peer, device_id_type=pl.DeviceIdType.MESH)

        for w in range(n):
            local(w).start()
        for w in range(n):
            for m in range(1, N_DEV):
                remote(w, m).start()
        for w in range(n):
            for m in range(1, N_DEV):
                arrival(w, m).wait_recv()
        for w in range(n):
            for m in range(1, N_DEV):
                remote(w, m).wait_send()
        for w in range(n):
            local(w).wait()

    anyspec = pl.BlockSpec(memory_space=pl.ANY)
    out_shape = tuple(SDS(a.shape if scatter else (N_DEV,) + a.shape, a.dtype) for a in arrs)
    return pl.pallas_call(
        body, out_shape=out_shape, in_specs=[anyspec] * n, out_specs=tuple([anyspec] * n),
        scratch_shapes=[pltpu.SemaphoreType.DMA((n, N_DEV - 1)), pltpu.SemaphoreType.DMA((n, N_DEV - 1)),
                        pltpu.SemaphoreType.DMA((n,))],
        name=name, compiler_params=pltpu.CompilerParams(has_side_effects=True),
    )(*arrs)


_HBM = pl.BlockSpec(memory_space=pltpu.HBM)
_SEM = pl.BlockSpec(memory_space=pltpu.SEMAPHORE)
_EFFECT = pltpu.SideEffectType.DATAFLOW_SIDE_EFFECTING


def _sem_index(w, m):
    return w * (N_DEV - 1) + m - 1


_ALL_MASKS = tuple(range(1, N_DEV))
_CHIP_MASKS = (2, 4, 6)
_FIRST_HOP_MASKS = (1,) + _CHIP_MASKS


def _exchange_start(arrs, scatter, name, masks=_ALL_MASKS):
    n = len(arrs)
    lands = [lax.empty(a.shape if scatter else (N_DEV,) + a.shape, a.dtype) for a in arrs]

    def body(*refs):
        ins, zones = refs[:n], refs[n:2 * n]
        send_sems, recv_sems = refs[2 * n], refs[2 * n + 1]
        token = refs[-1]
        x, y, c = lax.axis_index("x"), lax.axis_index("y"), lax.axis_index("c")
        me = _dev_index((x, y, c))
        for w in range(n):
            for m in masks:
                peer = _peer(x, y, c, m)
                pltpu.make_async_remote_copy(
                    src_ref=ins[w].at[_dev_index(peer)] if scatter else ins[w], dst_ref=zones[w].at[me],
                    send_sem=send_sems.at[_sem_index(w, m)], recv_sem=recv_sems.at[_sem_index(w, m)],
                    device_id=peer, device_id_type=pl.DeviceIdType.MESH).start()
        token[...] = jnp.zeros_like(token)

    sems = pltpu.SemaphoreType.DMA((n * (N_DEV - 1),))
    res = pl.pallas_call(
        body, name=name,
        out_shape=(sems, sems, *[pltpu.HBM(a.shape, a.dtype) for a in arrs], *[pltpu.HBM(z.shape, z.dtype) for z in lands],
                   SDS((8, 128), F32)),
        in_specs=[_HBM] * (2 * n), out_specs=(_SEM, _SEM, *([_HBM] * (2 * n)), pl.BlockSpec(memory_space=pltpu.VMEM)),
        input_output_aliases={i: 2 + i for i in range(2 * n)},
        compiler_params=pltpu.CompilerParams(has_side_effects=_EFFECT),
    )(*[pltpu.with_memory_space_constraint(a, pltpu.HBM) for a in arrs],
      *[pltpu.with_memory_space_constraint(z, pltpu.HBM) for z in lands])
    return (res[0], res[1], list(res[2:2 + n]), list(res[2 + n:2 + 2 * n])), res[-1]


def _exchange_wait(handle, after, scatter, name, masks=_ALL_MASKS):
    send_sems, recv_sems, thru, lands = handle
    n = len(thru)

    def body(*refs):
        ins, zones = refs[:n], refs[n:2 * n]
        send_sems, recv_sems = refs[2 * n], refs[2 * n + 1]
        x, y, c = lax.axis_index("x"), lax.axis_index("y"), lax.axis_index("c")
        me = _dev_index((x, y, c))
        for w in range(n):
            for m in masks:
                peer = _peer(x, y, c, m)
                copy = pltpu.make_async_remote_copy(
                    src_ref=ins[w].at[me] if scatter else ins[w], dst_ref=zones[w].at[_dev_index(peer)],
                    send_sem=send_sems.at[_sem_index(w, m)], recv_sem=recv_sems.at[_sem_index(w, m)],
                    device_id=peer, device_id_type=pl.DeviceIdType.MESH)
                copy.wait_send()
                copy.wait_recv()

    res = pl.pallas_call(
        body, name=name,
        out_shape=(*[pltpu.HBM(a.shape, a.dtype) for a in thru], *[pltpu.HBM(z.shape, z.dtype) for z in lands]),
        in_specs=[_HBM] * (2 * n) + [_SEM, _SEM, pl.BlockSpec(memory_space=pl.ANY)], out_specs=tuple([_HBM] * (2 * n)),
        input_output_aliases={i: i for i in range(2 * n)},
        compiler_params=pltpu.CompilerParams(has_side_effects=_EFFECT),
    )(*thru, *lands, send_sems, recv_sems, after)
    return list(res[:n]), list(res[n:])


def _forward_start(zones, name):
    n = len(zones)

    def body(*refs):
        zs = refs[:n]
        send_sems, recv_sems = refs[n], refs[n + 1]
        token = refs[-1]
        x, y, c = lax.axis_index("x"), lax.axis_index("y"), lax.axis_index("c")
        for w in range(n):
            for m in _CHIP_MASKS:
                slot = zs[w].at[_dev_index(_peer(x, y, c, m))]
                pltpu.make_async_remote_copy(
                    src_ref=slot, dst_ref=slot, send_sem=send_sems.at[_sem_index(w, m)],
                    recv_sem=recv_sems.at[_sem_index(w, m)], device_id=(x, y, 1 - c),
                    device_id_type=pl.DeviceIdType.MESH).start()
        token[...] = jnp.zeros_like(token)

    sems = pltpu.SemaphoreType.DMA((n * (N_DEV - 1),))
    res = pl.pallas_call(
        body, name=name, out_shape=(sems, sems, *[pltpu.HBM(z.shape, z.dtype) for z in zones], SDS((8, 128), F32)),
        in_specs=[_HBM] * n, out_specs=(_SEM, _SEM, *([_HBM] * n), pl.BlockSpec(memory_space=pltpu.VMEM)),
        input_output_aliases={i: 2 + i for i in range(n)},
        compiler_params=pltpu.CompilerParams(has_side_effects=_EFFECT),
    )(*[pltpu.with_memory_space_constraint(z, pltpu.HBM) for z in zones])
    return (res[0], res[1], list(res[2:2 + n])), res[-1]


def _forward_wait(handle, after, name):
    send_sems, recv_sems, zones = handle
    n = len(zones)

    def body(*refs):
        zs = refs[:n]
        send_sems, recv_sems = refs[n], refs[n + 1]
        x, y, c = lax.axis_index("x"), lax.axis_index("y"), lax.axis_index("c")
        for w in range(n):
            for m in _CHIP_MASKS:
                copy = pltpu.make_async_remote_copy(
                    src_ref=zs[w].at[_dev_index(_peer(x, y, c, m))], dst_ref=zs[w].at[_dev_index(_peer(x, y, 1 - c, m))],
                    send_sem=send_sems.at[_sem_index(w, m)], recv_sem=recv_sems.at[_sem_index(w, m)],
                    device_id=(x, y, 1 - c), device_id_type=pl.DeviceIdType.MESH)
                copy.wait_send()
                copy.wait_recv()

    res = pl.pallas_call(
        body, name=name, out_shape=tuple(pltpu.HBM(z.shape, z.dtype) for z in zones),
        in_specs=[_HBM] * n + [_SEM, _SEM, pl.BlockSpec(memory_space=pl.ANY)], out_specs=tuple([_HBM] * n),
        input_output_aliases={i: i for i in range(n)},
        compiler_params=pltpu.CompilerParams(has_side_effects=_EFFECT),
    )(*zones, send_sems, recv_sems, after)
    return list(res)


def _adam_math(g, w, m, v):
    m = ADAM_B1 * m + (1.0 - ADAM_B1) * g
    v = ADAM_B2 * v + (1.0 - ADAM_B2) * (g * g)
    m_hat = m / (1.0 - ADAM_B1 ** ADAM_STEP)
    v_hat = v / (1.0 - ADAM_B2 ** ADAM_STEP)
    delta = -ADAM_LR * (m_hat / (jnp.sqrt(v_hat) + ADAM_EPS) + ADAM_WD * w)
    return delta, m, v


def _adam(parts, w, m, v, name, tr=128):
    r, c = w.shape
    tr = next(t for t in (tr, 64, 32, 16, 8) if r % t == 0)

    def fn(cc, rr, pb, wb, mb, vb):
        g = pb[0].astype(F32)
        for d in range(1, N_DEV):
            g = g + pb[d].astype(F32)
        delta, nm, nv = _adam_math(g, wb, mb, vb)
        return g, delta, nm, nv

    blk = ((tr, c), lambda cc, rr: (rr, 0))
    o = SDS((r, c), F32)
    return _ew(fn, [(parts, (N_DEV, tr, c), lambda cc, rr: (0, rr, 0)), (w, *blk), (m, *blk), (v, *blk)],
               [(o, *blk, None)] * 4, (1, r // tr), name)


_SHARDED = ("w_in", "w_glu", "w_branch_attn", "w_branch_ssm", "w_out", "w_up", "w_down")
_COL_SHARDED = ("w_in", "w_glu", "w_branch_attn", "w_branch_ssm", "w_up")
_GROUPS = {"a": ("w_in",), "b": ("w_glu", "w_branch_attn", "w_branch_ssm", "w_out"), "c": ("w_up", "w_down")}
_SMALL = ("attn_norm_g", "b_in", "attn_sinks", "ssm_a_re", "ssm_a_im", "ssm_log_dt", "ssm_b_re", "ssm_b_im",
          "ssm_c_re", "ssm_c_im", "ssm_d", "b_glu", "ffn_norm_g", "conv_w", "conv_b", "final_norm_g")
_WEIGHTS = ("attn_norm_g", "w_in", "b_in", "attn_sinks", "ssm_a_re", "ssm_a_im", "ssm_log_dt", "ssm_b_re", "ssm_b_im",
            "ssm_c_re", "ssm_c_im", "ssm_d", "w_glu", "b_glu", "w_branch_attn", "w_branch_ssm", "w_out", "ffn_norm_g",
            "w_up", "conv_w", "conv_b", "w_down", "final_norm_g")


def _unstack_cols(g):
    return g.transpose(1, 0, 2).reshape(g.shape[1], g.shape[0] * g.shape[2])


def _stack_cols(a, d=N_DEV):
    k, n = a.shape
    return a.reshape(k, d, n // d).transpose(1, 0, 2)


def _pack(arrs):
    flat = jnp.concatenate([a.reshape(-1) for a in arrs])
    pad = (-flat.shape[0]) % 1024
    return jnp.pad(flat, (0, pad)).reshape(-1, 128)


def _local_step(x, tgt, wget, small, gput):
    L = x.shape[0]
    nr = lambda tm: L // tm

    h = _rmsnorm_fwd(x, small["attn_norm_g"], "norm1")
    wts = dict(wget("a", h))
    projb = _mm(h, wts["w_in"], bias=small["b_in_p"], out_dtype=BF16, name="proj")
    proj = projb
    attn_bias = _attn_bias()
    attn = _attn_fwd(projb, small["attn_sinks"], attn_bias, "attn_fwd")

    ab, bmat, cmat = _ssm_prep(small["a_re"], small["a_im"], small["logdt"], small["b_re"], small["b_im"],
                               small["c_re"], small["c_im"], "ssm_prep")
    u_seg = _to_segments(proj[:, C_U:C_PAD])
    ends_f = _ssm_scan(u_seg, bmat, ab, reverse=False, tk=256, name="ssm_ends_fwd")
    xs, init_f, y_seg = _ssm_scan(u_seg, bmat, ab, reverse=False, ends=ends_f, wproj=cmat, tk=64, name="ssm_scan_fwd")
    y_mm = _from_segments(y_seg)

    def gelu_fn(c, r, yb, ub, db):
        yv = yb + db * ub
        return yv, _gelu(yv)

    tm = 512
    y, gy = _ew(gelu_fn, [(y_mm, *_rc(tm, 256)), (proj, *_rc(tm, 256, C_U // 256)), (small["ssm_d"], *_col(1, 256))],
                [(SDS((L, SW), F32), *_rc(tm, 256), None), (SDS((L, SW), BF16), *_rc(tm, 256), None)],
                (2, nr(tm)), "ssm_gelu")
    wts.update(wget("b", gy))
    glu = _mm(gy, wts["w_glu"], bias=small["b_glu"], name="glu")

    def glu_fn(c, r, vb, gb):
        return (vb * _sigmoid(gb),)

    (ssm,) = _ew(glu_fn, [(glu, *_rc(tm, SW)), (glu, *_rc(tm, SW, 1))], [(SDS((L, SW), BF16), *_rc(tm, SW), None)],
                 (1, nr(tm)), "glu_gate")
    f32 = lambda ref, cols: ref[:, cols].astype(F32)
    tnm = 1024
    gate_tiles = [(projb, "tile", C_GA // tnm), (projb, "tile", C_GS // tnm)]

    def merge_ep(i, cols, ra, rs, ga, gs):
        sa, ss = _sigmoid(f32(ga, cols)), _sigmoid(f32(gs, cols))
        return sa * ra + ss * rs, ra, rs, sa, ss

    merged, br_a, br_s, sig_a, sig_s = _mm(attn, wts["w_branch_attn"], a2=ssm, b2=wts["w_branch_ssm"], tm=512, tn=tnm,
                                           extras=gate_tiles, epilogue=merge_ep,
                                           outs=[(SDS((L, D), BF16), "tile")] * 5, name="branch_merge")
    def norm2_ep(i, cols, r, _, g_ref):
        rstd = lax.rsqrt(jnp.mean(r * r, axis=-1, keepdims=True) + RMS_EPS)
        return r, (r * rstd) * g_ref[...]

    x1, h2 = _mm(merged, wts["w_out"], res=x, tm=512, tn=D, epilogue=norm2_ep, extras=[(small["ffn_norm_g"], "col", 0)],
                 outs=[(SDS((L, D), F32), "tile"), (SDS((L, D), BF16), "tile")], name="out_proj_norm2")
    wts.update(wget("c", h2))
    conv_w = wts["conv_w"]
    w_up = wts["w_up"]
    tcf = 1408
    tma = 256
    hb = 16

    def conv_gate(first, gate, halo, cw, cb):
        halo = halo * jnp.logical_not(first).astype(F32)
        g1, g2 = _shift_rows(gate, halo, 1), _shift_rows(gate, halo, 2)
        return cb + cw[2:3] * gate + cw[1:2] * g1 + cw[0:1] * g2, g1, g2

    tmu, tnu = 1024, 512

    def up_ep(i, cols, rg, rv, h2_halo, wg, cw, cb):
        halo = jnp.dot(h2_halo[...], wg[:, cols], preferred_element_type=F32)[hb - 8:]
        gl, glg = _gelu_and_grad(conv_gate(i == 0, rg, halo, cw[:, cols], cb[:, cols])[0])
        return rg, rv * gl, gl, rv * glg

    up_g, act, gelu_cg, val_gelu_grad = _mm(
        h2, w_up, b2=w_up, n_cols=DFF, b_col_off=DFF // tnu, tm=tmu, tn=tnu, epilogue=up_ep,
        outs=[(SDS((L, DFF), BF16), "tile")] * 4, name="ffn_up_act",
        extras=[(h2, "spec", ((hb, D), lambda j, i: (jnp.maximum(i * (tmu // hb) - 1, 0), 0))),
                (w_up, "spec", ((D, tnu), lambda j, i: (0, j + DFF // tnu))), (conv_w, "col", 0),
                (small["conv_b"], "col", 0)])
    x2 = _mm(act, wts["w_down"], res=x1, name="ffn_down")
    d_x2, d_x2b, loss_cols, d_gf = _final_loss(x2, small["final_norm_g"], tgt, "final_loss")
    loss = jnp.sum(loss_cols)

    dw_down = _mm(act, d_x2b, ta=True, out_dtype=BF16, tm=tcf, tk=2048, name="dw_down")
    tmd, tnd = 1024, 512

    def dact_ep(i, cols, da, _, gate_ref, halo_ref, gl_ref, vg_ref):
        gate, gl = f32(gate_ref, cols), f32(gl_ref, cols)
        halo = f32(halo_ref, cols)[hb - 8:] * (i > 0).astype(F32)
        g1, g2 = _shift_rows(gate, halo, 1), _shift_rows(gate, halo, 2)
        d_cg = da * f32(vg_ref, cols)
        row3 = lax.broadcasted_iota(jnp.int32, (3, da.shape[1]), 0)
        s0 = jnp.sum(d_cg * g2, axis=0, keepdims=True)
        s1 = jnp.sum(d_cg * g1, axis=0, keepdims=True)
        s2 = jnp.sum(d_cg * gate, axis=0, keepdims=True)
        dcw = jnp.where(row3 == 0, s0, jnp.where(row3 == 1, s1, s2))
        return da * gl, d_cg, dcw, jnp.sum(d_cg, axis=0, keepdims=True)

    d_up, d_cg, d_conv_w, d_conv_b = _mm(
        d_x2b, wts["w_down"], tb=True, tm=tmd, tn=tnd, epilogue=dact_ep, name="d_act_bwd",
        extras=[(up_g, "tile", 0), (up_g, "spec", ((hb, tnd), lambda j, i: (jnp.maximum(i * (tmd // hb) - 1, 0), j))),
                (gelu_cg, "tile", 0), (val_gelu_grad, "tile", 0)],
        outs=[(SDS((L, 2 * DFF), BF16), "tile"), (SDS((L, DFF), BF16), "tile"), (SDS((3, DFF), F32), "colacc"),
              (SDS((1, DFF), F32), "colacc")])
    ncf = DFF // tcf

    tmg = 512

    def gate_bwd(c, r, dcg, halo, cw):
        halo = halo[:8] * (r < nr(tmg) - 1).astype(F32)
        return (cw[2:3] * dcg + cw[1:2] * _shift_rows_up(dcg, halo, 1) + cw[0:1] * _shift_rows_up(dcg, halo, 2),)

    (d_up,) = _ew(gate_bwd, [(d_cg, *_rc(tmg, tcf)),
                             (d_cg, (hb, tcf), lambda c, r: (jnp.minimum((r + 1) * (tmg // hb), L // hb - 1), c)),
                             (conv_w, *_col(3, tcf))],
                  [(SDS((L, 2 * DFF), BF16), *_rc(tmg, tcf, ncf), None)], (ncf, nr(tmg)), "ffn_gate_bwd", into=d_up)
    d_h2 = _mm(d_up, w_up, tb=True, name="d_h2")
    assert tcf == 2 * DFF // N_DEV
    dw_up = _mm(h2, d_up, ta=True, out_dtype=BF16, tn=tcf, tk=2048, stack_out=True, name="dw_up")
    tok = gput("c", {"w_up": dw_up, "w_down": dw_down})
    d_x1, d_g2 = _rmsnorm_bwd(d_h2, x1, small["ffn_norm_g"] + tok[0, 0], d_x2, "norm2_bwd")

    dw_out = _mm(merged, d_x1, ta=True, out_dtype=BF16, name="dw_out")

    def dmerge_ep(i, cols, dm, _, a_ref, s_ref, sa_ref, ss_ref):
        sa, ss = f32(sa_ref, cols), f32(ss_ref, cols)
        return dm * sa, dm * ss, dm * (f32(a_ref, cols) * (sa * (1.0 - sa))), dm * (f32(s_ref, cols) * (ss * (1.0 - ss)))

    d_bra, d_brs, d_ga, d_gs = _mm(d_x1, wts["w_out"], tb=True, tm=512, tn=tnm, epilogue=dmerge_ep,
                                   extras=[(br_a, "tile", 0), (br_s, "tile", 0), (sig_a, "tile", 0), (sig_s, "tile", 0)],
                                   outs=[(SDS((L, D), BF16), "tile")] * 4, name="d_merged_bwd")
    d_attn = _mm(d_bra, wts["w_branch_attn"], tb=True, out_dtype=BF16, name="d_attn")
    dw_ba = _mm(attn, d_bra, ta=True, out_dtype=BF16, name="dw_branch_attn")
    d_ssm = _mm(d_brs, wts["w_branch_ssm"], tb=True, name="d_ssm")
    dw_bs = _mm(ssm, d_brs, ta=True, out_dtype=BF16, name="dw_branch_ssm")
    dq, dkv_cur, dkv_prev, d_sinks = _attn_bwd(projb, small["attn_sinks"], attn_bias, d_attn, "attn_bwd")

    def glu_bwd(c, r, ds, vb, gb):
        sg = _sigmoid(gb)
        return ds * sg, ds * vb * (sg * (1.0 - sg))

    d_glu_v, d_glu_g = _ew(glu_bwd, [(d_ssm, *_rc(tm, SW)), (glu, *_rc(tm, SW)), (glu, *_rc(tm, SW, 1))],
                           [(SDS((L, SW), F32), *_rc(tm, SW), None)] * 2, (1, nr(tm)), "glu_gate_bwd")
    d_glu = jnp.concatenate([d_glu_v, d_glu_g], axis=1)
    d_gy = _mm(d_glu, wts["w_glu"], tb=True, name="d_gelu_y")
    dw_glu = _mm(gy, d_glu, ta=True, out_dtype=BF16, name="dw_glu")

    tok = gput("b", {"w_glu": dw_glu, "w_branch_attn": dw_ba, "w_branch_ssm": dw_bs, "w_out": dw_out})
    ab = ab + tok[0, 0]

    def gelu_bwd(c, r, dg, yb, ub, dgl):
        dy = dg * _gelu_grad(yb)
        return dy, jnp.sum(dy * ub, axis=0, keepdims=True), jnp.sum(dgl, axis=0, keepdims=True)

    dy, d_ssm_d, d_b_glu = _ew(
        gelu_bwd, [(d_gy, *_rc(tm, 256)), (y, *_rc(tm, 256)), (proj, *_rc(tm, 256, C_U // 256)), (d_glu, *_rc(tm, 512))],
        [(SDS((L, SW), F32), *_rc(tm, 256), None), (SDS((1, SW), F32), *_col(1, 256), "r"),
         (SDS((1, 2 * SW), F32), *_col(1, 512), "r")], (2, nr(tm)), "ssm_gelu_bwd")

    dy_seg = _to_segments(dy)
    ends_r = _ssm_scan(dy_seg, cmat, ab, reverse=True, tk=256, name="ssm_ends_bwd")
    lam, dab8, du_seg = _ssm_scan(dy_seg, cmat, ab, reverse=True, ends=ends_r, xs=xs, init=init_f, wproj=bmat,
                                  tk=64, name="ssm_scan_bwd")
    du_mm = _from_segments(du_seg)
    dbm = _mm(u_seg, lam, ta=True, tm=512, name="ssm_dbmat")
    dcm = _mm(dy_seg, xs, ta=True, tm=512, name="ssm_dcmat")
    d_are, d_aim, d_ldt, d_bre, d_bim, d_cre, d_cim = _ssm_param_bwd(
        small["a_re"], small["a_im"], small["logdt"], small["b_re"], small["b_im"], dab8, dbm, dcm, "ssm_param_bwd")

    nb = L // BLK

    def dproj_fn(c, r, dqb, cur, prv, du, dyb, dsk, dga, dgs):
        dkv = cur + prv * (r < nb - 1).astype(F32)
        dub = du + dsk * dyb
        full = jnp.concatenate([dqb, dkv, dub, jnp.zeros((BLK, C_GA - C_PAD), F32), dga, dgs], axis=1)
        return full, jnp.sum(full, axis=0, keepdims=True)

    rowb = lambda w: ((BLK, w), lambda c, r: (r, 0))
    dproj, d_b_in = _ew(
        dproj_fn, [(dq, *rowb(AW)), (dkv_cur, *rowb(256)),
                   (dkv_prev, (BLK, 256), lambda c, r: (jnp.minimum(r + 1, nb - 1), 0)),
                   (du_mm, *rowb(SW)), (dy, *rowb(SW)), (small["ssm_d"], *_col(1, SW)), (d_ga, *rowb(D)), (d_gs, *rowb(D))],
        [(SDS((L, INP), BF16), *rowb(INP), None), (SDS((1, INP), F32), *_col(1, INP), "all")], (1, nb), "dproj")
    tok_small = gput("small", {
        "b_in": _unpad_cols(d_b_in), "attn_sinks": d_sinks[:, :NQ], "a_re": d_are, "a_im": d_aim, "logdt": d_ldt,
        "b_re": d_bre, "b_im": d_bim, "c_re": d_cre, "c_im": d_cim, "ssm_d": d_ssm_d, "b_glu": d_b_glu,
        "ffn_norm_g": d_g2, "conv_w": d_conv_w, "conv_b": d_conv_b, "final_norm_g": d_gf})
    dw_in = _mm(h, dproj, ta=True, out_dtype=BF16, name="dw_in")
    tok = gput("a", {"w_in": _unpad_cols(dw_in)}) + tok_small
    d_h = _mm(dproj, wts["w_in"], tb=True, bias=jnp.zeros((1, D), F32) + tok[0, 0], name="d_h")
    grad_x, d_g1 = _rmsnorm_bwd(d_h, x, small["attn_norm_g"], d_x1, "norm1_bwd")
    return loss, grad_x, {"attn_norm_g": d_g1}


def _small_layouts(p):
    gp = lambda a: a.reshape(1, NS)
    hgp = lambda a: a.transpose(2, 0, 1).reshape(H, NS)
    chgp = lambda a: a.transpose(1, 0, 2).reshape(H, NS)
    return {
        "attn_norm_g": p["attn_norm_g"].reshape(1, D), "ffn_norm_g": p["ffn_norm_g"].reshape(1, D),
        "final_norm_g": p["final_norm_g"].reshape(1, D),
        "b_in_p": _pad_cols(p["b_in"].reshape(1, INC)),
        "attn_sinks": p["attn_sinks"].reshape(1, NQ),
        "a_re": gp(p["ssm_a_re"]), "a_im": gp(p["ssm_a_im"]), "logdt": jnp.repeat(p["ssm_log_dt"], P).reshape(1, NS),
        "b_re": hgp(p["ssm_b_re"]), "b_im": hgp(p["ssm_b_im"]), "c_re": chgp(p["ssm_c_re"]), "c_im": chgp(p["ssm_c_im"]),
        "ssm_d": p["ssm_d"].reshape(1, SW), "b_glu": p["b_glu"].reshape(1, 2 * SW),
        "conv_b": p["conv_b"].reshape(1, DFF),
    }


def _small_grads_to_param_shapes(sg):
    from_hgp = lambda a: a.reshape(H, G, P).transpose(1, 2, 0)
    from_chgp = lambda a: a.reshape(H, G, P).transpose(1, 0, 2)
    flat = lambda a: a.reshape(-1)
    to_param = {
        "attn_norm_g": ("attn_norm_g", flat), "b_in": ("b_in", flat), "attn_sinks": ("attn_sinks", flat),
        "a_re": ("ssm_a_re", lambda a: a.reshape(G, P)), "a_im": ("ssm_a_im", lambda a: a.reshape(G, P)),
        "logdt": ("ssm_log_dt", lambda a: jnp.sum(a.reshape(G, P), axis=1)),
        "b_re": ("ssm_b_re", from_hgp), "b_im": ("ssm_b_im", from_hgp),
        "c_re": ("ssm_c_re", from_chgp), "c_im": ("ssm_c_im", from_chgp),
        "ssm_d": ("ssm_d", flat), "b_glu": ("b_glu", flat), "ffn_norm_g": ("ffn_norm_g", flat),
        "conv_w": ("conv_w", lambda a: a), "conv_b": ("conv_b", flat), "final_norm_g": ("final_norm_g", flat),
    }
    return {to_param[k][0]: to_param[k][1](a) for k, a in sg.items()}


def kernel(x, attn_norm_g, w_in, b_in, attn_sinks, ssm_a_re, ssm_a_im, ssm_log_dt, ssm_b_re, ssm_b_im, ssm_c_re, ssm_c_im, ssm_d, w_glu, b_glu, w_branch_attn, w_branch_ssm, w_out, ffn_norm_g, w_up, conv_w, conv_b, w_down, final_norm_g, loss_target, m_attn_norm_g, m_w_in, m_b_in, m_attn_sinks, m_ssm_a_re, m_ssm_a_im, m_ssm_log_dt, m_ssm_b_re, m_ssm_b_im, m_ssm_c_re, m_ssm_c_im, m_ssm_d, m_w_glu, m_b_glu, m_w_branch_attn, m_w_branch_ssm, m_w_out, m_ffn_norm_g, m_w_up, m_conv_w, m_conv_b, m_w_down, m_final_norm_g, v_attn_norm_g, v_w_in, v_b_in, v_attn_sinks, v_ssm_a_re, v_ssm_a_im, v_ssm_log_dt, v_ssm_b_re, v_ssm_b_im, v_ssm_c_re, v_ssm_c_im, v_ssm_d, v_w_glu, v_b_glu, v_w_branch_attn, v_w_branch_ssm, v_w_out, v_ffn_norm_g, v_w_up, v_conv_w, v_conv_b, v_w_down, v_final_norm_g):
    args = dict(locals())
    sq = lambda a: a if a.ndim == 1 else a[0]
    wv = {n: sq(args[n]) for n in _WEIGHTS}
    mv = {n: sq(args["m_" + n]) for n in _WEIGHTS}
    vv = {n: sq(args["v_" + n]) for n in _WEIGHTS}
    me = 4 * lax.axis_index("x") + 2 * lax.axis_index("y") + lax.axis_index("c")

    gather, tok = {}, jnp.zeros((8, 128), F32)
    for grp in ("a", "b", "c"):
        shards = [(wv[n] + tok[0, 0]).astype(BF16) for n in _GROUPS[grp]]
        if grp == "c":
            shards.append(jnp.pad(wv["conv_w"] + tok[0, 0], ((0, 5), (0, 64))))
        gather[grp], tok = _exchange_start(shards, False, "gather_start_" + grp,
                                           masks=_FIRST_HOP_MASKS if grp == "a" else _ALL_MASKS)
    small = _small_layouts(wv)
    small["attn_norm_g"] = small["attn_norm_g"] + tok[0, 0]

    def own_slot(land, src):
        return lax.dynamic_update_slice_in_dim(land, src, me, axis=0)

    def wget(grp, after):
        if grp == "a":
            thru, lands = _exchange_wait(gather[grp], after, False, "gather_wait_a", masks=_FIRST_HOP_MASKS)
            fwd, fwd_tok = _forward_start(lands, "gather_forward_start_a")
            lands = _forward_wait(fwd, fwd_tok, "gather_forward_wait_a")
        else:
            thru, lands = _exchange_wait(gather[grp], after, False, "gather_wait_" + grp)
        full = {}
        for n, t, g in zip(_GROUPS[grp], thru, lands):
            g = own_slot(g, t[None])
            full[n] = _unstack_cols(g) if n in _COL_SHARDED else g.reshape(N_DEV * g.shape[1], g.shape[2])
        if grp == "a":
            full["w_in"] = _pad_cols(full["w_in"])
        if grp == "c":
            full["conv_w"] = _unstack_cols(own_slot(lands[-1], thru[-1][None])[:, :3, :DFF // N_DEV])
        return full

    scatter = {}

    early_names = [n for n in _SMALL if n != "attn_norm_g"]
    sgp = {}

    def gput(grp, grads):
        if grp == "small":
            sgp.update(_small_grads_to_param_shapes(grads))
            scatter[grp], token = _exchange_start([_pack([sgp[n] for n in early_names])], False, "gather_small_start")
            return token
        stacked = [grads[n] if n == "w_up" else
                   _stack_cols(grads[n]) if n in _COL_SHARDED else grads[n].reshape(N_DEV, -1, D) for n in _GROUPS[grp]]
        scatter[grp], token = _exchange_start(stacked, True, "scatter_start_" + grp)
        return token

    loss, grad_x, sg = _local_step(x[0], loss_target[0], wget, small, gput)
    loss = lax.psum(loss, MESH_AXES)

    sgp.update(_small_grads_to_param_shapes(sg))
    small_names = [n for n in _SMALL]
    (norm_all,) = _exchange([jnp.pad(sgp["attn_norm_g"].reshape(1, D), ((0, 7), (0, 0)))], False, "gather_norm_grad")
    thru, (small_all,) = _exchange_wait(scatter["small"], norm_all, False, "gather_small_wait")
    small_all = own_slot(small_all, thru[0][None])

    outs_g, outs_d, outs_m, outs_v = {}, {}, {}, {}
    for grp in ("c", "b", "a"):
        thru, lands = _exchange_wait(scatter[grp], norm_all, True, "scatter_wait_" + grp)
        for n, t, pt in zip(_GROUPS[grp], thru, lands):
            pt = own_slot(pt, lax.dynamic_slice_in_dim(t, me, 1, axis=0))
            outs_g[n], outs_d[n], outs_m[n], outs_v[n] = _adam(pt, wv[n], mv[n], vv[n], "adam_" + n)

    sizes = [int(math.prod(sgp[n].shape)) for n in early_names]
    offs = [0]
    for s in sizes:
        offs.append(offs[-1] + s)

    def local_part(n, a):
        if n == "conv_w":
            return lax.dynamic_slice(a, (0, me * (DFF // N_DEV)), (3, DFF // N_DEV))
        return a

    rows = small_all.shape[1]

    def sum_fn(cc, rr, pb, nb_):
        g, gn = pb[0], nb_[0]
        for d in range(1, N_DEV):
            g, gn = g + pb[d], gn + nb_[d]
        return g, gn

    gsum, gnorm = _ew(sum_fn, [(small_all, (N_DEV, rows, 128), lambda cc, rr: (0, 0, 0)),
                               (norm_all, (N_DEV, 8, D), lambda cc, rr: (0, 0, 0))],
                      [(SDS((rows, 128), F32), (rows, 128), lambda cc, rr: (0, 0), None),
                       (SDS((8, D), F32), (8, D), lambda cc, rr: (0, 0), None)], (1, 1), "sum_small_grads")
    gflat = gsum.reshape(-1)
    gsmall = {n: local_part(n, gflat[offs[i]:offs[i + 1]].reshape(sgp[n].shape)) for i, n in enumerate(early_names)}
    gsmall["attn_norm_g"] = gnorm[0]
    as2d = lambda a: a.reshape(1, -1) if a.ndim == 1 else a.reshape(a.shape[0], -1)
    n_small = len(small_names)

    def adam_small(*refs):
        for i in range(n_small):
            g_ref, w_ref, m_ref, v_ref = refs[4 * i:4 * i + 4]
            outs = refs[4 * n_small + 3 * i:4 * n_small + 3 * i + 3]
            for o_ref, val in zip(outs, _adam_math(g_ref[...], w_ref[...], m_ref[...], v_ref[...])):
                o_ref[...] = val

    small_ins = [as2d(t[n]) for n in small_names for t in (gsmall, wv, mv, vv)]
    small_outs = pl.pallas_call(adam_small, name="adam_small",
                                out_shape=[SDS(as2d(wv[n]).shape, F32) for n in small_names for _ in range(3)])(*small_ins)
    for i, n in enumerate(small_names):
        sd, sm, sv = (t.reshape(wv[n].shape) for t in small_outs[3 * i:3 * i + 3])
        outs_g[n], outs_d[n], outs_m[n], outs_v[n] = gsmall[n], sd, sm, sv

    lead = lambda n, a: a if args[n].ndim == 1 else a[None]
    grad_x = grad_x[None]
    return (loss, grad_x, *[lead(n, outs_g[n]) for n in _WEIGHTS], *[lead(n, outs_d[n]) for n in _WEIGHTS],
            *[lead(n, outs_m[n]) for n in _WEIGHTS], *[lead(n, outs_v[n]) for n in _WEIGHTS])
```

```python
import functools
import math

import jax
import jax.numpy as jnp
from jax import lax
from jax.experimental import pallas as pl
from jax.experimental.pallas import tpu as pltpu

F32 = jnp.float32
BF16 = jnp.bfloat16
SDS = jax.ShapeDtypeStruct

N_DEV = 8
D = 2048
NQ, NKV, HD = 16, 2, 64
AW = NQ * HD
BLK = 128
SW, G, H, P = 512, 32, 16, 64
NS = G * P
DFF = 5632
INC = AW + 2 * NKV * HD + SW + 2 * D
C_K, C_U, C_PAD = AW, AW + 2 * NKV * HD, AW + 2 * NKV * HD + SW
C_GA, C_GS, INP = D, 2 * D, 3 * D
RMS_EPS = 1e-6
NEG_BIG = -1e30
ADAM_LR, ADAM_B1, ADAM_B2, ADAM_EPS, ADAM_WD, ADAM_STEP = 0.001, 0.9, 0.999, 1e-08, 0.01, 10
NSEG = 8
VMEM_CAP_MB = 60
MESH_AXES = ("x", "y", "c")


def _pad_cols(a):
    zeros = jnp.zeros(a.shape[:-1] + (C_GA - C_PAD,), a.dtype)
    return jnp.concatenate([a[..., :C_PAD], zeros, a[..., C_PAD:]], axis=-1)


def _unpad_cols(a):
    return jnp.concatenate([a[..., :C_PAD], a[..., C_GA:]], axis=-1)


def _cparams(sem, vmem_mb):
    return pltpu.CompilerParams(dimension_semantics=sem, vmem_limit_bytes=min(int(vmem_mb), VMEM_CAP_MB) << 20)


LANES = 128


def _tile(dim, pref):
    if dim <= pref:
        return dim
    for t in range(pref - pref % LANES, 0, -LANES):
        if dim % t == 0:
            return t
    raise ValueError(f"no tile for {dim}")


def _mm(a, b, *, ta=False, tb=False, bias=None, res=None, out_dtype=F32, tm=1024, tn=1024, tk=3072, name,
        a2=None, b2=None, extras=(), epilogue=None, outs=None, stack_out=False, n_cols=None,
        b_col_off=0, b2_col_off=0):
    m, k = (a.shape[1], a.shape[0]) if ta else a.shape
    n = n_cols or (b.shape[0] if tb else b.shape[1])
    assert (b.shape[1] if tb else b.shape[0]) == k, (a.shape, b.shape, ta, tb)
    tm, tn, tk = _tile(m, tm), _tile(n, tn), _tile(k, tk)
    nk = k // tk
    dims = (((0 if ta else 1,), (1 if tb else 0,)), ((), ()))
    has_bias, has_res, has_b2 = bias is not None, res is not None, b2 is not None
    has_a2 = a2 is not None
    assert not (has_b2 and (nk > 1 or ta or tb)) and not (has_a2 and not has_b2)
    if epilogue is None:
        outs = [(SDS((n // tn, m, tn) if stack_out else (m, n), out_dtype), "tile")]
    n_ex, n_out = len(extras), len(outs)

    def body(*refs):
        a_ref, b_ref = refs[0], refs[1]
        pos = 2
        a2_ref = refs[pos] if has_a2 else a_ref
        pos += has_a2
        b2_ref = refs[pos] if has_b2 else None
        pos += has_b2
        bias_ref = refs[pos] if has_bias else None
        pos += has_bias
        res_ref = refs[pos] if has_res else None
        pos += has_res
        ex_refs = refs[pos:pos + n_ex]
        o_refs = refs[pos + n_ex:pos + n_ex + n_out]
        i = pl.program_id(1)

        def product(rhs_ref, cols=None, lhs=None):
            rhs = rhs_ref[...] if cols is None else (rhs_ref[cols, :] if tb else rhs_ref[:, cols])
            lhs = a_ref[...].astype(BF16) if lhs is None else lhs
            return lax.dot_general(lhs, rhs.astype(BF16), dims, preferred_element_type=F32)

        def finish(r, cols):
            if has_bias:
                r = r + bias_ref[:, cols]
            if has_res:
                r = r + res_ref[:, cols].astype(F32)
            if epilogue is None:
                o_refs[0][:, cols] = r.astype(o_refs[0].dtype)
                return
            r2 = None
            if has_b2:
                r2 = jnp.dot(a2_ref[...].astype(BF16), b2_ref[:, cols].astype(BF16), preferred_element_type=F32)
            vals = epilogue(i, cols, r, r2, *ex_refs)
            for o_ref, v, (_, kind) in zip(o_refs, vals, outs):
                if kind == "tile":
                    o_ref[:, cols] = v.astype(o_ref.dtype)
                else:
                    @pl.when(i == 0)
                    def _(o_ref=o_ref, v=v):
                        o_ref[:, cols] = v.astype(o_ref.dtype)

                    @pl.when(i > 0)
                    def _(o_ref=o_ref, v=v):
                        o_ref[:, cols] += v.astype(o_ref.dtype)

        whole = pl.ds(0, tn)
        if nk == 1:
            finish(product(b_ref), whole)
            return
        acc_ref = refs[-1]
        kk = pl.program_id(2)

        @pl.when(kk == 0)
        def _():
            acc_ref[...] = product(b_ref)

        @pl.when(jnp.logical_and(kk > 0, kk < nk - 1))
        def _():
            acc_ref[...] += product(b_ref)

        @pl.when(kk == nk - 1)
        def _():
            finish(acc_ref[...] + product(b_ref), whole)

    b_spec = (pl.BlockSpec((tn, tk), lambda j, i, kk: (j + b_col_off, kk)) if tb else
              pl.BlockSpec((tk, tn), lambda j, i, kk: (kk, j + b_col_off)))
    ins = [a, b]
    in_specs = [pl.BlockSpec((tk, tm), lambda j, i, kk: (kk, i)) if ta else pl.BlockSpec((tm, tk), lambda j, i, kk: (i, kk)),
                b_spec]
    tile_spec = pl.BlockSpec((tm, tn), lambda j, i, kk: (i, j))
    byt = 2 * tm * tk * a.dtype.itemsize + 2 * tk * tn * b.dtype.itemsize
    byt += (2 + has_b2) * 4 * tm * tn
    if has_a2:
        ins.append(a2)
        in_specs.append(pl.BlockSpec((tm, a2.shape[1]), lambda j, i, kk: (i, 0)))
        byt += 2 * tm * a2.shape[1] * a2.dtype.itemsize
    if has_b2:
        ins.append(b2)
        in_specs.append(pl.BlockSpec((b2.shape[0], tn), lambda j, i, kk: (0, j + b2_col_off)))
        byt += 2 * b2.shape[0] * tn * b2.dtype.itemsize
    if has_bias:
        ins.append(bias)
        in_specs.append(pl.BlockSpec((1, tn), lambda j, i, kk: (0, j)))
    if has_res:
        ins.append(res)
        in_specs.append(tile_spec)
        byt += 2 * tm * tn * res.dtype.itemsize
    for arr, kind, arg in extras:
        ins.append(arr)
        if kind == "tile":
            in_specs.append(pl.BlockSpec((tm, tn), lambda j, i, kk, arg=arg: (i, j + arg)))
            byt += 2 * tm * tn * arr.dtype.itemsize + 4 * tm * tn
        elif kind == "col":
            in_specs.append(pl.BlockSpec((arr.shape[0], tn), lambda j, i, kk, arg=arg: (0, j + arg)))
        else:
            in_specs.append(pl.BlockSpec(arg[0], lambda j, i, kk, im=arg[1]: im(j, i)))
    out_specs = []
    for sds, kind in outs:
        if kind == "tile":
            out_specs.append(pl.BlockSpec((None, tm, tn), lambda j, i, kk: (j, i, 0)) if stack_out else tile_spec)
            byt += 2 * tm * tn * jnp.dtype(sds.dtype).itemsize
        else:
            out_specs.append(pl.BlockSpec((sds.shape[0], tn), lambda j, i, kk: (0, j)))
    res_ = pl.pallas_call(
        body, out_shape=tuple(o[0] for o in outs), grid=(n // tn, m // tm, nk), in_specs=in_specs,
        out_specs=tuple(out_specs), scratch_shapes=[pltpu.VMEM((tm, tn), F32)] if nk > 1 else [], name=name,
        compiler_params=_cparams(("arbitrary", "arbitrary", "arbitrary"), byt / 2**20 + (8 if epilogue is None else 20)),
    )(*ins)
    return res_[0] if epilogue is None else res_


def _ew(fn, ins, outs, grid, name, vmem_mb=40, into=None):
    n_in = len(ins)
    accs = [o[3] for o in outs]

    def body(*refs):
        c, r = pl.program_id(0), pl.program_id(1)
        vals = fn(c, r, *[ref[...].astype(F32) for ref in refs[:n_in]])
        for o_ref, v, acc in zip(refs[n_in + (into is not None):], vals, accs):
            if acc is None:
                o_ref[...] = v.astype(o_ref.dtype)
            else:
                first = (r == 0) if acc == "r" else jnp.logical_and(r == 0, c == 0)

                @pl.when(first)
                def _(o_ref=o_ref, v=v):
                    o_ref[...] = v.astype(o_ref.dtype)

                @pl.when(jnp.logical_not(first))
                def _(o_ref=o_ref, v=v):
                    o_ref[...] += v.astype(o_ref.dtype)

    in_specs = [pl.BlockSpec(bs, im) for _, bs, im in ins]
    args = [a for a, _, _ in ins]
    if into is not None:
        in_specs.append(pl.BlockSpec(memory_space=pl.ANY))
        args.append(into)
    res = pl.pallas_call(
        body, out_shape=tuple(o[0] for o in outs), grid=grid, in_specs=in_specs,
        out_specs=tuple(pl.BlockSpec(bs, im) for _, bs, im, _ in outs), name=name,
        input_output_aliases={} if into is None else {n_in: 0},
        compiler_params=_cparams(("arbitrary", "arbitrary"), vmem_mb),
    )(*args)
    return res


def _rc(tm, tc, coff=0):
    return (tm, tc), (lambda c, r: (r, c + coff))


def _col(rows, tc, coff=0):
    return (rows, tc), (lambda c, r: (0, c + coff))


def _gelu(x):
    return 0.5 * x * (1.0 + lax.erf(x * (2.0 ** -0.5)))


def _gelu_and_grad(x):
    cdf = 0.5 * (1.0 + lax.erf(x * (2.0 ** -0.5)))
    return x * cdf, cdf + x * jnp.exp(-0.5 * x * x) * (1.0 / math.sqrt(2.0 * math.pi))


def _gelu_grad(x):
    return _gelu_and_grad(x)[1]


def _sigmoid(x):
    return 1.0 / (1.0 + jnp.exp(-x))


def _shift_rows(x, halo, s):
    rolled = pltpu.roll(x, s, 0)
    row8 = lax.broadcasted_iota(jnp.int32, halo.shape, 0)
    head = jnp.where(row8 < s, pltpu.roll(halo, s, 0), rolled[0:8])
    return jnp.concatenate([head, rolled[8:]], axis=0)


def _shift_rows_up(x, halo, s):
    tm = x.shape[0]
    rolled = pltpu.roll(x, tm - s, 0)
    row8 = lax.broadcasted_iota(jnp.int32, halo.shape, 0)
    tail = jnp.where(row8 >= 8 - s, pltpu.roll(halo, 8 - s, 0), rolled[tm - 8:])
    return jnp.concatenate([rolled[:tm - 8], tail], axis=0)


def _rmsnorm_fwd(x, g, name, tm=512):
    L = x.shape[0]

    def fn(c, r, xb, gb):
        rstd = lax.rsqrt(jnp.mean(xb * xb, axis=-1, keepdims=True) + RMS_EPS)
        return ((xb * rstd) * gb,)

    return _ew(fn, [(x, *_rc(tm, D)), (g, *_col(1, D))], [(SDS((L, D), BF16), *_rc(tm, D), None)], (1, L // tm), name)[0]


def _rmsnorm_bwd(dh, x, g, dres, name, tm=512):
    L = x.shape[0]

    def fn(c, r, dhb, xb, gb, drb):
        rstd = lax.rsqrt(jnp.mean(xb * xb, axis=-1, keepdims=True) + RMS_EPS)
        y = xb * rstd
        dy = dhb * gb
        dx = rstd * (dy - y * jnp.mean(dy * y, axis=-1, keepdims=True))
        return drb + dx, jnp.sum(dhb * y, axis=0, keepdims=True)

    return _ew(fn, [(dh, *_rc(tm, D)), (x, *_rc(tm, D)), (g, *_col(1, D)), (dres, *_rc(tm, D))],
               [(SDS((L, D), F32), *_rc(tm, D), None), (SDS((1, D), F32), *_col(1, D), "all")], (1, L // tm), name,
               vmem_mb=56)


def _final_loss(x2, g, tgt, name, tm=512):
    L = x2.shape[0]

    def fn(c, r, xb, gb, tb):
        rstd = lax.rsqrt(jnp.mean(xb * xb, axis=-1, keepdims=True) + RMS_EPS)
        y = xb * rstd
        err = y * gb - tb
        dout = err * (1.0 / D)
        dy = dout * gb
        dx = rstd * (dy - y * jnp.mean(dy * y, axis=-1, keepdims=True))
        return dx, dx, jnp.sum(err * err, axis=0, keepdims=True) * (0.5 / D), jnp.sum(dout * y, axis=0, keepdims=True)

    return _ew(fn, [(x2, *_rc(tm, D)), (g, *_col(1, D)), (tgt, *_rc(tm, D))],
               [(SDS((L, D), F32), *_rc(tm, D), None), (SDS((L, D), BF16), *_rc(tm, D), None),
                (SDS((1, D), F32), *_col(1, D), "all"),
                (SDS((1, D), F32), *_col(1, D), "all")], (1, L // tm), name, vmem_mb=56)


def _slope(h):
    return 2.0 ** (-8.0 * (h + 1) / NQ)


def _attn_bias():
    qi = lax.broadcasted_iota(jnp.int32, (BLK, 2 * BLK), 0)
    si = lax.broadcasted_iota(jnp.int32, (BLK, 2 * BLK), 1)
    dist = qi + BLK - si
    band = (dist >= 0) & (dist < BLK)
    slopes = jnp.asarray([_slope(h) for h in range(NQ)], F32)[:, None, None]
    alibi = -slopes * dist.astype(F32)[None]
    return jnp.stack([jnp.where((band & (si >= BLK))[None], alibi, NEG_BIG), jnp.where(band[None], alibi, NEG_BIG)])


def _attn_kv(kvc, kvp):
    kv = jnp.concatenate([kvp, kvc], axis=0).astype(F32)
    lo = lax.broadcasted_iota(jnp.int32, (2 * BLK, 128), 1) < HD

    def halves(t):
        tr = pltpu.roll(t, HD, 1)
        z = jnp.zeros_like(t)
        return {(0, 0): jnp.where(lo, t, z).astype(BF16), (0, 1): jnp.where(lo, z, tr).astype(BF16),
                (1, 0): jnp.where(lo, tr, z).astype(BF16), (1, 1): jnp.where(lo, z, t).astype(BF16)}

    return halves(kv[:, :128]), halves(kv[:, 128:])


_NT = (((1,), (1,)), ((), ()))
_TN = (((0,), (0,)), ((), ()))
_ATTN_SPECS = [pl.BlockSpec(memory_space=pltpu.SMEM),
               pl.BlockSpec((None, NQ, BLK, 2 * BLK), lambda n: (jnp.minimum(n, 1), 0, 0, 0)),
               pl.BlockSpec((BLK, AW), lambda n: (n, 0)),
               pl.BlockSpec((BLK, 256), lambda n: (n, C_K // 256)),
               pl.BlockSpec((BLK, 256), lambda n: (jnp.maximum(n - 1, 0), C_K // 256))]


def _attn_scores(q_ref, bias_ref, kmat, sc_ref):
    for j in range(NQ // 2):
        qs = q_ref[:, 128 * j:128 * (j + 1)] * (HD ** -0.5)
        for e in range(2):
            h = 2 * j + e
            sc_ref[h] = lax.dot_general(qs, kmat[(j // (NQ // 4), e)], _NT, preferred_element_type=F32) + bias_ref[h]


def _softmax_with_sink(s, sink):
    m = jnp.maximum(jnp.max(s, axis=-1, keepdims=True), sink)
    p = jnp.exp(s - m)
    esink = jnp.exp(sink - m)
    den = jnp.sum(p, axis=-1, keepdims=True) + esink
    return p / den, esink / den


def _attn_fwd(projb, sinks, bias, name):
    L = projb.shape[0]

    def body(s_ref, bias_ref, q_ref, kvc_ref, kvp_ref, o_ref, sc_ref, pr_ref):
        kmat, vmat = _attn_kv(kvc_ref[...], kvp_ref[...])
        _attn_scores(q_ref, bias_ref, kmat, sc_ref)
        for h in range(NQ):
            pr_ref[h] = _softmax_with_sink(sc_ref[h], s_ref[0, h])[0].astype(BF16)
        for j in range(NQ // 2):
            g = j // (NQ // 4)
            acc = jnp.dot(pr_ref[2 * j], vmat[(g, 0)], preferred_element_type=F32)
            acc = acc + jnp.dot(pr_ref[2 * j + 1], vmat[(g, 1)], preferred_element_type=F32)
            o_ref[:, 128 * j:128 * (j + 1)] = acc.astype(BF16)

    return pl.pallas_call(
        body, out_shape=SDS((L, AW), BF16), grid=(L // BLK,), in_specs=_ATTN_SPECS,
        out_specs=pl.BlockSpec((BLK, AW), lambda n: (n, 0)), name=name,
        scratch_shapes=[pltpu.VMEM((NQ, BLK, 2 * BLK), F32), pltpu.VMEM((NQ, BLK, 2 * BLK), BF16)],
        compiler_params=_cparams(("arbitrary",), 32),
    )(sinks, bias, projb, projb, projb)


def _attn_bwd(projb, sinks, bias, dattn, name):
    L = projb.shape[0]

    def body(s_ref, bias_ref, q_ref, kvc_ref, kvp_ref, do_ref, dq_ref, dcur_ref, dprev_ref, dsink_ref,
             sc_ref, dp_ref, ds_ref, pr_ref, qm_ref, dm_ref):
        n = pl.program_id(0)
        kmat, vmat = _attn_kv(kvc_ref[...], kvp_ref[...])
        _attn_scores(q_ref, bias_ref, kmat, sc_ref)
        for h in range(NQ):
            j, e = h // 2, h % 2
            dp_ref[h] = lax.dot_general(do_ref[:, 128 * j:128 * (j + 1)], vmat[(j // (NQ // 4), e)], _NT,
                                        preferred_element_type=F32)
        lane = lax.broadcasted_iota(jnp.int32, (1, 128), 1)
        dsv = jnp.zeros((1, 128), F32)
        for h in range(NQ):
            p, psink = _softmax_with_sink(sc_ref[h], s_ref[0, h])
            dp = dp_ref[h]
            drow = jnp.sum(p * dp, axis=-1, keepdims=True)
            ds_ref[h] = (p * (dp - drow)).astype(BF16)
            pr_ref[h] = p.astype(BF16)
            dsv = dsv + jnp.where(lane == h, -jnp.sum(psink * drow, axis=0, keepdims=True), 0.0)
        lo128 = lax.broadcasted_iota(jnp.int32, (BLK, 128), 1) < HD
        for j in range(NQ // 2):
            g = j // (NQ // 4)
            qs = q_ref[:, 128 * j:128 * (j + 1)] * (HD ** -0.5)
            dop = do_ref[:, 128 * j:128 * (j + 1)]
            zb = jnp.zeros_like(qs)
            dqp = jnp.zeros((BLK, 128), F32)
            for e in range(2):
                h = 2 * j + e
                half = lo128 if e == 0 else jnp.logical_not(lo128)
                dqp = dqp + jnp.dot(ds_ref[h], kmat[(g, e)], preferred_element_type=F32)
                qm_ref[h] = jnp.where(half, qs, zb)
                dm_ref[h] = jnp.where(half, dop, zb)
            dq_ref[:, 128 * j:128 * (j + 1)] = (dqp * (HD ** -0.5)).astype(BF16)
        hk = NQ // NKV
        rows = lambda ref, g: ref[g * hk:(g + 1) * hk].reshape(hk * BLK, ref.shape[-1])
        dk = [lax.dot_general(rows(ds_ref, g), rows(qm_ref, g), _TN, preferred_element_type=F32) for g in range(NKV)]
        dv = [lax.dot_general(rows(pr_ref, g), rows(dm_ref, g), _TN, preferred_element_type=F32) for g in range(NKV)]
        lo256 = lax.broadcasted_iota(jnp.int32, (2 * BLK, 128), 1) < HD
        tot = [t + pltpu.roll(t, HD, 1) for t in (dk[0], dk[1], dv[0], dv[1])]
        dkv = jnp.concatenate([jnp.where(lo256, tot[0], tot[1]), jnp.where(lo256, tot[2], tot[3])], axis=1)
        dprev_ref[...] = dkv[:BLK]
        dcur_ref[...] = dkv[BLK:]

        @pl.when(n == 0)
        def _():
            dsink_ref[...] = dsv

        @pl.when(n > 0)
        def _():
            dsink_ref[...] += dsv

    tile = (NQ, BLK, 2 * BLK)
    return pl.pallas_call(
        body, out_shape=(SDS((L, AW), BF16), SDS((L, 256), F32), SDS((L, 256), F32), SDS((1, 128), F32)), grid=(L // BLK,),
        in_specs=_ATTN_SPECS + [pl.BlockSpec((BLK, AW), lambda n: (n, 0))],
        out_specs=(pl.BlockSpec((BLK, AW), lambda n: (n, 0)), pl.BlockSpec((BLK, 256), lambda n: (n, 0)),
                   pl.BlockSpec((BLK, 256), lambda n: (n, 0)), pl.BlockSpec((1, 128), lambda n: (0, 0))),
        scratch_shapes=[pltpu.VMEM(tile, F32), pltpu.VMEM(tile, F32), pltpu.VMEM(tile, BF16), pltpu.VMEM(tile, BF16),
                        pltpu.VMEM((NQ, BLK, 128), BF16), pltpu.VMEM((NQ, BLK, 128), BF16)],
        name=name, compiler_params=_cparams(("arbitrary",), 40),
    )(sinks, bias, projb, projb, projb, dattn)


def _disc(a_re, a_im, logdt, b_re, b_im):
    dt = jnp.exp(logdt)
    mag = jnp.exp(a_re * dt)
    ab_re = mag * jnp.cos(a_im * dt)
    ab_im = mag * jnp.sin(a_im * dt)
    nr = ab_re - 1.0
    ni = ab_im
    den = a_re * a_re + a_im * a_im
    z_re = (nr * a_re + ni * a_im) / den
    z_im = (ni * a_re - nr * a_im) / den
    return ab_re, ab_im, z_re * b_re - z_im * b_im, z_re * b_im + z_im * b_re


def _group_mask():
    row = lax.broadcasted_iota(jnp.int32, (SW, NS), 0) // H
    col = lax.broadcasted_iota(jnp.int32, (SW, NS), 1) // P
    return row == col


def _block_diag(re, im):
    mask = _group_mask()
    z = jnp.zeros((SW, NS), F32)
    return jnp.concatenate([jnp.where(mask, jnp.tile(re, (G, 1)), z), jnp.where(mask, jnp.tile(im, (G, 1)), z)], axis=1)


def _block_diag_t(big):
    mask = _group_mask()
    z = jnp.zeros((SW, NS), F32)
    re = jnp.sum(jnp.where(mask, big[:, :NS], z).reshape(G, H, NS), axis=0)
    im = jnp.sum(jnp.where(mask, big[:, NS:], z).reshape(G, H, NS), axis=0)
    return re, im


def _ssm_prep(a_re, a_im, logdt, b_re, b_im, c_re, c_im, name):
    def body(are, aim, ldt, bre, bim, cre, cim, ab_ref, bm_ref, cm_ref):
        ab_re, ab_im, bb_re, bb_im = _disc(are[...], aim[...], ldt[...], bre[...], bim[...])
        ab_ref[...] = jnp.concatenate([ab_re, ab_im], axis=1)
        bm_ref[...] = _block_diag(bb_re, bb_im).astype(BF16)
        cm_ref[...] = _block_diag(cre[...], -cim[...]).astype(BF16)

    return pl.pallas_call(body, out_shape=(SDS((1, 2 * NS), F32), SDS((SW, 2 * NS), BF16), SDS((SW, 2 * NS), BF16)),
                          name=name, compiler_params=pltpu.CompilerParams(vmem_limit_bytes=48 << 20),
                          )(a_re, a_im, logdt, b_re, b_im, c_re, c_im)


def _ssm_param_bwd(a_re, a_im, logdt, b_re, b_im, dab8, dbm, dcm, name):
    def body(are, aim, ldt, bre, bim, dab_ref, dbm_ref, dcm_ref, o_are, o_aim, o_ldt, o_bre, o_bim, o_cre, o_cim):
        dab = jnp.sum(dab_ref[...], axis=0, keepdims=True)
        dbb_re, dbb_im = _block_diag_t(dbm_ref[...])
        _, vjp = jax.vjp(_disc, are[...], aim[...], ldt[...], bre[...], bim[...])
        d_are, d_aim, d_ldt, d_bre, d_bim = vjp((dab[:, :NS], dab[:, NS:], dbb_re, dbb_im))
        o_are[...], o_aim[...], o_ldt[...], o_bre[...], o_bim[...] = d_are, d_aim, d_ldt, d_bre, d_bim
        dc_re, dc_imn = _block_diag_t(dcm_ref[...])
        o_cre[...] = dc_re
        o_cim[...] = -dc_imn

    v1, vh = SDS((1, NS), F32), SDS((H, NS), F32)
    return pl.pallas_call(body, out_shape=(v1, v1, v1, vh, vh, vh, vh), name=name,
                          compiler_params=pltpu.CompilerParams(vmem_limit_bytes=56 << 20),
                          )(a_re, a_im, logdt, b_re, b_im, dab8, dbm, dcm)


def _ssm_scan(src, wmat, ab, *, reverse, ends=None, xs=None, init=None, wproj=None, name, tk=32):
    L = src.shape[0]
    seg_len = L // NSEG
    tk = min(tk, seg_len)
    rows = NSEG * tk
    nch = L // rows
    n_sq = int(math.log2(seg_len))
    assert 2 ** n_sq == seg_len and L % rows == 0
    first_pass = ends is None
    with_dab = (not first_pass) and reverse
    with_proj = wproj is not None
    assert not (with_proj and first_pass)
    slab = 512
    n_slab = NS // slab

    def body(*refs):
        src_ref, w_ref, ab_ref = refs[:3]
        pos = 3
        if not first_pass:
            ends_ref = refs[pos]
            pos += 1
        if with_dab:
            xs_ref, xsh_ref, init_ref = refs[pos:pos + 3]
            pos += 3
        if with_proj:
            wproj_ref = refs[pos]
            pos += 1
        if first_pass:
            (e_ref,) = refs[pos:pos + 1]
            pos += 1
        else:
            st_out_ref, aux_ref = refs[pos:pos + 2]
            pos += 2
        if with_proj:
            proj_ref = refs[pos]
            pos += 1
        buf_ref, st_ref = refs[pos:pos + 2]
        i = pl.program_id(0)
        a_re = ab_ref[:, :NS]
        a_im = -ab_ref[:, NS:] if reverse else ab_ref[:, NS:]

        @pl.when(i == 0)
        def _():
            if first_pass:
                st_ref[...] = jnp.zeros_like(st_ref)
            else:
                pr, pi = a_re, a_im
                for _ in range(n_sq):
                    pr, pi = pr * pr - pi * pi, 2.0 * pr * pi
                zr = jnp.zeros((1, NS), F32)
                cr, ci = zr, zr
                order = list(range(NSEG - 1, -1, -1)) if reverse else list(range(NSEG))
                st_ref[order[0]:order[0] + 1, :] = jnp.zeros((1, 2 * NS), F32)
                for jprev, j in zip(order[:-1], order[1:]):
                    er, ei = ends_ref[jprev:jprev + 1, :NS], ends_ref[jprev:jprev + 1, NS:]
                    cr, ci = er + pr * cr - pi * ci, ei + pr * ci + pi * cr
                    st_ref[j:j + 1, :NS] = cr
                    st_ref[j:j + 1, NS:] = ci
                if not reverse:
                    aux_ref[...] = st_ref[...]
                else:
                    aux_ref[...] = jnp.zeros_like(aux_ref)

        buf_ref[...] = jnp.dot(src_ref[...].astype(BF16), w_ref[...], preferred_element_type=F32)

        for s in range(n_slab):
            re_sl, im_sl = pl.ds(s * slab, slab), pl.ds(NS + s * slab, slab)
            ar = jnp.broadcast_to(a_re[:, s * slab:(s + 1) * slab], (NSEG, slab))
            ai = jnp.broadcast_to(a_im[:, s * slab:(s + 1) * slab], (NSEG, slab))

            def step(t, carry, re_sl=re_sl, im_sl=im_sl, ar=ar, ai=ai):
                k = (tk - 1 - t) if reverse else t
                r0 = pl.multiple_of(k * NSEG, NSEG)
                xr, xi = carry[0], carry[1]
                nr = ar * xr - ai * xi + buf_ref[pl.ds(r0, NSEG), re_sl]
                ni = ar * xi + ai * xr + buf_ref[pl.ds(r0, NSEG), im_sl]
                if not first_pass:
                    buf_ref[pl.ds(r0, NSEG), re_sl] = nr
                    buf_ref[pl.ds(r0, NSEG), im_sl] = ni
                if not with_dab:
                    return nr, ni
                rp = pl.multiple_of((k - 1) * NSEG, NSEG)
                xpr, xpi = xs_ref[pl.ds(rp, NSEG), re_sl], xs_ref[pl.ds(rp, NSEG), im_sl]
                return nr, ni, carry[2] + nr * xpr + ni * xpi, carry[3] + ni * xpr - nr * xpi

            carry = (st_ref[:, re_sl], st_ref[:, im_sl])
            if with_dab:
                z = jnp.zeros((NSEG, slab), F32)
                carry = lax.fori_loop(0, tk - 1, step, carry + (z, z))
                xr, xi, dr, di = carry
                nr = ar * xr - ai * xi + buf_ref[pl.ds(0, NSEG), re_sl]
                ni = ar * xi + ai * xr + buf_ref[pl.ds(0, NSEG), im_sl]
                buf_ref[pl.ds(0, NSEG), re_sl] = nr
                buf_ref[pl.ds(0, NSEG), im_sl] = ni
                at_start = i == nch - 1
                xpr = jnp.where(at_start, init_ref[:, re_sl], xsh_ref[:, re_sl])
                xpi = jnp.where(at_start, init_ref[:, im_sl], xsh_ref[:, im_sl])
                aux_ref[:, re_sl] += dr + nr * xpr + ni * xpi
                aux_ref[:, im_sl] += di + ni * xpr - nr * xpi
                carry = (nr, ni)
            else:
                carry = lax.fori_loop(0, tk, step, carry)
            st_ref[:, re_sl] = carry[0]
            st_ref[:, im_sl] = carry[1]

        if first_pass:
            @pl.when(i == nch - 1)
            def _():
                e_ref[...] = st_ref[...]
        else:
            st_out_ref[...] = buf_ref[...].astype(st_out_ref.dtype)
            if with_proj:
                proj_ref[...] = lax.dot_general(buf_ref[...].astype(BF16), wproj_ref[...], _NT, preferred_element_type=F32)

    chunk = (lambda i: (nch - 1 - i, 0)) if reverse else (lambda i: (i, 0))
    whole = lambda i: (0, 0)
    ins = [src, wmat, ab]
    once = pl.Buffered(1)
    in_specs = [pl.BlockSpec((rows, SW), chunk), pl.BlockSpec((SW, 2 * NS), whole, pipeline_mode=once),
                pl.BlockSpec((1, 2 * NS), whole)]
    small = SDS((NSEG, 2 * NS), F32)
    small_spec = pl.BlockSpec((NSEG, 2 * NS), whole)
    if not first_pass:
        ins.append(ends)
        in_specs.append(small_spec)
    if with_dab:
        ins += [xs, xs, init]
        in_specs += [pl.BlockSpec((rows, 2 * NS), chunk),
                     pl.BlockSpec((NSEG, 2 * NS), lambda i: (jnp.maximum((nch - 1 - i) * tk - 1, 0), 0)),
                     small_spec]
    if with_proj:
        ins.append(wproj)
        in_specs.append(pl.BlockSpec((SW, 2 * NS), whole, pipeline_mode=once))
    if first_pass:
        out_shape, out_specs = small, small_spec
    else:
        out_shape = (SDS((L, 2 * NS), BF16 if reverse else F32), small)
        out_specs = (pl.BlockSpec((rows, 2 * NS), chunk), small_spec)
        if with_proj:
            out_shape += (SDS((L, SW), F32),)
            out_specs += (pl.BlockSpec((rows, SW), chunk),)
    return pl.pallas_call(
        body, out_shape=out_shape, grid=(nch,), in_specs=in_specs, out_specs=out_specs,
        scratch_shapes=[pltpu.VMEM((rows, 2 * NS), F32), pltpu.VMEM((NSEG, 2 * NS), F32)], name=name,
        compiler_params=_cparams(("arbitrary",), 56),
    )(*ins)


def _to_segments(a):
    L, c = a.shape
    return a.reshape(NSEG, L // NSEG, c).transpose(1, 0, 2).reshape(L, c)


def _from_segments(a):
    L, c = a.shape
    return a.reshape(L // NSEG, NSEG, c).transpose(1, 0, 2).reshape(L, c)


def _peer(x, y, c, m):
    return ((1 - x) if (m >> 2) & 1 else x, (1 - y) if (m >> 1) & 1 else y, (1 - c) if m & 1 else c)


def _dev_index(p):
    return 4 * p[0] + 2 * p[1] + p[2]


def _exchange(arrs, scatter, name):
    n = len(arrs)

    def body(*refs):
        ins, outs = refs[:n], refs[n:2 * n]
        send_sems, recv_sems, loc_sems = refs[2 * n:]
        x, y, c = lax.axis_index("x"), lax.axis_index("y"), lax.axis_index("c")
        me = _dev_index((x, y, c))

        def src(w, to):
            return ins[w].at[to] if scatter else ins[w]

        def local(w):
            return pltpu.make_async_copy(src(w, me), outs[w].at[me], loc_sems.at[w])

        def remote(w, m):
            peer = _peer(x, y, c, m)
            return pltpu.make_async_remote_copy(src_ref=src(w, _dev_index(peer)), dst_ref=outs[w].at[me],
                                                send_sem=send_sems.at[w, m - 1], recv_sem=recv_sems.at[w, m - 1],
                                                device_id=peer, device_id_type=pl.DeviceIdType.MESH)

        def arrival(w, m):
            peer = _peer(x, y, c, m)
            return pltpu.make_async_remote_copy(src_ref=src(w, me), dst_ref=outs[w].at[_dev_index(peer)],
                                                send_sem=send_sems.at[w, m - 1], recv_sem=recv_sems.at[w, m - 1],
                                                device_id=peer, device_id_type=pl.DeviceIdType.MESH)

        for w in range(n):
            local(w).start()
        for w in range(n):
            for m in range(1, N_DEV):
                remote(w, m).start()
        for w in range(n):
            for m in range(1, N_DEV):
                arrival(w, m).wait_recv()
        for w in range(n):
            for m in range(1, N_DEV):
                remote(w, m).wait_send()
        for w in range(n):
            local(w).wait()

    anyspec = pl.BlockSpec(memory_space=pl.ANY)
    out_shape = tuple(SDS(a.shape if scatter else (N_DEV,) + a.shape, a.dtype) for a in arrs)
    return pl.pallas_call(
        body, out_shape=out_shape, in_specs=[anyspec] * n, out_specs=tuple([anyspec] * n),
        scratch_shapes=[pltpu.SemaphoreType.DMA((n, N_DEV - 1)), pltpu.SemaphoreType.DMA((n, N_DEV - 1)),
                        pltpu.SemaphoreType.DMA((n,))],
        name=name, compiler_params=pltpu.CompilerParams(has_side_effects=True),
    )(*arrs)


_HBM = pl.BlockSpec(memory_space=pltpu.HBM)
_SEM = pl.BlockSpec(memory_space=pltpu.SEMAPHORE)
_EFFECT = pltpu.SideEffectType.DATAFLOW_SIDE_EFFECTING


def _sem_index(w, m):
    return w * (N_DEV - 1) + m - 1


_ALL_MASKS = tuple(range(1, N_DEV))
_CHIP_MASKS = (2, 4, 6)
_FIRST_HOP_MASKS = (1,) + _CHIP_MASKS


def _exchange_start(arrs, scatter, name, masks=_ALL_MASKS):
    n = len(arrs)
    lands = [lax.empty(a.shape if scatter else (N_DEV,) + a.shape, a.dtype) for a in arrs]

    def body(*refs):
        ins, zones = refs[:n], refs[n:2 * n]
        send_sems, recv_sems = refs[2 * n], refs[2 * n + 1]
        token = refs[-1]
        x, y, c = lax.axis_index("x"), lax.axis_index("y"), lax.axis_index("c")
        me = _dev_index((x, y, c))
        for w in range(n):
            for m in masks:
                peer = _peer(x, y, c, m)
                pltpu.make_async_remote_copy(
                    src_ref=ins[w].at[_dev_index(peer)] if scatter else ins[w], dst_ref=zones[w].at[me],
                    send_sem=send_sems.at[_sem_index(w, m)], recv_sem=recv_sems.at[_sem_index(w, m)],
                    device_id=peer, device_id_type=pl.DeviceIdType.MESH).start()
        token[...] = jnp.zeros_like(token)

    sems = pltpu.SemaphoreType.DMA((n * (N_DEV - 1),))
    res = pl.pallas_call(
        body, name=name,
        out_shape=(sems, sems, *[pltpu.HBM(a.shape, a.dtype) for a in arrs], *[pltpu.HBM(z.shape, z.dtype) for z in lands],
                   SDS((8, 128), F32)),
        in_specs=[_HBM] * (2 * n), out_specs=(_SEM, _SEM, *([_HBM] * (2 * n)), pl.BlockSpec(memory_space=pltpu.VMEM)),
        input_output_aliases={i: 2 + i for i in range(2 * n)},
        compiler_params=pltpu.CompilerParams(has_side_effects=_EFFECT),
    )(*[pltpu.with_memory_space_constraint(a, pltpu.HBM) for a in arrs],
      *[pltpu.with_memory_space_constraint(z, pltpu.HBM) for z in lands])
    return (res[0], res[1], list(res[2:2 + n]), list(res[2 + n:2 + 2 * n])), res[-1]


def _exchange_wait(handle, after, scatter, name, masks=_ALL_MASKS):
    send_sems, recv_sems, thru, lands = handle
    n = len(thru)

    def body(*refs):
        ins, zones = refs[:n], refs[n:2 * n]
        send_sems, recv_sems = refs[2 * n], refs[2 * n + 1]
        x, y, c = lax.axis_index("x"), lax.axis_index("y"), lax.axis_index("c")
        me = _dev_index((x, y, c))
        for w in range(n):
            for m in masks:
                peer = _peer(x, y, c, m)
                copy = pltpu.make_async_remote_copy(
                    src_ref=ins[w].at[me] if scatter else ins[w], dst_ref=zones[w].at[_dev_index(peer)],
                    send_sem=send_sems.at[_sem_index(w, m)], recv_sem=recv_sems.at[_sem_index(w, m)],
                    device_id=peer, device_id_type=pl.DeviceIdType.MESH)
                copy.wait_send()
                copy.wait_recv()

    res = pl.pallas_call(
        body, name=name,
        out_shape=(*[pltpu.HBM(a.shape, a.dtype) for a in thru], *[pltpu.HBM(z.shape, z.dtype) for z in lands]),
        in_specs=[_HBM] * (2 * n) + [_SEM, _SEM, pl.BlockSpec(memory_space=pl.ANY)], out_specs=tuple([_HBM] * (2 * n)),
        input_output_aliases={i: i for i in range(2 * n)},
        compiler_params=pltpu.CompilerParams(has_side_effects=_EFFECT),
    )(*thru, *lands, send_sems, recv_sems, after)
    return list(res[:n]), list(res[n:])


def _forward_start(zones, name):
    n = len(zones)

    def body(*refs):
        zs = refs[:n]
        send_sems, recv_sems = refs[n], refs[n + 1]
        token = refs[-1]
        x, y, c = lax.axis_index("x"), lax.axis_index("y"), lax.axis_index("c")
        for w in range(n):
            for m in _CHIP_MASKS:
                slot = zs[w].at[_dev_index(_peer(x, y, c, m))]
                pltpu.make_async_remote_copy(
                    src_ref=slot, dst_ref=slot, send_sem=send_sems.at[_sem_index(w, m)],
                    recv_sem=recv_sems.at[_sem_index(w, m)], device_id=(x, y, 1 - c),
                    device_id_type=pl.DeviceIdType.MESH).start()
        token[...] = jnp.zeros_like(token)

    sems = pltpu.SemaphoreType.DMA((n * (N_DEV - 1),))
    res = pl.pallas_call(
        body, name=name, out_shape=(sems, sems, *[pltpu.HBM(z.shape, z.dtype) for z in zones], SDS((8, 128), F32)),
        in_specs=[_HBM] * n, out_specs=(_SEM, _SEM, *([_HBM] * n), pl.BlockSpec(memory_space=pltpu.VMEM)),
        input_output_aliases={i: 2 + i for i in range(n)},
        compiler_params=pltpu.CompilerParams(has_side_effects=_EFFECT),
    )(*[pltpu.with_memory_space_constraint(z, pltpu.HBM) for z in zones])
    return (res[0], res[1], list(res[2:2 + n])), res[-1]


def _forward_wait(handle, after, name):
    send_sems, recv_sems, zones = handle
    n = len(zones)

    def body(*refs):
        zs = refs[:n]
        send_sems, recv_sems = refs[n], refs[n + 1]
        x, y, c = lax.axis_index("x"), lax.axis_index("y"), lax.axis_index("c")
        for w in range(n):
            for m in _CHIP_MASKS:
                copy = pltpu.make_async_remote_copy(
                    src_ref=zs[w].at[_dev_index(_peer(x, y, c, m))], dst_ref=zs[w].at[_dev_index(_peer(x, y, 1 - c, m))],
                    send_sem=send_sems.at[_sem_index(w, m)], recv_sem=recv_sems.at[_sem_index(w, m)],
                    device_id=(x, y, 1 - c), device_id_type=pl.DeviceIdType.MESH)
                copy.wait_send()
                copy.wait_recv()

    res = pl.pallas_call(
        body, name=name, out_shape=tuple(pltpu.HBM(z.shape, z.dtype) for z in zones),
        in_specs=[_HBM] * n + [_SEM, _SEM, pl.BlockSpec(memory_space=pl.ANY)], out_specs=tuple([_HBM] * n),
        input_output_aliases={i: i for i in range(n)},
        compiler_params=pltpu.CompilerParams(has_side_effects=_EFFECT),
    )(*zones, send_sems, recv_sems, after)
    return list(res)


def _adam_math(g, w, m, v):
    m = ADAM_B1 * m + (1.0 - ADAM_B1) * g
    v = ADAM_B2 * v + (1.0 - ADAM_B2) * (g * g)
    m_hat = m / (1.0 - ADAM_B1 ** ADAM_STEP)
    v_hat = v / (1.0 - ADAM_B2 ** ADAM_STEP)
    delta = -ADAM_LR * (m_hat / (jnp.sqrt(v_hat) + ADAM_EPS) + ADAM_WD * w)
    return delta, m, v


def _adam(parts, w, m, v, name, tr=128):
    r, c = w.shape
    tr = next(t for t in (tr, 64, 32, 16, 8) if r % t == 0)

    def fn(cc, rr, pb, wb, mb, vb):
        g = pb[0].astype(F32)
        for d in range(1, N_DEV):
            g = g + pb[d].astype(F32)
        delta, nm, nv = _adam_math(g, wb, mb, vb)
        return g, delta, nm, nv

    blk = ((tr, c), lambda cc, rr: (rr, 0))
    o = SDS((r, c), F32)
    return _ew(fn, [(parts, (N_DEV, tr, c), lambda cc, rr: (0, rr, 0)), (w, *blk), (m, *blk), (v, *blk)],
               [(o, *blk, None)] * 4, (1, r // tr), name)


_SHARDED = ("w_in", "w_glu", "w_branch_attn", "w_branch_ssm", "w_out", "w_up", "w_down")
_COL_SHARDED = ("w_in", "w_glu", "w_branch_attn", "w_branch_ssm", "w_up")
_GROUPS = {"a": ("w_in",), "b": ("w_glu", "w_branch_attn", "w_branch_ssm", "w_out"), "c": ("w_up", "w_down")}
_SMALL = ("attn_norm_g", "b_in", "attn_sinks", "ssm_a_re", "ssm_a_im", "ssm_log_dt", "ssm_b_re", "ssm_b_im",
          "ssm_c_re", "ssm_c_im", "ssm_d", "b_glu", "ffn_norm_g", "conv_w", "conv_b", "final_norm_g")
_WEIGHTS = ("attn_norm_g", "w_in", "b_in", "attn_sinks", "ssm_a_re", "ssm_a_im", "ssm_log_dt", "ssm_b_re", "ssm_b_im",
            "ssm_c_re", "ssm_c_im", "ssm_d", "w_glu", "b_glu", "w_branch_attn", "w_branch_ssm", "w_out", "ffn_norm_g",
            "w_up", "conv_w", "conv_b", "w_down", "final_norm_g")


def _unstack_cols(g):
    return g.transpose(1, 0, 2).reshape(g.shape[1], g.shape[0] * g.shape[2])


def _stack_cols(a, d=N_DEV):
    k, n = a.shape
    return a.reshape(k, d, n // d).transpose(1, 0, 2)


def _pack(arrs):
    flat = jnp.concatenate([a.reshape(-1) for a in arrs])
    pad = (-flat.shape[0]) % 1024
    return jnp.pad(flat, (0, pad)).reshape(-1, 128)


def _local_step(x, tgt, wget, small, gput):
    L = x.shape[0]
    nr = lambda tm: L // tm

    h = _rmsnorm_fwd(x, small["attn_norm_g"], "norm1")
    wts = dict(wget("a", h))
    projb = _mm(h, wts["w_in"], bias=small["b_in_p"], out_dtype=BF16, name="proj")
    proj = projb
    attn_bias = _attn_bias()
    attn = _attn_fwd(projb, small["attn_sinks"], attn_bias, "attn_fwd")

    ab, bmat, cmat = _ssm_prep(small["a_re"], small["a_im"], small["logdt"], small["b_re"], small["b_im"],
                               small["c_re"], small["c_im"], "ssm_prep")
    u_seg = _to_segments(proj[:, C_U:C_PAD])
    ends_f = _ssm_scan(u_seg, bmat, ab, reverse=False, tk=256, name="ssm_ends_fwd")
    xs, init_f, y_seg = _ssm_scan(u_seg, bmat, ab, reverse=False, ends=ends_f, wproj=cmat, tk=64, name="ssm_scan_fwd")
    y_mm = _from_segments(y_seg)

    def gelu_fn(c, r, yb, ub, db):
        yv = yb + db * ub
        return yv, _gelu(yv)

    tm = 512
    y, gy = _ew(gelu_fn, [(y_mm, *_rc(tm, 256)), (proj, *_rc(tm, 256, C_U // 256)), (small["ssm_d"], *_col(1, 256))],
                [(SDS((L, SW), F32), *_rc(tm, 256), None), (SDS((L, SW), BF16), *_rc(tm, 256), None)],
                (2, nr(tm)), "ssm_gelu")
    wts.update(wget("b", gy))
    def glu_ep(i, cols, rg, rv, bg_ref, bv_ref):
        val, sg = rv + bv_ref[...], _sigmoid(rg + bg_ref[...])
        return val * sg, val, sg

    ssm, glu_val, glu_sig = _mm(gy, wts["w_glu"], b2=wts["w_glu"], n_cols=SW, b_col_off=1, tn=SW, epilogue=glu_ep,
                                extras=[(small["b_glu"], "col", 1), (small["b_glu"], "col", 0)],
                                outs=[(SDS((L, SW), BF16), "tile"), (SDS((L, SW), F32), "tile"), (SDS((L, SW), F32), "tile")],
                                name="glu_gate")
    f32 = lambda ref, cols: ref[:, cols].astype(F32)
    tnm = 1024
    gate_tiles = [(projb, "tile", C_GA // tnm), (projb, "tile", C_GS // tnm)]

    def merge_ep(i, cols, ra, rs, ga, gs):
        sa, ss = _sigmoid(f32(ga, cols)), _sigmoid(f32(gs, cols))
        return sa * ra + ss * rs, ra, rs, sa, ss

    merged, br_a, br_s, sig_a, sig_s = _mm(attn, wts["w_branch_attn"], a2=ssm, b2=wts["w_branch_ssm"], tm=512, tn=tnm,
                                           extras=gate_tiles, epilogue=merge_ep,
                                           outs=[(SDS((L, D), BF16), "tile")] * 5, name="branch_merge")
    def norm2_ep(i, cols, r, _, g_ref):
        rstd = lax.rsqrt(jnp.mean(r * r, axis=-1, keepdims=True) + RMS_EPS)
        return r, (r * rstd) * g_ref[...]

    x1, h2 = _mm(merged, wts["w_out"], res=x, tm=512, tn=D, epilogue=norm2_ep, extras=[(small["ffn_norm_g"], "col", 0)],
                 outs=[(SDS((L, D), F32), "tile"), (SDS((L, D), BF16), "tile")], name="out_proj_norm2")
    wts.update(wget("c", h2))
    conv_w = wts["conv_w"]
    w_up = wts["w_up"]
    tcf = 1408
    tma = 256
    hb = 16

    def conv_gate(first, gate, halo, cw, cb):
        halo = halo * jnp.logical_not(first).astype(F32)
        g1, g2 = _shift_rows(gate, halo, 1), _shift_rows(gate, halo, 2)
        return cb + cw[2:3] * gate + cw[1:2] * g1 + cw[0:1] * g2, g1, g2

    tmu, tnu = 1024, 512

    def up_ep(i, cols, rg, rv, h2_halo, wg, cw, cb):
        halo = jnp.dot(h2_halo[...], wg[:, cols], preferred_element_type=F32)[hb - 8:]
        gl, glg = _gelu_and_grad(conv_gate(i == 0, rg, halo, cw[:, cols], cb[:, cols])[0])
        return rg, rv * gl, gl, rv * glg

    up_g, act, gelu_cg, val_gelu_grad = _mm(
        h2, w_up, b2=w_up, n_cols=DFF, b_col_off=DFF // tnu, tm=tmu, tn=tnu, epilogue=up_ep,
        outs=[(SDS((L, DFF), BF16), "tile")] * 4, name="ffn_up_act",
        extras=[(h2, "spec", ((hb, D), lambda j, i: (jnp.maximum(i * (tmu // hb) - 1, 0), 0))),
                (w_up, "spec", ((D, tnu), lambda j, i: (0, j + DFF // tnu))), (conv_w, "col", 0),
                (small["conv_b"], "col", 0)])
    x2 = _mm(act, wts["w_down"], res=x1, name="ffn_down")
    d_x2, d_x2b, loss_cols, d_gf = _final_loss(x2, small["final_norm_g"], tgt, "final_loss")
    loss = jnp.sum(loss_cols)

    dw_down = _mm(act, d_x2b, ta=True, out_dtype=BF16, tm=tcf, tk=2048, name="dw_down")
    tmd, tnd = 1024, 512

    def dact_ep(i, cols, da, _, gate_ref, halo_ref, gl_ref, vg_ref):
        gate, gl = f32(gate_ref, cols), f32(gl_ref, cols)
        halo = f32(halo_ref, cols)[hb - 8:] * (i > 0).astype(F32)
        g1, g2 = _shift_rows(gate, halo, 1), _shift_rows(gate, halo, 2)
        d_cg = da * f32(vg_ref, cols)
        row3 = lax.broadcasted_iota(jnp.int32, (3, da.shape[1]), 0)
        s0 = jnp.sum(d_cg * g2, axis=0, keepdims=True)
        s1 = jnp.sum(d_cg * g1, axis=0, keepdims=True)
        s2 = jnp.sum(d_cg * gate, axis=0, keepdims=True)
        dcw = jnp.where(row3 == 0, s0, jnp.where(row3 == 1, s1, s2))
        return da * gl, d_cg, dcw, jnp.sum(d_cg, axis=0, keepdims=True)

    d_up, d_cg, d_conv_w, d_conv_b = _mm(
        d_x2b, wts["w_down"], tb=True, tm=tmd, tn=tnd, epilogue=dact_ep, name="d_act_bwd",
        extras=[(up_g, "tile", 0), (up_g, "spec", ((hb, tnd), lambda j, i: (jnp.maximum(i * (tmd // hb) - 1, 0), j))),
                (gelu_cg, "tile", 0), (val_gelu_grad, "tile", 0)],
        outs=[(SDS((L, 2 * DFF), BF16), "tile"), (SDS((L, DFF), BF16), "tile"), (SDS((3, DFF), F32), "colacc"),
              (SDS((1, DFF), F32), "colacc")])
    ncf = DFF // tcf

    tmg = 512

    def gate_bwd(c, r, dcg, halo, cw):
        halo = halo[:8] * (r < nr(tmg) - 1).astype(F32)
        return (cw[2:3] * dcg + cw[1:2] * _shift_rows_up(dcg, halo, 1) + cw[0:1] * _shift_rows_up(dcg, halo, 2),)

    (d_up,) = _ew(gate_bwd, [(d_cg, *_rc(tmg, tcf)),
                             (d_cg, (hb, tcf), lambda c, r: (jnp.minimum((r + 1) * (tmg // hb), L // hb - 1), c)),
                             (conv_w, *_col(3, tcf))],
                  [(SDS((L, 2 * DFF), BF16), *_rc(tmg, tcf, ncf), None)], (ncf, nr(tmg)), "ffn_gate_bwd", into=d_up)
    d_h2 = _mm(d_up, w_up, tb=True, name="d_h2")
    assert tcf == 2 * DFF // N_DEV
    dw_up = _mm(h2, d_up, ta=True, out_dtype=BF16, tn=tcf, tk=2048, stack_out=True, name="dw_up")
    tok = gput("c", {"w_up": dw_up, "w_down": dw_down})
    d_x1, d_g2 = _rmsnorm_bwd(d_h2, x1, small["ffn_norm_g"] + tok[0, 0], d_x2, "norm2_bwd")

    dw_out = _mm(merged, d_x1, ta=True, out_dtype=BF16, name="dw_out")

    def dmerge_ep(i, cols, dm, _, a_ref, s_ref, sa_ref, ss_ref):
        sa, ss = f32(sa_ref, cols), f32(ss_ref, cols)
        return dm * sa, dm * ss, dm * (f32(a_ref, cols) * (sa * (1.0 - sa))), dm * (f32(s_ref, cols) * (ss * (1.0 - ss)))

    d_bra, d_brs, d_ga, d_gs = _mm(d_x1, wts["w_out"], tb=True, tm=512, tn=tnm, epilogue=dmerge_ep,
                                   extras=[(br_a, "tile", 0), (br_s, "tile", 0), (sig_a, "tile", 0), (sig_s, "tile", 0)],
                                   outs=[(SDS((L, D), BF16), "tile")] * 4, name="d_merged_bwd")
    d_attn = _mm(d_bra, wts["w_branch_attn"], tb=True, out_dtype=BF16, name="d_attn")
    dw_ba = _mm(attn, d_bra, ta=True, out_dtype=BF16, name="dw_branch_attn")
    d_ssm = _mm(d_brs, wts["w_branch_ssm"], tb=True, name="d_ssm")
    dw_bs = _mm(ssm, d_brs, ta=True, out_dtype=BF16, name="dw_branch_ssm")
    dq, dkv_cur, dkv_prev, d_sinks = _attn_bwd(projb, small["attn_sinks"], attn_bias, d_attn, "attn_bwd")

    def glu_bwd(c, r, ds, vb, sg):
        return ds * sg, ds * vb * (sg * (1.0 - sg))

    d_glu_v, d_glu_g = _ew(glu_bwd, [(d_ssm, *_rc(tm, SW)), (glu_val, *_rc(tm, SW)), (glu_sig, *_rc(tm, SW))],
                           [(SDS((L, SW), F32), *_rc(tm, SW), None)] * 2, (1, nr(tm)), "glu_gate_bwd")
    d_glu = jnp.concatenate([d_glu_v, d_glu_g], axis=1)
    d_gy = _mm(d_glu, wts["w_glu"], tb=True, name="d_gelu_y")
    dw_glu = _mm(gy, d_glu, ta=True, out_dtype=BF16, name="dw_glu")

    tok = gput("b", {"w_glu": dw_glu, "w_branch_attn": dw_ba, "w_branch_ssm": dw_bs, "w_out": dw_out})
    ab = ab + tok[0, 0]

    def gelu_bwd(c, r, dg, yb, ub, dgl):
        dy = dg * _gelu_grad(yb)
        return dy, jnp.sum(dy * ub, axis=0, keepdims=True), jnp.sum(dgl, axis=0, keepdims=True)

    dy, d_ssm_d, d_b_glu = _ew(
        gelu_bwd, [(d_gy, *_rc(tm, 256)), (y, *_rc(tm, 256)), (proj, *_rc(tm, 256, C_U // 256)), (d_glu, *_rc(tm, 512))],
        [(SDS((L, SW), F32), *_rc(tm, 256), None), (SDS((1, SW), F32), *_col(1, 256), "r"),
         (SDS((1, 2 * SW), F32), *_col(1, 512), "r")], (2, nr(tm)), "ssm_gelu_bwd")

    dy_seg = _to_segments(dy)
    ends_r = _ssm_scan(dy_seg, cmat, ab, reverse=True, tk=256, name="ssm_ends_bwd")
    lam, dab8, du_seg = _ssm_scan(dy_seg, cmat, ab, reverse=True, ends=ends_r, xs=xs, init=init_f, wproj=bmat,
                                  tk=64, name="ssm_scan_bwd")
    du_mm = _from_segments(du_seg)
    dbm = _mm(u_seg, lam, ta=True, tm=512, name="ssm_dbmat")
    dcm = _mm(dy_seg, xs, ta=True, tm=512, name="ssm_dcmat")
    d_are, d_aim, d_ldt, d_bre, d_bim, d_cre, d_cim = _ssm_param_bwd(
        small["a_re"], small["a_im"], small["logdt"], small["b_re"], small["b_im"], dab8, dbm, dcm, "ssm_param_bwd")

    nb = L // BLK

    def dproj_fn(c, r, dqb, cur, prv, du, dyb, dsk, dga, dgs):
        dkv = cur + prv * (r < nb - 1).astype(F32)
        dub = du + dsk * dyb
        full = jnp.concatenate([dqb, dkv, dub, jnp.zeros((BLK, C_GA - C_PAD), F32), dga, dgs], axis=1)
        return full, jnp.sum(full, axis=0, keepdims=True)

    rowb = lambda w: ((BLK, w), lambda c, r: (r, 0))
    dproj, d_b_in = _ew(
        dproj_fn, [(dq, *rowb(AW)), (dkv_cur, *rowb(256)),
                   (dkv_prev, (BLK, 256), lambda c, r: (jnp.minimum(r + 1, nb - 1), 0)),
                   (du_mm, *rowb(SW)), (dy, *rowb(SW)), (small["ssm_d"], *_col(1, SW)), (d_ga, *rowb(D)), (d_gs, *rowb(D))],
        [(SDS((L, INP), BF16), *rowb(INP), None), (SDS((1, INP), F32), *_col(1, INP), "all")], (1, nb), "dproj")
    tok_small = gput("small", {
        "b_in": _unpad_cols(d_b_in), "attn_sinks": d_sinks[:, :NQ], "a_re": d_are, "a_im": d_aim, "logdt": d_ldt,
        "b_re": d_bre, "b_im": d_bim, "c_re": d_cre, "c_im": d_cim, "ssm_d": d_ssm_d, "b_glu": d_b_glu,
        "ffn_norm_g": d_g2, "conv_w": d_conv_w, "conv_b": d_conv_b, "final_norm_g": d_gf})
    dw_in = _mm(h, dproj, ta=True, out_dtype=BF16, name="dw_in")
    tok = gput("a", {"w_in": _unpad_cols(dw_in)}) + tok_small
    d_h = _mm(dproj, wts["w_in"], tb=True, bias=jnp.zeros((1, D), F32) + tok[0, 0], name="d_h")
    grad_x, d_g1 = _rmsnorm_bwd(d_h, x, small["attn_norm_g"], d_x1, "norm1_bwd")
    return loss, grad_x, {"attn_norm_g": d_g1}


def _small_layouts(p):
    gp = lambda a: a.reshape(1, NS)
    hgp = lambda a: a.transpose(2, 0, 1).reshape(H, NS)
    chgp = lambda a: a.transpose(1, 0, 2).reshape(H, NS)
    return {
        "attn_norm_g": p["attn_norm_g"].reshape(1, D), "ffn_norm_g": p["ffn_norm_g"].reshape(1, D),
        "final_norm_g": p["final_norm_g"].reshape(1, D),
        "b_in_p": _pad_cols(p["b_in"].reshape(1, INC)),
        "attn_sinks": p["attn_sinks"].reshape(1, NQ),
        "a_re": gp(p["ssm_a_re"]), "a_im": gp(p["ssm_a_im"]), "logdt": jnp.repeat(p["ssm_log_dt"], P).reshape(1, NS),
        "b_re": hgp(p["ssm_b_re"]), "b_im": hgp(p["ssm_b_im"]), "c_re": chgp(p["ssm_c_re"]), "c_im": chgp(p["ssm_c_im"]),
        "ssm_d": p["ssm_d"].reshape(1, SW), "b_glu": p["b_glu"].reshape(1, 2 * SW),
        "conv_b": p["conv_b"].reshape(1, DFF),
    }


def _small_grads_to_param_shapes(sg):
    from_hgp = lambda a: a.reshape(H, G, P).transpose(1, 2, 0)
    from_chgp = lambda a: a.reshape(H, G, P).transpose(1, 0, 2)
    flat = lambda a: a.reshape(-1)
    to_param = {
        "attn_norm_g": ("attn_norm_g", flat), "b_in": ("b_in", flat), "attn_sinks": ("attn_sinks", flat),
        "a_re": ("ssm_a_re", lambda a: a.reshape(G, P)), "a_im": ("ssm_a_im", lambda a: a.reshape(G, P)),
        "logdt": ("ssm_log_dt", lambda a: jnp.sum(a.reshape(G, P), axis=1)),
        "b_re": ("ssm_b_re", from_hgp), "b_im": ("ssm_b_im", from_hgp),
        "c_re": ("ssm_c_re", from_chgp), "c_im": ("ssm_c_im", from_chgp),
        "ssm_d": ("ssm_d", flat), "b_glu": ("b_glu", flat), "ffn_norm_g": ("ffn_norm_g", flat),
        "conv_w": ("conv_w", lambda a: a), "conv_b": ("conv_b", flat), "final_norm_g": ("final_norm_g", flat),
    }
    return {to_param[k][0]: to_param[k][1](a) for k, a in sg.items()}


def kernel(x, attn_norm_g, w_in, b_in, attn_sinks, ssm_a_re, ssm_a_im, ssm_log_dt, ssm_b_re, ssm_b_im, ssm_c_re, ssm_c_im, ssm_d, w_glu, b_glu, w_branch_attn, w_branch_ssm, w_out, ffn_norm_g, w_up, conv_w, conv_b, w_down, final_norm_g, loss_target, m_attn_norm_g, m_w_in, m_b_in, m_attn_sinks, m_ssm_a_re, m_ssm_a_im, m_ssm_log_dt, m_ssm_b_re, m_ssm_b_im, m_ssm_c_re, m_ssm_c_im, m_ssm_d, m_w_glu, m_b_glu, m_w_branch_attn, m_w_branch_ssm, m_w_out, m_ffn_norm_g, m_w_up, m_conv_w, m_conv_b, m_w_down, m_final_norm_g, v_attn_norm_g, v_w_in, v_b_in, v_attn_sinks, v_ssm_a_re, v_ssm_a_im, v_ssm_log_dt, v_ssm_b_re, v_ssm_b_im, v_ssm_c_re, v_ssm_c_im, v_ssm_d, v_w_glu, v_b_glu, v_w_branch_attn, v_w_branch_ssm, v_w_out, v_ffn_norm_g, v_w_up, v_conv_w, v_conv_b, v_w_down, v_final_norm_g):
    args = dict(locals())
    sq = lambda a: a if a.ndim == 1 else a[0]
    wv = {n: sq(args[n]) for n in _WEIGHTS}
    mv = {n: sq(args["m_" + n]) for n in _WEIGHTS}
    vv = {n: sq(args["v_" + n]) for n in _WEIGHTS}
    me = 4 * lax.axis_index("x") + 2 * lax.axis_index("y") + lax.axis_index("c")

    gather, tok = {}, jnp.zeros((8, 128), F32)
    for grp in ("a", "b", "c"):
        shards = [(wv[n] + tok[0, 0]).astype(BF16) for n in _GROUPS[grp]]
        if grp == "c":
            shards.append(jnp.pad(wv["conv_w"] + tok[0, 0], ((0, 5), (0, 64))))
        gather[grp], tok = _exchange_start(shards, False, "gather_start_" + grp,
                                           masks=_FIRST_HOP_MASKS if grp == "a" else _ALL_MASKS)
    small = _small_layouts(wv)
    small["attn_norm_g"] = small["attn_norm_g"] + tok[0, 0]

    def own_slot(land, src):
        return lax.dynamic_update_slice_in_dim(land, src, me, axis=0)

    def wget(grp, after):
        if grp == "a":
            thru, lands = _exchange_wait(gather[grp], after, False, "gather_wait_a", masks=_FIRST_HOP_MASKS)
            fwd, fwd_tok = _forward_start(lands, "gather_forward_start_a")
            lands = _forward_wait(fwd, fwd_tok, "gather_forward_wait_a")
        else:
            thru, lands = _exchange_wait(gather[grp], after, False, "gather_wait_" + grp)
        full = {}
        for n, t, g in zip(_GROUPS[grp], thru, lands):
            g = own_slot(g, t[None])
            full[n] = _unstack_cols(g) if n in _COL_SHARDED else g.reshape(N_DEV * g.shape[1], g.shape[2])
        if grp == "a":
            full["w_in"] = _pad_cols(full["w_in"])
        if grp == "c":
            full["conv_w"] = _unstack_cols(own_slot(lands[-1], thru[-1][None])[:, :3, :DFF // N_DEV])
        return full

    scatter = {}

    early_names = [n for n in _SMALL if n != "attn_norm_g"]
    sgp = {}

    def gput(grp, grads):
        if grp == "small":
            sgp.update(_small_grads_to_param_shapes(grads))
            scatter[grp], token = _exchange_start([_pack([sgp[n] for n in early_names])], False, "gather_small_start")
            return token
        stacked = [grads[n] if n == "w_up" else
                   _stack_cols(grads[n]) if n in _COL_SHARDED else grads[n].reshape(N_DEV, -1, D) for n in _GROUPS[grp]]
        scatter[grp], token = _exchange_start(stacked, True, "scatter_start_" + grp)
        return token

    loss, grad_x, sg = _local_step(x[0], loss_target[0], wget, small, gput)
    loss = lax.psum(loss, MESH_AXES)

    sgp.update(_small_grads_to_param_shapes(sg))
    small_names = [n for n in _SMALL]
    (norm_all,) = _exchange([jnp.pad(sgp["attn_norm_g"].reshape(1, D), ((0, 7), (0, 0)))], False, "gather_norm_grad")
    thru, (small_all,) = _exchange_wait(scatter["small"], norm_all, False, "gather_small_wait")
    small_all = own_slot(small_all, thru[0][None])

    outs_g, outs_d, outs_m, outs_v = {}, {}, {}, {}
    for grp in ("c", "b", "a"):
        thru, lands = _exchange_wait(scatter[grp], norm_all, True, "scatter_wait_" + grp)
        for n, t, pt in zip(_GROUPS[grp], thru, lands):
            pt = own_slot(pt, lax.dynamic_slice_in_dim(t, me, 1, axis=0))
            outs_g[n], outs_d[n], outs_m[n], outs_v[n] = _adam(pt, wv[n], mv[n], vv[n], "adam_" + n)

    sizes = [int(math.prod(sgp[n].shape)) for n in early_names]
    offs = [0]
    for s in sizes:
        offs.append(offs[-1] + s)

    def local_part(n, a):
        if n == "conv_w":
            return lax.dynamic_slice(a, (0, me * (DFF // N_DEV)), (3, DFF // N_DEV))
        return a

    rows = small_all.shape[1]

    def sum_fn(cc, rr, pb, nb_):
        g, gn = pb[0], nb_[0]
        for d in range(1, N_DEV):
            g, gn = g + pb[d], gn + nb_[d]
        return g, gn

    gsum, gnorm = _ew(sum_fn, [(small_all, (N_DEV, rows, 128), lambda cc, rr: (0, 0, 0)),
                               (norm_all, (N_DEV, 8, D), lambda cc, rr: (0, 0, 0))],
                      [(SDS((rows, 128), F32), (rows, 128), lambda cc, rr: (0, 0), None),
                       (SDS((8, D), F32), (8, D), lambda cc, rr: (0, 0), None)], (1, 1), "sum_small_grads")
    gflat = gsum.reshape(-1)
    gsmall = {n: local_part(n, gflat[offs[i]:offs[i + 1]].reshape(sgp[n].shape)) for i, n in enumerate(early_names)}
    gsmall["attn_norm_g"] = gnorm[0]
    as2d = lambda a: a.reshape(1, -1) if a.ndim == 1 else a.reshape(a.shape[0], -1)
    n_small = len(small_names)

    def adam_small(*refs):
        for i in range(n_small):
            g_ref, w_ref, m_ref, v_ref = refs[4 * i:4 * i + 4]
            outs = refs[4 * n_small + 3 * i:4 * n_small + 3 * i + 3]
            for o_ref, val in zip(outs, _adam_math(g_ref[...], w_ref[...], m_ref[...], v_ref[...])):
                o_ref[...] = val

    small_ins = [as2d(t[n]) for n in small_names for t in (gsmall, wv, mv, vv)]
    small_outs = pl.pallas_call(adam_small, name="adam_small",
                                out_shape=[SDS(as2d(wv[n]).shape, F32) for n in small_names for _ in range(3)])(*small_ins)
    for i, n in enumerate(small_names):
        sd, sm, sv = (t.reshape(wv[n].shape) for t in small_outs[3 * i:3 * i + 3])
        outs_g[n], outs_d[n], outs_m[n], outs_v[n] = gsmall[n], sd, sm, sv

    lead = lambda n, a: a if args[n].ndim == 1 else a[None]
    grad_x = grad_x[None]
    return (loss, grad_x, *[lead(n, outs_g[n]) for n in _WEIGHTS], *[lead(n, outs_d[n]) for n in _WEIGHTS],
            *[lead(n, outs_m[n]) for n in _WEIGHTS], *[lead(n, outs_v[n]) for n in _WEIGHTS])
```

```python
import functools
import math

import jax
import jax.numpy as jnp
from jax import lax
from jax.experimental import pallas as pl
from jax.experimental.pallas import tpu as pltpu

F32 = jnp.float32
BF16 = jnp.bfloat16
SDS = jax.ShapeDtypeStruct

N_DEV = 8
D = 2048
NQ, NKV, HD = 16, 2, 64
AW = NQ * HD
BLK = 128
SW, G, H, P = 512, 32, 16, 64
NS = G * P
DFF = 5632
INC = AW + 2 * NKV * HD + SW + 2 * D
C_K, C_U, C_PAD = AW, AW + 2 * NKV * HD, AW + 2 * NKV * HD + SW
C_GA, C_GS, INP = D, 2 * D, 3 * D
RMS_EPS = 1e-6
NEG_BIG = -1e30
ADAM_LR, ADAM_B1, ADAM_B2, ADAM_EPS, ADAM_WD, ADAM_STEP = 0.001, 0.9, 0.999, 1e-08, 0.01, 10
NSEG = 8
VMEM_CAP_MB = 60
MESH_AXES = ("x", "y", "c")


def _pad_cols(a):
    zeros = jnp.zeros(a.shape[:-1] + (C_GA - C_PAD,), a.dtype)
    return jnp.concatenate([a[..., :C_PAD], zeros, a[..., C_PAD:]], axis=-1)


def _unpad_cols(a):
    return jnp.concatenate([a[..., :C_PAD], a[..., C_GA:]], axis=-1)


def _cparams(sem, vmem_mb):
    return pltpu.CompilerParams(dimension_semantics=sem, vmem_limit_bytes=min(int(vmem_mb), VMEM_CAP_MB) << 20)


LANES = 128


def _tile(dim, pref):
    if dim <= pref:
        return dim
    for t in range(pref - pref % LANES, 0, -LANES):
        if dim % t == 0:
            return t
    raise ValueError(f"no tile for {dim}")


def _mm(a, b, *, ta=False, tb=False, bias=None, res=None, out_dtype=F32, tm=1024, tn=1024, tk=3072, name,
        a2=None, b2=None, extras=(), epilogue=None, outs=None, stack_out=False, n_cols=None,
        b_col_off=0, b2_col_off=0):
    m, k = (a.shape[1], a.shape[0]) if ta else a.shape
    n = n_cols or (b.shape[0] if tb else b.shape[1])
    assert (b.shape[1] if tb else b.shape[0]) == k, (a.shape, b.shape, ta, tb)
    tm, tn, tk = _tile(m, tm), _tile(n, tn), _tile(k, tk)
    nk = k // tk
    dims = (((0 if ta else 1,), (1 if tb else 0,)), ((), ()))
    has_bias, has_res, has_b2 = bias is not None, res is not None, b2 is not None
    has_a2 = a2 is not None
    assert not (has_b2 and (nk > 1 or ta or tb)) and not (has_a2 and not has_b2)
    if epilogue is None:
        outs = [(SDS((n // tn, m, tn) if stack_out else (m, n), out_dtype), "tile")]
    n_ex, n_out = len(extras), len(outs)

    def body(*refs):
        a_ref, b_ref = refs[0], refs[1]
        pos = 2
        a2_ref = refs[pos] if has_a2 else a_ref
        pos += has_a2
        b2_ref = refs[pos] if has_b2 else None
        pos += has_b2
        bias_ref = refs[pos] if has_bias else None
        pos += has_bias
        res_ref = refs[pos] if has_res else None
        pos += has_res
        ex_refs = refs[pos:pos + n_ex]
        o_refs = refs[pos + n_ex:pos + n_ex + n_out]
        i = pl.program_id(1)

        def product(rhs_ref, cols=None, lhs=None):
            rhs = rhs_ref[...] if cols is None else (rhs_ref[cols, :] if tb else rhs_ref[:, cols])
            lhs = a_ref[...].astype(BF16) if lhs is None else lhs
            return lax.dot_general(lhs, rhs.astype(BF16), dims, preferred_element_type=F32)

        def finish(r, cols):
            if has_bias:
                r = r + bias_ref[:, cols]
            if has_res:
                r = r + res_ref[:, cols].astype(F32)
            if epilogue is None:
                o_refs[0][:, cols] = r.astype(o_refs[0].dtype)
                return
            r2 = None
            if has_b2:
                r2 = jnp.dot(a2_ref[...].astype(BF16), b2_ref[:, cols].astype(BF16), preferred_element_type=F32)
            vals = epilogue(i, cols, r, r2, *ex_refs)
            for o_ref, v, (_, kind) in zip(o_refs, vals, outs):
                if kind == "tile":
                    o_ref[:, cols] = v.astype(o_ref.dtype)
                else:
                    @pl.when(i == 0)
                    def _(o_ref=o_ref, v=v):
                        o_ref[:, cols] = v.astype(o_ref.dtype)

                    @pl.when(i > 0)
                    def _(o_ref=o_ref, v=v):
                        o_ref[:, cols] += v.astype(o_ref.dtype)

        whole = pl.ds(0, tn)
        if nk == 1:
            finish(product(b_ref), whole)
            return
        acc_ref = refs[-1]
        kk = pl.program_id(2)

        @pl.when(kk == 0)
        def _():
            acc_ref[...] = product(b_ref)

        @pl.when(jnp.logical_and(kk > 0, kk < nk - 1))
        def _():
            acc_ref[...] += product(b_ref)

        @pl.when(kk == nk - 1)
        def _():
            finish(acc_ref[...] + product(b_ref), whole)

    b_spec = (pl.BlockSpec((tn, tk), lambda j, i, kk: (j + b_col_off, kk)) if tb else
              pl.BlockSpec((tk, tn), lambda j, i, kk: (kk, j + b_col_off)))
    ins = [a, b]
    in_specs = [pl.BlockSpec((tk, tm), lambda j, i, kk: (kk, i)) if ta else pl.BlockSpec((tm, tk), lambda j, i, kk: (i, kk)),
                b_spec]
    tile_spec = pl.BlockSpec((tm, tn), lambda j, i, kk: (i, j))
    byt = 2 * tm * tk * a.dtype.itemsize + 2 * tk * tn * b.dtype.itemsize
    byt += (2 + has_b2) * 4 * tm * tn
    if has_a2:
        ins.append(a2)
        in_specs.append(pl.BlockSpec((tm, a2.shape[1]), lambda j, i, kk: (i, 0)))
        byt += 2 * tm * a2.shape[1] * a2.dtype.itemsize
    if has_b2:
        ins.append(b2)
        in_specs.append(pl.BlockSpec((b2.shape[0], tn), lambda j, i, kk: (0, j + b2_col_off)))
        byt += 2 * b2.shape[0] * tn * b2.dtype.itemsize
    if has_bias:
        ins.append(bias)
        in_specs.append(pl.BlockSpec((1, tn), lambda j, i, kk: (0, j)))
    if has_res:
        ins.append(res)
        in_specs.append(tile_spec)
        byt += 2 * tm * tn * res.dtype.itemsize
    for arr, kind, arg in extras:
        ins.append(arr)
        if kind == "tile":
            in_specs.append(pl.BlockSpec((tm, tn), lambda j, i, kk, arg=arg: (i, j + arg)))
            byt += 2 * tm * tn * arr.dtype.itemsize + 4 * tm * tn
        elif kind == "col":
            in_specs.append(pl.BlockSpec((arr.shape[0], tn), lambda j, i, kk, arg=arg: (0, j + arg)))
        else:
            in_specs.append(pl.BlockSpec(arg[0], lambda j, i, kk, im=arg[1]: im(j, i)))
    out_specs = []
    for sds, kind in outs:
        if kind == "tile":
            out_specs.append(pl.BlockSpec((None, tm, tn), lambda j, i, kk: (j, i, 0)) if stack_out else tile_spec)
            byt += 2 * tm * tn * jnp.dtype(sds.dtype).itemsize
        else:
            out_specs.append(pl.BlockSpec((sds.shape[0], tn), lambda j, i, kk: (0, j)))
    res_ = pl.pallas_call(
        body, out_shape=tuple(o[0] for o in outs), grid=(n // tn, m // tm, nk), in_specs=in_specs,
        out_specs=tuple(out_specs), scratch_shapes=[pltpu.VMEM((tm, tn), F32)] if nk > 1 else [], name=name,
        compiler_params=_cparams(("arbitrary", "arbitrary", "arbitrary"), byt / 2**20 + (8 if epilogue is None else 20)),
    )(*ins)
    return res_[0] if epilogue is None else res_


def _ew(fn, ins, outs, grid, name, vmem_mb=40, into=None):
    n_in = len(ins)
    accs = [o[3] for o in outs]

    def body(*refs):
        c, r = pl.program_id(0), pl.program_id(1)
        vals = fn(c, r, *[ref[...].astype(F32) for ref in refs[:n_in]])
        for o_ref, v, acc in zip(refs[n_in + (into is not None):], vals, accs):
            if acc is None:
                o_ref[...] = v.astype(o_ref.dtype)
            else:
                first = (r == 0) if acc == "r" else jnp.logical_and(r == 0, c == 0)

                @pl.when(first)
                def _(o_ref=o_ref, v=v):
                    o_ref[...] = v.astype(o_ref.dtype)

                @pl.when(jnp.logical_not(first))
                def _(o_ref=o_ref, v=v):
                    o_ref[...] += v.astype(o_ref.dtype)

    in_specs = [pl.BlockSpec(bs, im) for _, bs, im in ins]
    args = [a for a, _, _ in ins]
    if into is not None:
        in_specs.append(pl.BlockSpec(memory_space=pl.ANY))
        args.append(into)
    res = pl.pallas_call(
        body, out_shape=tuple(o[0] for o in outs), grid=grid, in_specs=in_specs,
        out_specs=tuple(pl.BlockSpec(bs, im) for _, bs, im, _ in outs), name=name,
        input_output_aliases={} if into is None else {n_in: 0},
        compiler_params=_cparams(("arbitrary", "arbitrary"), vmem_mb),
    )(*args)
    return res


def _rc(tm, tc, coff=0):
    return (tm, tc), (lambda c, r: (r, c + coff))


def _col(rows, tc, coff=0):
    return (rows, tc), (lambda c, r: (0, c + coff))


def _gelu(x):
    return 0.5 * x * (1.0 + lax.erf(x * (2.0 ** -0.5)))


def _gelu_and_grad(x):
    cdf = 0.5 * (1.0 + lax.erf(x * (2.0 ** -0.5)))
    return x * cdf, cdf + x * jnp.exp(-0.5 * x * x) * (1.0 / math.sqrt(2.0 * math.pi))


def _gelu_grad(x):
    return _gelu_and_grad(x)[1]


def _sigmoid(x):
    return 1.0 / (1.0 + jnp.exp(-x))


def _shift_rows(x, halo, s):
    rolled = pltpu.roll(x, s, 0)
    row8 = lax.broadcasted_iota(jnp.int32, halo.shape, 0)
    head = jnp.where(row8 < s, pltpu.roll(halo, s, 0), rolled[0:8])
    return jnp.concatenate([head, rolled[8:]], axis=0)


def _shift_rows_up(x, halo, s):
    tm = x.shape[0]
    rolled = pltpu.roll(x, tm - s, 0)
    row8 = lax.broadcasted_iota(jnp.int32, halo.shape, 0)
    tail = jnp.where(row8 >= 8 - s, pltpu.roll(halo, 8 - s, 0), rolled[tm - 8:])
    return jnp.concatenate([rolled[:tm - 8], tail], axis=0)


def _rmsnorm_fwd(x, g, name, tm=512):
    L = x.shape[0]

    def fn(c, r, xb, gb):
        rstd = lax.rsqrt(jnp.mean(xb * xb, axis=-1, keepdims=True) + RMS_EPS)
        return ((xb * rstd) * gb,)

    return _ew(fn, [(x, *_rc(tm, D)), (g, *_col(1, D))], [(SDS((L, D), BF16), *_rc(tm, D), None)], (1, L // tm), name)[0]


def _rmsnorm_bwd(dh, x, g, dres, name, tm=512):
    L = x.shape[0]

    def fn(c, r, dhb, xb, gb, drb):
        rstd = lax.rsqrt(jnp.mean(xb * xb, axis=-1, keepdims=True) + RMS_EPS)
        y = xb * rstd
        dy = dhb * gb
        dx = rstd * (dy - y * jnp.mean(dy * y, axis=-1, keepdims=True))
        return drb + dx, jnp.sum(dhb * y, axis=0, keepdims=True)

    return _ew(fn, [(dh, *_rc(tm, D)), (x, *_rc(tm, D)), (g, *_col(1, D)), (dres, *_rc(tm, D))],
               [(SDS((L, D), F32), *_rc(tm, D), None), (SDS((1, D), F32), *_col(1, D), "all")], (1, L // tm), name,
               vmem_mb=56)


def _final_loss(x2, g, tgt, name, tm=512):
    L = x2.shape[0]

    def fn(c, r, xb, gb, tb):
        rstd = lax.rsqrt(jnp.mean(xb * xb, axis=-1, keepdims=True) + RMS_EPS)
        y = xb * rstd
        err = y * gb - tb
        dout = err * (1.0 / D)
        dy = dout * gb
        dx = rstd * (dy - y * jnp.mean(dy * y, axis=-1, keepdims=True))
        return dx, dx, jnp.sum(err * err, axis=0, keepdims=True) * (0.5 / D), jnp.sum(dout * y, axis=0, keepdims=True)

    return _ew(fn, [(x2, *_rc(tm, D)), (g, *_col(1, D)), (tgt, *_rc(tm, D))],
               [(SDS((L, D), F32), *_rc(tm, D), None), (SDS((L, D), BF16), *_rc(tm, D), None),
                (SDS((1, D), F32), *_col(1, D), "all"),
                (SDS((1, D), F32), *_col(1, D), "all")], (1, L // tm), name, vmem_mb=56)


def _slope(h):
    return 2.0 ** (-8.0 * (h + 1) / NQ)


def _attn_bias():
    qi = lax.broadcasted_iota(jnp.int32, (BLK, 2 * BLK), 0)
    si = lax.broadcasted_iota(jnp.int32, (BLK, 2 * BLK), 1)
    dist = qi + BLK - si
    band = (dist >= 0) & (dist < BLK)
    slopes = jnp.asarray([_slope(h) for h in range(NQ)], F32)[:, None, None]
    alibi = -slopes * dist.astype(F32)[None]
    return jnp.stack([jnp.where((band & (si >= BLK))[None], alibi, NEG_BIG), jnp.where(band[None], alibi, NEG_BIG)])


def _attn_kv(kvc, kvp):
    kv = jnp.concatenate([kvp, kvc], axis=0).astype(F32)
    lo = lax.broadcasted_iota(jnp.int32, (2 * BLK, 128), 1) < HD

    def halves(t):
        tr = pltpu.roll(t, HD, 1)
        z = jnp.zeros_like(t)
        return {(0, 0): jnp.where(lo, t, z).astype(BF16), (0, 1): jnp.where(lo, z, tr).astype(BF16),
                (1, 0): jnp.where(lo, tr, z).astype(BF16), (1, 1): jnp.where(lo, z, t).astype(BF16)}

    return halves(kv[:, :128]), halves(kv[:, 128:])


_NT = (((1,), (1,)), ((), ()))
_TN = (((0,), (0,)), ((), ()))
_ATTN_SPECS = [pl.BlockSpec(memory_space=pltpu.SMEM),
               pl.BlockSpec((None, NQ, BLK, 2 * BLK), lambda n: (jnp.minimum(n, 1), 0, 0, 0)),
               pl.BlockSpec((BLK, AW), lambda n: (n, 0)),
               pl.BlockSpec((BLK, 256), lambda n: (n, C_K // 256)),
               pl.BlockSpec((BLK, 256), lambda n: (jnp.maximum(n - 1, 0), C_K // 256))]


def _attn_scores(q_ref, bias_ref, kmat, sc_ref):
    for j in range(NQ // 2):
        qs = q_ref[:, 128 * j:128 * (j + 1)] * (HD ** -0.5)
        for e in range(2):
            h = 2 * j + e
            sc_ref[h] = lax.dot_general(qs, kmat[(j // (NQ // 4), e)], _NT, preferred_element_type=F32) + bias_ref[h]


def _softmax_with_sink(s, sink):
    m = jnp.maximum(jnp.max(s, axis=-1, keepdims=True), sink)
    p = jnp.exp(s - m)
    esink = jnp.exp(sink - m)
    den = jnp.sum(p, axis=-1, keepdims=True) + esink
    return p / den, esink / den


def _attn_fwd(projb, sinks, bias, name):
    L = projb.shape[0]

    def body(s_ref, bias_ref, q_ref, kvc_ref, kvp_ref, o_ref, sc_ref, pr_ref):
        kmat, vmat = _attn_kv(kvc_ref[...], kvp_ref[...])
        _attn_scores(q_ref, bias_ref, kmat, sc_ref)
        for h in range(NQ):
            pr_ref[h] = _softmax_with_sink(sc_ref[h], s_ref[0, h])[0].astype(BF16)
        for j in range(NQ // 2):
            g = j // (NQ // 4)
            acc = jnp.dot(pr_ref[2 * j], vmat[(g, 0)], preferred_element_type=F32)
            acc = acc + jnp.dot(pr_ref[2 * j + 1], vmat[(g, 1)], preferred_element_type=F32)
            o_ref[:, 128 * j:128 * (j + 1)] = acc.astype(BF16)

    return pl.pallas_call(
        body, out_shape=SDS((L, AW), BF16), grid=(L // BLK,), in_specs=_ATTN_SPECS,
        out_specs=pl.BlockSpec((BLK, AW), lambda n: (n, 0)), name=name,
        scratch_shapes=[pltpu.VMEM((NQ, BLK, 2 * BLK), F32), pltpu.VMEM((NQ, BLK, 2 * BLK), BF16)],
        compiler_params=_cparams(("arbitrary",), 32),
    )(sinks, bias, projb, projb, projb)


def _attn_bwd(projb, sinks, bias, dattn, name):
    L = projb.shape[0]

    def body(s_ref, bias_ref, q_ref, kvc_ref, kvp_ref, do_ref, dq_ref, dcur_ref, dprev_ref, dsink_ref,
             sc_ref, dp_ref, ds_ref, pr_ref, qm_ref, dm_ref):
        n = pl.program_id(0)
        kmat, vmat = _attn_kv(kvc_ref[...], kvp_ref[...])
        _attn_scores(q_ref, bias_ref, kmat, sc_ref)
        for h in range(NQ):
            j, e = h // 2, h % 2
            dp_ref[h] = lax.dot_general(do_ref[:, 128 * j:128 * (j + 1)], vmat[(j // (NQ // 4), e)], _NT,
                                        preferred_element_type=F32)
        lane = lax.broadcasted_iota(jnp.int32, (1, 128), 1)
        dsv = jnp.zeros((1, 128), F32)
        for h in range(NQ):
            p, psink = _softmax_with_sink(sc_ref[h], s_ref[0, h])
            dp = dp_ref[h]
            drow = jnp.sum(p * dp, axis=-1, keepdims=True)
            ds_ref[h] = (p * (dp - drow)).astype(BF16)
            pr_ref[h] = p.astype(BF16)
            dsv = dsv + jnp.where(lane == h, -jnp.sum(psink * drow, axis=0, keepdims=True), 0.0)
        lo128 = lax.broadcasted_iota(jnp.int32, (BLK, 128), 1) < HD
        for j in range(NQ // 2):
            g = j // (NQ // 4)
            qs = q_ref[:, 128 * j:128 * (j + 1)] * (HD ** -0.5)
            dop = do_ref[:, 128 * j:128 * (j + 1)]
            zb = jnp.zeros_like(qs)
            dqp = jnp.zeros((BLK, 128), F32)
            for e in range(2):
                h = 2 * j + e
                half = lo128 if e == 0 else jnp.logical_not(lo128)
                dqp = dqp + jnp.dot(ds_ref[h], kmat[(g, e)], preferred_element_type=F32)
                qm_ref[h] = jnp.where(half, qs, zb)
                dm_ref[h] = jnp.where(half, dop, zb)
            dq_ref[:, 128 * j:128 * (j + 1)] = (dqp * (HD ** -0.5)).astype(BF16)
        hk = NQ // NKV
        rows = lambda ref, g: ref[g * hk:(g + 1) * hk].reshape(hk * BLK, ref.shape[-1])
        dk = [lax.dot_general(rows(ds_ref, g), rows(qm_ref, g), _TN, preferred_element_type=F32) for g in range(NKV)]
        dv = [lax.dot_general(rows(pr_ref, g), rows(dm_ref, g), _TN, preferred_element_type=F32) for g in range(NKV)]
        lo256 = lax.broadcasted_iota(jnp.int32, (2 * BLK, 128), 1) < HD
        tot = [t + pltpu.roll(t, HD, 1) for t in (dk[0], dk[1], dv[0], dv[1])]
        dkv = jnp.concatenate([jnp.where(lo256, tot[0], tot[1]), jnp.where(lo256, tot[2], tot[3])], axis=1)
        dprev_ref[...] = dkv[:BLK]
        dcur_ref[...] = dkv[BLK:]

        @pl.when(n == 0)
        def _():
            dsink_ref[...] = dsv

        @pl.when(n > 0)
        def _():
            dsink_ref[...] += dsv

    tile = (NQ, BLK, 2 * BLK)
    return pl.pallas_call(
        body, out_shape=(SDS((L, AW), BF16), SDS((L, 256), F32), SDS((L, 256), F32), SDS((1, 128), F32)), grid=(L // BLK,),
        in_specs=_ATTN_SPECS + [pl.BlockSpec((BLK, AW), lambda n: (n, 0))],
        out_specs=(pl.BlockSpec((BLK, AW), lambda n: (n, 0)), pl.BlockSpec((BLK, 256), lambda n: (n, 0)),
                   pl.BlockSpec((BLK, 256), lambda n: (n, 0)), pl.BlockSpec((1, 128), lambda n: (0, 0))),
        scratch_shapes=[pltpu.VMEM(tile, F32), pltpu.VMEM(tile, F32), pltpu.VMEM(tile, BF16), pltpu.VMEM(tile, BF16),
                        pltpu.VMEM((NQ, BLK, 128), BF16), pltpu.VMEM((NQ, BLK, 128), BF16)],
        name=name, compiler_params=_cparams(("arbitrary",), 40),
    )(sinks, bias, projb, projb, projb, dattn)


def _disc(a_re, a_im, logdt, b_re, b_im):
    dt = jnp.exp(logdt)
    mag = jnp.exp(a_re * dt)
    ab_re = mag * jnp.cos(a_im * dt)
    ab_im = mag * jnp.sin(a_im * dt)
    nr = ab_re - 1.0
    ni = ab_im
    den = a_re * a_re + a_im * a_im
    z_re = (nr * a_re + ni * a_im) / den
    z_im = (ni * a_re - nr * a_im) / den
    return ab_re, ab_im, z_re * b_re - z_im * b_im, z_re * b_im + z_im * b_re


def _group_mask():
    row = lax.broadcasted_iota(jnp.int32, (SW, NS), 0) // H
    col = lax.broadcasted_iota(jnp.int32, (SW, NS), 1) // P
    return row == col


def _block_diag(re, im):
    mask = _group_mask()
    z = jnp.zeros((SW, NS), F32)
    return jnp.concatenate([jnp.where(mask, jnp.tile(re, (G, 1)), z), jnp.where(mask, jnp.tile(im, (G, 1)), z)], axis=1)


def _block_diag_t(big):
    mask = _group_mask()
    z = jnp.zeros((SW, NS), F32)
    re = jnp.sum(jnp.where(mask, big[:, :NS], z).reshape(G, H, NS), axis=0)
    im = jnp.sum(jnp.where(mask, big[:, NS:], z).reshape(G, H, NS), axis=0)
    return re, im


def _ssm_prep(a_re, a_im, logdt, b_re, b_im, c_re, c_im, name):
    def body(are, aim, ldt, bre, bim, cre, cim, ab_ref, bm_ref, cm_ref):
        ab_re, ab_im, bb_re, bb_im = _disc(are[...], aim[...], ldt[...], bre[...], bim[...])
        ab_ref[...] = jnp.concatenate([ab_re, ab_im], axis=1)
        bm_ref[...] = _block_diag(bb_re, bb_im).astype(BF16)
        cm_ref[...] = _block_diag(cre[...], -cim[...]).astype(BF16)

    return pl.pallas_call(body, out_shape=(SDS((1, 2 * NS), F32), SDS((SW, 2 * NS), BF16), SDS((SW, 2 * NS), BF16)),
                          name=name, compiler_params=pltpu.CompilerParams(vmem_limit_bytes=48 << 20),
                          )(a_re, a_im, logdt, b_re, b_im, c_re, c_im)


def _ssm_param_bwd(a_re, a_im, logdt, b_re, b_im, dab8, dbm, dcm, name):
    def body(are, aim, ldt, bre, bim, dab_ref, dbm_ref, dcm_ref, o_are, o_aim, o_ldt, o_bre, o_bim, o_cre, o_cim):
        dab = jnp.sum(dab_ref[...], axis=0, keepdims=True)
        dbb_re, dbb_im = _block_diag_t(dbm_ref[...])
        _, vjp = jax.vjp(_disc, are[...], aim[...], ldt[...], bre[...], bim[...])
        d_are, d_aim, d_ldt, d_bre, d_bim = vjp((dab[:, :NS], dab[:, NS:], dbb_re, dbb_im))
        o_are[...], o_aim[...], o_ldt[...], o_bre[...], o_bim[...] = d_are, d_aim, d_ldt, d_bre, d_bim
        dc_re, dc_imn = _block_diag_t(dcm_ref[...])
        o_cre[...] = dc_re
        o_cim[...] = -dc_imn

    v1, vh = SDS((1, NS), F32), SDS((H, NS), F32)
    return pl.pallas_call(body, out_shape=(v1, v1, v1, vh, vh, vh, vh), name=name,
                          compiler_params=pltpu.CompilerParams(vmem_limit_bytes=56 << 20),
                          )(a_re, a_im, logdt, b_re, b_im, dab8, dbm, dcm)


def _ssm_scan(src, wmat, ab, *, reverse, ends=None, xs=None, init=None, wproj=None, name, tk=32):
    L = src.shape[0]
    seg_len = L // NSEG
    tk = min(tk, seg_len)
    rows = NSEG * tk
    nch = L // rows
    n_sq = int(math.log2(seg_len))
    assert 2 ** n_sq == seg_len and L % rows == 0
    first_pass = ends is None
    with_dab = (not first_pass) and reverse
    with_proj = wproj is not None
    assert not (with_proj and first_pass)
    slab = 512
    n_slab = NS // slab

    def body(*refs):
        src_ref, w_ref, ab_ref = refs[:3]
        pos = 3
        if not first_pass:
            ends_ref = refs[pos]
            pos += 1
        if with_dab:
            xs_ref, xsh_ref, init_ref = refs[pos:pos + 3]
            pos += 3
        if with_proj:
            wproj_ref = refs[pos]
            pos += 1
        if first_pass:
            (e_ref,) = refs[pos:pos + 1]
            pos += 1
        else:
            st_out_ref, aux_ref = refs[pos:pos + 2]
            pos += 2
        if with_proj:
            proj_ref = refs[pos]
            pos += 1
        buf_ref, st_ref = refs[pos:pos + 2]
        i = pl.program_id(0)
        a_re = ab_ref[:, :NS]
        a_im = -ab_ref[:, NS:] if reverse else ab_ref[:, NS:]

        @pl.when(i == 0)
        def _():
            if first_pass:
                st_ref[...] = jnp.zeros_like(st_ref)
            else:
                pr, pi = a_re, a_im
                for _ in range(n_sq):
                    pr, pi = pr * pr - pi * pi, 2.0 * pr * pi
                zr = jnp.zeros((1, NS), F32)
                cr, ci = zr, zr
                order = list(range(NSEG - 1, -1, -1)) if reverse else list(range(NSEG))
                st_ref[order[0]:order[0] + 1, :] = jnp.zeros((1, 2 * NS), F32)
                for jprev, j in zip(order[:-1], order[1:]):
                    er, ei = ends_ref[jprev:jprev + 1, :NS], ends_ref[jprev:jprev + 1, NS:]
                    cr, ci = er + pr * cr - pi * ci, ei + pr * ci + pi * cr
                    st_ref[j:j + 1, :NS] = cr
                    st_ref[j:j + 1, NS:] = ci
                if not reverse:
                    aux_ref[...] = st_ref[...]
                else:
                    aux_ref[...] = jnp.zeros_like(aux_ref)

        buf_ref[...] = jnp.dot(src_ref[...].astype(BF16), w_ref[...], preferred_element_type=F32)

        for s in range(n_slab):
            re_sl, im_sl = pl.ds(s * slab, slab), pl.ds(NS + s * slab, slab)
            ar = jnp.broadcast_to(a_re[:, s * slab:(s + 1) * slab], (NSEG, slab))
            ai = jnp.broadcast_to(a_im[:, s * slab:(s + 1) * slab], (NSEG, slab))

            def step(t, carry, re_sl=re_sl, im_sl=im_sl, ar=ar, ai=ai):
                k = (tk - 1 - t) if reverse else t
                r0 = pl.multiple_of(k * NSEG, NSEG)
                xr, xi = carry[0], carry[1]
                nr = ar * xr - ai * xi + buf_ref[pl.ds(r0, NSEG), re_sl]
                ni = ar * xi + ai * xr + buf_ref[pl.ds(r0, NSEG), im_sl]
                if not first_pass:
                    buf_ref[pl.ds(r0, NSEG), re_sl] = nr
                    buf_ref[pl.ds(r0, NSEG), im_sl] = ni
                if not with_dab:
                    return nr, ni
                rp = pl.multiple_of((k - 1) * NSEG, NSEG)
                xpr, xpi = xs_ref[pl.ds(rp, NSEG), re_sl], xs_ref[pl.ds(rp, NSEG), im_sl]
                return nr, ni, carry[2] + nr * xpr + ni * xpi, carry[3] + ni * xpr - nr * xpi

            carry = (st_ref[:, re_sl], st_ref[:, im_sl])
            if with_dab:
                z = jnp.zeros((NSEG, slab), F32)
                carry = lax.fori_loop(0, tk - 1, step, carry + (z, z))
                xr, xi, dr, di = carry
                nr = ar * xr - ai * xi + buf_ref[pl.ds(0, NSEG), re_sl]
                ni = ar * xi + ai * xr + buf_ref[pl.ds(0, NSEG), im_sl]
                buf_ref[pl.ds(0, NSEG), re_sl] = nr
                buf_ref[pl.ds(0, NSEG), im_sl] = ni
                at_start = i == nch - 1
                xpr = jnp.where(at_start, init_ref[:, re_sl], xsh_ref[:, re_sl])
                xpi = jnp.where(at_start, init_ref[:, im_sl], xsh_ref[:, im_sl])
                aux_ref[:, re_sl] += dr + nr * xpr + ni * xpi
                aux_ref[:, im_sl] += di + ni * xpr - nr * xpi
                carry = (nr, ni)
            else:
                carry = lax.fori_loop(0, tk, step, carry)
            st_ref[:, re_sl] = carry[0]
            st_ref[:, im_sl] = carry[1]

        if first_pass:
            @pl.when(i == nch - 1)
            def _():
                e_ref[...] = st_ref[...]
        else:
            st_out_ref[...] = buf_ref[...].astype(st_out_ref.dtype)
            if with_proj:
                proj_ref[...] = lax.dot_general(buf_ref[...].astype(BF16), wproj_ref[...], _NT, preferred_element_type=F32)

    chunk = (lambda i: (nch - 1 - i, 0)) if reverse else (lambda i: (i, 0))
    whole = lambda i: (0, 0)
    ins = [src, wmat, ab]
    once = pl.Buffered(1)
    in_specs = [pl.BlockSpec((rows, SW), chunk), pl.BlockSpec((SW, 2 * NS), whole, pipeline_mode=once),
                pl.BlockSpec((1, 2 * NS), whole)]
    small = SDS((NSEG, 2 * NS), F32)
    small_spec = pl.BlockSpec((NSEG, 2 * NS), whole)
    if not first_pass:
        ins.append(ends)
        in_specs.append(small_spec)
    if with_dab:
        ins += [xs, xs, init]
        in_specs += [pl.BlockSpec((rows, 2 * NS), chunk),
                     pl.BlockSpec((NSEG, 2 * NS), lambda i: (jnp.maximum((nch - 1 - i) * tk - 1, 0), 0)),
                     small_spec]
    if with_proj:
        ins.append(wproj)
        in_specs.append(pl.BlockSpec((SW, 2 * NS), whole, pipeline_mode=once))
    if first_pass:
        out_shape, out_specs = small, small_spec
    else:
        out_shape = (SDS((L, 2 * NS), BF16 if reverse else F32), small)
        out_specs = (pl.BlockSpec((rows, 2 * NS), chunk), small_spec)
        if with_proj:
            out_shape += (SDS((L, SW), F32),)
            out_specs += (pl.BlockSpec((rows, SW), chunk),)
    return pl.pallas_call(
        body, out_shape=out_shape, grid=(nch,), in_specs=in_specs, out_specs=out_specs,
        scratch_shapes=[pltpu.VMEM((rows, 2 * NS), F32), pltpu.VMEM((NSEG, 2 * NS), F32)], name=name,
        compiler_params=_cparams(("arbitrary",), 56),
    )(*ins)


def _to_segments(a):
    L, c = a.shape
    return a.reshape(NSEG, L // NSEG, c).transpose(1, 0, 2).reshape(L, c)


def _from_segments(a):
    L, c = a.shape
    return a.reshape(L // NSEG, NSEG, c).transpose(1, 0, 2).reshape(L, c)


def _peer(x, y, c, m):
    return ((1 - x) if (m >> 2) & 1 else x, (1 - y) if (m >> 1) & 1 else y, (1 - c) if m & 1 else c)


def _dev_index(p):
    return 4 * p[0] + 2 * p[1] + p[2]


def _exchange(arrs, scatter, name):
    n = len(arrs)

    def body(*refs):
        ins, outs = refs[:n], refs[n:2 * n]
        send_sems, recv_sems, loc_sems = refs[2 * n:]
        x, y, c = lax.axis_index("x"), lax.axis_index("y"), lax.axis_index("c")
        me = _dev_index((x, y, c))

        def src(w, to):
            return ins[w].at[to] if scatter else ins[w]

        def local(w):
            return pltpu.make_async_copy(src(w, me), outs[w].at[me], loc_sems.at[w])

        def remote(w, m):
            peer = _peer(x, y, c, m)
            return pltpu.make_async_remote_copy(src_ref=src(w, _dev_index(peer)), dst_ref=outs[w].at[me],
                                                send_sem=send_sems.at[w, m - 1], recv_sem=recv_sems.at[w, m - 1],
                                                device_id=peer, device_id_type=pl.DeviceIdType.MESH)

        def arrival(w, m):
            peer = _peer(x, y, c, m)
            return pltpu.make_async_remote_copy(src_ref=src(w, me), dst_ref=outs[w].at[_dev_index(peer)],
                                                send_sem=send_sems.at[w, m - 1], recv_sem=recv_sems.at[w, m - 1],
                                                device_id=peer, device_id_type=pl.DeviceIdType.MESH)

        for w in range(n):
            local(w).start()
        for w in range(n):
            for m in range(1, N_DEV):
                remote(w, m).start()
        for w in range(n):
            for m in range(1, N_DEV):
                arrival(w, m).wait_recv()
        for w in range(n):
            for m in range(1, N_DEV):
                remote(w, m).wait_send()
        for w in range(n):
            local(w).wait()

    anyspec = pl.BlockSpec(memory_space=pl.ANY)
    out_shape = tuple(SDS(a.shape if scatter else (N_DEV,) + a.shape, a.dtype) for a in arrs)
    return pl.pallas_call(
        body, out_shape=out_shape, in_specs=[anyspec] * n, out_specs=tuple([anyspec] * n),
        scratch_shapes=[pltpu.SemaphoreType.DMA((n, N_DEV - 1)), pltpu.SemaphoreType.DMA((n, N_DEV - 1)),
                        pltpu.SemaphoreType.DMA((n,))],
        name=name, compiler_params=pltpu.CompilerParams(has_side_effects=True),
    )(*arrs)


_HBM = pl.BlockSpec(memory_space=pltpu.HBM)
_SEM = pl.BlockSpec(memory_space=pltpu.SEMAPHORE)
_EFFECT = pltpu.SideEffectType.DATAFLOW_SIDE_EFFECTING


def _sem_index(w, m):
    return w * (N_DEV - 1) + m - 1


_ALL_MASKS = tuple(range(1, N_DEV))
_CHIP_MASKS = (2, 4, 6)
_FIRST_HOP_MASKS = (1,) + _CHIP_MASKS


def _exchange_start(arrs, scatter, name, masks=_ALL_MASKS):
    n = len(arrs)
    lands = [lax.empty(a.shape if scatter else (N_DEV,) + a.shape, a.dtype) for a in arrs]

    def body(*refs):
        ins, zones = refs[:n], refs[n:2 * n]
        send_sems, recv_sems = refs[2 * n], refs[2 * n + 1]
        token = refs[-1]
        x, y, c = lax.axis_index("x"), lax.axis_index("y"), lax.axis_index("c")
        me = _dev_index((x, y, c))
        for w in range(n):
            for m in masks:
                peer = _peer(x, y, c, m)
                pltpu.make_async_remote_copy(
                    src_ref=ins[w].at[_dev_index(peer)] if scatter else ins[w], dst_ref=zones[w].at[me],
                    send_sem=send_sems.at[_sem_index(w, m)], recv_sem=recv_sems.at[_sem_index(w, m)],
                    device_id=peer, device_id_type=pl.DeviceIdType.MESH).start()
        token[...] = jnp.zeros_like(token)

    sems = pltpu.SemaphoreType.DMA((n * (N_DEV - 1),))
    res = pl.pallas_call(
        body, name=name,
        out_shape=(sems, sems, *[pltpu.HBM(a.shape, a.dtype) for a in arrs], *[pltpu.HBM(z.shape, z.dtype) for z in lands],
                   SDS((8, 128), F32)),
        in_specs=[_HBM] * (2 * n), out_specs=(_SEM, _SEM, *([_HBM] * (2 * n)), pl.BlockSpec(memory_space=pltpu.VMEM)),
        input_output_aliases={i: 2 + i for i in range(2 * n)},
        compiler_params=pltpu.CompilerParams(has_side_effects=_EFFECT),
    )(*[pltpu.with_memory_space_constraint(a, pltpu.HBM) for a in arrs],
      *[pltpu.with_memory_space_constraint(z, pltpu.HBM) for z in lands])
    return (res[0], res[1], list(res[2:2 + n]), list(res[2 + n:2 + 2 * n])), res[-1]


def _exchange_wait(handle, after, scatter, name, masks=_ALL_MASKS):
    send_sems, recv_sems, thru, lands = handle
    n = len(thru)

    def body(*refs):
        ins, zones = refs[:n], refs[n:2 * n]
        send_sems, recv_sems = refs[2 * n], refs[2 * n + 1]
        x, y, c = lax.axis_index("x"), lax.axis_index("y"), lax.axis_index("c")
        me = _dev_index((x, y, c))
        for w in range(n):
            for m in masks:
                peer = _peer(x, y, c, m)
                copy = pltpu.make_async_remote_copy(
                    src_ref=ins[w].at[me] if scatter else ins[w], dst_ref=zones[w].at[_dev_index(peer)],
                    send_sem=send_sems.at[_sem_index(w, m)], recv_sem=recv_sems.at[_sem_index(w, m)],
                    device_id=peer, device_id_type=pl.DeviceIdType.MESH)
                copy.wait_send()
                copy.wait_recv()

    res = pl.pallas_call(
        body, name=name,
        out_shape=(*[pltpu.HBM(a.shape, a.dtype) for a in thru], *[pltpu.HBM(z.shape, z.dtype) for z in lands]),
        in_specs=[_HBM] * (2 * n) + [_SEM, _SEM, pl.BlockSpec(memory_space=pl.ANY)], out_specs=tuple([_HBM] * (2 * n)),
        input_output_aliases={i: i for i in range(2 * n)},
        compiler_params=pltpu.CompilerParams(has_side_effects=_EFFECT),
    )(*thru, *lands, send_sems, recv_sems, after)
    return list(res[:n]), list(res[n:])


def _forward_start(zones, name):
    n = len(zones)

    def body(*refs):
        zs = refs[:n]
        send_sems, recv_sems = refs[n], refs[n + 1]
        token = refs[-1]
        x, y, c = lax.axis_index("x"), lax.axis_index("y"), lax.axis_index("c")
        for w in range(n):
            for m in _CHIP_MASKS:
                slot = zs[w].at[_dev_index(_peer(x, y, c, m))]
                pltpu.make_async_remote_copy(
                    src_ref=slot, dst_ref=slot, send_sem=send_sems.at[_sem_index(w, m)],
                    recv_sem=recv_sems.at[_sem_index(w, m)], device_id=(x, y, 1 - c),
                    device_id_type=pl.DeviceIdType.MESH).start()
        token[...] = jnp.zeros_like(token)

    sems = pltpu.SemaphoreType.DMA((n * (N_DEV - 1),))
    res = pl.pallas_call(
        body, name=name, out_shape=(sems, sems, *[pltpu.HBM(z.shape, z.dtype) for z in zones], SDS((8, 128), F32)),
        in_specs=[_HBM] * n, out_specs=(_SEM, _SEM, *([_HBM] * n), pl.BlockSpec(memory_space=pltpu.VMEM)),
        input_output_aliases={i: 2 + i for i in range(n)},
        compiler_params=pltpu.CompilerParams(has_side_effects=_EFFECT),
    )(*[pltpu.with_memory_space_constraint(z, pltpu.HBM) for z in zones])
    return (res[0], res[1], list(res[2:2 + n])), res[-1]


def _forward_wait(handle, after, name):
    send_sems, recv_sems, zones = handle
    n = len(zones)

    def body(*refs):
        zs = refs[:n]
        send_sems, recv_sems = refs[n], refs[n + 1]
        x, y, c = lax.axis_index("x"), lax.axis_index("y"), lax.axis_index("c")
        for w in range(n):
            for m in _CHIP_MASKS:
                copy = pltpu.make_async_remote_copy(
                    src_ref=zs[w].at[_dev_index(_peer(x, y, c, m))], dst_ref=zs[w].at[_dev_index(_peer(x, y, 1 - c, m))],
                    send_sem=send_sems.at[_sem_index(w, m)], recv_sem=recv_sems.at[_sem_index(w, m)],
                    device_id=(x, y, 1 - c), device_id_type=pl.DeviceIdType.MESH)
                copy.wait_send()
                copy.wait_recv()

    res = pl.pallas_call(
        body, name=name, out_shape=tuple(pltpu.HBM(z.shape, z.dtype) for z in zones),
        in_specs=[_HBM] * n + [_SEM, _SEM, pl.BlockSpec(memory_space=pl.ANY)], out_specs=tuple([_HBM] * n),
        input_output_aliases={i: i for i in range(n)},
        compiler_params=pltpu.CompilerParams(has_side_effects=_EFFECT),
    )(*zones, send_sems, recv_sems, after)
    return list(res)


def _adam_math(g, w, m, v):
    m = ADAM_B1 * m + (1.0 - ADAM_B1) * g
    v = ADAM_B2 * v + (1.0 - ADAM_B2) * (g * g)
    m_hat = m / (1.0 - ADAM_B1 ** ADAM_STEP)
    v_hat = v / (1.0 - ADAM_B2 ** ADAM_STEP)
    delta = -ADAM_LR * (m_hat / (jnp.sqrt(v_hat) + ADAM_EPS) + ADAM_WD * w)
    return delta, m, v


def _adam(parts, w, m, v, name, tr=128):
    r, c = w.shape
    tr = next(t for t in (tr, 64, 32, 16, 8) if r % t == 0)

    def fn(cc, rr, pb, wb, mb, vb):
        g = pb[0].astype(F32)
        for d in range(1, N_DEV):
            g = g + pb[d].astype(F32)
        delta, nm, nv = _adam_math(g, wb, mb, vb)
        return g, delta, nm, nv

    blk = ((tr, c), lambda cc, rr: (rr, 0))
    o = SDS((r, c), F32)
    return _ew(fn, [(parts, (N_DEV, tr, c), lambda cc, rr: (0, rr, 0)), (w, *blk), (m, *blk), (v, *blk)],
               [(o, *blk, None)] * 4, (1, r // tr), name)


_SHARDED = ("w_in", "w_glu", "w_branch_attn", "w_branch_ssm", "w_out", "w_up", "w_down")
_COL_SHARDED = ("w_in", "w_glu", "w_branch_attn", "w_branch_ssm", "w_up")
_GROUPS = {"a": ("w_in",), "b": ("w_glu", "w_branch_attn", "w_branch_ssm", "w_out"), "c": ("w_up", "w_down")}
_SMALL = ("attn_norm_g", "b_in", "attn_sinks", "ssm_a_re", "ssm_a_im", "ssm_log_dt", "ssm_b_re", "ssm_b_im",
          "ssm_c_re", "ssm_c_im", "ssm_d", "b_glu", "ffn_norm_g", "conv_w", "conv_b", "final_norm_g")
_WEIGHTS = ("attn_norm_g", "w_in", "b_in", "attn_sinks", "ssm_a_re", "ssm_a_im", "ssm_log_dt", "ssm_b_re", "ssm_b_im",
            "ssm_c_re", "ssm_c_im", "ssm_d", "w_glu", "b_glu", "w_branch_attn", "w_branch_ssm", "w_out", "ffn_norm_g",
            "w_up", "conv_w", "conv_b", "w_down", "final_norm_g")


def _unstack_cols(g):
    return g.transpose(1, 0, 2).reshape(g.shape[1], g.shape[0] * g.shape[2])


def _stack_cols(a, d=N_DEV):
    k, n = a.shape
    return a.reshape(k, d, n // d).transpose(1, 0, 2)


def _pack(arrs):
    flat = jnp.concatenate([a.reshape(-1) for a in arrs])
    pad = (-flat.shape[0]) % 1024
    return jnp.pad(flat, (0, pad)).reshape(-1, 128)


def _local_step(x, tgt, wget, small, gput):
    L = x.shape[0]
    nr = lambda tm: L // tm

    h = _rmsnorm_fwd(x, small["attn_norm_g"], "norm1")
    wts = dict(wget("a", h))
    projb = _mm(h, wts["w_in"], bias=small["b_in_p"], out_dtype=BF16, name="proj")
    proj = projb
    attn_bias = _attn_bias()
    attn = _attn_fwd(projb, small["attn_sinks"], attn_bias, "attn_fwd")

    ab, bmat, cmat = _ssm_prep(small["a_re"], small["a_im"], small["logdt"], small["b_re"], small["b_im"],
                               small["c_re"], small["c_im"], "ssm_prep")
    u_seg = _to_segments(proj[:, C_U:C_PAD])
    ends_f = _ssm_scan(u_seg, bmat, ab, reverse=False, tk=256, name="ssm_ends_fwd")
    xs, init_f, y_seg = _ssm_scan(u_seg, bmat, ab, reverse=False, ends=ends_f, wproj=cmat, tk=64, name="ssm_scan_fwd")
    y_mm = _from_segments(y_seg)

    def gelu_fn(c, r, yb, ub, db):
        yv = yb + db * ub
        return yv, _gelu(yv)

    tm = 512
    y, gy = _ew(gelu_fn, [(y_mm, *_rc(tm, 256)), (proj, *_rc(tm, 256, C_U // 256)), (small["ssm_d"], *_col(1, 256))],
                [(SDS((L, SW), F32), *_rc(tm, 256), None), (SDS((L, SW), BF16), *_rc(tm, 256), None)],
                (2, nr(tm)), "ssm_gelu")
    wts.update(wget("b", gy))
    def glu_ep(i, cols, rg, rv, bg_ref, bv_ref):
        val, sg = rv + bv_ref[...], _sigmoid(rg + bg_ref[...])
        return val * sg, val, sg

    ssm, glu_val, glu_sig = _mm(gy, wts["w_glu"], b2=wts["w_glu"], n_cols=SW, b_col_off=1, tn=SW, epilogue=glu_ep,
                                extras=[(small["b_glu"], "col", 1), (small["b_glu"], "col", 0)],
                                outs=[(SDS((L, SW), BF16), "tile"), (SDS((L, SW), F32), "tile"), (SDS((L, SW), F32), "tile")],
                                name="glu_gate")
    f32 = lambda ref, cols: ref[:, cols].astype(F32)
    tnm = 1024
    gate_tiles = [(projb, "tile", C_GA // tnm), (projb, "tile", C_GS // tnm)]

    def merge_ep(i, cols, ra, rs, ga, gs):
        sa, ss = _sigmoid(f32(ga, cols)), _sigmoid(f32(gs, cols))
        return sa * ra + ss * rs, ra, rs, sa, ss

    merged, br_a, br_s, sig_a, sig_s = _mm(attn, wts["w_branch_attn"], a2=ssm, b2=wts["w_branch_ssm"], tm=512, tn=tnm,
                                           extras=gate_tiles, epilogue=merge_ep,
                                           outs=[(SDS((L, D), BF16), "tile")] * 5, name="branch_merge")
    def norm2_ep(i, cols, r, _, g_ref):
        rstd = lax.rsqrt(jnp.mean(r * r, axis=-1, keepdims=True) + RMS_EPS)
        return r, (r * rstd) * g_ref[...]

    x1, h2 = _mm(merged, wts["w_out"], res=x, tm=512, tn=D, epilogue=norm2_ep, extras=[(small["ffn_norm_g"], "col", 0)],
                 outs=[(SDS((L, D), F32), "tile"), (SDS((L, D), BF16), "tile")], name="out_proj_norm2")
    wts.update(wget("c", h2))
    conv_w = wts["conv_w"]
    w_up = wts["w_up"]
    tcf = 1408
    tma = 256
    hb = 16

    def conv_gate(first, gate, halo, cw, cb):
        halo = halo * jnp.logical_not(first).astype(F32)
        g1, g2 = _shift_rows(gate, halo, 1), _shift_rows(gate, halo, 2)
        return cb + cw[2:3] * gate + cw[1:2] * g1 + cw[0:1] * g2, g1, g2

    tmu, tnu = 1024, 512

    def up_ep(i, cols, rg, rv, h2_halo, wg, cw, cb):
        halo = jnp.dot(h2_halo[...], wg[:, cols], preferred_element_type=F32)[hb - 8:]
        gl, glg = _gelu_and_grad(conv_gate(i == 0, rg, halo, cw[:, cols], cb[:, cols])[0])
        return rg, rv * gl, gl, rv * glg

    up_g, act, gelu_cg, val_gelu_grad = _mm(
        h2, w_up, b2=w_up, n_cols=DFF, b_col_off=DFF // tnu, tm=tmu, tn=tnu, epilogue=up_ep,
        outs=[(SDS((L, DFF), BF16), "tile")] * 4, name="ffn_up_act",
        extras=[(h2, "spec", ((hb, D), lambda j, i: (jnp.maximum(i * (tmu // hb) - 1, 0), 0))),
                (w_up, "spec", ((D, tnu), lambda j, i: (0, j + DFF // tnu))), (conv_w, "col", 0),
                (small["conv_b"], "col", 0)])
    x2 = _mm(act, wts["w_down"], res=x1, name="ffn_down")
    d_x2, d_x2b, loss_cols, d_gf = _final_loss(x2, small["final_norm_g"], tgt, "final_loss")
    loss = jnp.sum(loss_cols)

    dw_down = _mm(act, d_x2b, ta=True, out_dtype=BF16, tm=tcf, tk=2048, name="dw_down")
    tmd, tnd = 1024, 512

    def dact_ep(i, cols, da, _, gate_ref, halo_ref, gl_ref, vg_ref):
        gate, gl = f32(gate_ref, cols), f32(gl_ref, cols)
        halo = f32(halo_ref, cols)[hb - 8:] * (i > 0).astype(F32)
        g1, g2 = _shift_rows(gate, halo, 1), _shift_rows(gate, halo, 2)
        d_cg = da * f32(vg_ref, cols)
        row3 = lax.broadcasted_iota(jnp.int32, (3, da.shape[1]), 0)
        s0 = jnp.sum(d_cg * g2, axis=0, keepdims=True)
        s1 = jnp.sum(d_cg * g1, axis=0, keepdims=True)
        s2 = jnp.sum(d_cg * gate, axis=0, keepdims=True)
        dcw = jnp.where(row3 == 0, s0, jnp.where(row3 == 1, s1, s2))
        return da * gl, d_cg, dcw, jnp.sum(d_cg, axis=0, keepdims=True)

    d_up, d_cg, d_conv_w, d_conv_b = _mm(
        d_x2b, wts["w_down"], tb=True, tm=tmd, tn=tnd, epilogue=dact_ep, name="d_act_bwd",
        extras=[(up_g, "tile", 0), (up_g, "spec", ((hb, tnd), lambda j, i: (jnp.maximum(i * (tmd // hb) - 1, 0), j))),
                (gelu_cg, "tile", 0), (val_gelu_grad, "tile", 0)],
        outs=[(SDS((L, 2 * DFF), BF16), "tile"), (SDS((L, DFF), BF16), "tile"), (SDS((3, DFF), F32), "colacc"),
              (SDS((1, DFF), F32), "colacc")])
    ncf = DFF // tcf

    tmg = 512

    def gate_bwd(c, r, dcg, halo, cw):
        halo = halo[:8] * (r < nr(tmg) - 1).astype(F32)
        return (cw[2:3] * dcg + cw[1:2] * _shift_rows_up(dcg, halo, 1) + cw[0:1] * _shift_rows_up(dcg, halo, 2),)

    (d_up,) = _ew(gate_bwd, [(d_cg, *_rc(tmg, tcf)),
                             (d_cg, (hb, tcf), lambda c, r: (jnp.minimum((r + 1) * (tmg // hb), L // hb - 1), c)),
                             (conv_w, *_col(3, tcf))],
                  [(SDS((L, 2 * DFF), BF16), *_rc(tmg, tcf, ncf), None)], (ncf, nr(tmg)), "ffn_gate_bwd", into=d_up)
    d_h2 = _mm(d_up, w_up, tb=True, name="d_h2")
    assert tcf == 2 * DFF // N_DEV
    dw_up = _mm(h2, d_up, ta=True, out_dtype=BF16, tn=tcf, tk=2048, stack_out=True, name="dw_up")
    tok = gput("c", {"w_up": dw_up, "w_down": dw_down})
    d_x1, d_g2 = _rmsnorm_bwd(d_h2, x1, small["ffn_norm_g"] + tok[0, 0], d_x2, "norm2_bwd")

    dw_out = _mm(merged, d_x1, ta=True, out_dtype=BF16, name="dw_out")

    def dmerge_ep(i, cols, dm, _, a_ref, s_ref, sa_ref, ss_ref):
        sa, ss = f32(sa_ref, cols), f32(ss_ref, cols)
        return dm * sa, dm * ss, dm * (f32(a_ref, cols) * (sa * (1.0 - sa))), dm * (f32(s_ref, cols) * (ss * (1.0 - ss)))

    d_bra, d_brs, d_ga, d_gs = _mm(d_x1, wts["w_out"], tb=True, tm=512, tn=tnm, epilogue=dmerge_ep,
                                   extras=[(br_a, "tile", 0), (br_s, "tile", 0), (sig_a, "tile", 0), (sig_s, "tile", 0)],
                                   outs=[(SDS((L, D), BF16), "tile")] * 4, name="d_merged_bwd")
    d_attn = _mm(d_bra, wts["w_branch_attn"], tb=True, out_dtype=BF16, name="d_attn")
    dw_ba = _mm(attn, d_bra, ta=True, out_dtype=BF16, name="dw_branch_attn")
    d_ssm = _mm(d_brs, wts["w_branch_ssm"], tb=True, name="d_ssm")
    dw_bs = _mm(ssm, d_brs, ta=True, out_dtype=BF16, name="dw_branch_ssm")
    dq, dkv_cur, dkv_prev, d_sinks = _attn_bwd(projb, small["attn_sinks"], attn_bias, d_attn, "attn_bwd")

    def glu_bwd(c, r, ds, vb, sg):
        return ds * sg, ds * vb * (sg * (1.0 - sg))

    d_glu_v, d_glu_g = _ew(glu_bwd, [(d_ssm, *_rc(tm, SW)), (glu_val, *_rc(tm, SW)), (glu_sig, *_rc(tm, SW))],
                           [(SDS((L, SW), F32), *_rc(tm, SW), None)] * 2, (1, nr(tm)), "glu_gate_bwd")
    d_glu = jnp.concatenate([d_glu_v, d_glu_g], axis=1)
    d_gy = _mm(d_glu, wts["w_glu"], tb=True, name="d_gelu_y")
    dw_glu = _mm(gy, d_glu, ta=True, out_dtype=BF16, name="dw_glu")

    tok = gput("b", {"w_glu": dw_glu, "w_branch_attn": dw_ba, "w_branch_ssm": dw_bs, "w_out": dw_out})
    ab = ab + tok[0, 0]

    def gelu_bwd(c, r, dg, yb, ub, dgl):
        dy = dg * _gelu_grad(yb)
        return dy, jnp.sum(dy * ub, axis=0, keepdims=True), jnp.sum(dgl, axis=0, keepdims=True)

    dy, d_ssm_d, d_b_glu = _ew(
        gelu_bwd, [(d_gy, *_rc(tm, 256)), (y, *_rc(tm, 256)), (proj, *_rc(tm, 256, C_U // 256)), (d_glu, *_rc(tm, 512))],
        [(SDS((L, SW), F32), *_rc(tm, 256), None), (SDS((1, SW), F32), *_col(1, 256), "r"),
         (SDS((1, 2 * SW), F32), *_col(1, 512), "r")], (2, nr(tm)), "ssm_gelu_bwd")

    dy_seg = _to_segments(dy)
    ends_r = _ssm_scan(dy_seg, cmat, ab, reverse=True, tk=256, name="ssm_ends_bwd")
    lam, dab8, du_seg = _ssm_scan(dy_seg, cmat, ab, reverse=True, ends=ends_r, xs=xs, init=init_f, wproj=bmat,
                                  tk=64, name="ssm_scan_bwd")
    du_mm = _from_segments(du_seg)
    dbm = _mm(u_seg, lam, ta=True, tm=512, name="ssm_dbmat")
    dcm = _mm(dy_seg, xs, ta=True, tm=512, name="ssm_dcmat")
    d_are, d_aim, d_ldt, d_bre, d_bim, d_cre, d_cim = _ssm_param_bwd(
        small["a_re"], small["a_im"], small["logdt"], small["b_re"], small["b_im"], dab8, dbm, dcm, "ssm_param_bwd")

    nb = L // BLK

    def dproj_fn(c, r, dqb, cur, prv, du, dyb, dsk, dga, dgs):
        dkv = cur + prv * (r < nb - 1).astype(F32)
        dub = du + dsk * dyb
        full = jnp.concatenate([dqb, dkv, dub, jnp.zeros((BLK, C_GA - C_PAD), F32), dga, dgs], axis=1)
        return full, jnp.sum(full, axis=0, keepdims=True)

    rowb = lambda w: ((BLK, w), lambda c, r: (r, 0))
    dproj, d_b_in = _ew(
        dproj_fn, [(dq, *rowb(AW)), (dkv_cur, *rowb(256)),
                   (dkv_prev, (BLK, 256), lambda c, r: (jnp.minimum(r + 1, nb - 1), 0)),
                   (du_mm, *rowb(SW)), (dy, *rowb(SW)), (small["ssm_d"], *_col(1, SW)), (d_ga, *rowb(D)), (d_gs, *rowb(D))],
        [(SDS((L, INP), BF16), *rowb(INP), None), (SDS((1, INP), F32), *_col(1, INP), "all")], (1, nb), "dproj")
    tok_small = gput("small", {
        "b_in": _unpad_cols(d_b_in), "attn_sinks": d_sinks[:, :NQ], "a_re": d_are, "a_im": d_aim, "logdt": d_ldt,
        "b_re": d_bre, "b_im": d_bim, "c_re": d_cre, "c_im": d_cim, "ssm_d": d_ssm_d, "b_glu": d_b_glu,
        "ffn_norm_g": d_g2, "conv_w": d_conv_w, "conv_b": d_conv_b, "final_norm_g": d_gf})
    dw_in = _mm(h, dproj, ta=True, out_dtype=BF16, name="dw_in")
    tok = gput("a", {"w_in": _unpad_cols(dw_in)}) + tok_small
    d_h = _mm(dproj, wts["w_in"], tb=True, bias=jnp.zeros((1, D), F32) + tok[0, 0], name="d_h")
    grad_x, d_g1 = _rmsnorm_bwd(d_h, x, small["attn_norm_g"], d_x1, "norm1_bwd")
    return loss, grad_x, {"attn_norm_g": d_g1}


def _small_layouts(p):
    gp = lambda a: a.reshape(1, NS)
    hgp = lambda a: a.transpose(2, 0, 1).reshape(H, NS)
    chgp = lambda a: a.transpose(1, 0, 2).reshape(H, NS)
    return {
        "attn_norm_g": p["attn_norm_g"].reshape(1, D), "ffn_norm_g": p["ffn_norm_g"].reshape(1, D),
        "final_norm_g": p["final_norm_g"].reshape(1, D),
        "b_in_p": _pad_cols(p["b_in"].reshape(1, INC)),
        "attn_sinks": p["attn_sinks"].reshape(1, NQ),
        "a_re": gp(p["ssm_a_re"]), "a_im": gp(p["ssm_a_im"]), "logdt": jnp.repeat(p["ssm_log_dt"], P).reshape(1, NS),
        "b_re": hgp(p["ssm_b_re"]), "b_im": hgp(p["ssm_b_im"]), "c_re": chgp(p["ssm_c_re"]), "c_im": chgp(p["ssm_c_im"]),
        "ssm_d": p["ssm_d"].reshape(1, SW), "b_glu": p["b_glu"].reshape(1, 2 * SW),
        "conv_b": p["conv_b"].reshape(1, DFF),
    }


def _small_grads_to_param_shapes(sg):
    from_hgp = lambda a: a.reshape(H, G, P).transpose(1, 2, 0)
    from_chgp = lambda a: a.reshape(H, G, P).transpose(1, 0, 2)
    flat = lambda a: a.reshape(-1)
    to_param = {
        "attn_norm_g": ("attn_norm_g", flat), "b_in": ("b_in", flat), "attn_sinks": ("attn_sinks", flat),
        "a_re": ("ssm_a_re", lambda a: a.reshape(G, P)), "a_im": ("ssm_a_im", lambda a: a.reshape(G, P)),
        "logdt": ("ssm_log_dt", lambda a: jnp.sum(a.reshape(G, P), axis=1)),
        "b_re": ("ssm_b_re", from_hgp), "b_im": ("ssm_b_im", from_hgp),
        "c_re": ("ssm_c_re", from_chgp), "c_im": ("ssm_c_im", from_chgp),
        "ssm_d": ("ssm_d", flat), "b_glu": ("b_glu", flat), "ffn_norm_g": ("ffn_norm_g", flat),
        "conv_w": ("conv_w", lambda a: a), "conv_b": ("conv_b", flat), "final_norm_g": ("final_norm_g", flat),
    }
    return {to_param[k][0]: to_param[k][1](a) for k, a in sg.items()}


def kernel(x, attn_norm_g, w_in, b_in, attn_sinks, ssm_a_re, ssm_a_im, ssm_log_dt, ssm_b_re, ssm_b_im, ssm_c_re, ssm_c_im, ssm_d, w_glu, b_glu, w_branch_attn, w_branch_ssm, w_out, ffn_norm_g, w_up, conv_w, conv_b, w_down, final_norm_g, loss_target, m_attn_norm_g, m_w_in, m_b_in, m_attn_sinks, m_ssm_a_re, m_ssm_a_im, m_ssm_log_dt, m_ssm_b_re, m_ssm_b_im, m_ssm_c_re, m_ssm_c_im, m_ssm_d, m_w_glu, m_b_glu, m_w_branch_attn, m_w_branch_ssm, m_w_out, m_ffn_norm_g, m_w_up, m_conv_w, m_conv_b, m_w_down, m_final_norm_g, v_attn_norm_g, v_w_in, v_b_in, v_attn_sinks, v_ssm_a_re, v_ssm_a_im, v_ssm_log_dt, v_ssm_b_re, v_ssm_b_im, v_ssm_c_re, v_ssm_c_im, v_ssm_d, v_w_glu, v_b_glu, v_w_branch_attn, v_w_branch_ssm, v_w_out, v_ffn_norm_g, v_w_up, v_conv_w, v_conv_b, v_w_down, v_final_norm_g):
    args = dict(locals())
    sq = lambda a: a if a.ndim == 1 else a[0]
    wv = {n: sq(args[n]) for n in _WEIGHTS}
    mv = {n: sq(args["m_" + n]) for n in _WEIGHTS}
    vv = {n: sq(args["v_" + n]) for n in _WEIGHTS}
    me = 4 * lax.axis_index("x") + 2 * lax.axis_index("y") + lax.axis_index("c")

    gather, tok = {}, jnp.zeros((8, 128), F32)
    for grp in ("a", "b", "c"):
        shards = [(wv[n] + tok[0, 0]).astype(BF16) for n in _GROUPS[grp]]
        if grp == "c":
            shards.append(jnp.pad(wv["conv_w"] + tok[0, 0], ((0, 5), (0, 64))))
        gather[grp], tok = _exchange_start(shards, False, "gather_start_" + grp,
                                           masks=_FIRST_HOP_MASKS if grp == "a" else _ALL_MASKS)
    small = _small_layouts(wv)
    small["attn_norm_g"] = small["attn_norm_g"] + tok[0, 0]

    def own_slot(land, src):
        return lax.dynamic_update_slice_in_dim(land, src, me, axis=0)

    def wget(grp, after):
        if grp == "a":
            thru, lands = _exchange_wait(gather[grp], after, False, "gather_wait_a", masks=_FIRST_HOP_MASKS)
            fwd, fwd_tok = _forward_start(lands, "gather_forward_start_a")
            lands = _forward_wait(fwd, fwd_tok, "gather_forward_wait_a")
        else:
            thru, lands = _exchange_wait(gather[grp], after, False, "gather_wait_" + grp)
        full = {}
        for n, t, g in zip(_GROUPS[grp], thru, lands):
            g = own_slot(g, t[None])
            full[n] = _unstack_cols(g) if n in _COL_SHARDED else g.reshape(N_DEV * g.shape[1], g.shape[2])
        if grp == "a":
            full["w_in"] = _pad_cols(full["w_in"])
        if grp == "c":
            full["conv_w"] = _unstack_cols(own_slot(lands[-1], thru[-1][None])[:, :3, :DFF // N_DEV])
        return full

    scatter = {}

    early_names = [n for n in _SMALL if n != "attn_norm_g"]
    sgp = {}

    def gput(grp, grads):
        if grp == "small":
            sgp.update(_small_grads_to_param_shapes(grads))
            scatter[grp], token = _exchange_start([_pack([sgp[n] for n in early_names])], False, "gather_small_start")
            return token
        stacked = [grads[n] if n == "w_up" else
                   _stack_cols(grads[n]) if n in _COL_SHARDED else grads[n].reshape(N_DEV, -1, D) for n in _GROUPS[grp]]
        scatter[grp], token = _exchange_start(stacked, True, "scatter_start_" + grp)
        return token

    loss, grad_x, sg = _local_step(x[0], loss_target[0], wget, small, gput)
    loss = lax.psum(loss, MESH_AXES)

    sgp.update(_small_grads_to_param_shapes(sg))
    small_names = [n for n in _SMALL]
    (norm_all,) = _exchange([jnp.pad(sgp["attn_norm_g"].reshape(1, D), ((0, 7), (0, 0)))], False, "gather_norm_grad")
    thru, (small_all,) = _exchange_wait(scatter["small"], norm_all, False, "gather_small_wait")
    small_all = own_slot(small_all, thru[0][None])

    outs_g, outs_d, outs_m, outs_v = {}, {}, {}, {}
    for grp in ("c", "b", "a"):
        after = sgp["ssm_a_re"] if grp == "c" else norm_all
        thru, lands = _exchange_wait(scatter[grp], after, True, "scatter_wait_" + grp)
        for n, t, pt in zip(_GROUPS[grp], thru, lands):
            pt = own_slot(pt, lax.dynamic_slice_in_dim(t, me, 1, axis=0))
            outs_g[n], outs_d[n], outs_m[n], outs_v[n] = _adam(pt, wv[n], mv[n], vv[n], "adam_" + n)

    sizes = [int(math.prod(sgp[n].shape)) for n in early_names]
    offs = [0]
    for s in sizes:
        offs.append(offs[-1] + s)

    def local_part(n, a):
        if n == "conv_w":
            return lax.dynamic_slice(a, (0, me * (DFF // N_DEV)), (3, DFF // N_DEV))
        return a

    rows = small_all.shape[1]

    def sum_fn(cc, rr, pb, nb_):
        g, gn = pb[0], nb_[0]
        for d in range(1, N_DEV):
            g, gn = g + pb[d], gn + nb_[d]
        return g, gn

    gsum, gnorm = _ew(sum_fn, [(small_all, (N_DEV, rows, 128), lambda cc, rr: (0, 0, 0)),
                               (norm_all, (N_DEV, 8, D), lambda cc, rr: (0, 0, 0))],
                      [(SDS((rows, 128), F32), (rows, 128), lambda cc, rr: (0, 0), None),
                       (SDS((8, D), F32), (8, D), lambda cc, rr: (0, 0), None)], (1, 1), "sum_small_grads")
    gflat = gsum.reshape(-1)
    gsmall = {n: local_part(n, gflat[offs[i]:offs[i + 1]].reshape(sgp[n].shape)) for i, n in enumerate(early_names)}
    gsmall["attn_norm_g"] = gnorm[0]
    as2d = lambda a: a.reshape(1, -1) if a.ndim == 1 else a.reshape(a.shape[0], -1)
    n_small = len(small_names)

    def adam_small(*refs):
        for i in range(n_small):
            g_ref, w_ref, m_ref, v_ref = refs[4 * i:4 * i + 4]
            outs = refs[4 * n_small + 3 * i:4 * n_small + 3 * i + 3]
            for o_ref, val in zip(outs, _adam_math(g_ref[...], w_ref[...], m_ref[...], v_ref[...])):
                o_ref[...] = val

    small_ins = [as2d(t[n]) for n in small_names for t in (gsmall, wv, mv, vv)]
    small_outs = pl.pallas_call(adam_small, name="adam_small",
                                out_shape=[SDS(as2d(wv[n]).shape, F32) for n in small_names for _ in range(3)])(*small_ins)
    for i, n in enumerate(small_names):
        sd, sm, sv = (t.reshape(wv[n].shape) for t in small_outs[3 * i:3 * i + 3])
        outs_g[n], outs_d[n], outs_m[n], outs_v[n] = gsmall[n], sd, sm, sv

    lead = lambda n, a: a if args[n].ndim == 1 else a[None]
    grad_x = grad_x[None]
    return (loss, grad_x, *[lead(n, outs_g[n]) for n in _WEIGHTS], *[lead(n, outs_d[n]) for n in _WEIGHTS],
            *[lead(n, outs_m[n]) for n in _WEIGHTS], *[lead(n, outs_v[n]) for n in _WEIGHTS])
```

```python
import functools
import math

import jax
import jax.numpy as jnp
from jax import lax
from jax.experimental import pallas as pl
from jax.experimental.pallas import tpu as pltpu

F32 = jnp.float32
BF16 = jnp.bfloat16
SDS = jax.ShapeDtypeStruct

N_DEV = 8
D = 2048
NQ, NKV, HD = 16, 2, 64
AW = NQ * HD
BLK = 128
SW, G, H, P = 512, 32, 16, 64
NS = G * P
DFF = 5632
INC = AW + 2 * NKV * HD + SW + 2 * D
C_K, C_U, C_PAD = AW, AW + 2 * NKV * HD, AW + 2 * NKV * HD + SW
C_GA, C_GS, INP = D, 2 * D, 3 * D
RMS_EPS = 1e-6
NEG_BIG = -1e30
ADAM_LR, ADAM_B1, ADAM_B2, ADAM_EPS, ADAM_WD, ADAM_STEP = 0.001, 0.9, 0.999, 1e-08, 0.01, 10
NSEG = 8
VMEM_CAP_MB = 60
MESH_AXES = ("x", "y", "c")


def _pad_cols(a):
    zeros = jnp.zeros(a.shape[:-1] + (C_GA - C_PAD,), a.dtype)
    return jnp.concatenate([a[..., :C_PAD], zeros, a[..., C_PAD:]], axis=-1)


def _unpad_cols(a):
    return jnp.concatenate([a[..., :C_PAD], a[..., C_GA:]], axis=-1)


def _cparams(sem, vmem_mb):
    return pltpu.CompilerParams(dimension_semantics=sem, vmem_limit_bytes=min(int(vmem_mb), VMEM_CAP_MB) << 20)


LANES = 128


def _tile(dim, pref):
    if dim <= pref:
        return dim
    for t in range(pref - pref % LANES, 0, -LANES):
        if dim % t == 0:
            return t
    raise ValueError(f"no tile for {dim}")


def _mm(a, b, *, ta=False, tb=False, bias=None, res=None, out_dtype=F32, tm=1024, tn=1024, tk=3072, name,
        a2=None, b2=None, extras=(), epilogue=None, outs=None, stack_out=False, n_cols=None,
        b_col_off=0, b2_col_off=0):
    m, k = (a.shape[1], a.shape[0]) if ta else a.shape
    n = n_cols or (b.shape[0] if tb else b.shape[1])
    assert (b.shape[1] if tb else b.shape[0]) == k, (a.shape, b.shape, ta, tb)
    tm, tn, tk = _tile(m, tm), _tile(n, tn), _tile(k, tk)
    nk = k // tk
    dims = (((0 if ta else 1,), (1 if tb else 0,)), ((), ()))
    has_bias, has_res, has_b2 = bias is not None, res is not None, b2 is not None
    has_a2 = a2 is not None
    assert not (has_b2 and (nk > 1 or ta or tb)) and not (has_a2 and not has_b2)
    if epilogue is None:
        outs = [(SDS((n // tn, m, tn) if stack_out else (m, n), out_dtype), "tile")]
    n_ex, n_out = len(extras), len(outs)

    def body(*refs):
        a_ref, b_ref = refs[0], refs[1]
        pos = 2
        a2_ref = refs[pos] if has_a2 else a_ref
        pos += has_a2
        b2_ref = refs[pos] if has_b2 else None
        pos += has_b2
        bias_ref = refs[pos] if has_bias else None
        pos += has_bias
        res_ref = refs[pos] if has_res else None
        pos += has_res
        ex_refs = refs[pos:pos + n_ex]
        o_refs = refs[pos + n_ex:pos + n_ex + n_out]
        i = pl.program_id(1)

        def product(rhs_ref, cols=None, lhs=None):
            rhs = rhs_ref[...] if cols is None else (rhs_ref[cols, :] if tb else rhs_ref[:, cols])
            lhs = a_ref[...].astype(BF16) if lhs is None else lhs
            return lax.dot_general(lhs, rhs.astype(BF16), dims, preferred_element_type=F32)

        def finish(r, cols):
            if has_bias:
                r = r + bias_ref[:, cols]
            if has_res:
                r = r + res_ref[:, cols].astype(F32)
            if epilogue is None:
                o_refs[0][:, cols] = r.astype(o_refs[0].dtype)
                return
            r2 = None
            if has_b2:
                r2 = jnp.dot(a2_ref[...].astype(BF16), b2_ref[:, cols].astype(BF16), preferred_element_type=F32)
            vals = epilogue(i, cols, r, r2, *ex_refs)
            for o_ref, v, (_, kind) in zip(o_refs, vals, outs):
                if kind == "tile":
                    o_ref[:, cols] = v.astype(o_ref.dtype)
                else:
                    @pl.when(i == 0)
                    def _(o_ref=o_ref, v=v):
                        o_ref[:, cols] = v.astype(o_ref.dtype)

                    @pl.when(i > 0)
                    def _(o_ref=o_ref, v=v):
                        o_ref[:, cols] += v.astype(o_ref.dtype)

        whole = pl.ds(0, tn)
        if nk == 1:
            finish(product(b_ref), whole)
            return
        acc_ref = refs[-1]
        kk = pl.program_id(2)

        @pl.when(kk == 0)
        def _():
            acc_ref[...] = product(b_ref)

        @pl.when(jnp.logical_and(kk > 0, kk < nk - 1))
        def _():
            acc_ref[...] += product(b_ref)

        @pl.when(kk == nk - 1)
        def _():
            finish(acc_ref[...] + product(b_ref), whole)

    b_spec = (pl.BlockSpec((tn, tk), lambda j, i, kk: (j + b_col_off, kk)) if tb else
              pl.BlockSpec((tk, tn), lambda j, i, kk: (kk, j + b_col_off)))
    ins = [a, b]
    in_specs = [pl.BlockSpec((tk, tm), lambda j, i, kk: (kk, i)) if ta else pl.BlockSpec((tm, tk), lambda j, i, kk: (i, kk)),
                b_spec]
    tile_spec = pl.BlockSpec((tm, tn), lambda j, i, kk: (i, j))
    byt = 2 * tm * tk * a.dtype.itemsize + 2 * tk * tn * b.dtype.itemsize
    byt += (2 + has_b2) * 4 * tm * tn
    if has_a2:
        ins.append(a2)
        in_specs.append(pl.BlockSpec((tm, a2.shape[1]), lambda j, i, kk: (i, 0)))
        byt += 2 * tm * a2.shape[1] * a2.dtype.itemsize
    if has_b2:
        ins.append(b2)
        in_specs.append(pl.BlockSpec((b2.shape[0], tn), lambda j, i, kk: (0, j + b2_col_off)))
        byt += 2 * b2.shape[0] * tn * b2.dtype.itemsize
    if has_bias:
        ins.append(bias)
        in_specs.append(pl.BlockSpec((1, tn), lambda j, i, kk: (0, j)))
    if has_res:
        ins.append(res)
        in_specs.append(tile_spec)
        byt += 2 * tm * tn * res.dtype.itemsize
    for arr, kind, arg in extras:
        ins.append(arr)
        if kind == "tile":
            in_specs.append(pl.BlockSpec((tm, tn), lambda j, i, kk, arg=arg: (i, j + arg)))
            byt += 2 * tm * tn * arr.dtype.itemsize + 4 * tm * tn
        elif kind == "col":
            in_specs.append(pl.BlockSpec((arr.shape[0], tn), lambda j, i, kk, arg=arg: (0, j + arg)))
        else:
            in_specs.append(pl.BlockSpec(arg[0], lambda j, i, kk, im=arg[1]: im(j, i)))
    out_specs = []
    for sds, kind in outs:
        if kind == "tile":
            out_specs.append(pl.BlockSpec((None, tm, tn), lambda j, i, kk: (j, i, 0)) if stack_out else tile_spec)
            byt += 2 * tm * tn * jnp.dtype(sds.dtype).itemsize
        else:
            out_specs.append(pl.BlockSpec((sds.shape[0], tn), lambda j, i, kk: (0, j)))
    res_ = pl.pallas_call(
        body, out_shape=tuple(o[0] for o in outs), grid=(n // tn, m // tm, nk), in_specs=in_specs,
        out_specs=tuple(out_specs), scratch_shapes=[pltpu.VMEM((tm, tn), F32)] if nk > 1 else [], name=name,
        compiler_params=_cparams(("arbitrary", "arbitrary", "arbitrary"), byt / 2**20 + (8 if epilogue is None else 20)),
    )(*ins)
    return res_[0] if epilogue is None else res_


def _ew(fn, ins, outs, grid, name, vmem_mb=40, into=None):
    n_in = len(ins)
    accs = [o[3] for o in outs]

    def body(*refs):
        c, r = pl.program_id(0), pl.program_id(1)
        vals = fn(c, r, *[ref[...].astype(F32) for ref in refs[:n_in]])
        for o_ref, v, acc in zip(refs[n_in + (into is not None):], vals, accs):
            if acc is None:
                o_ref[...] = v.astype(o_ref.dtype)
            else:
                first = (r == 0) if acc == "r" else jnp.logical_and(r == 0, c == 0)

                @pl.when(first)
                def _(o_ref=o_ref, v=v):
                    o_ref[...] = v.astype(o_ref.dtype)

                @pl.when(jnp.logical_not(first))
                def _(o_ref=o_ref, v=v):
                    o_ref[...] += v.astype(o_ref.dtype)

    in_specs = [pl.BlockSpec(bs, im) for _, bs, im in ins]
    args = [a for a, _, _ in ins]
    if into is not None:
        in_specs.append(pl.BlockSpec(memory_space=pl.ANY))
        args.append(into)
    res = pl.pallas_call(
        body, out_shape=tuple(o[0] for o in outs), grid=grid, in_specs=in_specs,
        out_specs=tuple(pl.BlockSpec(bs, im) for _, bs, im, _ in outs), name=name,
        input_output_aliases={} if into is None else {n_in: 0},
        compiler_params=_cparams(("arbitrary", "arbitrary"), vmem_mb),
    )(*args)
    return res


def _rc(tm, tc, coff=0):
    return (tm, tc), (lambda c, r: (r, c + coff))


def _col(rows, tc, coff=0):
    return (rows, tc), (lambda c, r: (0, c + coff))


def _gelu(x):
    return 0.5 * x * (1.0 + lax.erf(x * (2.0 ** -0.5)))


def _gelu_and_grad(x):
    cdf = 0.5 * (1.0 + lax.erf(x * (2.0 ** -0.5)))
    return x * cdf, cdf + x * jnp.exp(-0.5 * x * x) * (1.0 / math.sqrt(2.0 * math.pi))


def _gelu_grad(x):
    return _gelu_and_grad(x)[1]


def _sigmoid(x):
    return 1.0 / (1.0 + jnp.exp(-x))


def _shift_rows(x, halo, s):
    rolled = pltpu.roll(x, s, 0)
    row8 = lax.broadcasted_iota(jnp.int32, halo.shape, 0)
    head = jnp.where(row8 < s, pltpu.roll(halo, s, 0), rolled[0:8])
    return jnp.concatenate([head, rolled[8:]], axis=0)


def _shift_rows_up(x, halo, s):
    tm = x.shape[0]
    rolled = pltpu.roll(x, tm - s, 0)
    row8 = lax.broadcasted_iota(jnp.int32, halo.shape, 0)
    tail = jnp.where(row8 >= 8 - s, pltpu.roll(halo, 8 - s, 0), rolled[tm - 8:])
    return jnp.concatenate([rolled[:tm - 8], tail], axis=0)


def _rmsnorm_fwd(x, g, name, tm=512):
    L = x.shape[0]

    def fn(c, r, xb, gb):
        rstd = lax.rsqrt(jnp.mean(xb * xb, axis=-1, keepdims=True) + RMS_EPS)
        return ((xb * rstd) * gb,)

    return _ew(fn, [(x, *_rc(tm, D)), (g, *_col(1, D))], [(SDS((L, D), BF16), *_rc(tm, D), None)], (1, L // tm), name)[0]


def _rmsnorm_bwd(dh, x, g, dres, name, tm=512):
    L = x.shape[0]

    def fn(c, r, dhb, xb, gb, drb):
        rstd = lax.rsqrt(jnp.mean(xb * xb, axis=-1, keepdims=True) + RMS_EPS)
        y = xb * rstd
        dy = dhb * gb
        dx = rstd * (dy - y * jnp.mean(dy * y, axis=-1, keepdims=True))
        return drb + dx, jnp.sum(dhb * y, axis=0, keepdims=True)

    return _ew(fn, [(dh, *_rc(tm, D)), (x, *_rc(tm, D)), (g, *_col(1, D)), (dres, *_rc(tm, D))],
               [(SDS((L, D), F32), *_rc(tm, D), None), (SDS((1, D), F32), *_col(1, D), "all")], (1, L // tm), name,
               vmem_mb=56)


def _final_loss(x2, g, tgt, name, tm=512):
    L = x2.shape[0]

    def fn(c, r, xb, gb, tb):
        rstd = lax.rsqrt(jnp.mean(xb * xb, axis=-1, keepdims=True) + RMS_EPS)
        y = xb * rstd
        err = y * gb - tb
        dout = err * (1.0 / D)
        dy = dout * gb
        dx = rstd * (dy - y * jnp.mean(dy * y, axis=-1, keepdims=True))
        return dx, dx, jnp.sum(err * err, axis=0, keepdims=True) * (0.5 / D), jnp.sum(dout * y, axis=0, keepdims=True)

    return _ew(fn, [(x2, *_rc(tm, D)), (g, *_col(1, D)), (tgt, *_rc(tm, D))],
               [(SDS((L, D), F32), *_rc(tm, D), None), (SDS((L, D), BF16), *_rc(tm, D), None),
                (SDS((1, D), F32), *_col(1, D), "all"),
                (SDS((1, D), F32), *_col(1, D), "all")], (1, L // tm), name, vmem_mb=56)


def _slope(h):
    return 2.0 ** (-8.0 * (h + 1) / NQ)


def _attn_bias():
    qi = lax.broadcasted_iota(jnp.int32, (BLK, 2 * BLK), 0)
    si = lax.broadcasted_iota(jnp.int32, (BLK, 2 * BLK), 1)
    dist = qi + BLK - si
    band = (dist >= 0) & (dist < BLK)
    slopes = jnp.asarray([_slope(h) for h in range(NQ)], F32)[:, None, None]
    alibi = -slopes * dist.astype(F32)[None]
    return jnp.stack([jnp.where((band & (si >= BLK))[None], alibi, NEG_BIG), jnp.where(band[None], alibi, NEG_BIG)])


def _attn_kv(kvc, kvp):
    kv = jnp.concatenate([kvp, kvc], axis=0).astype(F32)
    lo = lax.broadcasted_iota(jnp.int32, (2 * BLK, 128), 1) < HD

    def halves(t):
        tr = pltpu.roll(t, HD, 1)
        z = jnp.zeros_like(t)
        return {(0, 0): jnp.where(lo, t, z).astype(BF16), (0, 1): jnp.where(lo, z, tr).astype(BF16),
                (1, 0): jnp.where(lo, tr, z).astype(BF16), (1, 1): jnp.where(lo, z, t).astype(BF16)}

    return halves(kv[:, :128]), halves(kv[:, 128:])


_NT = (((1,), (1,)), ((), ()))
_TN = (((0,), (0,)), ((), ()))
_ATTN_SPECS = [pl.BlockSpec(memory_space=pltpu.SMEM),
               pl.BlockSpec((None, NQ, BLK, 2 * BLK), lambda n: (jnp.minimum(n, 1), 0, 0, 0)),
               pl.BlockSpec((BLK, AW), lambda n: (n, 0)),
               pl.BlockSpec((BLK, 256), lambda n: (n, C_K // 256)),
               pl.BlockSpec((BLK, 256), lambda n: (jnp.maximum(n - 1, 0), C_K // 256))]


def _attn_scores(q_ref, bias_ref, kmat, sc_ref):
    for j in range(NQ // 2):
        qs = q_ref[:, 128 * j:128 * (j + 1)] * (HD ** -0.5)
        for e in range(2):
            h = 2 * j + e
            sc_ref[h] = lax.dot_general(qs, kmat[(j // (NQ // 4), e)], _NT, preferred_element_type=F32) + bias_ref[h]


def _softmax_with_sink(s, sink):
    m = jnp.maximum(jnp.max(s, axis=-1, keepdims=True), sink)
    p = jnp.exp(s - m)
    esink = jnp.exp(sink - m)
    den = jnp.sum(p, axis=-1, keepdims=True) + esink
    return p / den, esink / den


def _attn_fwd(projb, sinks, bias, name):
    L = projb.shape[0]

    def body(s_ref, bias_ref, q_ref, kvc_ref, kvp_ref, o_ref, sc_ref, pr_ref):
        kmat, vmat = _attn_kv(kvc_ref[...], kvp_ref[...])
        _attn_scores(q_ref, bias_ref, kmat, sc_ref)
        for h in range(NQ):
            pr_ref[h] = _softmax_with_sink(sc_ref[h], s_ref[0, h])[0].astype(BF16)
        for j in range(NQ // 2):
            g = j // (NQ // 4)
            acc = jnp.dot(pr_ref[2 * j], vmat[(g, 0)], preferred_element_type=F32)
            acc = acc + jnp.dot(pr_ref[2 * j + 1], vmat[(g, 1)], preferred_element_type=F32)
            o_ref[:, 128 * j:128 * (j + 1)] = acc.astype(BF16)

    return pl.pallas_call(
        body, out_shape=SDS((L, AW), BF16), grid=(L // BLK,), in_specs=_ATTN_SPECS,
        out_specs=pl.BlockSpec((BLK, AW), lambda n: (n, 0)), name=name,
        scratch_shapes=[pltpu.VMEM((NQ, BLK, 2 * BLK), F32), pltpu.VMEM((NQ, BLK, 2 * BLK), BF16)],
        compiler_params=_cparams(("arbitrary",), 32),
    )(sinks, bias, projb, projb, projb)


def _attn_bwd(projb, sinks, bias, dattn, name):
    L = projb.shape[0]

    def body(s_ref, bias_ref, q_ref, kvc_ref, kvp_ref, do_ref, dq_ref, dcur_ref, dprev_ref, dsink_ref,
             sc_ref, dp_ref, ds_ref, pr_ref, qm_ref, dm_ref):
        n = pl.program_id(0)
        kmat, vmat = _attn_kv(kvc_ref[...], kvp_ref[...])
        _attn_scores(q_ref, bias_ref, kmat, sc_ref)
        for h in range(NQ):
            j, e = h // 2, h % 2
            dp_ref[h] = lax.dot_general(do_ref[:, 128 * j:128 * (j + 1)], vmat[(j // (NQ // 4), e)], _NT,
                                        preferred_element_type=F32)
        lane = lax.broadcasted_iota(jnp.int32, (1, 128), 1)
        dsv = jnp.zeros((1, 128), F32)
        for h in range(NQ):
            p, psink = _softmax_with_sink(sc_ref[h], s_ref[0, h])
            dp = dp_ref[h]
            drow = jnp.sum(p * dp, axis=-1, keepdims=True)
            ds_ref[h] = (p * (dp - drow)).astype(BF16)
            pr_ref[h] = p.astype(BF16)
            dsv = dsv + jnp.where(lane == h, -jnp.sum(psink * drow, axis=0, keepdims=True), 0.0)
        lo128 = lax.broadcasted_iota(jnp.int32, (BLK, 128), 1) < HD
        for j in range(NQ // 2):
            g = j // (NQ // 4)
            qs = q_ref[:, 128 * j:128 * (j + 1)] * (HD ** -0.5)
            dop = do_ref[:, 128 * j:128 * (j + 1)]
            zb = jnp.zeros_like(qs)
            dqp = jnp.zeros((BLK, 128), F32)
            for e in range(2):
                h = 2 * j + e
                half = lo128 if e == 0 else jnp.logical_not(lo128)
                dqp = dqp + jnp.dot(ds_ref[h], kmat[(g, e)], preferred_element_type=F32)
                qm_ref[h] = jnp.where(half, qs, zb)
                dm_ref[h] = jnp.where(half, dop, zb)
            dq_ref[:, 128 * j:128 * (j + 1)] = (dqp * (HD ** -0.5)).astype(BF16)
        hk = NQ // NKV
        rows = lambda ref, g: ref[g * hk:(g + 1) * hk].reshape(hk * BLK, ref.shape[-1])
        dk = [lax.dot_general(rows(ds_ref, g), rows(qm_ref, g), _TN, preferred_element_type=F32) for g in range(NKV)]
        dv = [lax.dot_general(rows(pr_ref, g), rows(dm_ref, g), _TN, preferred_element_type=F32) for g in range(NKV)]
        lo256 = lax.broadcasted_iota(jnp.int32, (2 * BLK, 128), 1) < HD
        tot = [t + pltpu.roll(t, HD, 1) for t in (dk[0], dk[1], dv[0], dv[1])]
        dkv = jnp.concatenate([jnp.where(lo256, tot[0], tot[1]), jnp.where(lo256, tot[2], tot[3])], axis=1)
        dprev_ref[...] = dkv[:BLK]
        dcur_ref[...] = dkv[BLK:]

        @pl.when(n == 0)
        def _():
            dsink_ref[...] = dsv

        @pl.when(n > 0)
        def _():
            dsink_ref[...] += dsv

    tile = (NQ, BLK, 2 * BLK)
    return pl.pallas_call(
        body, out_shape=(SDS((L, AW), BF16), SDS((L, 256), F32), SDS((L, 256), F32), SDS((1, 128), F32)), grid=(L // BLK,),
        in_specs=_ATTN_SPECS + [pl.BlockSpec((BLK, AW), lambda n: (n, 0))],
        out_specs=(pl.BlockSpec((BLK, AW), lambda n: (n, 0)), pl.BlockSpec((BLK, 256), lambda n: (n, 0)),
                   pl.BlockSpec((BLK, 256), lambda n: (n, 0)), pl.BlockSpec((1, 128), lambda n: (0, 0))),
        scratch_shapes=[pltpu.VMEM(tile, F32), pltpu.VMEM(tile, F32), pltpu.VMEM(tile, BF16), pltpu.VMEM(tile, BF16),
                        pltpu.VMEM((NQ, BLK, 128), BF16), pltpu.VMEM((NQ, BLK, 128), BF16)],
        name=name, compiler_params=_cparams(("arbitrary",), 40),
    )(sinks, bias, projb, projb, projb, dattn)


def _disc(a_re, a_im, logdt, b_re, b_im):
    dt = jnp.exp(logdt)
    mag = jnp.exp(a_re * dt)
    ab_re = mag * jnp.cos(a_im * dt)
    ab_im = mag * jnp.sin(a_im * dt)
    nr = ab_re - 1.0
    ni = ab_im
    den = a_re * a_re + a_im * a_im
    z_re = (nr * a_re + ni * a_im) / den
    z_im = (ni * a_re - nr * a_im) / den
    return ab_re, ab_im, z_re * b_re - z_im * b_im, z_re * b_im + z_im * b_re


def _group_mask():
    row = lax.broadcasted_iota(jnp.int32, (SW, NS), 0) // H
    col = lax.broadcasted_iota(jnp.int32, (SW, NS), 1) // P
    return row == col


def _block_diag(re, im):
    mask = _group_mask()
    z = jnp.zeros((SW, NS), F32)
    return jnp.concatenate([jnp.where(mask, jnp.tile(re, (G, 1)), z), jnp.where(mask, jnp.tile(im, (G, 1)), z)], axis=1)


def _block_diag_t(big):
    mask = _group_mask()
    z = jnp.zeros((SW, NS), F32)
    re = jnp.sum(jnp.where(mask, big[:, :NS], z).reshape(G, H, NS), axis=0)
    im = jnp.sum(jnp.where(mask, big[:, NS:], z).reshape(G, H, NS), axis=0)
    return re, im


def _ssm_prep(a_re, a_im, logdt, b_re, b_im, c_re, c_im, name):
    def body(are, aim, ldt, bre, bim, cre, cim, ab_ref, bm_ref, cm_ref):
        ab_re, ab_im, bb_re, bb_im = _disc(are[...], aim[...], ldt[...], bre[...], bim[...])
        ab_ref[...] = jnp.concatenate([ab_re, ab_im], axis=1)
        bm_ref[...] = _block_diag(bb_re, bb_im).astype(BF16)
        cm_ref[...] = _block_diag(cre[...], -cim[...]).astype(BF16)

    return pl.pallas_call(body, out_shape=(SDS((1, 2 * NS), F32), SDS((SW, 2 * NS), BF16), SDS((SW, 2 * NS), BF16)),
                          name=name, compiler_params=pltpu.CompilerParams(vmem_limit_bytes=48 << 20),
                          )(a_re, a_im, logdt, b_re, b_im, c_re, c_im)


def _ssm_param_bwd(a_re, a_im, logdt, b_re, b_im, dab8, dbm, dcm, name):
    def body(are, aim, ldt, bre, bim, dab_ref, dbm_ref, dcm_ref, o_are, o_aim, o_ldt, o_bre, o_bim, o_cre, o_cim):
        dab = jnp.sum(dab_ref[...], axis=0, keepdims=True)
        dbb_re, dbb_im = _block_diag_t(dbm_ref[...])
        _, vjp = jax.vjp(_disc, are[...], aim[...], ldt[...], bre[...], bim[...])
        d_are, d_aim, d_ldt, d_bre, d_bim = vjp((dab[:, :NS], dab[:, NS:], dbb_re, dbb_im))
        o_are[...], o_aim[...], o_ldt[...], o_bre[...], o_bim[...] = d_are, d_aim, d_ldt, d_bre, d_bim
        dc_re, dc_imn = _block_diag_t(dcm_ref[...])
        o_cre[...] = dc_re
        o_cim[...] = -dc_imn

    v1, vh = SDS((1, NS), F32), SDS((H, NS), F32)
    return pl.pallas_call(body, out_shape=(v1, v1, v1, vh, vh, vh, vh), name=name,
                          compiler_params=pltpu.CompilerParams(vmem_limit_bytes=56 << 20),
                          )(a_re, a_im, logdt, b_re, b_im, dab8, dbm, dcm)


def _ssm_scan(src, wmat, ab, *, reverse, ends=None, xs=None, init=None, wproj=None, name, tk=32):
    L = src.shape[0]
    seg_len = L // NSEG
    tk = min(tk, seg_len)
    rows = NSEG * tk
    nch = L // rows
    n_sq = int(math.log2(seg_len))
    assert 2 ** n_sq == seg_len and L % rows == 0
    first_pass = ends is None
    with_dab = (not first_pass) and reverse
    with_proj = wproj is not None
    assert not (with_proj and first_pass)
    slab = 512
    n_slab = NS // slab

    def body(*refs):
        src_ref, w_ref, ab_ref = refs[:3]
        pos = 3
        if not first_pass:
            ends_ref = refs[pos]
            pos += 1
        if with_dab:
            xs_ref, xsh_ref, init_ref = refs[pos:pos + 3]
            pos += 3
        if with_proj:
            wproj_ref = refs[pos]
            pos += 1
        if first_pass:
            (e_ref,) = refs[pos:pos + 1]
            pos += 1
        else:
            st_out_ref, aux_ref = refs[pos:pos + 2]
            pos += 2
        if with_proj:
            proj_ref = refs[pos]
            pos += 1
        buf_ref, st_ref = refs[pos:pos + 2]
        i = pl.program_id(0)
        a_re = ab_ref[:, :NS]
        a_im = -ab_ref[:, NS:] if reverse else ab_ref[:, NS:]

        @pl.when(i == 0)
        def _():
            if first_pass:
                st_ref[...] = jnp.zeros_like(st_ref)
            else:
                pr, pi = a_re, a_im
                for _ in range(n_sq):
                    pr, pi = pr * pr - pi * pi, 2.0 * pr * pi
                zr = jnp.zeros((1, NS), F32)
                cr, ci = zr, zr
                order = list(range(NSEG - 1, -1, -1)) if reverse else list(range(NSEG))
                st_ref[order[0]:order[0] + 1, :] = jnp.zeros((1, 2 * NS), F32)
                for jprev, j in zip(order[:-1], order[1:]):
                    er, ei = ends_ref[jprev:jprev + 1, :NS], ends_ref[jprev:jprev + 1, NS:]
                    cr, ci = er + pr * cr - pi * ci, ei + pr * ci + pi * cr
                    st_ref[j:j + 1, :NS] = cr
                    st_ref[j:j + 1, NS:] = ci
                if not reverse:
                    aux_ref[...] = st_ref[...]
                else:
                    aux_ref[...] = jnp.zeros_like(aux_ref)

        buf_ref[...] = jnp.dot(src_ref[...].astype(BF16), w_ref[...], preferred_element_type=F32)

        for s in range(n_slab):
            re_sl, im_sl = pl.ds(s * slab, slab), pl.ds(NS + s * slab, slab)
            ar = jnp.broadcast_to(a_re[:, s * slab:(s + 1) * slab], (NSEG, slab))
            ai = jnp.broadcast_to(a_im[:, s * slab:(s + 1) * slab], (NSEG, slab))

            def step(t, carry, re_sl=re_sl, im_sl=im_sl, ar=ar, ai=ai):
                k = (tk - 1 - t) if reverse else t
                r0 = pl.multiple_of(k * NSEG, NSEG)
                xr, xi = carry[0], carry[1]
                nr = ar * xr - ai * xi + buf_ref[pl.ds(r0, NSEG), re_sl]
                ni = ar * xi + ai * xr + buf_ref[pl.ds(r0, NSEG), im_sl]
                if not first_pass:
                    buf_ref[pl.ds(r0, NSEG), re_sl] = nr
                    buf_ref[pl.ds(r0, NSEG), im_sl] = ni
                if not with_dab:
                    return nr, ni
                rp = pl.multiple_of((k - 1) * NSEG, NSEG)
                xpr, xpi = xs_ref[pl.ds(rp, NSEG), re_sl], xs_ref[pl.ds(rp, NSEG), im_sl]
                return nr, ni, carry[2] + nr * xpr + ni * xpi, carry[3] + ni * xpr - nr * xpi

            carry = (st_ref[:, re_sl], st_ref[:, im_sl])
            if with_dab:
                z = jnp.zeros((NSEG, slab), F32)
                carry = lax.fori_loop(0, tk - 1, step, carry + (z, z), unroll=True)
                xr, xi, dr, di = carry
                nr = ar * xr - ai * xi + buf_ref[pl.ds(0, NSEG), re_sl]
                ni = ar * xi + ai * xr + buf_ref[pl.ds(0, NSEG), im_sl]
                buf_ref[pl.ds(0, NSEG), re_sl] = nr
                buf_ref[pl.ds(0, NSEG), im_sl] = ni
                at_start = i == nch - 1
                xpr = jnp.where(at_start, init_ref[:, re_sl], xsh_ref[:, re_sl])
                xpi = jnp.where(at_start, init_ref[:, im_sl], xsh_ref[:, im_sl])
                aux_ref[:, re_sl] += dr + nr * xpr + ni * xpi
                aux_ref[:, im_sl] += di + ni * xpr - nr * xpi
                carry = (nr, ni)
            else:
                carry = lax.fori_loop(0, tk, step, carry, unroll=True)
            st_ref[:, re_sl] = carry[0]
            st_ref[:, im_sl] = carry[1]

        if first_pass:
            @pl.when(i == nch - 1)
            def _():
                e_ref[...] = st_ref[...]
        else:
            st_out_ref[...] = buf_ref[...].astype(st_out_ref.dtype)
            if with_proj:
                proj_ref[...] = lax.dot_general(buf_ref[...].astype(BF16), wproj_ref[...], _NT, preferred_element_type=F32)

    chunk = (lambda i: (nch - 1 - i, 0)) if reverse else (lambda i: (i, 0))
    whole = lambda i: (0, 0)
    ins = [src, wmat, ab]
    once = pl.Buffered(1)
    in_specs = [pl.BlockSpec((rows, SW), chunk), pl.BlockSpec((SW, 2 * NS), whole, pipeline_mode=once),
                pl.BlockSpec((1, 2 * NS), whole)]
    small = SDS((NSEG, 2 * NS), F32)
    small_spec = pl.BlockSpec((NSEG, 2 * NS), whole)
    if not first_pass:
        ins.append(ends)
        in_specs.append(small_spec)
    if with_dab:
        ins += [xs, xs, init]
        in_specs += [pl.BlockSpec((rows, 2 * NS), chunk),
                     pl.BlockSpec((NSEG, 2 * NS), lambda i: (jnp.maximum((nch - 1 - i) * tk - 1, 0), 0)),
                     small_spec]
    if with_proj:
        ins.append(wproj)
        in_specs.append(pl.BlockSpec((SW, 2 * NS), whole, pipeline_mode=once))
    if first_pass:
        out_shape, out_specs = small, small_spec
    else:
        out_shape = (SDS((L, 2 * NS), BF16 if reverse else F32), small)
        out_specs = (pl.BlockSpec((rows, 2 * NS), chunk), small_spec)
        if with_proj:
            out_shape += (SDS((L, SW), F32),)
            out_specs += (pl.BlockSpec((rows, SW), chunk),)
    return pl.pallas_call(
        body, out_shape=out_shape, grid=(nch,), in_specs=in_specs, out_specs=out_specs,
        scratch_shapes=[pltpu.VMEM((rows, 2 * NS), F32), pltpu.VMEM((NSEG, 2 * NS), F32)], name=name,
        compiler_params=_cparams(("arbitrary",), 56),
    )(*ins)


def _to_segments(a):
    L, c = a.shape
    return a.reshape(NSEG, L // NSEG, c).transpose(1, 0, 2).reshape(L, c)


def _from_segments(a):
    L, c = a.shape
    return a.reshape(L // NSEG, NSEG, c).transpose(1, 0, 2).reshape(L, c)


def _peer(x, y, c, m):
    return ((1 - x) if (m >> 2) & 1 else x, (1 - y) if (m >> 1) & 1 else y, (1 - c) if m & 1 else c)


def _dev_index(p):
    return 4 * p[0] + 2 * p[1] + p[2]


def _exchange(arrs, scatter, name):
    n = len(arrs)

    def body(*refs):
        ins, outs = refs[:n], refs[n:2 * n]
        send_sems, recv_sems, loc_sems = refs[2 * n:]
        x, y, c = lax.axis_index("x"), lax.axis_index("y"), lax.axis_index("c")
        me = _dev_index((x, y, c))

        def src(w, to):
            return ins[w].at[to] if scatter else ins[w]

        def local(w):
            return pltpu.make_async_copy(src(w, me), outs[w].at[me], loc_sems.at[w])

        def remote(w, m):
            peer = _peer(x, y, c, m)
            return pltpu.make_async_remote_copy(src_ref=src(w, _dev_index(peer)), dst_ref=outs[w].at[me],
                                                send_sem=send_sems.at[w, m - 1], recv_sem=recv_sems.at[w, m - 1],
                                                device_id=peer, device_id_type=pl.DeviceIdType.MESH)

        def arrival(w, m):
            peer = _peer(x, y, c, m)
            return pltpu.make_async_remote_copy(src_ref=src(w, me), dst_ref=outs[w].at[_dev_index(peer)],
                                                send_sem=send_sems.at[w, m - 1], recv_sem=recv_sems.at[w, m - 1],
                                                device_id=peer, device_id_type=pl.DeviceIdType.MESH)

        for w in range(n):
            local(w).start()
        for w in range(n):
            for m in range(1, N_DEV):
                remote(w, m).start()
        for w in range(n):
            for m in range(1, N_DEV):
                arrival(w, m).wait_recv()
        for w in range(n):
            for m in range(1, N_DEV):
                remote(w, m).wait_send()
        for w in range(n):
            local(w).wait()

    anyspec = pl.BlockSpec(memory_space=pl.ANY)
    out_shape = tuple(SDS(a.shape if scatter else (N_DEV,) + a.shape, a.dtype) for a in arrs)
    return pl.pallas_call(
        body, out_shape=out_shape, in_specs=[anyspec] * n, out_specs=tuple([anyspec] * n),
        scratch_shapes=[pltpu.SemaphoreType.DMA((n, N_DEV - 1)), pltpu.SemaphoreType.DMA((n, N_DEV - 1)),
                        pltpu.SemaphoreType.DMA((n,))],
        name=name, compiler_params=pltpu.CompilerParams(has_side_effects=True),
    )(*arrs)


_HBM = pl.BlockSpec(memory_space=pltpu.HBM)
_SEM = pl.BlockSpec(memory_space=pltpu.SEMAPHORE)
_EFFECT = pltpu.SideEffectType.DATAFLOW_SIDE_EFFECTING


def _sem_index(w, m):
    return w * (N_DEV - 1) + m - 1


_ALL_MASKS = tuple(range(1, N_DEV))
_CHIP_MASKS = (2, 4, 6)
_FIRST_HOP_MASKS = (1,) + _CHIP_MASKS


def _exchange_start(arrs, scatter, name, masks=_ALL_MASKS):
    n = len(arrs)
    lands = [lax.empty(a.shape if scatter else (N_DEV,) + a.shape, a.dtype) for a in arrs]

    def body(*refs):
        ins, zones = refs[:n], refs[n:2 * n]
        send_sems, recv_sems = refs[2 * n], refs[2 * n + 1]
        token = refs[-1]
        x, y, c = lax.axis_index("x"), lax.axis_index("y"), lax.axis_index("c")
        me = _dev_index((x, y, c))
        for w in range(n):
            for m in masks:
                peer = _peer(x, y, c, m)
                pltpu.make_async_remote_copy(
                    src_ref=ins[w].at[_dev_index(peer)] if scatter else ins[w], dst_ref=zones[w].at[me],
                    send_sem=send_sems.at[_sem_index(w, m)], recv_sem=recv_sems.at[_sem_index(w, m)],
                    device_id=peer, device_id_type=pl.DeviceIdType.MESH).start()
        token[...] = jnp.zeros_like(token)

    sems = pltpu.SemaphoreType.DMA((n * (N_DEV - 1),))
    res = pl.pallas_call(
        body, name=name,
        out_shape=(sems, sems, *[pltpu.HBM(a.shape, a.dtype) for a in arrs], *[pltpu.HBM(z.shape, z.dtype) for z in lands],
                   SDS((8, 128), F32)),
        in_specs=[_HBM] * (2 * n), out_specs=(_SEM, _SEM, *([_HBM] * (2 * n)), pl.BlockSpec(memory_space=pltpu.VMEM)),
        input_output_aliases={i: 2 + i for i in range(2 * n)},
        compiler_params=pltpu.CompilerParams(has_side_effects=_EFFECT),
    )(*[pltpu.with_memory_space_constraint(a, pltpu.HBM) for a in arrs],
      *[pltpu.with_memory_space_constraint(z, pltpu.HBM) for z in lands])
    return (res[0], res[1], list(res[2:2 + n]), list(res[2 + n:2 + 2 * n])), res[-1]


def _exchange_wait(handle, after, scatter, name, masks=_ALL_MASKS):
    send_sems, recv_sems, thru, lands = handle
    n = len(thru)

    def body(*refs):
        ins, zones = refs[:n], refs[n:2 * n]
        send_sems, recv_sems = refs[2 * n], refs[2 * n + 1]
        x, y, c = lax.axis_index("x"), lax.axis_index("y"), lax.axis_index("c")
        me = _dev_index((x, y, c))
        for w in range(n):
            for m in masks:
                peer = _peer(x, y, c, m)
                copy = pltpu.make_async_remote_copy(
                    src_ref=ins[w].at[me] if scatter else ins[w], dst_ref=zones[w].at[_dev_index(peer)],
                    send_sem=send_sems.at[_sem_index(w, m)], recv_sem=recv_sems.at[_sem_index(w, m)],
                    device_id=peer, device_id_type=pl.DeviceIdType.MESH)
                copy.wait_send()
                copy.wait_recv()

    res = pl.pallas_call(
        body, name=name,
        out_shape=(*[pltpu.HBM(a.shape, a.dtype) for a in thru], *[pltpu.HBM(z.shape, z.dtype) for z in lands]),
        in_specs=[_HBM] * (2 * n) + [_SEM, _SEM, pl.BlockSpec(memory_space=pl.ANY)], out_specs=tuple([_HBM] * (2 * n)),
        input_output_aliases={i: i for i in range(2 * n)},
        compiler_params=pltpu.CompilerParams(has_side_effects=_EFFECT),
    )(*thru, *lands, send_sems, recv_sems, after)
    return list(res[:n]), list(res[n:])


def _forward_start(zones, name):
    n = len(zones)

    def body(*refs):
        zs = refs[:n]
        send_sems, recv_sems = refs[n], refs[n + 1]
        token = refs[-1]
        x, y, c = lax.axis_index("x"), lax.axis_index("y"), lax.axis_index("c")
        for w in range(n):
            for m in _CHIP_MASKS:
                slot = zs[w].at[_dev_index(_peer(x, y, c, m))]
                pltpu.make_async_remote_copy(
                    src_ref=slot, dst_ref=slot, send_sem=send_sems.at[_sem_index(w, m)],
                    recv_sem=recv_sems.at[_sem_index(w, m)], device_id=(x, y, 1 - c),
                    device_id_type=pl.DeviceIdType.MESH).start()
        token[...] = jnp.zeros_like(token)

    sems = pltpu.SemaphoreType.DMA((n * (N_DEV - 1),))
    res = pl.pallas_call(
        body, name=name, out_shape=(sems, sems, *[pltpu.HBM(z.shape, z.dtype) for z in zones], SDS((8, 128), F32)),
        in_specs=[_HBM] * n, out_specs=(_SEM, _SEM, *([_HBM] * n), pl.BlockSpec(memory_space=pltpu.VMEM)),
        input_output_aliases={i: 2 + i for i in range(n)},
        compiler_params=pltpu.CompilerParams(has_side_effects=_EFFECT),
    )(*[pltpu.with_memory_space_constraint(z, pltpu.HBM) for z in zones])
    return (res[0], res[1], list(res[2:2 + n])), res[-1]


def _forward_wait(handle, after, name):
    send_sems, recv_sems, zones = handle
    n = len(zones)

    def body(*refs):
        zs = refs[:n]
        send_sems, recv_sems = refs[n], refs[n + 1]
        x, y, c = lax.axis_index("x"), lax.axis_index("y"), lax.axis_index("c")
        for w in range(n):
            for m in _CHIP_MASKS:
                copy = pltpu.make_async_remote_copy(
                    src_ref=zs[w].at[_dev_index(_peer(x, y, c, m))], dst_ref=zs[w].at[_dev_index(_peer(x, y, 1 - c, m))],
                    send_sem=send_sems.at[_sem_index(w, m)], recv_sem=recv_sems.at[_sem_index(w, m)],
                    device_id=(x, y, 1 - c), device_id_type=pl.DeviceIdType.MESH)
                copy.wait_send()
                copy.wait_recv()

    res = pl.pallas_call(
        body, name=name, out_shape=tuple(pltpu.HBM(z.shape, z.dtype) for z in zones),
        in_specs=[_HBM] * n + [_SEM, _SEM, pl.BlockSpec(memory_space=pl.ANY)], out_specs=tuple([_HBM] * n),
        input_output_aliases={i: i for i in range(n)},
        compiler_params=pltpu.CompilerParams(has_side_effects=_EFFECT),
    )(*zones, send_sems, recv_sems, after)
    return list(res)


def _adam_math(g, w, m, v):
    m = ADAM_B1 * m + (1.0 - ADAM_B1) * g
    v = ADAM_B2 * v + (1.0 - ADAM_B2) * (g * g)
    m_hat = m / (1.0 - ADAM_B1 ** ADAM_STEP)
    v_hat = v / (1.0 - ADAM_B2 ** ADAM_STEP)
    delta = -ADAM_LR * (m_hat / (jnp.sqrt(v_hat) + ADAM_EPS) + ADAM_WD * w)
    return delta, m, v


def _adam(parts, w, m, v, name, tr=128):
    r, c = w.shape
    tr = next(t for t in (tr, 64, 32, 16, 8) if r % t == 0)

    def fn(cc, rr, pb, wb, mb, vb):
        g = pb[0].astype(F32)
        for d in range(1, N_DEV):
            g = g + pb[d].astype(F32)
        delta, nm, nv = _adam_math(g, wb, mb, vb)
        return g, delta, nm, nv

    blk = ((tr, c), lambda cc, rr: (rr, 0))
    o = SDS((r, c), F32)
    return _ew(fn, [(parts, (N_DEV, tr, c), lambda cc, rr: (0, rr, 0)), (w, *blk), (m, *blk), (v, *blk)],
               [(o, *blk, None)] * 4, (1, r // tr), name)


_SHARDED = ("w_in", "w_glu", "w_branch_attn", "w_branch_ssm", "w_out", "w_up", "w_down")
_COL_SHARDED = ("w_in", "w_glu", "w_branch_attn", "w_branch_ssm", "w_up")
_GROUPS = {"a": ("w_in",), "b": ("w_glu", "w_branch_attn", "w_branch_ssm", "w_out"), "c": ("w_up", "w_down")}
_SMALL = ("attn_norm_g", "b_in", "attn_sinks", "ssm_a_re", "ssm_a_im", "ssm_log_dt", "ssm_b_re", "ssm_b_im",
          "ssm_c_re", "ssm_c_im", "ssm_d", "b_glu", "ffn_norm_g", "conv_w", "conv_b", "final_norm_g")
_WEIGHTS = ("attn_norm_g", "w_in", "b_in", "attn_sinks", "ssm_a_re", "ssm_a_im", "ssm_log_dt", "ssm_b_re", "ssm_b_im",
            "ssm_c_re", "ssm_c_im", "ssm_d", "w_glu", "b_glu", "w_branch_attn", "w_branch_ssm", "w_out", "ffn_norm_g",
            "w_up", "conv_w", "conv_b", "w_down", "final_norm_g")


def _unstack_cols(g):
    return g.transpose(1, 0, 2).reshape(g.shape[1], g.shape[0] * g.shape[2])


def _stack_cols(a, d=N_DEV):
    k, n = a.shape
    return a.reshape(k, d, n // d).transpose(1, 0, 2)


def _pack(arrs):
    flat = jnp.concatenate([a.reshape(-1) for a in arrs])
    pad = (-flat.shape[0]) % 1024
    return jnp.pad(flat, (0, pad)).reshape(-1, 128)


def _local_step(x, tgt, wget, small, gput):
    L = x.shape[0]
    nr = lambda tm: L // tm

    h = _rmsnorm_fwd(x, small["attn_norm_g"], "norm1")
    wts = dict(wget("a", h))
    projb = _mm(h, wts["w_in"], bias=small["b_in_p"], out_dtype=BF16, name="proj")
    proj = projb
    attn_bias = _attn_bias()
    attn = _attn_fwd(projb, small["attn_sinks"], attn_bias, "attn_fwd")

    ab, bmat, cmat = _ssm_prep(small["a_re"], small["a_im"], small["logdt"], small["b_re"], small["b_im"],
                               small["c_re"], small["c_im"], "ssm_prep")
    u_seg = _to_segments(proj[:, C_U:C_PAD])
    ends_f = _ssm_scan(u_seg, bmat, ab, reverse=False, tk=256, name="ssm_ends_fwd")
    xs, init_f, y_seg = _ssm_scan(u_seg, bmat, ab, reverse=False, ends=ends_f, wproj=cmat, tk=64, name="ssm_scan_fwd")
    y_mm = _from_segments(y_seg)

    def gelu_fn(c, r, yb, ub, db):
        yv = yb + db * ub
        return yv, _gelu(yv)

    tm = 512
    y, gy = _ew(gelu_fn, [(y_mm, *_rc(tm, 256)), (proj, *_rc(tm, 256, C_U // 256)), (small["ssm_d"], *_col(1, 256))],
                [(SDS((L, SW), F32), *_rc(tm, 256), None), (SDS((L, SW), BF16), *_rc(tm, 256), None)],
                (2, nr(tm)), "ssm_gelu")
    wts.update(wget("b", gy))
    def glu_ep(i, cols, rg, rv, bg_ref, bv_ref):
        val, sg = rv + bv_ref[...], _sigmoid(rg + bg_ref[...])
        return val * sg, val, sg

    ssm, glu_val, glu_sig = _mm(gy, wts["w_glu"], b2=wts["w_glu"], n_cols=SW, b_col_off=1, tn=SW, epilogue=glu_ep,
                                extras=[(small["b_glu"], "col", 1), (small["b_glu"], "col", 0)],
                                outs=[(SDS((L, SW), BF16), "tile"), (SDS((L, SW), F32), "tile"), (SDS((L, SW), F32), "tile")],
                                name="glu_gate")
    f32 = lambda ref, cols: ref[:, cols].astype(F32)
    tnm = 1024
    gate_tiles = [(projb, "tile", C_GA // tnm), (projb, "tile", C_GS // tnm)]

    def merge_ep(i, cols, ra, rs, ga, gs):
        sa, ss = _sigmoid(f32(ga, cols)), _sigmoid(f32(gs, cols))
        return sa * ra + ss * rs, ra, rs, sa, ss

    merged, br_a, br_s, sig_a, sig_s = _mm(attn, wts["w_branch_attn"], a2=ssm, b2=wts["w_branch_ssm"], tm=512, tn=tnm,
                                           extras=gate_tiles, epilogue=merge_ep,
                                           outs=[(SDS((L, D), BF16), "tile")] * 5, name="branch_merge")
    def norm2_ep(i, cols, r, _, g_ref):
        rstd = lax.rsqrt(jnp.mean(r * r, axis=-1, keepdims=True) + RMS_EPS)
        return r, (r * rstd) * g_ref[...]

    x1, h2 = _mm(merged, wts["w_out"], res=x, tm=512, tn=D, epilogue=norm2_ep, extras=[(small["ffn_norm_g"], "col", 0)],
                 outs=[(SDS((L, D), F32), "tile"), (SDS((L, D), BF16), "tile")], name="out_proj_norm2")
    wts.update(wget("c", h2))
    conv_w = wts["conv_w"]
    w_up = wts["w_up"]
    tcf = 1408
    tma = 256
    hb = 16

    def conv_gate(first, gate, halo, cw, cb):
        halo = halo * jnp.logical_not(first).astype(F32)
        g1, g2 = _shift_rows(gate, halo, 1), _shift_rows(gate, halo, 2)
        return cb + cw[2:3] * gate + cw[1:2] * g1 + cw[0:1] * g2, g1, g2

    tmu, tnu = 1024, 512

    def up_ep(i, cols, rg, rv, h2_halo, wg, cw, cb):
        halo = jnp.dot(h2_halo[...], wg[:, cols], preferred_element_type=F32)[hb - 8:]
        gl, glg = _gelu_and_grad(conv_gate(i == 0, rg, halo, cw[:, cols], cb[:, cols])[0])
        return rg, rv * gl, gl, rv * glg

    up_g, act, gelu_cg, val_gelu_grad = _mm(
        h2, w_up, b2=w_up, n_cols=DFF, b_col_off=DFF // tnu, tm=tmu, tn=tnu, epilogue=up_ep,
        outs=[(SDS((L, DFF), BF16), "tile")] * 4, name="ffn_up_act",
        extras=[(h2, "spec", ((hb, D), lambda j, i: (jnp.maximum(i * (tmu // hb) - 1, 0), 0))),
                (w_up, "spec", ((D, tnu), lambda j, i: (0, j + DFF // tnu))), (conv_w, "col", 0),
                (small["conv_b"], "col", 0)])
    x2 = _mm(act, wts["w_down"], res=x1, name="ffn_down")
    d_x2, d_x2b, loss_cols, d_gf = _final_loss(x2, small["final_norm_g"], tgt, "final_loss")
    loss = jnp.sum(loss_cols)

    dw_down = _mm(act, d_x2b, ta=True, out_dtype=BF16, tm=tcf, tk=2048, name="dw_down")
    tmd, tnd = 1024, 512

    def dact_ep(i, cols, da, _, gate_ref, halo_ref, gl_ref, vg_ref):
        gate, gl = f32(gate_ref, cols), f32(gl_ref, cols)
        halo = f32(halo_ref, cols)[hb - 8:] * (i > 0).astype(F32)
        g1, g2 = _shift_rows(gate, halo, 1), _shift_rows(gate, halo, 2)
        d_cg = da * f32(vg_ref, cols)
        row3 = lax.broadcasted_iota(jnp.int32, (3, da.shape[1]), 0)
        s0 = jnp.sum(d_cg * g2, axis=0, keepdims=True)
        s1 = jnp.sum(d_cg * g1, axis=0, keepdims=True)
        s2 = jnp.sum(d_cg * gate, axis=0, keepdims=True)
        dcw = jnp.where(row3 == 0, s0, jnp.where(row3 == 1, s1, s2))
        return da * gl, d_cg, dcw, jnp.sum(d_cg, axis=0, keepdims=True)

    d_up, d_cg, d_conv_w, d_conv_b = _mm(
        d_x2b, wts["w_down"], tb=True, tm=tmd, tn=tnd, epilogue=dact_ep, name="d_act_bwd",
        extras=[(up_g, "tile", 0), (up_g, "spec", ((hb, tnd), lambda j, i: (jnp.maximum(i * (tmd // hb) - 1, 0), j))),
                (gelu_cg, "tile", 0), (val_gelu_grad, "tile", 0)],
        outs=[(SDS((L, 2 * DFF), BF16), "tile"), (SDS((L, DFF), BF16), "tile"), (SDS((3, DFF), F32), "colacc"),
              (SDS((1, DFF), F32), "colacc")])
    ncf = DFF // tcf

    tmg = 512

    def gate_bwd(c, r, dcg, halo, cw):
        halo = halo[:8] * (r < nr(tmg) - 1).astype(F32)
        return (cw[2:3] * dcg + cw[1:2] * _shift_rows_up(dcg, halo, 1) + cw[0:1] * _shift_rows_up(dcg, halo, 2),)

    (d_up,) = _ew(gate_bwd, [(d_cg, *_rc(tmg, tcf)),
                             (d_cg, (hb, tcf), lambda c, r: (jnp.minimum((r + 1) * (tmg // hb), L // hb - 1), c)),
                             (conv_w, *_col(3, tcf))],
                  [(SDS((L, 2 * DFF), BF16), *_rc(tmg, tcf, ncf), None)], (ncf, nr(tmg)), "ffn_gate_bwd", into=d_up)
    d_h2 = _mm(d_up, w_up, tb=True, name="d_h2")
    assert tcf == 2 * DFF // N_DEV
    dw_up = _mm(h2, d_up, ta=True, out_dtype=BF16, tn=tcf, tk=2048, stack_out=True, name="dw_up")
    tok = gput("c", {"w_up": dw_up, "w_down": dw_down})
    d_x1, d_g2 = _rmsnorm_bwd(d_h2, x1, small["ffn_norm_g"] + tok[0, 0], d_x2, "norm2_bwd")

    dw_out = _mm(merged, d_x1, ta=True, out_dtype=BF16, name="dw_out")

    def dmerge_ep(i, cols, dm, _, a_ref, s_ref, sa_ref, ss_ref):
        sa, ss = f32(sa_ref, cols), f32(ss_ref, cols)
        return dm * sa, dm * ss, dm * (f32(a_ref, cols) * (sa * (1.0 - sa))), dm * (f32(s_ref, cols) * (ss * (1.0 - ss)))

    d_bra, d_brs, d_ga, d_gs = _mm(d_x1, wts["w_out"], tb=True, tm=512, tn=tnm, epilogue=dmerge_ep,
                                   extras=[(br_a, "tile", 0), (br_s, "tile", 0), (sig_a, "tile", 0), (sig_s, "tile", 0)],
                                   outs=[(SDS((L, D), BF16), "tile")] * 4, name="d_merged_bwd")
    d_attn = _mm(d_bra, wts["w_branch_attn"], tb=True, out_dtype=BF16, name="d_attn")
    dw_ba = _mm(attn, d_bra, ta=True, out_dtype=BF16, name="dw_branch_attn")
    d_ssm = _mm(d_brs, wts["w_branch_ssm"], tb=True, name="d_ssm")
    dw_bs = _mm(ssm, d_brs, ta=True, out_dtype=BF16, name="dw_branch_ssm")
    dq, dkv_cur, dkv_prev, d_sinks = _attn_bwd(projb, small["attn_sinks"], attn_bias, d_attn, "attn_bwd")

    def glu_bwd(c, r, ds, vb, sg):
        return ds * sg, ds * vb * (sg * (1.0 - sg))

    d_glu_v, d_glu_g = _ew(glu_bwd, [(d_ssm, *_rc(tm, SW)), (glu_val, *_rc(tm, SW)), (glu_sig, *_rc(tm, SW))],
                           [(SDS((L, SW), F32), *_rc(tm, SW), None)] * 2, (1, nr(tm)), "glu_gate_bwd")
    d_glu = jnp.concatenate([d_glu_v, d_glu_g], axis=1)
    d_gy = _mm(d_glu, wts["w_glu"], tb=True, name="d_gelu_y")
    dw_glu = _mm(gy, d_glu, ta=True, out_dtype=BF16, name="dw_glu")

    tok = gput("b", {"w_glu": dw_glu, "w_branch_attn": dw_ba, "w_branch_ssm": dw_bs, "w_out": dw_out})
    ab = ab + tok[0, 0]

    def gelu_bwd(c, r, dg, yb, ub, dgl):
        dy = dg * _gelu_grad(yb)
        return dy, jnp.sum(dy * ub, axis=0, keepdims=True), jnp.sum(dgl, axis=0, keepdims=True)

    dy, d_ssm_d, d_b_glu = _ew(
        gelu_bwd, [(d_gy, *_rc(tm, 256)), (y, *_rc(tm, 256)), (proj, *_rc(tm, 256, C_U // 256)), (d_glu, *_rc(tm, 512))],
        [(SDS((L, SW), F32), *_rc(tm, 256), None), (SDS((1, SW), F32), *_col(1, 256), "r"),
         (SDS((1, 2 * SW), F32), *_col(1, 512), "r")], (2, nr(tm)), "ssm_gelu_bwd")

    dy_seg = _to_segments(dy)
    ends_r = _ssm_scan(dy_seg, cmat, ab, reverse=True, tk=256, name="ssm_ends_bwd")
    lam, dab8, du_seg = _ssm_scan(dy_seg, cmat, ab, reverse=True, ends=ends_r, xs=xs, init=init_f, wproj=bmat,
                                  tk=64, name="ssm_scan_bwd")
    du_mm = _from_segments(du_seg)
    dbm = _mm(u_seg, lam, ta=True, tm=512, name="ssm_dbmat")
    dcm = _mm(dy_seg, xs, ta=True, tm=512, name="ssm_dcmat")
    d_are, d_aim, d_ldt, d_bre, d_bim, d_cre, d_cim = _ssm_param_bwd(
        small["a_re"], small["a_im"], small["logdt"], small["b_re"], small["b_im"], dab8, dbm, dcm, "ssm_param_bwd")

    nb = L // BLK

    def dproj_fn(c, r, dqb, cur, prv, du, dyb, dsk, dga, dgs):
        dkv = cur + prv * (r < nb - 1).astype(F32)
        dub = du + dsk * dyb
        full = jnp.concatenate([dqb, dkv, dub, jnp.zeros((BLK, C_GA - C_PAD), F32), dga, dgs], axis=1)
        return full, jnp.sum(full, axis=0, keepdims=True)

    rowb = lambda w: ((BLK, w), lambda c, r: (r, 0))
    dproj, d_b_in = _ew(
        dproj_fn, [(dq, *rowb(AW)), (dkv_cur, *rowb(256)),
                   (dkv_prev, (BLK, 256), lambda c, r: (jnp.minimum(r + 1, nb - 1), 0)),
                   (du_mm, *rowb(SW)), (dy, *rowb(SW)), (small["ssm_d"], *_col(1, SW)), (d_ga, *rowb(D)), (d_gs, *rowb(D))],
        [(SDS((L, INP), BF16), *rowb(INP), None), (SDS((1, INP), F32), *_col(1, INP), "all")], (1, nb), "dproj")
    tok_small = gput("small", {
        "b_in": _unpad_cols(d_b_in), "attn_sinks": d_sinks[:, :NQ], "a_re": d_are, "a_im": d_aim, "logdt": d_ldt,
        "b_re": d_bre, "b_im": d_bim, "c_re": d_cre, "c_im": d_cim, "ssm_d": d_ssm_d, "b_glu": d_b_glu,
        "ffn_norm_g": d_g2, "conv_w": d_conv_w, "conv_b": d_conv_b, "final_norm_g": d_gf})
    dw_in = _mm(h, dproj, ta=True, out_dtype=BF16, name="dw_in")
    tok = gput("a", {"w_in": _unpad_cols(dw_in)}) + tok_small
    d_h = _mm(dproj, wts["w_in"], tb=True, bias=jnp.zeros((1, D), F32) + tok[0, 0], name="d_h")
    grad_x, d_g1 = _rmsnorm_bwd(d_h, x, small["attn_norm_g"], d_x1, "norm1_bwd")
    return loss, grad_x, {"attn_norm_g": d_g1}


def _small_layouts(p):
    gp = lambda a: a.reshape(1, NS)
    hgp = lambda a: a.transpose(2, 0, 1).reshape(H, NS)
    chgp = lambda a: a.transpose(1, 0, 2).reshape(H, NS)
    return {
        "attn_norm_g": p["attn_norm_g"].reshape(1, D), "ffn_norm_g": p["ffn_norm_g"].reshape(1, D),
        "final_norm_g": p["final_norm_g"].reshape(1, D),
        "b_in_p": _pad_cols(p["b_in"].reshape(1, INC)),
        "attn_sinks": p["attn_sinks"].reshape(1, NQ),
        "a_re": gp(p["ssm_a_re"]), "a_im": gp(p["ssm_a_im"]), "logdt": jnp.repeat(p["ssm_log_dt"], P).reshape(1, NS),
        "b_re": hgp(p["ssm_b_re"]), "b_im": hgp(p["ssm_b_im"]), "c_re": chgp(p["ssm_c_re"]), "c_im": chgp(p["ssm_c_im"]),
        "ssm_d": p["ssm_d"].reshape(1, SW), "b_glu": p["b_glu"].reshape(1, 2 * SW),
        "conv_b": p["conv_b"].reshape(1, DFF),
    }


def _small_grads_to_param_shapes(sg):
    from_hgp = lambda a: a.reshape(H, G, P).transpose(1, 2, 0)
    from_chgp = lambda a: a.reshape(H, G, P).transpose(1, 0, 2)
    flat = lambda a: a.reshape(-1)
    to_param = {
        "attn_norm_g": ("attn_norm_g", flat), "b_in": ("b_in", flat), "attn_sinks": ("attn_sinks", flat),
        "a_re": ("ssm_a_re", lambda a: a.reshape(G, P)), "a_im": ("ssm_a_im", lambda a: a.reshape(G, P)),
        "logdt": ("ssm_log_dt", lambda a: jnp.sum(a.reshape(G, P), axis=1)),
        "b_re": ("ssm_b_re", from_hgp), "b_im": ("ssm_b_im", from_hgp),
        "c_re": ("ssm_c_re", from_chgp), "c_im": ("ssm_c_im", from_chgp),
        "ssm_d": ("ssm_d", flat), "b_glu": ("b_glu", flat), "ffn_norm_g": ("ffn_norm_g", flat),
        "conv_w": ("conv_w", lambda a: a), "conv_b": ("conv_b", flat), "final_norm_g": ("final_norm_g", flat),
    }
    return {to_param[k][0]: to_param[k][1](a) for k, a in sg.items()}


def kernel(x, attn_norm_g, w_in, b_in, attn_sinks, ssm_a_re, ssm_a_im, ssm_log_dt, ssm_b_re, ssm_b_im, ssm_c_re, ssm_c_im, ssm_d, w_glu, b_glu, w_branch_attn, w_branch_ssm, w_out, ffn_norm_g, w_up, conv_w, conv_b, w_down, final_norm_g, loss_target, m_attn_norm_g, m_w_in, m_b_in, m_attn_sinks, m_ssm_a_re, m_ssm_a_im, m_ssm_log_dt, m_ssm_b_re, m_ssm_b_im, m_ssm_c_re, m_ssm_c_im, m_ssm_d, m_w_glu, m_b_glu, m_w_branch_attn, m_w_branch_ssm, m_w_out, m_ffn_norm_g, m_w_up, m_conv_w, m_conv_b, m_w_down, m_final_norm_g, v_attn_norm_g, v_w_in, v_b_in, v_attn_sinks, v_ssm_a_re, v_ssm_a_im, v_ssm_log_dt, v_ssm_b_re, v_ssm_b_im, v_ssm_c_re, v_ssm_c_im, v_ssm_d, v_w_glu, v_b_glu, v_w_branch_attn, v_w_branch_ssm, v_w_out, v_ffn_norm_g, v_w_up, v_conv_w, v_conv_b, v_w_down, v_final_norm_g):
    args = dict(locals())
    sq = lambda a: a if a.ndim == 1 else a[0]
    wv = {n: sq(args[n]) for n in _WEIGHTS}
    mv = {n: sq(args["m_" + n]) for n in _WEIGHTS}
    vv = {n: sq(args["v_" + n]) for n in _WEIGHTS}
    me = 4 * lax.axis_index("x") + 2 * lax.axis_index("y") + lax.axis_index("c")

    gather, tok = {}, jnp.zeros((8, 128), F32)
    for grp in ("a", "b", "c"):
        shards = [(wv[n] + tok[0, 0]).astype(BF16) for n in _GROUPS[grp]]
        if grp == "c":
            shards.append(jnp.pad(wv["conv_w"] + tok[0, 0], ((0, 5), (0, 64))))
        gather[grp], tok = _exchange_start(shards, False, "gather_start_" + grp,
                                           masks=_FIRST_HOP_MASKS if grp == "a" else _ALL_MASKS)
    small = _small_layouts(wv)
    small["attn_norm_g"] = small["attn_norm_g"] + tok[0, 0]

    def own_slot(land, src):
        return lax.dynamic_update_slice_in_dim(land, src, me, axis=0)

    def wget(grp, after):
        if grp == "a":
            thru, lands = _exchange_wait(gather[grp], after, False, "gather_wait_a", masks=_FIRST_HOP_MASKS)
            fwd, fwd_tok = _forward_start(lands, "gather_forward_start_a")
            lands = _forward_wait(fwd, fwd_tok, "gather_forward_wait_a")
        else:
            thru, lands = _exchange_wait(gather[grp], after, False, "gather_wait_" + grp)
        full = {}
        for n, t, g in zip(_GROUPS[grp], thru, lands):
            g = own_slot(g, t[None])
            full[n] = _unstack_cols(g) if n in _COL_SHARDED else g.reshape(N_DEV * g.shape[1], g.shape[2])
        if grp == "a":
            full["w_in"] = _pad_cols(full["w_in"])
        if grp == "c":
            full["conv_w"] = _unstack_cols(own_slot(lands[-1], thru[-1][None])[:, :3, :DFF // N_DEV])
        return full

    scatter = {}

    early_names = [n for n in _SMALL if n != "attn_norm_g"]
    sgp = {}

    def gput(grp, grads):
        if grp == "small":
            sgp.update(_small_grads_to_param_shapes(grads))
            scatter[grp], token = _exchange_start([_pack([sgp[n] for n in early_names])], False, "gather_small_start")
            return token
        stacked = [grads[n] if n == "w_up" else
                   _stack_cols(grads[n]) if n in _COL_SHARDED else grads[n].reshape(N_DEV, -1, D) for n in _GROUPS[grp]]
        scatter[grp], token = _exchange_start(stacked, True, "scatter_start_" + grp)
        return token

    loss, grad_x, sg = _local_step(x[0], loss_target[0], wget, small, gput)
    loss = lax.psum(loss, MESH_AXES)

    sgp.update(_small_grads_to_param_shapes(sg))
    small_names = [n for n in _SMALL]
    (norm_all,) = _exchange([jnp.pad(sgp["attn_norm_g"].reshape(1, D), ((0, 7), (0, 0)))], False, "gather_norm_grad")
    thru, (small_all,) = _exchange_wait(scatter["small"], norm_all, False, "gather_small_wait")
    small_all = own_slot(small_all, thru[0][None])

    outs_g, outs_d, outs_m, outs_v = {}, {}, {}, {}
    for grp in ("c", "b", "a"):
        thru, lands = _exchange_wait(scatter[grp], norm_all, True, "scatter_wait_" + grp)
        for n, t, pt in zip(_GROUPS[grp], thru, lands):
            pt = own_slot(pt, lax.dynamic_slice_in_dim(t, me, 1, axis=0))
            outs_g[n], outs_d[n], outs_m[n], outs_v[n] = _adam(pt, wv[n], mv[n], vv[n], "adam_" + n)

    sizes = [int(math.prod(sgp[n].shape)) for n in early_names]
    offs = [0]
    for s in sizes:
        offs.append(offs[-1] + s)

    def local_part(n, a):
        if n == "conv_w":
            return lax.dynamic_slice(a, (0, me * (DFF // N_DEV)), (3, DFF // N_DEV))
        return a

    rows = small_all.shape[1]

    def sum_fn(cc, rr, pb, nb_):
        g, gn = pb[0], nb_[0]
        for d in range(1, N_DEV):
            g, gn = g + pb[d], gn + nb_[d]
        return g, gn

    gsum, gnorm = _ew(sum_fn, [(small_all, (N_DEV, rows, 128), lambda cc, rr: (0, 0, 0)),
                               (norm_all, (N_DEV, 8, D), lambda cc, rr: (0, 0, 0))],
                      [(SDS((rows, 128), F32), (rows, 128), lambda cc, rr: (0, 0), None),
                       (SDS((8, D), F32), (8, D), lambda cc, rr: (0, 0), None)], (1, 1), "sum_small_grads")
    gflat = gsum.reshape(-1)
    gsmall = {n: local_part(n, gflat[offs[i]:offs[i + 1]].reshape(sgp[n].shape)) for i, n in enumerate(early_names)}
    gsmall["attn_norm_g"] = gnorm[0]
    as2d = lambda a: a.reshape(1, -1) if a.ndim == 1 else a.reshape(a.shape[0], -1)
    n_small = len(small_names)

    def adam_small(*refs):
        for i in range(n_small):
            g_ref, w_ref, m_ref, v_ref = refs[4 * i:4 * i + 4]
            outs = refs[4 * n_small + 3 * i:4 * n_small + 3 * i + 3]
            for o_ref, val in zip(outs, _adam_math(g_ref[...], w_ref[...], m_ref[...], v_ref[...])):
                o_ref[...] = val

    small_ins = [as2d(t[n]) for n in small_names for t in (gsmall, wv, mv, vv)]
    small_outs = pl.pallas_call(adam_small, name="adam_small",
                                out_shape=[SDS(as2d(wv[n]).shape, F32) for n in small_names for _ in range(3)])(*small_ins)
    for i, n in enumerate(small_names):
        sd, sm, sv = (t.reshape(wv[n].shape) for t in small_outs[3 * i:3 * i + 3])
        outs_g[n], outs_d[n], outs_m[n], outs_v[n] = gsmall[n], sd, sm, sv

    lead = lambda n, a: a if args[n].ndim == 1 else a[None]
    grad_x = grad_x[None]
    return (loss, grad_x, *[lead(n, outs_g[n]) for n in _WEIGHTS], *[lead(n, outs_d[n]) for n in _WEIGHTS],
            *[lead(n, outs_m[n]) for n in _WEIGHTS], *[lead(n, outs_v[n]) for n in _WEIGHTS])
```
